```python
import math
import jax
import jax.numpy as jnp
from jax import lax
import numpy as np

D_MODEL = 2048
BATCH = 4
SEQ = 2048
DEPTH = 2
DEC_BATCH = 128
DEC_SEQ = 1
PAST_LEN = 16384
PAGE_SIZE = 128

D_MIX = D_MODEL
A_DIM = D_MIX // 2
A_HEAD = 64
A_HEADS = A_DIM // A_HEAD
A_GROUPS = 2
A_STATE = 128
A_GN = A_GROUPS * A_STATE
CONV_W = 4
CONV_CH = A_DIM + 2 * A_GN
A_CHUNK = 256
B_DIM = D_MIX // 4
B_HEAD = 128
B_HEADS = B_DIM // B_HEAD
B_CHUNK = 64
C_DIM = D_MIX - A_DIM - B_DIM
C_HEAD = 64
C_HEADS = C_DIM // C_HEAD
C_LR_W = 64
C_LR_A = 64
C_LR_G = 128
C_COLS = 3 * C_DIM + C_LR_W + C_LR_A + C_LR_G
COL_SIZES = (A_DIM, CONV_CH, A_HEADS, B_DIM, B_DIM, B_DIM, B_DIM, C_COLS)
IN_COLS = A_DIM + CONV_CH + A_HEADS + 4 * B_DIM + C_COLS
D_FF = 5504
N_EXPERTS = 8
TOP_K = 2
E_FF = 7168
MOE_BLOCK = 128
N_DENSE = (DEPTH + 1) // 2
N_MOE = DEPTH // 2
NORM_EPS = 1e-6
GN_EPS = 64e-5
L2_EPS = 1e-12

kernel_name = 'hymba_ssd_hgrn2_rwkv7_moe_step'


def rmsnorm(x, g):
    xf = x.astype(jnp.float32)
    y = xf * lax.rsqrt(jnp.mean(xf * xf, axis=-1, keepdims=True) + NORM_EPS)
    return (y * g.astype(jnp.float32)).astype(x.dtype)


def _split_cols(t, sizes):
    out, off = [], 0
    for s in sizes:
        out.append(t[..., off:off + s])
        off += s
    return out


def _chunks(t, n_chunks):
    b, l = t.shape[0], t.shape[1]
    return jnp.moveaxis(t.reshape((b, n_chunks, l // n_chunks) + t.shape[2:]), 1, 0)


def _unchunk(t):
    t = jnp.moveaxis(t, 0, 1)
    return t.reshape((t.shape[0], t.shape[1] * t.shape[2]) + t.shape[3:])


def ssd_chunk_scan(xs, dt, log_a, bm, cm, h0):
    seq = xs.shape[1]
    csz = math.gcd(seq, A_CHUNK)
    nch = seq // csz
    rep = A_HEADS // A_GROUPS
    causal = jnp.tril(jnp.ones((csz, csz), dtype=bool))

    def step(h, blk):
        x_c, dt_c, la_c, b_c, c_c = blk
        cum = jnp.cumsum(la_c, axis=1)
        c_h = jnp.repeat(c_c, rep, axis=2)
        b_h = jnp.repeat(b_c, rep, axis=2)
        y_state = jnp.einsum('bthn,bhpn->bthp', c_h, h) * jnp.exp(cum)[..., None]
        cb = jnp.repeat(jnp.einsum('btgn,bsgn->btsg', c_c, b_c), rep, axis=3)
        seg = cum[:, :, None, :] - cum[:, None, :, :]
        w = jnp.exp(jnp.where(causal[None, :, :, None], seg, -jnp.inf)) * cb * dt_c[:, None, :, :]
        y_local = jnp.einsum('btsh,bshp->bthp', w, x_c)
        tail = jnp.exp(cum[:, -1:, :] - cum) * dt_c
        h_new = h * jnp.exp(cum[:, -1, :])[:, :, None, None] + jnp.einsum('bsh,bshp,bshn->bhpn', tail, x_c, b_h)
        return h_new, y_state + y_local

    h_last, y = lax.scan(step, h0.astype(jnp.float32),
                         tuple(_chunks(t, nch) for t in (xs, dt, log_a, bm, cm)))
    return _unchunk(y), h_last


def mamba2_mixer(z, xbc, dt_raw, conv_buf, h0, conv_w, conv_b, dt_bias, a_log, d_skip, norm_g):
    bsz, seq = xbc.shape[0], xbc.shape[1]
    xpad = jnp.concatenate([conv_buf.astype(xbc.dtype), xbc], axis=1)
    conv = conv_b.astype(jnp.float32)
    for tap in range(CONV_W):
        conv = conv + xpad[:, tap:tap + seq].astype(jnp.float32) * conv_w[tap].astype(jnp.float32)
    new_buf = xpad[:, seq:]
    u = jax.nn.silu(conv)
    xs = u[..., :A_DIM].reshape(bsz, seq, A_HEADS, A_HEAD)
    bm = u[..., A_DIM:A_DIM + A_GN].reshape(bsz, seq, A_GROUPS, A_STATE)
    cm = u[..., A_DIM + A_GN:].reshape(bsz, seq, A_GROUPS, A_STATE)
    dt = jax.nn.softplus(dt_raw.astype(jnp.float32) + dt_bias.astype(jnp.float32))
    log_a = dt * (-jnp.exp(a_log.astype(jnp.float32)))
    y, h_last = ssd_chunk_scan(xs, dt, log_a, bm, cm, h0)
    y = y + d_skip.astype(jnp.float32)[:, None] * xs
    y = y.reshape(bsz, seq, A_DIM) * jax.nn.silu(z.astype(jnp.float32))
    yg = y.reshape(bsz, seq, A_GROUPS, A_DIM // A_GROUPS)
    yg = yg * lax.rsqrt(jnp.mean(yg * yg, axis=-1, keepdims=True) + NORM_EPS)
    y = yg.reshape(bsz, seq, A_DIM) * norm_g.astype(jnp.float32)
    return y, new_buf, h_last


def hgrn2_mixer(q, f, i, g, s0, lower_bound, norm_g):
    bsz, seq = q.shape[0], q.shape[1]
    shp = (bsz, seq, B_HEADS, B_HEAD)
    ff = f.astype(jnp.float32)
    lb = jnp.clip(lower_bound.astype(jnp.float32), 0.0, 1.0)
    qf = jax.nn.silu(q.astype(jnp.float32)).reshape(shp)
    kf = ((1.0 - lb) * jax.nn.sigmoid(-ff)).reshape(shp)
    logf = jnp.logaddexp(jnp.log(lb), jnp.log1p(-lb) + jax.nn.log_sigmoid(ff)).reshape(shp)
    vf = i.astype(jnp.float32).reshape(shp)
    csz = math.gcd(seq, B_CHUNK)
    nch = seq // csz
    causal = jnp.tril(jnp.ones((csz, csz), dtype=bool))

    def step(s, blk):
        q_c, k_c, v_c, lf_c = blk
        cum = jnp.cumsum(lf_c, axis=1)
        o_state = jnp.einsum('bthk,bhkv->bthv', q_c * jnp.exp(cum), s)
        rel = jnp.exp(jnp.where(causal[None, :, :, None, None],
                                cum[:, :, None] - cum[:, None], -jnp.inf))
        att = jnp.einsum('bthk,bshk,btshk->btsh', q_c, k_c, rel)
        o_local = jnp.einsum('btsh,bshv->bthv', att, v_c)
        tail = jnp.exp(cum[:, -1:] - cum) * k_c
        s_new = s * jnp.exp(cum[:, -1])[..., None] + jnp.einsum('bshk,bshv->bhkv', tail, v_c)
        return s_new, o_state + o_local

    s_last, o = lax.scan(step, s0.astype(jnp.float32),
                         tuple(_chunks(t, nch) for t in (qf, kf, vf, logf)))
    o = _unchunk(o)
    o = o * lax.rsqrt(jnp.mean(o * o, axis=-1, keepdims=True) + NORM_EPS) * norm_g.astype(jnp.float32)
    o = o.reshape(bsz, seq, B_DIM) * jax.nn.silu(g.astype(jnp.float32))
    return o, s_last


def rwkv7_mixer(p, shift_buf, s0, mu, w0, w2, a0, a2, g2, k_k, k_a, r_k, lnx_w, lnx_b):
    bsz, seq = p.shape[0], p.shape[1]
    prev = jnp.concatenate([shift_buf[:, None].astype(p.dtype), p[:, :-1]], axis=1)
    new_buf = p[:, -1]
    ps = (p + (prev - p) * mu).astype(jnp.float32)
    r, k, v, wl, al, gl = _split_cols(ps, (C_DIM, C_DIM, C_DIM, C_LR_W, C_LR_A, C_LR_G))
    w = -jax.nn.softplus(-(w0 + jnp.tanh(wl) @ w2)) - 0.5
    decay = jnp.exp(-jnp.exp(w))
    a = jax.nn.sigmoid(a0 + al @ a2)
    g = jax.nn.sigmoid(gl) @ g2
    hs = (bsz, seq, C_HEADS, C_HEAD)
    kk = (k * k_k).reshape(hs)
    kk = kk / jnp.maximum(jnp.sqrt(jnp.sum(kk * kk, axis=-1, keepdims=True)), L2_EPS)
    k = (k * (1.0 + (a - 1.0) * k_a)).reshape(hs)
    r, v, decay, a = r.reshape(hs), v.reshape(hs), decay.reshape(hs), a.reshape(hs)

    def step(s, inp):
        r_t, w_t, k_t, v_t, kk_t, a_t = inp
        sa = jnp.einsum('bhvk,bhk->bhv', s, -kk_t)
        s = (s * w_t[:, :, None, :] + sa[..., None] * (kk_t * a_t)[:, :, None, :]
             + v_t[..., None] * k_t[:, :, None, :])
        return s, jnp.einsum('bhvk,bhk->bhv', s, r_t)

    s_last, y = lax.scan(step, s0.astype(jnp.float32),
                         tuple(jnp.moveaxis(t, 1, 0) for t in (r, decay, k, v, kk, a)))
    y = jnp.moveaxis(y, 0, 1)
    mean = jnp.mean(y, axis=-1, keepdims=True)
    var = jnp.mean(jnp.square(y - mean), axis=-1, keepdims=True)
    y = ((y - mean) * lax.rsqrt(var + GN_EPS)).reshape(bsz, seq, C_DIM) * lnx_w + lnx_b
    bonus = jnp.sum(r * k * r_k, axis=-1, keepdims=True) * v
    y = (y + bonus.reshape(bsz, seq, C_DIM)) * g
    return y, new_buf, s_last


def swiglu(h, wg, wu, wd):
    return (jax.nn.silu(h @ wg) * (h @ wu)) @ wd


def moe_swiglu(h, w_router, e_gate, e_up, e_down):
    n_tok = h.shape[0]
    logits = jnp.dot(h, w_router).astype(jnp.float32)
    top_logit, top_idx = lax.top_k(logits, TOP_K)
    gate = jax.nn.softmax(top_logit, axis=-1)
    n_assign = n_tok * TOP_K
    flat_e = top_idx.reshape(-1).astype(jnp.int32)
    flat_tok = jnp.repeat(jnp.arange(n_tok, dtype=jnp.int32), TOP_K)
    flat_gate = gate.reshape(-1)
    order = jnp.argsort(flat_e)
    se, stok, sgate = flat_e[order], flat_tok[order], flat_gate[order]
    counts = jnp.sum(jax.nn.one_hot(flat_e, N_EXPERTS, dtype=jnp.int32), axis=0)
    padded = (counts + MOE_BLOCK - 1) // MOE_BLOCK * MOE_BLOCK
    pad_end = jnp.cumsum(padded)
    pad_start = pad_end - padded
    start = jnp.cumsum(counts) - counts
    slot = pad_start[se] + jnp.arange(n_assign, dtype=jnp.int32) - start[se]
    n_blocks = -(-n_assign // MOE_BLOCK) + N_EXPERTS
    n_slots = n_blocks * MOE_BLOCK
    slot_tok = jnp.full((n_slots,), n_tok, jnp.int32).at[slot].set(stok)
    slot_gate = jnp.zeros((n_slots,), jnp.float32).at[slot].set(sgate)
    block_start = jnp.arange(n_blocks, dtype=jnp.int32) * MOE_BLOCK
    block_exp = jnp.minimum(jnp.searchsorted(pad_end, block_start, side='right'), N_EXPERTS - 1)
    h_pad = jnp.concatenate([h, jnp.zeros((1, h.shape[1]), h.dtype)], axis=0)
    xb = h_pad[slot_tok].reshape(n_blocks, MOE_BLOCK, h.shape[1])

    def expert_block(args):
        xblk, e = args
        return swiglu(xblk, e_gate[e], e_up[e], e_down[e])

    yb = lax.map(expert_block, (xb, block_exp)).reshape(n_slots, h.shape[1])
    yb = yb * slot_gate[:, None].astype(yb.dtype)
    y = jnp.zeros((n_tok + 1, h.shape[1]), yb.dtype).at[slot_tok].add(yb)
    return y[:n_tok].astype(h.dtype)


def trunk(x, ssm0, conv0, hgrn0, rwkv0, shift0, lower_bounds, prm):
    ssm_l, conv_l, hgrn_l, rwkv_l, shift_l = [], [], [], [], []
    for layer in range(DEPTH):
        h = rmsnorm(x, prm['norm1_g'][layer])
        proj = jnp.einsum('bld,dc->blc', h, prm['w_in'][layer])
        z, xbc, dt_raw, q, f, i, g, pc = _split_cols(proj, COL_SIZES)
        ya, conv_new, ssm_new = mamba2_mixer(
            z, xbc, dt_raw, conv0[layer], ssm0[layer], prm['conv_w'][layer], prm['conv_b'][layer],
            prm['dt_bias'][layer], prm['a_log'][layer], prm['d_skip'][layer], prm['ssm_norm_g'][layer])
        yb, hgrn_new = hgrn2_mixer(q, f, i, g, hgrn0[layer], lower_bounds[layer], prm['hgrn_norm_g'][layer])
        yc, shift_new, rwkv_new = rwkv7_mixer(
            pc, shift0[layer], rwkv0[layer], prm['shift_mu'][layer], prm['w0'][layer], prm['w2'][layer],
            prm['a0'][layer], prm['a2'][layer], prm['g2'][layer], prm['k_k'][layer], prm['k_a'][layer],
            prm['r_k'][layer], prm['lnx_w'][layer], prm['lnx_b'][layer])
        mix = jnp.concatenate([ya, yb, yc], axis=-1).astype(x.dtype)
        x = x + jnp.einsum('blc,cd->bld', mix, prm['w_out'][layer]).astype(x.dtype)
        h = rmsnorm(x, prm['norm2_g'][layer])
        j = layer // 2
        if layer % 2 == 0:
            x = x + swiglu(h, prm['ffn_w_gate'][j], prm['ffn_w_up'][j], prm['ffn_w_down'][j]).astype(x.dtype)
        else:
            y_moe = moe_swiglu(h.reshape(-1, D_MODEL), prm['router_w'][j], prm['exp_w_gate'][j],
                               prm['exp_w_up'][j], prm['exp_w_down'][j])
            x = x + y_moe.reshape(x.shape).astype(x.dtype)
        ssm_l.append(ssm_new)
        conv_l.append(conv_new)
        hgrn_l.append(hgrn_new)
        rwkv_l.append(rwkv_new)
        shift_l.append(shift_new)
    y = rmsnorm(x, prm['final_norm_g'])
    return y, jnp.stack(ssm_l), jnp.stack(conv_l), jnp.stack(hgrn_l), jnp.stack(rwkv_l), jnp.stack(shift_l)


def setup_inputs(seed: int = 0) -> dict:
    key = jax.random.key(seed)
    keys = list(jax.random.split(key, 48))

    def nrm(shape, scale):
        return jax.random.normal(keys.pop(), shape, jnp.float32) * scale

    def uni(shape, lo, hi):
        return jax.random.uniform(keys.pop(), shape, jnp.float32, lo, hi)

    def gain(shape):
        return 1.0 + nrm(shape, 0.02)

    dt = jnp.exp(uni((DEPTH, A_HEADS), math.log(1e-3), math.log(1e-1)))
    dt_bias = dt + jnp.log(-jnp.expm1(-dt))
    return {
        'x_prompt': nrm((BATCH, SEQ, D_MODEL), 1.0),
        'x_sample': nrm((DEC_BATCH, DEC_SEQ, D_MODEL), 1.0),
        'state_ssm': nrm((DEPTH, DEC_BATCH, A_HEADS, A_HEAD, A_STATE), 0.5),
        'state_conv': nrm((DEPTH, DEC_BATCH, CONV_W - 1, CONV_CH), 1.0),
        'state_hgrn': nrm((DEPTH, DEC_BATCH, B_HEADS, B_HEAD, B_HEAD), 0.5),
        'state_rwkv': nrm((DEPTH, DEC_BATCH, C_HEADS, C_HEAD, C_HEAD), 0.5),
        'state_shift': nrm((DEPTH, DEC_BATCH, C_COLS), 1.0),
        'norm1_g': gain((DEPTH, D_MODEL)),
        'w_in': nrm((DEPTH, D_MODEL, IN_COLS), D_MODEL ** -0.5),
        'conv_w': nrm((DEPTH, CONV_W, CONV_CH), CONV_W ** -0.5),
        'conv_b': nrm((DEPTH, CONV_CH), 0.01),
        'dt_bias': dt_bias,
        'a_log': jnp.log(uni((DEPTH, A_HEADS), 1.0, 16.0)),
        'd_skip': gain((DEPTH, A_HEADS)),
        'ssm_norm_g': gain((DEPTH, A_DIM)),
        'lb_logits': 1.0 + nrm((DEPTH, B_DIM), 0.1),
        'hgrn_norm_g': gain((DEPTH, B_HEAD)),
        'shift_mu': uni((DEPTH, C_COLS), 0.0, 1.0),
        'w0': uni((DEPTH, C_DIM), -2.0, 1.5),
        'w2': nrm((DEPTH, C_LR_W, C_DIM), 0.1 * C_LR_W ** -0.5),
        'a0': nrm((DEPTH, C_DIM), 0.1),
        'a2': nrm((DEPTH, C_LR_A, C_DIM), 0.5 * C_LR_A ** -0.5),
        'g2': nrm((DEPTH, C_LR_G, C_DIM), C_LR_G ** -0.5),
        'k_k': 0.85 + nrm((DEPTH, C_DIM), 0.05),
        'k_a': gain((DEPTH, C_DIM)),
        'r_k': nrm((DEPTH, C_HEADS, C_HEAD), 0.1),
        'lnx_w': gain((DEPTH, C_DIM)),
        'lnx_b': nrm((DEPTH, C_DIM), 0.01),
        'w_out': nrm((DEPTH, D_MIX, D_MODEL), D_MIX ** -0.5),
        'norm2_g': gain((DEPTH, D_MODEL)),
        'ffn_w_gate': nrm((N_DENSE, D_MODEL, D_FF), D_MODEL ** -0.5),
        'ffn_w_up': nrm((N_DENSE, D_MODEL, D_FF), D_MODEL ** -0.5),
        'ffn_w_down': nrm((N_DENSE, D_FF, D_MODEL), D_FF ** -0.5),
        'router_w': nrm((N_MOE, D_MODEL, N_EXPERTS), D_MODEL ** -0.5),
        'exp_w_gate': nrm((N_MOE, N_EXPERTS, D_MODEL, E_FF), D_MODEL ** -0.5),
        'exp_w_up': nrm((N_MOE, N_EXPERTS, D_MODEL, E_FF), D_MODEL ** -0.5),
        'exp_w_down': nrm((N_MOE, N_EXPERTS, E_FF, D_MODEL), E_FF ** -0.5),
        'final_norm_g': gain((D_MODEL,)),
    }


def reference(x_prompt, x_sample, state_ssm, state_conv, state_hgrn, state_rwkv, state_shift,
              norm1_g, w_in, conv_w, conv_b, dt_bias, a_log, d_skip, ssm_norm_g,
              lb_logits, hgrn_norm_g, shift_mu, w0, w2, a0, a2, g2, k_k, k_a, r_k, lnx_w, lnx_b,
              w_out, norm2_g, ffn_w_gate, ffn_w_up, ffn_w_down, router_w,
              exp_w_gate, exp_w_up, exp_w_down, final_norm_g):
    prm = dict(norm1_g=norm1_g, w_in=w_in, conv_w=conv_w, conv_b=conv_b, dt_bias=dt_bias, a_log=a_log,
               d_skip=d_skip, ssm_norm_g=ssm_norm_g, hgrn_norm_g=hgrn_norm_g, shift_mu=shift_mu,
               w0=w0, w2=w2, a0=a0, a2=a2, g2=g2, k_k=k_k, k_a=k_a, r_k=r_k, lnx_w=lnx_w, lnx_b=lnx_b,
               w_out=w_out, norm2_g=norm2_g, ffn_w_gate=ffn_w_gate, ffn_w_up=ffn_w_up,
               ffn_w_down=ffn_w_down, router_w=router_w, exp_w_gate=exp_w_gate, exp_w_up=exp_w_up,
               exp_w_down=exp_w_down, final_norm_g=final_norm_g)
    lb_soft = jax.nn.softmax(lb_logits.astype(jnp.float32), axis=0)
    lower_bounds = jnp.cumsum(lb_soft, axis=0) - lb_soft[0:1]
    bp = x_prompt.shape[0]
    z_ssm = jnp.zeros((DEPTH, bp, A_HEADS, A_HEAD, A_STATE), jnp.float32)
    z_conv = jnp.zeros((DEPTH, bp, CONV_W - 1, CONV_CH), jnp.float32)
    z_hgrn = jnp.zeros((DEPTH, bp, B_HEADS, B_HEAD, B_HEAD), jnp.float32)
    z_rwkv = jnp.zeros((DEPTH, bp, C_HEADS, C_HEAD, C_HEAD), jnp.float32)
    z_shift = jnp.zeros((DEPTH, bp, C_COLS), jnp.float32)
    y_prompt, p_ssm, p_conv, p_hgrn, p_rwkv, p_shift = trunk(
        x_prompt, z_ssm, z_conv, z_hgrn, z_rwkv, z_shift, lower_bounds, prm)
    y_sample, s_ssm, s_conv, s_hgrn, s_rwkv, s_shift = trunk(
        x_sample, state_ssm, state_conv, state_hgrn, state_rwkv, state_shift, lower_bounds, prm)
    return (y_prompt, y_sample, p_ssm, p_conv, p_hgrn, p_rwkv, p_shift,
            s_ssm, s_conv, s_hgrn, s_rwkv, s_shift)
```

```python
import functools
import math

import jax
import jax.numpy as jnp
from jax import lax
from jax.experimental import pallas as pl
from jax.experimental.pallas import tpu as pltpu

F32 = jnp.float32
BF16 = jnp.bfloat16
HI = lax.Precision.HIGHEST

D_MODEL = 2048
A_DIM, A_HEAD, A_HEADS, A_GROUPS, A_STATE = 1024, 64, 16, 2, 128
A_GN = A_GROUPS * A_STATE
CONV_W = 4
CONV_CH = A_DIM + 2 * A_GN
B_DIM, B_HEAD, B_HEADS = 512, 128, 4
C_DIM, C_HEAD, C_HEADS = 512, 64, 8
C_LR_W, C_LR_A, C_LR_G = 64, 64, 128
C_LR = C_LR_W + C_LR_A + C_LR_G
C_COLS = 3 * C_DIM + C_LR
IN_COLS = A_DIM + CONV_CH + A_HEADS + 4 * B_DIM + C_COLS
D_FF, N_EXPERTS, TOP_K, E_FF = 5504, 8, 2, 7168
NORM_EPS, GN_EPS, L2_EPS = 1e-6, 64e-5, 1e-12

P_Z, P_Q, P_F, P_I, P_G = 0, 1024, 1536, 2048, 2560
P_R, P_K, P_V, P_LR, P_DT = 3072, 3584, 4096, 4608, 4864
P_X, P_B, P_C, P_COLS = 5120, 6144, 6400, 6656
LANES = 128
DT_PAD = P_X - P_DT - A_HEADS

SSD_CHUNK = 256
HGRN_BLOCK, HGRN_CHUNK = 256, 16
RWKV_BLOCK, RWKV_CHUNK = 256, 64
STEP_BB = 8

FFN_TF = 512
D_FF_PAD = -(-D_FF // FFN_TF) * FFN_TF
MOE_BM, MOE_TF = 512, 512

NT_DIMS = (((1,), (1,)), ((), ()))
TN_DIMS = (((0,), (0,)), ((), ()))


def _cparams(sem, vmem_mb=48):
    return pltpu.CompilerParams(dimension_semantics=sem, vmem_limit_bytes=vmem_mb * 1024 * 1024)


def _dot(a, b):
    return jnp.dot(a.astype(BF16), b.astype(BF16), preferred_element_type=F32)


def _dot_nt(a, b):
    return lax.dot_general(a.astype(BF16), b.astype(BF16), NT_DIMS, preferred_element_type=F32)


def _dot_tn(a, b):
    return lax.dot_general(a.astype(BF16), b.astype(BF16), TN_DIMS, preferred_element_type=F32)


def _dot_hi(a, b):
    return jnp.dot(a, b, precision=HI, preferred_element_type=F32)


def _dot_nt_hi(a, b):
    return lax.dot_general(a, b, NT_DIMS, precision=HI, preferred_element_type=F32)


def _dot_tn_hi(a, b):
    return lax.dot_general(a, b, TN_DIMS, precision=HI, preferred_element_type=F32)


def _silu(x):
    return x * jax.nn.sigmoid(x)


def _softplus(x):
    return jnp.maximum(x, 0.0) + jnp.log1p(jnp.exp(-jnp.abs(x)))


def _iota(shape, dim):
    return lax.broadcasted_iota(jnp.int32, shape, dim)


def _rms(x, g):
    return x * lax.rsqrt(jnp.mean(x * x, axis=-1, keepdims=True) + NORM_EPS) * g


def _full(shape):
    nd = len(shape)
    return pl.BlockSpec(shape, lambda *_: (0,) * nd)


def _norm_matmul_kernel(x_ref, g_ref, w_ref, o_ref, h_scr):
    @pl.when(pl.program_id(1) == 0)
    def _():
        h_scr[...] = _rms(x_ref[...], g_ref[...]).astype(BF16)

    o_ref[...] = jnp.dot(h_scr[...], w_ref[...], preferred_element_type=F32)


def _norm_matmul(x, g, w, tm, tn):
    m, k = x.shape
    n = w.shape[1]
    return pl.pallas_call(
        _norm_matmul_kernel,
        grid=(m // tm, n // tn),
        in_specs=[pl.BlockSpec((tm, k), lambda i, j: (i, 0)),
                  pl.BlockSpec((1, k), lambda i, j: (0, 0)),
                  pl.BlockSpec((k, tn), lambda i, j: (0, j))],
        out_specs=pl.BlockSpec((tm, tn), lambda i, j: (i, j)),
        out_shape=jax.ShapeDtypeStruct((m, n), F32),
        scratch_shapes=[pltpu.VMEM((tm, k), BF16)],
        compiler_params=_cparams(("parallel", "arbitrary")),
        name="norm_matmul",
    )(x, g.reshape(1, k), w)


def _matmul_res_kernel(a_ref, w_ref, r_ref, o_ref):
    o_ref[...] = r_ref[...] + jnp.dot(a_ref[...], w_ref[...], preferred_element_type=F32)


def _matmul_res(a, w, res, tm, tn):
    m, k = a.shape
    n = w.shape[1]
    return pl.pallas_call(
        _matmul_res_kernel,
        grid=(m // tm, n // tn),
        in_specs=[pl.BlockSpec((tm, k), lambda i, j: (i, 0)),
                  pl.BlockSpec((k, tn), lambda i, j: (0, j)),
                  pl.BlockSpec((tm, tn), lambda i, j: (i, j))],
        out_specs=pl.BlockSpec((tm, tn), lambda i, j: (i, j)),
        out_shape=jax.ShapeDtypeStruct((m, n), F32),
        compiler_params=_cparams(("parallel", "parallel")),
        name="matmul_res",
    )(a, w, res)


def _ffn_kernel(x_ref, g_ref, wg_ref, wu_ref, wd_ref, o_ref, h_scr):
    @pl.when(pl.program_id(1) == 0)
    def _():
        x = x_ref[...]
        h_scr[...] = _rms(x, g_ref[...]).astype(BF16)
        o_ref[...] = x

    h = h_scr[...]
    a = jnp.dot(h, wg_ref[...], preferred_element_type=F32)
    u = jnp.dot(h, wu_ref[...], preferred_element_type=F32)
    act = (_silu(a) * u).astype(BF16)
    o_ref[...] += jnp.dot(act, wd_ref[...], preferred_element_type=F32)


def _ffn(x, g, wg, wu, wd, tm):
    m, k = x.shape
    nf = wg.shape[1] // FFN_TF
    return pl.pallas_call(
        _ffn_kernel,
        grid=(m // tm, nf),
        in_specs=[pl.BlockSpec((tm, k), lambda i, f: (i, 0)),
                  pl.BlockSpec((1, k), lambda i, f: (0, 0)),
                  pl.BlockSpec((k, FFN_TF), lambda i, f: (0, f)),
                  pl.BlockSpec((k, FFN_TF), lambda i, f: (0, f)),
                  pl.BlockSpec((FFN_TF, k), lambda i, f: (f, 0))],
        out_specs=pl.BlockSpec((tm, k), lambda i, f: (i, 0)),
        out_shape=jax.ShapeDtypeStruct((m, k), F32),
        scratch_shapes=[pltpu.VMEM((tm, k), BF16)],
        compiler_params=_cparams(("parallel", "arbitrary")),
        name="ffn_dense",
    )(x, g.reshape(1, k), wg, wu, wd)


def _norm_router_kernel(x_ref, g_ref, rw_ref, h_ref, l_ref):
    h = _rms(x_ref[...], g_ref[...])
    h_ref[...] = h.astype(BF16)
    l_ref[...] = _dot_hi(h, rw_ref[...])


def _norm_router(x, g, rw, tm):
    m, k = x.shape
    return pl.pallas_call(
        _norm_router_kernel,
        grid=(m // tm,),
        in_specs=[pl.BlockSpec((tm, k), lambda i: (i, 0)),
                  pl.BlockSpec((1, k), lambda i: (0, 0)),
                  pl.BlockSpec((k, LANES), lambda i: (0, 0))],
        out_specs=[pl.BlockSpec((tm, k), lambda i: (i, 0)),
                   pl.BlockSpec((tm, LANES), lambda i: (i, 0))],
        out_shape=[jax.ShapeDtypeStruct((m, k), BF16), jax.ShapeDtypeStruct((m, LANES), F32)],
        compiler_params=_cparams(("parallel",)),
        name="norm_router",
    )(x, g.reshape(1, k), rw)


def _moe_kernel(bexp_ref, nv_ref, x_ref, wg_ref, wu_ref, wd_ref, o_ref):
    b = pl.program_id(0)

    @pl.when(pl.program_id(1) == 0)
    def _():
        o_ref[...] = jnp.zeros_like(o_ref)

    @pl.when(b < nv_ref[0])
    def _():
        x = x_ref[...]
        a = jnp.dot(x, wg_ref[...], preferred_element_type=F32)
        u = jnp.dot(x, wu_ref[...], preferred_element_type=F32)
        act = (_silu(a) * u).astype(BF16)
        o_ref[...] += jnp.dot(act, wd_ref[...], preferred_element_type=F32)


def _moe_experts(xb, bexp, nvalid, wg, wu, wd):
    n_slots, k = xb.shape
    nb = n_slots // MOE_BM
    nf = wg.shape[2] // MOE_TF

    def f_eff(b, f, nv):
        return jnp.where(b < nv[0], f, nf - 1)

    return pl.pallas_call(
        _moe_kernel,
        grid_spec=pltpu.PrefetchScalarGridSpec(
            num_scalar_prefetch=2,
            grid=(nb, nf),
            in_specs=[pl.BlockSpec((MOE_BM, k), lambda b, f, be, nv: (b, 0)),
                      pl.BlockSpec((None, k, MOE_TF), lambda b, f, be, nv: (be[b], 0, f_eff(b, f, nv))),
                      pl.BlockSpec((None, k, MOE_TF), lambda b, f, be, nv: (be[b], 0, f_eff(b, f, nv))),
                      pl.BlockSpec((None, MOE_TF, k), lambda b, f, be, nv: (be[b], f_eff(b, f, nv), 0))],
            out_specs=pl.BlockSpec((MOE_BM, k), lambda b, f, be, nv: (b, 0)),
        ),
        out_shape=jax.ShapeDtypeStruct((n_slots, k), F32),
        compiler_params=_cparams(("arbitrary", "arbitrary")),
        name="moe_experts",
    )(bexp, nvalid, xb, wg, wu, wd)


def _final_norm_kernel(x_ref, y_ref, g_ref, o_ref):
    o_ref[...] = _rms(x_ref[...] + y_ref[...], g_ref[...])


def _final_norm(x, y, g, tm):
    m, k = x.shape
    return pl.pallas_call(
        _final_norm_kernel,
        grid=(m // tm,),
        in_specs=[pl.BlockSpec((tm, k), lambda i: (i, 0)),
                  pl.BlockSpec((tm, k), lambda i: (i, 0)),
                  pl.BlockSpec((1, k), lambda i: (0, 0))],
        out_specs=pl.BlockSpec((tm, k), lambda i: (i, 0)),
        out_shape=jax.ShapeDtypeStruct((m, k), F32),
        compiler_params=_cparams(("parallel",)),
        name="final_norm",
    )(x, y, g.reshape(1, k))


def _ssd_gate_norm(y, z, ng):
    y = y * _silu(z)
    gw = A_DIM // A_GROUPS
    outs = []
    for g in range(A_GROUPS):
        yg = y[:, g * gw:(g + 1) * gw]
        outs.append(yg * lax.rsqrt(jnp.mean(yg * yg, axis=-1, keepdims=True) + NORM_EPS))
    return jnp.concatenate(outs, axis=-1) * ng


def _ssd_prompt_kernel(z_ref, x_ref, b_ref, c_ref, dt_ref, cwx, cwb, cwc, cbx, cbb, cbc,
                       dtb, aneg, dsk, ng, y_ref, st_ref, h_scr, xpx, xpb, xpc, yacc):
    c = pl.program_id(1)
    ch = SSD_CHUNK

    @pl.when(c == 0)
    def _():
        h_scr[...] = jnp.zeros_like(h_scr)
        xpx[0:8, :] = jnp.zeros((8, A_DIM), F32)
        xpb[0:8, :] = jnp.zeros((8, A_STATE * A_GROUPS), F32)
        xpc[0:8, :] = jnp.zeros((8, A_STATE * A_GROUPS), F32)

    xpx[8:8 + ch, :] = x_ref[...]
    xpb[8:8 + ch, :] = b_ref[...]
    xpc[8:8 + ch, :] = c_ref[...]

    def conv(xp, cw, cb):
        acc = cb[...] + xp[pl.ds(8 - (CONV_W - 1), ch), :] * cw[0:1, :]
        for tap in range(1, CONV_W):
            acc = acc + xp[pl.ds(8 - (CONV_W - 1) + tap, ch), :] * cw[tap:tap + 1, :]
        return _silu(acc)

    ux = conv(xpx, cwx, cbx)
    ub = conv(xpb, cwb, cbb)
    uc = conv(xpc, cwc, cbc)
    xpx[0:8, :] = xpx[ch:ch + 8, :]
    xpb[0:8, :] = xpb[ch:ch + 8, :]
    xpc[0:8, :] = xpc[ch:ch + 8, :]

    dt = _softplus(dt_ref[...] + dtb[...])
    la = dt * aneg[...]
    tri = (_iota((ch, ch), 1) <= _iota((ch, ch), 0)).astype(F32)
    cum = _dot_hi(tri, la)
    eye = (_iota((A_HEADS, LANES), 0) == _iota((A_HEADS, LANES), 1)).astype(F32)
    cum_r = _dot_nt_hi(eye, cum)
    dt_r = _dot_nt_hi(eye, dt)
    cl = cum[ch - 1:ch, :]
    tail = jnp.exp(cl - cum) * dt
    ecum = jnp.exp(cum)
    ecl = jnp.exp(cl)
    causal = _iota((ch, ch), 1) <= _iota((ch, ch), 0)
    rep = A_HEADS // A_GROUPS

    for g in range(A_GROUPS):
        bg = ub[:, g * A_STATE:(g + 1) * A_STATE].astype(BF16)
        cg = uc[:, g * A_STATE:(g + 1) * A_STATE].astype(BF16)
        cb_g = _dot_nt(cg, bg)
        for h in range(g * rep, (g + 1) * rep):
            hs = slice(h * A_HEAD, (h + 1) * A_HEAD)
            seg = cum[:, h:h + 1] - cum_r[h:h + 1, :]
            w = jnp.exp(jnp.where(causal, seg, -jnp.inf)) * cb_g * dt_r[h:h + 1, :]
            xh = ux[:, hs]
            hst = h_scr[h]
            y_h = _dot(w, xh) + _dot_nt(cg, hst) * ecum[:, h:h + 1]
            yacc[:, hs] = y_h
            h_scr[h] = hst * ecl[:, h:h + 1] + _dot_tn(xh * tail[:, h:h + 1], bg)

    y = yacc[...] + dsk[...] * ux
    y_ref[...] = _ssd_gate_norm(y, z_ref[...], ng[...]).astype(BF16)

    @pl.when(c == pl.num_programs(1) - 1)
    def _():
        st_ref[...] = h_scr[...]


def _ssd_params(p):
    cw, cb = p["conv_w"], p["conv_b"].reshape(1, CONV_CH)
    pad = LANES - A_HEADS
    return dict(
        cwx=cw[:, :A_DIM], cwb=cw[:, A_DIM:A_DIM + A_GN], cwc=cw[:, A_DIM + A_GN:],
        cbx=cb[:, :A_DIM], cbb=cb[:, A_DIM:A_DIM + A_GN], cbc=cb[:, A_DIM + A_GN:],
        dtb=jnp.pad(p["dt_bias"], (0, pad)).reshape(1, LANES),
        aneg=jnp.pad(-jnp.exp(p["a_log"]), (0, pad)).reshape(1, LANES),
        dsk=jnp.repeat(p["d_skip"], A_HEAD).reshape(1, A_DIM),
        ng=p["ssm_norm_g"].reshape(1, A_DIM),
    )


def _ssd_prompt(proj, sp):
    bsz, seq, _ = proj.shape
    ch = SSD_CHUNK
    gn = A_GN

    def col(width, off):
        return pl.BlockSpec((None, ch, width), lambda b, c: (b, c, off // width))

    names = ("cwx", "cwb", "cwc", "cbx", "cbb", "cbc", "dtb", "aneg", "dsk", "ng")
    y, st = pl.pallas_call(
        _ssd_prompt_kernel,
        grid=(bsz, seq // ch),
        in_specs=[col(A_DIM, P_Z), col(A_DIM, P_X), col(gn, P_B), col(gn, P_C), col(LANES, P_DT)]
                 + [_full(sp[n].shape) for n in names],
        out_specs=[pl.BlockSpec((None, ch, A_DIM), lambda b, c: (b, c, 0)),
                   pl.BlockSpec((None, A_HEADS, A_HEAD, A_STATE), lambda b, c: (b, 0, 0, 0))],
        out_shape=[jax.ShapeDtypeStruct((bsz, seq, A_DIM), BF16),
                   jax.ShapeDtypeStruct((bsz, A_HEADS, A_HEAD, A_STATE), F32)],
        scratch_shapes=[pltpu.VMEM((A_HEADS, A_HEAD, A_STATE), F32),
                        pltpu.VMEM((ch + 8, A_DIM), F32),
                        pltpu.VMEM((ch + 8, gn), F32),
                        pltpu.VMEM((ch + 8, gn), F32),
                        pltpu.VMEM((ch, A_DIM), F32)],
        compiler_params=_cparams(("parallel", "arbitrary")),
        name="ssd_prompt",
    )(proj, proj, proj, proj, proj, *[sp[n] for n in names])
    return y, st


def _ssd_step_kernel(z_ref, x_ref, b_ref, c_ref, dt_ref, cs_ref, cw, cb, dtb, anegx, expand, dsk, ng,
                     h_ref, y_ref, ho_ref, y_scr):
    bb = STEP_BB
    xbc = (x_ref[...], b_ref[...], c_ref[...])
    offs = (0, A_DIM, A_DIM + A_GN, CONV_CH)
    u = []
    for i in range(3):
        sl = slice(offs[i], offs[i + 1])
        acc = cb[:, sl] + xbc[i] * cw[CONV_W - 1:CONV_W, sl]
        for tap in range(CONV_W - 1):
            acc = acc + cs_ref[tap][:, sl] * cw[tap:tap + 1, sl]
        u.append(_silu(acc))
    ux, ub, uc = u
    dt = _softplus(dt_ref[...] + dtb[...])
    dtx = _dot_hi(dt, expand[...])
    dec = jnp.exp(dtx * anegx[...])
    xdt = ux * dtx
    rows8 = _iota((bb, LANES), 0)
    gw = A_DIM // A_GROUPS
    for b in range(bb):
        sel = (rows8 == b).astype(F32)
        hn = h_ref[b] * _dot_tn_hi(dec, sel)
        for g in range(A_GROUPS):
            rs = slice(g * gw, (g + 1) * gw)
            ns = slice(g * A_STATE, (g + 1) * A_STATE)
            hg = hn[rs, :] + _dot_tn(xdt[:, rs], sel * ub[b:b + 1, ns])
            ho_ref[b, rs, :] = hg
            y_scr[b:b + 1, rs] = _dot_nt(uc[:, ns], hg)[b:b + 1, :]
    y = y_scr[...] + dsk[...] * ux
    y_ref[...] = _ssd_gate_norm(y, z_ref[...], ng[...]).astype(BF16)


def _ssd_step(proj, conv_state, h0, p, sp):
    bsz = proj.shape[0]
    bb = STEP_BB
    cs = jnp.swapaxes(conv_state, 0, 1)
    hflat = h0.reshape(bsz, A_DIM, A_STATE)
    expand = (jnp.arange(LANES)[:, None] == (jnp.arange(A_DIM) // A_HEAD)[None, :]).astype(F32)
    anegx = jnp.repeat(-jnp.exp(p["a_log"]), A_HEAD).reshape(1, A_DIM)
    cw, cb = p["conv_w"], p["conv_b"].reshape(1, CONV_CH)

    def col(width, off):
        return pl.BlockSpec((bb, width), lambda i: (i, off // width))

    y, hn = pl.pallas_call(
        _ssd_step_kernel,
        grid=(bsz // bb,),
        in_specs=[col(A_DIM, P_Z), col(A_DIM, P_X), col(A_GN, P_B), col(A_GN, P_C), col(LANES, P_DT),
                  pl.BlockSpec((CONV_W - 1, bb, CONV_CH), lambda i: (0, i, 0)),
                  _full(cw.shape), _full(cb.shape), _full(sp["dtb"].shape), _full(anegx.shape),
                  _full(expand.shape), _full(sp["dsk"].shape), _full(sp["ng"].shape),
                  pl.BlockSpec((bb, A_DIM, A_STATE), lambda i: (i, 0, 0))],
        out_specs=[pl.BlockSpec((bb, A_DIM), lambda i: (i, 0)),
                   pl.BlockSpec((bb, A_DIM, A_STATE), lambda i: (i, 0, 0))],
        out_shape=[jax.ShapeDtypeStruct((bsz, A_DIM), BF16),
                   jax.ShapeDtypeStruct((bsz, A_DIM, A_STATE), F32)],
        scratch_shapes=[pltpu.VMEM((bb, A_DIM), F32)],
        compiler_params=_cparams(("parallel",)),
        name="ssd_step",
    )(proj, proj, proj, proj, proj, cs, cw, cb, sp["dtb"], anegx, expand, sp["dsk"], sp["ng"], hflat)
    return y, hn.reshape(h0.shape)


def _hgrn_gates(q, f, lbp):
    qf = _silu(q)
    kf = lbp[0:1, :] * jax.nn.sigmoid(-f)
    log_sig = jnp.minimum(f, 0.0) - jnp.log1p(jnp.exp(-jnp.abs(f)))
    a = lbp[1:2, :]
    b = lbp[2:3, :] + log_sig
    logf = jnp.maximum(a, b) + jnp.log1p(jnp.exp(-jnp.abs(a - b)))
    return qf, kf, logf


def _hgrn_out(o, g, ng):
    outs = []
    for h in range(B_HEADS):
        oh = o[:, h * B_HEAD:(h + 1) * B_HEAD]
        outs.append(oh * lax.rsqrt(jnp.mean(oh * oh, axis=-1, keepdims=True) + NORM_EPS) * ng)
    return jnp.concatenate(outs, axis=-1) * _silu(g)


def _hgrn_prompt_kernel(q_ref, f_ref, i_ref, g_ref, lbp, ng, o_ref, st_ref,
                        s_scr, cum_scr, q_scr, k_scr, o_scr):
    c = pl.program_id(1)
    tb, cs = HGRN_BLOCK, HGRN_CHUNK

    @pl.when(c == 0)
    def _():
        s_scr[...] = jnp.zeros_like(s_scr)

    qf, kf, logf = _hgrn_gates(q_ref[...], f_ref[...], lbp[...])
    r_i, c_i = _iota((tb, tb), 0), _iota((tb, tb), 1)
    bd = ((r_i // cs == c_i // cs) & (c_i <= r_i)).astype(F32)
    cum_scr[...] = _dot_hi(bd, logf)
    q_scr[...] = qf
    k_scr[...] = kf
    rows = _iota((cs, B_HEAD), 0)

    def chunk(cc, carry):
        r0 = pl.multiple_of(cc * cs, cs)
        for h in range(B_HEADS):
            hs = slice(h * B_HEAD, (h + 1) * B_HEAD)
            cu = cum_scr[pl.ds(r0, cs), hs]
            q = q_scr[pl.ds(r0, cs), hs]
            k = k_scr[pl.ds(r0, cs), hs]
            v = i_ref[pl.ds(r0, cs), hs]
            st = s_scr[h]
            o = _dot_nt(q * jnp.exp(cu), st)
            for s in range(cs):
                d = jnp.exp(jnp.where(rows >= s, cu - cu[s:s + 1, :], -jnp.inf))
                att = jnp.sum(q * k[s:s + 1, :] * d, axis=-1, keepdims=True)
                o = o + att * v[s:s + 1, :]
            o_scr[pl.ds(r0, cs), hs] = o
            cl = cu[cs - 1:cs, :]
            s_scr[h] = st * jnp.exp(cl) + _dot_tn(v, k * jnp.exp(cl - cu))
        return carry

    lax.fori_loop(0, tb // cs, chunk, 0)
    o_ref[...] = _hgrn_out(o_scr[...], g_ref[...], ng[...]).astype(BF16)

    @pl.when(c == pl.num_programs(1) - 1)
    def _():
        for h in range(B_HEADS):
            st_ref[h] = s_scr[h].T


def _hgrn_prompt(proj, lbp, ng):
    bsz, seq, _ = proj.shape
    tb = HGRN_BLOCK

    def col(off):
        return pl.BlockSpec((None, tb, B_DIM), lambda b, c: (b, c, off // B_DIM))

    o, st = pl.pallas_call(
        _hgrn_prompt_kernel,
        grid=(bsz, seq // tb),
        in_specs=[col(P_Q), col(P_F), col(P_I), col(P_G), _full(lbp.shape), _full(ng.shape)],
        out_specs=[pl.BlockSpec((None, tb, B_DIM), lambda b, c: (b, c, 0)),
                   pl.BlockSpec((None, B_HEADS, B_HEAD, B_HEAD), lambda b, c: (b, 0, 0, 0))],
        out_shape=[jax.ShapeDtypeStruct((bsz, seq, B_DIM), BF16),
                   jax.ShapeDtypeStruct((bsz, B_HEADS, B_HEAD, B_HEAD), F32)],
        scratch_shapes=[pltpu.VMEM((B_HEADS, B_HEAD, B_HEAD), F32)]
                       + [pltpu.VMEM((tb, B_DIM), F32)] * 4,
        compiler_params=_cparams(("parallel", "arbitrary")),
        name="hgrn_prompt",
    )(proj, proj, proj, proj, lbp, ng)
    return o, st


def _hgrn_step_kernel(q_ref, f_ref, i_ref, g_ref, lbp, ng, s_ref, o_ref, so_ref, o_scr):
    bb = STEP_BB
    qf, kf, logf = _hgrn_gates(q_ref[...], f_ref[...], lbp[...])
    vf = i_ref[...]
    dec = jnp.exp(logf)
    rows8 = _iota((bb, LANES), 0)
    for b in range(bb):
        sel = (rows8 == b).astype(F32)
        sn = s_ref[b] * _dot_tn_hi(dec, sel)
        for h in range(B_HEADS):
            hs = slice(h * B_HEAD, (h + 1) * B_HEAD)
            sh = sn[hs, :] + _dot_tn(kf[:, hs], sel * vf[b:b + 1, hs])
            so_ref[b, hs, :] = sh
            o_scr[b:b + 1, hs] = _dot(qf[:, hs], sh)[b:b + 1, :]
    o_ref[...] = _hgrn_out(o_scr[...], g_ref[...], ng[...]).astype(BF16)


def _hgrn_step(proj, s0, lbp, ng):
    bsz = proj.shape[0]
    bb = STEP_BB
    sflat = s0.reshape(bsz, B_DIM, B_HEAD)

    def col(off):
        return pl.BlockSpec((bb, B_DIM), lambda i: (i, off // B_DIM))

    o, sn = pl.pallas_call(
        _hgrn_step_kernel,
        grid=(bsz // bb,),
        in_specs=[col(P_Q), col(P_F), col(P_I), col(P_G), _full(lbp.shape), _full(ng.shape),
                  pl.BlockSpec((bb, B_DIM, B_HEAD), lambda i: (i, 0, 0))],
        out_specs=[pl.BlockSpec((bb, B_DIM), lambda i: (i, 0)),
                   pl.BlockSpec((bb, B_DIM, B_HEAD), lambda i: (i, 0, 0))],
        out_shape=[jax.ShapeDtypeStruct((bsz, B_DIM), BF16),
                   jax.ShapeDtypeStruct((bsz, B_DIM, B_HEAD), F32)],
        scratch_shapes=[pltpu.VMEM((bb, B_DIM), F32)],
        compiler_params=_cparams(("parallel",)),
        name="hgrn_step",
    )(proj, proj, proj, proj, lbp, ng, sflat)
    return o, sn.reshape(s0.shape)


RW_NAMES = ("mu_r", "mu_k", "mu_v", "mu_lr", "w0", "w2", "a0", "a2", "g2", "k_k", "k_a", "r_k")


def _rwkv_params(p):
    mu = p["shift_mu"].reshape(1, C_COLS)
    row = lambda a: a.reshape(1, C_DIM)
    return dict(
        mu_r=mu[:, :C_DIM], mu_k=mu[:, C_DIM:2 * C_DIM], mu_v=mu[:, 2 * C_DIM:3 * C_DIM],
        mu_lr=mu[:, 3 * C_DIM:],
        w0=row(p["w0"]), w2=p["w2"].astype(BF16), a0=row(p["a0"]), a2=p["a2"].astype(BF16),
        g2=p["g2"].astype(BF16), k_k=row(p["k_k"]), k_a=row(p["k_a"]), r_k=row(p["r_k"]),
        lnx_w=row(p["lnx_w"]), lnx_b=row(p["lnx_b"]),
    )


def _head_sum(x):
    return [jnp.sum(x[:, h * C_HEAD:(h + 1) * C_HEAD], axis=-1, keepdims=True) for h in range(C_HEADS)]


def _head_bcast(cols, fn=lambda c: c):
    rows = cols[0].shape[0]
    return jnp.concatenate([jnp.broadcast_to(fn(c), (rows, C_HEAD)) for c in cols], axis=-1)


def _rwkv_prep(r, k, v, lr, pr, pk, pv, plr, prm):
    mu_r, mu_k, mu_v, mu_lr, w0, w2, a0, a2, g2, k_k, k_a, r_k = prm
    r = r + (pr - r) * mu_r
    k = k + (pk - k) * mu_k
    v = v + (pv - v) * mu_v
    lr = lr + (plr - lr) * mu_lr
    wl, al, gl = lr[:, :C_LR_W], lr[:, C_LR_W:C_LR_W + C_LR_A], lr[:, C_LR_W + C_LR_A:]
    wraw = -_softplus(-(w0 + _dot(jnp.tanh(wl), w2))) - 0.5
    ld = -jnp.exp(wraw)
    a = jax.nn.sigmoid(a0 + _dot(al, a2))
    g = _dot(jax.nn.sigmoid(gl), g2)
    kk = k * k_k
    kk = kk * _head_bcast(_head_sum(kk * kk), lambda n2: 1.0 / jnp.maximum(jnp.sqrt(n2), L2_EPS))
    k2 = k * (1.0 + (a - 1.0) * k_a)
    bonus = _head_bcast(_head_sum(r * k2 * r_k)) * v
    return r, k2, v, kk, kk * a, ld, g, bonus


def _rwkv_out(y, bonus, g, lnx_w, lnx_b):
    n = float(C_HEAD)
    mean = _head_bcast(_head_sum(y), lambda s: s / n)
    yc = y - mean
    rstd = _head_bcast(_head_sum(yc * yc), lambda s: lax.rsqrt(s / n + GN_EPS))
    return (yc * rstd * lnx_w + lnx_b + bonus) * g


def _rwkv_a_kernel(r_ref, k_ref, v_ref, lr_ref, rp_ref, kp_ref, vp_ref, lrp_ref, *rest):
    prm = tuple(x[...] for x in rest[:len(RW_NAMES)])
    (kd_ref, rd_ref, bg_ref, m2_ref, rkv_ref, rb_ref, aab_ref, vk_ref, ge_ref, bonus_ref, gate_ref,
     xs_scr, b1_scr, b2_scr, b3_scr, b4_scr) = rest[len(RW_NAMES):]
    c = pl.program_id(1)
    tb, cs = RWKV_BLOCK, RWKV_CHUNK
    first = (c == 0)

    def prev(cur_ref, prev_ref, width):
        xs_scr[8:8 + tb, 0:width] = cur_ref[...]
        xs_scr[7:8, 0:width] = jnp.where(first, 0.0, prev_ref[7:8, :])
        return xs_scr[pl.ds(7, tb), 0:width]

    pr = prev(r_ref, rp_ref, C_DIM)
    pk = prev(k_ref, kp_ref, C_DIM)
    pv = prev(v_ref, vp_ref, C_DIM)
    plr = prev(lr_ref, lrp_ref, C_LR)
    r, k2, v, kk, beta, ld, g, bonus = _rwkv_prep(
        r_ref[...], k_ref[...], v_ref[...], lr_ref[...], pr, pk, pv, plr, prm)
    bonus_ref[...] = bonus
    gate_ref[...] = g

    r_i, c_i = _iota((tb, tb), 0), _iota((tb, tb), 1)
    same = (r_i // cs == c_i // cs)
    lcum = _dot_hi((same & (c_i <= r_i)).astype(F32), ld)
    lend = _dot_hi(same.astype(F32), ld)
    einv = jnp.exp(-lcum)
    eend = jnp.exp(lend - lcum)
    ge_ref[...] = jnp.exp(lend)
    kd_ref[...] = kk * jnp.exp(lcum - ld)
    rd_ref[...] = r * jnp.exp(lcum)
    bg_ref[...] = beta * eend
    b1_scr[...] = beta * einv
    b2_scr[...] = k2 * einv
    b3_scr[...] = k2 * eend
    b4_scr[...] = v
    rr, cc = _iota((cs, cs), 0), _iota((cs, cs), 1)
    strict, incl = rr > cc, rr >= cc

    def chunk(j, carry):
        r0 = pl.multiple_of(j * cs, cs)
        for h in range(C_HEADS):
            hs = slice(h * C_HEAD, (h + 1) * C_HEAD)
            kd = kd_ref[pl.ds(r0, cs), hs].astype(BF16)
            rd = rd_ref[pl.ds(r0, cs), hs].astype(BF16)
            bi = b1_scr[pl.ds(r0, cs), hs].astype(BF16)
            ki = b2_scr[pl.ds(r0, cs), hs].astype(BF16)
            kg = b3_scr[pl.ds(r0, cs), hs]
            vh = b4_scr[pl.ds(r0, cs), hs].astype(BF16)
            aab_ref[pl.ds(r0, cs), hs] = jnp.where(strict, _dot_nt(kd, bi), 0.0)
            aak = jnp.where(strict, _dot_nt(kd, ki), 0.0)
            rb_ref[pl.ds(r0, cs), hs] = jnp.where(incl, _dot_nt(rd, bi), 0.0)
            rk = jnp.where(incl, _dot_nt(rd, ki), 0.0)
            m2_ref[pl.ds(r0, cs), hs] = _dot(aak, vh)
            rkv_ref[pl.ds(r0, cs), hs] = _dot(rk, vh)
            vk_ref[pl.ds(r0, cs), hs] = _dot_tn(vh, kg)
        return carry

    lax.fori_loop(0, tb // cs, chunk, 0)


def _tri_solve_kernel(a_ref, t_ref):
    n, _, npb = a_ref.shape
    rowj = _iota((n, npb), 0)

    def row(t, carry):
        def inner(s, acc):
            return acc - a_ref[t, pl.ds(s, 1), :] * t_ref[s]

        t_ref[t] = lax.fori_loop(0, t, inner, (rowj == t).astype(F32))
        return carry

    lax.fori_loop(0, n, row, 0)


def _tri_solve(a):
    n, _, nprob = a.shape
    npb = min(256, nprob)
    return pl.pallas_call(
        _tri_solve_kernel,
        grid=(nprob // npb,),
        in_specs=[pl.BlockSpec((n, n, npb), lambda i: (0, 0, i))],
        out_specs=pl.BlockSpec((n, n, npb), lambda i: (0, 0, i)),
        out_shape=jax.ShapeDtypeStruct(a.shape, F32),
        compiler_params=_cparams(("parallel",)),
        name="tri_solve",
    )(a)


def _rwkv_c_kernel(t_ref, kd_ref, rd_ref, bg_ref, m2_ref, rkv_ref, rb_ref, vk_ref, ge_ref,
                   bonus_ref, gate_ref, lnw, lnb, y_ref, st_ref, s_scr, y_scr):
    c = pl.program_id(1)
    tb, cs = RWKV_BLOCK, RWKV_CHUNK

    @pl.when(c == 0)
    def _():
        s_scr[...] = jnp.zeros_like(s_scr)

    def chunk(j, carry):
        r0 = pl.multiple_of(j * cs, cs)
        for h in range(C_HEADS):
            hs = slice(h * C_HEAD, (h + 1) * C_HEAD)
            tm = t_ref[pl.ds(r0, cs), hs].astype(BF16)
            w = _dot(tm, kd_ref[pl.ds(r0, cs), hs])
            u = _dot(tm, m2_ref[pl.ds(r0, cs), hs])
            s = s_scr[h]
            p = -_dot_nt(w, s) - u
            y = _dot_nt(rd_ref[pl.ds(r0, cs), hs], s) + _dot(rb_ref[pl.ds(r0, cs), hs], p)
            y_scr[pl.ds(r0, cs), hs] = y + rkv_ref[pl.ds(r0, cs), hs]
            s_scr[h] = (s * ge_ref[pl.ds(r0, 1), hs] + _dot_tn(p, bg_ref[pl.ds(r0, cs), hs])
                        + vk_ref[pl.ds(r0, cs), hs])
        return carry

    lax.fori_loop(0, tb // cs, chunk, 0)
    y_ref[...] = _rwkv_out(y_scr[...], bonus_ref[...], gate_ref[...], lnw[...], lnb[...]).astype(BF16)

    @pl.when(c == pl.num_programs(1) - 1)
    def _():
        st_ref[...] = s_scr[...]


def _rwkv_prompt(proj, rp):
    bsz, seq, _ = proj.shape
    tb, cs = RWKV_BLOCK, RWKV_CHUNK
    nblk = seq // tb

    def col(width, off):
        return pl.BlockSpec((None, tb, width), lambda b, c: (b, c, off // width))

    def pcol(width, off):
        return pl.BlockSpec((None, 8, width), lambda b, c: (b, jnp.maximum(c * (tb // 8) - 1, 0), off // width))

    tok = pl.BlockSpec((None, tb, C_DIM), lambda b, c: (b, c, 0))
    tok_shape = jax.ShapeDtypeStruct((bsz, seq, C_DIM), F32)
    prm = [rp[n] for n in RW_NAMES]
    outs = pl.pallas_call(
        _rwkv_a_kernel,
        grid=(bsz, nblk),
        in_specs=[col(C_DIM, P_R), col(C_DIM, P_K), col(C_DIM, P_V), col(C_LR, P_LR),
                  pcol(C_DIM, P_R), pcol(C_DIM, P_K), pcol(C_DIM, P_V), pcol(C_LR, P_LR)]
                 + [_full(x.shape) for x in prm],
        out_specs=[tok] * 11,
        out_shape=[tok_shape] * 11,
        scratch_shapes=[pltpu.VMEM((tb + 8, C_DIM), F32)] + [pltpu.VMEM((tb, C_DIM), F32)] * 4,
        compiler_params=_cparams(("parallel", "parallel")),
        name="rwkv_prep",
    )(proj, proj, proj, proj, proj, proj, proj, proj, *prm)
    kd, rd, bg, m2, rkv, rb, aab, vk, ge, bonus, gate = outs

    nch = seq // cs
    a_t = aab.reshape(bsz, nch, cs, C_HEADS, cs).transpose(2, 4, 0, 1, 3).reshape(cs, cs, -1)
    t_t = _tri_solve(a_t)
    tmat = t_t.reshape(cs, cs, bsz, nch, C_HEADS).transpose(2, 3, 0, 4, 1).reshape(bsz, seq, C_DIM)

    y, st = pl.pallas_call(
        _rwkv_c_kernel,
        grid=(bsz, nblk),
        in_specs=[tok] * 11 + [_full((1, C_DIM))] * 2,
        out_specs=[tok, pl.BlockSpec((None, C_HEADS, C_HEAD, C_HEAD), lambda b, c: (b, 0, 0, 0))],
        out_shape=[jax.ShapeDtypeStruct((bsz, seq, C_DIM), BF16),
                   jax.ShapeDtypeStruct((bsz, C_HEADS, C_HEAD, C_HEAD), F32)],
        scratch_shapes=[pltpu.VMEM((C_HEADS, C_HEAD, C_HEAD), F32), pltpu.VMEM((tb, C_DIM), F32)],
        compiler_params=_cparams(("parallel", "arbitrary")),
        name="rwkv_scan",
    )(tmat, kd, rd, bg, m2, rkv, rb, vk, ge, bonus, gate, rp["lnx_w"], rp["lnx_b"])
    return y, st


def _rwkv_step_kernel(r_ref, k_ref, v_ref, lr_ref, sh_ref, *rest):
    prm = tuple(x[...] for x in rest[:len(RW_NAMES)])
    lnw, lnb, s_ref, y_ref, so_ref, y_scr = rest[len(RW_NAMES):]
    bb = STEP_BB
    sh = sh_ref[...]
    r, k2, v, kk, beta, ld, g, bonus = _rwkv_prep(
        r_ref[...], k_ref[...], v_ref[...], lr_ref[...],
        sh[:, :C_DIM], sh[:, C_DIM:2 * C_DIM], sh[:, 2 * C_DIM:3 * C_DIM], sh[:, 3 * C_DIM:], prm)
    w = jnp.exp(ld)
    rows8 = _iota((bb, C_HEAD), 0)
    for b in range(bb):
        for h in range(C_HEADS):
            hs = slice(h * C_HEAD, (h + 1) * C_HEAD)
            s = s_ref[b, hs, :]
            sa = -jnp.sum(s * kk[b:b + 1, hs], axis=-1, keepdims=True)
            vk = _dot_tn_hi(v[:, hs], jnp.where(rows8 == b, k2[:, hs], 0.0))
            sn = s * w[b:b + 1, hs] + sa * beta[b:b + 1, hs] + vk
            so_ref[b, hs, :] = sn
            y_scr[b:b + 1, hs] = _dot_nt(r[:, hs], sn)[b:b + 1, :]
    y_ref[...] = _rwkv_out(y_scr[...], bonus, g, lnw[...], lnb[...]).astype(BF16)


def _rwkv_step(proj, shift, s0, rp):
    bsz = proj.shape[0]
    bb = STEP_BB
    sflat = s0.reshape(bsz, C_DIM, C_HEAD)

    def col(width, off):
        return pl.BlockSpec((bb, width), lambda i: (i, off // width))

    prm = [rp[n] for n in RW_NAMES]
    y, sn = pl.pallas_call(
        _rwkv_step_kernel,
        grid=(bsz // bb,),
        in_specs=[col(C_DIM, P_R), col(C_DIM, P_K), col(C_DIM, P_V), col(C_LR, P_LR),
                  pl.BlockSpec((bb, C_COLS), lambda i: (i, 0))]
                 + [_full(x.shape) for x in prm] + [_full((1, C_DIM))] * 2
                 + [pl.BlockSpec((bb, C_DIM, C_HEAD), lambda i: (i, 0, 0))],
        out_specs=[pl.BlockSpec((bb, C_DIM), lambda i: (i, 0)),
                   pl.BlockSpec((bb, C_DIM, C_HEAD), lambda i: (i, 0, 0))],
        out_shape=[jax.ShapeDtypeStruct((bsz, C_DIM), BF16),
                   jax.ShapeDtypeStruct((bsz, C_DIM, C_HEAD), F32)],
        scratch_shapes=[pltpu.VMEM((bb, C_DIM), F32)],
        compiler_params=_cparams(("parallel",)),
        name="rwkv_step",
    )(proj, proj, proj, proj, shift, *prm, rp["lnx_w"], rp["lnx_b"], sflat)
    return y, sn.reshape(s0.shape)


def _layout_w_in(w):
    z, x = w[:, :A_DIM], w[:, A_DIM:2 * A_DIM]
    bm, cm = w[:, 2 * A_DIM:2 * A_DIM + A_GN], w[:, 2 * A_DIM + A_GN:A_DIM + CONV_CH]
    o = A_DIM + CONV_CH
    dt = w[:, o:o + A_HEADS]
    qfig = w[:, o + A_HEADS:o + A_HEADS + 4 * B_DIM]
    rw = w[:, o + A_HEADS + 4 * B_DIM:]
    pad = jnp.zeros((w.shape[0], DT_PAD), w.dtype)
    return jnp.concatenate([z, qfig, rw, dt, pad, x, bm, cm], axis=1).astype(BF16)


def _xbc_cols(proj):
    return jnp.concatenate([proj[..., P_X:P_X + A_DIM], proj[..., P_B:P_B + A_GN],
                            proj[..., P_C:P_C + A_GN]], axis=-1)


def _moe_route(logits, n_tok):
    top_logit, top_idx = lax.top_k(logits, TOP_K)
    gate = jax.nn.softmax(top_logit, axis=-1)
    n_assign = n_tok * TOP_K
    flat_e = top_idx.reshape(-1).astype(jnp.int32)
    order = jnp.argsort(flat_e)
    se = flat_e[order]
    counts = jnp.sum(jax.nn.one_hot(flat_e, N_EXPERTS, dtype=jnp.int32), axis=0)
    padded = (counts + MOE_BM - 1) // MOE_BM * MOE_BM
    pad_end = jnp.cumsum(padded)
    pad_start = pad_end - padded
    start = jnp.cumsum(counts) - counts
    slot_sorted = pad_start[se] + jnp.arange(n_assign, dtype=jnp.int32) - start[se]
    n_blocks = -(-n_assign // MOE_BM) + N_EXPERTS
    n_slots = n_blocks * MOE_BM
    slot_tok = jnp.full((n_slots,), n_tok, jnp.int32).at[slot_sorted].set(order // TOP_K)
    slot_of = jnp.zeros((n_assign,), jnp.int32).at[order].set(slot_sorted).reshape(n_tok, TOP_K)
    block_start = jnp.arange(n_blocks, dtype=jnp.int32) * MOE_BM
    nvalid = (pad_end[-1] // MOE_BM).astype(jnp.int32)
    bexp = jnp.minimum(jnp.searchsorted(pad_end, block_start, side="right"), N_EXPERTS - 1).astype(jnp.int32)
    last = bexp[jnp.maximum(nvalid - 1, 0)]
    bexp = jnp.where(jnp.arange(n_blocks) < nvalid, bexp, last)
    return gate, slot_tok, slot_of, bexp, nvalid.reshape(1)


def _moe(x, g, rw, wg, wu, wd):
    n_tok = x.shape[0]
    h, logits = _norm_router(x, g, rw, 128)
    gate, slot_tok, slot_of, bexp, nvalid = _moe_route(logits[:, :N_EXPERTS], n_tok)
    h_pad = jnp.concatenate([h, jnp.zeros((1, h.shape[1]), h.dtype)], axis=0)
    xb = h_pad[slot_tok]
    yb = _moe_experts(xb, bexp, nvalid, wg, wu, wd)
    return gate[:, 0:1] * yb[slot_of[:, 0]] + gate[:, 1:2] * yb[slot_of[:, 1]]


def kernel(x_prompt, x_sample, state_ssm, state_conv, state_hgrn, state_rwkv, state_shift, norm1_g, w_in, conv_w, conv_b, dt_bias, a_log, d_skip, ssm_norm_g, lb_logits, hgrn_norm_g, shift_mu, w0, w2, a0, a2, g2, k_k, k_a, r_k, lnx_w, lnx_b, w_out, norm2_g, ffn_w_gate, ffn_w_up, ffn_w_down, router_w, exp_w_gate, exp_w_up, exp_w_down, final_norm_g):
    depth = w_in.shape[0]
    bp, seq, _ = x_prompt.shape
    bs = x_sample.shape[0]
    tp = bp * seq
    lb_soft = jax.nn.softmax(lb_logits.astype(F32), axis=0)
    lower_bounds = jnp.clip(jnp.cumsum(lb_soft, axis=0) - lb_soft[0:1], 0.0, 1.0)

    xp = x_prompt.reshape(tp, D_MODEL)
    xs = x_sample.reshape(bs, D_MODEL)
    outs = {k: [] for k in ("p_ssm", "p_conv", "p_hgrn", "p_rwkv", "p_shift",
                            "s_ssm", "s_conv", "s_hgrn", "s_rwkv", "s_shift")}
    y_moe_p = y_moe_s = None
    for layer in range(depth):
        p = dict(conv_w=conv_w[layer], conv_b=conv_b[layer], dt_bias=dt_bias[layer], a_log=a_log[layer],
                 d_skip=d_skip[layer], ssm_norm_g=ssm_norm_g[layer], shift_mu=shift_mu[layer],
                 w0=w0[layer], w2=w2[layer], a0=a0[layer], a2=a2[layer], g2=g2[layer], k_k=k_k[layer],
                 k_a=k_a[layer], r_k=r_k[layer].reshape(-1), lnx_w=lnx_w[layer], lnx_b=lnx_b[layer])
        sp = _ssd_params(p)
        rp = _rwkv_params(p)
        lb = lower_bounds[layer]
        lbp = jnp.stack([1.0 - lb, jnp.log(lb), jnp.log1p(-lb)])
        hng = hgrn_norm_g[layer].reshape(1, B_HEAD)
        w_in_l = _layout_w_in(w_in[layer])
        w_out_l = w_out[layer].astype(BF16)

        proj = _norm_matmul(xp, norm1_g[layer], w_in_l, 512, 512).reshape(bp, seq, P_COLS)
        ya, ssm_p = _ssd_prompt(proj, sp)
        yb, hgrn_p = _hgrn_prompt(proj, lbp, hng)
        yc, rwkv_p = _rwkv_prompt(proj, rp)
        mix = jnp.concatenate([ya, yb, yc], axis=-1).reshape(tp, D_MODEL)
        xp = _matmul_res(mix, w_out_l, xp, 512, 512)
        outs["p_ssm"].append(ssm_p)
        outs["p_conv"].append(_xbc_cols(proj[:, seq - (CONV_W - 1):, :]))
        outs["p_hgrn"].append(hgrn_p)
        outs["p_rwkv"].append(rwkv_p)
        outs["p_shift"].append(proj[:, seq - 1, P_R:P_R + C_COLS])

        projs = _norm_matmul(xs, norm1_g[layer], w_in_l, 128, 512)
        ya, ssm_s = _ssd_step(projs, state_conv[layer], state_ssm[layer], p, sp)
        yb, hgrn_s = _hgrn_step(projs, state_hgrn[layer], lbp, hng)
        yc, rwkv_s = _rwkv_step(projs, state_shift[layer], state_rwkv[layer], rp)
        mix = jnp.concatenate([ya, yb, yc], axis=-1)
        xs = _matmul_res(mix, w_out_l, xs, 128, 512)
        outs["s_ssm"].append(ssm_s)
        outs["s_conv"].append(jnp.concatenate([state_conv[layer][:, 1:, :], _xbc_cols(projs)[:, None, :]], axis=1))
        outs["s_hgrn"].append(hgrn_s)
        outs["s_rwkv"].append(rwkv_s)
        outs["s_shift"].append(projs[:, P_R:P_R + C_COLS])

        j = layer // 2
        if layer % 2 == 0:
            fpad = ((0, 0), (0, D_FF_PAD - D_FF))
            wg = jnp.pad(ffn_w_gate[j], fpad).astype(BF16)
            wu = jnp.pad(ffn_w_up[j], fpad).astype(BF16)
            wd = jnp.pad(ffn_w_down[j], fpad[::-1]).astype(BF16)
            xp = _ffn(xp, norm2_g[layer], wg, wu, wd, 512)
            xs = _ffn(xs, norm2_g[layer], wg, wu, wd, 128)
            y_moe_p = y_moe_s = None
        else:
            if y_moe_p is not None:
                xp, xs = xp + y_moe_p, xs + y_moe_s
            rw = jnp.pad(router_w[j], ((0, 0), (0, LANES - N_EXPERTS)))
            xall = jnp.concatenate([xp, xs], axis=0)
            y_moe = _moe(xall, norm2_g[layer], rw, exp_w_gate[j].astype(BF16), exp_w_up[j].astype(BF16),
                         exp_w_down[j].astype(BF16))
            if layer == depth - 1:
                y_moe_p, y_moe_s = y_moe[:tp], y_moe[tp:]
            else:
                xp, xs = xp + y_moe[:tp], xs + y_moe[tp:]

    zp = y_moe_p if y_moe_p is not None else jnp.zeros_like(xp)
    zs = y_moe_s if y_moe_s is not None else jnp.zeros_like(xs)
    y_prompt = _final_norm(xp, zp, final_norm_g, 512).reshape(bp, seq, D_MODEL)
    y_sample = _final_norm(xs, zs, final_norm_g, 128).reshape(bs, 1, D_MODEL)
    st = {k: jnp.stack(v) for k, v in outs.items()}
    return (y_prompt, y_sample, st["p_ssm"], st["p_conv"], st["p_hgrn"], st["p_rwkv"], st["p_shift"],
            st["s_ssm"], st["s_conv"], st["s_hgrn"], st["s_rwkv"], st["s_shift"])
```

```python
import functools
import math

import jax
import jax.numpy as jnp
from jax import lax
from jax.experimental import pallas as pl
from jax.experimental.pallas import tpu as pltpu

F32 = jnp.float32
BF16 = jnp.bfloat16
HI = lax.Precision.HIGHEST

D_MODEL = 2048
A_DIM, A_HEAD, A_HEADS, A_GROUPS, A_STATE = 1024, 64, 16, 2, 128
A_GN = A_GROUPS * A_STATE
CONV_W = 4
CONV_CH = A_DIM + 2 * A_GN
B_DIM, B_HEAD, B_HEADS = 512, 128, 4
C_DIM, C_HEAD, C_HEADS = 512, 64, 8
C_LR_W, C_LR_A, C_LR_G = 64, 64, 128
C_LR = C_LR_W + C_LR_A + C_LR_G
C_COLS = 3 * C_DIM + C_LR
IN_COLS = A_DIM + CONV_CH + A_HEADS + 4 * B_DIM + C_COLS
D_FF, N_EXPERTS, TOP_K, E_FF = 5504, 8, 2, 7168
NORM_EPS, GN_EPS, L2_EPS = 1e-6, 64e-5, 1e-12

P_Z, P_Q, P_F, P_I, P_G = 0, 1024, 1536, 2048, 2560
P_R, P_K, P_V, P_LR, P_DT = 3072, 3584, 4096, 4608, 4864
P_X, P_B, P_C, P_COLS = 5120, 6144, 6400, 6656
LANES = 128
DT_PAD = P_X - P_DT - A_HEADS

SSD_CHUNK = 256
HGRN_BLOCK, HGRN_CHUNK = 256, 16
RWKV_BLOCK, RWKV_CHUNK = 256, 64
STEP_BB = 8

FFN_TF, FFN_TN = 512, 512
MOE_BM, MOE_TF, MOE_TN = 256, 1024, 512

NT_DIMS = (((1,), (1,)), ((), ()))
TN_DIMS = (((0,), (0,)), ((), ()))


def _cparams(sem, vmem_mb=48):
    return pltpu.CompilerParams(dimension_semantics=sem, vmem_limit_bytes=vmem_mb * 1024 * 1024)


def _dot(a, b):
    return jnp.dot(a.astype(BF16), b.astype(BF16), preferred_element_type=F32)


def _dot_nt(a, b):
    return lax.dot_general(a.astype(BF16), b.astype(BF16), NT_DIMS, preferred_element_type=F32)


def _dot_tn(a, b):
    return lax.dot_general(a.astype(BF16), b.astype(BF16), TN_DIMS, preferred_element_type=F32)


def _dot_hi(a, b):
    return jnp.dot(a, b, precision=HI, preferred_element_type=F32)


def _dot_nt_hi(a, b):
    return lax.dot_general(a, b, NT_DIMS, precision=HI, preferred_element_type=F32)


def _dot_tn_hi(a, b):
    return lax.dot_general(a, b, TN_DIMS, precision=HI, preferred_element_type=F32)


def _silu(x):
    return x * jax.nn.sigmoid(x)


def _softplus(x):
    return jnp.maximum(x, 0.0) + jnp.log1p(jnp.exp(-jnp.abs(x)))


def _iota(shape, dim):
    return lax.broadcasted_iota(jnp.int32, shape, dim)


def _rms(x, g):
    return x * lax.rsqrt(jnp.mean(x * x, axis=-1, keepdims=True) + NORM_EPS) * g


def _full(shape):
    nd = len(shape)
    return pl.BlockSpec(shape, lambda *_: (0,) * nd)


def _norm_matmul_kernel(x_ref, g_ref, w_ref, o_ref, h_scr):
    @pl.when(pl.program_id(1) == 0)
    def _():
        h_scr[...] = _rms(x_ref[...], g_ref[...]).astype(BF16)

    o_ref[...] = jnp.dot(h_scr[...], w_ref[...], preferred_element_type=F32)


def _norm_matmul(x, g, w, tm, tn):
    m, k = x.shape
    n = w.shape[1]
    return pl.pallas_call(
        _norm_matmul_kernel,
        grid=(m // tm, n // tn),
        in_specs=[pl.BlockSpec((tm, k), lambda i, j: (i, 0)),
                  pl.BlockSpec((1, k), lambda i, j: (0, 0)),
                  pl.BlockSpec((k, tn), lambda i, j: (0, j))],
        out_specs=pl.BlockSpec((tm, tn), lambda i, j: (i, j)),
        out_shape=jax.ShapeDtypeStruct((m, n), F32),
        scratch_shapes=[pltpu.VMEM((tm, k), BF16)],
        compiler_params=_cparams(("parallel", "arbitrary")),
        name="norm_matmul",
    )(x, g.reshape(1, k), w)


def _out_proj_kernel(routed, a_ref, w_ref, r_ref, g_ref, *rest):
    x = r_ref[...] + jnp.dot(a_ref[...], w_ref[...], preferred_element_type=F32)
    h = _rms(x, g_ref[...])
    if routed:
        rw_ref, x_ref, h_ref, l_ref = rest
        l_ref[...] = _dot_hi(h, rw_ref[...])
    else:
        x_ref, h_ref = rest
    x_ref[...] = x
    h_ref[...] = h.astype(BF16)


def _out_proj(a, w, res, g, rw, tm):
    m, k = a.shape
    n = w.shape[1]
    routed = rw is not None
    row = lambda width: pl.BlockSpec((tm, width), lambda i: (i, 0))
    in_specs = [row(k), _full(w.shape), row(n), _full((1, n))]
    out_specs = [row(n), row(n)]
    out_shape = [jax.ShapeDtypeStruct((m, n), F32), jax.ShapeDtypeStruct((m, n), BF16)]
    args = [a, w, res, g.reshape(1, n)]
    if routed:
        in_specs.append(_full(rw.shape))
        out_specs.append(row(LANES))
        out_shape.append(jax.ShapeDtypeStruct((m, LANES), F32))
        args.append(rw)
    outs = pl.pallas_call(
        functools.partial(_out_proj_kernel, routed),
        grid=(m // tm,),
        in_specs=in_specs, out_specs=out_specs, out_shape=out_shape,
        compiler_params=_cparams(("parallel",)),
        name="out_proj",
    )(*args)
    return outs if routed else (*outs, None)


def _ffn_up_kernel(h_ref, wg_ref, wu_ref, o_ref, wg_s, wu_s):
    @pl.when(pl.program_id(1) == 0)
    def _():
        wg_s[...] = wg_ref[...].astype(BF16)
        wu_s[...] = wu_ref[...].astype(BF16)

    h = h_ref[...]
    a = jnp.dot(h, wg_s[...], preferred_element_type=F32)
    u = jnp.dot(h, wu_s[...], preferred_element_type=F32)
    o_ref[...] = (_silu(a) * u).astype(BF16)


def _ffn_down_kernel(h_ref, wd_ref, x_ref, o_ref, wd_s):
    @pl.when(pl.program_id(1) == 0)
    def _():
        wd_s[...] = wd_ref[...].astype(BF16)

    o_ref[...] = x_ref[...] + jnp.dot(h_ref[...], wd_s[...], preferred_element_type=F32)


def _ffn(x, h, wg, wu, wd, tm):
    m, k = h.shape
    ff = wg.shape[1]
    tf, tn = FFN_TF, FFN_TN
    act = pl.pallas_call(
        _ffn_up_kernel,
        grid=(pl.cdiv(ff, tf), m // tm),
        in_specs=[pl.BlockSpec((tm, k), lambda f, i: (i, 0)),
                  pl.BlockSpec((k, tf), lambda f, i: (0, f)),
                  pl.BlockSpec((k, tf), lambda f, i: (0, f))],
        out_specs=pl.BlockSpec((tm, tf), lambda f, i: (i, f)),
        out_shape=jax.ShapeDtypeStruct((m, ff), BF16),
        scratch_shapes=[pltpu.VMEM((k, tf), BF16)] * 2,
        compiler_params=_cparams(("arbitrary", "arbitrary")),
        name="ffn_up",
    )(h, wg, wu)
    return pl.pallas_call(
        _ffn_down_kernel,
        grid=(k // tn, m // tm),
        in_specs=[pl.BlockSpec((tm, ff), lambda n, i: (i, 0)),
                  pl.BlockSpec((ff, tn), lambda n, i: (0, n)),
                  pl.BlockSpec((tm, tn), lambda n, i: (i, n))],
        out_specs=pl.BlockSpec((tm, tn), lambda n, i: (i, n)),
        out_shape=jax.ShapeDtypeStruct((m, k), F32),
        scratch_shapes=[pltpu.VMEM((ff, tn), BF16)],
        compiler_params=_cparams(("arbitrary", "arbitrary")),
        name="ffn_down",
    )(act, wd, x)


def _moe_up_kernel(bexp_ref, nv_ref, first_ref, x_ref, wg_ref, wu_ref, o_ref, wg_s, wu_s):
    b = pl.program_id(1)

    @pl.when(first_ref[b] == 1)
    def _():
        wg_s[...] = wg_ref[...].astype(BF16)
        wu_s[...] = wu_ref[...].astype(BF16)

    @pl.when(b < nv_ref[0])
    def _():
        x = x_ref[...]
        a = jnp.dot(x, wg_s[...], preferred_element_type=F32)
        u = jnp.dot(x, wu_s[...], preferred_element_type=F32)
        o_ref[...] = (_silu(a) * u).astype(BF16)


def _moe_down_kernel(bexp_ref, nv_ref, first_ref, h_ref, wd_ref, o_ref, wd_s):
    b = pl.program_id(1)

    @pl.when(first_ref[b] == 1)
    def _():
        wd_s[...] = wd_ref[...].astype(BF16)

    @pl.when(b < nv_ref[0])
    def _():
        o_ref[...] = jnp.dot(h_ref[...], wd_s[...], preferred_element_type=F32)


def _moe_experts(xb, bexp, nvalid, first, wg, wu, wd):
    n_slots, k = xb.shape
    ff = wg.shape[2]
    nb = n_slots // MOE_BM
    tf, tn = MOE_TF, MOE_TN

    def blk(b, nv):
        return jnp.minimum(b, nv[0] - 1)

    act = pl.pallas_call(
        _moe_up_kernel,
        grid_spec=pltpu.PrefetchScalarGridSpec(
            num_scalar_prefetch=3,
            grid=(ff // tf, nb),
            in_specs=[pl.BlockSpec((MOE_BM, k), lambda f, b, be, nv, fr: (blk(b, nv), 0)),
                      pl.BlockSpec((None, k, tf), lambda f, b, be, nv, fr: (be[b], 0, f)),
                      pl.BlockSpec((None, k, tf), lambda f, b, be, nv, fr: (be[b], 0, f))],
            out_specs=pl.BlockSpec((MOE_BM, tf), lambda f, b, be, nv, fr: (blk(b, nv), f)),
            scratch_shapes=[pltpu.VMEM((k, tf), BF16)] * 2,
        ),
        out_shape=jax.ShapeDtypeStruct((n_slots, ff), BF16),
        compiler_params=_cparams(("arbitrary", "arbitrary"), 56),
        name="moe_up",
    )(bexp, nvalid, first, xb, wg, wu)
    return pl.pallas_call(
        _moe_down_kernel,
        grid_spec=pltpu.PrefetchScalarGridSpec(
            num_scalar_prefetch=3,
            grid=(k // tn, nb),
            in_specs=[pl.BlockSpec((MOE_BM, ff), lambda n, b, be, nv, fr: (blk(b, nv), 0)),
                      pl.BlockSpec((None, ff, tn), lambda n, b, be, nv, fr: (be[b], 0, n))],
            out_specs=pl.BlockSpec((MOE_BM, tn), lambda n, b, be, nv, fr: (blk(b, nv), n)),
            scratch_shapes=[pltpu.VMEM((ff, tn), BF16)],
        ),
        out_shape=jax.ShapeDtypeStruct((n_slots, k), F32),
        compiler_params=_cparams(("arbitrary", "arbitrary"), 56),
        name="moe_down",
    )(bexp, nvalid, first, act, wd)


def _final_norm_kernel(x_ref, y_ref, g_ref, o_ref):
    o_ref[...] = _rms(x_ref[...] + y_ref[...], g_ref[...])


def _final_norm(x, y, g, tm):
    m, k = x.shape
    return pl.pallas_call(
        _final_norm_kernel,
        grid=(m // tm,),
        in_specs=[pl.BlockSpec((tm, k), lambda i: (i, 0)),
                  pl.BlockSpec((tm, k), lambda i: (i, 0)),
                  pl.BlockSpec((1, k), lambda i: (0, 0))],
        out_specs=pl.BlockSpec((tm, k), lambda i: (i, 0)),
        out_shape=jax.ShapeDtypeStruct((m, k), F32),
        compiler_params=_cparams(("parallel",)),
        name="final_norm",
    )(x, y, g.reshape(1, k))


def _ssd_gate_norm(y, z, ng):
    y = y * _silu(z)
    gw = A_DIM // A_GROUPS
    outs = []
    for g in range(A_GROUPS):
        yg = y[:, g * gw:(g + 1) * gw]
        outs.append(yg * lax.rsqrt(jnp.mean(yg * yg, axis=-1, keepdims=True) + NORM_EPS))
    return jnp.concatenate(outs, axis=-1) * ng


def _ssd_prompt_kernel(z_ref, x_ref, b_ref, c_ref, dt_ref, cwx, cwb, cwc, cbx, cbb, cbc,
                       dtb, aneg, dsk, ng, y_ref, st_ref, h_scr, xpx, xpb, xpc, yacc):
    c = pl.program_id(1)
    ch = SSD_CHUNK

    @pl.when(c == 0)
    def _():
        h_scr[...] = jnp.zeros_like(h_scr)
        xpx[0:8, :] = jnp.zeros((8, A_DIM), F32)
        xpb[0:8, :] = jnp.zeros((8, A_STATE * A_GROUPS), F32)
        xpc[0:8, :] = jnp.zeros((8, A_STATE * A_GROUPS), F32)

    xpx[8:8 + ch, :] = x_ref[...]
    xpb[8:8 + ch, :] = b_ref[...]
    xpc[8:8 + ch, :] = c_ref[...]

    def conv(xp, cw, cb):
        acc = cb[...] + xp[pl.ds(8 - (CONV_W - 1), ch), :] * cw[0:1, :]
        for tap in range(1, CONV_W):
            acc = acc + xp[pl.ds(8 - (CONV_W - 1) + tap, ch), :] * cw[tap:tap + 1, :]
        return _silu(acc)

    ux = conv(xpx, cwx, cbx)
    ub = conv(xpb, cwb, cbb)
    uc = conv(xpc, cwc, cbc)
    xpx[0:8, :] = xpx[ch:ch + 8, :]
    xpb[0:8, :] = xpb[ch:ch + 8, :]
    xpc[0:8, :] = xpc[ch:ch + 8, :]

    dt = _softplus(dt_ref[...] + dtb[...])
    la = dt * aneg[...]
    tri = (_iota((ch, ch), 1) <= _iota((ch, ch), 0)).astype(F32)
    cum = _dot_hi(tri, la)
    eye = (_iota((A_HEADS, LANES), 0) == _iota((A_HEADS, LANES), 1)).astype(F32)
    cum_r = _dot_nt_hi(eye, cum)
    dt_r = _dot_nt_hi(eye, dt)
    cl = cum[ch - 1:ch, :]
    tail = jnp.exp(cl - cum) * dt
    ecum = jnp.exp(cum)
    ecl = jnp.exp(cl)
    causal = _iota((ch, ch), 1) <= _iota((ch, ch), 0)
    rep = A_HEADS // A_GROUPS

    for g in range(A_GROUPS):
        bg = ub[:, g * A_STATE:(g + 1) * A_STATE].astype(BF16)
        cg = uc[:, g * A_STATE:(g + 1) * A_STATE].astype(BF16)
        cb_g = _dot_nt(cg, bg)
        for h in range(g * rep, (g + 1) * rep):
            hs = slice(h * A_HEAD, (h + 1) * A_HEAD)
            seg = cum[:, h:h + 1] - cum_r[h:h + 1, :]
            w = jnp.exp(jnp.where(causal, seg, -jnp.inf)) * cb_g * dt_r[h:h + 1, :]
            xh = ux[:, hs]
            hst = h_scr[h]
            y_h = _dot(w, xh) + _dot_nt(cg, hst) * ecum[:, h:h + 1]
            yacc[:, hs] = y_h
            h_scr[h] = hst * ecl[:, h:h + 1] + _dot_tn(xh * tail[:, h:h + 1], bg)

    y = yacc[...] + dsk[...] * ux
    y_ref[...] = _ssd_gate_norm(y, z_ref[...], ng[...]).astype(BF16)

    @pl.when(c == pl.num_programs(1) - 1)
    def _():
        st_ref[...] = h_scr[...]


def _ssd_params(p):
    cw, cb = p["conv_w"], p["conv_b"].reshape(1, CONV_CH)
    pad = LANES - A_HEADS
    return dict(
        cwx=cw[:, :A_DIM], cwb=cw[:, A_DIM:A_DIM + A_GN], cwc=cw[:, A_DIM + A_GN:],
        cbx=cb[:, :A_DIM], cbb=cb[:, A_DIM:A_DIM + A_GN], cbc=cb[:, A_DIM + A_GN:],
        dtb=jnp.pad(p["dt_bias"], (0, pad)).reshape(1, LANES),
        aneg=jnp.pad(-jnp.exp(p["a_log"]), (0, pad)).reshape(1, LANES),
        dsk=jnp.repeat(p["d_skip"], A_HEAD).reshape(1, A_DIM),
        ng=p["ssm_norm_g"].reshape(1, A_DIM),
    )


def _ssd_prompt(proj, sp):
    bsz, seq, _ = proj.shape
    ch = SSD_CHUNK
    gn = A_GN

    def col(width, off):
        return pl.BlockSpec((None, ch, width), lambda b, c: (b, c, off // width))

    names = ("cwx", "cwb", "cwc", "cbx", "cbb", "cbc", "dtb", "aneg", "dsk", "ng")
    y, st = pl.pallas_call(
        _ssd_prompt_kernel,
        grid=(bsz, seq // ch),
        in_specs=[col(A_DIM, P_Z), col(A_DIM, P_X), col(gn, P_B), col(gn, P_C), col(LANES, P_DT)]
                 + [_full(sp[n].shape) for n in names],
        out_specs=[pl.BlockSpec((None, ch, A_DIM), lambda b, c: (b, c, 0)),
                   pl.BlockSpec((None, A_HEADS, A_HEAD, A_STATE), lambda b, c: (b, 0, 0, 0))],
        out_shape=[jax.ShapeDtypeStruct((bsz, seq, A_DIM), BF16),
                   jax.ShapeDtypeStruct((bsz, A_HEADS, A_HEAD, A_STATE), F32)],
        scratch_shapes=[pltpu.VMEM((A_HEADS, A_HEAD, A_STATE), F32),
                        pltpu.VMEM((ch + 8, A_DIM), F32),
                        pltpu.VMEM((ch + 8, gn), F32),
                        pltpu.VMEM((ch + 8, gn), F32),
                        pltpu.VMEM((ch, A_DIM), F32)],
        compiler_params=_cparams(("parallel", "arbitrary")),
        name="ssd_prompt",
    )(proj, proj, proj, proj, proj, *[sp[n] for n in names])
    return y, st


def _ssd_step_kernel(z_ref, x_ref, b_ref, c_ref, dt_ref, cs_ref, cw, cb, dtb, anegx, expand, dsk, ng,
                     h_ref, y_ref, ho_ref, y_scr):
    bb = STEP_BB
    xbc = (x_ref[...], b_ref[...], c_ref[...])
    offs = (0, A_DIM, A_DIM + A_GN, CONV_CH)
    u = []
    for i in range(3):
        sl = slice(offs[i], offs[i + 1])
        acc = cb[:, sl] + xbc[i] * cw[CONV_W - 1:CONV_W, sl]
        for tap in range(CONV_W - 1):
            acc = acc + cs_ref[tap][:, sl] * cw[tap:tap + 1, sl]
        u.append(_silu(acc))
    ux, ub, uc = u
    dt = _softplus(dt_ref[...] + dtb[...])
    dtx = _dot_hi(dt, expand[...])
    dec = jnp.exp(dtx * anegx[...])
    xdt = ux * dtx
    rows8 = _iota((bb, LANES), 0)
    gw = A_DIM // A_GROUPS
    for b in range(bb):
        sel = (rows8 == b).astype(F32)
        hn = h_ref[b] * _dot_tn_hi(dec, sel)
        for g in range(A_GROUPS):
            rs = slice(g * gw, (g + 1) * gw)
            ns = slice(g * A_STATE, (g + 1) * A_STATE)
            hg = hn[rs, :] + _dot_tn(xdt[:, rs], sel * ub[b:b + 1, ns])
            ho_ref[b, rs, :] = hg
            y_scr[b:b + 1, rs] = _dot_nt(uc[:, ns], hg)[b:b + 1, :]
    y = y_scr[...] + dsk[...] * ux
    y_ref[...] = _ssd_gate_norm(y, z_ref[...], ng[...]).astype(BF16)


def _ssd_step(proj, conv_state, h0, p, sp):
    bsz = proj.shape[0]
    bb = STEP_BB
    cs = jnp.swapaxes(conv_state, 0, 1)
    hflat = h0.reshape(bsz, A_DIM, A_STATE)
    expand = (jnp.arange(LANES)[:, None] == (jnp.arange(A_DIM) // A_HEAD)[None, :]).astype(F32)
    anegx = jnp.repeat(-jnp.exp(p["a_log"]), A_HEAD).reshape(1, A_DIM)
    cw, cb = p["conv_w"], p["conv_b"].reshape(1, CONV_CH)

    def col(width, off):
        return pl.BlockSpec((bb, width), lambda i: (i, off // width))

    y, hn = pl.pallas_call(
        _ssd_step_kernel,
        grid=(bsz // bb,),
        in_specs=[col(A_DIM, P_Z), col(A_DIM, P_X), col(A_GN, P_B), col(A_GN, P_C), col(LANES, P_DT),
                  pl.BlockSpec((CONV_W - 1, bb, CONV_CH), lambda i: (0, i, 0)),
                  _full(cw.shape), _full(cb.shape), _full(sp["dtb"].shape), _full(anegx.shape),
                  _full(expand.shape), _full(sp["dsk"].shape), _full(sp["ng"].shape),
                  pl.BlockSpec((bb, A_DIM, A_STATE), lambda i: (i, 0, 0))],
        out_specs=[pl.BlockSpec((bb, A_DIM), lambda i: (i, 0)),
                   pl.BlockSpec((bb, A_DIM, A_STATE), lambda i: (i, 0, 0))],
        out_shape=[jax.ShapeDtypeStruct((bsz, A_DIM), BF16),
                   jax.ShapeDtypeStruct((bsz, A_DIM, A_STATE), F32)],
        scratch_shapes=[pltpu.VMEM((bb, A_DIM), F32)],
        compiler_params=_cparams(("parallel",)),
        name="ssd_step",
    )(proj, proj, proj, proj, proj, cs, cw, cb, sp["dtb"], anegx, expand, sp["dsk"], sp["ng"], hflat)
    return y, hn.reshape(h0.shape)


def _hgrn_gates(q, f, lbp):
    qf = _silu(q)
    kf = lbp[0:1, :] * jax.nn.sigmoid(-f)
    log_sig = jnp.minimum(f, 0.0) - jnp.log1p(jnp.exp(-jnp.abs(f)))
    a = lbp[1:2, :]
    b = lbp[2:3, :] + log_sig
    logf = jnp.maximum(a, b) + jnp.log1p(jnp.exp(-jnp.abs(a - b)))
    return qf, kf, logf


def _hgrn_out(o, g, ng):
    outs = []
    for h in range(B_HEADS):
        oh = o[:, h * B_HEAD:(h + 1) * B_HEAD]
        outs.append(oh * lax.rsqrt(jnp.mean(oh * oh, axis=-1, keepdims=True) + NORM_EPS) * ng)
    return jnp.concatenate(outs, axis=-1) * _silu(g)


def _hgrn_prompt_kernel(q_ref, f_ref, i_ref, g_ref, lbp, ng, o_ref, st_ref,
                        s_scr, cum_scr, q_scr, k_scr, o_scr):
    c = pl.program_id(1)
    tb, cs = HGRN_BLOCK, HGRN_CHUNK

    @pl.when(c == 0)
    def _():
        s_scr[...] = jnp.zeros_like(s_scr)

    qf, kf, logf = _hgrn_gates(q_ref[...], f_ref[...], lbp[...])
    r_i, c_i = _iota((tb, tb), 0), _iota((tb, tb), 1)
    bd = ((r_i // cs == c_i // cs) & (c_i <= r_i)).astype(F32)
    cum_scr[...] = _dot_hi(bd, logf)
    q_scr[...] = qf
    k_scr[...] = kf
    rows = _iota((cs, B_HEAD), 0)

    def chunk(cc, carry):
        r0 = pl.multiple_of(cc * cs, cs)
        for h in range(B_HEADS):
            hs = slice(h * B_HEAD, (h + 1) * B_HEAD)
            cu = cum_scr[pl.ds(r0, cs), hs]
            q = q_scr[pl.ds(r0, cs), hs]
            k = k_scr[pl.ds(r0, cs), hs]
            v = i_ref[pl.ds(r0, cs), hs]
            st = s_scr[h]
            o = _dot_nt(q * jnp.exp(cu), st)
            for s in range(cs):
                d = jnp.exp(jnp.where(rows >= s, cu - cu[s:s + 1, :], -jnp.inf))
                att = jnp.sum(q * k[s:s + 1, :] * d, axis=-1, keepdims=True)
                o = o + att * v[s:s + 1, :]
            o_scr[pl.ds(r0, cs), hs] = o
            cl = cu[cs - 1:cs, :]
            s_scr[h] = st * jnp.exp(cl) + _dot_tn(v, k * jnp.exp(cl - cu))
        return carry

    lax.fori_loop(0, tb // cs, chunk, 0)
    o_ref[...] = _hgrn_out(o_scr[...], g_ref[...], ng[...]).astype(BF16)

    @pl.when(c == pl.num_programs(1) - 1)
    def _():
        for h in range(B_HEADS):
            st_ref[h] = s_scr[h].T


def _hgrn_prompt(proj, lbp, ng):
    bsz, seq, _ = proj.shape
    tb = HGRN_BLOCK

    def col(off):
        return pl.BlockSpec((None, tb, B_DIM), lambda b, c: (b, c, off // B_DIM))

    o, st = pl.pallas_call(
        _hgrn_prompt_kernel,
        grid=(bsz, seq // tb),
        in_specs=[col(P_Q), col(P_F), col(P_I), col(P_G), _full(lbp.shape), _full(ng.shape)],
        out_specs=[pl.BlockSpec((None, tb, B_DIM), lambda b, c: (b, c, 0)),
                   pl.BlockSpec((None, B_HEADS, B_HEAD, B_HEAD), lambda b, c: (b, 0, 0, 0))],
        out_shape=[jax.ShapeDtypeStruct((bsz, seq, B_DIM), BF16),
                   jax.ShapeDtypeStruct((bsz, B_HEADS, B_HEAD, B_HEAD), F32)],
        scratch_shapes=[pltpu.VMEM((B_HEADS, B_HEAD, B_HEAD), F32)]
                       + [pltpu.VMEM((tb, B_DIM), F32)] * 4,
        compiler_params=_cparams(("parallel", "arbitrary")),
        name="hgrn_prompt",
    )(proj, proj, proj, proj, lbp, ng)
    return o, st


def _hgrn_step_kernel(q_ref, f_ref, i_ref, g_ref, lbp, ng, s_ref, o_ref, so_ref, o_scr):
    bb = STEP_BB
    qf, kf, logf = _hgrn_gates(q_ref[...], f_ref[...], lbp[...])
    vf = i_ref[...]
    dec = jnp.exp(logf)
    rows8 = _iota((bb, LANES), 0)
    for b in range(bb):
        sel = (rows8 == b).astype(F32)
        sn = s_ref[b] * _dot_tn_hi(dec, sel)
        for h in range(B_HEADS):
            hs = slice(h * B_HEAD, (h + 1) * B_HEAD)
            sh = sn[hs, :] + _dot_tn(kf[:, hs], sel * vf[b:b + 1, hs])
            so_ref[b, hs, :] = sh
            o_scr[b:b + 1, hs] = _dot(qf[:, hs], sh)[b:b + 1, :]
    o_ref[...] = _hgrn_out(o_scr[...], g_ref[...], ng[...]).astype(BF16)


def _hgrn_step(proj, s0, lbp, ng):
    bsz = proj.shape[0]
    bb = STEP_BB
    sflat = s0.reshape(bsz, B_DIM, B_HEAD)

    def col(off):
        return pl.BlockSpec((bb, B_DIM), lambda i: (i, off // B_DIM))

    o, sn = pl.pallas_call(
        _hgrn_step_kernel,
        grid=(bsz // bb,),
        in_specs=[col(P_Q), col(P_F), col(P_I), col(P_G), _full(lbp.shape), _full(ng.shape),
                  pl.BlockSpec((bb, B_DIM, B_HEAD), lambda i: (i, 0, 0))],
        out_specs=[pl.BlockSpec((bb, B_DIM), lambda i: (i, 0)),
                   pl.BlockSpec((bb, B_DIM, B_HEAD), lambda i: (i, 0, 0))],
        out_shape=[jax.ShapeDtypeStruct((bsz, B_DIM), BF16),
                   jax.ShapeDtypeStruct((bsz, B_DIM, B_HEAD), F32)],
        scratch_shapes=[pltpu.VMEM((bb, B_DIM), F32)],
        compiler_params=_cparams(("parallel",)),
        name="hgrn_step",
    )(proj, proj, proj, proj, lbp, ng, sflat)
    return o, sn.reshape(s0.shape)


RW_NAMES = ("mu_r", "mu_k", "mu_v", "mu_lr", "w0", "w2", "a0", "a2", "g2", "k_k", "k_a", "r_k")


def _rwkv_params(p):
    mu = p["shift_mu"].reshape(1, C_COLS)
    row = lambda a: a.reshape(1, C_DIM)
    return dict(
        mu_r=mu[:, :C_DIM], mu_k=mu[:, C_DIM:2 * C_DIM], mu_v=mu[:, 2 * C_DIM:3 * C_DIM],
        mu_lr=mu[:, 3 * C_DIM:],
        w0=row(p["w0"]), w2=p["w2"].astype(BF16), a0=row(p["a0"]), a2=p["a2"].astype(BF16),
        g2=p["g2"].astype(BF16), k_k=row(p["k_k"]), k_a=row(p["k_a"]), r_k=row(p["r_k"]),
        lnx_w=row(p["lnx_w"]), lnx_b=row(p["lnx_b"]),
    )


def _head_sum(x):
    return [jnp.sum(x[:, h * C_HEAD:(h + 1) * C_HEAD], axis=-1, keepdims=True) for h in range(C_HEADS)]


def _head_bcast(cols, fn=lambda c: c):
    rows = cols[0].shape[0]
    return jnp.concatenate([jnp.broadcast_to(fn(c), (rows, C_HEAD)) for c in cols], axis=-1)


def _rwkv_prep(r, k, v, lr, pr, pk, pv, plr, prm):
    mu_r, mu_k, mu_v, mu_lr, w0, w2, a0, a2, g2, k_k, k_a, r_k = prm
    r = r + (pr - r) * mu_r
    k = k + (pk - k) * mu_k
    v = v + (pv - v) * mu_v
    lr = lr + (plr - lr) * mu_lr
    wl, al, gl = lr[:, :C_LR_W], lr[:, C_LR_W:C_LR_W + C_LR_A], lr[:, C_LR_W + C_LR_A:]
    wraw = -_softplus(-(w0 + _dot(jnp.tanh(wl), w2))) - 0.5
    ld = -jnp.exp(wraw)
    a = jax.nn.sigmoid(a0 + _dot(al, a2))
    g = _dot(jax.nn.sigmoid(gl), g2)
    kk = k * k_k
    kk = kk * _head_bcast(_head_sum(kk * kk), lambda n2: 1.0 / jnp.maximum(jnp.sqrt(n2), L2_EPS))
    k2 = k * (1.0 + (a - 1.0) * k_a)
    bonus = _head_bcast(_head_sum(r * k2 * r_k)) * v
    return r, k2, v, kk, kk * a, ld, g, bonus


def _rwkv_out(y, bonus, g, lnx_w, lnx_b):
    n = float(C_HEAD)
    mean = _head_bcast(_head_sum(y), lambda s: s / n)
    yc = y - mean
    rstd = _head_bcast(_head_sum(yc * yc), lambda s: lax.rsqrt(s / n + GN_EPS))
    return (yc * rstd * lnx_w + lnx_b + bonus) * g


def _rwkv_a_kernel(r_ref, k_ref, v_ref, lr_ref, rp_ref, kp_ref, vp_ref, lrp_ref, *rest):
    prm = tuple(x[...] for x in rest[:len(RW_NAMES)])
    (kd_ref, rd_ref, bg_ref, m2_ref, rkv_ref, rb_ref, aab_ref, vk_ref, ge_ref, bonus_ref, gate_ref,
     xs_scr, b1_scr, b2_scr, b3_scr, b4_scr) = rest[len(RW_NAMES):]
    c = pl.program_id(1)
    tb, cs = RWKV_BLOCK, RWKV_CHUNK
    first = (c == 0)

    def prev(cur_ref, prev_ref, width):
        xs_scr[8:8 + tb, 0:width] = cur_ref[...]
        xs_scr[7:8, 0:width] = jnp.where(first, 0.0, prev_ref[7:8, :])
        return xs_scr[pl.ds(7, tb), 0:width]

    pr = prev(r_ref, rp_ref, C_DIM)
    pk = prev(k_ref, kp_ref, C_DIM)
    pv = prev(v_ref, vp_ref, C_DIM)
    plr = prev(lr_ref, lrp_ref, C_LR)
    r, k2, v, kk, beta, ld, g, bonus = _rwkv_prep(
        r_ref[...], k_ref[...], v_ref[...], lr_ref[...], pr, pk, pv, plr, prm)
    bonus_ref[...] = bonus
    gate_ref[...] = g

    r_i, c_i = _iota((tb, tb), 0), _iota((tb, tb), 1)
    same = (r_i // cs == c_i // cs)
    lcum = _dot_hi((same & (c_i <= r_i)).astype(F32), ld)
    lend = _dot_hi(same.astype(F32), ld)
    einv = jnp.exp(-lcum)
    eend = jnp.exp(lend - lcum)
    ge_ref[...] = jnp.exp(lend)
    kd_ref[...] = kk * jnp.exp(lcum - ld)
    rd_ref[...] = r * jnp.exp(lcum)
    bg_ref[...] = beta * eend
    b1_scr[...] = beta * einv
    b2_scr[...] = k2 * einv
    b3_scr[...] = k2 * eend
    b4_scr[...] = v
    rr, cc = _iota((cs, cs), 0), _iota((cs, cs), 1)
    strict, incl = rr > cc, rr >= cc

    def chunk(j, carry):
        r0 = pl.multiple_of(j * cs, cs)
        for h in range(C_HEADS):
            hs = slice(h * C_HEAD, (h + 1) * C_HEAD)
            kd = kd_ref[pl.ds(r0, cs), hs].astype(BF16)
            rd = rd_ref[pl.ds(r0, cs), hs].astype(BF16)
            bi = b1_scr[pl.ds(r0, cs), hs].astype(BF16)
            ki = b2_scr[pl.ds(r0, cs), hs].astype(BF16)
            kg = b3_scr[pl.ds(r0, cs), hs]
            vh = b4_scr[pl.ds(r0, cs), hs].astype(BF16)
            aab_ref[pl.ds(r0, cs), hs] = jnp.where(strict, _dot_nt(kd, bi), 0.0)
            aak = jnp.where(strict, _dot_nt(kd, ki), 0.0)
            rb_ref[pl.ds(r0, cs), hs] = jnp.where(incl, _dot_nt(rd, bi), 0.0)
            rk = jnp.where(incl, _dot_nt(rd, ki), 0.0)
            m2_ref[pl.ds(r0, cs), hs] = _dot(aak, vh)
            rkv_ref[pl.ds(r0, cs), hs] = _dot(rk, vh)
            vk_ref[pl.ds(r0, cs), hs] = _dot_tn(vh, kg)
        return carry

    lax.fori_loop(0, tb // cs, chunk, 0)


def _tri_solve_kernel(a_ref, t_ref):
    n, _, npb = a_ref.shape
    rowj = _iota((n, npb), 0)

    def row(t, carry):
        def inner(s, acc):
            return acc - a_ref[t, pl.ds(s, 1), :] * t_ref[s]

        t_ref[t] = lax.fori_loop(0, t, inner, (rowj == t).astype(F32))
        return carry

    lax.fori_loop(0, n, row, 0)


def _tri_solve(a):
    n, _, nprob = a.shape
    npb = min(256, nprob)
    return pl.pallas_call(
        _tri_solve_kernel,
        grid=(nprob // npb,),
        in_specs=[pl.BlockSpec((n, n, npb), lambda i: (0, 0, i))],
        out_specs=pl.BlockSpec((n, n, npb), lambda i: (0, 0, i)),
        out_shape=jax.ShapeDtypeStruct(a.shape, F32),
        compiler_params=_cparams(("parallel",)),
        name="tri_solve",
    )(a)


def _rwkv_c_kernel(t_ref, kd_ref, rd_ref, bg_ref, m2_ref, rkv_ref, rb_ref, vk_ref, ge_ref,
                   bonus_ref, gate_ref, lnw, lnb, y_ref, st_ref, s_scr, y_scr):
    c = pl.program_id(1)
    tb, cs = RWKV_BLOCK, RWKV_CHUNK

    @pl.when(c == 0)
    def _():
        s_scr[...] = jnp.zeros_like(s_scr)

    def chunk(j, carry):
        r0 = pl.multiple_of(j * cs, cs)
        for h in range(C_HEADS):
            hs = slice(h * C_HEAD, (h + 1) * C_HEAD)
            tm = t_ref[pl.ds(r0, cs), hs].astype(BF16)
            w = _dot(tm, kd_ref[pl.ds(r0, cs), hs])
            u = _dot(tm, m2_ref[pl.ds(r0, cs), hs])
            s = s_scr[h]
            p = -_dot_nt(w, s) - u
            y = _dot_nt(rd_ref[pl.ds(r0, cs), hs], s) + _dot(rb_ref[pl.ds(r0, cs), hs], p)
            y_scr[pl.ds(r0, cs), hs] = y + rkv_ref[pl.ds(r0, cs), hs]
            s_scr[h] = (s * ge_ref[pl.ds(r0, 1), hs] + _dot_tn(p, bg_ref[pl.ds(r0, cs), hs])
                        + vk_ref[pl.ds(r0, cs), hs])
        return carry

    lax.fori_loop(0, tb // cs, chunk, 0)
    y_ref[...] = _rwkv_out(y_scr[...], bonus_ref[...], gate_ref[...], lnw[...], lnb[...]).astype(BF16)

    @pl.when(c == pl.num_programs(1) - 1)
    def _():
        st_ref[...] = s_scr[...]


def _rwkv_prompt(proj, rp):
    bsz, seq, _ = proj.shape
    tb, cs = RWKV_BLOCK, RWKV_CHUNK
    nblk = seq // tb

    def col(width, off):
        return pl.BlockSpec((None, tb, width), lambda b, c: (b, c, off // width))

    def pcol(width, off):
        return pl.BlockSpec((None, 8, width), lambda b, c: (b, jnp.maximum(c * (tb // 8) - 1, 0), off // width))

    tok = pl.BlockSpec((None, tb, C_DIM), lambda b, c: (b, c, 0))
    tok_shape = jax.ShapeDtypeStruct((bsz, seq, C_DIM), F32)
    prm = [rp[n] for n in RW_NAMES]
    outs = pl.pallas_call(
        _rwkv_a_kernel,
        grid=(bsz, nblk),
        in_specs=[col(C_DIM, P_R), col(C_DIM, P_K), col(C_DIM, P_V), col(C_LR, P_LR),
                  pcol(C_DIM, P_R), pcol(C_DIM, P_K), pcol(C_DIM, P_V), pcol(C_LR, P_LR)]
                 + [_full(x.shape) for x in prm],
        out_specs=[tok] * 11,
        out_shape=[tok_shape] * 11,
        scratch_shapes=[pltpu.VMEM((tb + 8, C_DIM), F32)] + [pltpu.VMEM((tb, C_DIM), F32)] * 4,
        compiler_params=_cparams(("parallel", "parallel")),
        name="rwkv_prep",
    )(proj, proj, proj, proj, proj, proj, proj, proj, *prm)
    kd, rd, bg, m2, rkv, rb, aab, vk, ge, bonus, gate = outs

    nch = seq // cs
    a_t = aab.reshape(bsz, nch, cs, C_HEADS, cs).transpose(2, 4, 0, 1, 3).reshape(cs, cs, -1)
    t_t = _tri_solve(a_t)
    tmat = t_t.reshape(cs, cs, bsz, nch, C_HEADS).transpose(2, 3, 0, 4, 1).reshape(bsz, seq, C_DIM)

    y, st = pl.pallas_call(
        _rwkv_c_kernel,
        grid=(bsz, nblk),
        in_specs=[tok] * 11 + [_full((1, C_DIM))] * 2,
        out_specs=[tok, pl.BlockSpec((None, C_HEADS, C_HEAD, C_HEAD), lambda b, c: (b, 0, 0, 0))],
        out_shape=[jax.ShapeDtypeStruct((bsz, seq, C_DIM), BF16),
                   jax.ShapeDtypeStruct((bsz, C_HEADS, C_HEAD, C_HEAD), F32)],
        scratch_shapes=[pltpu.VMEM((C_HEADS, C_HEAD, C_HEAD), F32), pltpu.VMEM((tb, C_DIM), F32)],
        compiler_params=_cparams(("parallel", "arbitrary")),
        name="rwkv_scan",
    )(tmat, kd, rd, bg, m2, rkv, rb, vk, ge, bonus, gate, rp["lnx_w"], rp["lnx_b"])
    return y, st


def _rwkv_step_kernel(r_ref, k_ref, v_ref, lr_ref, sh_ref, *rest):
    prm = tuple(x[...] for x in rest[:len(RW_NAMES)])
    lnw, lnb, s_ref, y_ref, so_ref, y_scr = rest[len(RW_NAMES):]
    bb = STEP_BB
    sh = sh_ref[...]
    r, k2, v, kk, beta, ld, g, bonus = _rwkv_prep(
        r_ref[...], k_ref[...], v_ref[...], lr_ref[...],
        sh[:, :C_DIM], sh[:, C_DIM:2 * C_DIM], sh[:, 2 * C_DIM:3 * C_DIM], sh[:, 3 * C_DIM:], prm)
    w = jnp.exp(ld)
    rows8 = _iota((bb, C_HEAD), 0)
    for b in range(bb):
        for h in range(C_HEADS):
            hs = slice(h * C_HEAD, (h + 1) * C_HEAD)
            s = s_ref[b, hs, :]
            sa = -jnp.sum(s * kk[b:b + 1, hs], axis=-1, keepdims=True)
            vk = _dot_tn_hi(v[:, hs], jnp.where(rows8 == b, k2[:, hs], 0.0))
            sn = s * w[b:b + 1, hs] + sa * beta[b:b + 1, hs] + vk
            so_ref[b, hs, :] = sn
            y_scr[b:b + 1, hs] = _dot_nt(r[:, hs], sn)[b:b + 1, :]
    y_ref[...] = _rwkv_out(y_scr[...], bonus, g, lnw[...], lnb[...]).astype(BF16)


def _rwkv_step(proj, shift, s0, rp):
    bsz = proj.shape[0]
    bb = STEP_BB
    sflat = s0.reshape(bsz, C_DIM, C_HEAD)

    def col(width, off):
        return pl.BlockSpec((bb, width), lambda i: (i, off // width))

    prm = [rp[n] for n in RW_NAMES]
    y, sn = pl.pallas_call(
        _rwkv_step_kernel,
        grid=(bsz // bb,),
        in_specs=[col(C_DIM, P_R), col(C_DIM, P_K), col(C_DIM, P_V), col(C_LR, P_LR),
                  pl.BlockSpec((bb, C_COLS), lambda i: (i, 0))]
                 + [_full(x.shape) for x in prm] + [_full((1, C_DIM))] * 2
                 + [pl.BlockSpec((bb, C_DIM, C_HEAD), lambda i: (i, 0, 0))],
        out_specs=[pl.BlockSpec((bb, C_DIM), lambda i: (i, 0)),
                   pl.BlockSpec((bb, C_DIM, C_HEAD), lambda i: (i, 0, 0))],
        out_shape=[jax.ShapeDtypeStruct((bsz, C_DIM), BF16),
                   jax.ShapeDtypeStruct((bsz, C_DIM, C_HEAD), F32)],
        scratch_shapes=[pltpu.VMEM((bb, C_DIM), F32)],
        compiler_params=_cparams(("parallel",)),
        name="rwkv_step",
    )(proj, proj, proj, proj, shift, *prm, rp["lnx_w"], rp["lnx_b"], sflat)
    return y, sn.reshape(s0.shape)


def _layout_w_in(w):
    z, x = w[:, :A_DIM], w[:, A_DIM:2 * A_DIM]
    bm, cm = w[:, 2 * A_DIM:2 * A_DIM + A_GN], w[:, 2 * A_DIM + A_GN:A_DIM + CONV_CH]
    o = A_DIM + CONV_CH
    dt = w[:, o:o + A_HEADS]
    qfig = w[:, o + A_HEADS:o + A_HEADS + 4 * B_DIM]
    rw = w[:, o + A_HEADS + 4 * B_DIM:]
    pad = jnp.zeros((w.shape[0], DT_PAD), w.dtype)
    return jnp.concatenate([z, qfig, rw, dt, pad, x, bm, cm], axis=1).astype(BF16)


def _xbc_cols(proj):
    return jnp.concatenate([proj[..., P_X:P_X + A_DIM], proj[..., P_B:P_B + A_GN],
                            proj[..., P_C:P_C + A_GN]], axis=-1)


def _moe_route(logits, n_tok):
    top_logit, top_idx = lax.top_k(logits, TOP_K)
    gate = jax.nn.softmax(top_logit, axis=-1)
    n_assign = n_tok * TOP_K
    flat_e = top_idx.reshape(-1).astype(jnp.int32)
    order = jnp.argsort(flat_e)
    se = flat_e[order]
    counts = jnp.sum(jax.nn.one_hot(flat_e, N_EXPERTS, dtype=jnp.int32), axis=0)
    padded = (counts + MOE_BM - 1) // MOE_BM * MOE_BM
    pad_end = jnp.cumsum(padded)
    pad_start = pad_end - padded
    start = jnp.cumsum(counts) - counts
    slot_sorted = pad_start[se] + jnp.arange(n_assign, dtype=jnp.int32) - start[se]
    n_blocks = -(-n_assign // MOE_BM) + N_EXPERTS
    n_slots = n_blocks * MOE_BM
    slot_tok = jnp.full((n_slots,), n_tok, jnp.int32).at[slot_sorted].set(order // TOP_K)
    slot_of = jnp.zeros((n_assign,), jnp.int32).at[order].set(slot_sorted).reshape(n_tok, TOP_K)
    block_start = jnp.arange(n_blocks, dtype=jnp.int32) * MOE_BM
    nvalid = (pad_end[-1] // MOE_BM).astype(jnp.int32)
    bexp = jnp.minimum(jnp.searchsorted(pad_end, block_start, side="right"), N_EXPERTS - 1).astype(jnp.int32)
    last = bexp[jnp.maximum(nvalid - 1, 0)]
    bexp = jnp.where(jnp.arange(n_blocks) < nvalid, bexp, last)
    first = jnp.concatenate([jnp.ones((1,), jnp.int32), (bexp[1:] != bexp[:-1]).astype(jnp.int32)])
    return gate, slot_tok, slot_of, bexp, nvalid.reshape(1), first


def _moe(h, logits, wg, wu, wd):
    n_tok = h.shape[0]
    gate, slot_tok, slot_of, bexp, nvalid, first = _moe_route(logits[:, :N_EXPERTS], n_tok)
    h_pad = jnp.concatenate([h, jnp.zeros((1, h.shape[1]), h.dtype)], axis=0)
    xb = h_pad[slot_tok]
    yb = _moe_experts(xb, bexp, nvalid, first, wg, wu, wd)
    return gate[:, 0:1] * yb[slot_of[:, 0]] + gate[:, 1:2] * yb[slot_of[:, 1]]


def kernel(x_prompt, x_sample, state_ssm, state_conv, state_hgrn, state_rwkv, state_shift, norm1_g, w_in, conv_w, conv_b, dt_bias, a_log, d_skip, ssm_norm_g, lb_logits, hgrn_norm_g, shift_mu, w0, w2, a0, a2, g2, k_k, k_a, r_k, lnx_w, lnx_b, w_out, norm2_g, ffn_w_gate, ffn_w_up, ffn_w_down, router_w, exp_w_gate, exp_w_up, exp_w_down, final_norm_g):
    depth = w_in.shape[0]
    bp, seq, _ = x_prompt.shape
    bs = x_sample.shape[0]
    tp = bp * seq
    lb_soft = jax.nn.softmax(lb_logits.astype(F32), axis=0)
    lower_bounds = jnp.clip(jnp.cumsum(lb_soft, axis=0) - lb_soft[0:1], 0.0, 1.0)

    xp = x_prompt.reshape(tp, D_MODEL)
    xs = x_sample.reshape(bs, D_MODEL)
    outs = {k: [] for k in ("p_ssm", "p_conv", "p_hgrn", "p_rwkv", "p_shift",
                            "s_ssm", "s_conv", "s_hgrn", "s_rwkv", "s_shift")}
    y_moe_p = y_moe_s = None
    for layer in range(depth):
        p = dict(conv_w=conv_w[layer], conv_b=conv_b[layer], dt_bias=dt_bias[layer], a_log=a_log[layer],
                 d_skip=d_skip[layer], ssm_norm_g=ssm_norm_g[layer], shift_mu=shift_mu[layer],
                 w0=w0[layer], w2=w2[layer], a0=a0[layer], a2=a2[layer], g2=g2[layer], k_k=k_k[layer],
                 k_a=k_a[layer], r_k=r_k[layer].reshape(-1), lnx_w=lnx_w[layer], lnx_b=lnx_b[layer])
        sp = _ssd_params(p)
        rp = _rwkv_params(p)
        lb = lower_bounds[layer]
        lbp = jnp.stack([1.0 - lb, jnp.log(lb), jnp.log1p(-lb)])
        hng = hgrn_norm_g[layer].reshape(1, B_HEAD)
        w_in_l = _layout_w_in(w_in[layer])
        w_out_l = w_out[layer].astype(BF16)
        j = layer // 2
        routed = layer % 2 == 1
        rw = jnp.pad(router_w[j], ((0, 0), (0, LANES - N_EXPERTS))) if routed else None

        proj = _norm_matmul(xp, norm1_g[layer], w_in_l, 1024, 512).reshape(bp, seq, P_COLS)
        ya, ssm_p = _ssd_prompt(proj, sp)
        yb, hgrn_p = _hgrn_prompt(proj, lbp, hng)
        yc, rwkv_p = _rwkv_prompt(proj, rp)
        mix = jnp.concatenate([ya, yb, yc], axis=-1).reshape(tp, D_MODEL)
        xp, hp, lp = _out_proj(mix, w_out_l, xp, norm2_g[layer], rw, 256)
        outs["p_ssm"].append(ssm_p)
        outs["p_conv"].append(_xbc_cols(proj[:, seq - (CONV_W - 1):, :]))
        outs["p_hgrn"].append(hgrn_p)
        outs["p_rwkv"].append(rwkv_p)
        outs["p_shift"].append(proj[:, seq - 1, P_R:P_R + C_COLS])

        projs = _norm_matmul(xs, norm1_g[layer], w_in_l, 128, 512)
        ya, ssm_s = _ssd_step(projs, state_conv[layer], state_ssm[layer], p, sp)
        yb, hgrn_s = _hgrn_step(projs, state_hgrn[layer], lbp, hng)
        yc, rwkv_s = _rwkv_step(projs, state_shift[layer], state_rwkv[layer], rp)
        mix = jnp.concatenate([ya, yb, yc], axis=-1)
        xs, hs, ls = _out_proj(mix, w_out_l, xs, norm2_g[layer], rw, 128)
        outs["s_ssm"].append(ssm_s)
        outs["s_conv"].append(jnp.concatenate([state_conv[layer][:, 1:, :], _xbc_cols(projs)[:, None, :]], axis=1))
        outs["s_hgrn"].append(hgrn_s)
        outs["s_rwkv"].append(rwkv_s)
        outs["s_shift"].append(projs[:, P_R:P_R + C_COLS])

        if not routed:
            xp = _ffn(xp, hp, ffn_w_gate[j], ffn_w_up[j], ffn_w_down[j], 512)
            xs = _ffn(xs, hs, ffn_w_gate[j], ffn_w_up[j], ffn_w_down[j], 128)
        else:
            y_moe = _moe(jnp.concatenate([hp, hs], axis=0), jnp.concatenate([lp, ls], axis=0),
                         exp_w_gate[j], exp_w_up[j], exp_w_down[j])
            if layer == depth - 1:
                y_moe_p, y_moe_s = y_moe[:tp], y_moe[tp:]
            else:
                xp, xs = xp + y_moe[:tp], xs + y_moe[tp:]

    zp = y_moe_p if y_moe_p is not None else jnp.zeros_like(xp)
    zs = y_moe_s if y_moe_s is not None else jnp.zeros_like(xs)
    y_prompt = _final_norm(xp, zp, final_norm_g, 512).reshape(bp, seq, D_MODEL)
    y_sample = _final_norm(xs, zs, final_norm_g, 128).reshape(bs, 1, D_MODEL)
    st = {k: jnp.stack(v) for k, v in outs.items()}
    return (y_prompt, y_sample, st["p_ssm"], st["p_conv"], st["p_hgrn"], st["p_rwkv"], st["p_shift"],
            st["s_ssm"], st["s_conv"], st["s_hgrn"], st["s_rwkv"], st["s_shift"])
```

```python
import functools
import math

import jax
import jax.numpy as jnp
from jax import lax
from jax.experimental import pallas as pl
from jax.experimental.pallas import tpu as pltpu

F32 = jnp.float32
BF16 = jnp.bfloat16
HI = lax.Precision.HIGHEST

D_MODEL = 2048
A_DIM, A_HEAD, A_HEADS, A_GROUPS, A_STATE = 1024, 64, 16, 2, 128
A_GN = A_GROUPS * A_STATE
CONV_W = 4
CONV_CH = A_DIM + 2 * A_GN
B_DIM, B_HEAD, B_HEADS = 512, 128, 4
C_DIM, C_HEAD, C_HEADS = 512, 64, 8
C_LR_W, C_LR_A, C_LR_G = 64, 64, 128
C_LR = C_LR_W + C_LR_A + C_LR_G
C_COLS = 3 * C_DIM + C_LR
IN_COLS = A_DIM + CONV_CH + A_HEADS + 4 * B_DIM + C_COLS
D_FF, N_EXPERTS, TOP_K, E_FF = 5504, 8, 2, 7168
NORM_EPS, GN_EPS, L2_EPS = 1e-6, 64e-5, 1e-12

P_Z, P_Q, P_F, P_I, P_G = 0, 1024, 1536, 2048, 2560
P_R, P_K, P_V, P_LR, P_DT = 3072, 3584, 4096, 4608, 4864
P_X, P_B, P_C, P_COLS = 5120, 6144, 6400, 6656
LANES = 128
DT_PAD = P_X - P_DT - A_HEADS

SSD_CHUNK = 256
HGRN_BLOCK, HGRN_CHUNK = 256, 16
RWKV_BLOCK, RWKV_CHUNK = 256, 64
STEP_BB = 8

FFN_TF, FFN_TN = 512, 512
MOE_BM, MOE_TF, MOE_TN = 256, 1024, 512

NT_DIMS = (((1,), (1,)), ((), ()))
TN_DIMS = (((0,), (0,)), ((), ()))


def _cparams(sem, vmem_mb=48):
    return pltpu.CompilerParams(dimension_semantics=sem, vmem_limit_bytes=vmem_mb * 1024 * 1024)


def _dot(a, b):
    return jnp.dot(a.astype(BF16), b.astype(BF16), preferred_element_type=F32)


def _dot_nt(a, b):
    return lax.dot_general(a.astype(BF16), b.astype(BF16), NT_DIMS, preferred_element_type=F32)


def _dot_tn(a, b):
    return lax.dot_general(a.astype(BF16), b.astype(BF16), TN_DIMS, preferred_element_type=F32)


def _dot_hi(a, b):
    return jnp.dot(a, b, precision=HI, preferred_element_type=F32)


def _dot_nt_hi(a, b):
    return lax.dot_general(a, b, NT_DIMS, precision=HI, preferred_element_type=F32)


def _dot_tn_hi(a, b):
    return lax.dot_general(a, b, TN_DIMS, precision=HI, preferred_element_type=F32)


def _silu(x):
    return x * jax.nn.sigmoid(x)


def _softplus(x):
    return jnp.maximum(x, 0.0) + jnp.log1p(jnp.exp(-jnp.abs(x)))


def _iota(shape, dim):
    return lax.broadcasted_iota(jnp.int32, shape, dim)


def _rms(x, g):
    return x * lax.rsqrt(jnp.mean(x * x, axis=-1, keepdims=True) + NORM_EPS) * g


def _full(shape):
    nd = len(shape)
    return pl.BlockSpec(shape, lambda *_: (0,) * nd)


def _norm_matmul_kernel(x_ref, g_ref, w_ref, o_ref, h_scr):
    @pl.when(pl.program_id(1) == 0)
    def _():
        h_scr[...] = _rms(x_ref[...], g_ref[...]).astype(BF16)

    o_ref[...] = jnp.dot(h_scr[...], w_ref[...], preferred_element_type=F32)


def _norm_matmul(x, g, w, tm, tn):
    m, k = x.shape
    n = w.shape[1]
    return pl.pallas_call(
        _norm_matmul_kernel,
        grid=(m // tm, n // tn),
        in_specs=[pl.BlockSpec((tm, k), lambda i, j: (i, 0)),
                  pl.BlockSpec((1, k), lambda i, j: (0, 0)),
                  pl.BlockSpec((k, tn), lambda i, j: (0, j))],
        out_specs=pl.BlockSpec((tm, tn), lambda i, j: (i, j)),
        out_shape=jax.ShapeDtypeStruct((m, n), F32),
        scratch_shapes=[pltpu.VMEM((tm, k), BF16)],
        compiler_params=_cparams(("parallel", "arbitrary")),
        name="norm_matmul",
    )(x, g.reshape(1, k), w)


def _out_proj_kernel(routed, ya_ref, yb_ref, yc_ref, w_ref, r_ref, g_ref, *rest):
    x = (r_ref[...]
         + jnp.dot(ya_ref[...], w_ref[0:A_DIM, :], preferred_element_type=F32)
         + jnp.dot(yb_ref[...], w_ref[A_DIM:A_DIM + B_DIM, :], preferred_element_type=F32)
         + jnp.dot(yc_ref[...], w_ref[A_DIM + B_DIM:, :], preferred_element_type=F32))
    h = _rms(x, g_ref[...])
    h_hi = h.astype(BF16)
    if routed:
        rw_ref, x_ref, h_ref, l_ref = rest
        h_lo = (h - h_hi.astype(F32)).astype(BF16)
        rw = rw_ref[...]
        l2 = jnp.dot(h_hi, rw, preferred_element_type=F32)
        l_ref[...] = (l2[:, :LANES] + l2[:, LANES:]
                      + jnp.dot(h_lo, rw[:, :LANES], preferred_element_type=F32))
    else:
        x_ref, h_ref = rest
    x_ref[...] = x
    h_ref[...] = h_hi


def _out_proj(ya, yb, yc, w, res, g, rw, tm):
    m = ya.shape[0]
    n = w.shape[1]
    routed = rw is not None
    row = lambda width: pl.BlockSpec((tm, width), lambda i: (i, 0))
    in_specs = [row(A_DIM), row(B_DIM), row(C_DIM), _full(w.shape), row(n), _full((1, n))]
    out_specs = [row(n), row(n)]
    out_shape = [jax.ShapeDtypeStruct((m, n), F32), jax.ShapeDtypeStruct((m, n), BF16)]
    args = [ya, yb, yc, w, res, g.reshape(1, n)]
    if routed:
        in_specs.append(_full(rw.shape))
        out_specs.append(row(LANES))
        out_shape.append(jax.ShapeDtypeStruct((m, LANES), F32))
        args.append(rw)
    outs = pl.pallas_call(
        functools.partial(_out_proj_kernel, routed),
        grid=(m // tm,),
        in_specs=in_specs, out_specs=out_specs, out_shape=out_shape,
        compiler_params=_cparams(("parallel",)),
        name="out_proj",
    )(*args)
    return outs if routed else (*outs, None)


def _ffn_up_kernel(h_ref, wg_ref, wu_ref, o_ref, wg_s, wu_s):
    @pl.when(pl.program_id(1) == 0)
    def _():
        wg_s[...] = wg_ref[...].astype(BF16)
        wu_s[...] = wu_ref[...].astype(BF16)

    h = h_ref[...]
    a = jnp.dot(h, wg_s[...], preferred_element_type=F32)
    u = jnp.dot(h, wu_s[...], preferred_element_type=F32)
    o_ref[...] = (_silu(a) * u).astype(BF16)


def _ffn_down_kernel(h_ref, wd_ref, x_ref, o_ref, wd_s):
    @pl.when(pl.program_id(1) == 0)
    def _():
        wd_s[...] = wd_ref[...].astype(BF16)

    o_ref[...] = x_ref[...] + jnp.dot(h_ref[...], wd_s[...], preferred_element_type=F32)


def _ffn(x, h, wg, wu, wd, tm):
    m, k = h.shape
    ff = wg.shape[1]
    tf, tn = FFN_TF, FFN_TN
    act = pl.pallas_call(
        _ffn_up_kernel,
        grid=(pl.cdiv(ff, tf), m // tm),
        in_specs=[pl.BlockSpec((tm, k), lambda f, i: (i, 0)),
                  pl.BlockSpec((k, tf), lambda f, i: (0, f)),
                  pl.BlockSpec((k, tf), lambda f, i: (0, f))],
        out_specs=pl.BlockSpec((tm, tf), lambda f, i: (i, f)),
        out_shape=jax.ShapeDtypeStruct((m, ff), BF16),
        scratch_shapes=[pltpu.VMEM((k, tf), BF16)] * 2,
        compiler_params=_cparams(("arbitrary", "arbitrary")),
        name="ffn_up",
    )(h, wg, wu)
    return pl.pallas_call(
        _ffn_down_kernel,
        grid=(k // tn, m // tm),
        in_specs=[pl.BlockSpec((tm, ff), lambda n, i: (i, 0)),
                  pl.BlockSpec((ff, tn), lambda n, i: (0, n)),
                  pl.BlockSpec((tm, tn), lambda n, i: (i, n))],
        out_specs=pl.BlockSpec((tm, tn), lambda n, i: (i, n)),
        out_shape=jax.ShapeDtypeStruct((m, k), F32),
        scratch_shapes=[pltpu.VMEM((ff, tn), BF16)],
        compiler_params=_cparams(("arbitrary", "arbitrary")),
        name="ffn_down",
    )(act, wd, x)


def _moe_up_kernel(bexp_ref, nv_ref, first_ref, x_ref, wg_ref, wu_ref, o_ref, wg_s, wu_s):
    b = pl.program_id(1)

    @pl.when(first_ref[b] == 1)
    def _():
        wg_s[...] = wg_ref[...].astype(BF16)
        wu_s[...] = wu_ref[...].astype(BF16)

    @pl.when(b < nv_ref[0])
    def _():
        x = x_ref[...]
        a = jnp.dot(x, wg_s[...], preferred_element_type=F32)
        u = jnp.dot(x, wu_s[...], preferred_element_type=F32)
        o_ref[...] = (_silu(a) * u).astype(BF16)


def _moe_down_kernel(bexp_ref, nv_ref, first_ref, h_ref, wd_ref, o_ref, wd_s):
    b = pl.program_id(1)

    @pl.when(first_ref[b] == 1)
    def _():
        wd_s[...] = wd_ref[...].astype(BF16)

    @pl.when(b < nv_ref[0])
    def _():
        o_ref[...] = jnp.dot(h_ref[...], wd_s[...], preferred_element_type=F32)


def _moe_experts(xb, bexp, nvalid, first, wg, wu, wd):
    n_slots, k = xb.shape
    ff = wg.shape[2]
    nb = n_slots // MOE_BM
    tf, tn = MOE_TF, MOE_TN

    def blk(b, nv):
        return jnp.minimum(b, nv[0] - 1)

    act = pl.pallas_call(
        _moe_up_kernel,
        grid_spec=pltpu.PrefetchScalarGridSpec(
            num_scalar_prefetch=3,
            grid=(ff // tf, nb),
            in_specs=[pl.BlockSpec((MOE_BM, k), lambda f, b, be, nv, fr: (blk(b, nv), 0)),
                      pl.BlockSpec((None, k, tf), lambda f, b, be, nv, fr: (be[b], 0, f)),
                      pl.BlockSpec((None, k, tf), lambda f, b, be, nv, fr: (be[b], 0, f))],
            out_specs=pl.BlockSpec((MOE_BM, tf), lambda f, b, be, nv, fr: (blk(b, nv), f)),
            scratch_shapes=[pltpu.VMEM((k, tf), BF16)] * 2,
        ),
        out_shape=jax.ShapeDtypeStruct((n_slots, ff), BF16),
        compiler_params=_cparams(("arbitrary", "arbitrary"), 56),
        name="moe_up",
    )(bexp, nvalid, first, xb, wg, wu)
    return pl.pallas_call(
        _moe_down_kernel,
        grid_spec=pltpu.PrefetchScalarGridSpec(
            num_scalar_prefetch=3,
            grid=(k // tn, nb),
            in_specs=[pl.BlockSpec((MOE_BM, ff), lambda n, b, be, nv, fr: (blk(b, nv), 0)),
                      pl.BlockSpec((None, ff, tn), lambda n, b, be, nv, fr: (be[b], 0, n))],
            out_specs=pl.BlockSpec((MOE_BM, tn), lambda n, b, be, nv, fr: (blk(b, nv), n)),
            scratch_shapes=[pltpu.VMEM((ff, tn), BF16)],
        ),
        out_shape=jax.ShapeDtypeStruct((n_slots, k), F32),
        compiler_params=_cparams(("arbitrary", "arbitrary"), 56),
        name="moe_down",
    )(bexp, nvalid, first, act, wd)


def _final_norm_kernel(x_ref, y_ref, g_ref, o_ref):
    o_ref[...] = _rms(x_ref[...] + y_ref[...], g_ref[...])


def _final_norm(x, y, g, tm):
    m, k = x.shape
    return pl.pallas_call(
        _final_norm_kernel,
        grid=(m // tm,),
        in_specs=[pl.BlockSpec((tm, k), lambda i: (i, 0)),
                  pl.BlockSpec((tm, k), lambda i: (i, 0)),
                  pl.BlockSpec((1, k), lambda i: (0, 0))],
        out_specs=pl.BlockSpec((tm, k), lambda i: (i, 0)),
        out_shape=jax.ShapeDtypeStruct((m, k), F32),
        compiler_params=_cparams(("parallel",)),
        name="final_norm",
    )(x, y, g.reshape(1, k))


def _ssd_gate_norm(y, z, ng):
    y = y * _silu(z)
    gw = A_DIM // A_GROUPS
    outs = []
    for g in range(A_GROUPS):
        yg = y[:, g * gw:(g + 1) * gw]
        outs.append(yg * lax.rsqrt(jnp.mean(yg * yg, axis=-1, keepdims=True) + NORM_EPS))
    return jnp.concatenate(outs, axis=-1) * ng


def _ssd_prompt_kernel(z_ref, x_ref, b_ref, c_ref, dt_ref, cwx, cwb, cwc, cbx, cbb, cbc,
                       dtb, aneg, dsk, ng, y_ref, st_ref, h_scr, xpx, xpb, xpc, yacc):
    c = pl.program_id(1)
    ch = SSD_CHUNK

    @pl.when(c == 0)
    def _():
        h_scr[...] = jnp.zeros_like(h_scr)
        xpx[0:8, :] = jnp.zeros((8, A_DIM), F32)
        xpb[0:8, :] = jnp.zeros((8, A_STATE * A_GROUPS), F32)
        xpc[0:8, :] = jnp.zeros((8, A_STATE * A_GROUPS), F32)

    xpx[8:8 + ch, :] = x_ref[...]
    xpb[8:8 + ch, :] = b_ref[...]
    xpc[8:8 + ch, :] = c_ref[...]

    def conv(xp, cw, cb):
        acc = cb[...] + xp[pl.ds(8 - (CONV_W - 1), ch), :] * cw[0:1, :]
        for tap in range(1, CONV_W):
            acc = acc + xp[pl.ds(8 - (CONV_W - 1) + tap, ch), :] * cw[tap:tap + 1, :]
        return _silu(acc)

    ux = conv(xpx, cwx, cbx)
    ub = conv(xpb, cwb, cbb)
    uc = conv(xpc, cwc, cbc)
    xpx[0:8, :] = xpx[ch:ch + 8, :]
    xpb[0:8, :] = xpb[ch:ch + 8, :]
    xpc[0:8, :] = xpc[ch:ch + 8, :]

    dt = _softplus(dt_ref[...] + dtb[...])
    la = dt * aneg[...]
    tri = (_iota((ch, ch), 1) <= _iota((ch, ch), 0)).astype(F32)
    cum = _dot_hi(tri, la)
    eye = (_iota((A_HEADS, LANES), 0) == _iota((A_HEADS, LANES), 1)).astype(F32)
    cum_r = _dot_nt_hi(eye, cum)
    dt_r = _dot_nt_hi(eye, dt)
    cl = cum[ch - 1:ch, :]
    tail = jnp.exp(cl - cum) * dt
    ecum = jnp.exp(cum)
    ecl = jnp.exp(cl)
    causal = _iota((ch, ch), 1) <= _iota((ch, ch), 0)
    rep = A_HEADS // A_GROUPS

    for g in range(A_GROUPS):
        bg = ub[:, g * A_STATE:(g + 1) * A_STATE].astype(BF16)
        cg = uc[:, g * A_STATE:(g + 1) * A_STATE].astype(BF16)
        cb_g = _dot_nt(cg, bg)
        for h in range(g * rep, (g + 1) * rep):
            hs = slice(h * A_HEAD, (h + 1) * A_HEAD)
            seg = cum[:, h:h + 1] - cum_r[h:h + 1, :]
            w = jnp.exp(jnp.where(causal, seg, -jnp.inf)) * cb_g * dt_r[h:h + 1, :]
            xh = ux[:, hs]
            hst = h_scr[h]
            y_h = _dot(w, xh) + _dot_nt(cg, hst) * ecum[:, h:h + 1]
            yacc[:, hs] = y_h
            h_scr[h] = hst * ecl[:, h:h + 1] + _dot_tn(xh * tail[:, h:h + 1], bg)

    y = yacc[...] + dsk[...] * ux
    y_ref[...] = _ssd_gate_norm(y, z_ref[...], ng[...]).astype(BF16)

    @pl.when(c == pl.num_programs(1) - 1)
    def _():
        st_ref[...] = h_scr[...]


def _ssd_params(p):
    cw, cb = p["conv_w"], p["conv_b"].reshape(1, CONV_CH)
    pad = LANES - A_HEADS
    return dict(
        cwx=cw[:, :A_DIM], cwb=cw[:, A_DIM:A_DIM + A_GN], cwc=cw[:, A_DIM + A_GN:],
        cbx=cb[:, :A_DIM], cbb=cb[:, A_DIM:A_DIM + A_GN], cbc=cb[:, A_DIM + A_GN:],
        dtb=jnp.pad(p["dt_bias"], (0, pad)).reshape(1, LANES),
        aneg=jnp.pad(-jnp.exp(p["a_log"]), (0, pad)).reshape(1, LANES),
        dsk=jnp.repeat(p["d_skip"], A_HEAD).reshape(1, A_DIM),
        ng=p["ssm_norm_g"].reshape(1, A_DIM),
    )


def _ssd_prompt(proj, sp):
    bsz, seq, _ = proj.shape
    ch = SSD_CHUNK
    gn = A_GN

    def col(width, off):
        return pl.BlockSpec((None, ch, width), lambda b, c: (b, c, off // width))

    names = ("cwx", "cwb", "cwc", "cbx", "cbb", "cbc", "dtb", "aneg", "dsk", "ng")
    y, st = pl.pallas_call(
        _ssd_prompt_kernel,
        grid=(bsz, seq // ch),
        in_specs=[col(A_DIM, P_Z), col(A_DIM, P_X), col(gn, P_B), col(gn, P_C), col(LANES, P_DT)]
                 + [_full(sp[n].shape) for n in names],
        out_specs=[pl.BlockSpec((None, ch, A_DIM), lambda b, c: (b, c, 0)),
                   pl.BlockSpec((None, A_HEADS, A_HEAD, A_STATE), lambda b, c: (b, 0, 0, 0))],
        out_shape=[jax.ShapeDtypeStruct((bsz, seq, A_DIM), BF16),
                   jax.ShapeDtypeStruct((bsz, A_HEADS, A_HEAD, A_STATE), F32)],
        scratch_shapes=[pltpu.VMEM((A_HEADS, A_HEAD, A_STATE), F32),
                        pltpu.VMEM((ch + 8, A_DIM), F32),
                        pltpu.VMEM((ch + 8, gn), F32),
                        pltpu.VMEM((ch + 8, gn), F32),
                        pltpu.VMEM((ch, A_DIM), F32)],
        compiler_params=_cparams(("parallel", "arbitrary")),
        name="ssd_prompt",
    )(proj, proj, proj, proj, proj, *[sp[n] for n in names])
    return y, st


def _ssd_step_kernel(z_ref, x_ref, b_ref, c_ref, dt_ref, cs_ref, cw, cb, dtb, anegx, expand, dsk, ng,
                     h_ref, y_ref, ho_ref, y_scr):
    bb = STEP_BB
    xbc = (x_ref[...], b_ref[...], c_ref[...])
    offs = (0, A_DIM, A_DIM + A_GN, CONV_CH)
    u = []
    for i in range(3):
        sl = slice(offs[i], offs[i + 1])
        acc = cb[:, sl] + xbc[i] * cw[CONV_W - 1:CONV_W, sl]
        for tap in range(CONV_W - 1):
            acc = acc + cs_ref[tap][:, sl] * cw[tap:tap + 1, sl]
        u.append(_silu(acc))
    ux, ub, uc = u
    dt = _softplus(dt_ref[...] + dtb[...])
    dtx = _dot_hi(dt, expand[...])
    dec_t = jnp.exp(dtx * anegx[...]).T
    xdt_t = (ux * dtx).T
    gw = A_DIM // A_GROUPS
    for b in range(bb):
        for g in range(A_GROUPS):
            rs = slice(g * gw, (g + 1) * gw)
            ns = slice(g * A_STATE, (g + 1) * A_STATE)
            hg = h_ref[b, rs, :] * dec_t[rs, b:b + 1] + xdt_t[rs, b:b + 1] * ub[b:b + 1, ns]
            ho_ref[b, rs, :] = hg
            y_scr[b:b + 1, rs] = _dot_nt(uc[:, ns], hg)[b:b + 1, :]
    y = y_scr[...] + dsk[...] * ux
    y_ref[...] = _ssd_gate_norm(y, z_ref[...], ng[...]).astype(BF16)


def _ssd_step(proj, conv_state, h0, p, sp):
    bsz = proj.shape[0]
    bb = STEP_BB
    cs = jnp.swapaxes(conv_state, 0, 1)
    hflat = h0.reshape(bsz, A_DIM, A_STATE)
    expand = (jnp.arange(LANES)[:, None] == (jnp.arange(A_DIM) // A_HEAD)[None, :]).astype(F32)
    anegx = jnp.repeat(-jnp.exp(p["a_log"]), A_HEAD).reshape(1, A_DIM)
    cw, cb = p["conv_w"], p["conv_b"].reshape(1, CONV_CH)

    def col(width, off):
        return pl.BlockSpec((bb, width), lambda i: (i, off // width))

    y, hn = pl.pallas_call(
        _ssd_step_kernel,
        grid=(bsz // bb,),
        in_specs=[col(A_DIM, P_Z), col(A_DIM, P_X), col(A_GN, P_B), col(A_GN, P_C), col(LANES, P_DT),
                  pl.BlockSpec((CONV_W - 1, bb, CONV_CH), lambda i: (0, i, 0)),
                  _full(cw.shape), _full(cb.shape), _full(sp["dtb"].shape), _full(anegx.shape),
                  _full(expand.shape), _full(sp["dsk"].shape), _full(sp["ng"].shape),
                  pl.BlockSpec((bb, A_DIM, A_STATE), lambda i: (i, 0, 0))],
        out_specs=[pl.BlockSpec((bb, A_DIM), lambda i: (i, 0)),
                   pl.BlockSpec((bb, A_DIM, A_STATE), lambda i: (i, 0, 0))],
        out_shape=[jax.ShapeDtypeStruct((bsz, A_DIM), BF16),
                   jax.ShapeDtypeStruct((bsz, A_DIM, A_STATE), F32)],
        scratch_shapes=[pltpu.VMEM((bb, A_DIM), F32)],
        compiler_params=_cparams(("parallel",)),
        name="ssd_step",
    )(proj, proj, proj, proj, proj, cs, cw, cb, sp["dtb"], anegx, expand, sp["dsk"], sp["ng"], hflat)
    return y, hn.reshape(h0.shape)


def _hgrn_gates(q, f, lbp):
    qf = _silu(q)
    kf = lbp[0:1, :] * jax.nn.sigmoid(-f)
    log_sig = jnp.minimum(f, 0.0) - jnp.log1p(jnp.exp(-jnp.abs(f)))
    a = lbp[1:2, :]
    b = lbp[2:3, :] + log_sig
    logf = jnp.maximum(a, b) + jnp.log1p(jnp.exp(-jnp.abs(a - b)))
    return qf, kf, logf


def _hgrn_out(o, g, ng):
    outs = []
    for h in range(B_HEADS):
        oh = o[:, h * B_HEAD:(h + 1) * B_HEAD]
        outs.append(oh * lax.rsqrt(jnp.mean(oh * oh, axis=-1, keepdims=True) + NORM_EPS) * ng)
    return jnp.concatenate(outs, axis=-1) * _silu(g)


def _hgrn_prompt_kernel(q_ref, f_ref, i_ref, g_ref, lbp, ng, o_ref, st_ref,
                        s_scr, cum_scr, q_scr, k_scr, o_scr):
    c = pl.program_id(1)
    tb, cs = HGRN_BLOCK, HGRN_CHUNK

    @pl.when(c == 0)
    def _():
        s_scr[...] = jnp.zeros_like(s_scr)

    qf, kf, logf = _hgrn_gates(q_ref[...], f_ref[...], lbp[...])
    r_i, c_i = _iota((tb, tb), 0), _iota((tb, tb), 1)
    bd = ((r_i // cs == c_i // cs) & (c_i <= r_i)).astype(F32)
    cum_scr[...] = _dot_hi(bd, logf)
    q_scr[...] = qf
    k_scr[...] = kf
    rows = _iota((cs, B_HEAD), 0)

    def chunk(cc, carry):
        r0 = pl.multiple_of(cc * cs, cs)
        for h in range(B_HEADS):
            hs = slice(h * B_HEAD, (h + 1) * B_HEAD)
            cu = cum_scr[pl.ds(r0, cs), hs]
            q = q_scr[pl.ds(r0, cs), hs]
            k = k_scr[pl.ds(r0, cs), hs]
            v = i_ref[pl.ds(r0, cs), hs]
            st = s_scr[h]
            o = _dot_nt(q * jnp.exp(cu), st)
            for s in range(cs):
                d = jnp.exp(jnp.where(rows >= s, cu - cu[s:s + 1, :], -jnp.inf))
                att = jnp.sum(q * k[s:s + 1, :] * d, axis=-1, keepdims=True)
                o = o + att * v[s:s + 1, :]
            o_scr[pl.ds(r0, cs), hs] = o
            cl = cu[cs - 1:cs, :]
            s_scr[h] = st * jnp.exp(cl) + _dot_tn(v, k * jnp.exp(cl - cu))
        return carry

    lax.fori_loop(0, tb // cs, chunk, 0)
    o_ref[...] = _hgrn_out(o_scr[...], g_ref[...], ng[...]).astype(BF16)

    @pl.when(c == pl.num_programs(1) - 1)
    def _():
        for h in range(B_HEADS):
            st_ref[h] = s_scr[h].T


def _hgrn_prompt(proj, lbp, ng):
    bsz, seq, _ = proj.shape
    tb = HGRN_BLOCK

    def col(off):
        return pl.BlockSpec((None, tb, B_DIM), lambda b, c: (b, c, off // B_DIM))

    o, st = pl.pallas_call(
        _hgrn_prompt_kernel,
        grid=(bsz, seq // tb),
        in_specs=[col(P_Q), col(P_F), col(P_I), col(P_G), _full(lbp.shape), _full(ng.shape)],
        out_specs=[pl.BlockSpec((None, tb, B_DIM), lambda b, c: (b, c, 0)),
                   pl.BlockSpec((None, B_HEADS, B_HEAD, B_HEAD), lambda b, c: (b, 0, 0, 0))],
        out_shape=[jax.ShapeDtypeStruct((bsz, seq, B_DIM), BF16),
                   jax.ShapeDtypeStruct((bsz, B_HEADS, B_HEAD, B_HEAD), F32)],
        scratch_shapes=[pltpu.VMEM((B_HEADS, B_HEAD, B_HEAD), F32)]
                       + [pltpu.VMEM((tb, B_DIM), F32)] * 4,
        compiler_params=_cparams(("parallel", "arbitrary")),
        name="hgrn_prompt",
    )(proj, proj, proj, proj, lbp, ng)
    return o, st


def _hgrn_step_kernel(q_ref, f_ref, i_ref, g_ref, lbp, ng, s_ref, o_ref, so_ref, o_scr):
    bb = STEP_BB
    qf, kf, logf = _hgrn_gates(q_ref[...], f_ref[...], lbp[...])
    vf = i_ref[...]
    dec_t = jnp.exp(logf).T
    k_t = kf.T
    q_t = qf.T
    for b in range(bb):
        for h in range(B_HEADS):
            hs = slice(h * B_HEAD, (h + 1) * B_HEAD)
            sh = s_ref[b, hs, :] * dec_t[hs, b:b + 1] + k_t[hs, b:b + 1] * vf[b:b + 1, hs]
            so_ref[b, hs, :] = sh
            o_scr[b:b + 1, hs] = jnp.sum(q_t[hs, b:b + 1] * sh, axis=0, keepdims=True)
    o_ref[...] = _hgrn_out(o_scr[...], g_ref[...], ng[...]).astype(BF16)


def _hgrn_step(proj, s0, lbp, ng):
    bsz = proj.shape[0]
    bb = STEP_BB
    sflat = s0.reshape(bsz, B_DIM, B_HEAD)

    def col(off):
        return pl.BlockSpec((bb, B_DIM), lambda i: (i, off // B_DIM))

    o, sn = pl.pallas_call(
        _hgrn_step_kernel,
        grid=(bsz // bb,),
        in_specs=[col(P_Q), col(P_F), col(P_I), col(P_G), _full(lbp.shape), _full(ng.shape),
                  pl.BlockSpec((bb, B_DIM, B_HEAD), lambda i: (i, 0, 0))],
        out_specs=[pl.BlockSpec((bb, B_DIM), lambda i: (i, 0)),
                   pl.BlockSpec((bb, B_DIM, B_HEAD), lambda i: (i, 0, 0))],
        out_shape=[jax.ShapeDtypeStruct((bsz, B_DIM), BF16),
                   jax.ShapeDtypeStruct((bsz, B_DIM, B_HEAD), F32)],
        scratch_shapes=[pltpu.VMEM((bb, B_DIM), F32)],
        compiler_params=_cparams(("parallel",)),
        name="hgrn_step",
    )(proj, proj, proj, proj, lbp, ng, sflat)
    return o, sn.reshape(s0.shape)


RW_NAMES = ("mu_r", "mu_k", "mu_v", "mu_lr", "w0", "w2", "a0", "a2", "g2", "k_k", "k_a", "r_k")


def _rwkv_params(p):
    mu = p["shift_mu"].reshape(1, C_COLS)
    row = lambda a: a.reshape(1, C_DIM)
    return dict(
        mu_r=mu[:, :C_DIM], mu_k=mu[:, C_DIM:2 * C_DIM], mu_v=mu[:, 2 * C_DIM:3 * C_DIM],
        mu_lr=mu[:, 3 * C_DIM:],
        w0=row(p["w0"]), w2=p["w2"].astype(BF16), a0=row(p["a0"]), a2=p["a2"].astype(BF16),
        g2=p["g2"].astype(BF16), k_k=row(p["k_k"]), k_a=row(p["k_a"]), r_k=row(p["r_k"]),
        lnx_w=row(p["lnx_w"]), lnx_b=row(p["lnx_b"]),
    )


def _head_sum(x):
    return [jnp.sum(x[:, h * C_HEAD:(h + 1) * C_HEAD], axis=-1, keepdims=True) for h in range(C_HEADS)]


def _head_bcast(cols, fn=lambda c: c):
    rows = cols[0].shape[0]
    return jnp.concatenate([jnp.broadcast_to(fn(c), (rows, C_HEAD)) for c in cols], axis=-1)


def _rwkv_prep(r, k, v, lr, pr, pk, pv, plr, prm):
    mu_r, mu_k, mu_v, mu_lr, w0, w2, a0, a2, g2, k_k, k_a, r_k = prm
    r = r + (pr - r) * mu_r
    k = k + (pk - k) * mu_k
    v = v + (pv - v) * mu_v
    lr = lr + (plr - lr) * mu_lr
    wl, al, gl = lr[:, :C_LR_W], lr[:, C_LR_W:C_LR_W + C_LR_A], lr[:, C_LR_W + C_LR_A:]
    wraw = -_softplus(-(w0 + _dot(jnp.tanh(wl), w2))) - 0.5
    ld = -jnp.exp(wraw)
    a = jax.nn.sigmoid(a0 + _dot(al, a2))
    g = _dot(jax.nn.sigmoid(gl), g2)
    kk = k * k_k
    kk = kk * _head_bcast(_head_sum(kk * kk), lambda n2: 1.0 / jnp.maximum(jnp.sqrt(n2), L2_EPS))
    k2 = k * (1.0 + (a - 1.0) * k_a)
    bonus = _head_bcast(_head_sum(r * k2 * r_k)) * v
    return r, k2, v, kk, kk * a, ld, g, bonus


def _rwkv_out(y, bonus, g, lnx_w, lnx_b):
    n = float(C_HEAD)
    mean = _head_bcast(_head_sum(y), lambda s: s / n)
    yc = y - mean
    rstd = _head_bcast(_head_sum(yc * yc), lambda s: lax.rsqrt(s / n + GN_EPS))
    return (yc * rstd * lnx_w + lnx_b + bonus) * g


def _rwkv_a_kernel(r_ref, k_ref, v_ref, lr_ref, rp_ref, kp_ref, vp_ref, lrp_ref, *rest):
    prm = tuple(x[...] for x in rest[:len(RW_NAMES)])
    (kd_ref, rd_ref, bg_ref, m2_ref, rkv_ref, rb_ref, aab_ref, vk_ref, ge_ref, bonus_ref, gate_ref,
     xs_scr, b1_scr, b2_scr, b3_scr, b4_scr) = rest[len(RW_NAMES):]
    c = pl.program_id(1)
    tb, cs = RWKV_BLOCK, RWKV_CHUNK
    first = (c == 0)

    def prev(cur_ref, prev_ref, width):
        xs_scr[8:8 + tb, 0:width] = cur_ref[...]
        xs_scr[7:8, 0:width] = jnp.where(first, 0.0, prev_ref[7:8, :])
        return xs_scr[pl.ds(7, tb), 0:width]

    pr = prev(r_ref, rp_ref, C_DIM)
    pk = prev(k_ref, kp_ref, C_DIM)
    pv = prev(v_ref, vp_ref, C_DIM)
    plr = prev(lr_ref, lrp_ref, C_LR)
    r, k2, v, kk, beta, ld, g, bonus = _rwkv_prep(
        r_ref[...], k_ref[...], v_ref[...], lr_ref[...], pr, pk, pv, plr, prm)
    bonus_ref[...] = bonus
    gate_ref[...] = g

    r_i, c_i = _iota((tb, tb), 0), _iota((tb, tb), 1)
    same = (r_i // cs == c_i // cs)
    lcum = _dot_hi((same & (c_i <= r_i)).astype(F32), ld)
    lend = _dot_hi(same.astype(F32), ld)
    einv = jnp.exp(-lcum)
    eend = jnp.exp(lend - lcum)
    ge_ref[...] = jnp.exp(lend)
    kd_ref[...] = kk * jnp.exp(lcum - ld)
    rd_ref[...] = r * jnp.exp(lcum)
    bg_ref[...] = beta * eend
    b1_scr[...] = beta * einv
    b2_scr[...] = k2 * einv
    b3_scr[...] = k2 * eend
    b4_scr[...] = v
    rr, cc = _iota((cs, cs), 0), _iota((cs, cs), 1)
    strict, incl = rr > cc, rr >= cc

    def chunk(j, carry):
        r0 = pl.multiple_of(j * cs, cs)
        for h in range(C_HEADS):
            hs = slice(h * C_HEAD, (h + 1) * C_HEAD)
            kd = kd_ref[pl.ds(r0, cs), hs].astype(BF16)
            rd = rd_ref[pl.ds(r0, cs), hs].astype(BF16)
            bi = b1_scr[pl.ds(r0, cs), hs].astype(BF16)
            ki = b2_scr[pl.ds(r0, cs), hs].astype(BF16)
            kg = b3_scr[pl.ds(r0, cs), hs]
            vh = b4_scr[pl.ds(r0, cs), hs].astype(BF16)
            aab_ref[pl.ds(r0, cs), hs] = jnp.where(strict, _dot_nt(kd, bi), 0.0)
            aak = jnp.where(strict, _dot_nt(kd, ki), 0.0)
            rb_ref[pl.ds(r0, cs), hs] = jnp.where(incl, _dot_nt(rd, bi), 0.0)
            rk = jnp.where(incl, _dot_nt(rd, ki), 0.0)
            m2_ref[pl.ds(r0, cs), hs] = _dot(aak, vh)
            rkv_ref[pl.ds(r0, cs), hs] = _dot(rk, vh)
            vk_ref[pl.ds(r0, cs), hs] = _dot_tn(vh, kg)
        return carry

    lax.fori_loop(0, tb // cs, chunk, 0)


def _tri_solve_kernel(a_ref, t_ref):
    n, _, npb = a_ref.shape
    rowj = _iota((n, npb), 0)

    def row(t, carry):
        def inner(s, acc):
            return acc - a_ref[t, pl.ds(s, 1), :] * t_ref[s]

        t_ref[t] = lax.fori_loop(0, t, inner, (rowj == t).astype(F32))
        return carry

    lax.fori_loop(0, n, row, 0)


def _tri_solve(a):
    n, _, nprob = a.shape
    npb = min(256, nprob)
    return pl.pallas_call(
        _tri_solve_kernel,
        grid=(nprob // npb,),
        in_specs=[pl.BlockSpec((n, n, npb), lambda i: (0, 0, i))],
        out_specs=pl.BlockSpec((n, n, npb), lambda i: (0, 0, i)),
        out_shape=jax.ShapeDtypeStruct(a.shape, F32),
        compiler_params=_cparams(("parallel",)),
        name="tri_solve",
    )(a)


def _rwkv_c_kernel(t_ref, kd_ref, rd_ref, bg_ref, m2_ref, rkv_ref, rb_ref, vk_ref, ge_ref,
                   bonus_ref, gate_ref, lnw, lnb, y_ref, st_ref, s_scr, y_scr):
    c = pl.program_id(1)
    tb, cs = RWKV_BLOCK, RWKV_CHUNK

    @pl.when(c == 0)
    def _():
        s_scr[...] = jnp.zeros_like(s_scr)

    def chunk(j, carry):
        r0 = pl.multiple_of(j * cs, cs)
        for h in range(C_HEADS):
            hs = slice(h * C_HEAD, (h + 1) * C_HEAD)
            tm = t_ref[pl.ds(r0, cs), hs].astype(BF16)
            w = _dot(tm, kd_ref[pl.ds(r0, cs), hs])
            u = _dot(tm, m2_ref[pl.ds(r0, cs), hs])
            s = s_scr[h]
            p = -_dot_nt(w, s) - u
            y = _dot_nt(rd_ref[pl.ds(r0, cs), hs], s) + _dot(rb_ref[pl.ds(r0, cs), hs], p)
            y_scr[pl.ds(r0, cs), hs] = y + rkv_ref[pl.ds(r0, cs), hs]
            s_scr[h] = (s * ge_ref[pl.ds(r0, 1), hs] + _dot_tn(p, bg_ref[pl.ds(r0, cs), hs])
                        + vk_ref[pl.ds(r0, cs), hs])
        return carry

    lax.fori_loop(0, tb // cs, chunk, 0)
    y_ref[...] = _rwkv_out(y_scr[...], bonus_ref[...], gate_ref[...], lnw[...], lnb[...]).astype(BF16)

    @pl.when(c == pl.num_programs(1) - 1)
    def _():
        st_ref[...] = s_scr[...]


def _rwkv_prompt(proj, rp):
    bsz, seq, _ = proj.shape
    tb, cs = RWKV_BLOCK, RWKV_CHUNK
    nblk = seq // tb

    def col(width, off):
        return pl.BlockSpec((None, tb, width), lambda b, c: (b, c, off // width))

    def pcol(width, off):
        return pl.BlockSpec((None, 8, width), lambda b, c: (b, jnp.maximum(c * (tb // 8) - 1, 0), off // width))

    tok = pl.BlockSpec((None, tb, C_DIM), lambda b, c: (b, c, 0))
    tok_shape = jax.ShapeDtypeStruct((bsz, seq, C_DIM), F32)
    prm = [rp[n] for n in RW_NAMES]
    outs = pl.pallas_call(
        _rwkv_a_kernel,
        grid=(bsz, nblk),
        in_specs=[col(C_DIM, P_R), col(C_DIM, P_K), col(C_DIM, P_V), col(C_LR, P_LR),
                  pcol(C_DIM, P_R), pcol(C_DIM, P_K), pcol(C_DIM, P_V), pcol(C_LR, P_LR)]
                 + [_full(x.shape) for x in prm],
        out_specs=[tok] * 11,
        out_shape=[tok_shape] * 11,
        scratch_shapes=[pltpu.VMEM((tb + 8, C_DIM), F32)] + [pltpu.VMEM((tb, C_DIM), F32)] * 4,
        compiler_params=_cparams(("parallel", "parallel")),
        name="rwkv_prep",
    )(proj, proj, proj, proj, proj, proj, proj, proj, *prm)
    kd, rd, bg, m2, rkv, rb, aab, vk, ge, bonus, gate = outs

    nch = seq // cs
    a_t = aab.reshape(bsz, nch, cs, C_HEADS, cs).transpose(2, 4, 0, 1, 3).reshape(cs, cs, -1)
    t_t = _tri_solve(a_t)
    tmat = t_t.reshape(cs, cs, bsz, nch, C_HEADS).transpose(2, 3, 0, 4, 1).reshape(bsz, seq, C_DIM)

    y, st = pl.pallas_call(
        _rwkv_c_kernel,
        grid=(bsz, nblk),
        in_specs=[tok] * 11 + [_full((1, C_DIM))] * 2,
        out_specs=[tok, pl.BlockSpec((None, C_HEADS, C_HEAD, C_HEAD), lambda b, c: (b, 0, 0, 0))],
        out_shape=[jax.ShapeDtypeStruct((bsz, seq, C_DIM), BF16),
                   jax.ShapeDtypeStruct((bsz, C_HEADS, C_HEAD, C_HEAD), F32)],
        scratch_shapes=[pltpu.VMEM((C_HEADS, C_HEAD, C_HEAD), F32), pltpu.VMEM((tb, C_DIM), F32)],
        compiler_params=_cparams(("parallel", "arbitrary")),
        name="rwkv_scan",
    )(tmat, kd, rd, bg, m2, rkv, rb, vk, ge, bonus, gate, rp["lnx_w"], rp["lnx_b"])
    return y, st


def _rwkv_step_kernel(r_ref, k_ref, v_ref, lr_ref, sh_ref, *rest):
    prm = tuple(x[...] for x in rest[:len(RW_NAMES)])
    lnw, lnb, s_ref, y_ref, so_ref = rest[len(RW_NAMES):]
    bb = STEP_BB
    sh = sh_ref[...]
    r, k2, v, kk, beta, ld, g, bonus = _rwkv_prep(
        r_ref[...], k_ref[...], v_ref[...], lr_ref[...],
        sh[:, :C_DIM], sh[:, C_DIM:2 * C_DIM], sh[:, 2 * C_DIM:3 * C_DIM], sh[:, 3 * C_DIM:], prm)
    w = jnp.exp(ld)
    v_t = v.T
    lane = _iota((C_DIM, LANES), 1)
    heads = [[x[:, h * C_HEAD:(h + 1) * C_HEAD] for x in (kk, w, beta, k2, r)] for h in range(C_HEADS)]

    def rows(i, b):
        return jnp.concatenate(
            [jnp.broadcast_to(heads[h][i][b:b + 1, :], (C_HEAD, C_HEAD)) for h in range(C_HEADS)], axis=0)

    y_t = jnp.zeros((C_DIM, LANES), F32)
    for b in range(bb):
        s = s_ref[b]
        sa = -jnp.sum(s * rows(0, b), axis=-1, keepdims=True)
        sn = s * rows(1, b) + sa * rows(2, b) + v_t[:, b:b + 1] * rows(3, b)
        so_ref[b] = sn
        y_t = jnp.where(lane == b, jnp.sum(sn * rows(4, b), axis=-1, keepdims=True), y_t)
    y = y_t.T[0:bb, :]
    y_ref[...] = _rwkv_out(y, bonus, g, lnw[...], lnb[...]).astype(BF16)


def _rwkv_step(proj, shift, s0, rp):
    bsz = proj.shape[0]
    bb = STEP_BB
    sflat = s0.reshape(bsz, C_DIM, C_HEAD)

    def col(width, off):
        return pl.BlockSpec((bb, width), lambda i: (i, off // width))

    prm = [rp[n] for n in RW_NAMES]
    y, sn = pl.pallas_call(
        _rwkv_step_kernel,
        grid=(bsz // bb,),
        in_specs=[col(C_DIM, P_R), col(C_DIM, P_K), col(C_DIM, P_V), col(C_LR, P_LR),
                  pl.BlockSpec((bb, C_COLS), lambda i: (i, 0))]
                 + [_full(x.shape) for x in prm] + [_full((1, C_DIM))] * 2
                 + [pl.BlockSpec((bb, C_DIM, C_HEAD), lambda i: (i, 0, 0))],
        out_specs=[pl.BlockSpec((bb, C_DIM), lambda i: (i, 0)),
                   pl.BlockSpec((bb, C_DIM, C_HEAD), lambda i: (i, 0, 0))],
        out_shape=[jax.ShapeDtypeStruct((bsz, C_DIM), BF16),
                   jax.ShapeDtypeStruct((bsz, C_DIM, C_HEAD), F32)],
        compiler_params=_cparams(("parallel",)),
        name="rwkv_step",
    )(proj, proj, proj, proj, shift, *prm, rp["lnx_w"], rp["lnx_b"], sflat)
    return y, sn.reshape(s0.shape)


def _layout_w_in(w):
    z, x = w[:, :A_DIM], w[:, A_DIM:2 * A_DIM]
    bm, cm = w[:, 2 * A_DIM:2 * A_DIM + A_GN], w[:, 2 * A_DIM + A_GN:A_DIM + CONV_CH]
    o = A_DIM + CONV_CH
    dt = w[:, o:o + A_HEADS]
    qfig = w[:, o + A_HEADS:o + A_HEADS + 4 * B_DIM]
    rw = w[:, o + A_HEADS + 4 * B_DIM:]
    pad = jnp.zeros((w.shape[0], DT_PAD), w.dtype)
    return jnp.concatenate([z, qfig, rw, dt, pad, x, bm, cm], axis=1).astype(BF16)


def _xbc_cols(proj):
    return jnp.concatenate([proj[..., P_X:P_X + A_DIM], proj[..., P_B:P_B + A_GN],
                            proj[..., P_C:P_C + A_GN]], axis=-1)


def _moe_route(logits, n_tok):
    top_logit, top_idx = lax.top_k(logits, TOP_K)
    gate = jax.nn.softmax(top_logit, axis=-1)
    n_assign = n_tok * TOP_K
    flat_e = top_idx.reshape(-1).astype(jnp.int32)
    order = jnp.argsort(flat_e)
    se = flat_e[order]
    counts = jnp.sum(jax.nn.one_hot(flat_e, N_EXPERTS, dtype=jnp.int32), axis=0)
    padded = (counts + MOE_BM - 1) // MOE_BM * MOE_BM
    pad_end = jnp.cumsum(padded)
    pad_start = pad_end - padded
    start = jnp.cumsum(counts) - counts
    slot_sorted = pad_start[se] + jnp.arange(n_assign, dtype=jnp.int32) - start[se]
    n_blocks = -(-n_assign // MOE_BM) + N_EXPERTS
    n_slots = n_blocks * MOE_BM
    slot_tok = jnp.full((n_slots,), n_tok, jnp.int32).at[slot_sorted].set(order // TOP_K)
    slot_of = jnp.zeros((n_assign,), jnp.int32).at[order].set(slot_sorted).reshape(n_tok, TOP_K)
    block_start = jnp.arange(n_blocks, dtype=jnp.int32) * MOE_BM
    nvalid = (pad_end[-1] // MOE_BM).astype(jnp.int32)
    bexp = jnp.minimum(jnp.searchsorted(pad_end, block_start, side="right"), N_EXPERTS - 1).astype(jnp.int32)
    last = bexp[jnp.maximum(nvalid - 1, 0)]
    bexp = jnp.where(jnp.arange(n_blocks) < nvalid, bexp, last)
    first = jnp.concatenate([jnp.ones((1,), jnp.int32), (bexp[1:] != bexp[:-1]).astype(jnp.int32)])
    return gate, slot_tok, slot_of, bexp, nvalid.reshape(1), first


def _moe(h, logits, wg, wu, wd):
    n_tok = h.shape[0]
    gate, slot_tok, slot_of, bexp, nvalid, first = _moe_route(logits[:, :N_EXPERTS], n_tok)
    h_pad = jnp.concatenate([h, jnp.zeros((1, h.shape[1]), h.dtype)], axis=0)
    xb = h_pad[slot_tok]
    yb = _moe_experts(xb, bexp, nvalid, first, wg, wu, wd)
    return gate[:, 0:1] * yb[slot_of[:, 0]] + gate[:, 1:2] * yb[slot_of[:, 1]]


def kernel(x_prompt, x_sample, state_ssm, state_conv, state_hgrn, state_rwkv, state_shift, norm1_g, w_in, conv_w, conv_b, dt_bias, a_log, d_skip, ssm_norm_g, lb_logits, hgrn_norm_g, shift_mu, w0, w2, a0, a2, g2, k_k, k_a, r_k, lnx_w, lnx_b, w_out, norm2_g, ffn_w_gate, ffn_w_up, ffn_w_down, router_w, exp_w_gate, exp_w_up, exp_w_down, final_norm_g):
    depth = w_in.shape[0]
    bp, seq, _ = x_prompt.shape
    bs = x_sample.shape[0]
    tp = bp * seq
    lb_soft = jax.nn.softmax(lb_logits.astype(F32), axis=0)
    lower_bounds = jnp.clip(jnp.cumsum(lb_soft, axis=0) - lb_soft[0:1], 0.0, 1.0)

    xp = x_prompt.reshape(tp, D_MODEL)
    xs = x_sample.reshape(bs, D_MODEL)
    outs = {k: [] for k in ("p_ssm", "p_conv", "p_hgrn", "p_rwkv", "p_shift",
                            "s_ssm", "s_conv", "s_hgrn", "s_rwkv", "s_shift")}
    y_moe_p = y_moe_s = None
    for layer in range(depth):
        p = dict(conv_w=conv_w[layer], conv_b=conv_b[layer], dt_bias=dt_bias[layer], a_log=a_log[layer],
                 d_skip=d_skip[layer], ssm_norm_g=ssm_norm_g[layer], shift_mu=shift_mu[layer],
                 w0=w0[layer], w2=w2[layer], a0=a0[layer], a2=a2[layer], g2=g2[layer], k_k=k_k[layer],
                 k_a=k_a[layer], r_k=r_k[layer].reshape(-1), lnx_w=lnx_w[layer], lnx_b=lnx_b[layer])
        sp = _ssd_params(p)
        rp = _rwkv_params(p)
        lb = lower_bounds[layer]
        lbp = jnp.stack([1.0 - lb, jnp.log(lb), jnp.log1p(-lb)])
        hng = hgrn_norm_g[layer].reshape(1, B_HEAD)
        w_in_l = _layout_w_in(w_in[layer])
        w_out_l = w_out[layer].astype(BF16)
        j = layer // 2
        routed = layer % 2 == 1
        rw = None
        if routed:
            rw_f = jnp.pad(router_w[j], ((0, 0), (0, LANES - N_EXPERTS)))
            rw_hi = rw_f.astype(BF16)
            rw = jnp.concatenate([rw_hi, (rw_f - rw_hi.astype(F32)).astype(BF16)], axis=1)

        proj = _norm_matmul(xp, norm1_g[layer], w_in_l, 1024, 512).reshape(bp, seq, P_COLS)
        ya, ssm_p = _ssd_prompt(proj, sp)
        yb, hgrn_p = _hgrn_prompt(proj, lbp, hng)
        yc, rwkv_p = _rwkv_prompt(proj, rp)
        xp, hp, lp = _out_proj(ya.reshape(tp, A_DIM), yb.reshape(tp, B_DIM), yc.reshape(tp, C_DIM),
                               w_out_l, xp, norm2_g[layer], rw, 256)
        outs["p_ssm"].append(ssm_p)
        outs["p_conv"].append(_xbc_cols(proj[:, seq - (CONV_W - 1):, :]))
        outs["p_hgrn"].append(hgrn_p)
        outs["p_rwkv"].append(rwkv_p)
        outs["p_shift"].append(proj[:, seq - 1, P_R:P_R + C_COLS])

        projs = _norm_matmul(xs, norm1_g[layer], w_in_l, 128, 512)
        ya, ssm_s = _ssd_step(projs, state_conv[layer], state_ssm[layer], p, sp)
        yb, hgrn_s = _hgrn_step(projs, state_hgrn[layer], lbp, hng)
        yc, rwkv_s = _rwkv_step(projs, state_shift[layer], state_rwkv[layer], rp)
        xs, hs, ls = _out_proj(ya, yb, yc, w_out_l, xs, norm2_g[layer], rw, 128)
        outs["s_ssm"].append(ssm_s)
        outs["s_conv"].append(jnp.concatenate([state_conv[layer][:, 1:, :], _xbc_cols(projs)[:, None, :]], axis=1))
        outs["s_hgrn"].append(hgrn_s)
        outs["s_rwkv"].append(rwkv_s)
        outs["s_shift"].append(projs[:, P_R:P_R + C_COLS])

        if not routed:
            xp = _ffn(xp, hp, ffn_w_gate[j], ffn_w_up[j], ffn_w_down[j], 512)
            xs = _ffn(xs, hs, ffn_w_gate[j], ffn_w_up[j], ffn_w_down[j], 128)
        else:
            y_moe = _moe(jnp.concatenate([hp, hs], axis=0), jnp.concatenate([lp, ls], axis=0),
                         exp_w_gate[j], exp_w_up[j], exp_w_down[j])
            if layer == depth - 1:
                y_moe_p, y_moe_s = y_moe[:tp], y_moe[tp:]
            else:
                xp, xs = xp + y_moe[:tp], xs + y_moe[tp:]

    zp = y_moe_p if y_moe_p is not None else jnp.zeros_like(xp)
    zs = y_moe_s if y_moe_s is not None else jnp.zeros_like(xs)
    y_prompt = _final_norm(xp, zp, final_norm_g, 512).reshape(bp, seq, D_MODEL)
    y_sample = _final_norm(xs, zs, final_norm_g, 128).reshape(bs, 1, D_MODEL)
    st = {k: jnp.stack(v) for k, v in outs.items()}
    return (y_prompt, y_sample, st["p_ssm"], st["p_conv"], st["p_hgrn"], st["p_rwkv"], st["p_shift"],
            st["s_ssm"], st["s_conv"], st["s_hgrn"], st["s_rwkv"], st["s_shift"])
```

```python
import functools
import math

import jax
import jax.numpy as jnp
from jax import lax
from jax.experimental import pallas as pl
from jax.experimental.pallas import tpu as pltpu

F32 = jnp.float32
BF16 = jnp.bfloat16
HI = lax.Precision.HIGHEST

D_MODEL = 2048
A_DIM, A_HEAD, A_HEADS, A_GROUPS, A_STATE = 1024, 64, 16, 2, 128
A_GN = A_GROUPS * A_STATE
CONV_W = 4
CONV_CH = A_DIM + 2 * A_GN
B_DIM, B_HEAD, B_HEADS = 512, 128, 4
C_DIM, C_HEAD, C_HEADS = 512, 64, 8
C_LR_W, C_LR_A, C_LR_G = 64, 64, 128
C_LR = C_LR_W + C_LR_A + C_LR_G
C_COLS = 3 * C_DIM + C_LR
IN_COLS = A_DIM + CONV_CH + A_HEADS + 4 * B_DIM + C_COLS
D_FF, N_EXPERTS, TOP_K, E_FF = 5504, 8, 2, 7168
NORM_EPS, GN_EPS, L2_EPS = 1e-6, 64e-5, 1e-12

P_Z, P_X, P_B, P_C = 0, 1024, 2048, 2304
P_Q, P_F, P_I, P_G = 2560, 3072, 3584, 4096
P_R, P_K, P_V, P_LR, P_DT, P_COLS = 4608, 5120, 5632, 6144, 6400, 6656
LANES = 128
DT_PAD = P_COLS - P_DT - A_HEADS

SSD_CHUNK = 256
HGRN_BLOCK, HGRN_CHUNK = 256, 16
RWKV_BLOCK, RWKV_CHUNK = 256, 64
STEP_BB = 8

FFN_TF, FFN_TN = 512, 512
MOE_BM, MOE_TF, MOE_TN = 256, 1024, 512

NT_DIMS = (((1,), (1,)), ((), ()))
TN_DIMS = (((0,), (0,)), ((), ()))


def _cparams(sem, vmem_mb=48):
    return pltpu.CompilerParams(dimension_semantics=sem, vmem_limit_bytes=vmem_mb * 1024 * 1024)


def _dot(a, b):
    return jnp.dot(a.astype(BF16), b.astype(BF16), preferred_element_type=F32)


def _dot_nt(a, b):
    return lax.dot_general(a.astype(BF16), b.astype(BF16), NT_DIMS, preferred_element_type=F32)


def _dot_tn(a, b):
    return lax.dot_general(a.astype(BF16), b.astype(BF16), TN_DIMS, preferred_element_type=F32)


def _dot_hi(a, b):
    return jnp.dot(a, b, precision=HI, preferred_element_type=F32)


def _dot_nt_hi(a, b):
    return lax.dot_general(a, b, NT_DIMS, precision=HI, preferred_element_type=F32)


def _dot_tn_hi(a, b):
    return lax.dot_general(a, b, TN_DIMS, precision=HI, preferred_element_type=F32)


def _silu(x):
    return x * jax.nn.sigmoid(x)


def _softplus(x):
    return jnp.maximum(x, 0.0) + jnp.log1p(jnp.exp(-jnp.abs(x)))


def _iota(shape, dim):
    return lax.broadcasted_iota(jnp.int32, shape, dim)


def _rms(x, g):
    return x * lax.rsqrt(jnp.mean(x * x, axis=-1, keepdims=True) + NORM_EPS) * g


def _full(shape):
    nd = len(shape)
    return pl.BlockSpec(shape, lambda *_: (0,) * nd)


def _norm_matmul_kernel(x_ref, g_ref, w_ref, o_ref, h_scr):
    @pl.when(pl.program_id(1) == 0)
    def _():
        h_scr[...] = _rms(x_ref[...], g_ref[...]).astype(BF16)

    o_ref[...] = jnp.dot(h_scr[...], w_ref[...], preferred_element_type=F32)


def _norm_matmul(x, g, w, tm, tn):
    m, k = x.shape
    n = w.shape[1]
    return pl.pallas_call(
        _norm_matmul_kernel,
        grid=(m // tm, n // tn),
        in_specs=[pl.BlockSpec((tm, k), lambda i, j: (i, 0)),
                  pl.BlockSpec((1, k), lambda i, j: (0, 0)),
                  pl.BlockSpec((k, tn), lambda i, j: (0, j))],
        out_specs=pl.BlockSpec((tm, tn), lambda i, j: (i, j)),
        out_shape=jax.ShapeDtypeStruct((m, n), F32),
        scratch_shapes=[pltpu.VMEM((tm, k), BF16)],
        compiler_params=_cparams(("parallel", "arbitrary")),
        name="norm_matmul",
    )(x, g.reshape(1, k), w)


def _out_proj_kernel(routed, ya_ref, yb_ref, yc_ref, w_ref, r_ref, g_ref, *rest):
    x = (r_ref[...]
         + jnp.dot(ya_ref[...], w_ref[0:A_DIM, :], preferred_element_type=F32)
         + jnp.dot(yb_ref[...], w_ref[A_DIM:A_DIM + B_DIM, :], preferred_element_type=F32)
         + jnp.dot(yc_ref[...], w_ref[A_DIM + B_DIM:, :], preferred_element_type=F32))
    h = _rms(x, g_ref[...])
    h_hi = h.astype(BF16)
    if routed:
        rw_ref, x_ref, h_ref, l_ref = rest
        h_lo = (h - h_hi.astype(F32)).astype(BF16)
        rw = rw_ref[...]
        l2 = jnp.dot(h_hi, rw, preferred_element_type=F32)
        l_ref[...] = (l2[:, :LANES] + l2[:, LANES:]
                      + jnp.dot(h_lo, rw[:, :LANES], preferred_element_type=F32))
    else:
        x_ref, h_ref = rest
    x_ref[...] = x
    h_ref[...] = h_hi


def _out_proj(ya, yb, yc, w, res, g, rw, tm):
    m = ya.shape[0]
    n = w.shape[1]
    routed = rw is not None
    row = lambda width: pl.BlockSpec((tm, width), lambda i: (i, 0))
    in_specs = [row(A_DIM), row(B_DIM), row(C_DIM), _full(w.shape), row(n), _full((1, n))]
    out_specs = [row(n), row(n)]
    out_shape = [jax.ShapeDtypeStruct((m, n), F32), jax.ShapeDtypeStruct((m, n), BF16)]
    args = [ya, yb, yc, w, res, g.reshape(1, n)]
    if routed:
        in_specs.append(_full(rw.shape))
        out_specs.append(row(LANES))
        out_shape.append(jax.ShapeDtypeStruct((m, LANES), F32))
        args.append(rw)
    outs = pl.pallas_call(
        functools.partial(_out_proj_kernel, routed),
        grid=(m // tm,),
        in_specs=in_specs, out_specs=out_specs, out_shape=out_shape,
        compiler_params=_cparams(("parallel",)),
        name="out_proj",
    )(*args)
    return outs if routed else (*outs, None)


def _ffn_up_kernel(h_ref, wg_ref, wu_ref, o_ref, wg_s, wu_s):
    @pl.when(pl.program_id(1) == 0)
    def _():
        wg_s[...] = wg_ref[...].astype(BF16)
        wu_s[...] = wu_ref[...].astype(BF16)

    h = h_ref[...]
    a = jnp.dot(h, wg_s[...], preferred_element_type=F32)
    u = jnp.dot(h, wu_s[...], preferred_element_type=F32)
    o_ref[...] = (_silu(a) * u).astype(BF16)


def _ffn_down_kernel(h_ref, wd_ref, x_ref, o_ref, wd_s):
    @pl.when(pl.program_id(1) == 0)
    def _():
        wd_s[...] = wd_ref[...].astype(BF16)

    o_ref[...] = x_ref[...] + jnp.dot(h_ref[...], wd_s[...], preferred_element_type=F32)


def _ffn(x, h, wg, wu, wd, tm):
    m, k = h.shape
    ff = wg.shape[1]
    tf, tn = FFN_TF, FFN_TN
    act = pl.pallas_call(
        _ffn_up_kernel,
        grid=(pl.cdiv(ff, tf), m // tm),
        in_specs=[pl.BlockSpec((tm, k), lambda f, i: (i, 0)),
                  pl.BlockSpec((k, tf), lambda f, i: (0, f)),
                  pl.BlockSpec((k, tf), lambda f, i: (0, f))],
        out_specs=pl.BlockSpec((tm, tf), lambda f, i: (i, f)),
        out_shape=jax.ShapeDtypeStruct((m, ff), BF16),
        scratch_shapes=[pltpu.VMEM((k, tf), BF16)] * 2,
        compiler_params=_cparams(("arbitrary", "arbitrary")),
        name="ffn_up",
    )(h, wg, wu)
    return pl.pallas_call(
        _ffn_down_kernel,
        grid=(k // tn, m // tm),
        in_specs=[pl.BlockSpec((tm, ff), lambda n, i: (i, 0)),
                  pl.BlockSpec((ff, tn), lambda n, i: (0, n)),
                  pl.BlockSpec((tm, tn), lambda n, i: (i, n))],
        out_specs=pl.BlockSpec((tm, tn), lambda n, i: (i, n)),
        out_shape=jax.ShapeDtypeStruct((m, k), F32),
        scratch_shapes=[pltpu.VMEM((ff, tn), BF16)],
        compiler_params=_cparams(("arbitrary", "arbitrary")),
        name="ffn_down",
    )(act, wd, x)


def _moe_up_kernel(bexp_ref, nv_ref, first_ref, x_ref, wg_ref, wu_ref, o_ref, wg_s, wu_s):
    b = pl.program_id(1)

    @pl.when(first_ref[b] == 1)
    def _():
        wg_s[...] = wg_ref[...].astype(BF16)
        wu_s[...] = wu_ref[...].astype(BF16)

    @pl.when(b < nv_ref[0])
    def _():
        x = x_ref[...]
        a = jnp.dot(x, wg_s[...], preferred_element_type=F32)
        u = jnp.dot(x, wu_s[...], preferred_element_type=F32)
        o_ref[...] = (_silu(a) * u).astype(BF16)

    @pl.when(b >= nv_ref[0])
    def _():
        o_ref[...] = jnp.zeros_like(o_ref)


def _moe_down_kernel(bexp_ref, nv_ref, first_ref, h_ref, wd_ref, o_ref, wd_s):
    b = pl.program_id(1)

    @pl.when(first_ref[b] == 1)
    def _():
        wd_s[...] = wd_ref[...].astype(BF16)

    @pl.when(b < nv_ref[0])
    def _():
        o_ref[...] = jnp.dot(h_ref[...], wd_s[...], preferred_element_type=F32)

    @pl.when(b >= nv_ref[0])
    def _():
        o_ref[...] = jnp.zeros_like(o_ref)


def _moe_experts(xb, bexp, nvalid, first, wg, wu, wd):
    n_slots, k = xb.shape
    ff = wg.shape[2]
    nb = n_slots // MOE_BM
    tf, tn = MOE_TF, MOE_TN

    def blk(b, nv):
        return jnp.minimum(b, nv[0] - 1)

    act = pl.pallas_call(
        _moe_up_kernel,
        grid_spec=pltpu.PrefetchScalarGridSpec(
            num_scalar_prefetch=3,
            grid=(ff // tf, nb),
            in_specs=[pl.BlockSpec((MOE_BM, k), lambda f, b, be, nv, fr: (blk(b, nv), 0)),
                      pl.BlockSpec((None, k, tf), lambda f, b, be, nv, fr: (be[b], 0, f)),
                      pl.BlockSpec((None, k, tf), lambda f, b, be, nv, fr: (be[b], 0, f))],
            out_specs=pl.BlockSpec((MOE_BM, tf), lambda f, b, be, nv, fr: (b, f)),
            scratch_shapes=[pltpu.VMEM((k, tf), BF16)] * 2,
        ),
        out_shape=jax.ShapeDtypeStruct((n_slots, ff), BF16),
        compiler_params=_cparams(("arbitrary", "arbitrary"), 56),
        name="moe_up",
    )(bexp, nvalid, first, xb, wg, wu)
    return pl.pallas_call(
        _moe_down_kernel,
        grid_spec=pltpu.PrefetchScalarGridSpec(
            num_scalar_prefetch=3,
            grid=(k // tn, nb),
            in_specs=[pl.BlockSpec((MOE_BM, ff), lambda n, b, be, nv, fr: (blk(b, nv), 0)),
                      pl.BlockSpec((None, ff, tn), lambda n, b, be, nv, fr: (be[b], 0, n))],
            out_specs=pl.BlockSpec((MOE_BM, tn), lambda n, b, be, nv, fr: (b, n)),
            scratch_shapes=[pltpu.VMEM((ff, tn), BF16)],
        ),
        out_shape=jax.ShapeDtypeStruct((n_slots, k), F32),
        compiler_params=_cparams(("arbitrary", "arbitrary"), 56),
        name="moe_down",
    )(bexp, nvalid, first, act, wd)


def _final_norm_kernel(x_ref, y_ref, g_ref, o_ref):
    o_ref[...] = _rms(x_ref[...] + y_ref[...], g_ref[...])


def _final_norm(x, y, g, tm):
    m, k = x.shape
    return pl.pallas_call(
        _final_norm_kernel,
        grid=(m // tm,),
        in_specs=[pl.BlockSpec((tm, k), lambda i: (i, 0)),
                  pl.BlockSpec((tm, k), lambda i: (i, 0)),
                  pl.BlockSpec((1, k), lambda i: (0, 0))],
        out_specs=pl.BlockSpec((tm, k), lambda i: (i, 0)),
        out_shape=jax.ShapeDtypeStruct((m, k), F32),
        compiler_params=_cparams(("parallel",)),
        name="final_norm",
    )(x, y, g.reshape(1, k))


def _ssd_gate_norm(y, z, ng):
    y = y * _silu(z)
    gw = A_DIM // A_GROUPS
    outs = []
    for g in range(A_GROUPS):
        yg = y[:, g * gw:(g + 1) * gw]
        outs.append(yg * lax.rsqrt(jnp.mean(yg * yg, axis=-1, keepdims=True) + NORM_EPS))
    return jnp.concatenate(outs, axis=-1) * ng


def _ssd_prompt_kernel(z_ref, x_ref, b_ref, c_ref, dt_ref, cwx, cwb, cwc, cbx, cbb, cbc,
                       dtb, aneg, dsk, ng, y_ref, st_ref, h_scr, xpx, xpb, xpc, yacc):
    c = pl.program_id(1)
    ch = SSD_CHUNK

    @pl.when(c == 0)
    def _():
        h_scr[...] = jnp.zeros_like(h_scr)
        xpx[0:8, :] = jnp.zeros((8, A_DIM), F32)
        xpb[0:8, :] = jnp.zeros((8, A_STATE * A_GROUPS), F32)
        xpc[0:8, :] = jnp.zeros((8, A_STATE * A_GROUPS), F32)

    xpx[8:8 + ch, :] = x_ref[...]
    xpb[8:8 + ch, :] = b_ref[...]
    xpc[8:8 + ch, :] = c_ref[...]

    def conv(xp, cw, cb):
        acc = cb[...] + xp[pl.ds(8 - (CONV_W - 1), ch), :] * cw[0:1, :]
        for tap in range(1, CONV_W):
            acc = acc + xp[pl.ds(8 - (CONV_W - 1) + tap, ch), :] * cw[tap:tap + 1, :]
        return _silu(acc)

    ux = conv(xpx, cwx, cbx)
    ub = conv(xpb, cwb, cbb)
    uc = conv(xpc, cwc, cbc)
    xpx[0:8, :] = xpx[ch:ch + 8, :]
    xpb[0:8, :] = xpb[ch:ch + 8, :]
    xpc[0:8, :] = xpc[ch:ch + 8, :]

    dt = _softplus(dt_ref[...] + dtb[...])
    la = dt * aneg[...]
    tri = (_iota((ch, ch), 1) <= _iota((ch, ch), 0)).astype(F32)
    cum = _dot_hi(tri, la)
    eye = (_iota((A_HEADS, LANES), 0) == _iota((A_HEADS, LANES), 1)).astype(F32)
    cum_r = _dot_nt_hi(eye, cum)
    dt_r = _dot_nt_hi(eye, dt)
    cl = cum[ch - 1:ch, :]
    tail = jnp.exp(cl - cum) * dt
    ecum = jnp.exp(cum)
    ecl = jnp.exp(cl)
    causal = _iota((ch, ch), 1) <= _iota((ch, ch), 0)
    rep = A_HEADS // A_GROUPS

    for g in range(A_GROUPS):
        bg = ub[:, g * A_STATE:(g + 1) * A_STATE].astype(BF16)
        cg = uc[:, g * A_STATE:(g + 1) * A_STATE].astype(BF16)
        cb_g = _dot_nt(cg, bg)
        for h in range(g * rep, (g + 1) * rep):
            hs = slice(h * A_HEAD, (h + 1) * A_HEAD)
            seg = cum[:, h:h + 1] - cum_r[h:h + 1, :]
            w = jnp.exp(jnp.where(causal, seg, -jnp.inf)) * cb_g * dt_r[h:h + 1, :]
            xh = ux[:, hs]
            hst = h_scr[h]
            y_h = _dot(w, xh) + _dot_nt(cg, hst) * ecum[:, h:h + 1]
            yacc[:, hs] = y_h
            h_scr[h] = hst * ecl[:, h:h + 1] + _dot_tn(xh * tail[:, h:h + 1], bg)

    y = yacc[...] + dsk[...] * ux
    y_ref[...] = _ssd_gate_norm(y, z_ref[...], ng[...]).astype(BF16)

    @pl.when(c == pl.num_programs(1) - 1)
    def _():
        st_ref[...] = h_scr[...]


def _ssd_params(p):
    cw, cb = p["conv_w"], p["conv_b"].reshape(1, CONV_CH)
    pad = LANES - A_HEADS
    return dict(
        cwx=cw[:, :A_DIM], cwb=cw[:, A_DIM:A_DIM + A_GN], cwc=cw[:, A_DIM + A_GN:],
        cbx=cb[:, :A_DIM], cbb=cb[:, A_DIM:A_DIM + A_GN], cbc=cb[:, A_DIM + A_GN:],
        dtb=jnp.pad(p["dt_bias"], (0, pad)).reshape(1, LANES),
        aneg=jnp.pad(-jnp.exp(p["a_log"]), (0, pad)).reshape(1, LANES),
        dsk=jnp.repeat(p["d_skip"], A_HEAD).reshape(1, A_DIM),
        ng=p["ssm_norm_g"].reshape(1, A_DIM),
    )


def _ssd_prompt(proj, sp):
    bsz, seq, _ = proj.shape
    ch = SSD_CHUNK
    gn = A_GN

    def col(width, off):
        return pl.BlockSpec((None, ch, width), lambda b, c: (b, c, off // width))

    names = ("cwx", "cwb", "cwc", "cbx", "cbb", "cbc", "dtb", "aneg", "dsk", "ng")
    y, st = pl.pallas_call(
        _ssd_prompt_kernel,
        grid=(bsz, seq // ch),
        in_specs=[col(A_DIM, P_Z), col(A_DIM, P_X), col(gn, P_B), col(gn, P_C), col(LANES, P_DT)]
                 + [_full(sp[n].shape) for n in names],
        out_specs=[pl.BlockSpec((None, ch, A_DIM), lambda b, c: (b, c, 0)),
                   pl.BlockSpec((None, A_HEADS, A_HEAD, A_STATE), lambda b, c: (b, 0, 0, 0))],
        out_shape=[jax.ShapeDtypeStruct((bsz, seq, A_DIM), BF16),
                   jax.ShapeDtypeStruct((bsz, A_HEADS, A_HEAD, A_STATE), F32)],
        scratch_shapes=[pltpu.VMEM((A_HEADS, A_HEAD, A_STATE), F32),
                        pltpu.VMEM((ch + 8, A_DIM), F32),
                        pltpu.VMEM((ch + 8, gn), F32),
                        pltpu.VMEM((ch + 8, gn), F32),
                        pltpu.VMEM((ch, A_DIM), F32)],
        compiler_params=_cparams(("parallel", "arbitrary")),
        name="ssd_prompt",
    )(proj, proj, proj, proj, proj, *[sp[n] for n in names])
    return y, st


def _ssd_step_kernel(z_ref, x_ref, b_ref, c_ref, dt_ref, cs_ref, cw, cb, dtb, anegx, expand, dsk, ng,
                     h_ref, y_ref, ho_ref, y_scr):
    bb = STEP_BB
    xbc = (x_ref[...], b_ref[...], c_ref[...])
    offs = (0, A_DIM, A_DIM + A_GN, CONV_CH)
    u = []
    for i in range(3):
        sl = slice(offs[i], offs[i + 1])
        acc = cb[:, sl] + xbc[i] * cw[CONV_W - 1:CONV_W, sl]
        for tap in range(CONV_W - 1):
            acc = acc + cs_ref[tap][:, sl] * cw[tap:tap + 1, sl]
        u.append(_silu(acc))
    ux, ub, uc = u
    dt = _softplus(dt_ref[...] + dtb[...])
    dtx = _dot_hi(dt, expand[...])
    dec_t = jnp.exp(dtx * anegx[...]).T
    xdt_t = (ux * dtx).T
    gw = A_DIM // A_GROUPS
    for b in range(bb):
        for g in range(A_GROUPS):
            rs = slice(g * gw, (g + 1) * gw)
            ns = slice(g * A_STATE, (g + 1) * A_STATE)
            hg = h_ref[b, rs, :] * dec_t[rs, b:b + 1] + xdt_t[rs, b:b + 1] * ub[b:b + 1, ns]
            ho_ref[b, rs, :] = hg
            y_scr[b:b + 1, rs] = _dot_nt(uc[:, ns], hg)[b:b + 1, :]
    y = y_scr[...] + dsk[...] * ux
    y_ref[...] = _ssd_gate_norm(y, z_ref[...], ng[...]).astype(BF16)


def _ssd_step(proj, conv_state, h0, p, sp):
    bsz = proj.shape[0]
    bb = STEP_BB
    cs = jnp.swapaxes(conv_state, 0, 1)
    hflat = h0.reshape(bsz, A_DIM, A_STATE)
    expand = (jnp.arange(LANES)[:, None] == (jnp.arange(A_DIM) // A_HEAD)[None, :]).astype(F32)
    anegx = jnp.repeat(-jnp.exp(p["a_log"]), A_HEAD).reshape(1, A_DIM)
    cw, cb = p["conv_w"], p["conv_b"].reshape(1, CONV_CH)

    def col(width, off):
        return pl.BlockSpec((bb, width), lambda i: (i, off // width))

    y, hn = pl.pallas_call(
        _ssd_step_kernel,
        grid=(bsz // bb,),
        in_specs=[col(A_DIM, P_Z), col(A_DIM, P_X), col(A_GN, P_B), col(A_GN, P_C), col(LANES, P_DT),
                  pl.BlockSpec((CONV_W - 1, bb, CONV_CH), lambda i: (0, i, 0)),
                  _full(cw.shape), _full(cb.shape), _full(sp["dtb"].shape), _full(anegx.shape),
                  _full(expand.shape), _full(sp["dsk"].shape), _full(sp["ng"].shape),
                  pl.BlockSpec((bb, A_DIM, A_STATE), lambda i: (i, 0, 0))],
        out_specs=[pl.BlockSpec((bb, A_DIM), lambda i: (i, 0)),
                   pl.BlockSpec((bb, A_DIM, A_STATE), lambda i: (i, 0, 0))],
        out_shape=[jax.ShapeDtypeStruct((bsz, A_DIM), BF16),
                   jax.ShapeDtypeStruct((bsz, A_DIM, A_STATE), F32)],
        scratch_shapes=[pltpu.VMEM((bb, A_DIM), F32)],
        compiler_params=_cparams(("parallel",)),
        name="ssd_step",
    )(proj, proj, proj, proj, proj, cs, cw, cb, sp["dtb"], anegx, expand, sp["dsk"], sp["ng"], hflat)
    return y, hn.reshape(h0.shape)


def _hgrn_gates(q, f, lbp):
    qf = _silu(q)
    kf = lbp[0:1, :] * jax.nn.sigmoid(-f)
    log_sig = jnp.minimum(f, 0.0) - jnp.log1p(jnp.exp(-jnp.abs(f)))
    a = lbp[1:2, :]
    b = lbp[2:3, :] + log_sig
    logf = jnp.maximum(a, b) + jnp.log1p(jnp.exp(-jnp.abs(a - b)))
    return qf, kf, logf


def _hgrn_out(o, g, ng):
    outs = []
    for h in range(B_HEADS):
        oh = o[:, h * B_HEAD:(h + 1) * B_HEAD]
        outs.append(oh * lax.rsqrt(jnp.mean(oh * oh, axis=-1, keepdims=True) + NORM_EPS) * ng)
    return jnp.concatenate(outs, axis=-1) * _silu(g)


def _hgrn_prompt_kernel(q_ref, f_ref, i_ref, g_ref, lbp, ng, o_ref, st_ref,
                        s_scr, cum_scr, q_scr, k_scr, o_scr):
    c = pl.program_id(1)
    tb, cs = HGRN_BLOCK, HGRN_CHUNK

    @pl.when(c == 0)
    def _():
        s_scr[...] = jnp.zeros_like(s_scr)

    qf, kf, logf = _hgrn_gates(q_ref[...], f_ref[...], lbp[...])
    r_i, c_i = _iota((tb, tb), 0), _iota((tb, tb), 1)
    bd = ((r_i // cs == c_i // cs) & (c_i <= r_i)).astype(F32)
    cum_scr[...] = _dot_hi(bd, logf)
    q_scr[...] = qf
    k_scr[...] = kf
    rows = _iota((cs, B_HEAD), 0)

    def chunk(cc, carry):
        r0 = pl.multiple_of(cc * cs, cs)
        for h in range(B_HEADS):
            hs = slice(h * B_HEAD, (h + 1) * B_HEAD)
            cu = cum_scr[pl.ds(r0, cs), hs]
            q = q_scr[pl.ds(r0, cs), hs]
            k = k_scr[pl.ds(r0, cs), hs]
            v = i_ref[pl.ds(r0, cs), hs]
            st = s_scr[h]
            o = _dot_nt(q * jnp.exp(cu), st)
            for s in range(cs):
                d = jnp.exp(jnp.where(rows >= s, cu - cu[s:s + 1, :], -jnp.inf))
                att = jnp.sum(q * k[s:s + 1, :] * d, axis=-1, keepdims=True)
                o = o + att * v[s:s + 1, :]
            o_scr[pl.ds(r0, cs), hs] = o
            cl = cu[cs - 1:cs, :]
            s_scr[h] = st * jnp.exp(cl) + _dot_tn(v, k * jnp.exp(cl - cu))
        return carry

    lax.fori_loop(0, tb // cs, chunk, 0)
    o_ref[...] = _hgrn_out(o_scr[...], g_ref[...], ng[...]).astype(BF16)

    @pl.when(c == pl.num_programs(1) - 1)
    def _():
        for h in range(B_HEADS):
            st_ref[h] = s_scr[h].T


def _hgrn_prompt(proj, lbp, ng):
    bsz, seq, _ = proj.shape
    tb = HGRN_BLOCK

    def col(off):
        return pl.BlockSpec((None, tb, B_DIM), lambda b, c: (b, c, off // B_DIM))

    o, st = pl.pallas_call(
        _hgrn_prompt_kernel,
        grid=(bsz, seq // tb),
        in_specs=[col(P_Q), col(P_F), col(P_I), col(P_G), _full(lbp.shape), _full(ng.shape)],
        out_specs=[pl.BlockSpec((None, tb, B_DIM), lambda b, c: (b, c, 0)),
                   pl.BlockSpec((None, B_HEADS, B_HEAD, B_HEAD), lambda b, c: (b, 0, 0, 0))],
        out_shape=[jax.ShapeDtypeStruct((bsz, seq, B_DIM), BF16),
                   jax.ShapeDtypeStruct((bsz, B_HEADS, B_HEAD, B_HEAD), F32)],
        scratch_shapes=[pltpu.VMEM((B_HEADS, B_HEAD, B_HEAD), F32)]
                       + [pltpu.VMEM((tb, B_DIM), F32)] * 4,
        compiler_params=_cparams(("parallel", "arbitrary")),
        name="hgrn_prompt",
    )(proj, proj, proj, proj, lbp, ng)
    return o, st


def _hgrn_step_kernel(q_ref, f_ref, i_ref, g_ref, lbp, ng, s_ref, o_ref, so_ref, o_scr):
    bb = STEP_BB
    qf, kf, logf = _hgrn_gates(q_ref[...], f_ref[...], lbp[...])
    vf = i_ref[...]
    dec_t = jnp.exp(logf).T
    k_t = kf.T
    q_t = qf.T
    for b in range(bb):
        for h in range(B_HEADS):
            hs = slice(h * B_HEAD, (h + 1) * B_HEAD)
            sh = s_ref[b, hs, :] * dec_t[hs, b:b + 1] + k_t[hs, b:b + 1] * vf[b:b + 1, hs]
            so_ref[b, hs, :] = sh
            o_scr[b:b + 1, hs] = jnp.sum(q_t[hs, b:b + 1] * sh, axis=0, keepdims=True)
    o_ref[...] = _hgrn_out(o_scr[...], g_ref[...], ng[...]).astype(BF16)


def _hgrn_step(proj, s0, lbp, ng):
    bsz = proj.shape[0]
    bb = STEP_BB
    sflat = s0.reshape(bsz, B_DIM, B_HEAD)

    def col(off):
        return pl.BlockSpec((bb, B_DIM), lambda i: (i, off // B_DIM))

    o, sn = pl.pallas_call(
        _hgrn_step_kernel,
        grid=(bsz // bb,),
        in_specs=[col(P_Q), col(P_F), col(P_I), col(P_G), _full(lbp.shape), _full(ng.shape),
                  pl.BlockSpec((bb, B_DIM, B_HEAD), lambda i: (i, 0, 0))],
        out_specs=[pl.BlockSpec((bb, B_DIM), lambda i: (i, 0)),
                   pl.BlockSpec((bb, B_DIM, B_HEAD), lambda i: (i, 0, 0))],
        out_shape=[jax.ShapeDtypeStruct((bsz, B_DIM), BF16),
                   jax.ShapeDtypeStruct((bsz, B_DIM, B_HEAD), F32)],
        scratch_shapes=[pltpu.VMEM((bb, B_DIM), F32)],
        compiler_params=_cparams(("parallel",)),
        name="hgrn_step",
    )(proj, proj, proj, proj, lbp, ng, sflat)
    return o, sn.reshape(s0.shape)


RW_NAMES = ("mu_r", "mu_k", "mu_v", "mu_lr", "w0", "w2", "a0", "a2", "g2", "k_k", "k_a", "r_k")


def _rwkv_params(p):
    mu = p["shift_mu"].reshape(1, C_COLS)
    row = lambda a: a.reshape(1, C_DIM)
    return dict(
        mu_r=mu[:, :C_DIM], mu_k=mu[:, C_DIM:2 * C_DIM], mu_v=mu[:, 2 * C_DIM:3 * C_DIM],
        mu_lr=mu[:, 3 * C_DIM:],
        w0=row(p["w0"]), w2=p["w2"].astype(BF16), a0=row(p["a0"]), a2=p["a2"].astype(BF16),
        g2=p["g2"].astype(BF16), k_k=row(p["k_k"]), k_a=row(p["k_a"]), r_k=row(p["r_k"]),
        lnx_w=row(p["lnx_w"]), lnx_b=row(p["lnx_b"]),
    )


def _head_sum(x):
    return [jnp.sum(x[:, h * C_HEAD:(h + 1) * C_HEAD], axis=-1, keepdims=True) for h in range(C_HEADS)]


def _head_bcast(cols, fn=lambda c: c):
    rows = cols[0].shape[0]
    return jnp.concatenate([jnp.broadcast_to(fn(c), (rows, C_HEAD)) for c in cols], axis=-1)


def _rwkv_prep(r, k, v, lr, pr, pk, pv, plr, prm):
    mu_r, mu_k, mu_v, mu_lr, w0, w2, a0, a2, g2, k_k, k_a, r_k = prm
    r = r + (pr - r) * mu_r
    k = k + (pk - k) * mu_k
    v = v + (pv - v) * mu_v
    lr = lr + (plr - lr) * mu_lr
    wl, al, gl = lr[:, :C_LR_W], lr[:, C_LR_W:C_LR_W + C_LR_A], lr[:, C_LR_W + C_LR_A:]
    wraw = -_softplus(-(w0 + _dot(jnp.tanh(wl), w2))) - 0.5
    ld = -jnp.exp(wraw)
    a = jax.nn.sigmoid(a0 + _dot(al, a2))
    g = _dot(jax.nn.sigmoid(gl), g2)
    kk = k * k_k
    kk = kk * _head_bcast(_head_sum(kk * kk), lambda n2: 1.0 / jnp.maximum(jnp.sqrt(n2), L2_EPS))
    k2 = k * (1.0 + (a - 1.0) * k_a)
    bonus = _head_bcast(_head_sum(r * k2 * r_k)) * v
    return r, k2, v, kk, kk * a, ld, g, bonus


def _rwkv_out(y, bonus, g, lnx_w, lnx_b):
    n = float(C_HEAD)
    mean = _head_bcast(_head_sum(y), lambda s: s / n)
    yc = y - mean
    rstd = _head_bcast(_head_sum(yc * yc), lambda s: lax.rsqrt(s / n + GN_EPS))
    return (yc * rstd * lnx_w + lnx_b + bonus) * g


RW_GH = 4
RW_GW = RW_GH * C_HEAD


def _bd_mask():
    return _iota((RW_GW, RW_GW), 0) // C_HEAD == _iota((RW_GW, RW_GW), 1) // C_HEAD


def _bd_rows(a):
    return jnp.where(_bd_mask(), jnp.concatenate([a] * RW_GH, axis=0), 0.0)


def _bd_lanes(s):
    return jnp.where(_bd_mask(), jnp.concatenate([s] * RW_GH, axis=1), 0.0)


def _bd_fold(m):
    m = jnp.where(_bd_mask(), m, 0.0)
    out = m[:, :C_HEAD]
    for h in range(1, RW_GH):
        out = out + m[:, h * C_HEAD:(h + 1) * C_HEAD]
    return out


def _to_stack(a):
    return jnp.concatenate([a[:, h * C_HEAD:(h + 1) * C_HEAD] for h in range(RW_GH)], axis=0)


def _to_all(s):
    return jnp.concatenate([s[h * C_HEAD:(h + 1) * C_HEAD, :] for h in range(RW_GH)], axis=1)


def _rwkv_a_kernel(r_ref, k_ref, v_ref, lr_ref, rp_ref, kp_ref, vp_ref, lrp_ref, *rest):
    prm = tuple(x[...] for x in rest[:len(RW_NAMES)])
    (kd_ref, rd_ref, bg_ref, ge_ref, bonus_ref, gate_ref, aab_ref, rb_ref, m2_ref, rkv_ref, vk_ref,
     xs_scr, b1_scr, b2_scr, b3_scr, b4_scr) = rest[len(RW_NAMES):]
    c = pl.program_id(1)
    tb, cs = RWKV_BLOCK, RWKV_CHUNK
    first = (c == 0)

    def prev(cur_ref, prev_ref, width):
        xs_scr[8:8 + tb, 0:width] = cur_ref[...]
        xs_scr[7:8, 0:width] = jnp.where(first, 0.0, prev_ref[7:8, :])
        return xs_scr[pl.ds(7, tb), 0:width]

    pr = prev(r_ref, rp_ref, C_DIM)
    pk = prev(k_ref, kp_ref, C_DIM)
    pv = prev(v_ref, vp_ref, C_DIM)
    plr = prev(lr_ref, lrp_ref, C_LR)
    r, k2, v, kk, beta, ld, g, bonus = _rwkv_prep(
        r_ref[...], k_ref[...], v_ref[...], lr_ref[...], pr, pk, pv, plr, prm)
    bonus_ref[...] = bonus
    gate_ref[...] = g

    r_i, c_i = _iota((tb, tb), 0), _iota((tb, tb), 1)
    same = (r_i // cs == c_i // cs)
    lcum = _dot_hi((same & (c_i <= r_i)).astype(F32), ld)
    lend = _dot_hi(same.astype(F32), ld)
    einv = jnp.exp(-lcum)
    eend = jnp.exp(lend - lcum)
    ge_ref[...] = jnp.exp(lend)
    kd_ref[...] = kk * jnp.exp(lcum - ld)
    rd_ref[...] = r * jnp.exp(lcum)
    bg_ref[...] = beta * eend
    b1_scr[...] = beta * einv
    b2_scr[...] = k2 * einv
    b3_scr[...] = k2 * eend
    b4_scr[...] = v
    gw = RW_GW
    tpos, spos = _iota((gw, cs), 0) % cs, _iota((gw, cs), 1)
    strict, incl = tpos > spos, tpos >= spos
    hc = C_HEAD

    def chunk(j, carry):
        r0 = pl.multiple_of(j * cs, cs)
        q0 = pl.multiple_of(j * gw, gw)
        for g in range(C_HEADS // RW_GH):
            gs = slice(g * gw, (g + 1) * gw)
            ls = slice(g * hc, (g + 1) * hc)
            kd = kd_ref[pl.ds(r0, cs), gs]
            rd = rd_ref[pl.ds(r0, cs), gs]
            v = b4_scr[pl.ds(r0, cs), gs]
            lhs = jnp.concatenate([_bd_rows(kd), _bd_rows(rd)], axis=0)
            rhs = jnp.concatenate([b1_scr[pl.ds(r0, cs), gs], b2_scr[pl.ds(r0, cs), gs]], axis=0)
            gm = _dot_nt(lhs, rhs)
            aab_ref[pl.ds(q0, gw), ls] = jnp.where(strict, gm[:gw, :hc], 0.0)
            aak = jnp.where(strict, gm[:gw, hc:], 0.0)
            rb_ref[pl.ds(q0, gw), ls] = jnp.where(incl, gm[gw:, :hc], 0.0)
            rk = jnp.where(incl, gm[gw:, hc:], 0.0)
            mr = _dot(jnp.concatenate([_bd_lanes(aak), _bd_lanes(rk)], axis=0), _to_stack(v))
            m2_ref[pl.ds(q0, gw), ls] = mr[:gw]
            rkv_ref[pl.ds(q0, gw), ls] = mr[gw:]
            vk_ref[pl.ds(q0, gw), ls] = _bd_fold(_dot_tn(v, b3_scr[pl.ds(r0, cs), gs]))
        return carry

    lax.fori_loop(0, tb // cs, chunk, 0)


def _tri_solve_kernel(a_ref, t_ref):
    n, _, npb = a_ref.shape
    rowj = _iota((n, npb), 0)

    def row(t, carry):
        def inner(s, acc):
            return acc - a_ref[t, pl.ds(s, 1), :] * t_ref[s]

        t_ref[t] = lax.fori_loop(0, t, inner, (rowj == t).astype(F32))
        return carry

    lax.fori_loop(0, n, row, 0)


def _tri_solve(a):
    n, _, nprob = a.shape
    npb = min(256, nprob)
    return pl.pallas_call(
        _tri_solve_kernel,
        grid=(nprob // npb,),
        in_specs=[pl.BlockSpec((n, n, npb), lambda i: (0, 0, i))],
        out_specs=pl.BlockSpec((n, n, npb), lambda i: (0, 0, i)),
        out_shape=jax.ShapeDtypeStruct(a.shape, F32),
        compiler_params=_cparams(("parallel",)),
        name="tri_solve",
    )(a)


def _rwkv_c_kernel(t_ref, kd_ref, rd_ref, bg_ref, ge_ref, bonus_ref, gate_ref, rb_ref, m2_ref, rkv_ref, vk_ref,
                   lnw, lnb, y_ref, st_ref, s_scr, y_scr):
    c = pl.program_id(1)
    tb, cs, gw, hc = RWKV_BLOCK, RWKV_CHUNK, RW_GW, C_HEAD

    @pl.when(c == 0)
    def _():
        s_scr[...] = jnp.zeros_like(s_scr)

    def chunk(j, carry):
        r0 = pl.multiple_of(j * cs, cs)
        q0 = pl.multiple_of(j * gw, gw)
        for g in range(C_HEADS // RW_GH):
            gs = slice(g * gw, (g + 1) * gw)
            ls = slice(g * hc, (g + 1) * hc)
            rhs = jnp.concatenate([_to_stack(kd_ref[pl.ds(r0, cs), gs]), m2_ref[pl.ds(q0, gw), ls]], axis=1)
            wu = _dot(_bd_lanes(t_ref[pl.ds(q0, gw), ls]), rhs)
            rbwu = _dot(_bd_lanes(rb_ref[pl.ds(q0, gw), ls]), wu)
            rt = _to_stack(rd_ref[pl.ds(r0, cs), gs]) - rbwu[:, :hc]
            yc = rkv_ref[pl.ds(q0, gw), ls] - rbwu[:, hc:]
            wu_bd = jnp.concatenate([_bd_lanes(wu[:, :hc]), _bd_lanes(wu[:, hc:])], axis=1)
            nu = _dot_tn(wu_bd, _to_stack(bg_ref[pl.ds(r0, cs), gs]))
            ge_row = ge_ref[pl.ds(r0, 1), gs]
            ge_st = jnp.concatenate(
                [jnp.broadcast_to(ge_row[:, h * hc:(h + 1) * hc], (hc, hc)) for h in range(RW_GH)], axis=0)
            s = s_scr[g]
            s_bd = _bd_lanes(s)
            y_scr[pl.ds(r0, cs), gs] = _dot_nt(_to_all(rt), s_bd) + _to_all(yc)
            s_scr[g] = s * ge_st - _dot(s_bd, nu[:gw]) + (vk_ref[pl.ds(q0, gw), ls] - nu[gw:])
        return carry

    lax.fori_loop(0, tb // cs, chunk, 0)
    y_ref[...] = _rwkv_out(y_scr[...], bonus_ref[...], gate_ref[...], lnw[...], lnb[...]).astype(BF16)

    @pl.when(c == pl.num_programs(1) - 1)
    def _():
        st_ref[...] = s_scr[...]


def _rwkv_prompt(proj, rp):
    bsz, seq, _ = proj.shape
    tb, cs = RWKV_BLOCK, RWKV_CHUNK
    nblk = seq // tb

    def col(width, off):
        return pl.BlockSpec((None, tb, width), lambda b, c: (b, c, off // width))

    def pcol(width, off):
        return pl.BlockSpec((None, 8, width), lambda b, c: (b, jnp.maximum(c * (tb // 8) - 1, 0), off // width))

    ng = C_HEADS // RW_GH
    srows = tb // cs * RW_GW
    tok = pl.BlockSpec((None, tb, C_DIM), lambda b, c: (b, c, 0))
    stk = pl.BlockSpec((None, srows, ng * C_HEAD), lambda b, c: (b, c, 0))
    tok_shape = jax.ShapeDtypeStruct((bsz, seq, C_DIM), F32)
    stk_shape = jax.ShapeDtypeStruct((bsz, nblk * srows, ng * C_HEAD), F32)
    prm = [rp[n] for n in RW_NAMES]
    outs = pl.pallas_call(
        _rwkv_a_kernel,
        grid=(bsz, nblk),
        in_specs=[col(C_DIM, P_R), col(C_DIM, P_K), col(C_DIM, P_V), col(C_LR, P_LR),
                  pcol(C_DIM, P_R), pcol(C_DIM, P_K), pcol(C_DIM, P_V), pcol(C_LR, P_LR)]
                 + [_full(x.shape) for x in prm],
        out_specs=[tok] * 6 + [stk] * 5,
        out_shape=[tok_shape] * 6 + [stk_shape] * 5,
        scratch_shapes=[pltpu.VMEM((tb + 8, C_DIM), F32)] + [pltpu.VMEM((tb, C_DIM), F32)] * 4,
        compiler_params=_cparams(("parallel", "parallel")),
        name="rwkv_prep",
    )(proj, proj, proj, proj, proj, proj, proj, proj, *prm)
    kd, rd, bg, ge, bonus, gate, aab, rb, m2, rkv, vk = outs

    nch = seq // cs
    a_t = aab.reshape(bsz, nch, RW_GH, cs, ng, cs).transpose(3, 5, 0, 1, 4, 2).reshape(cs, cs, -1)
    t_t = _tri_solve(a_t)
    tmat = t_t.reshape(cs, cs, bsz, nch, ng, RW_GH).transpose(2, 3, 5, 0, 4, 1).reshape(stk_shape.shape)

    y, st = pl.pallas_call(
        _rwkv_c_kernel,
        grid=(bsz, nblk),
        in_specs=[stk] + [tok] * 6 + [stk] * 4 + [_full((1, C_DIM))] * 2,
        out_specs=[tok, pl.BlockSpec((None, ng, RW_GW, C_HEAD), lambda b, c: (b, 0, 0, 0))],
        out_shape=[jax.ShapeDtypeStruct((bsz, seq, C_DIM), BF16),
                   jax.ShapeDtypeStruct((bsz, ng, RW_GW, C_HEAD), F32)],
        scratch_shapes=[pltpu.VMEM((ng, RW_GW, C_HEAD), F32), pltpu.VMEM((tb, C_DIM), F32)],
        compiler_params=_cparams(("parallel", "arbitrary")),
        name="rwkv_scan",
    )(tmat, kd, rd, bg, ge, bonus, gate, rb, m2, rkv, vk, rp["lnx_w"], rp["lnx_b"])
    return y, st.reshape(bsz, C_HEADS, C_HEAD, C_HEAD)


def _rwkv_step_kernel(r_ref, k_ref, v_ref, lr_ref, sh_ref, *rest):
    prm = tuple(x[...] for x in rest[:len(RW_NAMES)])
    lnw, lnb, s_ref, y_ref, so_ref = rest[len(RW_NAMES):]
    bb = STEP_BB
    sh = sh_ref[...]
    r, k2, v, kk, beta, ld, g, bonus = _rwkv_prep(
        r_ref[...], k_ref[...], v_ref[...], lr_ref[...],
        sh[:, :C_DIM], sh[:, C_DIM:2 * C_DIM], sh[:, 2 * C_DIM:3 * C_DIM], sh[:, 3 * C_DIM:], prm)
    w = jnp.exp(ld)
    v_t = v.T
    lane = _iota((C_DIM, LANES), 1)
    heads = [[x[:, h * C_HEAD:(h + 1) * C_HEAD] for x in (kk, w, beta, k2, r)] for h in range(C_HEADS)]

    def rows(i, b):
        return jnp.concatenate(
            [jnp.broadcast_to(heads[h][i][b:b + 1, :], (C_HEAD, C_HEAD)) for h in range(C_HEADS)], axis=0)

    y_t = jnp.zeros((C_DIM, LANES), F32)
    for b in range(bb):
        s = s_ref[b]
        sa = -jnp.sum(s * rows(0, b), axis=-1, keepdims=True)
        sn = s * rows(1, b) + sa * rows(2, b) + v_t[:, b:b + 1] * rows(3, b)
        so_ref[b] = sn
        y_t = jnp.where(lane == b, jnp.sum(sn * rows(4, b), axis=-1, keepdims=True), y_t)
    y = y_t.T[0:bb, :]
    y_ref[...] = _rwkv_out(y, bonus, g, lnw[...], lnb[...]).astype(BF16)


def _rwkv_step(proj, shift, s0, rp):
    bsz = proj.shape[0]
    bb = STEP_BB
    sflat = s0.reshape(bsz, C_DIM, C_HEAD)

    def col(width, off):
        return pl.BlockSpec((bb, width), lambda i: (i, off // width))

    prm = [rp[n] for n in RW_NAMES]
    y, sn = pl.pallas_call(
        _rwkv_step_kernel,
        grid=(bsz // bb,),
        in_specs=[col(C_DIM, P_R), col(C_DIM, P_K), col(C_DIM, P_V), col(C_LR, P_LR),
                  pl.BlockSpec((bb, C_COLS), lambda i: (i, 0))]
                 + [_full(x.shape) for x in prm] + [_full((1, C_DIM))] * 2
                 + [pl.BlockSpec((bb, C_DIM, C_HEAD), lambda i: (i, 0, 0))],
        out_specs=[pl.BlockSpec((bb, C_DIM), lambda i: (i, 0)),
                   pl.BlockSpec((bb, C_DIM, C_HEAD), lambda i: (i, 0, 0))],
        out_shape=[jax.ShapeDtypeStruct((bsz, C_DIM), BF16),
                   jax.ShapeDtypeStruct((bsz, C_DIM, C_HEAD), F32)],
        compiler_params=_cparams(("parallel",)),
        name="rwkv_step",
    )(proj, proj, proj, proj, shift, *prm, rp["lnx_w"], rp["lnx_b"], sflat)
    return y, sn.reshape(s0.shape)


def _layout_w_in(w):
    o = A_DIM + CONV_CH
    wb = w.astype(BF16)
    pad = jnp.zeros((w.shape[0], DT_PAD), BF16)
    return jnp.concatenate([wb[:, :o], wb[:, o + A_HEADS:], wb[:, o:o + A_HEADS], pad], axis=1)


def _xbc_cols(proj):
    return proj[..., P_X:P_X + CONV_CH]


def _moe_route(logits, n_tok):
    top_logit, top_idx = lax.top_k(logits, TOP_K)
    gate = jax.nn.softmax(top_logit, axis=-1)
    n_assign = n_tok * TOP_K
    flat_e = top_idx.reshape(-1).astype(jnp.int32)
    order = jnp.argsort(flat_e)
    onehot = jax.nn.one_hot(flat_e, N_EXPERTS, dtype=jnp.int32)
    counts = jnp.sum(onehot, axis=0)
    padded = (counts + MOE_BM - 1) // MOE_BM * MOE_BM
    pad_end = jnp.cumsum(padded)
    pad_start = pad_end - padded
    start = jnp.cumsum(counts) - counts
    rank = jnp.sum((jnp.cumsum(onehot, axis=0) - onehot) * onehot, axis=1)
    slot_of = (pad_start[flat_e] + rank).reshape(n_tok, TOP_K)
    n_blocks = -(-n_assign // MOE_BM) + N_EXPERTS
    n_slots = n_blocks * MOE_BM
    block_start = jnp.arange(n_blocks, dtype=jnp.int32) * MOE_BM
    nvalid = (pad_end[-1] // MOE_BM).astype(jnp.int32)
    bexp = jnp.minimum(jnp.searchsorted(pad_end, block_start, side="right"), N_EXPERTS - 1).astype(jnp.int32)
    slot_e = jnp.repeat(bexp, MOE_BM)
    within = jnp.arange(n_slots, dtype=jnp.int32) - pad_start[slot_e]
    src = order[jnp.clip(start[slot_e] + within, 0, n_assign - 1)] // TOP_K
    slot_tok = jnp.where(within < counts[slot_e], src, n_tok)
    last = bexp[jnp.maximum(nvalid - 1, 0)]
    bexp = jnp.where(jnp.arange(n_blocks) < nvalid, bexp, last)
    first = jnp.concatenate([jnp.ones((1,), jnp.int32), (bexp[1:] != bexp[:-1]).astype(jnp.int32)])
    return gate, slot_tok, slot_of, bexp, nvalid.reshape(1), first


def _moe(h, logits, wg, wu, wd):
    n_tok = h.shape[0]
    gate, slot_tok, slot_of, bexp, nvalid, first = _moe_route(logits[:, :N_EXPERTS], n_tok)
    h_pad = jnp.concatenate([h, jnp.zeros((1, h.shape[1]), h.dtype)], axis=0)
    xb = h_pad[slot_tok]
    yb = _moe_experts(xb, bexp, nvalid, first, wg, wu, wd)
    return gate[:, 0:1] * yb[slot_of[:, 0]] + gate[:, 1:2] * yb[slot_of[:, 1]]


def kernel(x_prompt, x_sample, state_ssm, state_conv, state_hgrn, state_rwkv, state_shift, norm1_g, w_in, conv_w, conv_b, dt_bias, a_log, d_skip, ssm_norm_g, lb_logits, hgrn_norm_g, shift_mu, w0, w2, a0, a2, g2, k_k, k_a, r_k, lnx_w, lnx_b, w_out, norm2_g, ffn_w_gate, ffn_w_up, ffn_w_down, router_w, exp_w_gate, exp_w_up, exp_w_down, final_norm_g):
    depth = w_in.shape[0]
    bp, seq, _ = x_prompt.shape
    bs = x_sample.shape[0]
    tp = bp * seq
    lb_soft = jax.nn.softmax(lb_logits.astype(F32), axis=0)
    lower_bounds = jnp.clip(jnp.cumsum(lb_soft, axis=0) - lb_soft[0:1], 0.0, 1.0)

    xp = x_prompt.reshape(tp, D_MODEL)
    xs = x_sample.reshape(bs, D_MODEL)
    outs = {k: [] for k in ("p_ssm", "p_conv", "p_hgrn", "p_rwkv", "p_shift",
                            "s_ssm", "s_conv", "s_hgrn", "s_rwkv", "s_shift")}
    y_moe_p = y_moe_s = None
    for layer in range(depth):
        p = dict(conv_w=conv_w[layer], conv_b=conv_b[layer], dt_bias=dt_bias[layer], a_log=a_log[layer],
                 d_skip=d_skip[layer], ssm_norm_g=ssm_norm_g[layer], shift_mu=shift_mu[layer],
                 w0=w0[layer], w2=w2[layer], a0=a0[layer], a2=a2[layer], g2=g2[layer], k_k=k_k[layer],
                 k_a=k_a[layer], r_k=r_k[layer].reshape(-1), lnx_w=lnx_w[layer], lnx_b=lnx_b[layer])
        sp = _ssd_params(p)
        rp = _rwkv_params(p)
        lb = lower_bounds[layer]
        lbp = jnp.stack([1.0 - lb, jnp.log(lb), jnp.log1p(-lb)])
        hng = hgrn_norm_g[layer].reshape(1, B_HEAD)
        w_in_l = _layout_w_in(w_in[layer])
        w_out_l = w_out[layer].astype(BF16)
        j = layer // 2
        routed = layer % 2 == 1
        rw = None
        if routed:
            rw_f = jnp.pad(router_w[j], ((0, 0), (0, LANES - N_EXPERTS)))
            rw_hi = rw_f.astype(BF16)
            rw = jnp.concatenate([rw_hi, (rw_f - rw_hi.astype(F32)).astype(BF16)], axis=1)

        proj = _norm_matmul(xp, norm1_g[layer], w_in_l, 1024, 512).reshape(bp, seq, P_COLS)
        ya, ssm_p = _ssd_prompt(proj, sp)
        yb, hgrn_p = _hgrn_prompt(proj, lbp, hng)
        yc, rwkv_p = _rwkv_prompt(proj, rp)
        xp, hp, lp = _out_proj(ya.reshape(tp, A_DIM), yb.reshape(tp, B_DIM), yc.reshape(tp, C_DIM),
                               w_out_l, xp, norm2_g[layer], rw, 256)
        outs["p_ssm"].append(ssm_p)
        outs["p_conv"].append(_xbc_cols(proj[:, seq - (CONV_W - 1):, :]))
        outs["p_hgrn"].append(hgrn_p)
        outs["p_rwkv"].append(rwkv_p)
        outs["p_shift"].append(proj[:, seq - 1, P_R:P_R + C_COLS])

        projs = _norm_matmul(xs, norm1_g[layer], w_in_l, 128, 512)
        ya, ssm_s = _ssd_step(projs, state_conv[layer], state_ssm[layer], p, sp)
        yb, hgrn_s = _hgrn_step(projs, state_hgrn[layer], lbp, hng)
        yc, rwkv_s = _rwkv_step(projs, state_shift[layer], state_rwkv[layer], rp)
        xs, hs, ls = _out_proj(ya, yb, yc, w_out_l, xs, norm2_g[layer], rw, 128)
        outs["s_ssm"].append(ssm_s)
        outs["s_conv"].append(jnp.concatenate([state_conv[layer][:, 1:, :], _xbc_cols(projs)[:, None, :]], axis=1))
        outs["s_hgrn"].append(hgrn_s)
        outs["s_rwkv"].append(rwkv_s)
        outs["s_shift"].append(projs[:, P_R:P_R + C_COLS])

        if not routed:
            xp = _ffn(xp, hp, ffn_w_gate[j], ffn_w_up[j], ffn_w_down[j], 512)
            xs = _ffn(xs, hs, ffn_w_gate[j], ffn_w_up[j], ffn_w_down[j], 128)
        else:
            y_moe = _moe(jnp.concatenate([hp, hs], axis=0), jnp.concatenate([lp, ls], axis=0),
                         exp_w_gate[j], exp_w_up[j], exp_w_down[j])
            if layer == depth - 1:
                y_moe_p, y_moe_s = y_moe[:tp], y_moe[tp:]
            else:
                xp, xs = xp + y_moe[:tp], xs + y_moe[tp:]

    zp = y_moe_p if y_moe_p is not None else jnp.zeros_like(xp)
    zs = y_moe_s if y_moe_s is not None else jnp.zeros_like(xs)
    y_prompt = _final_norm(xp, zp, final_norm_g, 512).reshape(bp, seq, D_MODEL)
    y_sample = _final_norm(xs, zs, final_norm_g, 128).reshape(bs, 1, D_MODEL)
    st = {k: jnp.stack(v) for k, v in outs.items()}
    return (y_prompt, y_sample, st["p_ssm"], st["p_conv"], st["p_hgrn"], st["p_rwkv"], st["p_shift"],
            st["s_ssm"], st["s_conv"], st["s_hgrn"], st["s_rwkv"], st["s_shift"])
```

```python
import functools
import math

import jax
import jax.numpy as jnp
from jax import lax
from jax.experimental import pallas as pl
from jax.experimental.pallas import tpu as pltpu

F32 = jnp.float32
BF16 = jnp.bfloat16
HI = lax.Precision.HIGHEST

D_MODEL = 2048
A_DIM, A_HEAD, A_HEADS, A_GROUPS, A_STATE = 1024, 64, 16, 2, 128
A_GN = A_GROUPS * A_STATE
CONV_W = 4
CONV_CH = A_DIM + 2 * A_GN
B_DIM, B_HEAD, B_HEADS = 512, 128, 4
C_DIM, C_HEAD, C_HEADS = 512, 64, 8
C_LR_W, C_LR_A, C_LR_G = 64, 64, 128
C_LR = C_LR_W + C_LR_A + C_LR_G
C_COLS = 3 * C_DIM + C_LR
IN_COLS = A_DIM + CONV_CH + A_HEADS + 4 * B_DIM + C_COLS
D_FF, N_EXPERTS, TOP_K, E_FF = 5504, 8, 2, 7168
NORM_EPS, GN_EPS, L2_EPS = 1e-6, 64e-5, 1e-12

P_Z, P_X, P_B, P_C = 0, 1024, 2048, 2304
P_Q, P_F, P_I, P_G = 2560, 3072, 3584, 4096
P_R, P_K, P_V, P_LR, P_DT, P_COLS = 4608, 5120, 5632, 6144, 6400, 6656
LANES = 128
DT_PAD = P_COLS - P_DT - A_HEADS

SSD_CHUNK = 256
HGRN_BLOCK, HGRN_CHUNK = 256, 16
RWKV_BLOCK, RWKV_CHUNK = 256, 64
STEP_BB = 8

FFN_TF, FFN_TN = 512, 512
MOE_BM, MOE_SUB, MOE_TF, MOE_TN = 512, 256, 1024, 512

NT_DIMS = (((1,), (1,)), ((), ()))
TN_DIMS = (((0,), (0,)), ((), ()))


def _cparams(sem, vmem_mb=48):
    return pltpu.CompilerParams(dimension_semantics=sem, vmem_limit_bytes=vmem_mb * 1024 * 1024)


def _dot(a, b):
    return jnp.dot(a.astype(BF16), b.astype(BF16), preferred_element_type=F32)


def _dot_nt(a, b):
    return lax.dot_general(a.astype(BF16), b.astype(BF16), NT_DIMS, preferred_element_type=F32)


def _dot_tn(a, b):
    return lax.dot_general(a.astype(BF16), b.astype(BF16), TN_DIMS, preferred_element_type=F32)


def _dot_hi(a, b):
    return jnp.dot(a, b, precision=HI, preferred_element_type=F32)


def _dot_nt_hi(a, b):
    return lax.dot_general(a, b, NT_DIMS, precision=HI, preferred_element_type=F32)


def _dot_tn_hi(a, b):
    return lax.dot_general(a, b, TN_DIMS, precision=HI, preferred_element_type=F32)


def _silu(x):
    return x * jax.nn.sigmoid(x)


def _softplus(x):
    return jnp.maximum(x, 0.0) + jnp.log1p(jnp.exp(-jnp.abs(x)))


def _iota(shape, dim):
    return lax.broadcasted_iota(jnp.int32, shape, dim)


def _rms(x, g):
    return x * lax.rsqrt(jnp.mean(x * x, axis=-1, keepdims=True) + NORM_EPS) * g


def _full(shape):
    nd = len(shape)
    return pl.BlockSpec(shape, lambda *_: (0,) * nd)


def _norm_matmul_kernel(x_ref, g_ref, w_ref, o_ref, h_scr):
    @pl.when(pl.program_id(1) == 0)
    def _():
        h_scr[...] = _rms(x_ref[...], g_ref[...]).astype(BF16)

    o_ref[...] = jnp.dot(h_scr[...], w_ref[...], preferred_element_type=F32)


def _norm_matmul(x, g, w, tm, tn):
    m, k = x.shape
    n = w.shape[1]
    return pl.pallas_call(
        _norm_matmul_kernel,
        grid=(m // tm, n // tn),
        in_specs=[pl.BlockSpec((tm, k), lambda i, j: (i, 0)),
                  pl.BlockSpec((1, k), lambda i, j: (0, 0)),
                  pl.BlockSpec((k, tn), lambda i, j: (0, j))],
        out_specs=pl.BlockSpec((tm, tn), lambda i, j: (i, j)),
        out_shape=jax.ShapeDtypeStruct((m, n), F32),
        scratch_shapes=[pltpu.VMEM((tm, k), BF16)],
        compiler_params=_cparams(("parallel", "arbitrary")),
        name="norm_matmul",
    )(x, g.reshape(1, k), w)


def _out_proj_kernel(routed, ya_ref, yb_ref, yc_ref, w_ref, r_ref, g_ref, *rest):
    x = (r_ref[...]
         + jnp.dot(ya_ref[...], w_ref[0:A_DIM, :], preferred_element_type=F32)
         + jnp.dot(yb_ref[...], w_ref[A_DIM:A_DIM + B_DIM, :], preferred_element_type=F32)
         + jnp.dot(yc_ref[...], w_ref[A_DIM + B_DIM:, :], preferred_element_type=F32))
    h = _rms(x, g_ref[...])
    h_hi = h.astype(BF16)
    if routed:
        rw_ref, x_ref, h_ref, l_ref = rest
        h_lo = (h - h_hi.astype(F32)).astype(BF16)
        rw = rw_ref[...]
        l2 = jnp.dot(h_hi, rw, preferred_element_type=F32)
        l_ref[...] = (l2[:, :LANES] + l2[:, LANES:]
                      + jnp.dot(h_lo, rw[:, :LANES], preferred_element_type=F32))
    else:
        x_ref, h_ref = rest
    x_ref[...] = x
    h_ref[...] = h_hi


def _out_proj(ya, yb, yc, w, res, g, rw, tm):
    m = ya.shape[0]
    n = w.shape[1]
    routed = rw is not None
    row = lambda width: pl.BlockSpec((tm, width), lambda i: (i, 0))
    in_specs = [row(A_DIM), row(B_DIM), row(C_DIM), _full(w.shape), row(n), _full((1, n))]
    out_specs = [row(n), row(n)]
    out_shape = [jax.ShapeDtypeStruct((m, n), F32), jax.ShapeDtypeStruct((m, n), BF16)]
    args = [ya, yb, yc, w, res, g.reshape(1, n)]
    if routed:
        in_specs.append(_full(rw.shape))
        out_specs.append(row(LANES))
        out_shape.append(jax.ShapeDtypeStruct((m, LANES), F32))
        args.append(rw)
    outs = pl.pallas_call(
        functools.partial(_out_proj_kernel, routed),
        grid=(m // tm,),
        in_specs=in_specs, out_specs=out_specs, out_shape=out_shape,
        compiler_params=_cparams(("parallel",)),
        name="out_proj",
    )(*args)
    return outs if routed else (*outs, None)


def _ffn_up_kernel(h_ref, wg_ref, wu_ref, o_ref, wg_s, wu_s):
    @pl.when(pl.program_id(1) == 0)
    def _():
        wg_s[...] = wg_ref[...].astype(BF16)
        wu_s[...] = wu_ref[...].astype(BF16)

    h = h_ref[...]
    a = jnp.dot(h, wg_s[...], preferred_element_type=F32)
    u = jnp.dot(h, wu_s[...], preferred_element_type=F32)
    o_ref[...] = (_silu(a) * u).astype(BF16)


def _ffn_down_kernel(h_ref, wd_ref, x_ref, o_ref, wd_s):
    @pl.when(pl.program_id(1) == 0)
    def _():
        wd_s[...] = wd_ref[...].astype(BF16)

    o_ref[...] = x_ref[...] + jnp.dot(h_ref[...], wd_s[...], preferred_element_type=F32)


def _ffn(x, h, wg, wu, wd, tm):
    m, k = h.shape
    ff = wg.shape[1]
    tf, tn = FFN_TF, FFN_TN
    tu = 2 * tm if m % (2 * tm) == 0 else tm
    act = pl.pallas_call(
        _ffn_up_kernel,
        grid=(pl.cdiv(ff, tf), m // tu),
        in_specs=[pl.BlockSpec((tu, k), lambda f, i: (i, 0)),
                  pl.BlockSpec((k, tf), lambda f, i: (0, f)),
                  pl.BlockSpec((k, tf), lambda f, i: (0, f))],
        out_specs=pl.BlockSpec((tu, tf), lambda f, i: (i, f)),
        out_shape=jax.ShapeDtypeStruct((m, ff), BF16),
        scratch_shapes=[pltpu.VMEM((k, tf), BF16)] * 2,
        compiler_params=_cparams(("arbitrary", "arbitrary")),
        name="ffn_up",
    )(h, wg, wu)
    return pl.pallas_call(
        _ffn_down_kernel,
        grid=(k // tn, m // tm),
        in_specs=[pl.BlockSpec((tm, ff), lambda n, i: (i, 0)),
                  pl.BlockSpec((ff, tn), lambda n, i: (0, n)),
                  pl.BlockSpec((tm, tn), lambda n, i: (i, n))],
        out_specs=pl.BlockSpec((tm, tn), lambda n, i: (i, n)),
        out_shape=jax.ShapeDtypeStruct((m, k), F32),
        scratch_shapes=[pltpu.VMEM((ff, tn), BF16)],
        compiler_params=_cparams(("arbitrary", "arbitrary")),
        name="ffn_down",
    )(act, wd, x)


def _moe_rows(nsub, o_ref, compute):
    bm = o_ref.shape[0]
    for ns in range(bm // MOE_SUB + 1):
        @pl.when(nsub == ns)
        def _(rows=ns * MOE_SUB):
            if rows:
                o_ref[0:rows, :] = compute(rows).astype(o_ref.dtype)
            if rows < bm:
                o_ref[rows:, :] = jnp.zeros((bm - rows, o_ref.shape[1]), o_ref.dtype)


def _moe_up_kernel(bexp_ref, nv_ref, first_ref, nsub_ref, x_ref, wg_ref, wu_ref, o_ref, wg_s, wu_s):
    b = pl.program_id(1)

    @pl.when(first_ref[b] == 1)
    def _():
        wg_s[...] = wg_ref[...].astype(BF16)
        wu_s[...] = wu_ref[...].astype(BF16)

    def compute(rows):
        x = x_ref[0:rows, :]
        a = jnp.dot(x, wg_s[...], preferred_element_type=F32)
        u = jnp.dot(x, wu_s[...], preferred_element_type=F32)
        return _silu(a) * u

    _moe_rows(nsub_ref[b], o_ref, compute)


def _moe_down_kernel(bexp_ref, nv_ref, first_ref, nsub_ref, h_ref, wd_ref, o_ref, wd_s):
    b = pl.program_id(1)

    @pl.when(first_ref[b] == 1)
    def _():
        wd_s[...] = wd_ref[...].astype(BF16)

    _moe_rows(nsub_ref[b], o_ref,
              lambda rows: jnp.dot(h_ref[0:rows, :], wd_s[...], preferred_element_type=F32))


def _moe_experts(xb, bexp, nvalid, first, nsub, wg, wu, wd):
    n_slots, k = xb.shape
    ff = wg.shape[2]
    nb = n_slots // MOE_BM
    tf, tn = MOE_TF, MOE_TN

    def blk(b, nv):
        return jnp.minimum(b, nv[0] - 1)

    act = pl.pallas_call(
        _moe_up_kernel,
        grid_spec=pltpu.PrefetchScalarGridSpec(
            num_scalar_prefetch=4,
            grid=(ff // tf, nb),
            in_specs=[pl.BlockSpec((MOE_BM, k), lambda f, b, be, nv, fr, ns: (blk(b, nv), 0)),
                      pl.BlockSpec((None, k, tf), lambda f, b, be, nv, fr, ns: (be[b], 0, f)),
                      pl.BlockSpec((None, k, tf), lambda f, b, be, nv, fr, ns: (be[b], 0, f))],
            out_specs=pl.BlockSpec((MOE_BM, tf), lambda f, b, be, nv, fr, ns: (b, f)),
            scratch_shapes=[pltpu.VMEM((k, tf), BF16)] * 2,
        ),
        out_shape=jax.ShapeDtypeStruct((n_slots, ff), BF16),
        compiler_params=_cparams(("arbitrary", "arbitrary"), 56),
        name="moe_up",
    )(bexp, nvalid, first, nsub, xb, wg, wu)
    per = MOE_BM // MOE_SUB
    bexp_d = jnp.repeat(bexp, per)
    first_d = jnp.concatenate([jnp.ones((1,), jnp.int32), (bexp_d[1:] != bexp_d[:-1]).astype(jnp.int32)])
    nsub_d = jnp.clip(jnp.repeat(nsub, per) - jnp.tile(jnp.arange(per, dtype=jnp.int32), nb), 0, 1)
    return pl.pallas_call(
        _moe_down_kernel,
        grid_spec=pltpu.PrefetchScalarGridSpec(
            num_scalar_prefetch=4,
            grid=(k // tn, nb * per),
            in_specs=[pl.BlockSpec((MOE_SUB, ff), lambda n, b, be, nv, fr, ns: (blk(b, nv), 0)),
                      pl.BlockSpec((None, ff, tn), lambda n, b, be, nv, fr, ns: (be[b], 0, n))],
            out_specs=pl.BlockSpec((MOE_SUB, tn), lambda n, b, be, nv, fr, ns: (b, n)),
            scratch_shapes=[pltpu.VMEM((ff, tn), BF16)],
        ),
        out_shape=jax.ShapeDtypeStruct((n_slots, k), F32),
        compiler_params=_cparams(("arbitrary", "arbitrary"), 56),
        name="moe_down",
    )(bexp_d, nvalid * per, first_d, nsub_d, act, wd)


def _final_norm_kernel(x_ref, y_ref, g_ref, o_ref):
    o_ref[...] = _rms(x_ref[...] + y_ref[...], g_ref[...])


def _final_norm(x, y, g, tm):
    m, k = x.shape
    return pl.pallas_call(
        _final_norm_kernel,
        grid=(m // tm,),
        in_specs=[pl.BlockSpec((tm, k), lambda i: (i, 0)),
                  pl.BlockSpec((tm, k), lambda i: (i, 0)),
                  pl.BlockSpec((1, k), lambda i: (0, 0))],
        out_specs=pl.BlockSpec((tm, k), lambda i: (i, 0)),
        out_shape=jax.ShapeDtypeStruct((m, k), F32),
        compiler_params=_cparams(("parallel",)),
        name="final_norm",
    )(x, y, g.reshape(1, k))


def _ssd_gate_norm(y, z, ng):
    y = y * _silu(z)
    gw = A_DIM // A_GROUPS
    outs = []
    for g in range(A_GROUPS):
        yg = y[:, g * gw:(g + 1) * gw]
        outs.append(yg * lax.rsqrt(jnp.mean(yg * yg, axis=-1, keepdims=True) + NORM_EPS))
    return jnp.concatenate(outs, axis=-1) * ng


def _ssd_prompt_kernel(z_ref, x_ref, b_ref, c_ref, dt_ref, cwx, cwb, cwc, cbx, cbb, cbc,
                       dtb, aneg, dsk, ng, y_ref, st_ref, h_scr, xpx, xpb, xpc, yacc):
    c = pl.program_id(1)
    ch = SSD_CHUNK

    @pl.when(c == 0)
    def _():
        h_scr[...] = jnp.zeros_like(h_scr)
        xpx[0:8, :] = jnp.zeros((8, A_DIM), F32)
        xpb[0:8, :] = jnp.zeros((8, A_STATE * A_GROUPS), F32)
        xpc[0:8, :] = jnp.zeros((8, A_STATE * A_GROUPS), F32)

    xpx[8:8 + ch, :] = x_ref[...]
    xpb[8:8 + ch, :] = b_ref[...]
    xpc[8:8 + ch, :] = c_ref[...]

    def conv(xp, cw, cb):
        acc = cb[...] + xp[pl.ds(8 - (CONV_W - 1), ch), :] * cw[0:1, :]
        for tap in range(1, CONV_W):
            acc = acc + xp[pl.ds(8 - (CONV_W - 1) + tap, ch), :] * cw[tap:tap + 1, :]
        return _silu(acc)

    ux = conv(xpx, cwx, cbx)
    ub = conv(xpb, cwb, cbb)
    uc = conv(xpc, cwc, cbc)
    xpx[0:8, :] = xpx[ch:ch + 8, :]
    xpb[0:8, :] = xpb[ch:ch + 8, :]
    xpc[0:8, :] = xpc[ch:ch + 8, :]

    dt = _softplus(dt_ref[...] + dtb[...])
    la = dt * aneg[...]
    tri = (_iota((ch, ch), 1) <= _iota((ch, ch), 0)).astype(F32)
    cum = _dot_hi(tri, la)
    eye = (_iota((A_HEADS, LANES), 0) == _iota((A_HEADS, LANES), 1)).astype(F32)
    cum_r = _dot_nt_hi(eye, cum)
    dt_r = _dot_nt_hi(eye, dt)
    cl = cum[ch - 1:ch, :]
    tail = jnp.exp(cl - cum) * dt
    ecum = jnp.exp(cum)
    ecl = jnp.exp(cl)
    causal = _iota((ch, ch), 1) <= _iota((ch, ch), 0)
    rep = A_HEADS // A_GROUPS

    for g in range(A_GROUPS):
        bg = ub[:, g * A_STATE:(g + 1) * A_STATE].astype(BF16)
        cg = uc[:, g * A_STATE:(g + 1) * A_STATE].astype(BF16)
        cb_g = _dot_nt(cg, bg)
        for h in range(g * rep, (g + 1) * rep):
            hs = slice(h * A_HEAD, (h + 1) * A_HEAD)
            seg = cum[:, h:h + 1] - cum_r[h:h + 1, :]
            w = jnp.exp(jnp.where(causal, seg, -jnp.inf)) * cb_g * dt_r[h:h + 1, :]
            xh = ux[:, hs]
            hst = h_scr[h]
            y_h = _dot(w, xh) + _dot_nt(cg, hst) * ecum[:, h:h + 1]
            yacc[:, hs] = y_h
            h_scr[h] = hst * ecl[:, h:h + 1] + _dot_tn(xh * tail[:, h:h + 1], bg)

    y = yacc[...] + dsk[...] * ux
    y_ref[...] = _ssd_gate_norm(y, z_ref[...], ng[...]).astype(BF16)

    @pl.when(c == pl.num_programs(1) - 1)
    def _():
        st_ref[...] = h_scr[...]


def _ssd_params(p):
    cw, cb = p["conv_w"], p["conv_b"].reshape(1, CONV_CH)
    pad = LANES - A_HEADS
    return dict(
        cwx=cw[:, :A_DIM], cwb=cw[:, A_DIM:A_DIM + A_GN], cwc=cw[:, A_DIM + A_GN:],
        cbx=cb[:, :A_DIM], cbb=cb[:, A_DIM:A_DIM + A_GN], cbc=cb[:, A_DIM + A_GN:],
        dtb=jnp.pad(p["dt_bias"], (0, pad)).reshape(1, LANES),
        aneg=jnp.pad(-jnp.exp(p["a_log"]), (0, pad)).reshape(1, LANES),
        dsk=jnp.repeat(p["d_skip"], A_HEAD).reshape(1, A_DIM),
        ng=p["ssm_norm_g"].reshape(1, A_DIM),
    )


def _ssd_prompt(proj, sp):
    bsz, seq, _ = proj.shape
    ch = SSD_CHUNK
    gn = A_GN

    def col(width, off):
        return pl.BlockSpec((None, ch, width), lambda b, c: (b, c, off // width))

    names = ("cwx", "cwb", "cwc", "cbx", "cbb", "cbc", "dtb", "aneg", "dsk", "ng")
    y, st = pl.pallas_call(
        _ssd_prompt_kernel,
        grid=(bsz, seq // ch),
        in_specs=[col(A_DIM, P_Z), col(A_DIM, P_X), col(gn, P_B), col(gn, P_C), col(LANES, P_DT)]
                 + [_full(sp[n].shape) for n in names],
        out_specs=[pl.BlockSpec((None, ch, A_DIM), lambda b, c: (b, c, 0)),
                   pl.BlockSpec((None, A_HEADS, A_HEAD, A_STATE), lambda b, c: (b, 0, 0, 0))],
        out_shape=[jax.ShapeDtypeStruct((bsz, seq, A_DIM), BF16),
                   jax.ShapeDtypeStruct((bsz, A_HEADS, A_HEAD, A_STATE), F32)],
        scratch_shapes=[pltpu.VMEM((A_HEADS, A_HEAD, A_STATE), F32),
                        pltpu.VMEM((ch + 8, A_DIM), F32),
                        pltpu.VMEM((ch + 8, gn), F32),
                        pltpu.VMEM((ch + 8, gn), F32),
                        pltpu.VMEM((ch, A_DIM), F32)],
        compiler_params=_cparams(("parallel", "arbitrary")),
        name="ssd_prompt",
    )(proj, proj, proj, proj, proj, *[sp[n] for n in names])
    return y, st


def _ssd_step_kernel(z_ref, x_ref, b_ref, c_ref, dt_ref, cs_ref, cw, cb, dtb, anegx, expand, dsk, ng,
                     h_ref, y_ref, ho_ref, y_scr):
    bb = STEP_BB
    xbc = (x_ref[...], b_ref[...], c_ref[...])
    offs = (0, A_DIM, A_DIM + A_GN, CONV_CH)
    u = []
    for i in range(3):
        sl = slice(offs[i], offs[i + 1])
        acc = cb[:, sl] + xbc[i] * cw[CONV_W - 1:CONV_W, sl]
        for tap in range(CONV_W - 1):
            acc = acc + cs_ref[tap][:, sl] * cw[tap:tap + 1, sl]
        u.append(_silu(acc))
    ux, ub, uc = u
    dt = _softplus(dt_ref[...] + dtb[...])
    dtx = _dot_hi(dt, expand[...])
    dec_t = jnp.exp(dtx * anegx[...]).T
    xdt_t = (ux * dtx).T
    gw = A_DIM // A_GROUPS
    for b in range(bb):
        for g in range(A_GROUPS):
            rs = slice(g * gw, (g + 1) * gw)
            ns = slice(g * A_STATE, (g + 1) * A_STATE)
            hg = h_ref[b, rs, :] * dec_t[rs, b:b + 1] + xdt_t[rs, b:b + 1] * ub[b:b + 1, ns]
            ho_ref[b, rs, :] = hg
            y_scr[b:b + 1, rs] = _dot_nt(uc[:, ns], hg)[b:b + 1, :]
    y = y_scr[...] + dsk[...] * ux
    y_ref[...] = _ssd_gate_norm(y, z_ref[...], ng[...]).astype(BF16)


def _ssd_step(proj, conv_state, h0, p, sp):
    bsz = proj.shape[0]
    bb = STEP_BB
    cs = jnp.swapaxes(conv_state, 0, 1)
    hflat = h0.reshape(bsz, A_DIM, A_STATE)
    expand = (jnp.arange(LANES)[:, None] == (jnp.arange(A_DIM) // A_HEAD)[None, :]).astype(F32)
    anegx = jnp.repeat(-jnp.exp(p["a_log"]), A_HEAD).reshape(1, A_DIM)
    cw, cb = p["conv_w"], p["conv_b"].reshape(1, CONV_CH)

    def col(width, off):
        return pl.BlockSpec((bb, width), lambda i: (i, off // width))

    y, hn = pl.pallas_call(
        _ssd_step_kernel,
        grid=(bsz // bb,),
        in_specs=[col(A_DIM, P_Z), col(A_DIM, P_X), col(A_GN, P_B), col(A_GN, P_C), col(LANES, P_DT),
                  pl.BlockSpec((CONV_W - 1, bb, CONV_CH), lambda i: (0, i, 0)),
                  _full(cw.shape), _full(cb.shape), _full(sp["dtb"].shape), _full(anegx.shape),
                  _full(expand.shape), _full(sp["dsk"].shape), _full(sp["ng"].shape),
                  pl.BlockSpec((bb, A_DIM, A_STATE), lambda i: (i, 0, 0))],
        out_specs=[pl.BlockSpec((bb, A_DIM), lambda i: (i, 0)),
                   pl.BlockSpec((bb, A_DIM, A_STATE), lambda i: (i, 0, 0))],
        out_shape=[jax.ShapeDtypeStruct((bsz, A_DIM), BF16),
                   jax.ShapeDtypeStruct((bsz, A_DIM, A_STATE), F32)],
        scratch_shapes=[pltpu.VMEM((bb, A_DIM), F32)],
        compiler_params=_cparams(("parallel",)),
        name="ssd_step",
    )(proj, proj, proj, proj, proj, cs, cw, cb, sp["dtb"], anegx, expand, sp["dsk"], sp["ng"], hflat)
    return y, hn.reshape(h0.shape)


def _hgrn_gates(q, f, lbp):
    qf = _silu(q)
    kf = lbp[0:1, :] * jax.nn.sigmoid(-f)
    log_sig = jnp.minimum(f, 0.0) - jnp.log1p(jnp.exp(-jnp.abs(f)))
    a = lbp[1:2, :]
    b = lbp[2:3, :] + log_sig
    logf = jnp.maximum(a, b) + jnp.log1p(jnp.exp(-jnp.abs(a - b)))
    return qf, kf, logf


def _hgrn_out(o, g, ng):
    outs = []
    for h in range(B_HEADS):
        oh = o[:, h * B_HEAD:(h + 1) * B_HEAD]
        outs.append(oh * lax.rsqrt(jnp.mean(oh * oh, axis=-1, keepdims=True) + NORM_EPS) * ng)
    return jnp.concatenate(outs, axis=-1) * _silu(g)


def _hgrn_prompt_kernel(q_ref, f_ref, i_ref, g_ref, lbp, ng, o_ref, st_ref,
                        s_scr, cum_scr, q_scr, k_scr, o_scr):
    c = pl.program_id(1)
    tb, cs = HGRN_BLOCK, HGRN_CHUNK

    @pl.when(c == 0)
    def _():
        s_scr[...] = jnp.zeros_like(s_scr)

    qf, kf, logf = _hgrn_gates(q_ref[...], f_ref[...], lbp[...])
    r_i, c_i = _iota((tb, tb), 0), _iota((tb, tb), 1)
    bd = ((r_i // cs == c_i // cs) & (c_i <= r_i)).astype(F32)
    cum_scr[...] = _dot_hi(bd, logf)
    q_scr[...] = qf
    k_scr[...] = kf
    rows = _iota((cs, B_HEAD), 0)

    def chunk(cc, carry):
        r0 = pl.multiple_of(cc * cs, cs)
        for h in range(B_HEADS):
            hs = slice(h * B_HEAD, (h + 1) * B_HEAD)
            cu = cum_scr[pl.ds(r0, cs), hs]
            q = q_scr[pl.ds(r0, cs), hs]
            k = k_scr[pl.ds(r0, cs), hs]
            v = i_ref[pl.ds(r0, cs), hs]
            st = s_scr[h]
            o = _dot_nt(q * jnp.exp(cu), st)
            for s in range(cs):
                d = jnp.exp(jnp.where(rows >= s, cu - cu[s:s + 1, :], -jnp.inf))
                att = jnp.sum(q * k[s:s + 1, :] * d, axis=-1, keepdims=True)
                o = o + att * v[s:s + 1, :]
            o_scr[pl.ds(r0, cs), hs] = o
            cl = cu[cs - 1:cs, :]
            s_scr[h] = st * jnp.exp(cl) + _dot_tn(v, k * jnp.exp(cl - cu))
        return carry

    lax.fori_loop(0, tb // cs, chunk, 0)
    o_ref[...] = _hgrn_out(o_scr[...], g_ref[...], ng[...]).astype(BF16)

    @pl.when(c == pl.num_programs(1) - 1)
    def _():
        for h in range(B_HEADS):
            st_ref[h] = s_scr[h].T


def _hgrn_prompt(proj, lbp, ng):
    bsz, seq, _ = proj.shape
    tb = HGRN_BLOCK

    def col(off):
        return pl.BlockSpec((None, tb, B_DIM), lambda b, c: (b, c, off // B_DIM))

    o, st = pl.pallas_call(
        _hgrn_prompt_kernel,
        grid=(bsz, seq // tb),
        in_specs=[col(P_Q), col(P_F), col(P_I), col(P_G), _full(lbp.shape), _full(ng.shape)],
        out_specs=[pl.BlockSpec((None, tb, B_DIM), lambda b, c: (b, c, 0)),
                   pl.BlockSpec((None, B_HEADS, B_HEAD, B_HEAD), lambda b, c: (b, 0, 0, 0))],
        out_shape=[jax.ShapeDtypeStruct((bsz, seq, B_DIM), BF16),
                   jax.ShapeDtypeStruct((bsz, B_HEADS, B_HEAD, B_HEAD), F32)],
        scratch_shapes=[pltpu.VMEM((B_HEADS, B_HEAD, B_HEAD), F32)]
                       + [pltpu.VMEM((tb, B_DIM), F32)] * 4,
        compiler_params=_cparams(("parallel", "arbitrary")),
        name="hgrn_prompt",
    )(proj, proj, proj, proj, lbp, ng)
    return o, st


def _hgrn_step_kernel(q_ref, f_ref, i_ref, g_ref, lbp, ng, s_ref, o_ref, so_ref, o_scr):
    bb = STEP_BB
    qf, kf, logf = _hgrn_gates(q_ref[...], f_ref[...], lbp[...])
    vf = i_ref[...]
    dec_t = jnp.exp(logf).T
    k_t = kf.T
    q_t = qf.T
    for b in range(bb):
        for h in range(B_HEADS):
            hs = slice(h * B_HEAD, (h + 1) * B_HEAD)
            sh = s_ref[b, hs, :] * dec_t[hs, b:b + 1] + k_t[hs, b:b + 1] * vf[b:b + 1, hs]
            so_ref[b, hs, :] = sh
            o_scr[b:b + 1, hs] = jnp.sum(q_t[hs, b:b + 1] * sh, axis=0, keepdims=True)
    o_ref[...] = _hgrn_out(o_scr[...], g_ref[...], ng[...]).astype(BF16)


def _hgrn_step(proj, s0, lbp, ng):
    bsz = proj.shape[0]
    bb = STEP_BB
    sflat = s0.reshape(bsz, B_DIM, B_HEAD)

    def col(off):
        return pl.BlockSpec((bb, B_DIM), lambda i: (i, off // B_DIM))

    o, sn = pl.pallas_call(
        _hgrn_step_kernel,
        grid=(bsz // bb,),
        in_specs=[col(P_Q), col(P_F), col(P_I), col(P_G), _full(lbp.shape), _full(ng.shape),
                  pl.BlockSpec((bb, B_DIM, B_HEAD), lambda i: (i, 0, 0))],
        out_specs=[pl.BlockSpec((bb, B_DIM), lambda i: (i, 0)),
                   pl.BlockSpec((bb, B_DIM, B_HEAD), lambda i: (i, 0, 0))],
        out_shape=[jax.ShapeDtypeStruct((bsz, B_DIM), BF16),
                   jax.ShapeDtypeStruct((bsz, B_DIM, B_HEAD), F32)],
        scratch_shapes=[pltpu.VMEM((bb, B_DIM), F32)],
        compiler_params=_cparams(("parallel",)),
        name="hgrn_step",
    )(proj, proj, proj, proj, lbp, ng, sflat)
    return o, sn.reshape(s0.shape)


RW_NAMES = ("mu_r", "mu_k", "mu_v", "mu_lr", "w0", "w2", "a0", "a2", "g2", "k_k", "k_a", "r_k")


def _rwkv_params(p):
    mu = p["shift_mu"].reshape(1, C_COLS)
    row = lambda a: a.reshape(1, C_DIM)
    return dict(
        mu_r=mu[:, :C_DIM], mu_k=mu[:, C_DIM:2 * C_DIM], mu_v=mu[:, 2 * C_DIM:3 * C_DIM],
        mu_lr=mu[:, 3 * C_DIM:],
        w0=row(p["w0"]), w2=p["w2"].astype(BF16), a0=row(p["a0"]), a2=p["a2"].astype(BF16),
        g2=p["g2"].astype(BF16), k_k=row(p["k_k"]), k_a=row(p["k_a"]), r_k=row(p["r_k"]),
        lnx_w=row(p["lnx_w"]), lnx_b=row(p["lnx_b"]),
    )


def _head_sum(x):
    return [jnp.sum(x[:, h * C_HEAD:(h + 1) * C_HEAD], axis=-1, keepdims=True) for h in range(C_HEADS)]


def _head_bcast(cols, fn=lambda c: c):
    rows = cols[0].shape[0]
    return jnp.concatenate([jnp.broadcast_to(fn(c), (rows, C_HEAD)) for c in cols], axis=-1)


def _rwkv_prep(r, k, v, lr, pr, pk, pv, plr, prm):
    mu_r, mu_k, mu_v, mu_lr, w0, w2, a0, a2, g2, k_k, k_a, r_k = prm
    r = r + (pr - r) * mu_r
    k = k + (pk - k) * mu_k
    v = v + (pv - v) * mu_v
    lr = lr + (plr - lr) * mu_lr
    wl, al, gl = lr[:, :C_LR_W], lr[:, C_LR_W:C_LR_W + C_LR_A], lr[:, C_LR_W + C_LR_A:]
    wraw = -_softplus(-(w0 + _dot(jnp.tanh(wl), w2))) - 0.5
    ld = -jnp.exp(wraw)
    a = jax.nn.sigmoid(a0 + _dot(al, a2))
    g = _dot(jax.nn.sigmoid(gl), g2)
    kk = k * k_k
    kk = kk * _head_bcast(_head_sum(kk * kk), lambda n2: 1.0 / jnp.maximum(jnp.sqrt(n2), L2_EPS))
    k2 = k * (1.0 + (a - 1.0) * k_a)
    bonus = _head_bcast(_head_sum(r * k2 * r_k)) * v
    return r, k2, v, kk, kk * a, ld, g, bonus


def _rwkv_out(y, bonus, g, lnx_w, lnx_b):
    n = float(C_HEAD)
    mean = _head_bcast(_head_sum(y), lambda s: s / n)
    yc = y - mean
    rstd = _head_bcast(_head_sum(yc * yc), lambda s: lax.rsqrt(s / n + GN_EPS))
    return (yc * rstd * lnx_w + lnx_b + bonus) * g


RW_GH = 4
RW_GW = RW_GH * C_HEAD


def _bd_mask():
    return _iota((RW_GW, RW_GW), 0) // C_HEAD == _iota((RW_GW, RW_GW), 1) // C_HEAD


def _bd_rows(a):
    return jnp.where(_bd_mask(), jnp.concatenate([a] * RW_GH, axis=0), 0.0)


def _bd_lanes(s):
    return jnp.where(_bd_mask(), jnp.concatenate([s] * RW_GH, axis=1), 0.0)


def _bd_fold(m):
    m = jnp.where(_bd_mask(), m, 0.0)
    out = m[:, :C_HEAD]
    for h in range(1, RW_GH):
        out = out + m[:, h * C_HEAD:(h + 1) * C_HEAD]
    return out


def _to_stack(a):
    return jnp.concatenate([a[:, h * C_HEAD:(h + 1) * C_HEAD] for h in range(RW_GH)], axis=0)


def _to_all(s):
    return jnp.concatenate([s[h * C_HEAD:(h + 1) * C_HEAD, :] for h in range(RW_GH)], axis=1)


def _rwkv_a_kernel(r_ref, k_ref, v_ref, lr_ref, rp_ref, kp_ref, vp_ref, lrp_ref, *rest):
    prm = tuple(x[...] for x in rest[:len(RW_NAMES)])
    (kd_ref, rd_ref, bg_ref, ge_ref, bonus_ref, gate_ref, aab_ref, rb_ref, m2_ref, rkv_ref, vk_ref,
     xs_scr, b1_scr, b2_scr, b3_scr, b4_scr) = rest[len(RW_NAMES):]
    c = pl.program_id(1)
    tb, cs = RWKV_BLOCK, RWKV_CHUNK
    first = (c == 0)

    def prev(cur_ref, prev_ref, width):
        xs_scr[8:8 + tb, 0:width] = cur_ref[...]
        xs_scr[7:8, 0:width] = jnp.where(first, 0.0, prev_ref[7:8, :])
        return xs_scr[pl.ds(7, tb), 0:width]

    pr = prev(r_ref, rp_ref, C_DIM)
    pk = prev(k_ref, kp_ref, C_DIM)
    pv = prev(v_ref, vp_ref, C_DIM)
    plr = prev(lr_ref, lrp_ref, C_LR)
    r, k2, v, kk, beta, ld, g, bonus = _rwkv_prep(
        r_ref[...], k_ref[...], v_ref[...], lr_ref[...], pr, pk, pv, plr, prm)
    bonus_ref[...] = bonus
    gate_ref[...] = g

    r_i, c_i = _iota((tb, tb), 0), _iota((tb, tb), 1)
    same = (r_i // cs == c_i // cs)
    lcum = _dot_hi((same & (c_i <= r_i)).astype(F32), ld)
    lend = _dot_hi(same.astype(F32), ld)
    einv = jnp.exp(-lcum)
    eend = jnp.exp(lend - lcum)
    ge_ref[...] = jnp.exp(lend)
    kd_ref[...] = kk * jnp.exp(lcum - ld)
    rd_ref[...] = r * jnp.exp(lcum)
    bg_ref[...] = beta * eend
    b1_scr[...] = beta * einv
    b2_scr[...] = k2 * einv
    b3_scr[...] = k2 * eend
    b4_scr[...] = v
    gw = RW_GW
    tpos, spos = _iota((gw, cs), 0) % cs, _iota((gw, cs), 1)
    strict, incl = tpos > spos, tpos >= spos
    hc = C_HEAD

    def chunk(j, carry):
        r0 = pl.multiple_of(j * cs, cs)
        q0 = pl.multiple_of(j * gw, gw)
        for g in range(C_HEADS // RW_GH):
            gs = slice(g * gw, (g + 1) * gw)
            ls = slice(g * hc, (g + 1) * hc)
            kd = kd_ref[pl.ds(r0, cs), gs]
            rd = rd_ref[pl.ds(r0, cs), gs]
            v = b4_scr[pl.ds(r0, cs), gs]
            lhs = jnp.concatenate([_bd_rows(kd), _bd_rows(rd)], axis=0)
            rhs = jnp.concatenate([b1_scr[pl.ds(r0, cs), gs], b2_scr[pl.ds(r0, cs), gs]], axis=0)
            gm = _dot_nt(lhs, rhs)
            aab_ref[pl.ds(q0, gw), ls] = jnp.where(strict, gm[:gw, :hc], 0.0)
            aak = jnp.where(strict, gm[:gw, hc:], 0.0)
            rb_ref[pl.ds(q0, gw), ls] = jnp.where(incl, gm[gw:, :hc], 0.0)
            rk = jnp.where(incl, gm[gw:, hc:], 0.0)
            mr = _dot(jnp.concatenate([_bd_lanes(aak), _bd_lanes(rk)], axis=0), _to_stack(v))
            m2_ref[pl.ds(q0, gw), ls] = mr[:gw]
            rkv_ref[pl.ds(q0, gw), ls] = mr[gw:]
            vk_ref[pl.ds(q0, gw), ls] = _bd_fold(_dot_tn(v, b3_scr[pl.ds(r0, cs), gs]))
        return carry

    lax.fori_loop(0, tb // cs, chunk, 0)


def _tri_solve_kernel(a_ref, t_ref, a_scr, t_scr):
    n, hf = RWKV_CHUNK, RWKV_CHUNK // 2
    npb = a_scr.shape[2]
    ng = a_scr.shape[1] // n

    def load(t, carry):
        a_scr[t] = a_ref[pl.ds(t, npb, stride=n), :].T
        return carry

    lax.fori_loop(0, n, load, 0)
    jrow = _iota((hf, npb), 0)
    zero = jnp.zeros((hf, npb), F32)

    def coef(t, g, s):
        return a_scr[t, pl.ds(g * n + s, 1), :]

    def row_lo(t, carry):
        def inner(s, accs):
            return tuple(acc - coef(t, g, s) * t_scr[s, g * n:g * n + hf, :] for g, acc in enumerate(accs))

        accs = lax.fori_loop(0, t, inner, ((jrow == t).astype(F32),) * ng)
        for g, acc in enumerate(accs):
            t_scr[t, g * n:g * n + hf, :] = acc
            t_scr[t, g * n + hf:(g + 1) * n, :] = zero
        return carry

    lax.fori_loop(0, hf, row_lo, 0)

    def row_hi(t, carry):
        def inner_lo(s, accs):
            return tuple(acc - coef(t, g, s) * t_scr[s, g * n:g * n + hf, :] for g, acc in enumerate(accs))

        def inner_hi(s, accs):
            lo, hi = accs[:ng], accs[ng:]
            lo = tuple(acc - coef(t, g, s) * t_scr[s, g * n:g * n + hf, :] for g, acc in enumerate(lo))
            hi = tuple(acc - coef(t, g, s) * t_scr[s, g * n + hf:(g + 1) * n, :] for g, acc in enumerate(hi))
            return lo + hi

        lo = lax.fori_loop(0, hf, inner_lo, (zero,) * ng)
        accs = lax.fori_loop(hf, t, inner_hi, lo + ((jrow + hf == t).astype(F32),) * ng)
        for g in range(ng):
            t_scr[t, g * n:g * n + hf, :] = accs[g]
            t_scr[t, g * n + hf:(g + 1) * n, :] = accs[ng + g]
        return carry

    lax.fori_loop(hf, n, row_hi, 0)

    def store(t, carry):
        t_ref[pl.ds(t, npb, stride=n), :] = t_scr[t].T
        return carry

    lax.fori_loop(0, n, store, 0)


def _tri_solve(a):
    rows, width = a.shape
    n = RWKV_CHUNK
    npb = min(LANES, rows // n)
    return pl.pallas_call(
        _tri_solve_kernel,
        grid=(rows // (npb * n),),
        in_specs=[pl.BlockSpec((npb * n, width), lambda i: (i, 0))],
        out_specs=pl.BlockSpec((npb * n, width), lambda i: (i, 0)),
        out_shape=jax.ShapeDtypeStruct(a.shape, F32),
        scratch_shapes=[pltpu.VMEM((n, width, npb), F32)] * 2,
        compiler_params=_cparams(("parallel",)),
        name="tri_solve",
    )(a)


def _rwkv_c_kernel(t_ref, kd_ref, rd_ref, bg_ref, ge_ref, bonus_ref, gate_ref, rb_ref, m2_ref, rkv_ref, vk_ref,
                   lnw, lnb, y_ref, st_ref, s_scr, y_scr):
    c = pl.program_id(1)
    tb, cs, gw, hc = RWKV_BLOCK, RWKV_CHUNK, RW_GW, C_HEAD

    @pl.when(c == 0)
    def _():
        s_scr[...] = jnp.zeros_like(s_scr)

    def chunk(j, carry):
        r0 = pl.multiple_of(j * cs, cs)
        q0 = pl.multiple_of(j * gw, gw)
        for g in range(C_HEADS // RW_GH):
            gs = slice(g * gw, (g + 1) * gw)
            ls = slice(g * hc, (g + 1) * hc)
            rhs = jnp.concatenate([_to_stack(kd_ref[pl.ds(r0, cs), gs]), m2_ref[pl.ds(q0, gw), ls]], axis=1)
            wu = _dot(_bd_lanes(t_ref[pl.ds(q0, gw), ls]), rhs)
            rbwu = _dot(_bd_lanes(rb_ref[pl.ds(q0, gw), ls]), wu)
            rt = _to_stack(rd_ref[pl.ds(r0, cs), gs]) - rbwu[:, :hc]
            yc = rkv_ref[pl.ds(q0, gw), ls] - rbwu[:, hc:]
            wu_bd = jnp.concatenate([_bd_lanes(wu[:, :hc]), _bd_lanes(wu[:, hc:])], axis=1)
            nu = _dot_tn(wu_bd, _to_stack(bg_ref[pl.ds(r0, cs), gs]))
            ge_row = ge_ref[pl.ds(r0, 1), gs]
            ge_st = jnp.concatenate(
                [jnp.broadcast_to(ge_row[:, h * hc:(h + 1) * hc], (hc, hc)) for h in range(RW_GH)], axis=0)
            s = s_scr[g]
            s_bd = _bd_lanes(s)
            y_scr[pl.ds(r0, cs), gs] = _dot_nt(_to_all(rt), s_bd) + _to_all(yc)
            s_scr[g] = s * ge_st - _dot(s_bd, nu[:gw]) + (vk_ref[pl.ds(q0, gw), ls] - nu[gw:])
        return carry

    lax.fori_loop(0, tb // cs, chunk, 0)
    y_ref[...] = _rwkv_out(y_scr[...], bonus_ref[...], gate_ref[...], lnw[...], lnb[...]).astype(BF16)

    @pl.when(c == pl.num_programs(1) - 1)
    def _():
        st_ref[...] = s_scr[...]


def _rwkv_prompt(proj, rp):
    bsz, seq, _ = proj.shape
    tb, cs = RWKV_BLOCK, RWKV_CHUNK
    nblk = seq // tb

    def col(width, off):
        return pl.BlockSpec((None, tb, width), lambda b, c: (b, c, off // width))

    def pcol(width, off):
        return pl.BlockSpec((None, 8, width), lambda b, c: (b, jnp.maximum(c * (tb // 8) - 1, 0), off // width))

    ng = C_HEADS // RW_GH
    srows = tb // cs * RW_GW
    tok = pl.BlockSpec((None, tb, C_DIM), lambda b, c: (b, c, 0))
    stk = pl.BlockSpec((None, srows, ng * C_HEAD), lambda b, c: (b, c, 0))
    tok_shape = jax.ShapeDtypeStruct((bsz, seq, C_DIM), F32)
    stk_shape = jax.ShapeDtypeStruct((bsz, nblk * srows, ng * C_HEAD), F32)
    prm = [rp[n] for n in RW_NAMES]
    outs = pl.pallas_call(
        _rwkv_a_kernel,
        grid=(bsz, nblk),
        in_specs=[col(C_DIM, P_R), col(C_DIM, P_K), col(C_DIM, P_V), col(C_LR, P_LR),
                  pcol(C_DIM, P_R), pcol(C_DIM, P_K), pcol(C_DIM, P_V), pcol(C_LR, P_LR)]
                 + [_full(x.shape) for x in prm],
        out_specs=[tok] * 6 + [stk] * 5,
        out_shape=[tok_shape] * 6 + [stk_shape] * 5,
        scratch_shapes=[pltpu.VMEM((tb + 8, C_DIM), F32)] + [pltpu.VMEM((tb, C_DIM), F32)] * 4,
        compiler_params=_cparams(("parallel", "parallel")),
        name="rwkv_prep",
    )(proj, proj, proj, proj, proj, proj, proj, proj, *prm)
    kd, rd, bg, ge, bonus, gate, aab, rb, m2, rkv, vk = outs

    tmat = _tri_solve(aab.reshape(-1, ng * C_HEAD)).reshape(stk_shape.shape)

    y, st = pl.pallas_call(
        _rwkv_c_kernel,
        grid=(bsz, nblk),
        in_specs=[stk] + [tok] * 6 + [stk] * 4 + [_full((1, C_DIM))] * 2,
        out_specs=[tok, pl.BlockSpec((None, ng, RW_GW, C_HEAD), lambda b, c: (b, 0, 0, 0))],
        out_shape=[jax.ShapeDtypeStruct((bsz, seq, C_DIM), BF16),
                   jax.ShapeDtypeStruct((bsz, ng, RW_GW, C_HEAD), F32)],
        scratch_shapes=[pltpu.VMEM((ng, RW_GW, C_HEAD), F32), pltpu.VMEM((tb, C_DIM), F32)],
        compiler_params=_cparams(("parallel", "arbitrary")),
        name="rwkv_scan",
    )(tmat, kd, rd, bg, ge, bonus, gate, rb, m2, rkv, vk, rp["lnx_w"], rp["lnx_b"])
    return y, st.reshape(bsz, C_HEADS, C_HEAD, C_HEAD)


def _rwkv_step_kernel(r_ref, k_ref, v_ref, lr_ref, sh_ref, *rest):
    prm = tuple(x[...] for x in rest[:len(RW_NAMES)])
    lnw, lnb, s_ref, y_ref, so_ref = rest[len(RW_NAMES):]
    bb = STEP_BB
    sh = sh_ref[...]
    r, k2, v, kk, beta, ld, g, bonus = _rwkv_prep(
        r_ref[...], k_ref[...], v_ref[...], lr_ref[...],
        sh[:, :C_DIM], sh[:, C_DIM:2 * C_DIM], sh[:, 2 * C_DIM:3 * C_DIM], sh[:, 3 * C_DIM:], prm)
    w = jnp.exp(ld)
    v_t = v.T
    lane = _iota((C_DIM, LANES), 1)
    heads = [[x[:, h * C_HEAD:(h + 1) * C_HEAD] for x in (kk, w, beta, k2, r)] for h in range(C_HEADS)]

    def rows(i, b):
        return jnp.concatenate(
            [jnp.broadcast_to(heads[h][i][b:b + 1, :], (C_HEAD, C_HEAD)) for h in range(C_HEADS)], axis=0)

    y_t = jnp.zeros((C_DIM, LANES), F32)
    for b in range(bb):
        s = s_ref[b]
        sa = -jnp.sum(s * rows(0, b), axis=-1, keepdims=True)
        sn = s * rows(1, b) + sa * rows(2, b) + v_t[:, b:b + 1] * rows(3, b)
        so_ref[b] = sn
        y_t = jnp.where(lane == b, jnp.sum(sn * rows(4, b), axis=-1, keepdims=True), y_t)
    y = y_t.T[0:bb, :]
    y_ref[...] = _rwkv_out(y, bonus, g, lnw[...], lnb[...]).astype(BF16)


def _rwkv_step(proj, shift, s0, rp):
    bsz = proj.shape[0]
    bb = STEP_BB
    sflat = s0.reshape(bsz, C_DIM, C_HEAD)

    def col(width, off):
        return pl.BlockSpec((bb, width), lambda i: (i, off // width))

    prm = [rp[n] for n in RW_NAMES]
    y, sn = pl.pallas_call(
        _rwkv_step_kernel,
        grid=(bsz // bb,),
        in_specs=[col(C_DIM, P_R), col(C_DIM, P_K), col(C_DIM, P_V), col(C_LR, P_LR),
                  pl.BlockSpec((bb, C_COLS), lambda i: (i, 0))]
                 + [_full(x.shape) for x in prm] + [_full((1, C_DIM))] * 2
                 + [pl.BlockSpec((bb, C_DIM, C_HEAD), lambda i: (i, 0, 0))],
        out_specs=[pl.BlockSpec((bb, C_DIM), lambda i: (i, 0)),
                   pl.BlockSpec((bb, C_DIM, C_HEAD), lambda i: (i, 0, 0))],
        out_shape=[jax.ShapeDtypeStruct((bsz, C_DIM), BF16),
                   jax.ShapeDtypeStruct((bsz, C_DIM, C_HEAD), F32)],
        compiler_params=_cparams(("parallel",)),
        name="rwkv_step",
    )(proj, proj, proj, proj, shift, *prm, rp["lnx_w"], rp["lnx_b"], sflat)
    return y, sn.reshape(s0.shape)


def _layout_w_in_kernel(w_ref, o_ref):
    o = A_DIM + CONV_CH
    rows = w_ref.shape[0]
    o_ref[:, :o] = w_ref[:, :o].astype(BF16)
    o_ref[:, o:P_DT] = w_ref[:, o + A_HEADS:].astype(BF16)
    dt_tile = jnp.where(_iota((rows, LANES), 1) < A_HEADS, w_ref[:, o:o + LANES], 0.0)
    o_ref[:, P_DT:P_DT + LANES] = dt_tile.astype(BF16)
    o_ref[:, P_DT + LANES:] = jnp.zeros((rows, P_COLS - P_DT - LANES), BF16)


def _layout_w_in(w):
    k = w.shape[0]
    tr = 256
    return pl.pallas_call(
        _layout_w_in_kernel,
        grid=(k // tr,),
        in_specs=[pl.BlockSpec((tr, IN_COLS), lambda i: (i, 0))],
        out_specs=pl.BlockSpec((tr, P_COLS), lambda i: (i, 0)),
        out_shape=jax.ShapeDtypeStruct((k, P_COLS), BF16),
        compiler_params=_cparams(("parallel",)),
        name="layout_w_in",
    )(w)


def _xbc_cols(proj):
    return proj[..., P_X:P_X + CONV_CH]


def _moe_route(logits, n_tok):
    top_logit, top_idx = lax.top_k(logits, TOP_K)
    gate = jax.nn.softmax(top_logit, axis=-1)
    n_assign = n_tok * TOP_K
    flat_e = top_idx.reshape(-1).astype(jnp.int32)
    order = jnp.argsort(flat_e)
    onehot = jax.nn.one_hot(flat_e, N_EXPERTS, dtype=jnp.int32)
    counts = jnp.sum(onehot, axis=0)
    padded = (counts + MOE_BM - 1) // MOE_BM * MOE_BM
    pad_end = jnp.cumsum(padded)
    pad_start = pad_end - padded
    start = jnp.cumsum(counts) - counts
    rank = jnp.sum((jnp.cumsum(onehot, axis=0) - onehot) * onehot, axis=1)
    slot_of = (pad_start[flat_e] + rank).reshape(n_tok, TOP_K)
    n_blocks = -(-n_assign // MOE_BM) + N_EXPERTS
    n_slots = n_blocks * MOE_BM
    block_start = jnp.arange(n_blocks, dtype=jnp.int32) * MOE_BM
    nvalid = (pad_end[-1] // MOE_BM).astype(jnp.int32)
    bexp = jnp.minimum(jnp.searchsorted(pad_end, block_start, side="right"), N_EXPERTS - 1).astype(jnp.int32)
    slot_e = jnp.repeat(bexp, MOE_BM)
    within = jnp.arange(n_slots, dtype=jnp.int32) - pad_start[slot_e]
    src = order[jnp.clip(start[slot_e] + within, 0, n_assign - 1)] // TOP_K
    slot_tok = jnp.where(within < counts[slot_e], src, n_tok)
    left = counts[bexp] - (block_start - pad_start[bexp])
    nsub = jnp.clip((left + MOE_SUB - 1) // MOE_SUB, 0, MOE_BM // MOE_SUB).astype(jnp.int32)
    last = bexp[jnp.maximum(nvalid - 1, 0)]
    bexp = jnp.where(jnp.arange(n_blocks) < nvalid, bexp, last)
    first = jnp.concatenate([jnp.ones((1,), jnp.int32), (bexp[1:] != bexp[:-1]).astype(jnp.int32)])
    return gate, slot_tok, slot_of, bexp, nvalid.reshape(1), first, nsub


def _moe(h, logits, wg, wu, wd):
    n_tok = h.shape[0]
    gate, slot_tok, slot_of, bexp, nvalid, first, nsub = _moe_route(logits[:, :N_EXPERTS], n_tok)
    h_pad = jnp.concatenate([h, jnp.zeros((1, h.shape[1]), h.dtype)], axis=0)
    xb = h_pad[slot_tok]
    yb = _moe_experts(xb, bexp, nvalid, first, nsub, wg, wu, wd)
    return gate[:, 0:1] * yb[slot_of[:, 0]] + gate[:, 1:2] * yb[slot_of[:, 1]]


def kernel(x_prompt, x_sample, state_ssm, state_conv, state_hgrn, state_rwkv, state_shift, norm1_g, w_in, conv_w, conv_b, dt_bias, a_log, d_skip, ssm_norm_g, lb_logits, hgrn_norm_g, shift_mu, w0, w2, a0, a2, g2, k_k, k_a, r_k, lnx_w, lnx_b, w_out, norm2_g, ffn_w_gate, ffn_w_up, ffn_w_down, router_w, exp_w_gate, exp_w_up, exp_w_down, final_norm_g):
    depth = w_in.shape[0]
    bp, seq, _ = x_prompt.shape
    bs = x_sample.shape[0]
    tp = bp * seq
    lb_soft = jax.nn.softmax(lb_logits.astype(F32), axis=0)
    lower_bounds = jnp.clip(jnp.cumsum(lb_soft, axis=0) - lb_soft[0:1], 0.0, 1.0)

    xp = x_prompt.reshape(tp, D_MODEL)
    xs = x_sample.reshape(bs, D_MODEL)
    outs = {k: [] for k in ("p_ssm", "p_conv", "p_hgrn", "p_rwkv", "p_shift",
                            "s_ssm", "s_conv", "s_hgrn", "s_rwkv", "s_shift")}
    y_moe_p = y_moe_s = None
    for layer in range(depth):
        p = dict(conv_w=conv_w[layer], conv_b=conv_b[layer], dt_bias=dt_bias[layer], a_log=a_log[layer],
                 d_skip=d_skip[layer], ssm_norm_g=ssm_norm_g[layer], shift_mu=shift_mu[layer],
                 w0=w0[layer], w2=w2[layer], a0=a0[layer], a2=a2[layer], g2=g2[layer], k_k=k_k[layer],
                 k_a=k_a[layer], r_k=r_k[layer].reshape(-1), lnx_w=lnx_w[layer], lnx_b=lnx_b[layer])
        sp = _ssd_params(p)
        rp = _rwkv_params(p)
        lb = lower_bounds[layer]
        lbp = jnp.stack([1.0 - lb, jnp.log(lb), jnp.log1p(-lb)])
        hng = hgrn_norm_g[layer].reshape(1, B_HEAD)
        w_in_l = _layout_w_in(w_in[layer])
        w_out_l = w_out[layer].astype(BF16)
        j = layer // 2
        routed = layer % 2 == 1
        rw = None
        if routed:
            rw_f = jnp.pad(router_w[j], ((0, 0), (0, LANES - N_EXPERTS)))
            rw_hi = rw_f.astype(BF16)
            rw = jnp.concatenate([rw_hi, (rw_f - rw_hi.astype(F32)).astype(BF16)], axis=1)

        proj = _norm_matmul(xp, norm1_g[layer], w_in_l, 1024, 512).reshape(bp, seq, P_COLS)
        ya, ssm_p = _ssd_prompt(proj, sp)
        yb, hgrn_p = _hgrn_prompt(proj, lbp, hng)
        yc, rwkv_p = _rwkv_prompt(proj, rp)
        xp, hp, lp = _out_proj(ya.reshape(tp, A_DIM), yb.reshape(tp, B_DIM), yc.reshape(tp, C_DIM),
                               w_out_l, xp, norm2_g[layer], rw, 256)
        outs["p_ssm"].append(ssm_p)
        outs["p_conv"].append(_xbc_cols(proj[:, seq - (CONV_W - 1):, :]))
        outs["p_hgrn"].append(hgrn_p)
        outs["p_rwkv"].append(rwkv_p)
        outs["p_shift"].append(proj[:, seq - 1, P_R:P_R + C_COLS])

        projs = _norm_matmul(xs, norm1_g[layer], w_in_l, 128, 512)
        ya, ssm_s = _ssd_step(projs, state_conv[layer], state_ssm[layer], p, sp)
        yb, hgrn_s = _hgrn_step(projs, state_hgrn[layer], lbp, hng)
        yc, rwkv_s = _rwkv_step(projs, state_shift[layer], state_rwkv[layer], rp)
        xs, hs, ls = _out_proj(ya, yb, yc, w_out_l, xs, norm2_g[layer], rw, 128)
        outs["s_ssm"].append(ssm_s)
        outs["s_conv"].append(jnp.concatenate([state_conv[layer][:, 1:, :], _xbc_cols(projs)[:, None, :]], axis=1))
        outs["s_hgrn"].append(hgrn_s)
        outs["s_rwkv"].append(rwkv_s)
        outs["s_shift"].append(projs[:, P_R:P_R + C_COLS])

        if not routed:
            xp = _ffn(xp, hp, ffn_w_gate[j], ffn_w_up[j], ffn_w_down[j], 512)
            xs = _ffn(xs, hs, ffn_w_gate[j], ffn_w_up[j], ffn_w_down[j], 128)
        else:
            y_moe = _moe(jnp.concatenate([hp, hs], axis=0), jnp.concatenate([lp, ls], axis=0),
                         exp_w_gate[j], exp_w_up[j], exp_w_down[j])
            if layer == depth - 1:
                y_moe_p, y_moe_s = y_moe[:tp], y_moe[tp:]
            else:
                xp, xs = xp + y_moe[:tp], xs + y_moe[tp:]

    zp = y_moe_p if y_moe_p is not None else jnp.zeros_like(xp)
    zs = y_moe_s if y_moe_s is not None else jnp.zeros_like(xs)
    y_prompt = _final_norm(xp, zp, final_norm_g, 512).reshape(bp, seq, D_MODEL)
    y_sample = _final_norm(xs, zs, final_norm_g, 128).reshape(bs, 1, D_MODEL)
    st = {k: jnp.stack(v) for k, v in outs.items()}
    return (y_prompt, y_sample, st["p_ssm"], st["p_conv"], st["p_hgrn"], st["p_rwkv"], st["p_shift"],
            st["s_ssm"], st["s_conv"], st["s_hgrn"], st["s_rwkv"], st["s_shift"])
```

```python
import functools
import math

import jax
import jax.numpy as jnp
from jax import lax
from jax.experimental import pallas as pl
from jax.experimental.pallas import tpu as pltpu

F32 = jnp.float32
BF16 = jnp.bfloat16
HI = lax.Precision.HIGHEST

D_MODEL = 2048
A_DIM, A_HEAD, A_HEADS, A_GROUPS, A_STATE = 1024, 64, 16, 2, 128
A_GN = A_GROUPS * A_STATE
CONV_W = 4
CONV_CH = A_DIM + 2 * A_GN
B_DIM, B_HEAD, B_HEADS = 512, 128, 4
C_DIM, C_HEAD, C_HEADS = 512, 64, 8
C_LR_W, C_LR_A, C_LR_G = 64, 64, 128
C_LR = C_LR_W + C_LR_A + C_LR_G
C_COLS = 3 * C_DIM + C_LR
IN_COLS = A_DIM + CONV_CH + A_HEADS + 4 * B_DIM + C_COLS
D_FF, N_EXPERTS, TOP_K, E_FF = 5504, 8, 2, 7168
NORM_EPS, GN_EPS, L2_EPS = 1e-6, 64e-5, 1e-12

P_Z, P_X, P_B, P_C = 0, 1024, 2048, 2304
P_Q, P_F, P_I, P_G = 2560, 3072, 3584, 4096
P_R, P_K, P_V, P_LR, P_DT, P_COLS = 4608, 5120, 5632, 6144, 6400, 6656
LANES = 128
DT_PAD = P_COLS - P_DT - A_HEADS

SSD_CHUNK = 256
HGRN_BLOCK, HGRN_CHUNK = 256, 16
RWKV_BLOCK, RWKV_CHUNK = 256, 64
STEP_BB = 8

FFN_TF, FFN_TN = 512, 512
MOE_BM, MOE_SUB, MOE_TF, MOE_TN = 512, 256, 1024, 512

NT_DIMS = (((1,), (1,)), ((), ()))
TN_DIMS = (((0,), (0,)), ((), ()))


def _cparams(sem, vmem_mb=48):
    return pltpu.CompilerParams(dimension_semantics=sem, vmem_limit_bytes=vmem_mb * 1024 * 1024)


def _dot(a, b):
    return jnp.dot(a.astype(BF16), b.astype(BF16), preferred_element_type=F32)


def _dot_nt(a, b):
    return lax.dot_general(a.astype(BF16), b.astype(BF16), NT_DIMS, preferred_element_type=F32)


def _dot_tn(a, b):
    return lax.dot_general(a.astype(BF16), b.astype(BF16), TN_DIMS, preferred_element_type=F32)


def _dot_hi(a, b):
    return jnp.dot(a, b, precision=HI, preferred_element_type=F32)


def _dot_nt_hi(a, b):
    return lax.dot_general(a, b, NT_DIMS, precision=HI, preferred_element_type=F32)


def _dot_tn_hi(a, b):
    return lax.dot_general(a, b, TN_DIMS, precision=HI, preferred_element_type=F32)


def _silu(x):
    return x * jax.nn.sigmoid(x)


def _softplus(x):
    return jnp.maximum(x, 0.0) + jnp.log1p(jnp.exp(-jnp.abs(x)))


def _iota(shape, dim):
    return lax.broadcasted_iota(jnp.int32, shape, dim)


def _rms(x, g):
    return x * lax.rsqrt(jnp.mean(x * x, axis=-1, keepdims=True) + NORM_EPS) * g


def _full(shape):
    nd = len(shape)
    return pl.BlockSpec(shape, lambda *_: (0,) * nd)


def _norm_matmul_kernel(x_ref, g_ref, w_ref, o_ref, h_scr):
    @pl.when(pl.program_id(1) == 0)
    def _():
        h_scr[...] = _rms(x_ref[...], g_ref[...]).astype(BF16)

    o_ref[...] = jnp.dot(h_scr[...], w_ref[...], preferred_element_type=F32)


def _norm_matmul(x, g, w, tm, tn):
    m, k = x.shape
    n = w.shape[1]
    return pl.pallas_call(
        _norm_matmul_kernel,
        grid=(m // tm, n // tn),
        in_specs=[pl.BlockSpec((tm, k), lambda i, j: (i, 0)),
                  pl.BlockSpec((1, k), lambda i, j: (0, 0)),
                  pl.BlockSpec((k, tn), lambda i, j: (0, j))],
        out_specs=pl.BlockSpec((tm, tn), lambda i, j: (i, j)),
        out_shape=jax.ShapeDtypeStruct((m, n), F32),
        scratch_shapes=[pltpu.VMEM((tm, k), BF16)],
        compiler_params=_cparams(("parallel", "arbitrary")),
        name="norm_matmul",
    )(x, g.reshape(1, k), w)


def _out_proj_kernel(routed, ya_ref, yb_ref, yc_ref, w_ref, r_ref, g_ref, *rest):
    x = (r_ref[...]
         + jnp.dot(ya_ref[...], w_ref[0:A_DIM, :], preferred_element_type=F32)
         + jnp.dot(yb_ref[...], w_ref[A_DIM:A_DIM + B_DIM, :], preferred_element_type=F32)
         + jnp.dot(yc_ref[...], w_ref[A_DIM + B_DIM:, :], preferred_element_type=F32))
    h = _rms(x, g_ref[...])
    h_hi = h.astype(BF16)
    if routed:
        rw_ref, _, _, x_ref, h_ref, l_ref = rest
        h_lo = (h - h_hi.astype(F32)).astype(BF16)
        rw = rw_ref[...]
        l2 = jnp.dot(h_hi, rw, preferred_element_type=F32)
        l_ref[...] = (l2[:, :LANES] + l2[:, LANES:]
                      + jnp.dot(h_lo, rw[:, :LANES], preferred_element_type=F32))
        h_ref[...] = h
    else:
        x_ref, h_ref = rest
        h_ref[...] = h_hi
    x_ref[...] = x


def _out_proj(ya, yb, yc, w, res, g, tm, routed=None):
    m = ya.shape[0]
    n = w.shape[1]
    row = lambda width: pl.BlockSpec((tm, width), lambda i: (i, 0))
    in_specs = [row(A_DIM), row(B_DIM), row(C_DIM), _full(w.shape), row(n), _full((1, n))]
    args = [ya, yb, yc, w, res, g.reshape(1, n)]
    aliases = {}
    if routed is None:
        out_specs = [row(n), row(n)]
        out_shape = [jax.ShapeDtypeStruct((m, n), F32), jax.ShapeDtypeStruct((m, n), BF16)]
    else:
        rw, h_buf, l_buf, row0 = routed
        blk0 = row0 // tm
        off = lambda width: pl.BlockSpec((tm, width), lambda i: (i + blk0, 0))
        in_specs += [_full(rw.shape), pl.BlockSpec(memory_space=pl.ANY), pl.BlockSpec(memory_space=pl.ANY)]
        args += [rw, h_buf, l_buf]
        out_specs = [row(n), off(n), off(LANES)]
        out_shape = [jax.ShapeDtypeStruct((m, n), F32), jax.ShapeDtypeStruct(h_buf.shape, F32),
                     jax.ShapeDtypeStruct(l_buf.shape, F32)]
        aliases = {len(args) - 2: 1, len(args) - 1: 2}
    outs = pl.pallas_call(
        functools.partial(_out_proj_kernel, routed is not None),
        grid=(m // tm,),
        in_specs=in_specs, out_specs=out_specs, out_shape=out_shape,
        input_output_aliases=aliases,
        compiler_params=_cparams(("parallel",)),
        name="out_proj",
    )(*args)
    return outs if routed is not None else (*outs, None)


def _ffn_up_kernel(h_ref, wg_ref, wu_ref, o_ref, wg_s, wu_s):
    @pl.when(pl.program_id(1) == 0)
    def _():
        wg_s[...] = wg_ref[...].astype(BF16)
        wu_s[...] = wu_ref[...].astype(BF16)

    h = h_ref[...]
    a = jnp.dot(h, wg_s[...], preferred_element_type=F32)
    u = jnp.dot(h, wu_s[...], preferred_element_type=F32)
    o_ref[...] = (_silu(a) * u).astype(BF16)


def _ffn_down_kernel(h_ref, wd_ref, x_ref, o_ref, wd_s):
    @pl.when(pl.program_id(1) == 0)
    def _():
        wd_s[...] = wd_ref[...].astype(BF16)

    o_ref[...] = x_ref[...] + jnp.dot(h_ref[...], wd_s[...], preferred_element_type=F32)


def _ffn(x, h, wg, wu, wd, tm):
    m, k = h.shape
    ff = wg.shape[1]
    tf, tn = FFN_TF, FFN_TN
    tu = 2 * tm if m % (2 * tm) == 0 else tm
    act = pl.pallas_call(
        _ffn_up_kernel,
        grid=(pl.cdiv(ff, tf), m // tu),
        in_specs=[pl.BlockSpec((tu, k), lambda f, i: (i, 0)),
                  pl.BlockSpec((k, tf), lambda f, i: (0, f)),
                  pl.BlockSpec((k, tf), lambda f, i: (0, f))],
        out_specs=pl.BlockSpec((tu, tf), lambda f, i: (i, f)),
        out_shape=jax.ShapeDtypeStruct((m, ff), BF16),
        scratch_shapes=[pltpu.VMEM((k, tf), BF16)] * 2,
        compiler_params=_cparams(("arbitrary", "arbitrary")),
        name="ffn_up",
    )(h, wg, wu)
    return pl.pallas_call(
        _ffn_down_kernel,
        grid=(k // tn, m // tm),
        in_specs=[pl.BlockSpec((tm, ff), lambda n, i: (i, 0)),
                  pl.BlockSpec((ff, tn), lambda n, i: (0, n)),
                  pl.BlockSpec((tm, tn), lambda n, i: (i, n))],
        out_specs=pl.BlockSpec((tm, tn), lambda n, i: (i, n)),
        out_shape=jax.ShapeDtypeStruct((m, k), F32),
        scratch_shapes=[pltpu.VMEM((ff, tn), BF16)],
        compiler_params=_cparams(("arbitrary", "arbitrary")),
        name="ffn_down",
    )(act, wd, x)


def _moe_rows(nsub, o_ref, compute):
    bm = o_ref.shape[0]
    for ns in range(bm // MOE_SUB + 1):
        @pl.when(nsub == ns)
        def _(rows=ns * MOE_SUB):
            if rows:
                o_ref[0:rows, :] = compute(rows).astype(o_ref.dtype)
            if rows < bm:
                o_ref[rows:, :] = jnp.zeros((bm - rows, o_ref.shape[1]), o_ref.dtype)


def _moe_up_kernel(bexp_ref, nv_ref, first_ref, nsub_ref, x_ref, wg_ref, wu_ref, o_ref, wg_s, wu_s):
    b = pl.program_id(1)

    @pl.when(first_ref[b] == 1)
    def _():
        wg_s[...] = wg_ref[...].astype(BF16)
        wu_s[...] = wu_ref[...].astype(BF16)

    def compute(rows):
        x = x_ref[0:rows, :]
        a = jnp.dot(x, wg_s[...], preferred_element_type=F32)
        u = jnp.dot(x, wu_s[...], preferred_element_type=F32)
        return _silu(a) * u

    _moe_rows(nsub_ref[b], o_ref, compute)


def _moe_down_kernel(bexp_ref, nv_ref, first_ref, nsub_ref, h_ref, wd_ref, o_ref, wd_s):
    b = pl.program_id(1)

    @pl.when(first_ref[b] == 1)
    def _():
        wd_s[...] = wd_ref[...].astype(BF16)

    _moe_rows(nsub_ref[b], o_ref,
              lambda rows: jnp.dot(h_ref[0:rows, :], wd_s[...], preferred_element_type=F32))


def _moe_experts(xb, bexp, nvalid, first, nsub, wg, wu, wd):
    n_slots, k = xb.shape
    ff = wg.shape[2]
    nb = n_slots // MOE_BM
    tf, tn = MOE_TF, MOE_TN

    def blk(b, nv):
        return jnp.minimum(b, nv[0] - 1)

    act = pl.pallas_call(
        _moe_up_kernel,
        grid_spec=pltpu.PrefetchScalarGridSpec(
            num_scalar_prefetch=4,
            grid=(ff // tf, nb),
            in_specs=[pl.BlockSpec((MOE_BM, k), lambda f, b, be, nv, fr, ns: (blk(b, nv), 0)),
                      pl.BlockSpec((None, k, tf), lambda f, b, be, nv, fr, ns: (be[b], 0, f)),
                      pl.BlockSpec((None, k, tf), lambda f, b, be, nv, fr, ns: (be[b], 0, f))],
            out_specs=pl.BlockSpec((MOE_BM, tf), lambda f, b, be, nv, fr, ns: (b, f)),
            scratch_shapes=[pltpu.VMEM((k, tf), BF16)] * 2,
        ),
        out_shape=jax.ShapeDtypeStruct((n_slots, ff), BF16),
        compiler_params=_cparams(("arbitrary", "arbitrary"), 56),
        name="moe_up",
    )(bexp, nvalid, first, nsub, xb, wg, wu)
    per = MOE_BM // MOE_SUB
    bexp_d = jnp.repeat(bexp, per)
    first_d = jnp.concatenate([jnp.ones((1,), jnp.int32), (bexp_d[1:] != bexp_d[:-1]).astype(jnp.int32)])
    nsub_d = jnp.clip(jnp.repeat(nsub, per) - jnp.tile(jnp.arange(per, dtype=jnp.int32), nb), 0, 1)
    ids = jnp.arange(nb * per, dtype=jnp.int32)
    src_d = lax.cummax(jnp.where(nsub_d > 0, ids, 0))
    return pl.pallas_call(
        _moe_down_kernel,
        grid_spec=pltpu.PrefetchScalarGridSpec(
            num_scalar_prefetch=4,
            grid=(k // tn, nb * per),
            in_specs=[pl.BlockSpec((MOE_SUB, ff), lambda n, b, be, sb, fr, ns: (sb[b], 0)),
                      pl.BlockSpec((None, ff, tn), lambda n, b, be, nv, fr, ns: (be[b], 0, n))],
            out_specs=pl.BlockSpec((MOE_SUB, tn), lambda n, b, be, nv, fr, ns: (b, n)),
            scratch_shapes=[pltpu.VMEM((ff, tn), BF16)],
        ),
        out_shape=jax.ShapeDtypeStruct((n_slots, k), F32),
        compiler_params=_cparams(("arbitrary", "arbitrary"), 56),
        name="moe_down",
    )(bexp_d, src_d, first_d, nsub_d, act, wd)


def _final_norm_kernel(x_ref, g_ref, o_ref):
    o_ref[...] = _rms(x_ref[...], g_ref[...])


def _final_norm(x, g, tm):
    m, k = x.shape
    return pl.pallas_call(
        _final_norm_kernel,
        grid=(m // tm,),
        in_specs=[pl.BlockSpec((tm, k), lambda i: (i, 0)),
                  pl.BlockSpec((1, k), lambda i: (0, 0))],
        out_specs=pl.BlockSpec((tm, k), lambda i: (i, 0)),
        out_shape=jax.ShapeDtypeStruct((m, k), F32),
        compiler_params=_cparams(("parallel",)),
        name="final_norm",
    )(x, g.reshape(1, k))


def _row_copy(src_hbm, row, dst, r, sem):
    return pltpu.make_async_copy(src_hbm.at[pl.ds(row, 1), :], dst.at[pl.ds(r, 1), :], sem)


def _moe_gather_kernel(idx_ref, h_hbm, o_ref, buf, sem):
    rows = buf.shape[0]
    base = pl.program_id(0) * rows

    def issue(r, carry):
        _row_copy(h_hbm, idx_ref[base + r], buf, r, sem).start()
        return carry

    lax.fori_loop(0, rows, issue, 0)
    pltpu.make_async_copy(buf, buf, sem).wait()
    o_ref[...] = buf[...].astype(BF16)


def _moe_gather(slot_tok, h):
    n_slots = slot_tok.shape[0]
    k = h.shape[1]
    rows = MOE_SUB
    return pl.pallas_call(
        _moe_gather_kernel,
        grid_spec=pltpu.PrefetchScalarGridSpec(
            num_scalar_prefetch=1,
            grid=(n_slots // rows,),
            in_specs=[pl.BlockSpec(memory_space=pl.ANY)],
            out_specs=pl.BlockSpec((rows, k), lambda i, idx: (i, 0)),
            scratch_shapes=[pltpu.VMEM((rows, k), F32), pltpu.SemaphoreType.DMA],
        ),
        out_shape=jax.ShapeDtypeStruct((n_slots, k), BF16),
        compiler_params=_cparams(("arbitrary",)),
        name="moe_gather",
    )(slot_tok, h)


def _moe_combine_kernel(final, s0_ref, s1_ref, x_ref, gate_ref, g_ref, yb_hbm, o_ref, buf, sem):
    rows = x_ref.shape[0]
    base = pl.program_id(0) * rows

    def issue(r, carry):
        _row_copy(yb_hbm, s0_ref[base + r], buf.at[0], r, sem).start()
        _row_copy(yb_hbm, s1_ref[base + r], buf.at[1], r, sem).start()
        return carry

    lax.fori_loop(0, rows, issue, 0)
    pltpu.make_async_copy(buf, buf, sem).wait()
    gate = gate_ref[...]
    y = x_ref[...] + gate[:, 0:1] * buf[0] + gate[:, 1:2] * buf[1]
    o_ref[...] = _rms(y, g_ref[...]) if final else y


def _moe_combine(x, yb, slot_of, gate, g, final):
    m, k = x.shape
    rows = 128
    gate_p = jnp.pad(gate, ((0, 0), (0, LANES - TOP_K)))
    return pl.pallas_call(
        functools.partial(_moe_combine_kernel, final),
        grid_spec=pltpu.PrefetchScalarGridSpec(
            num_scalar_prefetch=2,
            grid=(m // rows,),
            in_specs=[pl.BlockSpec((rows, k), lambda i, s0, s1: (i, 0)),
                      pl.BlockSpec((rows, LANES), lambda i, s0, s1: (i, 0)),
                      pl.BlockSpec((1, k), lambda i, s0, s1: (0, 0)),
                      pl.BlockSpec(memory_space=pl.ANY)],
            out_specs=pl.BlockSpec((rows, k), lambda i, s0, s1: (i, 0)),
            scratch_shapes=[pltpu.VMEM((TOP_K, rows, k), F32), pltpu.SemaphoreType.DMA],
        ),
        out_shape=jax.ShapeDtypeStruct((m, k), F32),
        compiler_params=_cparams(("arbitrary",)),
        name="moe_combine",
    )(slot_of[:, 0], slot_of[:, 1], x, gate_p, g.reshape(1, k), yb)


def _ssd_gate_norm(y, z, ng):
    y = y * _silu(z)
    gw = A_DIM // A_GROUPS
    outs = []
    for g in range(A_GROUPS):
        yg = y[:, g * gw:(g + 1) * gw]
        outs.append(yg * lax.rsqrt(jnp.mean(yg * yg, axis=-1, keepdims=True) + NORM_EPS))
    return jnp.concatenate(outs, axis=-1) * ng


def _ssd_prompt_kernel(z_ref, x_ref, b_ref, c_ref, dt_ref, cwx, cwb, cwc, cbx, cbb, cbc,
                       dtb, aneg, dsk, ng, y_ref, st_ref, h_scr, xpx, xpb, xpc, yacc):
    c = pl.program_id(1)
    ch = SSD_CHUNK

    @pl.when(c == 0)
    def _():
        h_scr[...] = jnp.zeros_like(h_scr)
        xpx[0:8, :] = jnp.zeros((8, A_DIM), F32)
        xpb[0:8, :] = jnp.zeros((8, A_STATE * A_GROUPS), F32)
        xpc[0:8, :] = jnp.zeros((8, A_STATE * A_GROUPS), F32)

    xpx[8:8 + ch, :] = x_ref[...]
    xpb[8:8 + ch, :] = b_ref[...]
    xpc[8:8 + ch, :] = c_ref[...]

    def conv(xp, cw, cb):
        acc = cb[...] + xp[pl.ds(8 - (CONV_W - 1), ch), :] * cw[0:1, :]
        for tap in range(1, CONV_W):
            acc = acc + xp[pl.ds(8 - (CONV_W - 1) + tap, ch), :] * cw[tap:tap + 1, :]
        return _silu(acc)

    ux = conv(xpx, cwx, cbx)
    ub = conv(xpb, cwb, cbb)
    uc = conv(xpc, cwc, cbc)
    xpx[0:8, :] = xpx[ch:ch + 8, :]
    xpb[0:8, :] = xpb[ch:ch + 8, :]
    xpc[0:8, :] = xpc[ch:ch + 8, :]

    dt = _softplus(dt_ref[...] + dtb[...])
    la = dt * aneg[...]
    tri = (_iota((ch, ch), 1) <= _iota((ch, ch), 0)).astype(F32)
    cum = _dot_hi(tri, la)
    eye = (_iota((A_HEADS, LANES), 0) == _iota((A_HEADS, LANES), 1)).astype(F32)
    cum_r = _dot_nt_hi(eye, cum)
    dt_r = _dot_nt_hi(eye, dt)
    cl = cum[ch - 1:ch, :]
    tail = jnp.exp(cl - cum) * dt
    ecum = jnp.exp(cum)
    ecl = jnp.exp(cl)
    causal = _iota((ch, ch), 1) <= _iota((ch, ch), 0)
    rep = A_HEADS // A_GROUPS

    for g in range(A_GROUPS):
        bg = ub[:, g * A_STATE:(g + 1) * A_STATE].astype(BF16)
        cg = uc[:, g * A_STATE:(g + 1) * A_STATE].astype(BF16)
        cb_g = _dot_nt(cg, bg)
        for h in range(g * rep, (g + 1) * rep):
            hs = slice(h * A_HEAD, (h + 1) * A_HEAD)
            seg = cum[:, h:h + 1] - cum_r[h:h + 1, :]
            w = jnp.exp(jnp.where(causal, seg, -jnp.inf)) * cb_g * dt_r[h:h + 1, :]
            xh = ux[:, hs]
            hst = h_scr[h]
            y_h = _dot(w, xh) + _dot_nt(cg, hst) * ecum[:, h:h + 1]
            yacc[:, hs] = y_h
            h_scr[h] = hst * ecl[:, h:h + 1] + _dot_tn(xh * tail[:, h:h + 1], bg)

    y = yacc[...] + dsk[...] * ux
    y_ref[...] = _ssd_gate_norm(y, z_ref[...], ng[...]).astype(BF16)

    @pl.when(c == pl.num_programs(1) - 1)
    def _():
        st_ref[...] = h_scr[...]


def _ssd_params(p):
    cw, cb = p["conv_w"], p["conv_b"].reshape(1, CONV_CH)
    pad = LANES - A_HEADS
    return dict(
        cwx=cw[:, :A_DIM], cwb=cw[:, A_DIM:A_DIM + A_GN], cwc=cw[:, A_DIM + A_GN:],
        cbx=cb[:, :A_DIM], cbb=cb[:, A_DIM:A_DIM + A_GN], cbc=cb[:, A_DIM + A_GN:],
        dtb=jnp.pad(p["dt_bias"], (0, pad)).reshape(1, LANES),
        aneg=jnp.pad(-jnp.exp(p["a_log"]), (0, pad)).reshape(1, LANES),
        dsk=jnp.repeat(p["d_skip"], A_HEAD).reshape(1, A_DIM),
        ng=p["ssm_norm_g"].reshape(1, A_DIM),
    )


def _ssd_prompt(proj, sp):
    bsz, seq, _ = proj.shape
    ch = SSD_CHUNK
    gn = A_GN

    def col(width, off):
        return pl.BlockSpec((None, ch, width), lambda b, c: (b, c, off // width))

    names = ("cwx", "cwb", "cwc", "cbx", "cbb", "cbc", "dtb", "aneg", "dsk", "ng")
    y, st = pl.pallas_call(
        _ssd_prompt_kernel,
        grid=(bsz, seq // ch),
        in_specs=[col(A_DIM, P_Z), col(A_DIM, P_X), col(gn, P_B), col(gn, P_C), col(LANES, P_DT)]
                 + [_full(sp[n].shape) for n in names],
        out_specs=[pl.BlockSpec((None, ch, A_DIM), lambda b, c: (b, c, 0)),
                   pl.BlockSpec((None, A_HEADS, A_HEAD, A_STATE), lambda b, c: (b, 0, 0, 0))],
        out_shape=[jax.ShapeDtypeStruct((bsz, seq, A_DIM), BF16),
                   jax.ShapeDtypeStruct((bsz, A_HEADS, A_HEAD, A_STATE), F32)],
        scratch_shapes=[pltpu.VMEM((A_HEADS, A_HEAD, A_STATE), F32),
                        pltpu.VMEM((ch + 8, A_DIM), F32),
                        pltpu.VMEM((ch + 8, gn), F32),
                        pltpu.VMEM((ch + 8, gn), F32),
                        pltpu.VMEM((ch, A_DIM), F32)],
        compiler_params=_cparams(("parallel", "arbitrary")),
        name="ssd_prompt",
    )(proj, proj, proj, proj, proj, *[sp[n] for n in names])
    return y, st


def _ssd_step_kernel(z_ref, x_ref, b_ref, c_ref, dt_ref, cs_ref, cw, cb, dtb, anegx, expand, dsk, ng,
                     h_ref, y_ref, ho_ref, y_scr):
    bb = STEP_BB
    xbc = (x_ref[...], b_ref[...], c_ref[...])
    offs = (0, A_DIM, A_DIM + A_GN, CONV_CH)
    u = []
    for i in range(3):
        sl = slice(offs[i], offs[i + 1])
        acc = cb[:, sl] + xbc[i] * cw[CONV_W - 1:CONV_W, sl]
        for tap in range(CONV_W - 1):
            acc = acc + cs_ref[tap][:, sl] * cw[tap:tap + 1, sl]
        u.append(_silu(acc))
    ux, ub, uc = u
    dt = _softplus(dt_ref[...] + dtb[...])
    dtx = _dot_hi(dt, expand[...])
    dec_t = jnp.exp(dtx * anegx[...]).T
    xdt_t = (ux * dtx).T
    gw = A_DIM // A_GROUPS
    for b in range(bb):
        for g in range(A_GROUPS):
            rs = slice(g * gw, (g + 1) * gw)
            ns = slice(g * A_STATE, (g + 1) * A_STATE)
            hg = h_ref[b, rs, :] * dec_t[rs, b:b + 1] + xdt_t[rs, b:b + 1] * ub[b:b + 1, ns]
            ho_ref[b, rs, :] = hg
            y_scr[b:b + 1, rs] = _dot_nt(uc[:, ns], hg)[b:b + 1, :]
    y = y_scr[...] + dsk[...] * ux
    y_ref[...] = _ssd_gate_norm(y, z_ref[...], ng[...]).astype(BF16)


def _ssd_step(proj, conv_state, h0, p, sp):
    bsz = proj.shape[0]
    bb = STEP_BB
    cs = jnp.swapaxes(conv_state, 0, 1)
    hflat = h0.reshape(bsz, A_DIM, A_STATE)
    expand = (jnp.arange(LANES)[:, None] == (jnp.arange(A_DIM) // A_HEAD)[None, :]).astype(F32)
    anegx = jnp.repeat(-jnp.exp(p["a_log"]), A_HEAD).reshape(1, A_DIM)
    cw, cb = p["conv_w"], p["conv_b"].reshape(1, CONV_CH)

    def col(width, off):
        return pl.BlockSpec((bb, width), lambda i: (i, off // width))

    y, hn = pl.pallas_call(
        _ssd_step_kernel,
        grid=(bsz // bb,),
        in_specs=[col(A_DIM, P_Z), col(A_DIM, P_X), col(A_GN, P_B), col(A_GN, P_C), col(LANES, P_DT),
                  pl.BlockSpec((CONV_W - 1, bb, CONV_CH), lambda i: (0, i, 0)),
                  _full(cw.shape), _full(cb.shape), _full(sp["dtb"].shape), _full(anegx.shape),
                  _full(expand.shape), _full(sp["dsk"].shape), _full(sp["ng"].shape),
                  pl.BlockSpec((bb, A_DIM, A_STATE), lambda i: (i, 0, 0))],
        out_specs=[pl.BlockSpec((bb, A_DIM), lambda i: (i, 0)),
                   pl.BlockSpec((bb, A_DIM, A_STATE), lambda i: (i, 0, 0))],
        out_shape=[jax.ShapeDtypeStruct((bsz, A_DIM), BF16),
                   jax.ShapeDtypeStruct((bsz, A_DIM, A_STATE), F32)],
        scratch_shapes=[pltpu.VMEM((bb, A_DIM), F32)],
        compiler_params=_cparams(("parallel",)),
        name="ssd_step",
    )(proj, proj, proj, proj, proj, cs, cw, cb, sp["dtb"], anegx, expand, sp["dsk"], sp["ng"], hflat)
    return y, hn.reshape(h0.shape)


def _hgrn_gates(q, f, lbp):
    qf = _silu(q)
    kf = lbp[0:1, :] * jax.nn.sigmoid(-f)
    log_sig = jnp.minimum(f, 0.0) - jnp.log1p(jnp.exp(-jnp.abs(f)))
    a = lbp[1:2, :]
    b = lbp[2:3, :] + log_sig
    logf = jnp.maximum(a, b) + jnp.log1p(jnp.exp(-jnp.abs(a - b)))
    return qf, kf, logf


def _hgrn_out(o, g, ng):
    outs = []
    for h in range(B_HEADS):
        oh = o[:, h * B_HEAD:(h + 1) * B_HEAD]
        outs.append(oh * lax.rsqrt(jnp.mean(oh * oh, axis=-1, keepdims=True) + NORM_EPS) * ng)
    return jnp.concatenate(outs, axis=-1) * _silu(g)


def _hgrn_prompt_kernel(q_ref, f_ref, i_ref, g_ref, lbp, ng, o_ref, st_ref,
                        s_scr, cum_scr, q_scr, k_scr, o_scr):
    c = pl.program_id(1)
    tb, cs = HGRN_BLOCK, HGRN_CHUNK

    @pl.when(c == 0)
    def _():
        s_scr[...] = jnp.zeros_like(s_scr)

    qf, kf, logf = _hgrn_gates(q_ref[...], f_ref[...], lbp[...])
    r_i, c_i = _iota((tb, tb), 0), _iota((tb, tb), 1)
    bd = ((r_i // cs == c_i // cs) & (c_i <= r_i)).astype(F32)
    cum_scr[...] = _dot_hi(bd, logf)
    q_scr[...] = qf
    k_scr[...] = kf
    rows = _iota((cs, B_HEAD), 0)

    def chunk(cc, carry):
        r0 = pl.multiple_of(cc * cs, cs)
        for h in range(B_HEADS):
            hs = slice(h * B_HEAD, (h + 1) * B_HEAD)
            cu = cum_scr[pl.ds(r0, cs), hs]
            q = q_scr[pl.ds(r0, cs), hs]
            k = k_scr[pl.ds(r0, cs), hs]
            v = i_ref[pl.ds(r0, cs), hs]
            st = s_scr[h]
            o = _dot_nt(q * jnp.exp(cu), st)
            for s in range(cs):
                d = jnp.exp(jnp.where(rows >= s, cu - cu[s:s + 1, :], -jnp.inf))
                att = jnp.sum(q * k[s:s + 1, :] * d, axis=-1, keepdims=True)
                o = o + att * v[s:s + 1, :]
            o_scr[pl.ds(r0, cs), hs] = o
            cl = cu[cs - 1:cs, :]
            s_scr[h] = st * jnp.exp(cl) + _dot_tn(v, k * jnp.exp(cl - cu))
        return carry

    lax.fori_loop(0, tb // cs, chunk, 0)
    o_ref[...] = _hgrn_out(o_scr[...], g_ref[...], ng[...]).astype(BF16)

    @pl.when(c == pl.num_programs(1) - 1)
    def _():
        for h in range(B_HEADS):
            st_ref[h] = s_scr[h].T


def _hgrn_prompt(proj, lbp, ng):
    bsz, seq, _ = proj.shape
    tb = HGRN_BLOCK

    def col(off):
        return pl.BlockSpec((None, tb, B_DIM), lambda b, c: (b, c, off // B_DIM))

    o, st = pl.pallas_call(
        _hgrn_prompt_kernel,
        grid=(bsz, seq // tb),
        in_specs=[col(P_Q), col(P_F), col(P_I), col(P_G), _full(lbp.shape), _full(ng.shape)],
        out_specs=[pl.BlockSpec((None, tb, B_DIM), lambda b, c: (b, c, 0)),
                   pl.BlockSpec((None, B_HEADS, B_HEAD, B_HEAD), lambda b, c: (b, 0, 0, 0))],
        out_shape=[jax.ShapeDtypeStruct((bsz, seq, B_DIM), BF16),
                   jax.ShapeDtypeStruct((bsz, B_HEADS, B_HEAD, B_HEAD), F32)],
        scratch_shapes=[pltpu.VMEM((B_HEADS, B_HEAD, B_HEAD), F32)]
                       + [pltpu.VMEM((tb, B_DIM), F32)] * 4,
        compiler_params=_cparams(("parallel", "arbitrary")),
        name="hgrn_prompt",
    )(proj, proj, proj, proj, lbp, ng)
    return o, st


def _hgrn_step_kernel(q_ref, f_ref, i_ref, g_ref, lbp, ng, s_ref, o_ref, so_ref, o_scr):
    bb = STEP_BB
    qf, kf, logf = _hgrn_gates(q_ref[...], f_ref[...], lbp[...])
    vf = i_ref[...]
    dec_t = jnp.exp(logf).T
    k_t = kf.T
    q_t = qf.T
    for b in range(bb):
        for h in range(B_HEADS):
            hs = slice(h * B_HEAD, (h + 1) * B_HEAD)
            sh = s_ref[b, hs, :] * dec_t[hs, b:b + 1] + k_t[hs, b:b + 1] * vf[b:b + 1, hs]
            so_ref[b, hs, :] = sh
            o_scr[b:b + 1, hs] = jnp.sum(q_t[hs, b:b + 1] * sh, axis=0, keepdims=True)
    o_ref[...] = _hgrn_out(o_scr[...], g_ref[...], ng[...]).astype(BF16)


def _hgrn_step(proj, s0, lbp, ng):
    bsz = proj.shape[0]
    bb = STEP_BB
    sflat = s0.reshape(bsz, B_DIM, B_HEAD)

    def col(off):
        return pl.BlockSpec((bb, B_DIM), lambda i: (i, off // B_DIM))

    o, sn = pl.pallas_call(
        _hgrn_step_kernel,
        grid=(bsz // bb,),
        in_specs=[col(P_Q), col(P_F), col(P_I), col(P_G), _full(lbp.shape), _full(ng.shape),
                  pl.BlockSpec((bb, B_DIM, B_HEAD), lambda i: (i, 0, 0))],
        out_specs=[pl.BlockSpec((bb, B_DIM), lambda i: (i, 0)),
                   pl.BlockSpec((bb, B_DIM, B_HEAD), lambda i: (i, 0, 0))],
        out_shape=[jax.ShapeDtypeStruct((bsz, B_DIM), BF16),
                   jax.ShapeDtypeStruct((bsz, B_DIM, B_HEAD), F32)],
        scratch_shapes=[pltpu.VMEM((bb, B_DIM), F32)],
        compiler_params=_cparams(("parallel",)),
        name="hgrn_step",
    )(proj, proj, proj, proj, lbp, ng, sflat)
    return o, sn.reshape(s0.shape)


RW_NAMES = ("mu_r", "mu_k", "mu_v", "mu_lr", "w0", "w2", "a0", "a2", "g2", "k_k", "k_a", "r_k")


def _rwkv_params(p):
    mu = p["shift_mu"].reshape(1, C_COLS)
    row = lambda a: a.reshape(1, C_DIM)
    return dict(
        mu_r=mu[:, :C_DIM], mu_k=mu[:, C_DIM:2 * C_DIM], mu_v=mu[:, 2 * C_DIM:3 * C_DIM],
        mu_lr=mu[:, 3 * C_DIM:],
        w0=row(p["w0"]), w2=p["w2"].astype(BF16), a0=row(p["a0"]), a2=p["a2"].astype(BF16),
        g2=p["g2"].astype(BF16), k_k=row(p["k_k"]), k_a=row(p["k_a"]), r_k=row(p["r_k"]),
        lnx_w=row(p["lnx_w"]), lnx_b=row(p["lnx_b"]),
    )


def _head_sum(x):
    return [jnp.sum(x[:, h * C_HEAD:(h + 1) * C_HEAD], axis=-1, keepdims=True) for h in range(C_HEADS)]


def _head_bcast(cols, fn=lambda c: c):
    rows = cols[0].shape[0]
    return jnp.concatenate([jnp.broadcast_to(fn(c), (rows, C_HEAD)) for c in cols], axis=-1)


def _rwkv_prep(r, k, v, lr, pr, pk, pv, plr, prm):
    mu_r, mu_k, mu_v, mu_lr, w0, w2, a0, a2, g2, k_k, k_a, r_k = prm
    r = r + (pr - r) * mu_r
    k = k + (pk - k) * mu_k
    v = v + (pv - v) * mu_v
    lr = lr + (plr - lr) * mu_lr
    wl, al, gl = lr[:, :C_LR_W], lr[:, C_LR_W:C_LR_W + C_LR_A], lr[:, C_LR_W + C_LR_A:]
    wraw = -_softplus(-(w0 + _dot(jnp.tanh(wl), w2))) - 0.5
    ld = -jnp.exp(wraw)
    a = jax.nn.sigmoid(a0 + _dot(al, a2))
    g = _dot(jax.nn.sigmoid(gl), g2)
    kk = k * k_k
    kk = kk * _head_bcast(_head_sum(kk * kk), lambda n2: 1.0 / jnp.maximum(jnp.sqrt(n2), L2_EPS))
    k2 = k * (1.0 + (a - 1.0) * k_a)
    bonus = _head_bcast(_head_sum(r * k2 * r_k)) * v
    return r, k2, v, kk, kk * a, ld, g, bonus


def _rwkv_out(y, bonus, g, lnx_w, lnx_b):
    n = float(C_HEAD)
    mean = _head_bcast(_head_sum(y), lambda s: s / n)
    yc = y - mean
    rstd = _head_bcast(_head_sum(yc * yc), lambda s: lax.rsqrt(s / n + GN_EPS))
    return (yc * rstd * lnx_w + lnx_b + bonus) * g


RW_GH = 4
RW_GW = RW_GH * C_HEAD


def _bd_mask():
    return _iota((RW_GW, RW_GW), 0) // C_HEAD == _iota((RW_GW, RW_GW), 1) // C_HEAD


def _bd_rows(a):
    return jnp.where(_bd_mask(), jnp.concatenate([a] * RW_GH, axis=0), 0.0)


def _bd_lanes(s):
    return jnp.where(_bd_mask(), jnp.concatenate([s] * RW_GH, axis=1), 0.0)


def _bd_fold(m):
    m = jnp.where(_bd_mask(), m, 0.0)
    out = m[:, :C_HEAD]
    for h in range(1, RW_GH):
        out = out + m[:, h * C_HEAD:(h + 1) * C_HEAD]
    return out


def _to_stack(a):
    return jnp.concatenate([a[:, h * C_HEAD:(h + 1) * C_HEAD] for h in range(RW_GH)], axis=0)


def _to_all(s):
    return jnp.concatenate([s[h * C_HEAD:(h + 1) * C_HEAD, :] for h in range(RW_GH)], axis=1)


def _rwkv_a_kernel(r_ref, k_ref, v_ref, lr_ref, rp_ref, kp_ref, vp_ref, lrp_ref, *rest):
    prm = tuple(x[...] for x in rest[:len(RW_NAMES)])
    (kd_ref, rd_ref, bg_ref, ge_ref, bonus_ref, gate_ref, aab_ref, rb_ref, m2_ref, rkv_ref, vk_ref,
     xs_scr, b1_scr, b2_scr, b3_scr, b4_scr) = rest[len(RW_NAMES):]
    c = pl.program_id(1)
    tb, cs = RWKV_BLOCK, RWKV_CHUNK
    first = (c == 0)

    def prev(cur_ref, prev_ref, width):
        xs_scr[8:8 + tb, 0:width] = cur_ref[...]
        xs_scr[7:8, 0:width] = jnp.where(first, 0.0, prev_ref[7:8, :])
        return xs_scr[pl.ds(7, tb), 0:width]

    pr = prev(r_ref, rp_ref, C_DIM)
    pk = prev(k_ref, kp_ref, C_DIM)
    pv = prev(v_ref, vp_ref, C_DIM)
    plr = prev(lr_ref, lrp_ref, C_LR)
    r, k2, v, kk, beta, ld, g, bonus = _rwkv_prep(
        r_ref[...], k_ref[...], v_ref[...], lr_ref[...], pr, pk, pv, plr, prm)
    bonus_ref[...] = bonus
    gate_ref[...] = g

    r_i, c_i = _iota((tb, tb), 0), _iota((tb, tb), 1)
    same = (r_i // cs == c_i // cs)
    lcum = _dot_hi((same & (c_i <= r_i)).astype(F32), ld)
    lend = _dot_hi(same.astype(F32), ld)
    einv = jnp.exp(-lcum)
    eend = jnp.exp(lend - lcum)
    ge_ref[...] = jnp.exp(lend)
    kd_ref[...] = kk * jnp.exp(lcum - ld)
    rd_ref[...] = r * jnp.exp(lcum)
    bg_ref[...] = beta * eend
    b1_scr[...] = beta * einv
    b2_scr[...] = k2 * einv
    b3_scr[...] = k2 * eend
    b4_scr[...] = v
    gw = RW_GW
    tpos, spos = _iota((gw, cs), 0) % cs, _iota((gw, cs), 1)
    strict, incl = tpos > spos, tpos >= spos
    hc = C_HEAD

    def chunk(j, carry):
        r0 = pl.multiple_of(j * cs, cs)
        q0 = pl.multiple_of(j * gw, gw)
        for g in range(C_HEADS // RW_GH):
            gs = slice(g * gw, (g + 1) * gw)
            ls = slice(g * hc, (g + 1) * hc)
            kd = kd_ref[pl.ds(r0, cs), gs]
            rd = rd_ref[pl.ds(r0, cs), gs]
            v = b4_scr[pl.ds(r0, cs), gs]
            lhs = jnp.concatenate([_bd_rows(kd), _bd_rows(rd)], axis=0)
            rhs = jnp.concatenate([b1_scr[pl.ds(r0, cs), gs], b2_scr[pl.ds(r0, cs), gs]], axis=0)
            gm = _dot_nt(lhs, rhs)
            aab_ref[pl.ds(q0, gw), ls] = jnp.where(strict, gm[:gw, :hc], 0.0)
            aak = jnp.where(strict, gm[:gw, hc:], 0.0)
            rb_ref[pl.ds(q0, gw), ls] = jnp.where(incl, gm[gw:, :hc], 0.0)
            rk = jnp.where(incl, gm[gw:, hc:], 0.0)
            mr = _dot(jnp.concatenate([_bd_lanes(aak), _bd_lanes(rk)], axis=0), _to_stack(v))
            m2_ref[pl.ds(q0, gw), ls] = mr[:gw]
            rkv_ref[pl.ds(q0, gw), ls] = mr[gw:]
            vk_ref[pl.ds(q0, gw), ls] = _bd_fold(_dot_tn(v, b3_scr[pl.ds(r0, cs), gs]))
        return carry

    lax.fori_loop(0, tb // cs, chunk, 0)


def _tri_solve_kernel(a_ref, t_ref, a_scr, t_scr):
    n, hf = RWKV_CHUNK, RWKV_CHUNK // 2
    npb = a_scr.shape[2]
    ng = a_scr.shape[1] // n

    def load(t, carry):
        a_scr[t] = a_ref[pl.ds(t, npb, stride=n), :].T
        return carry

    lax.fori_loop(0, n, load, 0)
    jrow = _iota((hf, npb), 0)
    zero = jnp.zeros((hf, npb), F32)

    def coef(t, g, s):
        return a_scr[t, pl.ds(g * n + s, 1), :]

    def row_lo(t, carry):
        def inner(s, accs):
            return tuple(acc - coef(t, g, s) * t_scr[s, g * n:g * n + hf, :] for g, acc in enumerate(accs))

        accs = lax.fori_loop(0, t, inner, ((jrow == t).astype(F32),) * ng)
        for g, acc in enumerate(accs):
            t_scr[t, g * n:g * n + hf, :] = acc
            t_scr[t, g * n + hf:(g + 1) * n, :] = zero
        return carry

    lax.fori_loop(0, hf, row_lo, 0)

    def row_hi(t, carry):
        def inner_lo(s, accs):
            return tuple(acc - coef(t, g, s) * t_scr[s, g * n:g * n + hf, :] for g, acc in enumerate(accs))

        def inner_hi(s, accs):
            lo, hi = accs[:ng], accs[ng:]
            lo = tuple(acc - coef(t, g, s) * t_scr[s, g * n:g * n + hf, :] for g, acc in enumerate(lo))
            hi = tuple(acc - coef(t, g, s) * t_scr[s, g * n + hf:(g + 1) * n, :] for g, acc in enumerate(hi))
            return lo + hi

        lo = lax.fori_loop(0, hf, inner_lo, (zero,) * ng)
        accs = lax.fori_loop(hf, t, inner_hi, lo + ((jrow + hf == t).astype(F32),) * ng)
        for g in range(ng):
            t_scr[t, g * n:g * n + hf, :] = accs[g]
            t_scr[t, g * n + hf:(g + 1) * n, :] = accs[ng + g]
        return carry

    lax.fori_loop(hf, n, row_hi, 0)

    def store(t, carry):
        t_ref[pl.ds(t, npb, stride=n), :] = t_scr[t].T
        return carry

    lax.fori_loop(0, n, store, 0)


def _tri_solve(a):
    rows, width = a.shape
    n = RWKV_CHUNK
    npb = min(LANES, rows // n)
    return pl.pallas_call(
        _tri_solve_kernel,
        grid=(rows // (npb * n),),
        in_specs=[pl.BlockSpec((npb * n, width), lambda i: (i, 0))],
        out_specs=pl.BlockSpec((npb * n, width), lambda i: (i, 0)),
        out_shape=jax.ShapeDtypeStruct(a.shape, F32),
        scratch_shapes=[pltpu.VMEM((n, width, npb), F32)] * 2,
        compiler_params=_cparams(("parallel",)),
        name="tri_solve",
    )(a)


def _rwkv_c_kernel(t_ref, kd_ref, rd_ref, bg_ref, ge_ref, bonus_ref, gate_ref, rb_ref, m2_ref, rkv_ref, vk_ref,
                   lnw, lnb, y_ref, st_ref, s_scr, y_scr):
    c = pl.program_id(1)
    tb, cs, gw, hc = RWKV_BLOCK, RWKV_CHUNK, RW_GW, C_HEAD

    @pl.when(c == 0)
    def _():
        s_scr[...] = jnp.zeros_like(s_scr)

    def chunk(j, carry):
        r0 = pl.multiple_of(j * cs, cs)
        q0 = pl.multiple_of(j * gw, gw)
        for g in range(C_HEADS // RW_GH):
            gs = slice(g * gw, (g + 1) * gw)
            ls = slice(g * hc, (g + 1) * hc)
            rhs = jnp.concatenate([_to_stack(kd_ref[pl.ds(r0, cs), gs]), m2_ref[pl.ds(q0, gw), ls]], axis=1)
            wu = _dot(_bd_lanes(t_ref[pl.ds(q0, gw), ls]), rhs)
            rbwu = _dot(_bd_lanes(rb_ref[pl.ds(q0, gw), ls]), wu)
            rt = _to_stack(rd_ref[pl.ds(r0, cs), gs]) - rbwu[:, :hc]
            yc = rkv_ref[pl.ds(q0, gw), ls] - rbwu[:, hc:]
            wu_bd = jnp.concatenate([_bd_lanes(wu[:, :hc]), _bd_lanes(wu[:, hc:])], axis=1)
            nu = _dot_tn(wu_bd, _to_stack(bg_ref[pl.ds(r0, cs), gs]))
            ge_row = ge_ref[pl.ds(r0, 1), gs]
            ge_st = jnp.concatenate(
                [jnp.broadcast_to(ge_row[:, h * hc:(h + 1) * hc], (hc, hc)) for h in range(RW_GH)], axis=0)
            s = s_scr[g]
            s_bd = _bd_lanes(s)
            y_scr[pl.ds(r0, cs), gs] = _dot_nt(_to_all(rt), s_bd) + _to_all(yc)
            s_scr[g] = s * ge_st - _dot(s_bd, nu[:gw]) + (vk_ref[pl.ds(q0, gw), ls] - nu[gw:])
        return carry

    lax.fori_loop(0, tb // cs, chunk, 0)
    y_ref[...] = _rwkv_out(y_scr[...], bonus_ref[...], gate_ref[...], lnw[...], lnb[...]).astype(BF16)

    @pl.when(c == pl.num_programs(1) - 1)
    def _():
        st_ref[...] = s_scr[...]


def _rwkv_prompt(proj, rp):
    bsz, seq, _ = proj.shape
    tb, cs = RWKV_BLOCK, RWKV_CHUNK
    nblk = seq // tb

    def col(width, off):
        return pl.BlockSpec((None, tb, width), lambda b, c: (b, c, off // width))

    def pcol(width, off):
        return pl.BlockSpec((None, 8, width), lambda b, c: (b, jnp.maximum(c * (tb // 8) - 1, 0), off // width))

    ng = C_HEADS // RW_GH
    srows = tb // cs * RW_GW
    tok = pl.BlockSpec((None, tb, C_DIM), lambda b, c: (b, c, 0))
    stk = pl.BlockSpec((None, srows, ng * C_HEAD), lambda b, c: (b, c, 0))
    tok_shape = jax.ShapeDtypeStruct((bsz, seq, C_DIM), F32)
    stk_shape = jax.ShapeDtypeStruct((bsz, nblk * srows, ng * C_HEAD), F32)
    prm = [rp[n] for n in RW_NAMES]
    outs = pl.pallas_call(
        _rwkv_a_kernel,
        grid=(bsz, nblk),
        in_specs=[col(C_DIM, P_R), col(C_DIM, P_K), col(C_DIM, P_V), col(C_LR, P_LR),
                  pcol(C_DIM, P_R), pcol(C_DIM, P_K), pcol(C_DIM, P_V), pcol(C_LR, P_LR)]
                 + [_full(x.shape) for x in prm],
        out_specs=[tok] * 6 + [stk] * 5,
        out_shape=[tok_shape] * 6 + [stk_shape] * 5,
        scratch_shapes=[pltpu.VMEM((tb + 8, C_DIM), F32)] + [pltpu.VMEM((tb, C_DIM), F32)] * 4,
        compiler_params=_cparams(("parallel", "parallel")),
        name="rwkv_prep",
    )(proj, proj, proj, proj, proj, proj, proj, proj, *prm)
    kd, rd, bg, ge, bonus, gate, aab, rb, m2, rkv, vk = outs

    tmat = _tri_solve(aab.reshape(-1, ng * C_HEAD)).reshape(stk_shape.shape)

    y, st = pl.pallas_call(
        _rwkv_c_kernel,
        grid=(bsz, nblk),
        in_specs=[stk] + [tok] * 6 + [stk] * 4 + [_full((1, C_DIM))] * 2,
        out_specs=[tok, pl.BlockSpec((None, ng, RW_GW, C_HEAD), lambda b, c: (b, 0, 0, 0))],
        out_shape=[jax.ShapeDtypeStruct((bsz, seq, C_DIM), BF16),
                   jax.ShapeDtypeStruct((bsz, ng, RW_GW, C_HEAD), F32)],
        scratch_shapes=[pltpu.VMEM((ng, RW_GW, C_HEAD), F32), pltpu.VMEM((tb, C_DIM), F32)],
        compiler_params=_cparams(("parallel", "arbitrary")),
        name="rwkv_scan",
    )(tmat, kd, rd, bg, ge, bonus, gate, rb, m2, rkv, vk, rp["lnx_w"], rp["lnx_b"])
    return y, st.reshape(bsz, C_HEADS, C_HEAD, C_HEAD)


def _rwkv_step_kernel(r_ref, k_ref, v_ref, lr_ref, sh_ref, *rest):
    prm = tuple(x[...] for x in rest[:len(RW_NAMES)])
    lnw, lnb, s_ref, y_ref, so_ref = rest[len(RW_NAMES):]
    bb = STEP_BB
    sh = sh_ref[...]
    r, k2, v, kk, beta, ld, g, bonus = _rwkv_prep(
        r_ref[...], k_ref[...], v_ref[...], lr_ref[...],
        sh[:, :C_DIM], sh[:, C_DIM:2 * C_DIM], sh[:, 2 * C_DIM:3 * C_DIM], sh[:, 3 * C_DIM:], prm)
    w = jnp.exp(ld)
    v_t = v.T
    lane = _iota((C_DIM, LANES), 1)
    heads = [[x[:, h * C_HEAD:(h + 1) * C_HEAD] for x in (kk, w, beta, k2, r)] for h in range(C_HEADS)]

    def rows(i, b):
        return jnp.concatenate(
            [jnp.broadcast_to(heads[h][i][b:b + 1, :], (C_HEAD, C_HEAD)) for h in range(C_HEADS)], axis=0)

    y_t = jnp.zeros((C_DIM, LANES), F32)
    for b in range(bb):
        s = s_ref[b]
        sa = -jnp.sum(s * rows(0, b), axis=-1, keepdims=True)
        sn = s * rows(1, b) + sa * rows(2, b) + v_t[:, b:b + 1] * rows(3, b)
        so_ref[b] = sn
        y_t = jnp.where(lane == b, jnp.sum(sn * rows(4, b), axis=-1, keepdims=True), y_t)
    y = y_t.T[0:bb, :]
    y_ref[...] = _rwkv_out(y, bonus, g, lnw[...], lnb[...]).astype(BF16)


def _rwkv_step(proj, shift, s0, rp):
    bsz = proj.shape[0]
    bb = STEP_BB
    sflat = s0.reshape(bsz, C_DIM, C_HEAD)

    def col(width, off):
        return pl.BlockSpec((bb, width), lambda i: (i, off // width))

    prm = [rp[n] for n in RW_NAMES]
    y, sn = pl.pallas_call(
        _rwkv_step_kernel,
        grid=(bsz // bb,),
        in_specs=[col(C_DIM, P_R), col(C_DIM, P_K), col(C_DIM, P_V), col(C_LR, P_LR),
                  pl.BlockSpec((bb, C_COLS), lambda i: (i, 0))]
                 + [_full(x.shape) for x in prm] + [_full((1, C_DIM))] * 2
                 + [pl.BlockSpec((bb, C_DIM, C_HEAD), lambda i: (i, 0, 0))],
        out_specs=[pl.BlockSpec((bb, C_DIM), lambda i: (i, 0)),
                   pl.BlockSpec((bb, C_DIM, C_HEAD), lambda i: (i, 0, 0))],
        out_shape=[jax.ShapeDtypeStruct((bsz, C_DIM), BF16),
                   jax.ShapeDtypeStruct((bsz, C_DIM, C_HEAD), F32)],
        compiler_params=_cparams(("parallel",)),
        name="rwkv_step",
    )(proj, proj, proj, proj, shift, *prm, rp["lnx_w"], rp["lnx_b"], sflat)
    return y, sn.reshape(s0.shape)


def _layout_w_in_kernel(w_ref, o_ref):
    o = A_DIM + CONV_CH
    rows = w_ref.shape[0]
    o_ref[:, :o] = w_ref[:, :o].astype(BF16)
    o_ref[:, o:P_DT] = w_ref[:, o + A_HEADS:].astype(BF16)
    dt_tile = jnp.where(_iota((rows, LANES), 1) < A_HEADS, w_ref[:, o:o + LANES], 0.0)
    o_ref[:, P_DT:P_DT + LANES] = dt_tile.astype(BF16)
    o_ref[:, P_DT + LANES:] = jnp.zeros((rows, P_COLS - P_DT - LANES), BF16)


def _layout_w_in(w):
    k = w.shape[0]
    tr = 256
    return pl.pallas_call(
        _layout_w_in_kernel,
        grid=(k // tr,),
        in_specs=[pl.BlockSpec((tr, IN_COLS), lambda i: (i, 0))],
        out_specs=pl.BlockSpec((tr, P_COLS), lambda i: (i, 0)),
        out_shape=jax.ShapeDtypeStruct((k, P_COLS), BF16),
        compiler_params=_cparams(("parallel",)),
        name="layout_w_in",
    )(w)


def _xbc_cols(proj):
    return proj[..., P_X:P_X + CONV_CH]


def _moe_route(logits, n_tok):
    top_logit, top_idx = lax.top_k(logits, TOP_K)
    gate = jax.nn.softmax(top_logit, axis=-1)
    n_assign = n_tok * TOP_K
    flat_e = top_idx.reshape(-1).astype(jnp.int32)
    order = jnp.argsort(flat_e)
    onehot = jax.nn.one_hot(flat_e, N_EXPERTS, dtype=jnp.int32)
    counts = jnp.sum(onehot, axis=0)
    padded = (counts + MOE_BM - 1) // MOE_BM * MOE_BM
    pad_end = jnp.cumsum(padded)
    pad_start = pad_end - padded
    start = jnp.cumsum(counts) - counts
    rank = jnp.sum((jnp.cumsum(onehot, axis=0) - onehot) * onehot, axis=1)
    slot_of = (pad_start[flat_e] + rank).reshape(n_tok, TOP_K)
    n_blocks = -(-n_assign // MOE_BM) + N_EXPERTS
    n_slots = n_blocks * MOE_BM
    block_start = jnp.arange(n_blocks, dtype=jnp.int32) * MOE_BM
    nvalid = (pad_end[-1] // MOE_BM).astype(jnp.int32)
    bexp = jnp.minimum(jnp.searchsorted(pad_end, block_start, side="right"), N_EXPERTS - 1).astype(jnp.int32)
    slot_e = jnp.repeat(bexp, MOE_BM)
    within = jnp.arange(n_slots, dtype=jnp.int32) - pad_start[slot_e]
    src = order[jnp.clip(start[slot_e] + within, 0, n_assign - 1)] // TOP_K
    slot_tok = jnp.where(within < counts[slot_e], src, n_tok)
    left = counts[bexp] - (block_start - pad_start[bexp])
    nsub = jnp.clip((left + MOE_SUB - 1) // MOE_SUB, 0, MOE_BM // MOE_SUB).astype(jnp.int32)
    last = bexp[jnp.maximum(nvalid - 1, 0)]
    bexp = jnp.where(jnp.arange(n_blocks) < nvalid, bexp, last)
    first = jnp.concatenate([jnp.ones((1,), jnp.int32), (bexp[1:] != bexp[:-1]).astype(jnp.int32)])
    return gate, slot_tok, slot_of, bexp, nvalid.reshape(1), first, nsub


def _moe(xp, xs, h, logits, wg, wu, wd, g, final):
    tp = xp.shape[0]
    n_tok = h.shape[0]
    gate, slot_tok, slot_of, bexp, nvalid, first, nsub = _moe_route(logits[:, :N_EXPERTS], n_tok)
    xb = _moe_gather(jnp.where(slot_tok == n_tok, 0, slot_tok), h)
    yb = _moe_experts(xb, bexp, nvalid, first, nsub, wg, wu, wd)
    return (_moe_combine(xp, yb, slot_of[:tp], gate[:tp], g, final),
            _moe_combine(xs, yb, slot_of[tp:], gate[tp:], g, final))


def kernel(x_prompt, x_sample, state_ssm, state_conv, state_hgrn, state_rwkv, state_shift, norm1_g, w_in, conv_w, conv_b, dt_bias, a_log, d_skip, ssm_norm_g, lb_logits, hgrn_norm_g, shift_mu, w0, w2, a0, a2, g2, k_k, k_a, r_k, lnx_w, lnx_b, w_out, norm2_g, ffn_w_gate, ffn_w_up, ffn_w_down, router_w, exp_w_gate, exp_w_up, exp_w_down, final_norm_g):
    depth = w_in.shape[0]
    bp, seq, _ = x_prompt.shape
    bs = x_sample.shape[0]
    tp = bp * seq
    lb_soft = jax.nn.softmax(lb_logits.astype(F32), axis=0)
    lower_bounds = jnp.clip(jnp.cumsum(lb_soft, axis=0) - lb_soft[0:1], 0.0, 1.0)

    xp = x_prompt.reshape(tp, D_MODEL)
    xs = x_sample.reshape(bs, D_MODEL)
    outs = {k: [] for k in ("p_ssm", "p_conv", "p_hgrn", "p_rwkv", "p_shift",
                            "s_ssm", "s_conv", "s_hgrn", "s_rwkv", "s_shift")}
    normed = False
    for layer in range(depth):
        p = dict(conv_w=conv_w[layer], conv_b=conv_b[layer], dt_bias=dt_bias[layer], a_log=a_log[layer],
                 d_skip=d_skip[layer], ssm_norm_g=ssm_norm_g[layer], shift_mu=shift_mu[layer],
                 w0=w0[layer], w2=w2[layer], a0=a0[layer], a2=a2[layer], g2=g2[layer], k_k=k_k[layer],
                 k_a=k_a[layer], r_k=r_k[layer].reshape(-1), lnx_w=lnx_w[layer], lnx_b=lnx_b[layer])
        sp = _ssd_params(p)
        rp = _rwkv_params(p)
        lb = lower_bounds[layer]
        lbp = jnp.stack([1.0 - lb, jnp.log(lb), jnp.log1p(-lb)])
        hng = hgrn_norm_g[layer].reshape(1, B_HEAD)
        w_in_l = _layout_w_in(w_in[layer])
        w_out_l = w_out[layer].astype(BF16)
        j = layer // 2
        routed = layer % 2 == 1
        if routed:
            rw_f = jnp.pad(router_w[j], ((0, 0), (0, LANES - N_EXPERTS)))
            rw_hi = rw_f.astype(BF16)
            rw = jnp.concatenate([rw_hi, (rw_f - rw_hi.astype(F32)).astype(BF16)], axis=1)
            h_all = jnp.zeros((tp + bs, D_MODEL), F32)
            l_all = jnp.zeros((tp + bs, LANES), F32)

        proj = _norm_matmul(xp, norm1_g[layer], w_in_l, 1024, 512).reshape(bp, seq, P_COLS)
        ya, ssm_p = _ssd_prompt(proj, sp)
        yb, hgrn_p = _hgrn_prompt(proj, lbp, hng)
        yc, rwkv_p = _rwkv_prompt(proj, rp)
        xp, hp, l_all = _out_proj(ya.reshape(tp, A_DIM), yb.reshape(tp, B_DIM), yc.reshape(tp, C_DIM),
                                  w_out_l, xp, norm2_g[layer], 256, (rw, h_all, l_all, 0) if routed else None)
        outs["p_ssm"].append(ssm_p)
        outs["p_conv"].append(_xbc_cols(proj[:, seq - (CONV_W - 1):, :]))
        outs["p_hgrn"].append(hgrn_p)
        outs["p_rwkv"].append(rwkv_p)
        outs["p_shift"].append(proj[:, seq - 1, P_R:P_R + C_COLS])

        projs = _norm_matmul(xs, norm1_g[layer], w_in_l, 128, 512)
        ya, ssm_s = _ssd_step(projs, state_conv[layer], state_ssm[layer], p, sp)
        yb, hgrn_s = _hgrn_step(projs, state_hgrn[layer], lbp, hng)
        yc, rwkv_s = _rwkv_step(projs, state_shift[layer], state_rwkv[layer], rp)
        xs, hs, l_all = _out_proj(ya, yb, yc, w_out_l, xs, norm2_g[layer], 128,
                                  (rw, hp, l_all, tp) if routed else None)
        outs["s_ssm"].append(ssm_s)
        outs["s_conv"].append(jnp.concatenate([state_conv[layer][:, 1:, :], _xbc_cols(projs)[:, None, :]], axis=1))
        outs["s_hgrn"].append(hgrn_s)
        outs["s_rwkv"].append(rwkv_s)
        outs["s_shift"].append(projs[:, P_R:P_R + C_COLS])

        if not routed:
            xp = _ffn(xp, hp, ffn_w_gate[j], ffn_w_up[j], ffn_w_down[j], 512)
            xs = _ffn(xs, hs, ffn_w_gate[j], ffn_w_up[j], ffn_w_down[j], 128)
        else:
            normed = layer == depth - 1
            xp, xs = _moe(xp, xs, hs, l_all, exp_w_gate[j], exp_w_up[j], exp_w_down[j], final_norm_g, normed)

    if not normed:
        xp, xs = _final_norm(xp, final_norm_g, 512), _final_norm(xs, final_norm_g, 128)
    y_prompt = xp.reshape(bp, seq, D_MODEL)
    y_sample = xs.reshape(bs, 1, D_MODEL)
    st = {k: jnp.stack(v) for k, v in outs.items()}
    return (y_prompt, y_sample, st["p_ssm"], st["p_conv"], st["p_hgrn"], st["p_rwkv"], st["p_shift"],
            st["s_ssm"], st["s_conv"], st["s_hgrn"], st["s_rwkv"], st["s_shift"])
```

```python
import functools
import math

import jax
import jax.numpy as jnp
from jax import lax
from jax.experimental import pallas as pl
from jax.experimental.pallas import tpu as pltpu

F32 = jnp.float32
BF16 = jnp.bfloat16
HI = lax.Precision.HIGHEST

D_MODEL = 2048
A_DIM, A_HEAD, A_HEADS, A_GROUPS, A_STATE = 1024, 64, 16, 2, 128
A_GN = A_GROUPS * A_STATE
CONV_W = 4
CONV_CH = A_DIM + 2 * A_GN
B_DIM, B_HEAD, B_HEADS = 512, 128, 4
C_DIM, C_HEAD, C_HEADS = 512, 64, 8
C_LR_W, C_LR_A, C_LR_G = 64, 64, 128
C_LR = C_LR_W + C_LR_A + C_LR_G
C_COLS = 3 * C_DIM + C_LR
IN_COLS = A_DIM + CONV_CH + A_HEADS + 4 * B_DIM + C_COLS
D_FF, N_EXPERTS, TOP_K, E_FF = 5504, 8, 2, 7168
NORM_EPS, GN_EPS, L2_EPS = 1e-6, 64e-5, 1e-12

P_Z, P_X, P_B, P_C = 0, 1024, 2048, 2304
P_Q, P_F, P_I, P_G = 2560, 3072, 3584, 4096
P_R, P_K, P_V, P_LR, P_DT, P_COLS = 4608, 5120, 5632, 6144, 6400, 6656
LANES = 128
DT_PAD = P_COLS - P_DT - A_HEADS

SSD_CHUNK = 256
HGRN_BLOCK, HGRN_CHUNK = 256, 16
RWKV_BLOCK, RWKV_CHUNK = 256, 64
STEP_BB = 8

FFN_TF, FFN_TN = 512, 512
MOE_BM, MOE_SUB, MOE_TF, MOE_TN = 512, 256, 1024, 512

NT_DIMS = (((1,), (1,)), ((), ()))
TN_DIMS = (((0,), (0,)), ((), ()))


def _cparams(sem, vmem_mb=48):
    return pltpu.CompilerParams(dimension_semantics=sem, vmem_limit_bytes=vmem_mb * 1024 * 1024)


def _dot(a, b):
    return jnp.dot(a.astype(BF16), b.astype(BF16), preferred_element_type=F32)


def _dot_nt(a, b):
    return lax.dot_general(a.astype(BF16), b.astype(BF16), NT_DIMS, preferred_element_type=F32)


def _dot_tn(a, b):
    return lax.dot_general(a.astype(BF16), b.astype(BF16), TN_DIMS, preferred_element_type=F32)


def _dot_hi(a, b):
    return jnp.dot(a, b, precision=HI, preferred_element_type=F32)


def _dot_nt_hi(a, b):
    return lax.dot_general(a, b, NT_DIMS, precision=HI, preferred_element_type=F32)


def _dot_tn_hi(a, b):
    return lax.dot_general(a, b, TN_DIMS, precision=HI, preferred_element_type=F32)


def _silu(x):
    return x * jax.nn.sigmoid(x)


def _softplus(x):
    return jnp.maximum(x, 0.0) + jnp.log1p(jnp.exp(-jnp.abs(x)))


def _iota(shape, dim):
    return lax.broadcasted_iota(jnp.int32, shape, dim)


def _rms(x, g):
    return x * lax.rsqrt(jnp.mean(x * x, axis=-1, keepdims=True) + NORM_EPS) * g


def _full(shape):
    nd = len(shape)
    return pl.BlockSpec(shape, lambda *_: (0,) * nd)


def _norm_matmul_kernel(x_ref, g_ref, w_ref, o_ref, h_scr):
    @pl.when(pl.program_id(1) == 0)
    def _():
        h_scr[...] = _rms(x_ref[...], g_ref[...]).astype(BF16)

    o_ref[...] = jnp.dot(h_scr[...], w_ref[...], preferred_element_type=F32)


def _norm_matmul(x, g, w, tm, tn):
    m, k = x.shape
    n = w.shape[1]
    return pl.pallas_call(
        _norm_matmul_kernel,
        grid=(m // tm, n // tn),
        in_specs=[pl.BlockSpec((tm, k), lambda i, j: (i, 0)),
                  pl.BlockSpec((1, k), lambda i, j: (0, 0)),
                  pl.BlockSpec((k, tn), lambda i, j: (0, j))],
        out_specs=pl.BlockSpec((tm, tn), lambda i, j: (i, j)),
        out_shape=jax.ShapeDtypeStruct((m, n), F32),
        scratch_shapes=[pltpu.VMEM((tm, k), BF16)],
        compiler_params=_cparams(("parallel", "arbitrary")),
        name="norm_matmul",
    )(x, g.reshape(1, k), w)


def _out_proj_kernel(routed, ya_ref, yb_ref, yc_ref, w_ref, r_ref, g_ref, *rest):
    x = (r_ref[...]
         + jnp.dot(ya_ref[...], w_ref[0:A_DIM, :], preferred_element_type=F32)
         + jnp.dot(yb_ref[...], w_ref[A_DIM:A_DIM + B_DIM, :], preferred_element_type=F32)
         + jnp.dot(yc_ref[...], w_ref[A_DIM + B_DIM:, :], preferred_element_type=F32))
    h = _rms(x, g_ref[...])
    h_hi = h.astype(BF16)
    if routed:
        rw_ref, _, _, x_ref, h_ref, l_ref = rest
        h_lo = (h - h_hi.astype(F32)).astype(BF16)
        rw = rw_ref[...]
        l2 = jnp.dot(h_hi, rw, preferred_element_type=F32)
        l_ref[...] = (l2[:, :LANES] + l2[:, LANES:]
                      + jnp.dot(h_lo, rw[:, :LANES], preferred_element_type=F32))
    else:
        x_ref, h_ref = rest
    x_ref[...] = x
    h_ref[...] = h_hi


def _out_proj(ya, yb, yc, w, res, g, tm, routed=None):
    m = ya.shape[0]
    n = w.shape[1]
    row = lambda width: pl.BlockSpec((tm, width), lambda i: (i, 0))
    in_specs = [row(A_DIM), row(B_DIM), row(C_DIM), _full(w.shape), row(n), _full((1, n))]
    args = [ya, yb, yc, w, res, g.reshape(1, n)]
    aliases = {}
    if routed is None:
        out_specs = [row(n), row(n)]
        out_shape = [jax.ShapeDtypeStruct((m, n), F32), jax.ShapeDtypeStruct((m, n), BF16)]
    else:
        rw, h_buf, l_buf, row0 = routed
        blk0 = row0 // tm
        off = lambda width: pl.BlockSpec((tm, width), lambda i: (i + blk0, 0))
        in_specs += [_full(rw.shape), pl.BlockSpec(memory_space=pl.ANY), pl.BlockSpec(memory_space=pl.ANY)]
        args += [rw, h_buf, l_buf]
        out_specs = [row(n), off(n), off(LANES)]
        out_shape = [jax.ShapeDtypeStruct((m, n), F32), jax.ShapeDtypeStruct(h_buf.shape, BF16),
                     jax.ShapeDtypeStruct(l_buf.shape, F32)]
        aliases = {len(args) - 2: 1, len(args) - 1: 2}
    outs = pl.pallas_call(
        functools.partial(_out_proj_kernel, routed is not None),
        grid=(m // tm,),
        in_specs=in_specs, out_specs=out_specs, out_shape=out_shape,
        input_output_aliases=aliases,
        compiler_params=_cparams(("parallel",)),
        name="out_proj",
    )(*args)
    return outs if routed is not None else (*outs, None)


def _ffn_up_kernel(h_ref, wg_ref, wu_ref, o_ref, wg_s, wu_s):
    @pl.when(pl.program_id(1) == 0)
    def _():
        wg_s[...] = wg_ref[...].astype(BF16)
        wu_s[...] = wu_ref[...].astype(BF16)

    h = h_ref[...]
    a = jnp.dot(h, wg_s[...], preferred_element_type=F32)
    u = jnp.dot(h, wu_s[...], preferred_element_type=F32)
    o_ref[...] = (_silu(a) * u).astype(BF16)


def _ffn_down_kernel(h_ref, wd_ref, x_ref, o_ref, wd_s):
    @pl.when(pl.program_id(1) == 0)
    def _():
        wd_s[...] = wd_ref[...].astype(BF16)

    o_ref[...] = x_ref[...] + jnp.dot(h_ref[...], wd_s[...], preferred_element_type=F32)


def _ffn(x, h, wg, wu, wd, tm):
    m, k = h.shape
    ff = wg.shape[1]
    tf, tn = FFN_TF, FFN_TN
    tu = 2 * tm if m % (2 * tm) == 0 else tm
    act = pl.pallas_call(
        _ffn_up_kernel,
        grid=(pl.cdiv(ff, tf), m // tu),
        in_specs=[pl.BlockSpec((tu, k), lambda f, i: (i, 0)),
                  pl.BlockSpec((k, tf), lambda f, i: (0, f)),
                  pl.BlockSpec((k, tf), lambda f, i: (0, f))],
        out_specs=pl.BlockSpec((tu, tf), lambda f, i: (i, f)),
        out_shape=jax.ShapeDtypeStruct((m, ff), BF16),
        scratch_shapes=[pltpu.VMEM((k, tf), BF16)] * 2,
        compiler_params=_cparams(("arbitrary", "arbitrary")),
        name="ffn_up",
    )(h, wg, wu)
    return pl.pallas_call(
        _ffn_down_kernel,
        grid=(k // tn, m // tm),
        in_specs=[pl.BlockSpec((tm, ff), lambda n, i: (i, 0)),
                  pl.BlockSpec((ff, tn), lambda n, i: (0, n)),
                  pl.BlockSpec((tm, tn), lambda n, i: (i, n))],
        out_specs=pl.BlockSpec((tm, tn), lambda n, i: (i, n)),
        out_shape=jax.ShapeDtypeStruct((m, k), F32),
        scratch_shapes=[pltpu.VMEM((ff, tn), BF16)],
        compiler_params=_cparams(("arbitrary", "arbitrary")),
        name="ffn_down",
    )(act, wd, x)


def _moe_rows(nsub, o_ref, compute):
    bm = o_ref.shape[0]
    for ns in range(bm // MOE_SUB + 1):
        @pl.when(nsub == ns)
        def _(rows=ns * MOE_SUB):
            if rows:
                o_ref[0:rows, :] = compute(rows).astype(o_ref.dtype)
            if rows < bm:
                o_ref[rows:, :] = jnp.zeros((bm - rows, o_ref.shape[1]), o_ref.dtype)


def _moe_up_kernel(bexp_ref, nv_ref, first_ref, nsub_ref, x_ref, wg_ref, wu_ref, o_ref, wg_s, wu_s):
    b = pl.program_id(1)

    @pl.when(first_ref[b] == 1)
    def _():
        wg_s[...] = wg_ref[...].astype(BF16)
        wu_s[...] = wu_ref[...].astype(BF16)

    def compute(rows):
        x = x_ref[0:rows, :]
        a = jnp.dot(x, wg_s[...], preferred_element_type=F32)
        u = jnp.dot(x, wu_s[...], preferred_element_type=F32)
        return _silu(a) * u

    _moe_rows(nsub_ref[b], o_ref, compute)


def _moe_down_kernel(bexp_ref, nv_ref, first_ref, nsub_ref, h_ref, wd_ref, o_ref, wd_s):
    b = pl.program_id(1)

    @pl.when(first_ref[b] == 1)
    def _():
        wd_s[...] = wd_ref[...].astype(BF16)

    _moe_rows(nsub_ref[b], o_ref,
              lambda rows: jnp.dot(h_ref[0:rows, :], wd_s[...], preferred_element_type=F32))


def _moe_experts(xb, bexp, nvalid, first, nsub, wg, wu, wd):
    n_slots, k = xb.shape
    ff = wg.shape[2]
    nb = n_slots // MOE_BM
    tf, tn = MOE_TF, MOE_TN

    def blk(b, nv):
        return jnp.minimum(b, nv[0] - 1)

    act = pl.pallas_call(
        _moe_up_kernel,
        grid_spec=pltpu.PrefetchScalarGridSpec(
            num_scalar_prefetch=4,
            grid=(ff // tf, nb),
            in_specs=[pl.BlockSpec((MOE_BM, k), lambda f, b, be, nv, fr, ns: (blk(b, nv), 0)),
                      pl.BlockSpec((None, k, tf), lambda f, b, be, nv, fr, ns: (be[b], 0, f)),
                      pl.BlockSpec((None, k, tf), lambda f, b, be, nv, fr, ns: (be[b], 0, f))],
            out_specs=pl.BlockSpec((MOE_BM, tf), lambda f, b, be, nv, fr, ns: (b, f)),
            scratch_shapes=[pltpu.VMEM((k, tf), BF16)] * 2,
        ),
        out_shape=jax.ShapeDtypeStruct((n_slots, ff), BF16),
        compiler_params=_cparams(("arbitrary", "arbitrary"), 56),
        name="moe_up",
    )(bexp, nvalid, first, nsub, xb, wg, wu)
    per = MOE_BM // MOE_SUB
    bexp_d = jnp.repeat(bexp, per)
    first_d = jnp.concatenate([jnp.ones((1,), jnp.int32), (bexp_d[1:] != bexp_d[:-1]).astype(jnp.int32)])
    nsub_d = jnp.clip(jnp.repeat(nsub, per) - jnp.tile(jnp.arange(per, dtype=jnp.int32), nb), 0, 1)
    ids = jnp.arange(nb * per, dtype=jnp.int32)
    src_d = lax.cummax(jnp.where(nsub_d > 0, ids, 0))
    return pl.pallas_call(
        _moe_down_kernel,
        grid_spec=pltpu.PrefetchScalarGridSpec(
            num_scalar_prefetch=4,
            grid=(k // tn, nb * per),
            in_specs=[pl.BlockSpec((MOE_SUB, ff), lambda n, b, be, sb, fr, ns: (sb[b], 0)),
                      pl.BlockSpec((None, ff, tn), lambda n, b, be, nv, fr, ns: (be[b], 0, n))],
            out_specs=pl.BlockSpec((MOE_SUB, tn), lambda n, b, be, nv, fr, ns: (b, n)),
            scratch_shapes=[pltpu.VMEM((ff, tn), BF16)],
        ),
        out_shape=jax.ShapeDtypeStruct((n_slots, k), F32),
        compiler_params=_cparams(("arbitrary", "arbitrary"), 56),
        name="moe_down",
    )(bexp_d, src_d, first_d, nsub_d, act, wd)


def _final_norm_kernel(x_ref, g_ref, o_ref):
    o_ref[...] = _rms(x_ref[...], g_ref[...])


def _final_norm(x, g, tm):
    m, k = x.shape
    return pl.pallas_call(
        _final_norm_kernel,
        grid=(m // tm,),
        in_specs=[pl.BlockSpec((tm, k), lambda i: (i, 0)),
                  pl.BlockSpec((1, k), lambda i: (0, 0))],
        out_specs=pl.BlockSpec((tm, k), lambda i: (i, 0)),
        out_shape=jax.ShapeDtypeStruct((m, k), F32),
        compiler_params=_cparams(("parallel",)),
        name="final_norm",
    )(x, g.reshape(1, k))


def _row_copy(src_hbm, row, dst, r, sem):
    return pltpu.make_async_copy(src_hbm.at[pl.ds(row, 1), :], dst.at[pl.ds(r, 1), :], sem)


def _moe_combine_kernel(final, s0_ref, s1_ref, x_ref, gate_ref, g_ref, yb_hbm, o_ref, buf, sem):
    rows = x_ref.shape[0]
    base = pl.program_id(0) * rows

    def issue(r, carry):
        _row_copy(yb_hbm, s0_ref[base + r], buf.at[0], r, sem).start()
        _row_copy(yb_hbm, s1_ref[base + r], buf.at[1], r, sem).start()
        return carry

    lax.fori_loop(0, rows, issue, 0)
    pltpu.make_async_copy(buf, buf, sem).wait()
    gate = gate_ref[...]
    y = x_ref[...] + gate[:, 0:1] * buf[0] + gate[:, 1:2] * buf[1]
    o_ref[...] = _rms(y, g_ref[...]) if final else y


def _moe_combine(x, yb, slot_of, gate, g, final):
    m, k = x.shape
    rows = 128
    gate_p = jnp.pad(gate, ((0, 0), (0, LANES - TOP_K)))
    return pl.pallas_call(
        functools.partial(_moe_combine_kernel, final),
        grid_spec=pltpu.PrefetchScalarGridSpec(
            num_scalar_prefetch=2,
            grid=(m // rows,),
            in_specs=[pl.BlockSpec((rows, k), lambda i, s0, s1: (i, 0)),
                      pl.BlockSpec((rows, LANES), lambda i, s0, s1: (i, 0)),
                      pl.BlockSpec((1, k), lambda i, s0, s1: (0, 0)),
                      pl.BlockSpec(memory_space=pl.ANY)],
            out_specs=pl.BlockSpec((rows, k), lambda i, s0, s1: (i, 0)),
            scratch_shapes=[pltpu.VMEM((TOP_K, rows, k), F32), pltpu.SemaphoreType.DMA],
        ),
        out_shape=jax.ShapeDtypeStruct((m, k), F32),
        compiler_params=_cparams(("arbitrary",)),
        name="moe_combine",
    )(slot_of[:, 0], slot_of[:, 1], x, gate_p, g.reshape(1, k), yb)


def _ssd_gate_norm(y, z, ng):
    y = y * _silu(z)
    gw = A_DIM // A_GROUPS
    outs = []
    for g in range(A_GROUPS):
        yg = y[:, g * gw:(g + 1) * gw]
        outs.append(yg * lax.rsqrt(jnp.mean(yg * yg, axis=-1, keepdims=True) + NORM_EPS))
    return jnp.concatenate(outs, axis=-1) * ng


def _ssd_prompt_kernel(z_ref, x_ref, b_ref, c_ref, dt_ref, cwx, cwb, cwc, cbx, cbb, cbc,
                       dtb, aneg, dsk, ng, y_ref, st_ref, h_scr, xpx, xpb, xpc, yacc):
    c = pl.program_id(1)
    ch = SSD_CHUNK

    @pl.when(c == 0)
    def _():
        h_scr[...] = jnp.zeros_like(h_scr)
        xpx[0:8, :] = jnp.zeros((8, A_DIM), F32)
        xpb[0:8, :] = jnp.zeros((8, A_STATE * A_GROUPS), F32)
        xpc[0:8, :] = jnp.zeros((8, A_STATE * A_GROUPS), F32)

    xpx[8:8 + ch, :] = x_ref[...]
    xpb[8:8 + ch, :] = b_ref[...]
    xpc[8:8 + ch, :] = c_ref[...]

    def conv(xp, cw, cb):
        acc = cb[...] + xp[pl.ds(8 - (CONV_W - 1), ch), :] * cw[0:1, :]
        for tap in range(1, CONV_W):
            acc = acc + xp[pl.ds(8 - (CONV_W - 1) + tap, ch), :] * cw[tap:tap + 1, :]
        return _silu(acc)

    ux = conv(xpx, cwx, cbx)
    ub = conv(xpb, cwb, cbb)
    uc = conv(xpc, cwc, cbc)
    xpx[0:8, :] = xpx[ch:ch + 8, :]
    xpb[0:8, :] = xpb[ch:ch + 8, :]
    xpc[0:8, :] = xpc[ch:ch + 8, :]

    dt = _softplus(dt_ref[...] + dtb[...])
    la = dt * aneg[...]
    tri = (_iota((ch, ch), 1) <= _iota((ch, ch), 0)).astype(F32)
    cum = _dot_hi(tri, la)
    eye = (_iota((A_HEADS, LANES), 0) == _iota((A_HEADS, LANES), 1)).astype(F32)
    cum_r = _dot_nt_hi(eye, cum)
    dt_r = _dot_nt_hi(eye, dt)
    cl = cum[ch - 1:ch, :]
    tail = jnp.exp(cl - cum) * dt
    ecum = jnp.exp(cum)
    ecl = jnp.exp(cl)
    causal = _iota((ch, ch), 1) <= _iota((ch, ch), 0)
    rep = A_HEADS // A_GROUPS

    for g in range(A_GROUPS):
        bg = ub[:, g * A_STATE:(g + 1) * A_STATE].astype(BF16)
        cg = uc[:, g * A_STATE:(g + 1) * A_STATE].astype(BF16)
        cb_g = _dot_nt(cg, bg)
        for h in range(g * rep, (g + 1) * rep):
            hs = slice(h * A_HEAD, (h + 1) * A_HEAD)
            seg = cum[:, h:h + 1] - cum_r[h:h + 1, :]
            w = jnp.exp(jnp.where(causal, seg, -jnp.inf)) * cb_g * dt_r[h:h + 1, :]
            xh = ux[:, hs]
            hst = h_scr[h]
            y_h = _dot(w, xh) + _dot_nt(cg, hst) * ecum[:, h:h + 1]
            yacc[:, hs] = y_h
            h_scr[h] = hst * ecl[:, h:h + 1] + _dot_tn(xh * tail[:, h:h + 1], bg)

    y = yacc[...] + dsk[...] * ux
    y_ref[...] = _ssd_gate_norm(y, z_ref[...], ng[...]).astype(BF16)

    @pl.when(c == pl.num_programs(1) - 1)
    def _():
        st_ref[...] = h_scr[...]


def _ssd_params(p):
    cw, cb = p["conv_w"], p["conv_b"].reshape(1, CONV_CH)
    pad = LANES - A_HEADS
    return dict(
        cwx=cw[:, :A_DIM], cwb=cw[:, A_DIM:A_DIM + A_GN], cwc=cw[:, A_DIM + A_GN:],
        cbx=cb[:, :A_DIM], cbb=cb[:, A_DIM:A_DIM + A_GN], cbc=cb[:, A_DIM + A_GN:],
        dtb=jnp.pad(p["dt_bias"], (0, pad)).reshape(1, LANES),
        aneg=jnp.pad(-jnp.exp(p["a_log"]), (0, pad)).reshape(1, LANES),
        dsk=jnp.repeat(p["d_skip"], A_HEAD).reshape(1, A_DIM),
        ng=p["ssm_norm_g"].reshape(1, A_DIM),
    )


def _ssd_prompt(proj, sp):
    bsz, seq, _ = proj.shape
    ch = SSD_CHUNK
    gn = A_GN

    def col(width, off):
        return pl.BlockSpec((None, ch, width), lambda b, c: (b, c, off // width))

    names = ("cwx", "cwb", "cwc", "cbx", "cbb", "cbc", "dtb", "aneg", "dsk", "ng")
    y, st = pl.pallas_call(
        _ssd_prompt_kernel,
        grid=(bsz, seq // ch),
        in_specs=[col(A_DIM, P_Z), col(A_DIM, P_X), col(gn, P_B), col(gn, P_C), col(LANES, P_DT)]
                 + [_full(sp[n].shape) for n in names],
        out_specs=[pl.BlockSpec((None, ch, A_DIM), lambda b, c: (b, c, 0)),
                   pl.BlockSpec((None, A_HEADS, A_HEAD, A_STATE), lambda b, c: (b, 0, 0, 0))],
        out_shape=[jax.ShapeDtypeStruct((bsz, seq, A_DIM), BF16),
                   jax.ShapeDtypeStruct((bsz, A_HEADS, A_HEAD, A_STATE), F32)],
        scratch_shapes=[pltpu.VMEM((A_HEADS, A_HEAD, A_STATE), F32),
                        pltpu.VMEM((ch + 8, A_DIM), F32),
                        pltpu.VMEM((ch + 8, gn), F32),
                        pltpu.VMEM((ch + 8, gn), F32),
                        pltpu.VMEM((ch, A_DIM), F32)],
        compiler_params=_cparams(("parallel", "arbitrary")),
        name="ssd_prompt",
    )(proj, proj, proj, proj, proj, *[sp[n] for n in names])
    return y, st


def _ssd_step_kernel(z_ref, x_ref, b_ref, c_ref, dt_ref, cs_ref, cw, cb, dtb, anegx, expand, dsk, ng,
                     h_ref, y_ref, ho_ref, y_scr):
    bb = STEP_BB
    xbc = (x_ref[...], b_ref[...], c_ref[...])
    offs = (0, A_DIM, A_DIM + A_GN, CONV_CH)
    u = []
    for i in range(3):
        sl = slice(offs[i], offs[i + 1])
        acc = cb[:, sl] + xbc[i] * cw[CONV_W - 1:CONV_W, sl]
        for tap in range(CONV_W - 1):
            acc = acc + cs_ref[tap][:, sl] * cw[tap:tap + 1, sl]
        u.append(_silu(acc))
    ux, ub, uc = u
    dt = _softplus(dt_ref[...] + dtb[...])
    dtx = _dot_hi(dt, expand[...])
    dec_t = jnp.exp(dtx * anegx[...]).T
    xdt_t = (ux * dtx).T
    gw = A_DIM // A_GROUPS
    for b in range(bb):
        for g in range(A_GROUPS):
            rs = slice(g * gw, (g + 1) * gw)
            ns = slice(g * A_STATE, (g + 1) * A_STATE)
            hg = h_ref[b, rs, :] * dec_t[rs, b:b + 1] + xdt_t[rs, b:b + 1] * ub[b:b + 1, ns]
            ho_ref[b, rs, :] = hg
            y_scr[b:b + 1, rs] = _dot_nt(uc[:, ns], hg)[b:b + 1, :]
    y = y_scr[...] + dsk[...] * ux
    y_ref[...] = _ssd_gate_norm(y, z_ref[...], ng[...]).astype(BF16)


def _ssd_step(proj, conv_state, h_all, layer, p, sp):
    bsz = proj.shape[0]
    bb = STEP_BB
    cs = jnp.swapaxes(conv_state, 0, 1)
    hflat = h_all.reshape(h_all.shape[0], bsz, A_DIM, A_STATE)
    expand = (jnp.arange(LANES)[:, None] == (jnp.arange(A_DIM) // A_HEAD)[None, :]).astype(F32)
    anegx = jnp.repeat(-jnp.exp(p["a_log"]), A_HEAD).reshape(1, A_DIM)
    cw, cb = p["conv_w"], p["conv_b"].reshape(1, CONV_CH)

    def col(width, off):
        return pl.BlockSpec((bb, width), lambda i: (i, off // width))

    y, hn = pl.pallas_call(
        _ssd_step_kernel,
        grid=(bsz // bb,),
        in_specs=[col(A_DIM, P_Z), col(A_DIM, P_X), col(A_GN, P_B), col(A_GN, P_C), col(LANES, P_DT),
                  pl.BlockSpec((CONV_W - 1, bb, CONV_CH), lambda i: (0, i, 0)),
                  _full(cw.shape), _full(cb.shape), _full(sp["dtb"].shape), _full(anegx.shape),
                  _full(expand.shape), _full(sp["dsk"].shape), _full(sp["ng"].shape),
                  pl.BlockSpec((None, bb, A_DIM, A_STATE), lambda i: (layer, i, 0, 0))],
        out_specs=[pl.BlockSpec((bb, A_DIM), lambda i: (i, 0)),
                   pl.BlockSpec((bb, A_DIM, A_STATE), lambda i: (i, 0, 0))],
        out_shape=[jax.ShapeDtypeStruct((bsz, A_DIM), BF16),
                   jax.ShapeDtypeStruct((bsz, A_DIM, A_STATE), F32)],
        scratch_shapes=[pltpu.VMEM((bb, A_DIM), F32)],
        compiler_params=_cparams(("parallel",)),
        name="ssd_step",
    )(proj, proj, proj, proj, proj, cs, cw, cb, sp["dtb"], anegx, expand, sp["dsk"], sp["ng"], hflat)
    return y, hn.reshape(h_all.shape[1:])


def _hgrn_gates(q, f, lbp):
    qf = _silu(q)
    kf = lbp[0:1, :] * jax.nn.sigmoid(-f)
    log_sig = jnp.minimum(f, 0.0) - jnp.log1p(jnp.exp(-jnp.abs(f)))
    a = lbp[1:2, :]
    b = lbp[2:3, :] + log_sig
    logf = jnp.maximum(a, b) + jnp.log1p(jnp.exp(-jnp.abs(a - b)))
    return qf, kf, logf


def _hgrn_out(o, g, ng):
    outs = []
    for h in range(B_HEADS):
        oh = o[:, h * B_HEAD:(h + 1) * B_HEAD]
        outs.append(oh * lax.rsqrt(jnp.mean(oh * oh, axis=-1, keepdims=True) + NORM_EPS) * ng)
    return jnp.concatenate(outs, axis=-1) * _silu(g)


def _hgrn_prompt_kernel(q_ref, f_ref, i_ref, g_ref, lbp, ng, o_ref, st_ref,
                        s_scr, cum_scr, q_scr, k_scr, o_scr):
    c = pl.program_id(1)
    tb, cs = HGRN_BLOCK, HGRN_CHUNK

    @pl.when(c == 0)
    def _():
        s_scr[...] = jnp.zeros_like(s_scr)

    qf, kf, logf = _hgrn_gates(q_ref[...], f_ref[...], lbp[...])
    r_i, c_i = _iota((tb, tb), 0), _iota((tb, tb), 1)
    bd = ((r_i // cs == c_i // cs) & (c_i <= r_i)).astype(F32)
    cum_scr[...] = _dot_hi(bd, logf)
    q_scr[...] = qf
    k_scr[...] = kf
    rows = _iota((cs, B_HEAD), 0)

    def chunk(cc, carry):
        r0 = pl.multiple_of(cc * cs, cs)
        for h in range(B_HEADS):
            hs = slice(h * B_HEAD, (h + 1) * B_HEAD)
            cu = cum_scr[pl.ds(r0, cs), hs]
            q = q_scr[pl.ds(r0, cs), hs]
            k = k_scr[pl.ds(r0, cs), hs]
            v = i_ref[pl.ds(r0, cs), hs]
            st = s_scr[h]
            o = _dot_nt(q * jnp.exp(cu), st)
            for s in range(cs):
                d = jnp.exp(jnp.where(rows >= s, cu - cu[s:s + 1, :], -jnp.inf))
                att = jnp.sum(q * k[s:s + 1, :] * d, axis=-1, keepdims=True)
                o = o + att * v[s:s + 1, :]
            o_scr[pl.ds(r0, cs), hs] = o
            cl = cu[cs - 1:cs, :]
            s_scr[h] = st * jnp.exp(cl) + _dot_tn(v, k * jnp.exp(cl - cu))
        return carry

    lax.fori_loop(0, tb // cs, chunk, 0)
    o_ref[...] = _hgrn_out(o_scr[...], g_ref[...], ng[...]).astype(BF16)

    @pl.when(c == pl.num_programs(1) - 1)
    def _():
        for h in range(B_HEADS):
            st_ref[h] = s_scr[h].T


def _hgrn_prompt(proj, lbp, ng):
    bsz, seq, _ = proj.shape
    tb = HGRN_BLOCK

    def col(off):
        return pl.BlockSpec((None, tb, B_DIM), lambda b, c: (b, c, off // B_DIM))

    o, st = pl.pallas_call(
        _hgrn_prompt_kernel,
        grid=(bsz, seq // tb),
        in_specs=[col(P_Q), col(P_F), col(P_I), col(P_G), _full(lbp.shape), _full(ng.shape)],
        out_specs=[pl.BlockSpec((None, tb, B_DIM), lambda b, c: (b, c, 0)),
                   pl.BlockSpec((None, B_HEADS, B_HEAD, B_HEAD), lambda b, c: (b, 0, 0, 0))],
        out_shape=[jax.ShapeDtypeStruct((bsz, seq, B_DIM), BF16),
                   jax.ShapeDtypeStruct((bsz, B_HEADS, B_HEAD, B_HEAD), F32)],
        scratch_shapes=[pltpu.VMEM((B_HEADS, B_HEAD, B_HEAD), F32)]
                       + [pltpu.VMEM((tb, B_DIM), F32)] * 4,
        compiler_params=_cparams(("parallel", "arbitrary")),
        name="hgrn_prompt",
    )(proj, proj, proj, proj, lbp, ng)
    return o, st


def _hgrn_step_kernel(q_ref, f_ref, i_ref, g_ref, lbp, ng, s_ref, o_ref, so_ref, o_scr):
    bb = STEP_BB
    qf, kf, logf = _hgrn_gates(q_ref[...], f_ref[...], lbp[...])
    vf = i_ref[...]
    dec_t = jnp.exp(logf).T
    k_t = kf.T
    q_t = qf.T
    for b in range(bb):
        for h in range(B_HEADS):
            hs = slice(h * B_HEAD, (h + 1) * B_HEAD)
            sh = s_ref[b, hs, :] * dec_t[hs, b:b + 1] + k_t[hs, b:b + 1] * vf[b:b + 1, hs]
            so_ref[b, hs, :] = sh
            o_scr[b:b + 1, hs] = jnp.sum(q_t[hs, b:b + 1] * sh, axis=0, keepdims=True)
    o_ref[...] = _hgrn_out(o_scr[...], g_ref[...], ng[...]).astype(BF16)


def _hgrn_step(proj, s_all, layer, lbp, ng):
    bsz = proj.shape[0]
    bb = STEP_BB
    sflat = s_all.reshape(s_all.shape[0], bsz, B_DIM, B_HEAD)

    def col(off):
        return pl.BlockSpec((bb, B_DIM), lambda i: (i, off // B_DIM))

    o, sn = pl.pallas_call(
        _hgrn_step_kernel,
        grid=(bsz // bb,),
        in_specs=[col(P_Q), col(P_F), col(P_I), col(P_G), _full(lbp.shape), _full(ng.shape),
                  pl.BlockSpec((None, bb, B_DIM, B_HEAD), lambda i: (layer, i, 0, 0))],
        out_specs=[pl.BlockSpec((bb, B_DIM), lambda i: (i, 0)),
                   pl.BlockSpec((bb, B_DIM, B_HEAD), lambda i: (i, 0, 0))],
        out_shape=[jax.ShapeDtypeStruct((bsz, B_DIM), BF16),
                   jax.ShapeDtypeStruct((bsz, B_DIM, B_HEAD), F32)],
        scratch_shapes=[pltpu.VMEM((bb, B_DIM), F32)],
        compiler_params=_cparams(("parallel",)),
        name="hgrn_step",
    )(proj, proj, proj, proj, lbp, ng, sflat)
    return o, sn.reshape(s_all.shape[1:])


RW_NAMES = ("mu_r", "mu_k", "mu_v", "mu_lr", "w0", "w2", "a0", "a2", "g2", "k_k", "k_a", "r_k")


def _rwkv_params(p):
    mu = p["shift_mu"].reshape(1, C_COLS)
    row = lambda a: a.reshape(1, C_DIM)
    return dict(
        mu_r=mu[:, :C_DIM], mu_k=mu[:, C_DIM:2 * C_DIM], mu_v=mu[:, 2 * C_DIM:3 * C_DIM],
        mu_lr=mu[:, 3 * C_DIM:],
        w0=row(p["w0"]), w2=p["w2"].astype(BF16), a0=row(p["a0"]), a2=p["a2"].astype(BF16),
        g2=p["g2"].astype(BF16), k_k=row(p["k_k"]), k_a=row(p["k_a"]), r_k=row(p["r_k"]),
        lnx_w=row(p["lnx_w"]), lnx_b=row(p["lnx_b"]),
    )


def _head_sum(x):
    return [jnp.sum(x[:, h * C_HEAD:(h + 1) * C_HEAD], axis=-1, keepdims=True) for h in range(C_HEADS)]


def _head_bcast(cols, fn=lambda c: c):
    rows = cols[0].shape[0]
    return jnp.concatenate([jnp.broadcast_to(fn(c), (rows, C_HEAD)) for c in cols], axis=-1)


def _rwkv_prep(r, k, v, lr, pr, pk, pv, plr, prm):
    mu_r, mu_k, mu_v, mu_lr, w0, w2, a0, a2, g2, k_k, k_a, r_k = prm
    r = r + (pr - r) * mu_r
    k = k + (pk - k) * mu_k
    v = v + (pv - v) * mu_v
    lr = lr + (plr - lr) * mu_lr
    wl, al, gl = lr[:, :C_LR_W], lr[:, C_LR_W:C_LR_W + C_LR_A], lr[:, C_LR_W + C_LR_A:]
    wraw = -_softplus(-(w0 + _dot(jnp.tanh(wl), w2))) - 0.5
    ld = -jnp.exp(wraw)
    a = jax.nn.sigmoid(a0 + _dot(al, a2))
    g = _dot(jax.nn.sigmoid(gl), g2)
    kk = k * k_k
    kk = kk * _head_bcast(_head_sum(kk * kk), lambda n2: 1.0 / jnp.maximum(jnp.sqrt(n2), L2_EPS))
    k2 = k * (1.0 + (a - 1.0) * k_a)
    bonus = _head_bcast(_head_sum(r * k2 * r_k)) * v
    return r, k2, v, kk, kk * a, ld, g, bonus


def _rwkv_out(y, bonus, g, lnx_w, lnx_b):
    n = float(C_HEAD)
    mean = _head_bcast(_head_sum(y), lambda s: s / n)
    yc = y - mean
    rstd = _head_bcast(_head_sum(yc * yc), lambda s: lax.rsqrt(s / n + GN_EPS))
    return (yc * rstd * lnx_w + lnx_b + bonus) * g


RW_GH = 4
RW_GW = RW_GH * C_HEAD


def _bd_mask():
    return _iota((RW_GW, RW_GW), 0) // C_HEAD == _iota((RW_GW, RW_GW), 1) // C_HEAD


def _bd_rows(a):
    return jnp.where(_bd_mask(), jnp.concatenate([a] * RW_GH, axis=0), 0.0)


def _bd_lanes(s):
    return jnp.where(_bd_mask(), jnp.concatenate([s] * RW_GH, axis=1), 0.0)


def _bd_fold(m):
    m = jnp.where(_bd_mask(), m, 0.0)
    out = m[:, :C_HEAD]
    for h in range(1, RW_GH):
        out = out + m[:, h * C_HEAD:(h + 1) * C_HEAD]
    return out


def _to_stack(a):
    return jnp.concatenate([a[:, h * C_HEAD:(h + 1) * C_HEAD] for h in range(RW_GH)], axis=0)


def _to_all(s):
    return jnp.concatenate([s[h * C_HEAD:(h + 1) * C_HEAD, :] for h in range(RW_GH)], axis=1)


def _rwkv_a_kernel(r_ref, k_ref, v_ref, lr_ref, rp_ref, kp_ref, vp_ref, lrp_ref, *rest):
    prm = tuple(x[...] for x in rest[:len(RW_NAMES)])
    (kd_ref, rd_ref, bg_ref, ge_ref, bonus_ref, gate_ref, aab_ref, rb_ref, m2_ref, rkv_ref, vk_ref,
     xs_scr, b1_scr, b2_scr, b3_scr, b4_scr) = rest[len(RW_NAMES):]
    c = pl.program_id(1)
    tb, cs = RWKV_BLOCK, RWKV_CHUNK
    first = (c == 0)

    def prev(cur_ref, prev_ref, width):
        xs_scr[8:8 + tb, 0:width] = cur_ref[...]
        xs_scr[7:8, 0:width] = jnp.where(first, 0.0, prev_ref[7:8, :])
        return xs_scr[pl.ds(7, tb), 0:width]

    pr = prev(r_ref, rp_ref, C_DIM)
    pk = prev(k_ref, kp_ref, C_DIM)
    pv = prev(v_ref, vp_ref, C_DIM)
    plr = prev(lr_ref, lrp_ref, C_LR)
    r, k2, v, kk, beta, ld, g, bonus = _rwkv_prep(
        r_ref[...], k_ref[...], v_ref[...], lr_ref[...], pr, pk, pv, plr, prm)
    bonus_ref[...] = bonus
    gate_ref[...] = g

    r_i, c_i = _iota((tb, tb), 0), _iota((tb, tb), 1)
    same = (r_i // cs == c_i // cs)
    lcum = _dot_hi((same & (c_i <= r_i)).astype(F32), ld)
    lend = _dot_hi(same.astype(F32), ld)
    einv = jnp.exp(-lcum)
    eend = jnp.exp(lend - lcum)
    ge_ref[...] = jnp.exp(lend)
    kd_ref[...] = kk * jnp.exp(lcum - ld)
    rd_ref[...] = r * jnp.exp(lcum)
    bg_ref[...] = beta * eend
    b1_scr[...] = beta * einv
    b2_scr[...] = k2 * einv
    b3_scr[...] = k2 * eend
    b4_scr[...] = v
    gw = RW_GW
    tpos, spos = _iota((gw, cs), 0) % cs, _iota((gw, cs), 1)
    strict, incl = tpos > spos, tpos >= spos
    hc = C_HEAD

    def chunk(j, carry):
        r0 = pl.multiple_of(j * cs, cs)
        q0 = pl.multiple_of(j * gw, gw)
        for g in range(C_HEADS // RW_GH):
            gs = slice(g * gw, (g + 1) * gw)
            ls = slice(g * hc, (g + 1) * hc)
            kd = kd_ref[pl.ds(r0, cs), gs]
            rd = rd_ref[pl.ds(r0, cs), gs]
            v = b4_scr[pl.ds(r0, cs), gs]
            lhs = jnp.concatenate([_bd_rows(kd), _bd_rows(rd)], axis=0)
            rhs = jnp.concatenate([b1_scr[pl.ds(r0, cs), gs], b2_scr[pl.ds(r0, cs), gs]], axis=0)
            gm = _dot_nt(lhs, rhs)
            aab_ref[pl.ds(q0, gw), ls] = jnp.where(strict, gm[:gw, :hc], 0.0)
            aak = jnp.where(strict, gm[:gw, hc:], 0.0)
            rb_ref[pl.ds(q0, gw), ls] = jnp.where(incl, gm[gw:, :hc], 0.0)
            rk = jnp.where(incl, gm[gw:, hc:], 0.0)
            mr = _dot(jnp.concatenate([_bd_lanes(aak), _bd_lanes(rk)], axis=0), _to_stack(v))
            m2_ref[pl.ds(q0, gw), ls] = mr[:gw]
            rkv_ref[pl.ds(q0, gw), ls] = mr[gw:]
            vk_ref[pl.ds(q0, gw), ls] = _bd_fold(_dot_tn(v, b3_scr[pl.ds(r0, cs), gs]))
        return carry

    lax.fori_loop(0, tb // cs, chunk, 0)


def _tri_solve_kernel(a_ref, t_ref, a_scr, t_scr):
    n, hf = RWKV_CHUNK, RWKV_CHUNK // 2
    npb = a_scr.shape[2]
    ng = a_scr.shape[1] // n

    def load(t, carry):
        a_scr[t] = a_ref[pl.ds(t, npb, stride=n), :].T
        return carry

    lax.fori_loop(0, n, load, 0)
    jrow = _iota((hf, npb), 0)
    zero = jnp.zeros((hf, npb), F32)

    def coef(t, g, s):
        return a_scr[t, pl.ds(g * n + s, 1), :]

    def row_lo(t, carry):
        def inner(s, accs):
            return tuple(acc - coef(t, g, s) * t_scr[s, g * n:g * n + hf, :] for g, acc in enumerate(accs))

        accs = lax.fori_loop(0, t, inner, ((jrow == t).astype(F32),) * ng)
        for g, acc in enumerate(accs):
            t_scr[t, g * n:g * n + hf, :] = acc
            t_scr[t, g * n + hf:(g + 1) * n, :] = zero
        return carry

    lax.fori_loop(0, hf, row_lo, 0)

    def row_hi(t, carry):
        def inner_lo(s, accs):
            return tuple(acc - coef(t, g, s) * t_scr[s, g * n:g * n + hf, :] for g, acc in enumerate(accs))

        def inner_hi(s, accs):
            lo, hi = accs[:ng], accs[ng:]
            lo = tuple(acc - coef(t, g, s) * t_scr[s, g * n:g * n + hf, :] for g, acc in enumerate(lo))
            hi = tuple(acc - coef(t, g, s) * t_scr[s, g * n + hf:(g + 1) * n, :] for g, acc in enumerate(hi))
            return lo + hi

        lo = lax.fori_loop(0, hf, inner_lo, (zero,) * ng)
        accs = lax.fori_loop(hf, t, inner_hi, lo + ((jrow + hf == t).astype(F32),) * ng)
        for g in range(ng):
            t_scr[t, g * n:g * n + hf, :] = accs[g]
            t_scr[t, g * n + hf:(g + 1) * n, :] = accs[ng + g]
        return carry

    lax.fori_loop(hf, n, row_hi, 0)

    def store(t, carry):
        t_ref[pl.ds(t, npb, stride=n), :] = t_scr[t].T
        return carry

    lax.fori_loop(0, n, store, 0)


def _tri_solve(a):
    rows, width = a.shape
    n = RWKV_CHUNK
    npb = min(LANES, rows // n)
    return pl.pallas_call(
        _tri_solve_kernel,
        grid=(rows // (npb * n),),
        in_specs=[pl.BlockSpec((npb * n, width), lambda i: (i, 0))],
        out_specs=pl.BlockSpec((npb * n, width), lambda i: (i, 0)),
        out_shape=jax.ShapeDtypeStruct(a.shape, F32),
        scratch_shapes=[pltpu.VMEM((n, width, npb), F32)] * 2,
        compiler_params=_cparams(("parallel",)),
        name="tri_solve",
    )(a)


def _rwkv_c_kernel(t_ref, kd_ref, rd_ref, bg_ref, ge_ref, bonus_ref, gate_ref, rb_ref, m2_ref, rkv_ref, vk_ref,
                   lnw, lnb, y_ref, st_ref, s_scr, y_scr):
    c = pl.program_id(1)
    tb, cs, gw, hc = RWKV_BLOCK, RWKV_CHUNK, RW_GW, C_HEAD

    @pl.when(c == 0)
    def _():
        s_scr[...] = jnp.zeros_like(s_scr)

    def chunk(j, carry):
        r0 = pl.multiple_of(j * cs, cs)
        q0 = pl.multiple_of(j * gw, gw)
        for g in range(C_HEADS // RW_GH):
            gs = slice(g * gw, (g + 1) * gw)
            ls = slice(g * hc, (g + 1) * hc)
            rhs = jnp.concatenate([_to_stack(kd_ref[pl.ds(r0, cs), gs]), m2_ref[pl.ds(q0, gw), ls]], axis=1)
            wu = _dot(_bd_lanes(t_ref[pl.ds(q0, gw), ls]), rhs)
            rbwu = _dot(_bd_lanes(rb_ref[pl.ds(q0, gw), ls]), wu)
            rt = _to_stack(rd_ref[pl.ds(r0, cs), gs]) - rbwu[:, :hc]
            yc = rkv_ref[pl.ds(q0, gw), ls] - rbwu[:, hc:]
            wu_bd = jnp.concatenate([_bd_lanes(wu[:, :hc]), _bd_lanes(wu[:, hc:])], axis=1)
            nu = _dot_tn(wu_bd, _to_stack(bg_ref[pl.ds(r0, cs), gs]))
            ge_row = ge_ref[pl.ds(r0, 1), gs]
            ge_st = jnp.concatenate(
                [jnp.broadcast_to(ge_row[:, h * hc:(h + 1) * hc], (hc, hc)) for h in range(RW_GH)], axis=0)
            s = s_scr[g]
            s_bd = _bd_lanes(s)
            y_scr[pl.ds(r0, cs), gs] = _dot_nt(_to_all(rt), s_bd) + _to_all(yc)
            s_scr[g] = s * ge_st - _dot(s_bd, nu[:gw]) + (vk_ref[pl.ds(q0, gw), ls] - nu[gw:])
        return carry

    lax.fori_loop(0, tb // cs, chunk, 0)
    y_ref[...] = _rwkv_out(y_scr[...], bonus_ref[...], gate_ref[...], lnw[...], lnb[...]).astype(BF16)

    @pl.when(c == pl.num_programs(1) - 1)
    def _():
        st_ref[...] = s_scr[...]


def _rwkv_prompt(proj, rp):
    bsz, seq, _ = proj.shape
    tb, cs = RWKV_BLOCK, RWKV_CHUNK
    nblk = seq // tb

    def col(width, off):
        return pl.BlockSpec((None, tb, width), lambda b, c: (b, c, off // width))

    def pcol(width, off):
        return pl.BlockSpec((None, 8, width), lambda b, c: (b, jnp.maximum(c * (tb // 8) - 1, 0), off // width))

    ng = C_HEADS // RW_GH
    srows = tb // cs * RW_GW
    tok = pl.BlockSpec((None, tb, C_DIM), lambda b, c: (b, c, 0))
    stk = pl.BlockSpec((None, srows, ng * C_HEAD), lambda b, c: (b, c, 0))
    tok_shape = jax.ShapeDtypeStruct((bsz, seq, C_DIM), F32)
    stk_shape = jax.ShapeDtypeStruct((bsz, nblk * srows, ng * C_HEAD), F32)
    prm = [rp[n] for n in RW_NAMES]
    outs = pl.pallas_call(
        _rwkv_a_kernel,
        grid=(bsz, nblk),
        in_specs=[col(C_DIM, P_R), col(C_DIM, P_K), col(C_DIM, P_V), col(C_LR, P_LR),
                  pcol(C_DIM, P_R), pcol(C_DIM, P_K), pcol(C_DIM, P_V), pcol(C_LR, P_LR)]
                 + [_full(x.shape) for x in prm],
        out_specs=[tok] * 6 + [stk] * 5,
        out_shape=[tok_shape] * 6 + [stk_shape] * 5,
        scratch_shapes=[pltpu.VMEM((tb + 8, C_DIM), F32)] + [pltpu.VMEM((tb, C_DIM), F32)] * 4,
        compiler_params=_cparams(("parallel", "parallel")),
        name="rwkv_prep",
    )(proj, proj, proj, proj, proj, proj, proj, proj, *prm)
    kd, rd, bg, ge, bonus, gate, aab, rb, m2, rkv, vk = outs

    tmat = _tri_solve(aab.reshape(-1, ng * C_HEAD)).reshape(stk_shape.shape)

    y, st = pl.pallas_call(
        _rwkv_c_kernel,
        grid=(bsz, nblk),
        in_specs=[stk] + [tok] * 6 + [stk] * 4 + [_full((1, C_DIM))] * 2,
        out_specs=[tok, pl.BlockSpec((None, ng, RW_GW, C_HEAD), lambda b, c: (b, 0, 0, 0))],
        out_shape=[jax.ShapeDtypeStruct((bsz, seq, C_DIM), BF16),
                   jax.ShapeDtypeStruct((bsz, ng, RW_GW, C_HEAD), F32)],
        scratch_shapes=[pltpu.VMEM((ng, RW_GW, C_HEAD), F32), pltpu.VMEM((tb, C_DIM), F32)],
        compiler_params=_cparams(("parallel", "arbitrary")),
        name="rwkv_scan",
    )(tmat, kd, rd, bg, ge, bonus, gate, rb, m2, rkv, vk, rp["lnx_w"], rp["lnx_b"])
    return y, st.reshape(bsz, C_HEADS, C_HEAD, C_HEAD)


def _rwkv_step_kernel(r_ref, k_ref, v_ref, lr_ref, sh_ref, *rest):
    prm = tuple(x[...] for x in rest[:len(RW_NAMES)])
    lnw, lnb, s_ref, y_ref, so_ref = rest[len(RW_NAMES):]
    bb = STEP_BB
    sh = sh_ref[...]
    r, k2, v, kk, beta, ld, g, bonus = _rwkv_prep(
        r_ref[...], k_ref[...], v_ref[...], lr_ref[...],
        sh[:, :C_DIM], sh[:, C_DIM:2 * C_DIM], sh[:, 2 * C_DIM:3 * C_DIM], sh[:, 3 * C_DIM:], prm)
    w = jnp.exp(ld)
    v_t = v.T
    lane = _iota((C_DIM, LANES), 1)
    heads = [[x[:, h * C_HEAD:(h + 1) * C_HEAD] for x in (kk, w, beta, k2, r)] for h in range(C_HEADS)]

    def rows(i, b):
        return jnp.concatenate(
            [jnp.broadcast_to(heads[h][i][b:b + 1, :], (C_HEAD, C_HEAD)) for h in range(C_HEADS)], axis=0)

    y_t = jnp.zeros((C_DIM, LANES), F32)
    for b in range(bb):
        s = s_ref[b]
        sa = -jnp.sum(s * rows(0, b), axis=-1, keepdims=True)
        sn = s * rows(1, b) + sa * rows(2, b) + v_t[:, b:b + 1] * rows(3, b)
        so_ref[b] = sn
        y_t = jnp.where(lane == b, jnp.sum(sn * rows(4, b), axis=-1, keepdims=True), y_t)
    y = y_t.T[0:bb, :]
    y_ref[...] = _rwkv_out(y, bonus, g, lnw[...], lnb[...]).astype(BF16)


def _rwkv_step(proj, shift, s_all, layer, rp):
    bsz = proj.shape[0]
    bb = STEP_BB
    sflat = s_all.reshape(s_all.shape[0], bsz, C_DIM, C_HEAD)

    def col(width, off):
        return pl.BlockSpec((bb, width), lambda i: (i, off // width))

    prm = [rp[n] for n in RW_NAMES]
    y, sn = pl.pallas_call(
        _rwkv_step_kernel,
        grid=(bsz // bb,),
        in_specs=[col(C_DIM, P_R), col(C_DIM, P_K), col(C_DIM, P_V), col(C_LR, P_LR),
                  pl.BlockSpec((bb, C_COLS), lambda i: (i, 0))]
                 + [_full(x.shape) for x in prm] + [_full((1, C_DIM))] * 2
                 + [pl.BlockSpec((None, bb, C_DIM, C_HEAD), lambda i: (layer, i, 0, 0))],
        out_specs=[pl.BlockSpec((bb, C_DIM), lambda i: (i, 0)),
                   pl.BlockSpec((bb, C_DIM, C_HEAD), lambda i: (i, 0, 0))],
        out_shape=[jax.ShapeDtypeStruct((bsz, C_DIM), BF16),
                   jax.ShapeDtypeStruct((bsz, C_DIM, C_HEAD), F32)],
        compiler_params=_cparams(("parallel",)),
        name="rwkv_step",
    )(proj, proj, proj, proj, shift, *prm, rp["lnx_w"], rp["lnx_b"], sflat)
    return y, sn.reshape(s_all.shape[1:])


def _layout_w_in_kernel(w_ref, o_ref):
    o = A_DIM + CONV_CH
    rows = w_ref.shape[0]
    o_ref[:, :o] = w_ref[:, :o].astype(BF16)
    o_ref[:, o:P_DT] = w_ref[:, o + A_HEADS:].astype(BF16)
    dt_tile = jnp.where(_iota((rows, LANES), 1) < A_HEADS, w_ref[:, o:o + LANES], 0.0)
    o_ref[:, P_DT:P_DT + LANES] = dt_tile.astype(BF16)
    o_ref[:, P_DT + LANES:] = jnp.zeros((rows, P_COLS - P_DT - LANES), BF16)


def _layout_w_in(w_all, layer):
    k = w_all.shape[1]
    tr = 256
    return pl.pallas_call(
        _layout_w_in_kernel,
        grid=(k // tr,),
        in_specs=[pl.BlockSpec((None, tr, IN_COLS), lambda i: (layer, i, 0))],
        out_specs=pl.BlockSpec((tr, P_COLS), lambda i: (i, 0)),
        out_shape=jax.ShapeDtypeStruct((k, P_COLS), BF16),
        compiler_params=_cparams(("parallel",)),
        name="layout_w_in",
    )(w_all)


def _xbc_cols(proj):
    return proj[..., P_X:P_X + CONV_CH]


def _moe_route(logits, n_tok):
    top_logit, top_idx = lax.top_k(logits, TOP_K)
    gate = jax.nn.softmax(top_logit, axis=-1)
    n_assign = n_tok * TOP_K
    flat_e = top_idx.reshape(-1).astype(jnp.int32)
    order = jnp.argsort(flat_e)
    onehot = jax.nn.one_hot(flat_e, N_EXPERTS, dtype=jnp.int32)
    counts = jnp.sum(onehot, axis=0)
    padded = (counts + MOE_BM - 1) // MOE_BM * MOE_BM
    pad_end = jnp.cumsum(padded)
    pad_start = pad_end - padded
    start = jnp.cumsum(counts) - counts
    rank = jnp.sum((jnp.cumsum(onehot, axis=0) - onehot) * onehot, axis=1)
    slot_of = (pad_start[flat_e] + rank).reshape(n_tok, TOP_K)
    n_blocks = -(-n_assign // MOE_BM) + N_EXPERTS
    n_slots = n_blocks * MOE_BM
    block_start = jnp.arange(n_blocks, dtype=jnp.int32) * MOE_BM
    nvalid = (pad_end[-1] // MOE_BM).astype(jnp.int32)
    bexp = jnp.minimum(jnp.searchsorted(pad_end, block_start, side="right"), N_EXPERTS - 1).astype(jnp.int32)
    slot_e = jnp.repeat(bexp, MOE_BM)
    within = jnp.arange(n_slots, dtype=jnp.int32) - pad_start[slot_e]
    src = order[jnp.clip(start[slot_e] + within, 0, n_assign - 1)] // TOP_K
    slot_tok = jnp.where(within < counts[slot_e], src, n_tok)
    left = counts[bexp] - (block_start - pad_start[bexp])
    nsub = jnp.clip((left + MOE_SUB - 1) // MOE_SUB, 0, MOE_BM // MOE_SUB).astype(jnp.int32)
    last = bexp[jnp.maximum(nvalid - 1, 0)]
    bexp = jnp.where(jnp.arange(n_blocks) < nvalid, bexp, last)
    first = jnp.concatenate([jnp.ones((1,), jnp.int32), (bexp[1:] != bexp[:-1]).astype(jnp.int32)])
    return gate, slot_tok, slot_of, bexp, nvalid.reshape(1), first, nsub


def _moe(xp, xs, h, logits, wg, wu, wd, g, final):
    tp = xp.shape[0]
    n_tok = h.shape[0]
    gate, slot_tok, slot_of, bexp, nvalid, first, nsub = _moe_route(logits[:, :N_EXPERTS], n_tok)
    xb = h[jnp.where(slot_tok == n_tok, 0, slot_tok)]
    yb = _moe_experts(xb, bexp, nvalid, first, nsub, wg, wu, wd)
    return (_moe_combine(xp, yb, slot_of[:tp], gate[:tp], g, final),
            _moe_combine(xs, yb, slot_of[tp:], gate[tp:], g, final))


def kernel(x_prompt, x_sample, state_ssm, state_conv, state_hgrn, state_rwkv, state_shift, norm1_g, w_in, conv_w, conv_b, dt_bias, a_log, d_skip, ssm_norm_g, lb_logits, hgrn_norm_g, shift_mu, w0, w2, a0, a2, g2, k_k, k_a, r_k, lnx_w, lnx_b, w_out, norm2_g, ffn_w_gate, ffn_w_up, ffn_w_down, router_w, exp_w_gate, exp_w_up, exp_w_down, final_norm_g):
    depth = w_in.shape[0]
    bp, seq, _ = x_prompt.shape
    bs = x_sample.shape[0]
    tp = bp * seq
    lb_soft = jax.nn.softmax(lb_logits.astype(F32), axis=0)
    lower_bounds = jnp.clip(jnp.cumsum(lb_soft, axis=0) - lb_soft[0:1], 0.0, 1.0)

    xp = x_prompt.reshape(tp, D_MODEL)
    xs = x_sample.reshape(bs, D_MODEL)
    outs = {k: [] for k in ("p_ssm", "p_conv", "p_hgrn", "p_rwkv", "p_shift",
                            "s_ssm", "s_conv", "s_hgrn", "s_rwkv", "s_shift")}
    normed = False
    for layer in range(depth):
        p = dict(conv_w=conv_w[layer], conv_b=conv_b[layer], dt_bias=dt_bias[layer], a_log=a_log[layer],
                 d_skip=d_skip[layer], ssm_norm_g=ssm_norm_g[layer], shift_mu=shift_mu[layer],
                 w0=w0[layer], w2=w2[layer], a0=a0[layer], a2=a2[layer], g2=g2[layer], k_k=k_k[layer],
                 k_a=k_a[layer], r_k=r_k[layer].reshape(-1), lnx_w=lnx_w[layer], lnx_b=lnx_b[layer])
        sp = _ssd_params(p)
        rp = _rwkv_params(p)
        lb = lower_bounds[layer]
        lbp = jnp.stack([1.0 - lb, jnp.log(lb), jnp.log1p(-lb)])
        hng = hgrn_norm_g[layer].reshape(1, B_HEAD)
        w_in_l = _layout_w_in(w_in, layer)
        w_out_l = w_out[layer].astype(BF16)
        j = layer // 2
        routed = layer % 2 == 1
        if routed:
            rw_f = jnp.pad(router_w[j], ((0, 0), (0, LANES - N_EXPERTS)))
            rw_hi = rw_f.astype(BF16)
            rw = jnp.concatenate([rw_hi, (rw_f - rw_hi.astype(F32)).astype(BF16)], axis=1)
            h_all = jnp.zeros((tp + bs, D_MODEL), BF16)
            l_all = jnp.zeros((tp + bs, LANES), F32)

        proj = _norm_matmul(xp, norm1_g[layer], w_in_l, 1024, 512).reshape(bp, seq, P_COLS)
        ya, ssm_p = _ssd_prompt(proj, sp)
        yb, hgrn_p = _hgrn_prompt(proj, lbp, hng)
        yc, rwkv_p = _rwkv_prompt(proj, rp)
        xp, hp, l_all = _out_proj(ya.reshape(tp, A_DIM), yb.reshape(tp, B_DIM), yc.reshape(tp, C_DIM),
                                  w_out_l, xp, norm2_g[layer], 256, (rw, h_all, l_all, 0) if routed else None)
        outs["p_ssm"].append(ssm_p)
        outs["p_conv"].append(_xbc_cols(proj[:, seq - (CONV_W - 1):, :]))
        outs["p_hgrn"].append(hgrn_p)
        outs["p_rwkv"].append(rwkv_p)
        outs["p_shift"].append(proj[:, seq - 1, P_R:P_R + C_COLS])

        projs = _norm_matmul(xs, norm1_g[layer], w_in_l, 128, 512)
        ya, ssm_s = _ssd_step(projs, state_conv[layer], state_ssm, layer, p, sp)
        yb, hgrn_s = _hgrn_step(projs, state_hgrn, layer, lbp, hng)
        yc, rwkv_s = _rwkv_step(projs, state_shift[layer], state_rwkv, layer, rp)
        xs, hs, l_all = _out_proj(ya, yb, yc, w_out_l, xs, norm2_g[layer], 128,
                                  (rw, hp, l_all, tp) if routed else None)
        outs["s_ssm"].append(ssm_s)
        outs["s_conv"].append(jnp.concatenate([state_conv[layer][:, 1:, :], _xbc_cols(projs)[:, None, :]], axis=1))
        outs["s_hgrn"].append(hgrn_s)
        outs["s_rwkv"].append(rwkv_s)
        outs["s_shift"].append(projs[:, P_R:P_R + C_COLS])

        if not routed:
            xp = _ffn(xp, hp, ffn_w_gate[j], ffn_w_up[j], ffn_w_down[j], 512)
            xs = _ffn(xs, hs, ffn_w_gate[j], ffn_w_up[j], ffn_w_down[j], 128)
        else:
            normed = layer == depth - 1
            xp, xs = _moe(xp, xs, hs, l_all, exp_w_gate[j], exp_w_up[j], exp_w_down[j], final_norm_g, normed)

    if not normed:
        xp, xs = _final_norm(xp, final_norm_g, 512), _final_norm(xs, final_norm_g, 128)
    y_prompt = xp.reshape(bp, seq, D_MODEL)
    y_sample = xs.reshape(bs, 1, D_MODEL)
    st = {k: jnp.stack(v) for k, v in outs.items()}
    return (y_prompt, y_sample, st["p_ssm"], st["p_conv"], st["p_hgrn"], st["p_rwkv"], st["p_shift"],
            st["s_ssm"], st["s_conv"], st["s_hgrn"], st["s_rwkv"], st["s_shift"])
```

```python
import functools
import math

import jax
import jax.numpy as jnp
from jax import lax
from jax.experimental import pallas as pl
from jax.experimental.pallas import tpu as pltpu

F32 = jnp.float32
BF16 = jnp.bfloat16
HI = lax.Precision.HIGHEST

D_MODEL = 2048
A_DIM, A_HEAD, A_HEADS, A_GROUPS, A_STATE = 1024, 64, 16, 2, 128
A_GN = A_GROUPS * A_STATE
CONV_W = 4
CONV_CH = A_DIM + 2 * A_GN
B_DIM, B_HEAD, B_HEADS = 512, 128, 4
C_DIM, C_HEAD, C_HEADS = 512, 64, 8
C_LR_W, C_LR_A, C_LR_G = 64, 64, 128
C_LR = C_LR_W + C_LR_A + C_LR_G
C_COLS = 3 * C_DIM + C_LR
IN_COLS = A_DIM + CONV_CH + A_HEADS + 4 * B_DIM + C_COLS
D_FF, N_EXPERTS, TOP_K, E_FF = 5504, 8, 2, 7168
NORM_EPS, GN_EPS, L2_EPS = 1e-6, 64e-5, 1e-12

P_Z, P_X, P_B, P_C = 0, 1024, 2048, 2304
P_Q, P_F, P_I, P_G = 2560, 3072, 3584, 4096
P_R, P_K, P_V, P_LR, P_DT, P_COLS = 4608, 5120, 5632, 6144, 6400, 6656
LANES = 128
DT_PAD = P_COLS - P_DT - A_HEADS

SSD_CHUNK = 256
HGRN_BLOCK, HGRN_CHUNK = 256, 16
RWKV_BLOCK, RWKV_CHUNK = 256, 64
STEP_BB = 8

FFN_TF, FFN_TN = 512, 512
MOE_BM, MOE_SUB, MOE_TF, MOE_TN = 512, 256, 1024, 512

NT_DIMS = (((1,), (1,)), ((), ()))
TN_DIMS = (((0,), (0,)), ((), ()))


def _cparams(sem, vmem_mb=48):
    return pltpu.CompilerParams(dimension_semantics=sem, vmem_limit_bytes=vmem_mb * 1024 * 1024)


def _dot(a, b):
    return jnp.dot(a.astype(BF16), b.astype(BF16), preferred_element_type=F32)


def _dot_nt(a, b):
    return lax.dot_general(a.astype(BF16), b.astype(BF16), NT_DIMS, preferred_element_type=F32)


def _dot_tn(a, b):
    return lax.dot_general(a.astype(BF16), b.astype(BF16), TN_DIMS, preferred_element_type=F32)


def _dot_hi(a, b):
    return jnp.dot(a, b, precision=HI, preferred_element_type=F32)


def _dot_nt_hi(a, b):
    return lax.dot_general(a, b, NT_DIMS, precision=HI, preferred_element_type=F32)


def _dot_tn_hi(a, b):
    return lax.dot_general(a, b, TN_DIMS, precision=HI, preferred_element_type=F32)


def _silu(x):
    return x * jax.nn.sigmoid(x)


def _softplus(x):
    return jnp.maximum(x, 0.0) + jnp.log1p(jnp.exp(-jnp.abs(x)))


def _iota(shape, dim):
    return lax.broadcasted_iota(jnp.int32, shape, dim)


def _rms(x, g):
    return x * lax.rsqrt(jnp.mean(x * x, axis=-1, keepdims=True) + NORM_EPS) * g


def _full(shape):
    nd = len(shape)
    return pl.BlockSpec(shape, lambda *_: (0,) * nd)


def _norm_matmul_kernel(x_ref, g_ref, w_ref, o_ref, h_scr):
    @pl.when(pl.program_id(1) == 0)
    def _():
        h_scr[...] = _rms(x_ref[...], g_ref[...]).astype(BF16)

    o_ref[...] = jnp.dot(h_scr[...], w_ref[...], preferred_element_type=F32)


def _norm_matmul(x, g, w, tm, tn):
    m, k = x.shape
    n = w.shape[1]
    return pl.pallas_call(
        _norm_matmul_kernel,
        grid=(m // tm, n // tn),
        in_specs=[pl.BlockSpec((tm, k), lambda i, j: (i, 0)),
                  pl.BlockSpec((1, k), lambda i, j: (0, 0)),
                  pl.BlockSpec((k, tn), lambda i, j: (0, j))],
        out_specs=pl.BlockSpec((tm, tn), lambda i, j: (i, j)),
        out_shape=jax.ShapeDtypeStruct((m, n), F32),
        scratch_shapes=[pltpu.VMEM((tm, k), BF16)],
        compiler_params=_cparams(("parallel", "arbitrary")),
        name="norm_matmul",
    )(x, g.reshape(1, k), w)


def _out_proj_kernel(routed, ya_ref, yb_ref, yc_ref, w_ref, r_ref, g_ref, *rest):
    x = (r_ref[...]
         + jnp.dot(ya_ref[...], w_ref[0:A_DIM, :], preferred_element_type=F32)
         + jnp.dot(yb_ref[...], w_ref[A_DIM:A_DIM + B_DIM, :], preferred_element_type=F32)
         + jnp.dot(yc_ref[...], w_ref[A_DIM + B_DIM:, :], preferred_element_type=F32))
    h = _rms(x, g_ref[...])
    h_hi = h.astype(BF16)
    if routed:
        rw_ref, _, _, x_ref, h_ref, l_ref = rest
        h_lo = (h - h_hi.astype(F32)).astype(BF16)
        rw = rw_ref[...]
        l2 = jnp.dot(h_hi, rw, preferred_element_type=F32)
        l_ref[...] = (l2[:, :LANES] + l2[:, LANES:]
                      + jnp.dot(h_lo, rw[:, :LANES], preferred_element_type=F32))
    else:
        x_ref, h_ref = rest
    x_ref[...] = x
    h_ref[...] = h_hi


def _out_proj(ya, yb, yc, w, res, g, tm, routed=None):
    m = ya.shape[0]
    n = w.shape[1]
    row = lambda width: pl.BlockSpec((tm, width), lambda i: (i, 0))
    in_specs = [row(A_DIM), row(B_DIM), row(C_DIM), _full(w.shape), row(n), _full((1, n))]
    args = [ya, yb, yc, w, res, g.reshape(1, n)]
    aliases = {}
    if routed is None:
        out_specs = [row(n), row(n)]
        out_shape = [jax.ShapeDtypeStruct((m, n), F32), jax.ShapeDtypeStruct((m, n), BF16)]
    else:
        rw, h_buf, l_buf, row0 = routed
        blk0 = row0 // tm
        off = lambda width: pl.BlockSpec((tm, width), lambda i: (i + blk0, 0))
        in_specs += [_full(rw.shape), pl.BlockSpec(memory_space=pl.ANY), pl.BlockSpec(memory_space=pl.ANY)]
        args += [rw, h_buf, l_buf]
        out_specs = [row(n), off(n), off(LANES)]
        out_shape = [jax.ShapeDtypeStruct((m, n), F32), jax.ShapeDtypeStruct(h_buf.shape, BF16),
                     jax.ShapeDtypeStruct(l_buf.shape, F32)]
        aliases = {len(args) - 2: 1, len(args) - 1: 2}
    outs = pl.pallas_call(
        functools.partial(_out_proj_kernel, routed is not None),
        grid=(m // tm,),
        in_specs=in_specs, out_specs=out_specs, out_shape=out_shape,
        input_output_aliases=aliases,
        compiler_params=_cparams(("parallel",)),
        name="out_proj",
    )(*args)
    return outs if routed is not None else (*outs, None)


def _ffn_up_kernel(h_ref, wg_ref, wu_ref, o_ref, wg_s, wu_s):
    @pl.when(pl.program_id(1) == 0)
    def _():
        wg_s[...] = wg_ref[...].astype(BF16)
        wu_s[...] = wu_ref[...].astype(BF16)

    h = h_ref[...]
    a = jnp.dot(h, wg_s[...], preferred_element_type=F32)
    u = jnp.dot(h, wu_s[...], preferred_element_type=F32)
    o_ref[...] = (_silu(a) * u).astype(BF16)


def _ffn_down_kernel(h_ref, wd_ref, x_ref, o_ref, wd_s):
    @pl.when(pl.program_id(1) == 0)
    def _():
        wd_s[...] = wd_ref[...].astype(BF16)

    o_ref[...] = x_ref[...] + jnp.dot(h_ref[...], wd_s[...], preferred_element_type=F32)


def _ffn(x, h, wg, wu, wd, tm):
    m, k = h.shape
    ff = wg.shape[1]
    tf, tn = FFN_TF, FFN_TN
    tu = 2 * tm if m % (2 * tm) == 0 else tm
    act = pl.pallas_call(
        _ffn_up_kernel,
        grid=(pl.cdiv(ff, tf), m // tu),
        in_specs=[pl.BlockSpec((tu, k), lambda f, i: (i, 0)),
                  pl.BlockSpec((k, tf), lambda f, i: (0, f)),
                  pl.BlockSpec((k, tf), lambda f, i: (0, f))],
        out_specs=pl.BlockSpec((tu, tf), lambda f, i: (i, f)),
        out_shape=jax.ShapeDtypeStruct((m, ff), BF16),
        scratch_shapes=[pltpu.VMEM((k, tf), BF16)] * 2,
        compiler_params=_cparams(("arbitrary", "arbitrary")),
        name="ffn_up",
    )(h, wg, wu)
    return pl.pallas_call(
        _ffn_down_kernel,
        grid=(k // tn, m // tm),
        in_specs=[pl.BlockSpec((tm, ff), lambda n, i: (i, 0)),
                  pl.BlockSpec((ff, tn), lambda n, i: (0, n)),
                  pl.BlockSpec((tm, tn), lambda n, i: (i, n))],
        out_specs=pl.BlockSpec((tm, tn), lambda n, i: (i, n)),
        out_shape=jax.ShapeDtypeStruct((m, k), F32),
        scratch_shapes=[pltpu.VMEM((ff, tn), BF16)],
        compiler_params=_cparams(("arbitrary", "arbitrary")),
        name="ffn_down",
    )(act, wd, x)


def _moe_rows(nsub, o_ref, compute):
    bm = o_ref.shape[0]
    for ns in range(bm // MOE_SUB + 1):
        @pl.when(nsub == ns)
        def _(rows=ns * MOE_SUB):
            if rows:
                o_ref[0:rows, :] = compute(rows).astype(o_ref.dtype)
            if rows < bm:
                o_ref[rows:, :] = jnp.zeros((bm - rows, o_ref.shape[1]), o_ref.dtype)


def _weight_stream(sched, w_hbms, wbuf, sem, dsts):
    bexp_ref, first_ref, run_ref, nxt_ref, lastrun_ref, nruns_ref = sched
    col, b = pl.program_id(0), pl.program_id(1)
    ncol = pl.num_programs(0)
    width = wbuf.shape[-1]

    def copies(e, c, slot):
        cols = pl.ds(pl.multiple_of(c * width, width), width)
        return [pltpu.make_async_copy(w.at[e, :, cols], wbuf.at[slot, i], sem.at[slot])
                for i, w in enumerate(w_hbms)]

    @pl.when(first_ref[b] == 1)
    def _():
        slot = lax.rem(col * nruns_ref[0] + run_ref[b], 2)
        last = lastrun_ref[b]

        @pl.when((col == 0) & (b == 0))
        def _():
            for cp in copies(bexp_ref[0], 0, 0):
                cp.start()

        @pl.when((last == 0) | (col < ncol - 1))
        def _():
            for cp in copies(nxt_ref[b], col + last, 1 - slot):
                cp.start()

        for cp in copies(bexp_ref[b], col, slot):
            cp.wait()
        for i, dst in enumerate(dsts):
            dst[...] = wbuf[slot, i].astype(BF16)


def _moe_up_kernel(*refs):
    sched, (nsub_ref, _, x_ref, wg_hbm, wu_hbm, o_ref, wbuf, sem, wg_s, wu_s) = refs[:6], refs[6:]
    _weight_stream(sched, (wg_hbm, wu_hbm), wbuf, sem, (wg_s, wu_s))

    def compute(rows):
        x = x_ref[0:rows, :]
        a = jnp.dot(x, wg_s[...], preferred_element_type=F32)
        u = jnp.dot(x, wu_s[...], preferred_element_type=F32)
        return _silu(a) * u

    _moe_rows(nsub_ref[pl.program_id(1)], o_ref, compute)


def _moe_down_kernel(*refs):
    sched, (nsub_ref, _, h_ref, wd_hbm, o_ref, wbuf, sem, wd_s) = refs[:6], refs[6:]
    _weight_stream(sched, (wd_hbm,), wbuf, sem, (wd_s,))
    _moe_rows(nsub_ref[pl.program_id(1)], o_ref,
              lambda rows: jnp.dot(h_ref[0:rows, :], wd_s[...], preferred_element_type=F32))


def _run_schedule(bexp, first):
    nb = bexp.shape[0]
    ids = jnp.arange(nb, dtype=jnp.int32)
    run = jnp.cumsum(first).astype(jnp.int32) - 1
    nruns = run[-1] + 1
    starts = jnp.where(first == 1, ids, nb)
    nxt_start = jnp.concatenate([lax.cummin(starts[::-1])[::-1][1:], jnp.full((1,), nb, jnp.int32)])
    nxt = jnp.where(nxt_start < nb, bexp[jnp.minimum(nxt_start, nb - 1)], bexp[0]).astype(jnp.int32)
    lastrun = (run == nruns - 1).astype(jnp.int32)
    return bexp, first, run, nxt, lastrun, nruns.reshape(1)


def _moe_experts(xb, bexp, first, nsub, wg, wu, wd):
    n_slots, k = xb.shape
    ff = wg.shape[2]
    nb = n_slots // MOE_BM
    tf, tn = MOE_TF, MOE_TN
    hbm = pl.BlockSpec(memory_space=pl.ANY)

    def src_blocks(ns):
        return lax.cummax(jnp.where(ns > 0, jnp.arange(ns.shape[0], dtype=jnp.int32), 0))

    act = pl.pallas_call(
        _moe_up_kernel,
        grid_spec=pltpu.PrefetchScalarGridSpec(
            num_scalar_prefetch=8,
            grid=(ff // tf, nb),
            in_specs=[pl.BlockSpec((MOE_BM, k), lambda f, b, *s: (s[7][b], 0)), hbm, hbm],
            out_specs=pl.BlockSpec((MOE_BM, tf), lambda f, b, *s: (b, f)),
            scratch_shapes=[pltpu.VMEM((2, 2, k, tf), F32), pltpu.SemaphoreType.DMA((2,)),
                            pltpu.VMEM((k, tf), BF16), pltpu.VMEM((k, tf), BF16)],
        ),
        out_shape=jax.ShapeDtypeStruct((n_slots, ff), BF16),
        compiler_params=_cparams(("arbitrary", "arbitrary"), 56),
        name="moe_up",
    )(*_run_schedule(bexp, first), nsub, src_blocks(nsub), xb, wg, wu)
    per = MOE_BM // MOE_SUB
    bexp_d = jnp.repeat(bexp, per)
    first_d = jnp.concatenate([jnp.ones((1,), jnp.int32), (bexp_d[1:] != bexp_d[:-1]).astype(jnp.int32)])
    nsub_d = jnp.clip(jnp.repeat(nsub, per) - jnp.tile(jnp.arange(per, dtype=jnp.int32), nb), 0, 1)
    return pl.pallas_call(
        _moe_down_kernel,
        grid_spec=pltpu.PrefetchScalarGridSpec(
            num_scalar_prefetch=8,
            grid=(k // tn, nb * per),
            in_specs=[pl.BlockSpec((MOE_SUB, ff), lambda n, b, *s: (s[7][b], 0)), hbm],
            out_specs=pl.BlockSpec((MOE_SUB, tn), lambda n, b, *s: (b, n)),
            scratch_shapes=[pltpu.VMEM((2, 1, ff, tn), F32), pltpu.SemaphoreType.DMA((2,)),
                            pltpu.VMEM((ff, tn), BF16)],
        ),
        out_shape=jax.ShapeDtypeStruct((n_slots, k), F32),
        compiler_params=_cparams(("arbitrary", "arbitrary"), 56),
        name="moe_down",
    )(*_run_schedule(bexp_d, first_d), nsub_d, src_blocks(nsub_d), act, wd)


def _final_norm_kernel(x_ref, g_ref, o_ref):
    o_ref[...] = _rms(x_ref[...], g_ref[...])


def _final_norm(x, g, tm):
    m, k = x.shape
    return pl.pallas_call(
        _final_norm_kernel,
        grid=(m // tm,),
        in_specs=[pl.BlockSpec((tm, k), lambda i: (i, 0)),
                  pl.BlockSpec((1, k), lambda i: (0, 0))],
        out_specs=pl.BlockSpec((tm, k), lambda i: (i, 0)),
        out_shape=jax.ShapeDtypeStruct((m, k), F32),
        compiler_params=_cparams(("parallel",)),
        name="final_norm",
    )(x, g.reshape(1, k))


def _row_copy(src_hbm, row, dst, r, sem):
    return pltpu.make_async_copy(src_hbm.at[pl.ds(row, 1), :], dst.at[pl.ds(r, 1), :], sem)


def _moe_combine_kernel(final, s0_ref, s1_ref, x_ref, gate_ref, g_ref, yb_hbm, o_ref, buf, sem):
    rows = x_ref.shape[0]
    base = pl.program_id(0) * rows

    def issue(r, carry):
        _row_copy(yb_hbm, s0_ref[base + r], buf.at[0], r, sem).start()
        _row_copy(yb_hbm, s1_ref[base + r], buf.at[1], r, sem).start()
        return carry

    lax.fori_loop(0, rows, issue, 0)
    pltpu.make_async_copy(buf, buf, sem).wait()
    gate = gate_ref[...]
    y = x_ref[...] + gate[:, 0:1] * buf[0] + gate[:, 1:2] * buf[1]
    o_ref[...] = _rms(y, g_ref[...]) if final else y


def _moe_combine(x, yb, slot_of, gate, g, final):
    m, k = x.shape
    rows = 128
    gate_p = jnp.pad(gate, ((0, 0), (0, LANES - TOP_K)))
    return pl.pallas_call(
        functools.partial(_moe_combine_kernel, final),
        grid_spec=pltpu.PrefetchScalarGridSpec(
            num_scalar_prefetch=2,
            grid=(m // rows,),
            in_specs=[pl.BlockSpec((rows, k), lambda i, s0, s1: (i, 0)),
                      pl.BlockSpec((rows, LANES), lambda i, s0, s1: (i, 0)),
                      pl.BlockSpec((1, k), lambda i, s0, s1: (0, 0)),
                      pl.BlockSpec(memory_space=pl.ANY)],
            out_specs=pl.BlockSpec((rows, k), lambda i, s0, s1: (i, 0)),
            scratch_shapes=[pltpu.VMEM((TOP_K, rows, k), F32), pltpu.SemaphoreType.DMA],
        ),
        out_shape=jax.ShapeDtypeStruct((m, k), F32),
        compiler_params=_cparams(("arbitrary",)),
        name="moe_combine",
    )(slot_of[:, 0], slot_of[:, 1], x, gate_p, g.reshape(1, k), yb)


def _ssd_gate_norm(y, z, ng):
    y = y * _silu(z)
    gw = A_DIM // A_GROUPS
    outs = []
    for g in range(A_GROUPS):
        yg = y[:, g * gw:(g + 1) * gw]
        outs.append(yg * lax.rsqrt(jnp.mean(yg * yg, axis=-1, keepdims=True) + NORM_EPS))
    return jnp.concatenate(outs, axis=-1) * ng


def _ssd_prompt_kernel(z_ref, x_ref, b_ref, c_ref, dt_ref, cwx, cwb, cwc, cbx, cbb, cbc,
                       dtb, aneg, dsk, ng, y_ref, st_ref, h_scr, xpx, xpb, xpc, yacc):
    c = pl.program_id(1)
    ch = SSD_CHUNK

    @pl.when(c == 0)
    def _():
        h_scr[...] = jnp.zeros_like(h_scr)
        xpx[0:8, :] = jnp.zeros((8, A_DIM), F32)
        xpb[0:8, :] = jnp.zeros((8, A_STATE * A_GROUPS), F32)
        xpc[0:8, :] = jnp.zeros((8, A_STATE * A_GROUPS), F32)

    xpx[8:8 + ch, :] = x_ref[...]
    xpb[8:8 + ch, :] = b_ref[...]
    xpc[8:8 + ch, :] = c_ref[...]

    def conv(xp, cw, cb):
        acc = cb[...] + xp[pl.ds(8 - (CONV_W - 1), ch), :] * cw[0:1, :]
        for tap in range(1, CONV_W):
            acc = acc + xp[pl.ds(8 - (CONV_W - 1) + tap, ch), :] * cw[tap:tap + 1, :]
        return _silu(acc)

    ux = conv(xpx, cwx, cbx)
    ub = conv(xpb, cwb, cbb)
    uc = conv(xpc, cwc, cbc)
    xpx[0:8, :] = xpx[ch:ch + 8, :]
    xpb[0:8, :] = xpb[ch:ch + 8, :]
    xpc[0:8, :] = xpc[ch:ch + 8, :]

    dt = _softplus(dt_ref[...] + dtb[...])
    la = dt * aneg[...]
    tri = (_iota((ch, ch), 1) <= _iota((ch, ch), 0)).astype(F32)
    cum = _dot_hi(tri, la)
    eye = (_iota((A_HEADS, LANES), 0) == _iota((A_HEADS, LANES), 1)).astype(F32)
    cum_r = _dot_nt_hi(eye, cum)
    dt_r = _dot_nt_hi(eye, dt)
    cl = cum[ch - 1:ch, :]
    tail = jnp.exp(cl - cum) * dt
    ecum = jnp.exp(cum)
    ecl = jnp.exp(cl)
    causal = _iota((ch, ch), 1) <= _iota((ch, ch), 0)
    rep = A_HEADS // A_GROUPS

    for g in range(A_GROUPS):
        bg = ub[:, g * A_STATE:(g + 1) * A_STATE].astype(BF16)
        cg = uc[:, g * A_STATE:(g + 1) * A_STATE].astype(BF16)
        cb_g = _dot_nt(cg, bg)
        for h in range(g * rep, (g + 1) * rep):
            hs = slice(h * A_HEAD, (h + 1) * A_HEAD)
            seg = cum[:, h:h + 1] - cum_r[h:h + 1, :]
            w = jnp.exp(jnp.where(causal, seg, -jnp.inf)) * cb_g * dt_r[h:h + 1, :]
            xh = ux[:, hs]
            hst = h_scr[h]
            y_h = _dot(w, xh) + _dot_nt(cg, hst) * ecum[:, h:h + 1]
            yacc[:, hs] = y_h
            h_scr[h] = hst * ecl[:, h:h + 1] + _dot_tn(xh * tail[:, h:h + 1], bg)

    y = yacc[...] + dsk[...] * ux
    y_ref[...] = _ssd_gate_norm(y, z_ref[...], ng[...]).astype(BF16)

    @pl.when(c == pl.num_programs(1) - 1)
    def _():
        st_ref[...] = h_scr[...]


def _ssd_params(p):
    cw, cb = p["conv_w"], p["conv_b"].reshape(1, CONV_CH)
    pad = LANES - A_HEADS
    return dict(
        cwx=cw[:, :A_DIM], cwb=cw[:, A_DIM:A_DIM + A_GN], cwc=cw[:, A_DIM + A_GN:],
        cbx=cb[:, :A_DIM], cbb=cb[:, A_DIM:A_DIM + A_GN], cbc=cb[:, A_DIM + A_GN:],
        dtb=jnp.pad(p["dt_bias"], (0, pad)).reshape(1, LANES),
        aneg=jnp.pad(-jnp.exp(p["a_log"]), (0, pad)).reshape(1, LANES),
        dsk=jnp.repeat(p["d_skip"], A_HEAD).reshape(1, A_DIM),
        ng=p["ssm_norm_g"].reshape(1, A_DIM),
    )


def _ssd_prompt(proj, sp):
    bsz, seq, _ = proj.shape
    ch = SSD_CHUNK
    gn = A_GN

    def col(width, off):
        return pl.BlockSpec((None, ch, width), lambda b, c: (b, c, off // width))

    names = ("cwx", "cwb", "cwc", "cbx", "cbb", "cbc", "dtb", "aneg", "dsk", "ng")
    y, st = pl.pallas_call(
        _ssd_prompt_kernel,
        grid=(bsz, seq // ch),
        in_specs=[col(A_DIM, P_Z), col(A_DIM, P_X), col(gn, P_B), col(gn, P_C), col(LANES, P_DT)]
                 + [_full(sp[n].shape) for n in names],
        out_specs=[pl.BlockSpec((None, ch, A_DIM), lambda b, c: (b, c, 0)),
                   pl.BlockSpec((None, A_HEADS, A_HEAD, A_STATE), lambda b, c: (b, 0, 0, 0))],
        out_shape=[jax.ShapeDtypeStruct((bsz, seq, A_DIM), BF16),
                   jax.ShapeDtypeStruct((bsz, A_HEADS, A_HEAD, A_STATE), F32)],
        scratch_shapes=[pltpu.VMEM((A_HEADS, A_HEAD, A_STATE), F32),
                        pltpu.VMEM((ch + 8, A_DIM), F32),
                        pltpu.VMEM((ch + 8, gn), F32),
                        pltpu.VMEM((ch + 8, gn), F32),
                        pltpu.VMEM((ch, A_DIM), F32)],
        compiler_params=_cparams(("parallel", "arbitrary")),
        name="ssd_prompt",
    )(proj, proj, proj, proj, proj, *[sp[n] for n in names])
    return y, st


def _ssd_step_kernel(z_ref, x_ref, b_ref, c_ref, dt_ref, cs_ref, cw, cb, dtb, anegx, expand, dsk, ng,
                     h_ref, y_ref, ho_ref, y_scr):
    bb = STEP_BB
    xbc = (x_ref[...], b_ref[...], c_ref[...])
    offs = (0, A_DIM, A_DIM + A_GN, CONV_CH)
    u = []
    for i in range(3):
        sl = slice(offs[i], offs[i + 1])
        acc = cb[:, sl] + xbc[i] * cw[CONV_W - 1:CONV_W, sl]
        for tap in range(CONV_W - 1):
            acc = acc + cs_ref[tap][:, sl] * cw[tap:tap + 1, sl]
        u.append(_silu(acc))
    ux, ub, uc = u
    dt = _softplus(dt_ref[...] + dtb[...])
    dtx = _dot_hi(dt, expand[...])
    dec_t = jnp.exp(dtx * anegx[...]).T
    xdt_t = (ux * dtx).T
    gw = A_DIM // A_GROUPS
    for b in range(bb):
        for g in range(A_GROUPS):
            rs = slice(g * gw, (g + 1) * gw)
            ns = slice(g * A_STATE, (g + 1) * A_STATE)
            hg = h_ref[b, rs, :] * dec_t[rs, b:b + 1] + xdt_t[rs, b:b + 1] * ub[b:b + 1, ns]
            ho_ref[b, rs, :] = hg
            y_scr[b:b + 1, rs] = _dot_nt(uc[:, ns], hg)[b:b + 1, :]
    y = y_scr[...] + dsk[...] * ux
    y_ref[...] = _ssd_gate_norm(y, z_ref[...], ng[...]).astype(BF16)


def _ssd_step(proj, conv_state, h_all, layer, p, sp):
    bsz = proj.shape[0]
    bb = STEP_BB
    cs = jnp.swapaxes(conv_state, 0, 1)
    hflat = h_all.reshape(h_all.shape[0], bsz, A_DIM, A_STATE)
    expand = (jnp.arange(LANES)[:, None] == (jnp.arange(A_DIM) // A_HEAD)[None, :]).astype(F32)
    anegx = jnp.repeat(-jnp.exp(p["a_log"]), A_HEAD).reshape(1, A_DIM)
    cw, cb = p["conv_w"], p["conv_b"].reshape(1, CONV_CH)

    def col(width, off):
        return pl.BlockSpec((bb, width), lambda i: (i, off // width))

    y, hn = pl.pallas_call(
        _ssd_step_kernel,
        grid=(bsz // bb,),
        in_specs=[col(A_DIM, P_Z), col(A_DIM, P_X), col(A_GN, P_B), col(A_GN, P_C), col(LANES, P_DT),
                  pl.BlockSpec((CONV_W - 1, bb, CONV_CH), lambda i: (0, i, 0)),
                  _full(cw.shape), _full(cb.shape), _full(sp["dtb"].shape), _full(anegx.shape),
                  _full(expand.shape), _full(sp["dsk"].shape), _full(sp["ng"].shape),
                  pl.BlockSpec((None, bb, A_DIM, A_STATE), lambda i: (layer, i, 0, 0))],
        out_specs=[pl.BlockSpec((bb, A_DIM), lambda i: (i, 0)),
                   pl.BlockSpec((bb, A_DIM, A_STATE), lambda i: (i, 0, 0))],
        out_shape=[jax.ShapeDtypeStruct((bsz, A_DIM), BF16),
                   jax.ShapeDtypeStruct((bsz, A_DIM, A_STATE), F32)],
        scratch_shapes=[pltpu.VMEM((bb, A_DIM), F32)],
        compiler_params=_cparams(("parallel",)),
        name="ssd_step",
    )(proj, proj, proj, proj, proj, cs, cw, cb, sp["dtb"], anegx, expand, sp["dsk"], sp["ng"], hflat)
    return y, hn.reshape(h_all.shape[1:])


def _hgrn_gates(q, f, lbp):
    qf = _silu(q)
    kf = lbp[0:1, :] * jax.nn.sigmoid(-f)
    log_sig = jnp.minimum(f, 0.0) - jnp.log1p(jnp.exp(-jnp.abs(f)))
    a = lbp[1:2, :]
    b = lbp[2:3, :] + log_sig
    logf = jnp.maximum(a, b) + jnp.log1p(jnp.exp(-jnp.abs(a - b)))
    return qf, kf, logf


def _hgrn_out(o, g, ng):
    outs = []
    for h in range(B_HEADS):
        oh = o[:, h * B_HEAD:(h + 1) * B_HEAD]
        outs.append(oh * lax.rsqrt(jnp.mean(oh * oh, axis=-1, keepdims=True) + NORM_EPS) * ng)
    return jnp.concatenate(outs, axis=-1) * _silu(g)


def _hgrn_prompt_kernel(q_ref, f_ref, i_ref, g_ref, lbp, ng, o_ref, st_ref,
                        s_scr, cum_scr, q_scr, k_scr, o_scr):
    c = pl.program_id(1)
    tb, cs = HGRN_BLOCK, HGRN_CHUNK

    @pl.when(c == 0)
    def _():
        s_scr[...] = jnp.zeros_like(s_scr)

    qf, kf, logf = _hgrn_gates(q_ref[...], f_ref[...], lbp[...])
    r_i, c_i = _iota((tb, tb), 0), _iota((tb, tb), 1)
    bd = ((r_i // cs == c_i // cs) & (c_i <= r_i)).astype(F32)
    cum_scr[...] = _dot_hi(bd, logf)
    q_scr[...] = qf
    k_scr[...] = kf
    rows = _iota((cs, B_HEAD), 0)

    def chunk(cc, carry):
        r0 = pl.multiple_of(cc * cs, cs)
        for h in range(B_HEADS):
            hs = slice(h * B_HEAD, (h + 1) * B_HEAD)
            cu = cum_scr[pl.ds(r0, cs), hs]
            q = q_scr[pl.ds(r0, cs), hs]
            k = k_scr[pl.ds(r0, cs), hs]
            v = i_ref[pl.ds(r0, cs), hs]
            st = s_scr[h]
            o = _dot_nt(q * jnp.exp(cu), st)
            for s in range(cs):
                d = jnp.exp(jnp.where(rows >= s, cu - cu[s:s + 1, :], -jnp.inf))
                att = jnp.sum(q * k[s:s + 1, :] * d, axis=-1, keepdims=True)
                o = o + att * v[s:s + 1, :]
            o_scr[pl.ds(r0, cs), hs] = o
            cl = cu[cs - 1:cs, :]
            s_scr[h] = st * jnp.exp(cl) + _dot_tn(v, k * jnp.exp(cl - cu))
        return carry

    lax.fori_loop(0, tb // cs, chunk, 0)
    o_ref[...] = _hgrn_out(o_scr[...], g_ref[...], ng[...]).astype(BF16)

    @pl.when(c == pl.num_programs(1) - 1)
    def _():
        for h in range(B_HEADS):
            st_ref[h] = s_scr[h].T


def _hgrn_prompt(proj, lbp, ng):
    bsz, seq, _ = proj.shape
    tb = HGRN_BLOCK

    def col(off):
        return pl.BlockSpec((None, tb, B_DIM), lambda b, c: (b, c, off // B_DIM))

    o, st = pl.pallas_call(
        _hgrn_prompt_kernel,
        grid=(bsz, seq // tb),
        in_specs=[col(P_Q), col(P_F), col(P_I), col(P_G), _full(lbp.shape), _full(ng.shape)],
        out_specs=[pl.BlockSpec((None, tb, B_DIM), lambda b, c: (b, c, 0)),
                   pl.BlockSpec((None, B_HEADS, B_HEAD, B_HEAD), lambda b, c: (b, 0, 0, 0))],
        out_shape=[jax.ShapeDtypeStruct((bsz, seq, B_DIM), BF16),
                   jax.ShapeDtypeStruct((bsz, B_HEADS, B_HEAD, B_HEAD), F32)],
        scratch_shapes=[pltpu.VMEM((B_HEADS, B_HEAD, B_HEAD), F32)]
                       + [pltpu.VMEM((tb, B_DIM), F32)] * 4,
        compiler_params=_cparams(("parallel", "arbitrary")),
        name="hgrn_prompt",
    )(proj, proj, proj, proj, lbp, ng)
    return o, st


def _hgrn_step_kernel(q_ref, f_ref, i_ref, g_ref, lbp, ng, s_ref, o_ref, so_ref, o_scr):
    bb = STEP_BB
    qf, kf, logf = _hgrn_gates(q_ref[...], f_ref[...], lbp[...])
    vf = i_ref[...]
    dec_t = jnp.exp(logf).T
    k_t = kf.T
    q_t = qf.T
    for b in range(bb):
        for h in range(B_HEADS):
            hs = slice(h * B_HEAD, (h + 1) * B_HEAD)
            sh = s_ref[b, hs, :] * dec_t[hs, b:b + 1] + k_t[hs, b:b + 1] * vf[b:b + 1, hs]
            so_ref[b, hs, :] = sh
            o_scr[b:b + 1, hs] = jnp.sum(q_t[hs, b:b + 1] * sh, axis=0, keepdims=True)
    o_ref[...] = _hgrn_out(o_scr[...], g_ref[...], ng[...]).astype(BF16)


def _hgrn_step(proj, s_all, layer, lbp, ng):
    bsz = proj.shape[0]
    bb = STEP_BB
    sflat = s_all.reshape(s_all.shape[0], bsz, B_DIM, B_HEAD)

    def col(off):
        return pl.BlockSpec((bb, B_DIM), lambda i: (i, off // B_DIM))

    o, sn = pl.pallas_call(
        _hgrn_step_kernel,
        grid=(bsz // bb,),
        in_specs=[col(P_Q), col(P_F), col(P_I), col(P_G), _full(lbp.shape), _full(ng.shape),
                  pl.BlockSpec((None, bb, B_DIM, B_HEAD), lambda i: (layer, i, 0, 0))],
        out_specs=[pl.BlockSpec((bb, B_DIM), lambda i: (i, 0)),
                   pl.BlockSpec((bb, B_DIM, B_HEAD), lambda i: (i, 0, 0))],
        out_shape=[jax.ShapeDtypeStruct((bsz, B_DIM), BF16),
                   jax.ShapeDtypeStruct((bsz, B_DIM, B_HEAD), F32)],
        scratch_shapes=[pltpu.VMEM((bb, B_DIM), F32)],
        compiler_params=_cparams(("parallel",)),
        name="hgrn_step",
    )(proj, proj, proj, proj, lbp, ng, sflat)
    return o, sn.reshape(s_all.shape[1:])


RW_NAMES = ("mu_r", "mu_k", "mu_v", "mu_lr", "w0", "w2", "a0", "a2", "g2", "k_k", "k_a", "r_k")


def _rwkv_params(p):
    mu = p["shift_mu"].reshape(1, C_COLS)
    row = lambda a: a.reshape(1, C_DIM)
    return dict(
        mu_r=mu[:, :C_DIM], mu_k=mu[:, C_DIM:2 * C_DIM], mu_v=mu[:, 2 * C_DIM:3 * C_DIM],
        mu_lr=mu[:, 3 * C_DIM:],
        w0=row(p["w0"]), w2=p["w2"].astype(BF16), a0=row(p["a0"]), a2=p["a2"].astype(BF16),
        g2=p["g2"].astype(BF16), k_k=row(p["k_k"]), k_a=row(p["k_a"]), r_k=row(p["r_k"]),
        lnx_w=row(p["lnx_w"]), lnx_b=row(p["lnx_b"]),
    )


def _head_sum(x):
    return [jnp.sum(x[:, h * C_HEAD:(h + 1) * C_HEAD], axis=-1, keepdims=True) for h in range(C_HEADS)]


def _head_bcast(cols, fn=lambda c: c):
    rows = cols[0].shape[0]
    return jnp.concatenate([jnp.broadcast_to(fn(c), (rows, C_HEAD)) for c in cols], axis=-1)


def _rwkv_prep(r, k, v, lr, pr, pk, pv, plr, prm):
    mu_r, mu_k, mu_v, mu_lr, w0, w2, a0, a2, g2, k_k, k_a, r_k = prm
    r = r + (pr - r) * mu_r
    k = k + (pk - k) * mu_k
    v = v + (pv - v) * mu_v
    lr = lr + (plr - lr) * mu_lr
    wl, al, gl = lr[:, :C_LR_W], lr[:, C_LR_W:C_LR_W + C_LR_A], lr[:, C_LR_W + C_LR_A:]
    wraw = -_softplus(-(w0 + _dot(jnp.tanh(wl), w2))) - 0.5
    ld = -jnp.exp(wraw)
    a = jax.nn.sigmoid(a0 + _dot(al, a2))
    g = _dot(jax.nn.sigmoid(gl), g2)
    kk = k * k_k
    kk = kk * _head_bcast(_head_sum(kk * kk), lambda n2: 1.0 / jnp.maximum(jnp.sqrt(n2), L2_EPS))
    k2 = k * (1.0 + (a - 1.0) * k_a)
    bonus = _head_bcast(_head_sum(r * k2 * r_k)) * v
    return r, k2, v, kk, kk * a, ld, g, bonus


def _rwkv_out(y, bonus, g, lnx_w, lnx_b):
    n = float(C_HEAD)
    mean = _head_bcast(_head_sum(y), lambda s: s / n)
    yc = y - mean
    rstd = _head_bcast(_head_sum(yc * yc), lambda s: lax.rsqrt(s / n + GN_EPS))
    return (yc * rstd * lnx_w + lnx_b + bonus) * g


RW_GH = 4
RW_GW = RW_GH * C_HEAD


def _bd_mask():
    return _iota((RW_GW, RW_GW), 0) // C_HEAD == _iota((RW_GW, RW_GW), 1) // C_HEAD


def _bd_rows(a):
    return jnp.where(_bd_mask(), jnp.concatenate([a] * RW_GH, axis=0), 0.0)


def _bd_lanes(s):
    return jnp.where(_bd_mask(), jnp.concatenate([s] * RW_GH, axis=1), 0.0)


def _bd_fold(m):
    m = jnp.where(_bd_mask(), m, 0.0)
    out = m[:, :C_HEAD]
    for h in range(1, RW_GH):
        out = out + m[:, h * C_HEAD:(h + 1) * C_HEAD]
    return out


def _to_stack(a):
    return jnp.concatenate([a[:, h * C_HEAD:(h + 1) * C_HEAD] for h in range(RW_GH)], axis=0)


def _to_all(s):
    return jnp.concatenate([s[h * C_HEAD:(h + 1) * C_HEAD, :] for h in range(RW_GH)], axis=1)


def _rwkv_a_kernel(r_ref, k_ref, v_ref, lr_ref, rp_ref, kp_ref, vp_ref, lrp_ref, *rest):
    prm = tuple(x[...] for x in rest[:len(RW_NAMES)])
    (kd_ref, rd_ref, bg_ref, ge_ref, bonus_ref, gate_ref, aab_ref, rb_ref, m2_ref, rkv_ref, vk_ref,
     xs_scr, b1_scr, b2_scr, b3_scr, b4_scr) = rest[len(RW_NAMES):]
    c = pl.program_id(1)
    tb, cs = RWKV_BLOCK, RWKV_CHUNK
    first = (c == 0)

    def prev(cur_ref, prev_ref, width):
        xs_scr[8:8 + tb, 0:width] = cur_ref[...]
        xs_scr[7:8, 0:width] = jnp.where(first, 0.0, prev_ref[7:8, :])
        return xs_scr[pl.ds(7, tb), 0:width]

    pr = prev(r_ref, rp_ref, C_DIM)
    pk = prev(k_ref, kp_ref, C_DIM)
    pv = prev(v_ref, vp_ref, C_DIM)
    plr = prev(lr_ref, lrp_ref, C_LR)
    r, k2, v, kk, beta, ld, g, bonus = _rwkv_prep(
        r_ref[...], k_ref[...], v_ref[...], lr_ref[...], pr, pk, pv, plr, prm)
    bonus_ref[...] = bonus
    gate_ref[...] = g

    r_i, c_i = _iota((tb, tb), 0), _iota((tb, tb), 1)
    same = (r_i // cs == c_i // cs)
    lcum = _dot_hi((same & (c_i <= r_i)).astype(F32), ld)
    lend = _dot_hi(same.astype(F32), ld)
    einv = jnp.exp(-lcum)
    eend = jnp.exp(lend - lcum)
    ge_ref[...] = jnp.exp(lend)
    kd_ref[...] = kk * jnp.exp(lcum - ld)
    rd_ref[...] = r * jnp.exp(lcum)
    bg_ref[...] = beta * eend
    b1_scr[...] = beta * einv
    b2_scr[...] = k2 * einv
    b3_scr[...] = k2 * eend
    b4_scr[...] = v
    gw = RW_GW
    tpos, spos = _iota((gw, cs), 0) % cs, _iota((gw, cs), 1)
    strict, incl = tpos > spos, tpos >= spos
    hc = C_HEAD

    def chunk(j, carry):
        r0 = pl.multiple_of(j * cs, cs)
        q0 = pl.multiple_of(j * gw, gw)
        for g in range(C_HEADS // RW_GH):
            gs = slice(g * gw, (g + 1) * gw)
            ls = slice(g * hc, (g + 1) * hc)
            kd = kd_ref[pl.ds(r0, cs), gs]
            rd = rd_ref[pl.ds(r0, cs), gs]
            v = b4_scr[pl.ds(r0, cs), gs]
            lhs = jnp.concatenate([_bd_rows(kd), _bd_rows(rd)], axis=0)
            rhs = jnp.concatenate([b1_scr[pl.ds(r0, cs), gs], b2_scr[pl.ds(r0, cs), gs]], axis=0)
            gm = _dot_nt(lhs, rhs)
            aab_ref[pl.ds(q0, gw), ls] = jnp.where(strict, gm[:gw, :hc], 0.0)
            aak = jnp.where(strict, gm[:gw, hc:], 0.0)
            rb_ref[pl.ds(q0, gw), ls] = jnp.where(incl, gm[gw:, :hc], 0.0)
            rk = jnp.where(incl, gm[gw:, hc:], 0.0)
            mr = _dot(jnp.concatenate([_bd_lanes(aak), _bd_lanes(rk)], axis=0), _to_stack(v))
            m2_ref[pl.ds(q0, gw), ls] = mr[:gw]
            rkv_ref[pl.ds(q0, gw), ls] = mr[gw:]
            vk_ref[pl.ds(q0, gw), ls] = _bd_fold(_dot_tn(v, b3_scr[pl.ds(r0, cs), gs]))
        return carry

    lax.fori_loop(0, tb // cs, chunk, 0)


def _tri_solve_kernel(a_ref, t_ref, a_scr, t_scr):
    n, hf = RWKV_CHUNK, RWKV_CHUNK // 2
    npb = a_scr.shape[2]
    ng = a_scr.shape[1] // n

    def load(t, carry):
        a_scr[t] = a_ref[pl.ds(t, npb, stride=n), :].T
        return carry

    lax.fori_loop(0, n, load, 0)
    jrow = _iota((hf, npb), 0)
    zero = jnp.zeros((hf, npb), F32)

    def coef(t, g, s):
        return a_scr[t, pl.ds(g * n + s, 1), :]

    def row_lo(t, carry):
        def inner(s, accs):
            return tuple(acc - coef(t, g, s) * t_scr[s, g * n:g * n + hf, :] for g, acc in enumerate(accs))

        accs = lax.fori_loop(0, t, inner, ((jrow == t).astype(F32),) * ng)
        for g, acc in enumerate(accs):
            t_scr[t, g * n:g * n + hf, :] = acc
            t_scr[t, g * n + hf:(g + 1) * n, :] = zero
        return carry

    lax.fori_loop(0, hf, row_lo, 0)

    def row_hi(t, carry):
        def inner_lo(s, accs):
            return tuple(acc - coef(t, g, s) * t_scr[s, g * n:g * n + hf, :] for g, acc in enumerate(accs))

        def inner_hi(s, accs):
            lo, hi = accs[:ng], accs[ng:]
            lo = tuple(acc - coef(t, g, s) * t_scr[s, g * n:g * n + hf, :] for g, acc in enumerate(lo))
            hi = tuple(acc - coef(t, g, s) * t_scr[s, g * n + hf:(g + 1) * n, :] for g, acc in enumerate(hi))
            return lo + hi

        lo = lax.fori_loop(0, hf, inner_lo, (zero,) * ng)
        accs = lax.fori_loop(hf, t, inner_hi, lo + ((jrow + hf == t).astype(F32),) * ng)
        for g in range(ng):
            t_scr[t, g * n:g * n + hf, :] = accs[g]
            t_scr[t, g * n + hf:(g + 1) * n, :] = accs[ng + g]
        return carry

    lax.fori_loop(hf, n, row_hi, 0)

    def store(t, carry):
        t_ref[pl.ds(t, npb, stride=n), :] = t_scr[t].T
        return carry

    lax.fori_loop(0, n, store, 0)


def _tri_solve(a):
    rows, width = a.shape
    n = RWKV_CHUNK
    npb = min(LANES, rows // n)
    return pl.pallas_call(
        _tri_solve_kernel,
        grid=(rows // (npb * n),),
        in_specs=[pl.BlockSpec((npb * n, width), lambda i: (i, 0))],
        out_specs=pl.BlockSpec((npb * n, width), lambda i: (i, 0)),
        out_shape=jax.ShapeDtypeStruct(a.shape, F32),
        scratch_shapes=[pltpu.VMEM((n, width, npb), F32)] * 2,
        compiler_params=_cparams(("parallel",)),
        name="tri_solve",
    )(a)


def _rwkv_c_kernel(t_ref, kd_ref, rd_ref, bg_ref, ge_ref, bonus_ref, gate_ref, rb_ref, m2_ref, rkv_ref, vk_ref,
                   lnw, lnb, y_ref, st_ref, s_scr, y_scr):
    c = pl.program_id(1)
    tb, cs, gw, hc = RWKV_BLOCK, RWKV_CHUNK, RW_GW, C_HEAD

    @pl.when(c == 0)
    def _():
        s_scr[...] = jnp.zeros_like(s_scr)

    def chunk(j, carry):
        r0 = pl.multiple_of(j * cs, cs)
        q0 = pl.multiple_of(j * gw, gw)
        for g in range(C_HEADS // RW_GH):
            gs = slice(g * gw, (g + 1) * gw)
            ls = slice(g * hc, (g + 1) * hc)
            rhs = jnp.concatenate([_to_stack(kd_ref[pl.ds(r0, cs), gs]), m2_ref[pl.ds(q0, gw), ls]], axis=1)
            wu = _dot(_bd_lanes(t_ref[pl.ds(q0, gw), ls]), rhs)
            rbwu = _dot(_bd_lanes(rb_ref[pl.ds(q0, gw), ls]), wu)
            rt = _to_stack(rd_ref[pl.ds(r0, cs), gs]) - rbwu[:, :hc]
            yc = rkv_ref[pl.ds(q0, gw), ls] - rbwu[:, hc:]
            wu_bd = jnp.concatenate([_bd_lanes(wu[:, :hc]), _bd_lanes(wu[:, hc:])], axis=1)
            nu = _dot_tn(wu_bd, _to_stack(bg_ref[pl.ds(r0, cs), gs]))
            ge_row = ge_ref[pl.ds(r0, 1), gs]
            ge_st = jnp.concatenate(
                [jnp.broadcast_to(ge_row[:, h * hc:(h + 1) * hc], (hc, hc)) for h in range(RW_GH)], axis=0)
            s = s_scr[g]
            s_bd = _bd_lanes(s)
            y_scr[pl.ds(r0, cs), gs] = _dot_nt(_to_all(rt), s_bd) + _to_all(yc)
            s_scr[g] = s * ge_st - _dot(s_bd, nu[:gw]) + (vk_ref[pl.ds(q0, gw), ls] - nu[gw:])
        return carry

    lax.fori_loop(0, tb // cs, chunk, 0)
    y_ref[...] = _rwkv_out(y_scr[...], bonus_ref[...], gate_ref[...], lnw[...], lnb[...]).astype(BF16)

    @pl.when(c == pl.num_programs(1) - 1)
    def _():
        st_ref[...] = s_scr[...]


def _rwkv_prompt(proj, rp):
    bsz, seq, _ = proj.shape
    tb, cs = RWKV_BLOCK, RWKV_CHUNK
    nblk = seq // tb

    def col(width, off):
        return pl.BlockSpec((None, tb, width), lambda b, c: (b, c, off // width))

    def pcol(width, off):
        return pl.BlockSpec((None, 8, width), lambda b, c: (b, jnp.maximum(c * (tb // 8) - 1, 0), off // width))

    ng = C_HEADS // RW_GH
    srows = tb // cs * RW_GW
    tok = pl.BlockSpec((None, tb, C_DIM), lambda b, c: (b, c, 0))
    stk = pl.BlockSpec((None, srows, ng * C_HEAD), lambda b, c: (b, c, 0))
    tok_shape = jax.ShapeDtypeStruct((bsz, seq, C_DIM), F32)
    stk_shape = jax.ShapeDtypeStruct((bsz, nblk * srows, ng * C_HEAD), F32)
    prm = [rp[n] for n in RW_NAMES]
    outs = pl.pallas_call(
        _rwkv_a_kernel,
        grid=(bsz, nblk),
        in_specs=[col(C_DIM, P_R), col(C_DIM, P_K), col(C_DIM, P_V), col(C_LR, P_LR),
                  pcol(C_DIM, P_R), pcol(C_DIM, P_K), pcol(C_DIM, P_V), pcol(C_LR, P_LR)]
                 + [_full(x.shape) for x in prm],
        out_specs=[tok] * 6 + [stk] * 5,
        out_shape=[tok_shape] * 6 + [stk_shape] * 5,
        scratch_shapes=[pltpu.VMEM((tb + 8, C_DIM), F32)] + [pltpu.VMEM((tb, C_DIM), F32)] * 4,
        compiler_params=_cparams(("parallel", "parallel")),
        name="rwkv_prep",
    )(proj, proj, proj, proj, proj, proj, proj, proj, *prm)
    kd, rd, bg, ge, bonus, gate, aab, rb, m2, rkv, vk = outs

    tmat = _tri_solve(aab.reshape(-1, ng * C_HEAD)).reshape(stk_shape.shape)

    y, st = pl.pallas_call(
        _rwkv_c_kernel,
        grid=(bsz, nblk),
        in_specs=[stk] + [tok] * 6 + [stk] * 4 + [_full((1, C_DIM))] * 2,
        out_specs=[tok, pl.BlockSpec((None, ng, RW_GW, C_HEAD), lambda b, c: (b, 0, 0, 0))],
        out_shape=[jax.ShapeDtypeStruct((bsz, seq, C_DIM), BF16),
                   jax.ShapeDtypeStruct((bsz, ng, RW_GW, C_HEAD), F32)],
        scratch_shapes=[pltpu.VMEM((ng, RW_GW, C_HEAD), F32), pltpu.VMEM((tb, C_DIM), F32)],
        compiler_params=_cparams(("parallel", "arbitrary")),
        name="rwkv_scan",
    )(tmat, kd, rd, bg, ge, bonus, gate, rb, m2, rkv, vk, rp["lnx_w"], rp["lnx_b"])
    return y, st.reshape(bsz, C_HEADS, C_HEAD, C_HEAD)


def _rwkv_step_kernel(r_ref, k_ref, v_ref, lr_ref, sh_ref, *rest):
    prm = tuple(x[...] for x in rest[:len(RW_NAMES)])
    lnw, lnb, s_ref, y_ref, so_ref = rest[len(RW_NAMES):]
    bb = STEP_BB
    sh = sh_ref[...]
    r, k2, v, kk, beta, ld, g, bonus = _rwkv_prep(
        r_ref[...], k_ref[...], v_ref[...], lr_ref[...],
        sh[:, :C_DIM], sh[:, C_DIM:2 * C_DIM], sh[:, 2 * C_DIM:3 * C_DIM], sh[:, 3 * C_DIM:], prm)
    w = jnp.exp(ld)
    v_t = v.T
    lane = _iota((C_DIM, LANES), 1)
    heads = [[x[:, h * C_HEAD:(h + 1) * C_HEAD] for x in (kk, w, beta, k2, r)] for h in range(C_HEADS)]

    def rows(i, b):
        return jnp.concatenate(
            [jnp.broadcast_to(heads[h][i][b:b + 1, :], (C_HEAD, C_HEAD)) for h in range(C_HEADS)], axis=0)

    y_t = jnp.zeros((C_DIM, LANES), F32)
    for b in range(bb):
        s = s_ref[b]
        sa = -jnp.sum(s * rows(0, b), axis=-1, keepdims=True)
        sn = s * rows(1, b) + sa * rows(2, b) + v_t[:, b:b + 1] * rows(3, b)
        so_ref[b] = sn
        y_t = jnp.where(lane == b, jnp.sum(sn * rows(4, b), axis=-1, keepdims=True), y_t)
    y = y_t.T[0:bb, :]
    y_ref[...] = _rwkv_out(y, bonus, g, lnw[...], lnb[...]).astype(BF16)


def _rwkv_step(proj, shift, s_all, layer, rp):
    bsz = proj.shape[0]
    bb = STEP_BB
    sflat = s_all.reshape(s_all.shape[0], bsz, C_DIM, C_HEAD)

    def col(width, off):
        return pl.BlockSpec((bb, width), lambda i: (i, off // width))

    prm = [rp[n] for n in RW_NAMES]
    y, sn = pl.pallas_call(
        _rwkv_step_kernel,
        grid=(bsz // bb,),
        in_specs=[col(C_DIM, P_R), col(C_DIM, P_K), col(C_DIM, P_V), col(C_LR, P_LR),
                  pl.BlockSpec((bb, C_COLS), lambda i: (i, 0))]
                 + [_full(x.shape) for x in prm] + [_full((1, C_DIM))] * 2
                 + [pl.BlockSpec((None, bb, C_DIM, C_HEAD), lambda i: (layer, i, 0, 0))],
        out_specs=[pl.BlockSpec((bb, C_DIM), lambda i: (i, 0)),
                   pl.BlockSpec((bb, C_DIM, C_HEAD), lambda i: (i, 0, 0))],
        out_shape=[jax.ShapeDtypeStruct((bsz, C_DIM), BF16),
                   jax.ShapeDtypeStruct((bsz, C_DIM, C_HEAD), F32)],
        compiler_params=_cparams(("parallel",)),
        name="rwkv_step",
    )(proj, proj, proj, proj, shift, *prm, rp["lnx_w"], rp["lnx_b"], sflat)
    return y, sn.reshape(s_all.shape[1:])


def _layout_w_in_kernel(w_ref, o_ref):
    o = A_DIM + CONV_CH
    rows = w_ref.shape[0]
    o_ref[:, :o] = w_ref[:, :o].astype(BF16)
    o_ref[:, o:P_DT] = w_ref[:, o + A_HEADS:].astype(BF16)
    dt_tile = jnp.where(_iota((rows, LANES), 1) < A_HEADS, w_ref[:, o:o + LANES], 0.0)
    o_ref[:, P_DT:P_DT + LANES] = dt_tile.astype(BF16)
    o_ref[:, P_DT + LANES:] = jnp.zeros((rows, P_COLS - P_DT - LANES), BF16)


def _layout_w_in(w_all, layer):
    k = w_all.shape[1]
    tr = 256
    return pl.pallas_call(
        _layout_w_in_kernel,
        grid=(k // tr,),
        in_specs=[pl.BlockSpec((None, tr, IN_COLS), lambda i: (layer, i, 0))],
        out_specs=pl.BlockSpec((tr, P_COLS), lambda i: (i, 0)),
        out_shape=jax.ShapeDtypeStruct((k, P_COLS), BF16),
        compiler_params=_cparams(("parallel",)),
        name="layout_w_in",
    )(w_all)


def _xbc_cols(proj):
    return proj[..., P_X:P_X + CONV_CH]


def _moe_route(logits, n_tok):
    top_logit, top_idx = lax.top_k(logits, TOP_K)
    gate = jax.nn.softmax(top_logit, axis=-1)
    n_assign = n_tok * TOP_K
    flat_e = top_idx.reshape(-1).astype(jnp.int32)
    order = jnp.argsort(flat_e)
    onehot = jax.nn.one_hot(flat_e, N_EXPERTS, dtype=jnp.int32)
    counts = jnp.sum(onehot, axis=0)
    padded = (counts + MOE_BM - 1) // MOE_BM * MOE_BM
    pad_end = jnp.cumsum(padded)
    pad_start = pad_end - padded
    start = jnp.cumsum(counts) - counts
    rank = jnp.sum((jnp.cumsum(onehot, axis=0) - onehot) * onehot, axis=1)
    slot_of = (pad_start[flat_e] + rank).reshape(n_tok, TOP_K)
    n_blocks = -(-n_assign // MOE_BM) + N_EXPERTS
    n_slots = n_blocks * MOE_BM
    block_start = jnp.arange(n_blocks, dtype=jnp.int32) * MOE_BM
    nvalid = (pad_end[-1] // MOE_BM).astype(jnp.int32)
    bexp = jnp.minimum(jnp.searchsorted(pad_end, block_start, side="right"), N_EXPERTS - 1).astype(jnp.int32)
    slot_e = jnp.repeat(bexp, MOE_BM)
    within = jnp.arange(n_slots, dtype=jnp.int32) - pad_start[slot_e]
    src = order[jnp.clip(start[slot_e] + within, 0, n_assign - 1)] // TOP_K
    slot_tok = jnp.where(within < counts[slot_e], src, n_tok)
    left = counts[bexp] - (block_start - pad_start[bexp])
    nsub = jnp.clip((left + MOE_SUB - 1) // MOE_SUB, 0, MOE_BM // MOE_SUB).astype(jnp.int32)
    last = bexp[jnp.maximum(nvalid - 1, 0)]
    bexp = jnp.where(jnp.arange(n_blocks) < nvalid, bexp, last)
    first = jnp.concatenate([jnp.ones((1,), jnp.int32), (bexp[1:] != bexp[:-1]).astype(jnp.int32)])
    return gate, slot_tok, slot_of, bexp, first, nsub


def _moe(xp, xs, h, logits, wg, wu, wd, g, final):
    tp = xp.shape[0]
    n_tok = h.shape[0]
    gate, slot_tok, slot_of, bexp, first, nsub = _moe_route(logits[:, :N_EXPERTS], n_tok)
    xb = h[jnp.where(slot_tok == n_tok, 0, slot_tok)]
    yb = _moe_experts(xb, bexp, first, nsub, wg, wu, wd)
    return (_moe_combine(xp, yb, slot_of[:tp], gate[:tp], g, final),
            _moe_combine(xs, yb, slot_of[tp:], gate[tp:], g, final))


def kernel(x_prompt, x_sample, state_ssm, state_conv, state_hgrn, state_rwkv, state_shift, norm1_g, w_in, conv_w, conv_b, dt_bias, a_log, d_skip, ssm_norm_g, lb_logits, hgrn_norm_g, shift_mu, w0, w2, a0, a2, g2, k_k, k_a, r_k, lnx_w, lnx_b, w_out, norm2_g, ffn_w_gate, ffn_w_up, ffn_w_down, router_w, exp_w_gate, exp_w_up, exp_w_down, final_norm_g):
    depth = w_in.shape[0]
    bp, seq, _ = x_prompt.shape
    bs = x_sample.shape[0]
    tp = bp * seq
    lb_soft = jax.nn.softmax(lb_logits.astype(F32), axis=0)
    lower_bounds = jnp.clip(jnp.cumsum(lb_soft, axis=0) - lb_soft[0:1], 0.0, 1.0)

    xp = x_prompt.reshape(tp, D_MODEL)
    xs = x_sample.reshape(bs, D_MODEL)
    outs = {k: [] for k in ("p_ssm", "p_conv", "p_hgrn", "p_rwkv", "p_shift",
                            "s_ssm", "s_conv", "s_hgrn", "s_rwkv", "s_shift")}
    normed = False
    for layer in range(depth):
        p = dict(conv_w=conv_w[layer], conv_b=conv_b[layer], dt_bias=dt_bias[layer], a_log=a_log[layer],
                 d_skip=d_skip[layer], ssm_norm_g=ssm_norm_g[layer], shift_mu=shift_mu[layer],
                 w0=w0[layer], w2=w2[layer], a0=a0[layer], a2=a2[layer], g2=g2[layer], k_k=k_k[layer],
                 k_a=k_a[layer], r_k=r_k[layer].reshape(-1), lnx_w=lnx_w[layer], lnx_b=lnx_b[layer])
        sp = _ssd_params(p)
        rp = _rwkv_params(p)
        lb = lower_bounds[layer]
        lbp = jnp.stack([1.0 - lb, jnp.log(lb), jnp.log1p(-lb)])
        hng = hgrn_norm_g[layer].reshape(1, B_HEAD)
        w_in_l = _layout_w_in(w_in, layer)
        w_out_l = w_out[layer].astype(BF16)
        j = layer // 2
        routed = layer % 2 == 1
        if routed:
            rw_f = jnp.pad(router_w[j], ((0, 0), (0, LANES - N_EXPERTS)))
            rw_hi = rw_f.astype(BF16)
            rw = jnp.concatenate([rw_hi, (rw_f - rw_hi.astype(F32)).astype(BF16)], axis=1)
            h_all = jnp.zeros((tp + bs, D_MODEL), BF16)
            l_all = jnp.zeros((tp + bs, LANES), F32)

        proj = _norm_matmul(xp, norm1_g[layer], w_in_l, 1024, 512).reshape(bp, seq, P_COLS)
        ya, ssm_p = _ssd_prompt(proj, sp)
        yb, hgrn_p = _hgrn_prompt(proj, lbp, hng)
        yc, rwkv_p = _rwkv_prompt(proj, rp)
        xp, hp, l_all = _out_proj(ya.reshape(tp, A_DIM), yb.reshape(tp, B_DIM), yc.reshape(tp, C_DIM),
                                  w_out_l, xp, norm2_g[layer], 256, (rw, h_all, l_all, 0) if routed else None)
        outs["p_ssm"].append(ssm_p)
        outs["p_conv"].append(_xbc_cols(proj[:, seq - (CONV_W - 1):, :]))
        outs["p_hgrn"].append(hgrn_p)
        outs["p_rwkv"].append(rwkv_p)
        outs["p_shift"].append(proj[:, seq - 1, P_R:P_R + C_COLS])

        projs = _norm_matmul(xs, norm1_g[layer], w_in_l, 128, 512)
        ya, ssm_s = _ssd_step(projs, state_conv[layer], state_ssm, layer, p, sp)
        yb, hgrn_s = _hgrn_step(projs, state_hgrn, layer, lbp, hng)
        yc, rwkv_s = _rwkv_step(projs, state_shift[layer], state_rwkv, layer, rp)
        xs, hs, l_all = _out_proj(ya, yb, yc, w_out_l, xs, norm2_g[layer], 128,
                                  (rw, hp, l_all, tp) if routed else None)
        outs["s_ssm"].append(ssm_s)
        outs["s_conv"].append(jnp.concatenate([state_conv[layer][:, 1:, :], _xbc_cols(projs)[:, None, :]], axis=1))
        outs["s_hgrn"].append(hgrn_s)
        outs["s_rwkv"].append(rwkv_s)
        outs["s_shift"].append(projs[:, P_R:P_R + C_COLS])

        if not routed:
            xp = _ffn(xp, hp, ffn_w_gate[j], ffn_w_up[j], ffn_w_down[j], 512)
            xs = _ffn(xs, hs, ffn_w_gate[j], ffn_w_up[j], ffn_w_down[j], 128)
        else:
            normed = layer == depth - 1
            xp, xs = _moe(xp, xs, hs, l_all, exp_w_gate[j], exp_w_up[j], exp_w_down[j], final_norm_g, normed)

    if not normed:
        xp, xs = _final_norm(xp, final_norm_g, 512), _final_norm(xs, final_norm_g, 128)
    y_prompt = xp.reshape(bp, seq, D_MODEL)
    y_sample = xs.reshape(bs, 1, D_MODEL)
    st = {k: jnp.stack(v) for k, v in outs.items()}
    return (y_prompt, y_sample, st["p_ssm"], st["p_conv"], st["p_hgrn"], st["p_rwkv"], st["p_shift"],
            st["s_ssm"], st["s_conv"], st["s_hgrn"], st["s_rwkv"], st["s_shift"])
```

```python
import functools
import math

import jax
import jax.numpy as jnp
from jax import lax
from jax.experimental import pallas as pl
from jax.experimental.pallas import tpu as pltpu

F32 = jnp.float32
BF16 = jnp.bfloat16
HI = lax.Precision.HIGHEST

D_MODEL = 2048
A_DIM, A_HEAD, A_HEADS, A_GROUPS, A_STATE = 1024, 64, 16, 2, 128
A_GN = A_GROUPS * A_STATE
CONV_W = 4
CONV_CH = A_DIM + 2 * A_GN
B_DIM, B_HEAD, B_HEADS = 512, 128, 4
C_DIM, C_HEAD, C_HEADS = 512, 64, 8
C_LR_W, C_LR_A, C_LR_G = 64, 64, 128
C_LR = C_LR_W + C_LR_A + C_LR_G
C_COLS = 3 * C_DIM + C_LR
IN_COLS = A_DIM + CONV_CH + A_HEADS + 4 * B_DIM + C_COLS
D_FF, N_EXPERTS, TOP_K, E_FF = 5504, 8, 2, 7168
NORM_EPS, GN_EPS, L2_EPS = 1e-6, 64e-5, 1e-12

P_Z, P_X, P_B, P_C = 0, 1024, 2048, 2304
P_Q, P_F, P_I, P_G = 2560, 3072, 3584, 4096
P_R, P_K, P_V, P_LR, P_DT, P_COLS = 4608, 5120, 5632, 6144, 6400, 6656
LANES = 128
DT_PAD = P_COLS - P_DT - A_HEADS

SSD_CHUNK = 256
HGRN_BLOCK, HGRN_CHUNK = 256, 16
RWKV_BLOCK, RWKV_CHUNK = 256, 64
STEP_BB = 8

FFN_TF, FFN_TN, FFN_TM_UP, FFN_TM_DOWN = 512, 1024, 1024, 256
MOE_BM, MOE_SUB, MOE_TF, MOE_TN = 512, 256, 1024, 512

NT_DIMS = (((1,), (1,)), ((), ()))
TN_DIMS = (((0,), (0,)), ((), ()))


def _cparams(sem, vmem_mb=48):
    return pltpu.CompilerParams(dimension_semantics=sem, vmem_limit_bytes=vmem_mb * 1024 * 1024)


def _dot(a, b):
    return jnp.dot(a.astype(BF16), b.astype(BF16), preferred_element_type=F32)


def _dot_nt(a, b):
    return lax.dot_general(a.astype(BF16), b.astype(BF16), NT_DIMS, preferred_element_type=F32)


def _dot_tn(a, b):
    return lax.dot_general(a.astype(BF16), b.astype(BF16), TN_DIMS, preferred_element_type=F32)


def _dot_hi(a, b):
    return jnp.dot(a, b, precision=HI, preferred_element_type=F32)


def _dot_nt_hi(a, b):
    return lax.dot_general(a, b, NT_DIMS, precision=HI, preferred_element_type=F32)


def _dot_tn_hi(a, b):
    return lax.dot_general(a, b, TN_DIMS, precision=HI, preferred_element_type=F32)


def _silu(x):
    return x * jax.nn.sigmoid(x)


def _softplus(x):
    return jnp.maximum(x, 0.0) + jnp.log1p(jnp.exp(-jnp.abs(x)))


def _iota(shape, dim):
    return lax.broadcasted_iota(jnp.int32, shape, dim)


def _rms(x, g):
    return x * lax.rsqrt(jnp.mean(x * x, axis=-1, keepdims=True) + NORM_EPS) * g


def _full(shape):
    nd = len(shape)
    return pl.BlockSpec(shape, lambda *_: (0,) * nd)


def _norm_matmul_kernel(x_ref, g_ref, w_ref, o_ref):
    h = _rms(x_ref[...], g_ref[...]).astype(BF16)
    o_ref[...] = jnp.dot(h, w_ref[...], preferred_element_type=F32)


def _norm_matmul(x, g, w, tm):
    m, k = x.shape
    n = w.shape[1]
    return pl.pallas_call(
        _norm_matmul_kernel,
        grid=(m // tm,),
        in_specs=[pl.BlockSpec((tm, k), lambda i: (i, 0)),
                  pl.BlockSpec((1, k), lambda i: (0, 0)),
                  pl.BlockSpec((k, n), lambda i: (0, 0), pipeline_mode=pl.Buffered(1))],
        out_specs=pl.BlockSpec((tm, n), lambda i: (i, 0)),
        out_shape=jax.ShapeDtypeStruct((m, n), F32),
        compiler_params=_cparams(("parallel",), 56),
        name="norm_matmul",
    )(x, g.reshape(1, k), w)


def _out_proj_kernel(routed, ya_ref, yb_ref, yc_ref, w_ref, r_ref, g_ref, *rest):
    x = (r_ref[...]
         + jnp.dot(ya_ref[...], w_ref[0:A_DIM, :], preferred_element_type=F32)
         + jnp.dot(yb_ref[...], w_ref[A_DIM:A_DIM + B_DIM, :], preferred_element_type=F32)
         + jnp.dot(yc_ref[...], w_ref[A_DIM + B_DIM:, :], preferred_element_type=F32))
    h = _rms(x, g_ref[...])
    h_hi = h.astype(BF16)
    if routed:
        rw_ref, _, _, x_ref, h_ref, l_ref = rest
        h_lo = (h - h_hi.astype(F32)).astype(BF16)
        rw = rw_ref[...]
        l2 = jnp.dot(h_hi, rw, preferred_element_type=F32)
        l_ref[...] = (l2[:, :LANES] + l2[:, LANES:]
                      + jnp.dot(h_lo, rw[:, :LANES], preferred_element_type=F32))
    else:
        x_ref, h_ref = rest
    x_ref[...] = x
    h_ref[...] = h_hi


def _out_proj(ya, yb, yc, w, res, g, tm, routed=None):
    m = ya.shape[0]
    n = w.shape[1]
    row = lambda width: pl.BlockSpec((tm, width), lambda i: (i, 0))
    in_specs = [row(A_DIM), row(B_DIM), row(C_DIM), _full(w.shape), row(n), _full((1, n))]
    args = [ya, yb, yc, w, res, g.reshape(1, n)]
    aliases = {}
    if routed is None:
        out_specs = [row(n), row(n)]
        out_shape = [jax.ShapeDtypeStruct((m, n), F32), jax.ShapeDtypeStruct((m, n), BF16)]
    else:
        rw, h_buf, l_buf, row0 = routed
        blk0 = row0 // tm
        off = lambda width: pl.BlockSpec((tm, width), lambda i: (i + blk0, 0))
        in_specs += [_full(rw.shape), pl.BlockSpec(memory_space=pl.ANY), pl.BlockSpec(memory_space=pl.ANY)]
        args += [rw, h_buf, l_buf]
        out_specs = [row(n), off(n), off(LANES)]
        out_shape = [jax.ShapeDtypeStruct((m, n), F32), jax.ShapeDtypeStruct(h_buf.shape, BF16),
                     jax.ShapeDtypeStruct(l_buf.shape, F32)]
        aliases = {len(args) - 2: 1, len(args) - 1: 2}
    outs = pl.pallas_call(
        functools.partial(_out_proj_kernel, routed is not None),
        grid=(m // tm,),
        in_specs=in_specs, out_specs=out_specs, out_shape=out_shape,
        input_output_aliases=aliases,
        compiler_params=_cparams(("parallel",)),
        name="out_proj",
    )(*args)
    return outs if routed is not None else (*outs, None)


def _ffn_up_kernel(h_ref, wg_ref, wu_ref, o_ref, wg_s, wu_s):
    @pl.when(pl.program_id(1) == 0)
    def _():
        wg_s[...] = wg_ref[...].astype(BF16)
        wu_s[...] = wu_ref[...].astype(BF16)

    h = h_ref[...]
    a = jnp.dot(h, wg_s[...], preferred_element_type=F32)
    u = jnp.dot(h, wu_s[...], preferred_element_type=F32)
    o_ref[...] = (_silu(a) * u).astype(BF16)


def _ffn_down_kernel(h_ref, wd_ref, x_ref, o_ref, wd_s):
    @pl.when(pl.program_id(1) == 0)
    def _():
        wd_s[...] = wd_ref[...].astype(BF16)

    o_ref[...] = x_ref[...] + jnp.dot(h_ref[...], wd_s[...], preferred_element_type=F32)


def _ffn(x, h, wg, wu, wd):
    m, k = h.shape
    ff = wg.shape[1]
    tf, tn = FFN_TF, FFN_TN
    tu = min(m, FFN_TM_UP)
    tm = min(m, FFN_TM_DOWN)
    act = pl.pallas_call(
        _ffn_up_kernel,
        grid=(pl.cdiv(ff, tf), m // tu),
        in_specs=[pl.BlockSpec((tu, k), lambda f, i: (i, 0)),
                  pl.BlockSpec((k, tf), lambda f, i: (0, f)),
                  pl.BlockSpec((k, tf), lambda f, i: (0, f))],
        out_specs=pl.BlockSpec((tu, tf), lambda f, i: (i, f)),
        out_shape=jax.ShapeDtypeStruct((m, ff), BF16),
        scratch_shapes=[pltpu.VMEM((k, tf), BF16)] * 2,
        compiler_params=_cparams(("arbitrary", "arbitrary")),
        name="ffn_up",
    )(h, wg, wu)
    return pl.pallas_call(
        _ffn_down_kernel,
        grid=(k // tn, m // tm),
        in_specs=[pl.BlockSpec((tm, ff), lambda n, i: (i, 0)),
                  pl.BlockSpec((ff, tn), lambda n, i: (0, n), pipeline_mode=pl.Buffered(1)),
                  pl.BlockSpec((tm, tn), lambda n, i: (i, n))],
        out_specs=pl.BlockSpec((tm, tn), lambda n, i: (i, n)),
        out_shape=jax.ShapeDtypeStruct((m, k), F32),
        scratch_shapes=[pltpu.VMEM((ff, tn), BF16)],
        compiler_params=_cparams(("arbitrary", "arbitrary"), 56),
        name="ffn_down",
    )(act, wd, x)


def _moe_rows(nsub, o_ref, compute):
    bm = o_ref.shape[0]
    for ns in range(bm // MOE_SUB + 1):
        @pl.when(nsub == ns)
        def _(rows=ns * MOE_SUB):
            if rows:
                o_ref[0:rows, :] = compute(rows).astype(o_ref.dtype)
            if rows < bm:
                o_ref[rows:, :] = jnp.zeros((bm - rows, o_ref.shape[1]), o_ref.dtype)


def _weight_stream(sched, w_hbms, wbuf, sem, dsts):
    bexp_ref, first_ref, run_ref, nxt_ref, lastrun_ref, nruns_ref = sched
    col, b = pl.program_id(0), pl.program_id(1)
    ncol = pl.num_programs(0)
    width = wbuf.shape[-1]

    def copies(e, c, slot):
        cols = pl.ds(pl.multiple_of(c * width, width), width)
        return [pltpu.make_async_copy(w.at[e, :, cols], wbuf.at[slot, i], sem.at[slot])
                for i, w in enumerate(w_hbms)]

    @pl.when(first_ref[b] == 1)
    def _():
        slot = lax.rem(col * nruns_ref[0] + run_ref[b], 2)
        last = lastrun_ref[b]

        @pl.when((col == 0) & (b == 0))
        def _():
            for cp in copies(bexp_ref[0], 0, 0):
                cp.start()

        @pl.when((last == 0) | (col < ncol - 1))
        def _():
            for cp in copies(nxt_ref[b], col + last, 1 - slot):
                cp.start()

        for cp in copies(bexp_ref[b], col, slot):
            cp.wait()
        for i, dst in enumerate(dsts):
            dst[...] = wbuf[slot, i].astype(BF16)


def _moe_up_kernel(*refs):
    sched, (nsub_ref, _, x_ref, wg_hbm, wu_hbm, o_ref, wbuf, sem, wg_s, wu_s) = refs[:6], refs[6:]
    _weight_stream(sched, (wg_hbm, wu_hbm), wbuf, sem, (wg_s, wu_s))

    def compute(rows):
        x = x_ref[0:rows, :]
        a = jnp.dot(x, wg_s[...], preferred_element_type=F32)
        u = jnp.dot(x, wu_s[...], preferred_element_type=F32)
        return _silu(a) * u

    _moe_rows(nsub_ref[pl.program_id(1)], o_ref, compute)


def _moe_down_kernel(*refs):
    sched, (nsub_ref, _, h_ref, wd_hbm, o_ref, wbuf, sem, wd_s) = refs[:6], refs[6:]
    _weight_stream(sched, (wd_hbm,), wbuf, sem, (wd_s,))
    _moe_rows(nsub_ref[pl.program_id(1)], o_ref,
              lambda rows: jnp.dot(h_ref[0:rows, :], wd_s[...], preferred_element_type=F32))


def _run_schedule(bexp, first):
    nb = bexp.shape[0]
    ids = jnp.arange(nb, dtype=jnp.int32)
    run = jnp.cumsum(first).astype(jnp.int32) - 1
    nruns = run[-1] + 1
    starts = jnp.where(first == 1, ids, nb)
    nxt_start = jnp.concatenate([lax.cummin(starts[::-1])[::-1][1:], jnp.full((1,), nb, jnp.int32)])
    nxt = jnp.where(nxt_start < nb, bexp[jnp.minimum(nxt_start, nb - 1)], bexp[0]).astype(jnp.int32)
    lastrun = (run == nruns - 1).astype(jnp.int32)
    return bexp, first, run, nxt, lastrun, nruns.reshape(1)


def _moe_experts(xb, bexp, first, nsub, wg, wu, wd):
    n_slots, k = xb.shape
    ff = wg.shape[2]
    nb = n_slots // MOE_BM
    tf, tn = MOE_TF, MOE_TN
    hbm = pl.BlockSpec(memory_space=pl.ANY)

    def src_blocks(ns):
        return lax.cummax(jnp.where(ns > 0, jnp.arange(ns.shape[0], dtype=jnp.int32), 0))

    act = pl.pallas_call(
        _moe_up_kernel,
        grid_spec=pltpu.PrefetchScalarGridSpec(
            num_scalar_prefetch=8,
            grid=(ff // tf, nb),
            in_specs=[pl.BlockSpec((MOE_BM, k), lambda f, b, *s: (s[7][b], 0)), hbm, hbm],
            out_specs=pl.BlockSpec((MOE_BM, tf), lambda f, b, *s: (b, f)),
            scratch_shapes=[pltpu.VMEM((2, 2, k, tf), F32), pltpu.SemaphoreType.DMA((2,)),
                            pltpu.VMEM((k, tf), BF16), pltpu.VMEM((k, tf), BF16)],
        ),
        out_shape=jax.ShapeDtypeStruct((n_slots, ff), BF16),
        compiler_params=_cparams(("arbitrary", "arbitrary"), 56),
        name="moe_up",
    )(*_run_schedule(bexp, first), nsub, src_blocks(nsub), xb, wg, wu)
    per = MOE_BM // MOE_SUB
    bexp_d = jnp.repeat(bexp, per)
    first_d = jnp.concatenate([jnp.ones((1,), jnp.int32), (bexp_d[1:] != bexp_d[:-1]).astype(jnp.int32)])
    nsub_d = jnp.clip(jnp.repeat(nsub, per) - jnp.tile(jnp.arange(per, dtype=jnp.int32), nb), 0, 1)
    return pl.pallas_call(
        _moe_down_kernel,
        grid_spec=pltpu.PrefetchScalarGridSpec(
            num_scalar_prefetch=8,
            grid=(k // tn, nb * per),
            in_specs=[pl.BlockSpec((MOE_SUB, ff), lambda n, b, *s: (s[7][b], 0)), hbm],
            out_specs=pl.BlockSpec((MOE_SUB, tn), lambda n, b, *s: (b, n)),
            scratch_shapes=[pltpu.VMEM((2, 1, ff, tn), F32), pltpu.SemaphoreType.DMA((2,)),
                            pltpu.VMEM((ff, tn), BF16)],
        ),
        out_shape=jax.ShapeDtypeStruct((n_slots, k), F32),
        compiler_params=_cparams(("arbitrary", "arbitrary"), 56),
        name="moe_down",
    )(*_run_schedule(bexp_d, first_d), nsub_d, src_blocks(nsub_d), act, wd)


def _final_norm_kernel(x_ref, g_ref, o_ref):
    o_ref[...] = _rms(x_ref[...], g_ref[...])


def _final_norm(x, g, tm):
    m, k = x.shape
    return pl.pallas_call(
        _final_norm_kernel,
        grid=(m // tm,),
        in_specs=[pl.BlockSpec((tm, k), lambda i: (i, 0)),
                  pl.BlockSpec((1, k), lambda i: (0, 0))],
        out_specs=pl.BlockSpec((tm, k), lambda i: (i, 0)),
        out_shape=jax.ShapeDtypeStruct((m, k), F32),
        compiler_params=_cparams(("parallel",)),
        name="final_norm",
    )(x, g.reshape(1, k))


def _row_copy(src_hbm, row, dst, r, sem):
    return pltpu.make_async_copy(src_hbm.at[pl.ds(row, 1), :], dst.at[pl.ds(r, 1), :], sem)


def _moe_combine_kernel(final, s0_ref, s1_ref, x_ref, gate_ref, g_ref, yb_hbm, o_ref, buf, sem):
    rows = x_ref.shape[0]
    base = pl.program_id(0) * rows

    def issue(r, carry):
        _row_copy(yb_hbm, s0_ref[base + r], buf.at[0], r, sem).start()
        _row_copy(yb_hbm, s1_ref[base + r], buf.at[1], r, sem).start()
        return carry

    lax.fori_loop(0, rows, issue, 0)
    pltpu.make_async_copy(buf, buf, sem).wait()
    gate = gate_ref[...]
    y = x_ref[...] + gate[:, 0:1] * buf[0] + gate[:, 1:2] * buf[1]
    o_ref[...] = _rms(y, g_ref[...]) if final else y


def _moe_combine(x, yb, slot_of, gate, g, final):
    m, k = x.shape
    rows = 128
    gate_p = jnp.pad(gate, ((0, 0), (0, LANES - TOP_K)))
    return pl.pallas_call(
        functools.partial(_moe_combine_kernel, final),
        grid_spec=pltpu.PrefetchScalarGridSpec(
            num_scalar_prefetch=2,
            grid=(m // rows,),
            in_specs=[pl.BlockSpec((rows, k), lambda i, s0, s1: (i, 0)),
                      pl.BlockSpec((rows, LANES), lambda i, s0, s1: (i, 0)),
                      pl.BlockSpec((1, k), lambda i, s0, s1: (0, 0)),
                      pl.BlockSpec(memory_space=pl.ANY)],
            out_specs=pl.BlockSpec((rows, k), lambda i, s0, s1: (i, 0)),
            scratch_shapes=[pltpu.VMEM((TOP_K, rows, k), F32), pltpu.SemaphoreType.DMA],
        ),
        out_shape=jax.ShapeDtypeStruct((m, k), F32),
        compiler_params=_cparams(("arbitrary",)),
        name="moe_combine",
    )(slot_of[:, 0], slot_of[:, 1], x, gate_p, g.reshape(1, k), yb)


def _ssd_gate_norm(y, z, ng):
    y = y * _silu(z)
    gw = A_DIM // A_GROUPS
    outs = []
    for g in range(A_GROUPS):
        yg = y[:, g * gw:(g + 1) * gw]
        outs.append(yg * lax.rsqrt(jnp.mean(yg * yg, axis=-1, keepdims=True) + NORM_EPS))
    return jnp.concatenate(outs, axis=-1) * ng


def _ssd_prompt_kernel(z_ref, x_ref, b_ref, c_ref, dt_ref, cwx, cwb, cwc, cbx, cbb, cbc,
                       dtb, aneg, dsk, ng, y_ref, st_ref, h_scr, xpx, xpb, xpc, yacc):
    c = pl.program_id(1)
    ch = SSD_CHUNK

    @pl.when(c == 0)
    def _():
        h_scr[...] = jnp.zeros_like(h_scr)
        xpx[0:8, :] = jnp.zeros((8, A_DIM), F32)
        xpb[0:8, :] = jnp.zeros((8, A_STATE * A_GROUPS), F32)
        xpc[0:8, :] = jnp.zeros((8, A_STATE * A_GROUPS), F32)

    xpx[8:8 + ch, :] = x_ref[...]
    xpb[8:8 + ch, :] = b_ref[...]
    xpc[8:8 + ch, :] = c_ref[...]

    def conv(xp, cw, cb):
        acc = cb[...] + xp[pl.ds(8 - (CONV_W - 1), ch), :] * cw[0:1, :]
        for tap in range(1, CONV_W):
            acc = acc + xp[pl.ds(8 - (CONV_W - 1) + tap, ch), :] * cw[tap:tap + 1, :]
        return _silu(acc)

    ux = conv(xpx, cwx, cbx)
    ub = conv(xpb, cwb, cbb)
    uc = conv(xpc, cwc, cbc)
    xpx[0:8, :] = xpx[ch:ch + 8, :]
    xpb[0:8, :] = xpb[ch:ch + 8, :]
    xpc[0:8, :] = xpc[ch:ch + 8, :]

    dt = _softplus(dt_ref[...] + dtb[...])
    la = dt * aneg[...]
    tri = (_iota((ch, ch), 1) <= _iota((ch, ch), 0)).astype(F32)
    cum = _dot_hi(tri, la)
    eye = (_iota((A_HEADS, LANES), 0) == _iota((A_HEADS, LANES), 1)).astype(F32)
    cum_r = _dot_nt_hi(eye, cum)
    dt_r = _dot_nt_hi(eye, dt)
    cl = cum[ch - 1:ch, :]
    tail = jnp.exp(cl - cum) * dt
    ecum = jnp.exp(cum)
    ecl = jnp.exp(cl)
    causal = _iota((ch, ch), 1) <= _iota((ch, ch), 0)
    rep = A_HEADS // A_GROUPS

    for g in range(A_GROUPS):
        bg = ub[:, g * A_STATE:(g + 1) * A_STATE].astype(BF16)
        cg = uc[:, g * A_STATE:(g + 1) * A_STATE].astype(BF16)
        cb_g = _dot_nt(cg, bg)
        for h in range(g * rep, (g + 1) * rep):
            hs = slice(h * A_HEAD, (h + 1) * A_HEAD)
            seg = cum[:, h:h + 1] - cum_r[h:h + 1, :]
            w = jnp.exp(jnp.where(causal, seg, -jnp.inf)) * cb_g * dt_r[h:h + 1, :]
            xh = ux[:, hs]
            hst = h_scr[h]
            y_h = _dot(w, xh) + _dot_nt(cg, hst) * ecum[:, h:h + 1]
            yacc[:, hs] = y_h
            h_scr[h] = hst * ecl[:, h:h + 1] + _dot_tn(xh * tail[:, h:h + 1], bg)

    y = yacc[...] + dsk[...] * ux
    y_ref[...] = _ssd_gate_norm(y, z_ref[...], ng[...]).astype(BF16)

    @pl.when(c == pl.num_programs(1) - 1)
    def _():
        st_ref[...] = h_scr[...]


def _ssd_params(p):
    cw, cb = p["conv_w"], p["conv_b"].reshape(1, CONV_CH)
    pad = LANES - A_HEADS
    return dict(
        cwx=cw[:, :A_DIM], cwb=cw[:, A_DIM:A_DIM + A_GN], cwc=cw[:, A_DIM + A_GN:],
        cbx=cb[:, :A_DIM], cbb=cb[:, A_DIM:A_DIM + A_GN], cbc=cb[:, A_DIM + A_GN:],
        dtb=jnp.pad(p["dt_bias"], (0, pad)).reshape(1, LANES),
        aneg=jnp.pad(-jnp.exp(p["a_log"]), (0, pad)).reshape(1, LANES),
        dsk=jnp.repeat(p["d_skip"], A_HEAD).reshape(1, A_DIM),
        ng=p["ssm_norm_g"].reshape(1, A_DIM),
    )


def _ssd_prompt(proj, sp):
    bsz, seq, _ = proj.shape
    ch = SSD_CHUNK
    gn = A_GN

    def col(width, off):
        return pl.BlockSpec((None, ch, width), lambda b, c: (b, c, off // width))

    names = ("cwx", "cwb", "cwc", "cbx", "cbb", "cbc", "dtb", "aneg", "dsk", "ng")
    y, st = pl.pallas_call(
        _ssd_prompt_kernel,
        grid=(bsz, seq // ch),
        in_specs=[col(A_DIM, P_Z), col(A_DIM, P_X), col(gn, P_B), col(gn, P_C), col(LANES, P_DT)]
                 + [_full(sp[n].shape) for n in names],
        out_specs=[pl.BlockSpec((None, ch, A_DIM), lambda b, c: (b, c, 0)),
                   pl.BlockSpec((None, A_HEADS, A_HEAD, A_STATE), lambda b, c: (b, 0, 0, 0))],
        out_shape=[jax.ShapeDtypeStruct((bsz, seq, A_DIM), BF16),
                   jax.ShapeDtypeStruct((bsz, A_HEADS, A_HEAD, A_STATE), F32)],
        scratch_shapes=[pltpu.VMEM((A_HEADS, A_HEAD, A_STATE), F32),
                        pltpu.VMEM((ch + 8, A_DIM), F32),
                        pltpu.VMEM((ch + 8, gn), F32),
                        pltpu.VMEM((ch + 8, gn), F32),
                        pltpu.VMEM((ch, A_DIM), F32)],
        compiler_params=_cparams(("parallel", "arbitrary")),
        name="ssd_prompt",
    )(proj, proj, proj, proj, proj, *[sp[n] for n in names])
    return y, st


def _ssd_step_kernel(z_ref, x_ref, b_ref, c_ref, dt_ref, cs_ref, cw, cb, dtb, anegx, expand, dsk, ng,
                     h_ref, _, y_ref, ho_ref, y_scr):
    bb = STEP_BB
    xbc = (x_ref[...], b_ref[...], c_ref[...])
    offs = (0, A_DIM, A_DIM + A_GN, CONV_CH)
    u = []
    for i in range(3):
        sl = slice(offs[i], offs[i + 1])
        acc = cb[:, sl] + xbc[i] * cw[CONV_W - 1:CONV_W, sl]
        for tap in range(CONV_W - 1):
            acc = acc + cs_ref[tap][:, sl] * cw[tap:tap + 1, sl]
        u.append(_silu(acc))
    ux, ub, uc = u
    dt = _softplus(dt_ref[...] + dtb[...])
    dtx = _dot_hi(dt, expand[...])
    dec_t = jnp.exp(dtx * anegx[...]).T
    xdt_t = (ux * dtx).T
    gw = A_DIM // A_GROUPS
    for b in range(bb):
        for g in range(A_GROUPS):
            rs = slice(g * gw, (g + 1) * gw)
            ns = slice(g * A_STATE, (g + 1) * A_STATE)
            hg = h_ref[b, rs, :] * dec_t[rs, b:b + 1] + xdt_t[rs, b:b + 1] * ub[b:b + 1, ns]
            ho_ref[b, rs, :] = hg
            y_scr[b:b + 1, rs] = _dot_nt(uc[:, ns], hg)[b:b + 1, :]
    y = y_scr[...] + dsk[...] * ux
    y_ref[...] = _ssd_gate_norm(y, z_ref[...], ng[...]).astype(BF16)


def _ssd_step(proj, conv_state, h_all, h_new, layer, p, sp):
    bsz = proj.shape[0]
    bb = STEP_BB
    cs = jnp.swapaxes(conv_state, 0, 1)
    hflat = h_all.reshape(h_all.shape[0], bsz, A_DIM, A_STATE)
    expand = (jnp.arange(LANES)[:, None] == (jnp.arange(A_DIM) // A_HEAD)[None, :]).astype(F32)
    anegx = jnp.repeat(-jnp.exp(p["a_log"]), A_HEAD).reshape(1, A_DIM)
    cw, cb = p["conv_w"], p["conv_b"].reshape(1, CONV_CH)

    def col(width, off):
        return pl.BlockSpec((bb, width), lambda i: (i, off // width))

    y, hn = pl.pallas_call(
        _ssd_step_kernel,
        grid=(bsz // bb,),
        in_specs=[col(A_DIM, P_Z), col(A_DIM, P_X), col(A_GN, P_B), col(A_GN, P_C), col(LANES, P_DT),
                  pl.BlockSpec((CONV_W - 1, bb, CONV_CH), lambda i: (0, i, 0)),
                  _full(cw.shape), _full(cb.shape), _full(sp["dtb"].shape), _full(anegx.shape),
                  _full(expand.shape), _full(sp["dsk"].shape), _full(sp["ng"].shape),
                  pl.BlockSpec((None, bb, A_DIM, A_STATE), lambda i: (layer, i, 0, 0)),
                  pl.BlockSpec(memory_space=pl.ANY)],
        out_specs=[pl.BlockSpec((bb, A_DIM), lambda i: (i, 0)),
                   pl.BlockSpec((None, bb, A_DIM, A_STATE), lambda i: (layer, i, 0, 0))],
        out_shape=[jax.ShapeDtypeStruct((bsz, A_DIM), BF16),
                   jax.ShapeDtypeStruct(hflat.shape, F32)],
        input_output_aliases={14: 1},
        scratch_shapes=[pltpu.VMEM((bb, A_DIM), F32)],
        compiler_params=_cparams(("parallel",)),
        name="ssd_step",
    )(proj, proj, proj, proj, proj, cs, cw, cb, sp["dtb"], anegx, expand, sp["dsk"], sp["ng"], hflat,
      h_new.reshape(hflat.shape))
    return y, hn.reshape(h_all.shape)


def _hgrn_gates(q, f, lbp):
    qf = _silu(q)
    kf = lbp[0:1, :] * jax.nn.sigmoid(-f)
    log_sig = jnp.minimum(f, 0.0) - jnp.log1p(jnp.exp(-jnp.abs(f)))
    a = lbp[1:2, :]
    b = lbp[2:3, :] + log_sig
    logf = jnp.maximum(a, b) + jnp.log1p(jnp.exp(-jnp.abs(a - b)))
    return qf, kf, logf


def _hgrn_out(o, g, ng):
    outs = []
    for h in range(B_HEADS):
        oh = o[:, h * B_HEAD:(h + 1) * B_HEAD]
        outs.append(oh * lax.rsqrt(jnp.mean(oh * oh, axis=-1, keepdims=True) + NORM_EPS) * ng)
    return jnp.concatenate(outs, axis=-1) * _silu(g)


def _hgrn_prompt_kernel(q_ref, f_ref, i_ref, g_ref, lbp, ng, o_ref, st_ref,
                        s_scr, cum_scr, q_scr, k_scr, o_scr):
    c = pl.program_id(1)
    tb, cs = HGRN_BLOCK, HGRN_CHUNK

    @pl.when(c == 0)
    def _():
        s_scr[...] = jnp.zeros_like(s_scr)

    qf, kf, logf = _hgrn_gates(q_ref[...], f_ref[...], lbp[...])
    r_i, c_i = _iota((tb, tb), 0), _iota((tb, tb), 1)
    bd = ((r_i // cs == c_i // cs) & (c_i <= r_i)).astype(F32)
    cum_scr[...] = _dot_hi(bd, logf)
    q_scr[...] = qf
    k_scr[...] = kf
    rows = _iota((cs, B_HEAD), 0)

    def chunk(cc, carry):
        r0 = pl.multiple_of(cc * cs, cs)
        for h in range(B_HEADS):
            hs = slice(h * B_HEAD, (h + 1) * B_HEAD)
            cu = cum_scr[pl.ds(r0, cs), hs]
            q = q_scr[pl.ds(r0, cs), hs]
            k = k_scr[pl.ds(r0, cs), hs]
            v = i_ref[pl.ds(r0, cs), hs]
            st = s_scr[h]
            o = _dot_nt(q * jnp.exp(cu), st)
            for s in range(cs):
                d = jnp.exp(jnp.where(rows >= s, cu - cu[s:s + 1, :], -jnp.inf))
                att = jnp.sum(q * k[s:s + 1, :] * d, axis=-1, keepdims=True)
                o = o + att * v[s:s + 1, :]
            o_scr[pl.ds(r0, cs), hs] = o
            cl = cu[cs - 1:cs, :]
            s_scr[h] = st * jnp.exp(cl) + _dot_tn(v, k * jnp.exp(cl - cu))
        return carry

    lax.fori_loop(0, tb // cs, chunk, 0)
    o_ref[...] = _hgrn_out(o_scr[...], g_ref[...], ng[...]).astype(BF16)

    @pl.when(c == pl.num_programs(1) - 1)
    def _():
        for h in range(B_HEADS):
            st_ref[h] = s_scr[h].T


def _hgrn_prompt(proj, lbp, ng):
    bsz, seq, _ = proj.shape
    tb = HGRN_BLOCK

    def col(off):
        return pl.BlockSpec((None, tb, B_DIM), lambda b, c: (b, c, off // B_DIM))

    o, st = pl.pallas_call(
        _hgrn_prompt_kernel,
        grid=(bsz, seq // tb),
        in_specs=[col(P_Q), col(P_F), col(P_I), col(P_G), _full(lbp.shape), _full(ng.shape)],
        out_specs=[pl.BlockSpec((None, tb, B_DIM), lambda b, c: (b, c, 0)),
                   pl.BlockSpec((None, B_HEADS, B_HEAD, B_HEAD), lambda b, c: (b, 0, 0, 0))],
        out_shape=[jax.ShapeDtypeStruct((bsz, seq, B_DIM), BF16),
                   jax.ShapeDtypeStruct((bsz, B_HEADS, B_HEAD, B_HEAD), F32)],
        scratch_shapes=[pltpu.VMEM((B_HEADS, B_HEAD, B_HEAD), F32)]
                       + [pltpu.VMEM((tb, B_DIM), F32)] * 4,
        compiler_params=_cparams(("parallel", "arbitrary")),
        name="hgrn_prompt",
    )(proj, proj, proj, proj, lbp, ng)
    return o, st


def _hgrn_step_kernel(q_ref, f_ref, i_ref, g_ref, lbp, ng, s_ref, _, o_ref, so_ref, o_scr):
    bb = STEP_BB
    qf, kf, logf = _hgrn_gates(q_ref[...], f_ref[...], lbp[...])
    vf = i_ref[...]
    dec_t = jnp.exp(logf).T
    k_t = kf.T
    q_t = qf.T
    for b in range(bb):
        for h in range(B_HEADS):
            hs = slice(h * B_HEAD, (h + 1) * B_HEAD)
            sh = s_ref[b, hs, :] * dec_t[hs, b:b + 1] + k_t[hs, b:b + 1] * vf[b:b + 1, hs]
            so_ref[b, hs, :] = sh
            o_scr[b:b + 1, hs] = jnp.sum(q_t[hs, b:b + 1] * sh, axis=0, keepdims=True)
    o_ref[...] = _hgrn_out(o_scr[...], g_ref[...], ng[...]).astype(BF16)


def _hgrn_step(proj, s_all, s_new, layer, lbp, ng):
    bsz = proj.shape[0]
    bb = STEP_BB
    sflat = s_all.reshape(s_all.shape[0], bsz, B_DIM, B_HEAD)

    def col(off):
        return pl.BlockSpec((bb, B_DIM), lambda i: (i, off // B_DIM))

    o, sn = pl.pallas_call(
        _hgrn_step_kernel,
        grid=(bsz // bb,),
        in_specs=[col(P_Q), col(P_F), col(P_I), col(P_G), _full(lbp.shape), _full(ng.shape),
                  pl.BlockSpec((None, bb, B_DIM, B_HEAD), lambda i: (layer, i, 0, 0)),
                  pl.BlockSpec(memory_space=pl.ANY)],
        out_specs=[pl.BlockSpec((bb, B_DIM), lambda i: (i, 0)),
                   pl.BlockSpec((None, bb, B_DIM, B_HEAD), lambda i: (layer, i, 0, 0))],
        out_shape=[jax.ShapeDtypeStruct((bsz, B_DIM), BF16),
                   jax.ShapeDtypeStruct(sflat.shape, F32)],
        input_output_aliases={7: 1},
        scratch_shapes=[pltpu.VMEM((bb, B_DIM), F32)],
        compiler_params=_cparams(("parallel",)),
        name="hgrn_step",
    )(proj, proj, proj, proj, lbp, ng, sflat, s_new.reshape(sflat.shape))
    return o, sn.reshape(s_all.shape)


RW_NAMES = ("mu_r", "mu_k", "mu_v", "mu_lr", "w0", "w2", "a0", "a2", "g2", "k_k", "k_a", "r_k")


def _rwkv_params(p):
    mu = p["shift_mu"].reshape(1, C_COLS)
    row = lambda a: a.reshape(1, C_DIM)
    return dict(
        mu_r=mu[:, :C_DIM], mu_k=mu[:, C_DIM:2 * C_DIM], mu_v=mu[:, 2 * C_DIM:3 * C_DIM],
        mu_lr=mu[:, 3 * C_DIM:],
        w0=row(p["w0"]), w2=p["w2"].astype(BF16), a0=row(p["a0"]), a2=p["a2"].astype(BF16),
        g2=p["g2"].astype(BF16), k_k=row(p["k_k"]), k_a=row(p["k_a"]), r_k=row(p["r_k"]),
        lnx_w=row(p["lnx_w"]), lnx_b=row(p["lnx_b"]),
    )


def _head_sum(x):
    return [jnp.sum(x[:, h * C_HEAD:(h + 1) * C_HEAD], axis=-1, keepdims=True) for h in range(C_HEADS)]


def _head_bcast(cols, fn=lambda c: c):
    rows = cols[0].shape[0]
    return jnp.concatenate([jnp.broadcast_to(fn(c), (rows, C_HEAD)) for c in cols], axis=-1)


def _rwkv_prep(r, k, v, lr, pr, pk, pv, plr, prm):
    mu_r, mu_k, mu_v, mu_lr, w0, w2, a0, a2, g2, k_k, k_a, r_k = prm
    r = r + (pr - r) * mu_r
    k = k + (pk - k) * mu_k
    v = v + (pv - v) * mu_v
    lr = lr + (plr - lr) * mu_lr
    wl, al, gl = lr[:, :C_LR_W], lr[:, C_LR_W:C_LR_W + C_LR_A], lr[:, C_LR_W + C_LR_A:]
    wraw = -_softplus(-(w0 + _dot(jnp.tanh(wl), w2))) - 0.5
    ld = -jnp.exp(wraw)
    a = jax.nn.sigmoid(a0 + _dot(al, a2))
    g = _dot(jax.nn.sigmoid(gl), g2)
    kk = k * k_k
    kk = kk * _head_bcast(_head_sum(kk * kk), lambda n2: 1.0 / jnp.maximum(jnp.sqrt(n2), L2_EPS))
    k2 = k * (1.0 + (a - 1.0) * k_a)
    bonus = _head_bcast(_head_sum(r * k2 * r_k)) * v
    return r, k2, v, kk, kk * a, ld, g, bonus


def _rwkv_out(y, bonus, g, lnx_w, lnx_b):
    n = float(C_HEAD)
    mean = _head_bcast(_head_sum(y), lambda s: s / n)
    yc = y - mean
    rstd = _head_bcast(_head_sum(yc * yc), lambda s: lax.rsqrt(s / n + GN_EPS))
    return (yc * rstd * lnx_w + lnx_b + bonus) * g


RW_GH = 4
RW_GW = RW_GH * C_HEAD


def _bd_mask():
    return _iota((RW_GW, RW_GW), 0) // C_HEAD == _iota((RW_GW, RW_GW), 1) // C_HEAD


def _bd_rows(a):
    return jnp.where(_bd_mask(), jnp.concatenate([a] * RW_GH, axis=0), 0.0)


def _bd_lanes(s):
    return jnp.where(_bd_mask(), jnp.concatenate([s] * RW_GH, axis=1), 0.0)


def _bd_fold(m):
    m = jnp.where(_bd_mask(), m, 0.0)
    out = m[:, :C_HEAD]
    for h in range(1, RW_GH):
        out = out + m[:, h * C_HEAD:(h + 1) * C_HEAD]
    return out


def _to_stack(a):
    return jnp.concatenate([a[:, h * C_HEAD:(h + 1) * C_HEAD] for h in range(RW_GH)], axis=0)


def _to_all(s):
    return jnp.concatenate([s[h * C_HEAD:(h + 1) * C_HEAD, :] for h in range(RW_GH)], axis=1)


def _rwkv_a_kernel(r_ref, k_ref, v_ref, lr_ref, rp_ref, kp_ref, vp_ref, lrp_ref, *rest):
    prm = tuple(x[...] for x in rest[:len(RW_NAMES)])
    (kd_ref, rd_ref, bg_ref, ge_ref, bonus_ref, gate_ref, aab_ref, rb_ref, m2_ref, rkv_ref, vk_ref,
     xs_scr, b1_scr, b2_scr, b3_scr, b4_scr) = rest[len(RW_NAMES):]
    c = pl.program_id(1)
    tb, cs = RWKV_BLOCK, RWKV_CHUNK
    first = (c == 0)

    def prev(cur_ref, prev_ref, width):
        xs_scr[8:8 + tb, 0:width] = cur_ref[...]
        xs_scr[7:8, 0:width] = jnp.where(first, 0.0, prev_ref[7:8, :])
        return xs_scr[pl.ds(7, tb), 0:width]

    pr = prev(r_ref, rp_ref, C_DIM)
    pk = prev(k_ref, kp_ref, C_DIM)
    pv = prev(v_ref, vp_ref, C_DIM)
    plr = prev(lr_ref, lrp_ref, C_LR)
    r, k2, v, kk, beta, ld, g, bonus = _rwkv_prep(
        r_ref[...], k_ref[...], v_ref[...], lr_ref[...], pr, pk, pv, plr, prm)
    bonus_ref[...] = bonus
    gate_ref[...] = g

    r_i, c_i = _iota((tb, tb), 0), _iota((tb, tb), 1)
    same = (r_i // cs == c_i // cs)
    lcum = _dot_hi((same & (c_i <= r_i)).astype(F32), ld)
    lend = _dot_hi(same.astype(F32), ld)
    einv = jnp.exp(-lcum)
    eend = jnp.exp(lend - lcum)
    ge_ref[...] = jnp.exp(lend)
    kd_ref[...] = kk * jnp.exp(lcum - ld)
    rd_ref[...] = r * jnp.exp(lcum)
    bg_ref[...] = beta * eend
    b1_scr[...] = beta * einv
    b2_scr[...] = k2 * einv
    b3_scr[...] = k2 * eend
    b4_scr[...] = v
    gw = RW_GW
    tpos, spos = _iota((gw, cs), 0) % cs, _iota((gw, cs), 1)
    strict, incl = tpos > spos, tpos >= spos
    hc = C_HEAD

    def chunk(j, carry):
        r0 = pl.multiple_of(j * cs, cs)
        q0 = pl.multiple_of(j * gw, gw)
        for g in range(C_HEADS // RW_GH):
            gs = slice(g * gw, (g + 1) * gw)
            ls = slice(g * hc, (g + 1) * hc)
            kd = kd_ref[pl.ds(r0, cs), gs]
            rd = rd_ref[pl.ds(r0, cs), gs]
            v = b4_scr[pl.ds(r0, cs), gs]
            lhs = jnp.concatenate([_bd_rows(kd), _bd_rows(rd)], axis=0)
            rhs = jnp.concatenate([b1_scr[pl.ds(r0, cs), gs], b2_scr[pl.ds(r0, cs), gs]], axis=0)
            gm = _dot_nt(lhs, rhs)
            aab_ref[pl.ds(q0, gw), ls] = jnp.where(strict, gm[:gw, :hc], 0.0)
            aak = jnp.where(strict, gm[:gw, hc:], 0.0)
            rb_ref[pl.ds(q0, gw), ls] = jnp.where(incl, gm[gw:, :hc], 0.0)
            rk = jnp.where(incl, gm[gw:, hc:], 0.0)
            mr = _dot(jnp.concatenate([_bd_lanes(aak), _bd_lanes(rk)], axis=0), _to_stack(v))
            m2_ref[pl.ds(q0, gw), ls] = mr[:gw]
            rkv_ref[pl.ds(q0, gw), ls] = mr[gw:]
            vk_ref[pl.ds(q0, gw), ls] = _bd_fold(_dot_tn(v, b3_scr[pl.ds(r0, cs), gs]))
        return carry

    lax.fori_loop(0, tb // cs, chunk, 0)


def _tri_solve_kernel(a_ref, t_ref, a_scr, t_scr):
    n, hf = RWKV_CHUNK, RWKV_CHUNK // 2
    npb = a_scr.shape[2]
    ng = a_scr.shape[1] // n

    def load(t, carry):
        a_scr[t] = a_ref[pl.ds(t, npb, stride=n), :].T
        return carry

    lax.fori_loop(0, n, load, 0)
    jrow = _iota((hf, npb), 0)
    zero = jnp.zeros((hf, npb), F32)

    def coef(t, g, s):
        return a_scr[t, pl.ds(g * n + s, 1), :]

    def row_lo(t, carry):
        def inner(s, accs):
            return tuple(acc - coef(t, g, s) * t_scr[s, g * n:g * n + hf, :] for g, acc in enumerate(accs))

        accs = lax.fori_loop(0, t, inner, ((jrow == t).astype(F32),) * ng)
        for g, acc in enumerate(accs):
            t_scr[t, g * n:g * n + hf, :] = acc
            t_scr[t, g * n + hf:(g + 1) * n, :] = zero
        return carry

    lax.fori_loop(0, hf, row_lo, 0)

    def row_hi(t, carry):
        def inner_lo(s, accs):
            return tuple(acc - coef(t, g, s) * t_scr[s, g * n:g * n + hf, :] for g, acc in enumerate(accs))

        def inner_hi(s, accs):
            lo, hi = accs[:ng], accs[ng:]
            lo = tuple(acc - coef(t, g, s) * t_scr[s, g * n:g * n + hf, :] for g, acc in enumerate(lo))
            hi = tuple(acc - coef(t, g, s) * t_scr[s, g * n + hf:(g + 1) * n, :] for g, acc in enumerate(hi))
            return lo + hi

        lo = lax.fori_loop(0, hf, inner_lo, (zero,) * ng)
        accs = lax.fori_loop(hf, t, inner_hi, lo + ((jrow + hf == t).astype(F32),) * ng)
        for g in range(ng):
            t_scr[t, g * n:g * n + hf, :] = accs[g]
            t_scr[t, g * n + hf:(g + 1) * n, :] = accs[ng + g]
        return carry

    lax.fori_loop(hf, n, row_hi, 0)

    def store(t, carry):
        t_ref[pl.ds(t, npb, stride=n), :] = t_scr[t].T
        return carry

    lax.fori_loop(0, n, store, 0)


def _tri_solve(a):
    rows, width = a.shape
    n = RWKV_CHUNK
    npb = min(LANES, rows // n)
    return pl.pallas_call(
        _tri_solve_kernel,
        grid=(rows // (npb * n),),
        in_specs=[pl.BlockSpec((npb * n, width), lambda i: (i, 0))],
        out_specs=pl.BlockSpec((npb * n, width), lambda i: (i, 0)),
        out_shape=jax.ShapeDtypeStruct(a.shape, F32),
        scratch_shapes=[pltpu.VMEM((n, width, npb), F32)] * 2,
        compiler_params=_cparams(("parallel",)),
        name="tri_solve",
    )(a)


def _rwkv_c_kernel(t_ref, kd_ref, rd_ref, bg_ref, ge_ref, bonus_ref, gate_ref, rb_ref, m2_ref, rkv_ref, vk_ref,
                   lnw, lnb, y_ref, st_ref, s_scr, y_scr):
    c = pl.program_id(1)
    tb, cs, gw, hc = RWKV_BLOCK, RWKV_CHUNK, RW_GW, C_HEAD

    @pl.when(c == 0)
    def _():
        s_scr[...] = jnp.zeros_like(s_scr)

    def chunk(j, carry):
        r0 = pl.multiple_of(j * cs, cs)
        q0 = pl.multiple_of(j * gw, gw)
        for g in range(C_HEADS // RW_GH):
            gs = slice(g * gw, (g + 1) * gw)
            ls = slice(g * hc, (g + 1) * hc)
            rhs = jnp.concatenate([_to_stack(kd_ref[pl.ds(r0, cs), gs]), m2_ref[pl.ds(q0, gw), ls]], axis=1)
            wu = _dot(_bd_lanes(t_ref[pl.ds(q0, gw), ls]), rhs)
            rbwu = _dot(_bd_lanes(rb_ref[pl.ds(q0, gw), ls]), wu)
            rt = _to_stack(rd_ref[pl.ds(r0, cs), gs]) - rbwu[:, :hc]
            yc = rkv_ref[pl.ds(q0, gw), ls] - rbwu[:, hc:]
            wu_bd = jnp.concatenate([_bd_lanes(wu[:, :hc]), _bd_lanes(wu[:, hc:])], axis=1)
            nu = _dot_tn(wu_bd, _to_stack(bg_ref[pl.ds(r0, cs), gs]))
            ge_row = ge_ref[pl.ds(r0, 1), gs]
            ge_st = jnp.concatenate(
                [jnp.broadcast_to(ge_row[:, h * hc:(h + 1) * hc], (hc, hc)) for h in range(RW_GH)], axis=0)
            s = s_scr[g]
            s_bd = _bd_lanes(s)
            y_scr[pl.ds(r0, cs), gs] = _dot_nt(_to_all(rt), s_bd) + _to_all(yc)
            s_scr[g] = s * ge_st - _dot(s_bd, nu[:gw]) + (vk_ref[pl.ds(q0, gw), ls] - nu[gw:])
        return carry

    lax.fori_loop(0, tb // cs, chunk, 0)
    y_ref[...] = _rwkv_out(y_scr[...], bonus_ref[...], gate_ref[...], lnw[...], lnb[...]).astype(BF16)

    @pl.when(c == pl.num_programs(1) - 1)
    def _():
        st_ref[...] = s_scr[...]


def _rwkv_prompt(proj, rp):
    bsz, seq, _ = proj.shape
    tb, cs = RWKV_BLOCK, RWKV_CHUNK
    nblk = seq // tb

    def col(width, off):
        return pl.BlockSpec((None, tb, width), lambda b, c: (b, c, off // width))

    def pcol(width, off):
        return pl.BlockSpec((None, 8, width), lambda b, c: (b, jnp.maximum(c * (tb // 8) - 1, 0), off // width))

    ng = C_HEADS // RW_GH
    srows = tb // cs * RW_GW
    tok = pl.BlockSpec((None, tb, C_DIM), lambda b, c: (b, c, 0))
    stk = pl.BlockSpec((None, srows, ng * C_HEAD), lambda b, c: (b, c, 0))
    tok_shape = jax.ShapeDtypeStruct((bsz, seq, C_DIM), F32)
    stk_shape = jax.ShapeDtypeStruct((bsz, nblk * srows, ng * C_HEAD), F32)
    prm = [rp[n] for n in RW_NAMES]
    outs = pl.pallas_call(
        _rwkv_a_kernel,
        grid=(bsz, nblk),
        in_specs=[col(C_DIM, P_R), col(C_DIM, P_K), col(C_DIM, P_V), col(C_LR, P_LR),
                  pcol(C_DIM, P_R), pcol(C_DIM, P_K), pcol(C_DIM, P_V), pcol(C_LR, P_LR)]
                 + [_full(x.shape) for x in prm],
        out_specs=[tok] * 6 + [stk] * 5,
        out_shape=[tok_shape] * 6 + [stk_shape] * 5,
        scratch_shapes=[pltpu.VMEM((tb + 8, C_DIM), F32)] + [pltpu.VMEM((tb, C_DIM), F32)] * 4,
        compiler_params=_cparams(("parallel", "parallel")),
        name="rwkv_prep",
    )(proj, proj, proj, proj, proj, proj, proj, proj, *prm)
    kd, rd, bg, ge, bonus, gate, aab, rb, m2, rkv, vk = outs

    tmat = _tri_solve(aab.reshape(-1, ng * C_HEAD)).reshape(stk_shape.shape)

    y, st = pl.pallas_call(
        _rwkv_c_kernel,
        grid=(bsz, nblk),
        in_specs=[stk] + [tok] * 6 + [stk] * 4 + [_full((1, C_DIM))] * 2,
        out_specs=[tok, pl.BlockSpec((None, ng, RW_GW, C_HEAD), lambda b, c: (b, 0, 0, 0))],
        out_shape=[jax.ShapeDtypeStruct((bsz, seq, C_DIM), BF16),
                   jax.ShapeDtypeStruct((bsz, ng, RW_GW, C_HEAD), F32)],
        scratch_shapes=[pltpu.VMEM((ng, RW_GW, C_HEAD), F32), pltpu.VMEM((tb, C_DIM), F32)],
        compiler_params=_cparams(("parallel", "arbitrary")),
        name="rwkv_scan",
    )(tmat, kd, rd, bg, ge, bonus, gate, rb, m2, rkv, vk, rp["lnx_w"], rp["lnx_b"])
    return y, st.reshape(bsz, C_HEADS, C_HEAD, C_HEAD)


def _rwkv_step_kernel(r_ref, k_ref, v_ref, lr_ref, sh_ref, *rest):
    prm = tuple(x[...] for x in rest[:len(RW_NAMES)])
    lnw, lnb, s_ref, _, y_ref, so_ref = rest[len(RW_NAMES):]
    bb = STEP_BB
    sh = sh_ref[...]
    r, k2, v, kk, beta, ld, g, bonus = _rwkv_prep(
        r_ref[...], k_ref[...], v_ref[...], lr_ref[...],
        sh[:, :C_DIM], sh[:, C_DIM:2 * C_DIM], sh[:, 2 * C_DIM:3 * C_DIM], sh[:, 3 * C_DIM:], prm)
    w = jnp.exp(ld)
    v_t = v.T
    lane = _iota((C_DIM, LANES), 1)
    heads = [[x[:, h * C_HEAD:(h + 1) * C_HEAD] for x in (kk, w, beta, k2, r)] for h in range(C_HEADS)]

    def rows(i, b):
        return jnp.concatenate(
            [jnp.broadcast_to(heads[h][i][b:b + 1, :], (C_HEAD, C_HEAD)) for h in range(C_HEADS)], axis=0)

    y_t = jnp.zeros((C_DIM, LANES), F32)
    for b in range(bb):
        s = s_ref[b]
        sa = -jnp.sum(s * rows(0, b), axis=-1, keepdims=True)
        sn = s * rows(1, b) + sa * rows(2, b) + v_t[:, b:b + 1] * rows(3, b)
        so_ref[b] = sn
        y_t = jnp.where(lane == b, jnp.sum(sn * rows(4, b), axis=-1, keepdims=True), y_t)
    y = y_t.T[0:bb, :]
    y_ref[...] = _rwkv_out(y, bonus, g, lnw[...], lnb[...]).astype(BF16)


def _rwkv_step(proj, shift, s_all, s_new, layer, rp):
    bsz = proj.shape[0]
    bb = STEP_BB
    sflat = s_all.reshape(s_all.shape[0], bsz, C_DIM, C_HEAD)

    def col(width, off):
        return pl.BlockSpec((bb, width), lambda i: (i, off // width))

    prm = [rp[n] for n in RW_NAMES]
    y, sn = pl.pallas_call(
        _rwkv_step_kernel,
        grid=(bsz // bb,),
        in_specs=[col(C_DIM, P_R), col(C_DIM, P_K), col(C_DIM, P_V), col(C_LR, P_LR),
                  pl.BlockSpec((bb, C_COLS), lambda i: (i, 0))]
                 + [_full(x.shape) for x in prm] + [_full((1, C_DIM))] * 2
                 + [pl.BlockSpec((None, bb, C_DIM, C_HEAD), lambda i: (layer, i, 0, 0)),
                    pl.BlockSpec(memory_space=pl.ANY)],
        out_specs=[pl.BlockSpec((bb, C_DIM), lambda i: (i, 0)),
                   pl.BlockSpec((None, bb, C_DIM, C_HEAD), lambda i: (layer, i, 0, 0))],
        out_shape=[jax.ShapeDtypeStruct((bsz, C_DIM), BF16),
                   jax.ShapeDtypeStruct(sflat.shape, F32)],
        input_output_aliases={len(prm) + 8: 1},
        compiler_params=_cparams(("parallel",)),
        name="rwkv_step",
    )(proj, proj, proj, proj, shift, *prm, rp["lnx_w"], rp["lnx_b"], sflat, s_new.reshape(sflat.shape))
    return y, sn.reshape(s_all.shape)


def _layout_w_in_kernel(w_ref, o_ref):
    o = A_DIM + CONV_CH
    rows = w_ref.shape[0]
    o_ref[:, :o] = w_ref[:, :o].astype(BF16)
    o_ref[:, o:P_DT] = w_ref[:, o + A_HEADS:].astype(BF16)
    dt_tile = jnp.where(_iota((rows, LANES), 1) < A_HEADS, w_ref[:, o:o + LANES], 0.0)
    o_ref[:, P_DT:P_DT + LANES] = dt_tile.astype(BF16)
    o_ref[:, P_DT + LANES:] = jnp.zeros((rows, P_COLS - P_DT - LANES), BF16)


def _layout_w_in(w_all, layer):
    k = w_all.shape[1]
    tr = 256
    return pl.pallas_call(
        _layout_w_in_kernel,
        grid=(k // tr,),
        in_specs=[pl.BlockSpec((None, tr, IN_COLS), lambda i: (layer, i, 0))],
        out_specs=pl.BlockSpec((tr, P_COLS), lambda i: (i, 0)),
        out_shape=jax.ShapeDtypeStruct((k, P_COLS), BF16),
        compiler_params=_cparams(("parallel",)),
        name="layout_w_in",
    )(w_all)


def _xbc_cols(proj):
    return proj[..., P_X:P_X + CONV_CH]


def _moe_route(logits, n_tok):
    top_logit, top_idx = lax.top_k(logits, TOP_K)
    gate = jax.nn.softmax(top_logit, axis=-1)
    n_assign = n_tok * TOP_K
    flat_e = top_idx.reshape(-1).astype(jnp.int32)
    order = jnp.argsort(flat_e)
    onehot = jax.nn.one_hot(flat_e, N_EXPERTS, dtype=jnp.int32)
    counts = jnp.sum(onehot, axis=0)
    padded = (counts + MOE_BM - 1) // MOE_BM * MOE_BM
    pad_end = jnp.cumsum(padded)
    pad_start = pad_end - padded
    start = jnp.cumsum(counts) - counts
    rank = jnp.sum((jnp.cumsum(onehot, axis=0) - onehot) * onehot, axis=1)
    slot_of = (pad_start[flat_e] + rank).reshape(n_tok, TOP_K)
    n_blocks = -(-n_assign // MOE_BM) + N_EXPERTS
    n_slots = n_blocks * MOE_BM
    block_start = jnp.arange(n_blocks, dtype=jnp.int32) * MOE_BM
    nvalid = (pad_end[-1] // MOE_BM).astype(jnp.int32)
    bexp = jnp.minimum(jnp.searchsorted(pad_end, block_start, side="right"), N_EXPERTS - 1).astype(jnp.int32)
    slot_e = jnp.repeat(bexp, MOE_BM)
    within = jnp.arange(n_slots, dtype=jnp.int32) - pad_start[slot_e]
    src = order[jnp.clip(start[slot_e] + within, 0, n_assign - 1)] // TOP_K
    slot_tok = jnp.where(within < counts[slot_e], src, n_tok)
    left = counts[bexp] - (block_start - pad_start[bexp])
    nsub = jnp.clip((left + MOE_SUB - 1) // MOE_SUB, 0, MOE_BM // MOE_SUB).astype(jnp.int32)
    last = bexp[jnp.maximum(nvalid - 1, 0)]
    bexp = jnp.where(jnp.arange(n_blocks) < nvalid, bexp, last)
    first = jnp.concatenate([jnp.ones((1,), jnp.int32), (bexp[1:] != bexp[:-1]).astype(jnp.int32)])
    return gate, slot_tok, slot_of, bexp, first, nsub


def _moe(xp, xs, h, logits, wg, wu, wd, g, final):
    tp = xp.shape[0]
    n_tok = h.shape[0]
    gate, slot_tok, slot_of, bexp, first, nsub = _moe_route(logits[:, :N_EXPERTS], n_tok)
    xb = h[jnp.where(slot_tok == n_tok, 0, slot_tok)]
    yb = _moe_experts(xb, bexp, first, nsub, wg, wu, wd)
    return (_moe_combine(xp, yb, slot_of[:tp], gate[:tp], g, final),
            _moe_combine(xs, yb, slot_of[tp:], gate[tp:], g, final))


def kernel(x_prompt, x_sample, state_ssm, state_conv, state_hgrn, state_rwkv, state_shift, norm1_g, w_in, conv_w, conv_b, dt_bias, a_log, d_skip, ssm_norm_g, lb_logits, hgrn_norm_g, shift_mu, w0, w2, a0, a2, g2, k_k, k_a, r_k, lnx_w, lnx_b, w_out, norm2_g, ffn_w_gate, ffn_w_up, ffn_w_down, router_w, exp_w_gate, exp_w_up, exp_w_down, final_norm_g):
    depth = w_in.shape[0]
    bp, seq, _ = x_prompt.shape
    bs = x_sample.shape[0]
    tp = bp * seq
    lb_soft = jax.nn.softmax(lb_logits.astype(F32), axis=0)
    lower_bounds = jnp.clip(jnp.cumsum(lb_soft, axis=0) - lb_soft[0:1], 0.0, 1.0)

    xp = x_prompt.reshape(tp, D_MODEL)
    xs = x_sample.reshape(bs, D_MODEL)
    outs = {k: [] for k in ("p_ssm", "p_conv", "p_hgrn", "p_rwkv", "p_shift", "s_conv", "s_shift")}
    ssm_s, hgrn_s, rwkv_s = (jnp.zeros(s.shape, F32) for s in (state_ssm, state_hgrn, state_rwkv))
    normed = False
    for layer in range(depth):
        p = dict(conv_w=conv_w[layer], conv_b=conv_b[layer], dt_bias=dt_bias[layer], a_log=a_log[layer],
                 d_skip=d_skip[layer], ssm_norm_g=ssm_norm_g[layer], shift_mu=shift_mu[layer],
                 w0=w0[layer], w2=w2[layer], a0=a0[layer], a2=a2[layer], g2=g2[layer], k_k=k_k[layer],
                 k_a=k_a[layer], r_k=r_k[layer].reshape(-1), lnx_w=lnx_w[layer], lnx_b=lnx_b[layer])
        sp = _ssd_params(p)
        rp = _rwkv_params(p)
        lb = lower_bounds[layer]
        lbp = jnp.stack([1.0 - lb, jnp.log(lb), jnp.log1p(-lb)])
        hng = hgrn_norm_g[layer].reshape(1, B_HEAD)
        w_in_l = _layout_w_in(w_in, layer)
        w_out_l = w_out[layer].astype(BF16)
        j = layer // 2
        routed = layer % 2 == 1
        if routed:
            rw_f = jnp.pad(router_w[j], ((0, 0), (0, LANES - N_EXPERTS)))
            rw_hi = rw_f.astype(BF16)
            rw = jnp.concatenate([rw_hi, (rw_f - rw_hi.astype(F32)).astype(BF16)], axis=1)
            h_all = jnp.zeros((tp + bs, D_MODEL), BF16)
            l_all = jnp.zeros((tp + bs, LANES), F32)

        proj = _norm_matmul(xp, norm1_g[layer], w_in_l, 256).reshape(bp, seq, P_COLS)
        ya, ssm_p = _ssd_prompt(proj, sp)
        yb, hgrn_p = _hgrn_prompt(proj, lbp, hng)
        yc, rwkv_p = _rwkv_prompt(proj, rp)
        xp, hp, l_all = _out_proj(ya.reshape(tp, A_DIM), yb.reshape(tp, B_DIM), yc.reshape(tp, C_DIM),
                                  w_out_l, xp, norm2_g[layer], 256, (rw, h_all, l_all, 0) if routed else None)
        outs["p_ssm"].append(ssm_p)
        outs["p_conv"].append(_xbc_cols(proj[:, seq - (CONV_W - 1):, :]))
        outs["p_hgrn"].append(hgrn_p)
        outs["p_rwkv"].append(rwkv_p)
        outs["p_shift"].append(proj[:, seq - 1, P_R:P_R + C_COLS])

        projs = _norm_matmul(xs, norm1_g[layer], w_in_l, 128)
        ya, ssm_s = _ssd_step(projs, state_conv[layer], state_ssm, ssm_s, layer, p, sp)
        yb, hgrn_s = _hgrn_step(projs, state_hgrn, hgrn_s, layer, lbp, hng)
        yc, rwkv_s = _rwkv_step(projs, state_shift[layer], state_rwkv, rwkv_s, layer, rp)
        xs, hs, l_all = _out_proj(ya, yb, yc, w_out_l, xs, norm2_g[layer], 128,
                                  (rw, hp, l_all, tp) if routed else None)
        outs["s_conv"].append(jnp.concatenate([state_conv[layer][:, 1:, :], _xbc_cols(projs)[:, None, :]], axis=1))
        outs["s_shift"].append(projs[:, P_R:P_R + C_COLS])

        if not routed:
            xp = _ffn(xp, hp, ffn_w_gate[j], ffn_w_up[j], ffn_w_down[j])
            xs = _ffn(xs, hs, ffn_w_gate[j], ffn_w_up[j], ffn_w_down[j])
        else:
            normed = layer == depth - 1
            xp, xs = _moe(xp, xs, hs, l_all, exp_w_gate[j], exp_w_up[j], exp_w_down[j], final_norm_g, normed)

    if not normed:
        xp, xs = _final_norm(xp, final_norm_g, 512), _final_norm(xs, final_norm_g, 128)
    y_prompt = xp.reshape(bp, seq, D_MODEL)
    y_sample = xs.reshape(bs, 1, D_MODEL)
    st = {k: jnp.stack(v) for k, v in outs.items()}
    return (y_prompt, y_sample, st["p_ssm"], st["p_conv"], st["p_hgrn"], st["p_rwkv"], st["p_shift"],
            ssm_s, st["s_conv"], hgrn_s, rwkv_s, st["s_shift"])
```

```python
import functools

import jax
import jax.numpy as jnp
from jax import lax
from jax.experimental import pallas as pl
from jax.experimental.pallas import tpu as pltpu

F32 = jnp.float32
BF16 = jnp.bfloat16
HI = lax.Precision.HIGHEST

D_MODEL = 2048
A_DIM, A_HEAD, A_HEADS, A_GROUPS, A_STATE = 1024, 64, 16, 2, 128
A_GN = A_GROUPS * A_STATE
CONV_W = 4
CONV_CH = A_DIM + 2 * A_GN
B_DIM, B_HEAD, B_HEADS = 512, 128, 4
C_DIM, C_HEAD, C_HEADS = 512, 64, 8
C_LR_W, C_LR_A, C_LR_G = 64, 64, 128
C_LR = C_LR_W + C_LR_A + C_LR_G
C_COLS = 3 * C_DIM + C_LR
IN_COLS = A_DIM + CONV_CH + A_HEADS + 4 * B_DIM + C_COLS
D_FF, N_EXPERTS, TOP_K, E_FF = 5504, 8, 2, 7168
NORM_EPS, GN_EPS, L2_EPS = 1e-6, 64e-5, 1e-12

P_Z, P_X, P_B, P_C = 0, 1024, 2048, 2304
P_Q, P_F, P_I, P_G = 2560, 3072, 3584, 4096
P_R, P_K, P_V, P_LR, P_DT, P_COLS = 4608, 5120, 5632, 6144, 6400, 6656
LANES = 128

SSD_CHUNK = 256
HGRN_BLOCK, HGRN_CHUNK = 256, 16
RWKV_BLOCK, RWKV_CHUNK = 256, 64
STEP_BB = 8

FFN_TF, FFN_TN, FFN_TM_UP, FFN_TM_DOWN = 512, 1024, 1024, 256
MOE_BM, MOE_SUB, MOE_TF, MOE_TN, MOE_KP = 512, 256, 1024, 1024, 2

NT_DIMS = (((1,), (1,)), ((), ()))
TN_DIMS = (((0,), (0,)), ((), ()))


def _cparams(sem, vmem_mb=48):
    return pltpu.CompilerParams(dimension_semantics=sem, vmem_limit_bytes=vmem_mb * 1024 * 1024)


def _dot(a, b):
    return jnp.dot(a.astype(BF16), b.astype(BF16), preferred_element_type=F32)


def _dot_nt(a, b):
    return lax.dot_general(a.astype(BF16), b.astype(BF16), NT_DIMS, preferred_element_type=F32)


def _dot_tn(a, b):
    return lax.dot_general(a.astype(BF16), b.astype(BF16), TN_DIMS, preferred_element_type=F32)


def _dot_hi(a, b):
    return jnp.dot(a, b, precision=HI, preferred_element_type=F32)


def _dot_nt_hi(a, b):
    return lax.dot_general(a, b, NT_DIMS, precision=HI, preferred_element_type=F32)


def _silu(x):
    return x * jax.nn.sigmoid(x)


def _softplus(x):
    return jnp.maximum(x, 0.0) + jnp.log1p(jnp.exp(-jnp.abs(x)))


def _iota(shape, dim):
    return lax.broadcasted_iota(jnp.int32, shape, dim)


def _rms(x, g):
    return x * lax.rsqrt(jnp.mean(x * x, axis=-1, keepdims=True) + NORM_EPS) * g


def _full(shape):
    nd = len(shape)
    return pl.BlockSpec(shape, lambda *_: (0,) * nd)


def _norm_matmul_kernel(x_ref, g_ref, w_ref, o_ref):
    h = _rms(x_ref[...], g_ref[...]).astype(BF16)
    o_ref[...] = jnp.dot(h, w_ref[...], preferred_element_type=F32)


def _norm_matmul(x, g, w, tm):
    m, k = x.shape
    n = w.shape[1]
    return pl.pallas_call(
        _norm_matmul_kernel,
        grid=(m // tm,),
        in_specs=[pl.BlockSpec((tm, k), lambda i: (i, 0)),
                  pl.BlockSpec((1, k), lambda i: (0, 0)),
                  pl.BlockSpec((k, n), lambda i: (0, 0), pipeline_mode=pl.Buffered(1))],
        out_specs=pl.BlockSpec((tm, n), lambda i: (i, 0)),
        out_shape=jax.ShapeDtypeStruct((m, n), F32),
        compiler_params=_cparams(("parallel",), 56),
        name="norm_matmul",
    )(x, g.reshape(1, k), w)


def _out_proj_kernel(routed, ya_ref, yb_ref, yc_ref, w_ref, r_ref, g_ref, *rest):
    x = (r_ref[...]
         + jnp.dot(ya_ref[...], w_ref[0:A_DIM, :], preferred_element_type=F32)
         + jnp.dot(yb_ref[...], w_ref[A_DIM:A_DIM + B_DIM, :], preferred_element_type=F32)
         + jnp.dot(yc_ref[...], w_ref[A_DIM + B_DIM:, :], preferred_element_type=F32))
    h = _rms(x, g_ref[...])
    h_hi = h.astype(BF16)
    if routed:
        rw_ref, _, _, x_ref, h_ref, l_ref = rest
        h_lo = (h - h_hi.astype(F32)).astype(BF16)
        rw = rw_ref[...]
        l2 = jnp.dot(h_hi, rw, preferred_element_type=F32)
        l_ref[...] = (l2[:, :LANES] + l2[:, LANES:]
                      + jnp.dot(h_lo, rw[:, :LANES], preferred_element_type=F32))
    else:
        x_ref, h_ref = rest
    x_ref[...] = x
    h_ref[...] = h_hi


def _out_proj(ya, yb, yc, w, res, g, tm, routed=None):
    m = ya.shape[0]
    n = w.shape[1]
    row = lambda width: pl.BlockSpec((tm, width), lambda i: (i, 0))
    in_specs = [row(A_DIM), row(B_DIM), row(C_DIM), _full(w.shape), row(n), _full((1, n))]
    args = [ya, yb, yc, w, res, g.reshape(1, n)]
    aliases = {}
    if routed is None:
        out_specs = [row(n), row(n)]
        out_shape = [jax.ShapeDtypeStruct((m, n), F32), jax.ShapeDtypeStruct((m, n), BF16)]
    else:
        rw, h_buf, l_buf, row0 = routed
        blk0 = row0 // tm
        off = lambda width: pl.BlockSpec((tm, width), lambda i: (i + blk0, 0))
        in_specs += [_full(rw.shape), pl.BlockSpec(memory_space=pl.ANY), pl.BlockSpec(memory_space=pl.ANY)]
        args += [rw, h_buf, l_buf]
        out_specs = [row(n), off(n), off(LANES)]
        out_shape = [jax.ShapeDtypeStruct((m, n), F32), jax.ShapeDtypeStruct(h_buf.shape, BF16),
                     jax.ShapeDtypeStruct(l_buf.shape, F32)]
        aliases = {len(args) - 2: 1, len(args) - 1: 2}
    outs = pl.pallas_call(
        functools.partial(_out_proj_kernel, routed is not None),
        grid=(m // tm,),
        in_specs=in_specs, out_specs=out_specs, out_shape=out_shape,
        input_output_aliases=aliases,
        compiler_params=_cparams(("parallel",)),
        name="out_proj",
    )(*args)
    return outs if routed is not None else (*outs, None)


def _ffn_up_kernel(h_ref, wg_ref, wu_ref, o_ref, wg_s, wu_s):
    @pl.when(pl.program_id(1) == 0)
    def _():
        wg_s[...] = wg_ref[...].astype(BF16)
        wu_s[...] = wu_ref[...].astype(BF16)

    h = h_ref[...]
    a = jnp.dot(h, wg_s[...], preferred_element_type=F32)
    u = jnp.dot(h, wu_s[...], preferred_element_type=F32)
    o_ref[...] = (_silu(a) * u).astype(BF16)


def _ffn_down_kernel(h_ref, wd_ref, x_ref, o_ref, wd_s):
    @pl.when(pl.program_id(1) == 0)
    def _():
        wd_s[...] = wd_ref[...].astype(BF16)

    o_ref[...] = x_ref[...] + jnp.dot(h_ref[...], wd_s[...], preferred_element_type=F32)


def _ffn(x, h, wg, wu, wd):
    m, k = h.shape
    ff = wg.shape[1]
    tf, tn = FFN_TF, FFN_TN
    tu = min(m, FFN_TM_UP)
    tm = min(m, FFN_TM_DOWN)
    act = pl.pallas_call(
        _ffn_up_kernel,
        grid=(pl.cdiv(ff, tf), m // tu),
        in_specs=[pl.BlockSpec((tu, k), lambda f, i: (i, 0)),
                  pl.BlockSpec((k, tf), lambda f, i: (0, f)),
                  pl.BlockSpec((k, tf), lambda f, i: (0, f))],
        out_specs=pl.BlockSpec((tu, tf), lambda f, i: (i, f)),
        out_shape=jax.ShapeDtypeStruct((m, ff), BF16),
        scratch_shapes=[pltpu.VMEM((k, tf), BF16)] * 2,
        compiler_params=_cparams(("arbitrary", "arbitrary")),
        name="ffn_up",
    )(h, wg, wu)
    return pl.pallas_call(
        _ffn_down_kernel,
        grid=(k // tn, m // tm),
        in_specs=[pl.BlockSpec((tm, ff), lambda n, i: (i, 0)),
                  pl.BlockSpec((ff, tn), lambda n, i: (0, n), pipeline_mode=pl.Buffered(1)),
                  pl.BlockSpec((tm, tn), lambda n, i: (i, n))],
        out_specs=pl.BlockSpec((tm, tn), lambda n, i: (i, n)),
        out_shape=jax.ShapeDtypeStruct((m, k), F32),
        scratch_shapes=[pltpu.VMEM((ff, tn), BF16)],
        compiler_params=_cparams(("arbitrary", "arbitrary"), 56),
        name="ffn_down",
    )(act, wd, x)


def _moe_rows(nsub, o_ref, compute):
    bm = o_ref.shape[0]
    for ns in range(bm // MOE_SUB + 1):
        @pl.when(nsub == ns)
        def _(rows=ns * MOE_SUB):
            if rows:
                o_ref[0:rows, :] = compute(rows).astype(o_ref.dtype)
            if rows < bm:
                o_ref[rows:, :] = jnp.zeros((bm - rows, o_ref.shape[1]), o_ref.dtype)


def _weight_stream(sched, w_hbms, wbuf, sem, dsts):
    bexp_ref, first_ref, run_ref, nxt_ref, lastrun_ref, nruns_ref = sched
    col, b = pl.program_id(0), pl.program_id(1)
    ncol = pl.num_programs(0)
    width = wbuf.shape[-1]

    def copies(e, c, slot):
        cols = pl.ds(pl.multiple_of(c * width, width), width)
        return [pltpu.make_async_copy(w.at[e, :, cols], wbuf.at[slot, i], sem.at[slot])
                for i, w in enumerate(w_hbms)]

    @pl.when(first_ref[b] == 1)
    def _():
        slot = lax.rem(col * nruns_ref[0] + run_ref[b], 2)
        last = lastrun_ref[b]

        @pl.when((col == 0) & (b == 0))
        def _():
            for cp in copies(bexp_ref[0], 0, 0):
                cp.start()

        @pl.when((last == 0) | (col < ncol - 1))
        def _():
            for cp in copies(nxt_ref[b], col + last, 1 - slot):
                cp.start()

        for cp in copies(bexp_ref[b], col, slot):
            cp.wait()
        for i, dst in enumerate(dsts):
            dst[...] = wbuf[slot, i].astype(BF16)


def _moe_up_kernel(*refs):
    sched, (nsub_ref, _, x_ref, wg_hbm, wu_hbm, o_ref, wbuf, sem, wg_s, wu_s) = refs[:6], refs[6:]
    _weight_stream(sched, (wg_hbm, wu_hbm), wbuf, sem, (wg_s, wu_s))

    def compute(rows):
        x = x_ref[0:rows, :]
        a = jnp.dot(x, wg_s[...], preferred_element_type=F32)
        u = jnp.dot(x, wu_s[...], preferred_element_type=F32)
        return _silu(a) * u

    _moe_rows(nsub_ref[pl.program_id(1)], o_ref, compute)


def _moe_down_kernel(*refs):
    sched, (nsub_ref, _, h_ref, wd_hbm, o_ref, wbuf, sem, wd_s) = refs[:6], refs[6:]
    bexp_ref, first_ref, _, nxt_ref, lastrun_ref, _ = sched
    col, b, kk = pl.program_id(0), pl.program_id(1), pl.program_id(2)
    ncol = pl.num_programs(0)
    parts, kp, width = wbuf.shape

    def copies(e, c):
        cols = pl.ds(pl.multiple_of(c * width, width), width)
        return [pltpu.make_async_copy(wd_hbm.at[e, pl.ds(i * kp, kp), cols], wbuf.at[i], sem.at[i])
                for i in range(parts)]

    @pl.when((first_ref[b] == 1) & (kk == 0))
    def _():
        last = lastrun_ref[b]

        @pl.when((col == 0) & (b == 0))
        def _():
            for cp in copies(bexp_ref[0], 0):
                cp.start()

        for cp in copies(bexp_ref[b], col):
            cp.wait()
        rc = 512

        def cast(r, carry):
            rows = pl.ds(pl.multiple_of(r * rc, rc), rc)
            for i in range(parts):
                wd_s[i, rows, :] = wbuf[i, rows, :].astype(BF16)
            return carry

        lax.fori_loop(0, kp // rc, cast, 0)

        @pl.when((last == 0) | (col < ncol - 1))
        def _():
            for cp in copies(nxt_ref[b], col + last):
                cp.start()

    valid = nsub_ref[b] > 0
    for i in range(parts):
        @pl.when(valid & (kk == i))
        def _(i=i):
            acc = jnp.dot(h_ref[...], wd_s[i], preferred_element_type=F32)
            if i == 0:
                o_ref[...] = acc
            else:
                o_ref[...] += acc

    @pl.when(jnp.logical_not(valid) & (kk == 0))
    def _():
        o_ref[...] = jnp.zeros_like(o_ref)


def _run_schedule(bexp, first):
    nb = bexp.shape[0]
    ids = jnp.arange(nb, dtype=jnp.int32)
    run = jnp.cumsum(first).astype(jnp.int32) - 1
    nruns = run[-1] + 1
    starts = jnp.where(first == 1, ids, nb)
    nxt_start = jnp.concatenate([lax.cummin(starts[::-1])[::-1][1:], jnp.full((1,), nb, jnp.int32)])
    nxt = jnp.where(nxt_start < nb, bexp[jnp.minimum(nxt_start, nb - 1)], bexp[0]).astype(jnp.int32)
    lastrun = (run == nruns - 1).astype(jnp.int32)
    return bexp, first, run, nxt, lastrun, nruns.reshape(1)


def _moe_experts(xb, bexp, first, nsub, wg, wu, wd):
    n_slots, k = xb.shape
    ff = wg.shape[2]
    nb = n_slots // MOE_BM
    tf, tn = MOE_TF, MOE_TN
    hbm = pl.BlockSpec(memory_space=pl.ANY)

    def src_blocks(ns):
        return lax.cummax(jnp.where(ns > 0, jnp.arange(ns.shape[0], dtype=jnp.int32), 0))

    act = pl.pallas_call(
        _moe_up_kernel,
        grid_spec=pltpu.PrefetchScalarGridSpec(
            num_scalar_prefetch=8,
            grid=(ff // tf, nb),
            in_specs=[pl.BlockSpec((MOE_BM, k), lambda f, b, *s: (s[7][b], 0)), hbm, hbm],
            out_specs=pl.BlockSpec((MOE_BM, tf), lambda f, b, *s: (b, f)),
            scratch_shapes=[pltpu.VMEM((2, 2, k, tf), F32), pltpu.SemaphoreType.DMA((2,)),
                            pltpu.VMEM((k, tf), BF16), pltpu.VMEM((k, tf), BF16)],
        ),
        out_shape=jax.ShapeDtypeStruct((n_slots, ff), BF16),
        compiler_params=_cparams(("arbitrary", "arbitrary"), 56),
        name="moe_up",
    )(*_run_schedule(bexp, first), nsub, src_blocks(nsub), xb, wg, wu)
    per = MOE_BM // MOE_SUB
    bexp_d = jnp.repeat(bexp, per)
    first_d = jnp.concatenate([jnp.ones((1,), jnp.int32), (bexp_d[1:] != bexp_d[:-1]).astype(jnp.int32)])
    nsub_d = jnp.clip(jnp.repeat(nsub, per) - jnp.tile(jnp.arange(per, dtype=jnp.int32), nb), 0, 1)
    return pl.pallas_call(
        _moe_down_kernel,
        grid_spec=pltpu.PrefetchScalarGridSpec(
            num_scalar_prefetch=8,
            grid=(k // tn, nb * per, MOE_KP),
            in_specs=[pl.BlockSpec((MOE_SUB, ff // MOE_KP), lambda n, b, kk, *s: (s[7][b], kk)), hbm],
            out_specs=pl.BlockSpec((MOE_SUB, tn), lambda n, b, kk, *s: (b, n)),
            scratch_shapes=[pltpu.VMEM((MOE_KP, ff // MOE_KP, tn), F32), pltpu.SemaphoreType.DMA((MOE_KP,)),
                            pltpu.VMEM((MOE_KP, ff // MOE_KP, tn), BF16)],
        ),
        out_shape=jax.ShapeDtypeStruct((n_slots, k), F32),
        compiler_params=_cparams(("arbitrary", "arbitrary", "arbitrary"), 56),
        name="moe_down",
    )(*_run_schedule(bexp_d, first_d), nsub_d, src_blocks(nsub_d), act, wd)


def _final_norm_kernel(x_ref, g_ref, o_ref):
    o_ref[...] = _rms(x_ref[...], g_ref[...])


def _final_norm(x, g, tm):
    m, k = x.shape
    return pl.pallas_call(
        _final_norm_kernel,
        grid=(m // tm,),
        in_specs=[pl.BlockSpec((tm, k), lambda i: (i, 0)),
                  pl.BlockSpec((1, k), lambda i: (0, 0))],
        out_specs=pl.BlockSpec((tm, k), lambda i: (i, 0)),
        out_shape=jax.ShapeDtypeStruct((m, k), F32),
        compiler_params=_cparams(("parallel",)),
        name="final_norm",
    )(x, g.reshape(1, k))


def _row_copy(src_hbm, row, dst, r, sem):
    return pltpu.make_async_copy(src_hbm.at[pl.ds(row, 1), :], dst.at[pl.ds(r, 1), :], sem)


def _moe_combine_kernel(final, s0_ref, s1_ref, x_ref, gate_ref, g_ref, yb_hbm, o_ref, buf, sem):
    rows = x_ref.shape[0]
    base = pl.program_id(0) * rows

    def issue(r, carry):
        _row_copy(yb_hbm, s0_ref[base + r], buf.at[0], r, sem).start()
        _row_copy(yb_hbm, s1_ref[base + r], buf.at[1], r, sem).start()
        return carry

    lax.fori_loop(0, rows, issue, 0)
    pltpu.make_async_copy(buf, buf, sem).wait()
    gate = gate_ref[...]
    y = x_ref[...] + gate[:, 0:1] * buf[0] + gate[:, 1:2] * buf[1]
    o_ref[...] = _rms(y, g_ref[...]) if final else y


def _moe_combine(x, yb, slot_of, gate, g, final):
    m, k = x.shape
    rows = min(m, 256)
    gate_p = jnp.pad(gate, ((0, 0), (0, LANES - TOP_K)))
    return pl.pallas_call(
        functools.partial(_moe_combine_kernel, final),
        grid_spec=pltpu.PrefetchScalarGridSpec(
            num_scalar_prefetch=2,
            grid=(m // rows,),
            in_specs=[pl.BlockSpec((rows, k), lambda i, s0, s1: (i, 0)),
                      pl.BlockSpec((rows, LANES), lambda i, s0, s1: (i, 0)),
                      pl.BlockSpec((1, k), lambda i, s0, s1: (0, 0)),
                      pl.BlockSpec(memory_space=pl.ANY)],
            out_specs=pl.BlockSpec((rows, k), lambda i, s0, s1: (i, 0)),
            scratch_shapes=[pltpu.VMEM((TOP_K, rows, k), F32), pltpu.SemaphoreType.DMA],
        ),
        out_shape=jax.ShapeDtypeStruct((m, k), F32),
        compiler_params=_cparams(("arbitrary",)),
        name="moe_combine",
    )(slot_of[:, 0], slot_of[:, 1], x, gate_p, g.reshape(1, k), yb)


def _ssd_gate_norm(y, z, ng):
    y = y * _silu(z)
    gw = A_DIM // A_GROUPS
    outs = []
    for g in range(A_GROUPS):
        yg = y[:, g * gw:(g + 1) * gw]
        outs.append(yg * lax.rsqrt(jnp.mean(yg * yg, axis=-1, keepdims=True) + NORM_EPS))
    return jnp.concatenate(outs, axis=-1) * ng


def _ssd_prompt_kernel(z_ref, x_ref, b_ref, c_ref, dt_ref, cwx, cwb, cwc, cbx, cbb, cbc,
                       dtb, aneg, dsk, ng, y_ref, st_ref, h_scr, xpx, xpb, xpc, yacc):
    c = pl.program_id(1)
    ch = SSD_CHUNK

    @pl.when(c == 0)
    def _():
        h_scr[...] = jnp.zeros_like(h_scr)
        xpx[0:8, :] = jnp.zeros((8, A_DIM), F32)
        xpb[0:8, :] = jnp.zeros((8, A_STATE * A_GROUPS), F32)
        xpc[0:8, :] = jnp.zeros((8, A_STATE * A_GROUPS), F32)

    xpx[8:8 + ch, :] = x_ref[...]
    xpb[8:8 + ch, :] = b_ref[...]
    xpc[8:8 + ch, :] = c_ref[...]

    def conv(xp, cw, cb):
        acc = cb[...] + xp[pl.ds(8 - (CONV_W - 1), ch), :] * cw[0:1, :]
        for tap in range(1, CONV_W):
            acc = acc + xp[pl.ds(8 - (CONV_W - 1) + tap, ch), :] * cw[tap:tap + 1, :]
        return _silu(acc)

    ux = conv(xpx, cwx, cbx)
    ub = conv(xpb, cwb, cbb)
    uc = conv(xpc, cwc, cbc)
    xpx[0:8, :] = xpx[ch:ch + 8, :]
    xpb[0:8, :] = xpb[ch:ch + 8, :]
    xpc[0:8, :] = xpc[ch:ch + 8, :]

    dt = _softplus(dt_ref[...] + dtb[...])
    la = dt * aneg[...]
    tri = (_iota((ch, ch), 1) <= _iota((ch, ch), 0)).astype(F32)
    cum = _dot_hi(tri, la)
    eye = (_iota((A_HEADS, LANES), 0) == _iota((A_HEADS, LANES), 1)).astype(F32)
    cum_r = _dot_nt_hi(eye, cum)
    dt_r = _dot_nt_hi(eye, dt)
    cl = cum[ch - 1:ch, :]
    tail = jnp.exp(cl - cum) * dt
    ecum = jnp.exp(cum)
    ecl = jnp.exp(cl)
    causal = _iota((ch, ch), 1) <= _iota((ch, ch), 0)
    rep = A_HEADS // A_GROUPS

    for g in range(A_GROUPS):
        bg = ub[:, g * A_STATE:(g + 1) * A_STATE].astype(BF16)
        cg = uc[:, g * A_STATE:(g + 1) * A_STATE].astype(BF16)
        cb_g = _dot_nt(cg, bg)
        for h in range(g * rep, (g + 1) * rep):
            hs = slice(h * A_HEAD, (h + 1) * A_HEAD)
            seg = cum[:, h:h + 1] - cum_r[h:h + 1, :]
            w = jnp.exp(jnp.where(causal, seg, -jnp.inf)) * cb_g * dt_r[h:h + 1, :]
            xh = ux[:, hs]
            hst = h_scr[h]
            y_h = _dot(w, xh) + _dot_nt(cg, hst) * ecum[:, h:h + 1]
            yacc[:, hs] = y_h
            h_scr[h] = hst * ecl[:, h:h + 1] + _dot_tn(xh * tail[:, h:h + 1], bg)

    y = yacc[...] + dsk[...] * ux
    y_ref[...] = _ssd_gate_norm(y, z_ref[...], ng[...]).astype(BF16)

    @pl.when(c == pl.num_programs(1) - 1)
    def _():
        st_ref[...] = h_scr[...]


def _ssd_params(p):
    cw, cb = p["conv_w"], p["conv_b"].reshape(1, CONV_CH)
    pad = LANES - A_HEADS
    return dict(
        cwx=cw[:, :A_DIM], cwb=cw[:, A_DIM:A_DIM + A_GN], cwc=cw[:, A_DIM + A_GN:],
        cbx=cb[:, :A_DIM], cbb=cb[:, A_DIM:A_DIM + A_GN], cbc=cb[:, A_DIM + A_GN:],
        dtb=jnp.pad(p["dt_bias"], (0, pad)).reshape(1, LANES),
        aneg=jnp.pad(-jnp.exp(p["a_log"]), (0, pad)).reshape(1, LANES),
        dsk=jnp.repeat(p["d_skip"], A_HEAD).reshape(1, A_DIM),
        ng=p["ssm_norm_g"].reshape(1, A_DIM),
    )


def _ssd_prompt(proj, sp):
    bsz, seq, _ = proj.shape
    ch = SSD_CHUNK
    gn = A_GN

    def col(width, off):
        return pl.BlockSpec((None, ch, width), lambda b, c: (b, c, off // width))

    names = ("cwx", "cwb", "cwc", "cbx", "cbb", "cbc", "dtb", "aneg", "dsk", "ng")
    y, st = pl.pallas_call(
        _ssd_prompt_kernel,
        grid=(bsz, seq // ch),
        in_specs=[col(A_DIM, P_Z), col(A_DIM, P_X), col(gn, P_B), col(gn, P_C), col(LANES, P_DT)]
                 + [_full(sp[n].shape) for n in names],
        out_specs=[pl.BlockSpec((None, ch, A_DIM), lambda b, c: (b, c, 0)),
                   pl.BlockSpec((None, A_HEADS, A_HEAD, A_STATE), lambda b, c: (b, 0, 0, 0))],
        out_shape=[jax.ShapeDtypeStruct((bsz, seq, A_DIM), BF16),
                   jax.ShapeDtypeStruct((bsz, A_HEADS, A_HEAD, A_STATE), F32)],
        scratch_shapes=[pltpu.VMEM((A_HEADS, A_HEAD, A_STATE), F32),
                        pltpu.VMEM((ch + 8, A_DIM), F32),
                        pltpu.VMEM((ch + 8, gn), F32),
                        pltpu.VMEM((ch + 8, gn), F32),
                        pltpu.VMEM((ch, A_DIM), F32)],
        compiler_params=_cparams(("parallel", "arbitrary")),
        name="ssd_prompt",
    )(proj, proj, proj, proj, proj, *[sp[n] for n in names])
    return y, st


def _ssd_step_kernel(z_ref, x_ref, b_ref, c_ref, dt_ref, cs_ref, cw, cb, dtb, anegx, expand, dsk, ng,
                     h_ref, _, y_ref, ho_ref, y_scr):
    bb = STEP_BB
    xbc = (x_ref[...], b_ref[...], c_ref[...])
    offs = (0, A_DIM, A_DIM + A_GN, CONV_CH)
    u = []
    for i in range(3):
        sl = slice(offs[i], offs[i + 1])
        acc = cb[:, sl] + xbc[i] * cw[CONV_W - 1:CONV_W, sl]
        for tap in range(CONV_W - 1):
            acc = acc + cs_ref[tap][:, sl] * cw[tap:tap + 1, sl]
        u.append(_silu(acc))
    ux, ub, uc = u
    dt = _softplus(dt_ref[...] + dtb[...])
    dtx = _dot_hi(dt, expand[...])
    dec_t = jnp.exp(dtx * anegx[...]).T
    xdt_t = (ux * dtx).T
    gw = A_DIM // A_GROUPS
    for b in range(bb):
        for g in range(A_GROUPS):
            rs = slice(g * gw, (g + 1) * gw)
            ns = slice(g * A_STATE, (g + 1) * A_STATE)
            hg = h_ref[b, rs, :] * dec_t[rs, b:b + 1] + xdt_t[rs, b:b + 1] * ub[b:b + 1, ns]
            ho_ref[b, rs, :] = hg
            y_scr[b:b + 1, rs] = _dot_nt(uc[:, ns], hg)[b:b + 1, :]
    y = y_scr[...] + dsk[...] * ux
    y_ref[...] = _ssd_gate_norm(y, z_ref[...], ng[...]).astype(BF16)


def _ssd_step(proj, conv_state, h_all, h_new, layer, p, sp):
    bsz = proj.shape[0]
    bb = STEP_BB
    cs = jnp.swapaxes(conv_state, 0, 1)
    hflat = h_all.reshape(h_all.shape[0], bsz, A_DIM, A_STATE)
    expand = (jnp.arange(LANES)[:, None] == (jnp.arange(A_DIM) // A_HEAD)[None, :]).astype(F32)
    anegx = jnp.repeat(-jnp.exp(p["a_log"]), A_HEAD).reshape(1, A_DIM)
    cw, cb = p["conv_w"], p["conv_b"].reshape(1, CONV_CH)

    def col(width, off):
        return pl.BlockSpec((bb, width), lambda i: (i, off // width))

    y, hn = pl.pallas_call(
        _ssd_step_kernel,
        grid=(bsz // bb,),
        in_specs=[col(A_DIM, P_Z), col(A_DIM, P_X), col(A_GN, P_B), col(A_GN, P_C), col(LANES, P_DT),
                  pl.BlockSpec((CONV_W - 1, bb, CONV_CH), lambda i: (0, i, 0)),
                  _full(cw.shape), _full(cb.shape), _full(sp["dtb"].shape), _full(anegx.shape),
                  _full(expand.shape), _full(sp["dsk"].shape), _full(sp["ng"].shape),
                  pl.BlockSpec((None, bb, A_DIM, A_STATE), lambda i: (layer, i, 0, 0)),
                  pl.BlockSpec(memory_space=pl.ANY)],
        out_specs=[pl.BlockSpec((bb, A_DIM), lambda i: (i, 0)),
                   pl.BlockSpec((None, bb, A_DIM, A_STATE), lambda i: (layer, i, 0, 0))],
        out_shape=[jax.ShapeDtypeStruct((bsz, A_DIM), BF16),
                   jax.ShapeDtypeStruct(hflat.shape, F32)],
        input_output_aliases={14: 1},
        scratch_shapes=[pltpu.VMEM((bb, A_DIM), F32)],
        compiler_params=_cparams(("parallel",)),
        name="ssd_step",
    )(proj, proj, proj, proj, proj, cs, cw, cb, sp["dtb"], anegx, expand, sp["dsk"], sp["ng"], hflat,
      h_new.reshape(hflat.shape))
    return y, hn.reshape(h_all.shape)


def _hgrn_gates(q, f, lbp):
    qf = _silu(q)
    kf = lbp[0:1, :] * jax.nn.sigmoid(-f)
    log_sig = jnp.minimum(f, 0.0) - jnp.log1p(jnp.exp(-jnp.abs(f)))
    a = lbp[1:2, :]
    b = lbp[2:3, :] + log_sig
    logf = jnp.maximum(a, b) + jnp.log1p(jnp.exp(-jnp.abs(a - b)))
    return qf, kf, logf


def _hgrn_out(o, g, ng):
    outs = []
    for h in range(B_HEADS):
        oh = o[:, h * B_HEAD:(h + 1) * B_HEAD]
        outs.append(oh * lax.rsqrt(jnp.mean(oh * oh, axis=-1, keepdims=True) + NORM_EPS) * ng)
    return jnp.concatenate(outs, axis=-1) * _silu(g)


def _hgrn_prompt_kernel(q_ref, f_ref, i_ref, g_ref, lbp, ng, o_ref, st_ref,
                        s_scr, cum_scr, q_scr, k_scr, o_scr):
    c = pl.program_id(1)
    tb, cs = HGRN_BLOCK, HGRN_CHUNK

    @pl.when(c == 0)
    def _():
        s_scr[...] = jnp.zeros_like(s_scr)

    qf, kf, logf = _hgrn_gates(q_ref[...], f_ref[...], lbp[...])
    r_i, c_i = _iota((tb, tb), 0), _iota((tb, tb), 1)
    bd = ((r_i // cs == c_i // cs) & (c_i <= r_i)).astype(F32)
    cum_scr[...] = _dot_hi(bd, logf)
    q_scr[...] = qf
    k_scr[...] = kf
    rows = _iota((cs, B_HEAD), 0)

    def chunk(cc, carry):
        r0 = pl.multiple_of(cc * cs, cs)
        for h in range(B_HEADS):
            hs = slice(h * B_HEAD, (h + 1) * B_HEAD)
            cu = cum_scr[pl.ds(r0, cs), hs]
            q = q_scr[pl.ds(r0, cs), hs]
            k = k_scr[pl.ds(r0, cs), hs]
            v = i_ref[pl.ds(r0, cs), hs]
            st = s_scr[h]
            o = _dot_nt(q * jnp.exp(cu), st)
            for s in range(cs):
                d = jnp.exp(jnp.where(rows >= s, cu - cu[s:s + 1, :], -jnp.inf))
                att = jnp.sum(q * k[s:s + 1, :] * d, axis=-1, keepdims=True)
                o = o + att * v[s:s + 1, :]
            o_scr[pl.ds(r0, cs), hs] = o
            cl = cu[cs - 1:cs, :]
            s_scr[h] = st * jnp.exp(cl) + _dot_tn(v, k * jnp.exp(cl - cu))
        return carry

    lax.fori_loop(0, tb // cs, chunk, 0)
    o_ref[...] = _hgrn_out(o_scr[...], g_ref[...], ng[...]).astype(BF16)

    @pl.when(c == pl.num_programs(1) - 1)
    def _():
        for h in range(B_HEADS):
            st_ref[h] = s_scr[h].T


def _hgrn_prompt(proj, lbp, ng):
    bsz, seq, _ = proj.shape
    tb = HGRN_BLOCK

    def col(off):
        return pl.BlockSpec((None, tb, B_DIM), lambda b, c: (b, c, off // B_DIM))

    o, st = pl.pallas_call(
        _hgrn_prompt_kernel,
        grid=(bsz, seq // tb),
        in_specs=[col(P_Q), col(P_F), col(P_I), col(P_G), _full(lbp.shape), _full(ng.shape)],
        out_specs=[pl.BlockSpec((None, tb, B_DIM), lambda b, c: (b, c, 0)),
                   pl.BlockSpec((None, B_HEADS, B_HEAD, B_HEAD), lambda b, c: (b, 0, 0, 0))],
        out_shape=[jax.ShapeDtypeStruct((bsz, seq, B_DIM), BF16),
                   jax.ShapeDtypeStruct((bsz, B_HEADS, B_HEAD, B_HEAD), F32)],
        scratch_shapes=[pltpu.VMEM((B_HEADS, B_HEAD, B_HEAD), F32)]
                       + [pltpu.VMEM((tb, B_DIM), F32)] * 4,
        compiler_params=_cparams(("parallel", "arbitrary")),
        name="hgrn_prompt",
    )(proj, proj, proj, proj, lbp, ng)
    return o, st


def _hgrn_step_kernel(q_ref, f_ref, i_ref, g_ref, lbp, ng, s_ref, _, o_ref, so_ref, o_scr):
    bb = STEP_BB
    qf, kf, logf = _hgrn_gates(q_ref[...], f_ref[...], lbp[...])
    vf = i_ref[...]
    dec_t = jnp.exp(logf).T
    k_t = kf.T
    q_t = qf.T
    for b in range(bb):
        for h in range(B_HEADS):
            hs = slice(h * B_HEAD, (h + 1) * B_HEAD)
            sh = s_ref[b, hs, :] * dec_t[hs, b:b + 1] + k_t[hs, b:b + 1] * vf[b:b + 1, hs]
            so_ref[b, hs, :] = sh
            o_scr[b:b + 1, hs] = jnp.sum(q_t[hs, b:b + 1] * sh, axis=0, keepdims=True)
    o_ref[...] = _hgrn_out(o_scr[...], g_ref[...], ng[...]).astype(BF16)


def _hgrn_step(proj, s_all, s_new, layer, lbp, ng):
    bsz = proj.shape[0]
    bb = STEP_BB
    sflat = s_all.reshape(s_all.shape[0], bsz, B_DIM, B_HEAD)

    def col(off):
        return pl.BlockSpec((bb, B_DIM), lambda i: (i, off // B_DIM))

    o, sn = pl.pallas_call(
        _hgrn_step_kernel,
        grid=(bsz // bb,),
        in_specs=[col(P_Q), col(P_F), col(P_I), col(P_G), _full(lbp.shape), _full(ng.shape),
                  pl.BlockSpec((None, bb, B_DIM, B_HEAD), lambda i: (layer, i, 0, 0)),
                  pl.BlockSpec(memory_space=pl.ANY)],
        out_specs=[pl.BlockSpec((bb, B_DIM), lambda i: (i, 0)),
                   pl.BlockSpec((None, bb, B_DIM, B_HEAD), lambda i: (layer, i, 0, 0))],
        out_shape=[jax.ShapeDtypeStruct((bsz, B_DIM), BF16),
                   jax.ShapeDtypeStruct(sflat.shape, F32)],
        input_output_aliases={7: 1},
        scratch_shapes=[pltpu.VMEM((bb, B_DIM), F32)],
        compiler_params=_cparams(("parallel",)),
        name="hgrn_step",
    )(proj, proj, proj, proj, lbp, ng, sflat, s_new.reshape(sflat.shape))
    return o, sn.reshape(s_all.shape)


RW_NAMES = ("mu_r", "mu_k", "mu_v", "mu_lr", "w0", "w2", "a0", "a2", "g2", "k_k", "k_a", "r_k")


def _rwkv_params(p):
    mu = p["shift_mu"].reshape(1, C_COLS)
    row = lambda a: a.reshape(1, C_DIM)
    return dict(
        mu_r=mu[:, :C_DIM], mu_k=mu[:, C_DIM:2 * C_DIM], mu_v=mu[:, 2 * C_DIM:3 * C_DIM],
        mu_lr=mu[:, 3 * C_DIM:],
        w0=row(p["w0"]), w2=p["w2"].astype(BF16), a0=row(p["a0"]), a2=p["a2"].astype(BF16),
        g2=p["g2"].astype(BF16), k_k=row(p["k_k"]), k_a=row(p["k_a"]), r_k=row(p["r_k"]),
        lnx_w=row(p["lnx_w"]), lnx_b=row(p["lnx_b"]),
    )


def _head_sum(x):
    return [jnp.sum(x[:, h * C_HEAD:(h + 1) * C_HEAD], axis=-1, keepdims=True) for h in range(C_HEADS)]


def _head_bcast(cols, fn=lambda c: c):
    rows = cols[0].shape[0]
    return jnp.concatenate([jnp.broadcast_to(fn(c), (rows, C_HEAD)) for c in cols], axis=-1)


def _rwkv_prep(r, k, v, lr, pr, pk, pv, plr, prm):
    mu_r, mu_k, mu_v, mu_lr, w0, w2, a0, a2, g2, k_k, k_a, r_k = prm
    r = r + (pr - r) * mu_r
    k = k + (pk - k) * mu_k
    v = v + (pv - v) * mu_v
    lr = lr + (plr - lr) * mu_lr
    wl, al, gl = lr[:, :C_LR_W], lr[:, C_LR_W:C_LR_W + C_LR_A], lr[:, C_LR_W + C_LR_A:]
    wraw = -_softplus(-(w0 + _dot(jnp.tanh(wl), w2))) - 0.5
    ld = -jnp.exp(wraw)
    a = jax.nn.sigmoid(a0 + _dot(al, a2))
    g = _dot(jax.nn.sigmoid(gl), g2)
    kk = k * k_k
    kk = kk * _head_bcast(_head_sum(kk * kk), lambda n2: 1.0 / jnp.maximum(jnp.sqrt(n2), L2_EPS))
    k2 = k * (1.0 + (a - 1.0) * k_a)
    bonus = _head_bcast(_head_sum(r * k2 * r_k)) * v
    return r, k2, v, kk, kk * a, ld, g, bonus


def _rwkv_out(y, bonus, g, lnx_w, lnx_b):
    n = float(C_HEAD)
    mean = _head_bcast(_head_sum(y), lambda s: s / n)
    yc = y - mean
    rstd = _head_bcast(_head_sum(yc * yc), lambda s: lax.rsqrt(s / n + GN_EPS))
    return (yc * rstd * lnx_w + lnx_b + bonus) * g


RW_GH = 4
RW_GW = RW_GH * C_HEAD


def _bd_mask():
    return _iota((RW_GW, RW_GW), 0) // C_HEAD == _iota((RW_GW, RW_GW), 1) // C_HEAD


def _bd_rows(a):
    return jnp.where(_bd_mask(), jnp.concatenate([a] * RW_GH, axis=0), 0.0)


def _bd_lanes(s):
    return jnp.where(_bd_mask(), jnp.concatenate([s] * RW_GH, axis=1), 0.0)


def _bd_fold(m):
    m = jnp.where(_bd_mask(), m, 0.0)
    out = m[:, :C_HEAD]
    for h in range(1, RW_GH):
        out = out + m[:, h * C_HEAD:(h + 1) * C_HEAD]
    return out


def _to_stack(a):
    return jnp.concatenate([a[:, h * C_HEAD:(h + 1) * C_HEAD] for h in range(RW_GH)], axis=0)


def _to_all(s):
    return jnp.concatenate([s[h * C_HEAD:(h + 1) * C_HEAD, :] for h in range(RW_GH)], axis=1)


def _rwkv_a_kernel(r_ref, k_ref, v_ref, lr_ref, rp_ref, kp_ref, vp_ref, lrp_ref, *rest):
    prm = tuple(x[...] for x in rest[:len(RW_NAMES)])
    (kd_ref, rd_ref, bg_ref, ge_ref, bonus_ref, gate_ref, aab_ref, rb_ref, m2_ref, rkv_ref, vk_ref,
     xs_scr, b1_scr, b2_scr, b3_scr, b4_scr) = rest[len(RW_NAMES):]
    c = pl.program_id(1)
    tb, cs = RWKV_BLOCK, RWKV_CHUNK
    first = (c == 0)

    def prev(cur_ref, prev_ref, width):
        xs_scr[8:8 + tb, 0:width] = cur_ref[...]
        xs_scr[7:8, 0:width] = jnp.where(first, 0.0, prev_ref[7:8, :])
        return xs_scr[pl.ds(7, tb), 0:width]

    pr = prev(r_ref, rp_ref, C_DIM)
    pk = prev(k_ref, kp_ref, C_DIM)
    pv = prev(v_ref, vp_ref, C_DIM)
    plr = prev(lr_ref, lrp_ref, C_LR)
    r, k2, v, kk, beta, ld, g, bonus = _rwkv_prep(
        r_ref[...], k_ref[...], v_ref[...], lr_ref[...], pr, pk, pv, plr, prm)
    bonus_ref[...] = bonus
    gate_ref[...] = g

    r_i, c_i = _iota((tb, tb), 0), _iota((tb, tb), 1)
    same = (r_i // cs == c_i // cs)
    lcum = _dot_hi((same & (c_i <= r_i)).astype(F32), ld)
    lend = _dot_hi(same.astype(F32), ld)
    einv = jnp.exp(-lcum)
    eend = jnp.exp(lend - lcum)
    ge_ref[...] = jnp.exp(lend)
    kd_ref[...] = kk * jnp.exp(lcum - ld)
    rd_ref[...] = r * jnp.exp(lcum)
    bg_ref[...] = beta * eend
    b1_scr[...] = beta * einv
    b2_scr[...] = k2 * einv
    b3_scr[...] = k2 * eend
    b4_scr[...] = v
    gw = RW_GW
    tpos, spos = _iota((gw, cs), 0) % cs, _iota((gw, cs), 1)
    strict, incl = tpos > spos, tpos >= spos
    hc = C_HEAD

    def chunk(j, carry):
        r0 = pl.multiple_of(j * cs, cs)
        q0 = pl.multiple_of(j * gw, gw)
        for g in range(C_HEADS // RW_GH):
            gs = slice(g * gw, (g + 1) * gw)
            ls = slice(g * hc, (g + 1) * hc)
            kd = kd_ref[pl.ds(r0, cs), gs]
            rd = rd_ref[pl.ds(r0, cs), gs]
            v = b4_scr[pl.ds(r0, cs), gs]
            lhs = jnp.concatenate([_bd_rows(kd), _bd_rows(rd)], axis=0)
            rhs = jnp.concatenate([b1_scr[pl.ds(r0, cs), gs], b2_scr[pl.ds(r0, cs), gs]], axis=0)
            gm = _dot_nt(lhs, rhs)
            aab_ref[pl.ds(q0, gw), ls] = jnp.where(strict, gm[:gw, :hc], 0.0)
            aak = jnp.where(strict, gm[:gw, hc:], 0.0)
            rb_ref[pl.ds(q0, gw), ls] = jnp.where(incl, gm[gw:, :hc], 0.0)
            rk = jnp.where(incl, gm[gw:, hc:], 0.0)
            mr = _dot(jnp.concatenate([_bd_lanes(aak), _bd_lanes(rk)], axis=0), _to_stack(v))
            m2_ref[pl.ds(q0, gw), ls] = mr[:gw]
            rkv_ref[pl.ds(q0, gw), ls] = mr[gw:]
            vk_ref[pl.ds(q0, gw), ls] = _bd_fold(_dot_tn(v, b3_scr[pl.ds(r0, cs), gs]))
        return carry

    lax.fori_loop(0, tb // cs, chunk, 0)


def _tri_solve_kernel(a_ref, t_ref, a_scr, t_scr):
    n, hf = RWKV_CHUNK, RWKV_CHUNK // 2
    npb = a_scr.shape[2]
    ng = a_scr.shape[1] // n

    def load(t, carry):
        a_scr[t] = a_ref[pl.ds(t, npb, stride=n), :].T
        return carry

    lax.fori_loop(0, n, load, 0)
    jrow = _iota((hf, npb), 0)
    zero = jnp.zeros((hf, npb), F32)

    def coef(t, g, s):
        return a_scr[t, pl.ds(g * n + s, 1), :]

    def row_lo(t, carry):
        def inner(s, accs):
            return tuple(acc - coef(t, g, s) * t_scr[s, g * n:g * n + hf, :] for g, acc in enumerate(accs))

        accs = lax.fori_loop(0, t, inner, ((jrow == t).astype(F32),) * ng)
        for g, acc in enumerate(accs):
            t_scr[t, g * n:g * n + hf, :] = acc
            t_scr[t, g * n + hf:(g + 1) * n, :] = zero
        return carry

    lax.fori_loop(0, hf, row_lo, 0)

    def row_hi(t, carry):
        def inner_lo(s, accs):
            return tuple(acc - coef(t, g, s) * t_scr[s, g * n:g * n + hf, :] for g, acc in enumerate(accs))

        def inner_hi(s, accs):
            lo, hi = accs[:ng], accs[ng:]
            lo = tuple(acc - coef(t, g, s) * t_scr[s, g * n:g * n + hf, :] for g, acc in enumerate(lo))
            hi = tuple(acc - coef(t, g, s) * t_scr[s, g * n + hf:(g + 1) * n, :] for g, acc in enumerate(hi))
            return lo + hi

        lo = lax.fori_loop(0, hf, inner_lo, (zero,) * ng)
        accs = lax.fori_loop(hf, t, inner_hi, lo + ((jrow + hf == t).astype(F32),) * ng)
        for g in range(ng):
            t_scr[t, g * n:g * n + hf, :] = accs[g]
            t_scr[t, g * n + hf:(g + 1) * n, :] = accs[ng + g]
        return carry

    lax.fori_loop(hf, n, row_hi, 0)

    def store(t, carry):
        t_ref[pl.ds(t, npb, stride=n), :] = t_scr[t].T
        return carry

    lax.fori_loop(0, n, store, 0)


def _tri_solve(a):
    rows, width = a.shape
    n = RWKV_CHUNK
    npb = min(LANES, rows // n)
    return pl.pallas_call(
        _tri_solve_kernel,
        grid=(rows // (npb * n),),
        in_specs=[pl.BlockSpec((npb * n, width), lambda i: (i, 0))],
        out_specs=pl.BlockSpec((npb * n, width), lambda i: (i, 0)),
        out_shape=jax.ShapeDtypeStruct(a.shape, F32),
        scratch_shapes=[pltpu.VMEM((n, width, npb), F32)] * 2,
        compiler_params=_cparams(("parallel",)),
        name="tri_solve",
    )(a)


def _rwkv_c_kernel(t_ref, kd_ref, rd_ref, bg_ref, ge_ref, bonus_ref, gate_ref, rb_ref, m2_ref, rkv_ref, vk_ref,
                   lnw, lnb, y_ref, st_ref, s_scr, y_scr):
    c = pl.program_id(1)
    tb, cs, gw, hc = RWKV_BLOCK, RWKV_CHUNK, RW_GW, C_HEAD

    @pl.when(c == 0)
    def _():
        s_scr[...] = jnp.zeros_like(s_scr)

    def chunk(j, carry):
        r0 = pl.multiple_of(j * cs, cs)
        q0 = pl.multiple_of(j * gw, gw)
        for g in range(C_HEADS // RW_GH):
            gs = slice(g * gw, (g + 1) * gw)
            ls = slice(g * hc, (g + 1) * hc)
            rhs = jnp.concatenate([_to_stack(kd_ref[pl.ds(r0, cs), gs]), m2_ref[pl.ds(q0, gw), ls]], axis=1)
            wu = _dot(_bd_lanes(t_ref[pl.ds(q0, gw), ls]), rhs)
            rbwu = _dot(_bd_lanes(rb_ref[pl.ds(q0, gw), ls]), wu)
            rt = _to_stack(rd_ref[pl.ds(r0, cs), gs]) - rbwu[:, :hc]
            yc = rkv_ref[pl.ds(q0, gw), ls] - rbwu[:, hc:]
            wu_bd = jnp.concatenate([_bd_lanes(wu[:, :hc]), _bd_lanes(wu[:, hc:])], axis=1)
            nu = _dot_tn(wu_bd, _to_stack(bg_ref[pl.ds(r0, cs), gs]))
            ge_row = ge_ref[pl.ds(r0, 1), gs]
            ge_st = jnp.concatenate(
                [jnp.broadcast_to(ge_row[:, h * hc:(h + 1) * hc], (hc, hc)) for h in range(RW_GH)], axis=0)
            s = s_scr[g]
            s_bd = _bd_lanes(s)
            y_scr[pl.ds(r0, cs), gs] = _dot_nt(_to_all(rt), s_bd) + _to_all(yc)
            s_scr[g] = s * ge_st - _dot(s_bd, nu[:gw]) + (vk_ref[pl.ds(q0, gw), ls] - nu[gw:])
        return carry

    lax.fori_loop(0, tb // cs, chunk, 0)
    y_ref[...] = _rwkv_out(y_scr[...], bonus_ref[...], gate_ref[...], lnw[...], lnb[...]).astype(BF16)

    @pl.when(c == pl.num_programs(1) - 1)
    def _():
        st_ref[...] = s_scr[...]


def _rwkv_prompt(proj, rp):
    bsz, seq, _ = proj.shape
    tb, cs = RWKV_BLOCK, RWKV_CHUNK
    nblk = seq // tb

    def col(width, off):
        return pl.BlockSpec((None, tb, width), lambda b, c: (b, c, off // width))

    def pcol(width, off):
        return pl.BlockSpec((None, 8, width), lambda b, c: (b, jnp.maximum(c * (tb // 8) - 1, 0), off // width))

    ng = C_HEADS // RW_GH
    srows = tb // cs * RW_GW
    tok = pl.BlockSpec((None, tb, C_DIM), lambda b, c: (b, c, 0))
    stk = pl.BlockSpec((None, srows, ng * C_HEAD), lambda b, c: (b, c, 0))
    tok_shape = jax.ShapeDtypeStruct((bsz, seq, C_DIM), F32)
    stk_shape = jax.ShapeDtypeStruct((bsz, nblk * srows, ng * C_HEAD), F32)
    prm = [rp[n] for n in RW_NAMES]
    outs = pl.pallas_call(
        _rwkv_a_kernel,
        grid=(bsz, nblk),
        in_specs=[col(C_DIM, P_R), col(C_DIM, P_K), col(C_DIM, P_V), col(C_LR, P_LR),
                  pcol(C_DIM, P_R), pcol(C_DIM, P_K), pcol(C_DIM, P_V), pcol(C_LR, P_LR)]
                 + [_full(x.shape) for x in prm],
        out_specs=[tok] * 6 + [stk] * 5,
        out_shape=[tok_shape] * 6 + [stk_shape] * 5,
        scratch_shapes=[pltpu.VMEM((tb + 8, C_DIM), F32)] + [pltpu.VMEM((tb, C_DIM), F32)] * 4,
        compiler_params=_cparams(("parallel", "parallel")),
        name="rwkv_prep",
    )(proj, proj, proj, proj, proj, proj, proj, proj, *prm)
    kd, rd, bg, ge, bonus, gate, aab, rb, m2, rkv, vk = outs

    tmat = _tri_solve(aab.reshape(-1, ng * C_HEAD)).reshape(stk_shape.shape)

    y, st = pl.pallas_call(
        _rwkv_c_kernel,
        grid=(bsz, nblk),
        in_specs=[stk] + [tok] * 6 + [stk] * 4 + [_full((1, C_DIM))] * 2,
        out_specs=[tok, pl.BlockSpec((None, ng, RW_GW, C_HEAD), lambda b, c: (b, 0, 0, 0))],
        out_shape=[jax.ShapeDtypeStruct((bsz, seq, C_DIM), BF16),
                   jax.ShapeDtypeStruct((bsz, ng, RW_GW, C_HEAD), F32)],
        scratch_shapes=[pltpu.VMEM((ng, RW_GW, C_HEAD), F32), pltpu.VMEM((tb, C_DIM), F32)],
        compiler_params=_cparams(("parallel", "arbitrary")),
        name="rwkv_scan",
    )(tmat, kd, rd, bg, ge, bonus, gate, rb, m2, rkv, vk, rp["lnx_w"], rp["lnx_b"])
    return y, st.reshape(bsz, C_HEADS, C_HEAD, C_HEAD)


def _rwkv_step_kernel(r_ref, k_ref, v_ref, lr_ref, sh_ref, *rest):
    prm = tuple(x[...] for x in rest[:len(RW_NAMES)])
    lnw, lnb, s_ref, _, y_ref, so_ref = rest[len(RW_NAMES):]
    bb = STEP_BB
    sh = sh_ref[...]
    r, k2, v, kk, beta, ld, g, bonus = _rwkv_prep(
        r_ref[...], k_ref[...], v_ref[...], lr_ref[...],
        sh[:, :C_DIM], sh[:, C_DIM:2 * C_DIM], sh[:, 2 * C_DIM:3 * C_DIM], sh[:, 3 * C_DIM:], prm)
    w = jnp.exp(ld)
    v_t = v.T
    lane = _iota((C_DIM, LANES), 1)
    heads = [[x[:, h * C_HEAD:(h + 1) * C_HEAD] for x in (kk, w, beta, k2, r)] for h in range(C_HEADS)]

    def rows(i, b):
        return jnp.concatenate(
            [jnp.broadcast_to(heads[h][i][b:b + 1, :], (C_HEAD, C_HEAD)) for h in range(C_HEADS)], axis=0)

    y_t = jnp.zeros((C_DIM, LANES), F32)
    for b in range(bb):
        s = s_ref[b]
        sa = -jnp.sum(s * rows(0, b), axis=-1, keepdims=True)
        sn = s * rows(1, b) + sa * rows(2, b) + v_t[:, b:b + 1] * rows(3, b)
        so_ref[b] = sn
        y_t = jnp.where(lane == b, jnp.sum(sn * rows(4, b), axis=-1, keepdims=True), y_t)
    y = y_t.T[0:bb, :]
    y_ref[...] = _rwkv_out(y, bonus, g, lnw[...], lnb[...]).astype(BF16)


def _rwkv_step(proj, shift, s_all, s_new, layer, rp):
    bsz = proj.shape[0]
    bb = STEP_BB
    sflat = s_all.reshape(s_all.shape[0], bsz, C_DIM, C_HEAD)

    def col(width, off):
        return pl.BlockSpec((bb, width), lambda i: (i, off // width))

    prm = [rp[n] for n in RW_NAMES]
    y, sn = pl.pallas_call(
        _rwkv_step_kernel,
        grid=(bsz // bb,),
        in_specs=[col(C_DIM, P_R), col(C_DIM, P_K), col(C_DIM, P_V), col(C_LR, P_LR),
                  pl.BlockSpec((bb, C_COLS), lambda i: (i, 0))]
                 + [_full(x.shape) for x in prm] + [_full((1, C_DIM))] * 2
                 + [pl.BlockSpec((None, bb, C_DIM, C_HEAD), lambda i: (layer, i, 0, 0)),
                    pl.BlockSpec(memory_space=pl.ANY)],
        out_specs=[pl.BlockSpec((bb, C_DIM), lambda i: (i, 0)),
                   pl.BlockSpec((None, bb, C_DIM, C_HEAD), lambda i: (layer, i, 0, 0))],
        out_shape=[jax.ShapeDtypeStruct((bsz, C_DIM), BF16),
                   jax.ShapeDtypeStruct(sflat.shape, F32)],
        input_output_aliases={len(prm) + 8: 1},
        compiler_params=_cparams(("parallel",)),
        name="rwkv_step",
    )(proj, proj, proj, proj, shift, *prm, rp["lnx_w"], rp["lnx_b"], sflat, s_new.reshape(sflat.shape))
    return y, sn.reshape(s_all.shape)


def _layout_w_in_kernel(w_ref, o_ref):
    o = A_DIM + CONV_CH
    rows = w_ref.shape[0]
    o_ref[:, :o] = w_ref[:, :o].astype(BF16)
    o_ref[:, o:P_DT] = w_ref[:, o + A_HEADS:].astype(BF16)
    dt_tile = jnp.where(_iota((rows, LANES), 1) < A_HEADS, w_ref[:, o:o + LANES], 0.0)
    o_ref[:, P_DT:P_DT + LANES] = dt_tile.astype(BF16)
    o_ref[:, P_DT + LANES:] = jnp.zeros((rows, P_COLS - P_DT - LANES), BF16)


def _layout_w_in(w_all, layer):
    k = w_all.shape[1]
    tr = 256
    return pl.pallas_call(
        _layout_w_in_kernel,
        grid=(k // tr,),
        in_specs=[pl.BlockSpec((None, tr, IN_COLS), lambda i: (layer, i, 0))],
        out_specs=pl.BlockSpec((tr, P_COLS), lambda i: (i, 0)),
        out_shape=jax.ShapeDtypeStruct((k, P_COLS), BF16),
        compiler_params=_cparams(("parallel",)),
        name="layout_w_in",
    )(w_all)


def _xbc_cols(proj):
    return proj[..., P_X:P_X + CONV_CH]


def _moe_route(logits, n_tok):
    top_logit, top_idx = lax.top_k(logits, TOP_K)
    gate = jax.nn.softmax(top_logit, axis=-1)
    n_assign = n_tok * TOP_K
    flat_e = top_idx.reshape(-1).astype(jnp.int32)
    order = jnp.argsort(flat_e)
    onehot = jax.nn.one_hot(flat_e, N_EXPERTS, dtype=jnp.int32)
    counts = jnp.sum(onehot, axis=0)
    padded = (counts + MOE_BM - 1) // MOE_BM * MOE_BM
    pad_end = jnp.cumsum(padded)
    pad_start = pad_end - padded
    start = jnp.cumsum(counts) - counts
    rank = jnp.sum((jnp.cumsum(onehot, axis=0) - onehot) * onehot, axis=1)
    slot_of = (pad_start[flat_e] + rank).reshape(n_tok, TOP_K)
    n_blocks = -(-n_assign // MOE_BM) + N_EXPERTS
    n_slots = n_blocks * MOE_BM
    block_start = jnp.arange(n_blocks, dtype=jnp.int32) * MOE_BM
    nvalid = (pad_end[-1] // MOE_BM).astype(jnp.int32)
    bexp = jnp.minimum(jnp.searchsorted(pad_end, block_start, side="right"), N_EXPERTS - 1).astype(jnp.int32)
    slot_e = jnp.repeat(bexp, MOE_BM)
    within = jnp.arange(n_slots, dtype=jnp.int32) - pad_start[slot_e]
    src = order[jnp.clip(start[slot_e] + within, 0, n_assign - 1)] // TOP_K
    slot_tok = jnp.where(within < counts[slot_e], src, n_tok)
    left = counts[bexp] - (block_start - pad_start[bexp])
    nsub = jnp.clip((left + MOE_SUB - 1) // MOE_SUB, 0, MOE_BM // MOE_SUB).astype(jnp.int32)
    last = bexp[jnp.maximum(nvalid - 1, 0)]
    bexp = jnp.where(jnp.arange(n_blocks) < nvalid, bexp, last)
    first = jnp.concatenate([jnp.ones((1,), jnp.int32), (bexp[1:] != bexp[:-1]).astype(jnp.int32)])
    return gate, slot_tok, slot_of, bexp, first, nsub


def _moe(xp, xs, h, logits, wg, wu, wd, g, final):
    tp = xp.shape[0]
    n_tok = h.shape[0]
    gate, slot_tok, slot_of, bexp, first, nsub = _moe_route(logits[:, :N_EXPERTS], n_tok)
    xb = h[jnp.where(slot_tok == n_tok, 0, slot_tok)]
    yb = _moe_experts(xb, bexp, first, nsub, wg, wu, wd)
    return (_moe_combine(xp, yb, slot_of[:tp], gate[:tp], g, final),
            _moe_combine(xs, yb, slot_of[tp:], gate[tp:], g, final))


def kernel(x_prompt, x_sample, state_ssm, state_conv, state_hgrn, state_rwkv, state_shift, norm1_g, w_in, conv_w, conv_b, dt_bias, a_log, d_skip, ssm_norm_g, lb_logits, hgrn_norm_g, shift_mu, w0, w2, a0, a2, g2, k_k, k_a, r_k, lnx_w, lnx_b, w_out, norm2_g, ffn_w_gate, ffn_w_up, ffn_w_down, router_w, exp_w_gate, exp_w_up, exp_w_down, final_norm_g):
    depth = w_in.shape[0]
    bp, seq, _ = x_prompt.shape
    bs = x_sample.shape[0]
    tp = bp * seq
    lb_soft = jax.nn.softmax(lb_logits.astype(F32), axis=0)
    lower_bounds = jnp.clip(jnp.cumsum(lb_soft, axis=0) - lb_soft[0:1], 0.0, 1.0)

    xp = x_prompt.reshape(tp, D_MODEL)
    xs = x_sample.reshape(bs, D_MODEL)
    outs = {k: [] for k in ("p_ssm", "p_conv", "p_hgrn", "p_rwkv", "p_shift", "s_conv", "s_shift")}
    ssm_s, hgrn_s, rwkv_s = (jnp.zeros(s.shape, F32) for s in (state_ssm, state_hgrn, state_rwkv))
    normed = False
    for layer in range(depth):
        p = dict(conv_w=conv_w[layer], conv_b=conv_b[layer], dt_bias=dt_bias[layer], a_log=a_log[layer],
                 d_skip=d_skip[layer], ssm_norm_g=ssm_norm_g[layer], shift_mu=shift_mu[layer],
                 w0=w0[layer], w2=w2[layer], a0=a0[layer], a2=a2[layer], g2=g2[layer], k_k=k_k[layer],
                 k_a=k_a[layer], r_k=r_k[layer].reshape(-1), lnx_w=lnx_w[layer], lnx_b=lnx_b[layer])
        sp = _ssd_params(p)
        rp = _rwkv_params(p)
        lb = lower_bounds[layer]
        lbp = jnp.stack([1.0 - lb, jnp.log(lb), jnp.log1p(-lb)])
        hng = hgrn_norm_g[layer].reshape(1, B_HEAD)
        w_in_l = _layout_w_in(w_in, layer)
        w_out_l = w_out[layer].astype(BF16)
        j = layer // 2
        routed = layer % 2 == 1
        if routed:
            rw_f = jnp.pad(router_w[j], ((0, 0), (0, LANES - N_EXPERTS)))
            rw_hi = rw_f.astype(BF16)
            rw = jnp.concatenate([rw_hi, (rw_f - rw_hi.astype(F32)).astype(BF16)], axis=1)
            h_all = jnp.zeros((tp + bs, D_MODEL), BF16)
            l_all = jnp.zeros((tp + bs, LANES), F32)

        proj = _norm_matmul(xp, norm1_g[layer], w_in_l, 256).reshape(bp, seq, P_COLS)
        ya, ssm_p = _ssd_prompt(proj, sp)
        yb, hgrn_p = _hgrn_prompt(proj, lbp, hng)
        yc, rwkv_p = _rwkv_prompt(proj, rp)
        xp, hp, l_all = _out_proj(ya.reshape(tp, A_DIM), yb.reshape(tp, B_DIM), yc.reshape(tp, C_DIM),
                                  w_out_l, xp, norm2_g[layer], 256, (rw, h_all, l_all, 0) if routed else None)
        outs["p_ssm"].append(ssm_p)
        outs["p_conv"].append(_xbc_cols(proj[:, seq - (CONV_W - 1):, :]))
        outs["p_hgrn"].append(hgrn_p)
        outs["p_rwkv"].append(rwkv_p)
        outs["p_shift"].append(proj[:, seq - 1, P_R:P_R + C_COLS])

        projs = _norm_matmul(xs, norm1_g[layer], w_in_l, 128)
        ya, ssm_s = _ssd_step(projs, state_conv[layer], state_ssm, ssm_s, layer, p, sp)
        yb, hgrn_s = _hgrn_step(projs, state_hgrn, hgrn_s, layer, lbp, hng)
        yc, rwkv_s = _rwkv_step(projs, state_shift[layer], state_rwkv, rwkv_s, layer, rp)
        xs, hs, l_all = _out_proj(ya, yb, yc, w_out_l, xs, norm2_g[layer], 128,
                                  (rw, hp, l_all, tp) if routed else None)
        outs["s_conv"].append(jnp.concatenate([state_conv[layer][:, 1:, :], _xbc_cols(projs)[:, None, :]], axis=1))
        outs["s_shift"].append(projs[:, P_R:P_R + C_COLS])

        if not routed:
            xp = _ffn(xp, hp, ffn_w_gate[j], ffn_w_up[j], ffn_w_down[j])
            xs = _ffn(xs, hs, ffn_w_gate[j], ffn_w_up[j], ffn_w_down[j])
        else:
            normed = layer == depth - 1
            xp, xs = _moe(xp, xs, hs, l_all, exp_w_gate[j], exp_w_up[j], exp_w_down[j], final_norm_g, normed)

    if not normed:
        xp, xs = _final_norm(xp, final_norm_g, 512), _final_norm(xs, final_norm_g, 128)
    y_prompt = xp.reshape(bp, seq, D_MODEL)
    y_sample = xs.reshape(bs, 1, D_MODEL)
    st = {k: jnp.stack(v) for k, v in outs.items()}
    return (y_prompt, y_sample, st["p_ssm"], st["p_conv"], st["p_hgrn"], st["p_rwkv"], st["p_shift"],
            ssm_s, st["s_conv"], hgrn_s, rwkv_s, st["s_shift"])
```

```python
import functools

import jax
import jax.numpy as jnp
from jax import lax
from jax.experimental import pallas as pl
from jax.experimental.pallas import tpu as pltpu

F32 = jnp.float32
BF16 = jnp.bfloat16
HI = lax.Precision.HIGHEST

D_MODEL = 2048
A_DIM, A_HEAD, A_HEADS, A_GROUPS, A_STATE = 1024, 64, 16, 2, 128
A_GN = A_GROUPS * A_STATE
CONV_W = 4
CONV_CH = A_DIM + 2 * A_GN
B_DIM, B_HEAD, B_HEADS = 512, 128, 4
C_DIM, C_HEAD, C_HEADS = 512, 64, 8
C_LR_W, C_LR_A, C_LR_G = 64, 64, 128
C_LR = C_LR_W + C_LR_A + C_LR_G
C_COLS = 3 * C_DIM + C_LR
IN_COLS = A_DIM + CONV_CH + A_HEADS + 4 * B_DIM + C_COLS
D_FF, N_EXPERTS, TOP_K, E_FF = 5504, 8, 2, 7168
NORM_EPS, GN_EPS, L2_EPS = 1e-6, 64e-5, 1e-12

P_Z, P_X, P_B, P_C = 0, 1024, 2048, 2304
P_Q, P_F, P_I, P_G = 2560, 3072, 3584, 4096
P_R, P_K, P_V, P_LR, P_DT, P_COLS = 4608, 5120, 5632, 6144, 6400, 6656
LANES = 128

SSD_CHUNK = 256
HGRN_BLOCK, HGRN_CHUNK = 256, 16
RWKV_BLOCK, RWKV_CHUNK = 256, 64
STEP_BB = 8

FFN_TF, FFN_TN, FFN_TM_UP, FFN_TM_DOWN = 512, 1024, 1024, 256
MOE_BM, MOE_SUB, MOE_TF, MOE_TN, MOE_KP = 512, 256, 1024, 1024, 2

NT_DIMS = (((1,), (1,)), ((), ()))
TN_DIMS = (((0,), (0,)), ((), ()))


def _cparams(sem, vmem_mb=48):
    return pltpu.CompilerParams(dimension_semantics=sem, vmem_limit_bytes=vmem_mb * 1024 * 1024)


def _dot(a, b):
    return jnp.dot(a.astype(BF16), b.astype(BF16), preferred_element_type=F32)


def _dot_nt(a, b):
    return lax.dot_general(a.astype(BF16), b.astype(BF16), NT_DIMS, preferred_element_type=F32)


def _dot_tn(a, b):
    return lax.dot_general(a.astype(BF16), b.astype(BF16), TN_DIMS, preferred_element_type=F32)


def _dot_hi(a, b):
    return jnp.dot(a, b, precision=HI, preferred_element_type=F32)


def _dot_nt_hi(a, b):
    return lax.dot_general(a, b, NT_DIMS, precision=HI, preferred_element_type=F32)


def _silu(x):
    return x * jax.nn.sigmoid(x)


def _softplus(x):
    return jnp.maximum(x, 0.0) + jnp.log1p(jnp.exp(-jnp.abs(x)))


def _iota(shape, dim):
    return lax.broadcasted_iota(jnp.int32, shape, dim)


def _rms(x, g):
    return x * lax.rsqrt(jnp.mean(x * x, axis=-1, keepdims=True) + NORM_EPS) * g


def _full(shape):
    nd = len(shape)
    return pl.BlockSpec(shape, lambda *_: (0,) * nd)


def _norm_matmul_kernel(x_ref, g_ref, w_ref, o_ref):
    h = _rms(x_ref[...], g_ref[...]).astype(BF16)
    o_ref[...] = jnp.dot(h, w_ref[...], preferred_element_type=F32)


def _norm_matmul(x, g, w, tm):
    m, k = x.shape
    n = w.shape[1]
    return pl.pallas_call(
        _norm_matmul_kernel,
        grid=(m // tm,),
        in_specs=[pl.BlockSpec((tm, k), lambda i: (i, 0)),
                  pl.BlockSpec((1, k), lambda i: (0, 0)),
                  pl.BlockSpec((k, n), lambda i: (0, 0), pipeline_mode=pl.Buffered(1))],
        out_specs=pl.BlockSpec((tm, n), lambda i: (i, 0)),
        out_shape=jax.ShapeDtypeStruct((m, n), F32),
        compiler_params=_cparams(("parallel",), 56),
        name="norm_matmul",
    )(x, g.reshape(1, k), w)


def _out_proj_kernel(routed, ya_ref, yb_ref, yc_ref, w_ref, r_ref, g_ref, *rest):
    x = (r_ref[...]
         + jnp.dot(ya_ref[...], w_ref[0:A_DIM, :], preferred_element_type=F32)
         + jnp.dot(yb_ref[...], w_ref[A_DIM:A_DIM + B_DIM, :], preferred_element_type=F32)
         + jnp.dot(yc_ref[...], w_ref[A_DIM + B_DIM:, :], preferred_element_type=F32))
    h = _rms(x, g_ref[...])
    h_hi = h.astype(BF16)
    if routed:
        rw_ref, _, _, x_ref, h_ref, l_ref = rest
        h_lo = (h - h_hi.astype(F32)).astype(BF16)
        rw = rw_ref[...]
        l2 = jnp.dot(h_hi, rw, preferred_element_type=F32)
        l_ref[...] = (l2[:, :LANES] + l2[:, LANES:]
                      + jnp.dot(h_lo, rw[:, :LANES], preferred_element_type=F32))
    else:
        x_ref, h_ref = rest
    x_ref[...] = x
    h_ref[...] = h_hi


def _out_proj(ya, yb, yc, w, res, g, tm, routed=None):
    m = ya.shape[0]
    n = w.shape[1]
    row = lambda width: pl.BlockSpec((tm, width), lambda i: (i, 0))
    in_specs = [row(A_DIM), row(B_DIM), row(C_DIM), _full(w.shape), row(n), _full((1, n))]
    args = [ya, yb, yc, w, res, g.reshape(1, n)]
    aliases = {}
    if routed is None:
        out_specs = [row(n), row(n)]
        out_shape = [jax.ShapeDtypeStruct((m, n), F32), jax.ShapeDtypeStruct((m, n), BF16)]
    else:
        rw, h_buf, l_buf, row0 = routed
        blk0 = row0 // tm
        off = lambda width: pl.BlockSpec((tm, width), lambda i: (i + blk0, 0))
        in_specs += [_full(rw.shape), pl.BlockSpec(memory_space=pl.ANY), pl.BlockSpec(memory_space=pl.ANY)]
        args += [rw, h_buf, l_buf]
        out_specs = [row(n), off(n), off(LANES)]
        out_shape = [jax.ShapeDtypeStruct((m, n), F32), jax.ShapeDtypeStruct(h_buf.shape, BF16),
                     jax.ShapeDtypeStruct(l_buf.shape, F32)]
        aliases = {len(args) - 2: 1, len(args) - 1: 2}
    outs = pl.pallas_call(
        functools.partial(_out_proj_kernel, routed is not None),
        grid=(m // tm,),
        in_specs=in_specs, out_specs=out_specs, out_shape=out_shape,
        input_output_aliases=aliases,
        compiler_params=_cparams(("parallel",)),
        name="out_proj",
    )(*args)
    return outs if routed is not None else (*outs, None)


def _ffn_up_kernel(h_ref, wg_ref, wu_ref, o_ref, wg_s, wu_s):
    @pl.when(pl.program_id(1) == 0)
    def _():
        wg_s[...] = wg_ref[...].astype(BF16)
        wu_s[...] = wu_ref[...].astype(BF16)

    h = h_ref[...]
    a = jnp.dot(h, wg_s[...], preferred_element_type=F32)
    u = jnp.dot(h, wu_s[...], preferred_element_type=F32)
    o_ref[...] = (_silu(a) * u).astype(BF16)


def _ffn_down_kernel(h_ref, wd_ref, x_ref, o_ref, wd_s):
    @pl.when(pl.program_id(1) == 0)
    def _():
        wd_s[...] = wd_ref[...].astype(BF16)

    o_ref[...] = x_ref[...] + jnp.dot(h_ref[...], wd_s[...], preferred_element_type=F32)


def _ffn(x, h, wg, wu, wd):
    m, k = h.shape
    ff = wg.shape[1]
    tf, tn = FFN_TF, FFN_TN
    tu = min(m, FFN_TM_UP)
    tm = min(m, FFN_TM_DOWN)
    act = pl.pallas_call(
        _ffn_up_kernel,
        grid=(pl.cdiv(ff, tf), m // tu),
        in_specs=[pl.BlockSpec((tu, k), lambda f, i: (i, 0)),
                  pl.BlockSpec((k, tf), lambda f, i: (0, f)),
                  pl.BlockSpec((k, tf), lambda f, i: (0, f))],
        out_specs=pl.BlockSpec((tu, tf), lambda f, i: (i, f)),
        out_shape=jax.ShapeDtypeStruct((m, ff), BF16),
        scratch_shapes=[pltpu.VMEM((k, tf), BF16)] * 2,
        compiler_params=_cparams(("arbitrary", "arbitrary")),
        name="ffn_up",
    )(h, wg, wu)
    return pl.pallas_call(
        _ffn_down_kernel,
        grid=(k // tn, m // tm),
        in_specs=[pl.BlockSpec((tm, ff), lambda n, i: (i, 0)),
                  pl.BlockSpec((ff, tn), lambda n, i: (0, n), pipeline_mode=pl.Buffered(1)),
                  pl.BlockSpec((tm, tn), lambda n, i: (i, n))],
        out_specs=pl.BlockSpec((tm, tn), lambda n, i: (i, n)),
        out_shape=jax.ShapeDtypeStruct((m, k), F32),
        scratch_shapes=[pltpu.VMEM((ff, tn), BF16)],
        compiler_params=_cparams(("arbitrary", "arbitrary"), 56),
        name="ffn_down",
    )(act, wd, x)


def _moe_rows(nsub, o_ref, compute):
    bm = o_ref.shape[0]
    for ns in range(bm // MOE_SUB + 1):
        @pl.when(nsub == ns)
        def _(rows=ns * MOE_SUB):
            if rows:
                o_ref[0:rows, :] = compute(rows).astype(o_ref.dtype)
            if rows < bm:
                o_ref[rows:, :] = jnp.zeros((bm - rows, o_ref.shape[1]), o_ref.dtype)


def _weight_stream(sched, w_hbms, wbuf, sem, dsts):
    bexp_ref, first_ref, run_ref, nxt_ref, lastrun_ref, nruns_ref = sched
    col, b = pl.program_id(0), pl.program_id(1)
    ncol = pl.num_programs(0)
    width = wbuf.shape[-1]

    def copies(e, c, slot):
        cols = pl.ds(pl.multiple_of(c * width, width), width)
        return [pltpu.make_async_copy(w.at[e, :, cols], wbuf.at[slot, i], sem.at[slot])
                for i, w in enumerate(w_hbms)]

    @pl.when(first_ref[b] == 1)
    def _():
        slot = lax.rem(col * nruns_ref[0] + run_ref[b], 2)
        last = lastrun_ref[b]

        @pl.when((col == 0) & (b == 0))
        def _():
            for cp in copies(bexp_ref[0], 0, 0):
                cp.start()

        @pl.when((last == 0) | (col < ncol - 1))
        def _():
            for cp in copies(nxt_ref[b], col + last, 1 - slot):
                cp.start(priority=1)

        for cp in copies(bexp_ref[b], col, slot):
            cp.wait()
        for i, dst in enumerate(dsts):
            dst[...] = wbuf[slot, i].astype(BF16)


def _moe_up_kernel(*refs):
    sched, (nsub_ref, _, x_ref, wg_hbm, wu_hbm, o_ref, wbuf, sem, wg_s, wu_s) = refs[:6], refs[6:]
    _weight_stream(sched, (wg_hbm, wu_hbm), wbuf, sem, (wg_s, wu_s))

    def compute(rows):
        x = x_ref[0:rows, :]
        a = jnp.dot(x, wg_s[...], preferred_element_type=F32)
        u = jnp.dot(x, wu_s[...], preferred_element_type=F32)
        return _silu(a) * u

    _moe_rows(nsub_ref[pl.program_id(1)], o_ref, compute)


def _moe_down_kernel(*refs):
    sched, (nsub_ref, _, h_ref, wd_hbm, o_ref, wbuf, sem, wd_s) = refs[:6], refs[6:]
    bexp_ref, first_ref, _, nxt_ref, lastrun_ref, _ = sched
    col, b, kk = pl.program_id(0), pl.program_id(1), pl.program_id(2)
    ncol = pl.num_programs(0)
    parts, kp, width = wbuf.shape

    def copies(e, c):
        cols = pl.ds(pl.multiple_of(c * width, width), width)
        return [pltpu.make_async_copy(wd_hbm.at[e, pl.ds(i * kp, kp), cols], wbuf.at[i], sem.at[i])
                for i in range(parts)]

    @pl.when((first_ref[b] == 1) & (kk == 0))
    def _():
        last = lastrun_ref[b]

        @pl.when((col == 0) & (b == 0))
        def _():
            for cp in copies(bexp_ref[0], 0):
                cp.start()

        for cp in copies(bexp_ref[b], col):
            cp.wait()
        rc = 512

        def cast(r, carry):
            rows = pl.ds(pl.multiple_of(r * rc, rc), rc)
            for i in range(parts):
                wd_s[i, rows, :] = wbuf[i, rows, :].astype(BF16)
            return carry

        lax.fori_loop(0, kp // rc, cast, 0)

        @pl.when((last == 0) | (col < ncol - 1))
        def _():
            for cp in copies(nxt_ref[b], col + last):
                cp.start(priority=1)

    valid = nsub_ref[b] > 0
    for i in range(parts):
        @pl.when(valid & (kk == i))
        def _(i=i):
            acc = jnp.dot(h_ref[...], wd_s[i], preferred_element_type=F32)
            if i == 0:
                o_ref[...] = acc
            else:
                o_ref[...] += acc

    @pl.when(jnp.logical_not(valid) & (kk == 0))
    def _():
        o_ref[...] = jnp.zeros_like(o_ref)


def _run_schedule(bexp, first):
    nb = bexp.shape[0]
    ids = jnp.arange(nb, dtype=jnp.int32)
    run = jnp.cumsum(first).astype(jnp.int32) - 1
    nruns = run[-1] + 1
    starts = jnp.where(first == 1, ids, nb)
    nxt_start = jnp.concatenate([lax.cummin(starts[::-1])[::-1][1:], jnp.full((1,), nb, jnp.int32)])
    nxt = jnp.where(nxt_start < nb, bexp[jnp.minimum(nxt_start, nb - 1)], bexp[0]).astype(jnp.int32)
    lastrun = (run == nruns - 1).astype(jnp.int32)
    return bexp, first, run, nxt, lastrun, nruns.reshape(1)


def _moe_experts(xb, bexp, first, nsub, wg, wu, wd):
    n_slots, k = xb.shape
    ff = wg.shape[2]
    nb = n_slots // MOE_BM
    tf, tn = MOE_TF, MOE_TN
    hbm = pl.BlockSpec(memory_space=pl.ANY)

    def src_blocks(ns):
        return lax.cummax(jnp.where(ns > 0, jnp.arange(ns.shape[0], dtype=jnp.int32), 0))

    act = pl.pallas_call(
        _moe_up_kernel,
        grid_spec=pltpu.PrefetchScalarGridSpec(
            num_scalar_prefetch=8,
            grid=(ff // tf, nb),
            in_specs=[pl.BlockSpec((MOE_BM, k), lambda f, b, *s: (s[7][b], 0)), hbm, hbm],
            out_specs=pl.BlockSpec((MOE_BM, tf), lambda f, b, *s: (b, f)),
            scratch_shapes=[pltpu.VMEM((2, 2, k, tf), F32), pltpu.SemaphoreType.DMA((2,)),
                            pltpu.VMEM((k, tf), BF16), pltpu.VMEM((k, tf), BF16)],
        ),
        out_shape=jax.ShapeDtypeStruct((n_slots, ff), BF16),
        compiler_params=_cparams(("arbitrary", "arbitrary"), 56),
        name="moe_up",
    )(*_run_schedule(bexp, first), nsub, src_blocks(nsub), xb, wg, wu)
    per = MOE_BM // MOE_SUB
    bexp_d = jnp.repeat(bexp, per)
    first_d = jnp.concatenate([jnp.ones((1,), jnp.int32), (bexp_d[1:] != bexp_d[:-1]).astype(jnp.int32)])
    nsub_d = jnp.clip(jnp.repeat(nsub, per) - jnp.tile(jnp.arange(per, dtype=jnp.int32), nb), 0, 1)
    return pl.pallas_call(
        _moe_down_kernel,
        grid_spec=pltpu.PrefetchScalarGridSpec(
            num_scalar_prefetch=8,
            grid=(k // tn, nb * per, MOE_KP),
            in_specs=[pl.BlockSpec((MOE_SUB, ff // MOE_KP), lambda n, b, kk, *s: (s[7][b], kk)), hbm],
            out_specs=pl.BlockSpec((MOE_SUB, tn), lambda n, b, kk, *s: (b, n)),
            scratch_shapes=[pltpu.VMEM((MOE_KP, ff // MOE_KP, tn), F32), pltpu.SemaphoreType.DMA((MOE_KP,)),
                            pltpu.VMEM((MOE_KP, ff // MOE_KP, tn), BF16)],
        ),
        out_shape=jax.ShapeDtypeStruct((n_slots, k), F32),
        compiler_params=_cparams(("arbitrary", "arbitrary", "arbitrary"), 56),
        name="moe_down",
    )(*_run_schedule(bexp_d, first_d), nsub_d, src_blocks(nsub_d), act, wd)


def _final_norm_kernel(x_ref, g_ref, o_ref):
    o_ref[...] = _rms(x_ref[...], g_ref[...])


def _final_norm(x, g, tm):
    m, k = x.shape
    return pl.pallas_call(
        _final_norm_kernel,
        grid=(m // tm,),
        in_specs=[pl.BlockSpec((tm, k), lambda i: (i, 0)),
                  pl.BlockSpec((1, k), lambda i: (0, 0))],
        out_specs=pl.BlockSpec((tm, k), lambda i: (i, 0)),
        out_shape=jax.ShapeDtypeStruct((m, k), F32),
        compiler_params=_cparams(("parallel",)),
        name="final_norm",
    )(x, g.reshape(1, k))


def _row_copy(src_hbm, row, dst, r, sem):
    return pltpu.make_async_copy(src_hbm.at[pl.ds(row, 1), :], dst.at[pl.ds(r, 1), :], sem)


def _moe_combine_kernel(final, s0_ref, s1_ref, x_ref, gate_ref, g_ref, yb_hbm, o_ref, buf, sem):
    rows = x_ref.shape[0]
    base = pl.program_id(0) * rows

    def issue(r, carry):
        _row_copy(yb_hbm, s0_ref[base + r], buf.at[0], r, sem).start(priority=0)
        _row_copy(yb_hbm, s1_ref[base + r], buf.at[1], r, sem).start(priority=1)
        return carry

    lax.fori_loop(0, rows, issue, 0)
    pltpu.make_async_copy(buf, buf, sem).wait()
    gate = gate_ref[...]
    y = x_ref[...] + gate[:, 0:1] * buf[0] + gate[:, 1:2] * buf[1]
    o_ref[...] = _rms(y, g_ref[...]) if final else y


def _moe_combine(x, yb, slot_of, gate, g, final):
    m, k = x.shape
    rows = min(m, 256)
    gate_p = jnp.pad(gate, ((0, 0), (0, LANES - TOP_K)))
    return pl.pallas_call(
        functools.partial(_moe_combine_kernel, final),
        grid_spec=pltpu.PrefetchScalarGridSpec(
            num_scalar_prefetch=2,
            grid=(m // rows,),
            in_specs=[pl.BlockSpec((rows, k), lambda i, s0, s1: (i, 0)),
                      pl.BlockSpec((rows, LANES), lambda i, s0, s1: (i, 0)),
                      pl.BlockSpec((1, k), lambda i, s0, s1: (0, 0)),
                      pl.BlockSpec(memory_space=pl.ANY)],
            out_specs=pl.BlockSpec((rows, k), lambda i, s0, s1: (i, 0)),
            scratch_shapes=[pltpu.VMEM((TOP_K, rows, k), F32), pltpu.SemaphoreType.DMA],
        ),
        out_shape=jax.ShapeDtypeStruct((m, k), F32),
        compiler_params=_cparams(("arbitrary",)),
        name="moe_combine",
    )(slot_of[:, 0], slot_of[:, 1], x, gate_p, g.reshape(1, k), yb)


def _ssd_gate_norm(y, z, ng):
    y = y * _silu(z)
    gw = A_DIM // A_GROUPS
    outs = []
    for g in range(A_GROUPS):
        yg = y[:, g * gw:(g + 1) * gw]
        outs.append(yg * lax.rsqrt(jnp.mean(yg * yg, axis=-1, keepdims=True) + NORM_EPS))
    return jnp.concatenate(outs, axis=-1) * ng


def _ssd_prompt_kernel(z_ref, x_ref, b_ref, c_ref, dt_ref, cwx, cwb, cwc, cbx, cbb, cbc,
                       dtb, aneg, dsk, ng, y_ref, st_ref, h_scr, xpx, xpb, xpc, yacc):
    c = pl.program_id(1)
    ch = SSD_CHUNK

    @pl.when(c == 0)
    def _():
        h_scr[...] = jnp.zeros_like(h_scr)
        xpx[0:8, :] = jnp.zeros((8, A_DIM), F32)
        xpb[0:8, :] = jnp.zeros((8, A_STATE * A_GROUPS), F32)
        xpc[0:8, :] = jnp.zeros((8, A_STATE * A_GROUPS), F32)

    xpx[8:8 + ch, :] = x_ref[...]
    xpb[8:8 + ch, :] = b_ref[...]
    xpc[8:8 + ch, :] = c_ref[...]

    def conv(xp, cw, cb):
        acc = cb[...] + xp[pl.ds(8 - (CONV_W - 1), ch), :] * cw[0:1, :]
        for tap in range(1, CONV_W):
            acc = acc + xp[pl.ds(8 - (CONV_W - 1) + tap, ch), :] * cw[tap:tap + 1, :]
        return _silu(acc)

    ux = conv(xpx, cwx, cbx)
    ub = conv(xpb, cwb, cbb)
    uc = conv(xpc, cwc, cbc)
    xpx[0:8, :] = xpx[ch:ch + 8, :]
    xpb[0:8, :] = xpb[ch:ch + 8, :]
    xpc[0:8, :] = xpc[ch:ch + 8, :]

    dt = _softplus(dt_ref[...] + dtb[...])
    la = dt * aneg[...]
    tri = (_iota((ch, ch), 1) <= _iota((ch, ch), 0)).astype(F32)
    cum = _dot_hi(tri, la)
    eye = (_iota((A_HEADS, LANES), 0) == _iota((A_HEADS, LANES), 1)).astype(F32)
    cum_r = _dot_nt_hi(eye, cum)
    dt_r = _dot_nt_hi(eye, dt)
    cl = cum[ch - 1:ch, :]
    tail = jnp.exp(cl - cum) * dt
    ecum = jnp.exp(cum)
    ecl = jnp.exp(cl)
    causal = _iota((ch, ch), 1) <= _iota((ch, ch), 0)
    rep = A_HEADS // A_GROUPS

    for g in range(A_GROUPS):
        bg = ub[:, g * A_STATE:(g + 1) * A_STATE].astype(BF16)
        cg = uc[:, g * A_STATE:(g + 1) * A_STATE].astype(BF16)
        cb_g = _dot_nt(cg, bg)
        for h in range(g * rep, (g + 1) * rep):
            hs = slice(h * A_HEAD, (h + 1) * A_HEAD)
            seg = cum[:, h:h + 1] - cum_r[h:h + 1, :]
            w = jnp.exp(jnp.where(causal, seg, -jnp.inf)) * cb_g * dt_r[h:h + 1, :]
            xh = ux[:, hs]
            hst = h_scr[h]
            y_h = _dot(w, xh) + _dot_nt(cg, hst) * ecum[:, h:h + 1]
            yacc[:, hs] = y_h
            h_scr[h] = hst * ecl[:, h:h + 1] + _dot_tn(xh * tail[:, h:h + 1], bg)

    y = yacc[...] + dsk[...] * ux
    y_ref[...] = _ssd_gate_norm(y, z_ref[...], ng[...]).astype(BF16)

    @pl.when(c == pl.num_programs(1) - 1)
    def _():
        st_ref[...] = h_scr[...]


def _ssd_params(p):
    cw, cb = p["conv_w"], p["conv_b"].reshape(1, CONV_CH)
    pad = LANES - A_HEADS
    return dict(
        cwx=cw[:, :A_DIM], cwb=cw[:, A_DIM:A_DIM + A_GN], cwc=cw[:, A_DIM + A_GN:],
        cbx=cb[:, :A_DIM], cbb=cb[:, A_DIM:A_DIM + A_GN], cbc=cb[:, A_DIM + A_GN:],
        dtb=jnp.pad(p["dt_bias"], (0, pad)).reshape(1, LANES),
        aneg=jnp.pad(-jnp.exp(p["a_log"]), (0, pad)).reshape(1, LANES),
        dsk=jnp.repeat(p["d_skip"], A_HEAD).reshape(1, A_DIM),
        ng=p["ssm_norm_g"].reshape(1, A_DIM),
    )


def _ssd_prompt(proj, sp):
    bsz, seq, _ = proj.shape
    ch = SSD_CHUNK
    gn = A_GN

    def col(width, off):
        return pl.BlockSpec((None, ch, width), lambda b, c: (b, c, off // width))

    names = ("cwx", "cwb", "cwc", "cbx", "cbb", "cbc", "dtb", "aneg", "dsk", "ng")
    y, st = pl.pallas_call(
        _ssd_prompt_kernel,
        grid=(bsz, seq // ch),
        in_specs=[col(A_DIM, P_Z), col(A_DIM, P_X), col(gn, P_B), col(gn, P_C), col(LANES, P_DT)]
                 + [_full(sp[n].shape) for n in names],
        out_specs=[pl.BlockSpec((None, ch, A_DIM), lambda b, c: (b, c, 0)),
                   pl.BlockSpec((None, A_HEADS, A_HEAD, A_STATE), lambda b, c: (b, 0, 0, 0))],
        out_shape=[jax.ShapeDtypeStruct((bsz, seq, A_DIM), BF16),
                   jax.ShapeDtypeStruct((bsz, A_HEADS, A_HEAD, A_STATE), F32)],
        scratch_shapes=[pltpu.VMEM((A_HEADS, A_HEAD, A_STATE), F32),
                        pltpu.VMEM((ch + 8, A_DIM), F32),
                        pltpu.VMEM((ch + 8, gn), F32),
                        pltpu.VMEM((ch + 8, gn), F32),
                        pltpu.VMEM((ch, A_DIM), F32)],
        compiler_params=_cparams(("parallel", "arbitrary")),
        name="ssd_prompt",
    )(proj, proj, proj, proj, proj, *[sp[n] for n in names])
    return y, st


def _ssd_step_kernel(z_ref, x_ref, b_ref, c_ref, dt_ref, cs_ref, cw, cb, dtb, anegx, expand, dsk, ng,
                     h_ref, _, y_ref, ho_ref, y_scr):
    bb = STEP_BB
    xbc = (x_ref[...], b_ref[...], c_ref[...])
    offs = (0, A_DIM, A_DIM + A_GN, CONV_CH)
    u = []
    for i in range(3):
        sl = slice(offs[i], offs[i + 1])
        acc = cb[:, sl] + xbc[i] * cw[CONV_W - 1:CONV_W, sl]
        for tap in range(CONV_W - 1):
            acc = acc + cs_ref[tap][:, sl] * cw[tap:tap + 1, sl]
        u.append(_silu(acc))
    ux, ub, uc = u
    dt = _softplus(dt_ref[...] + dtb[...])
    dtx = _dot_hi(dt, expand[...])
    dec_t = jnp.exp(dtx * anegx[...]).T
    xdt_t = (ux * dtx).T
    gw = A_DIM // A_GROUPS
    for b in range(bb):
        for g in range(A_GROUPS):
            rs = slice(g * gw, (g + 1) * gw)
            ns = slice(g * A_STATE, (g + 1) * A_STATE)
            hg = h_ref[b, rs, :] * dec_t[rs, b:b + 1] + xdt_t[rs, b:b + 1] * ub[b:b + 1, ns]
            ho_ref[b, rs, :] = hg
            y_scr[b:b + 1, rs] = _dot_nt(uc[:, ns], hg)[b:b + 1, :]
    y = y_scr[...] + dsk[...] * ux
    y_ref[...] = _ssd_gate_norm(y, z_ref[...], ng[...]).astype(BF16)


def _ssd_step(proj, conv_state, h_all, h_new, layer, p, sp):
    bsz = proj.shape[0]
    bb = STEP_BB
    cs = jnp.swapaxes(conv_state, 0, 1)
    hflat = h_all.reshape(h_all.shape[0], bsz, A_DIM, A_STATE)
    expand = (jnp.arange(LANES)[:, None] == (jnp.arange(A_DIM) // A_HEAD)[None, :]).astype(F32)
    anegx = jnp.repeat(-jnp.exp(p["a_log"]), A_HEAD).reshape(1, A_DIM)
    cw, cb = p["conv_w"], p["conv_b"].reshape(1, CONV_CH)

    def col(width, off):
        return pl.BlockSpec((bb, width), lambda i: (i, off // width))

    y, hn = pl.pallas_call(
        _ssd_step_kernel,
        grid=(bsz // bb,),
        in_specs=[col(A_DIM, P_Z), col(A_DIM, P_X), col(A_GN, P_B), col(A_GN, P_C), col(LANES, P_DT),
                  pl.BlockSpec((CONV_W - 1, bb, CONV_CH), lambda i: (0, i, 0)),
                  _full(cw.shape), _full(cb.shape), _full(sp["dtb"].shape), _full(anegx.shape),
                  _full(expand.shape), _full(sp["dsk"].shape), _full(sp["ng"].shape),
                  pl.BlockSpec((None, bb, A_DIM, A_STATE), lambda i: (layer, i, 0, 0)),
                  pl.BlockSpec(memory_space=pl.ANY)],
        out_specs=[pl.BlockSpec((bb, A_DIM), lambda i: (i, 0)),
                   pl.BlockSpec((None, bb, A_DIM, A_STATE), lambda i: (layer, i, 0, 0))],
        out_shape=[jax.ShapeDtypeStruct((bsz, A_DIM), BF16),
                   jax.ShapeDtypeStruct(hflat.shape, F32)],
        input_output_aliases={14: 1},
        scratch_shapes=[pltpu.VMEM((bb, A_DIM), F32)],
        compiler_params=_cparams(("parallel",)),
        name="ssd_step",
    )(proj, proj, proj, proj, proj, cs, cw, cb, sp["dtb"], anegx, expand, sp["dsk"], sp["ng"], hflat,
      h_new.reshape(hflat.shape))
    return y, hn.reshape(h_all.shape)


def _hgrn_gates(q, f, lbp):
    qf = _silu(q)
    kf = lbp[0:1, :] * jax.nn.sigmoid(-f)
    log_sig = jnp.minimum(f, 0.0) - jnp.log1p(jnp.exp(-jnp.abs(f)))
    a = lbp[1:2, :]
    b = lbp[2:3, :] + log_sig
    logf = jnp.maximum(a, b) + jnp.log1p(jnp.exp(-jnp.abs(a - b)))
    return qf, kf, logf


def _hgrn_out(o, g, ng):
    outs = []
    for h in range(B_HEADS):
        oh = o[:, h * B_HEAD:(h + 1) * B_HEAD]
        outs.append(oh * lax.rsqrt(jnp.mean(oh * oh, axis=-1, keepdims=True) + NORM_EPS) * ng)
    return jnp.concatenate(outs, axis=-1) * _silu(g)


def _hgrn_prompt_kernel(q_ref, f_ref, i_ref, g_ref, lbp, ng, o_ref, st_ref,
                        s_scr, cum_scr, q_scr, k_scr, o_scr):
    c = pl.program_id(1)
    tb, cs = HGRN_BLOCK, HGRN_CHUNK

    @pl.when(c == 0)
    def _():
        s_scr[...] = jnp.zeros_like(s_scr)

    qf, kf, logf = _hgrn_gates(q_ref[...], f_ref[...], lbp[...])
    r_i, c_i = _iota((tb, tb), 0), _iota((tb, tb), 1)
    bd = ((r_i // cs == c_i // cs) & (c_i <= r_i)).astype(F32)
    cum_scr[...] = _dot_hi(bd, logf)
    q_scr[...] = qf
    k_scr[...] = kf
    rows = _iota((cs, B_HEAD), 0)

    def chunk(cc, carry):
        r0 = pl.multiple_of(cc * cs, cs)
        for h in range(B_HEADS):
            hs = slice(h * B_HEAD, (h + 1) * B_HEAD)
            cu = cum_scr[pl.ds(r0, cs), hs]
            q = q_scr[pl.ds(r0, cs), hs]
            k = k_scr[pl.ds(r0, cs), hs]
            v = i_ref[pl.ds(r0, cs), hs]
            st = s_scr[h]
            o = _dot_nt(q * jnp.exp(cu), st)
            for s in range(cs):
                d = jnp.exp(jnp.where(rows >= s, cu - cu[s:s + 1, :], -jnp.inf))
                att = jnp.sum(q * k[s:s + 1, :] * d, axis=-1, keepdims=True)
                o = o + att * v[s:s + 1, :]
            o_scr[pl.ds(r0, cs), hs] = o
            cl = cu[cs - 1:cs, :]
            s_scr[h] = st * jnp.exp(cl) + _dot_tn(v, k * jnp.exp(cl - cu))
        return carry

    lax.fori_loop(0, tb // cs, chunk, 0)
    o_ref[...] = _hgrn_out(o_scr[...], g_ref[...], ng[...]).astype(BF16)

    @pl.when(c == pl.num_programs(1) - 1)
    def _():
        for h in range(B_HEADS):
            st_ref[h] = s_scr[h].T


def _hgrn_prompt(proj, lbp, ng):
    bsz, seq, _ = proj.shape
    tb = HGRN_BLOCK

    def col(off):
        return pl.BlockSpec((None, tb, B_DIM), lambda b, c: (b, c, off // B_DIM))

    o, st = pl.pallas_call(
        _hgrn_prompt_kernel,
        grid=(bsz, seq // tb),
        in_specs=[col(P_Q), col(P_F), col(P_I), col(P_G), _full(lbp.shape), _full(ng.shape)],
        out_specs=[pl.BlockSpec((None, tb, B_DIM), lambda b, c: (b, c, 0)),
                   pl.BlockSpec((None, B_HEADS, B_HEAD, B_HEAD), lambda b, c: (b, 0, 0, 0))],
        out_shape=[jax.ShapeDtypeStruct((bsz, seq, B_DIM), BF16),
                   jax.ShapeDtypeStruct((bsz, B_HEADS, B_HEAD, B_HEAD), F32)],
        scratch_shapes=[pltpu.VMEM((B_HEADS, B_HEAD, B_HEAD), F32)]
                       + [pltpu.VMEM((tb, B_DIM), F32)] * 4,
        compiler_params=_cparams(("parallel", "arbitrary")),
        name="hgrn_prompt",
    )(proj, proj, proj, proj, lbp, ng)
    return o, st


def _hgrn_step_kernel(q_ref, f_ref, i_ref, g_ref, lbp, ng, s_ref, _, o_ref, so_ref, o_scr):
    bb = STEP_BB
    qf, kf, logf = _hgrn_gates(q_ref[...], f_ref[...], lbp[...])
    vf = i_ref[...]
    dec_t = jnp.exp(logf).T
    k_t = kf.T
    q_t = qf.T
    for b in range(bb):
        for h in range(B_HEADS):
            hs = slice(h * B_HEAD, (h + 1) * B_HEAD)
            sh = s_ref[b, hs, :] * dec_t[hs, b:b + 1] + k_t[hs, b:b + 1] * vf[b:b + 1, hs]
            so_ref[b, hs, :] = sh
            o_scr[b:b + 1, hs] = jnp.sum(q_t[hs, b:b + 1] * sh, axis=0, keepdims=True)
    o_ref[...] = _hgrn_out(o_scr[...], g_ref[...], ng[...]).astype(BF16)


def _hgrn_step(proj, s_all, s_new, layer, lbp, ng):
    bsz = proj.shape[0]
    bb = STEP_BB
    sflat = s_all.reshape(s_all.shape[0], bsz, B_DIM, B_HEAD)

    def col(off):
        return pl.BlockSpec((bb, B_DIM), lambda i: (i, off // B_DIM))

    o, sn = pl.pallas_call(
        _hgrn_step_kernel,
        grid=(bsz // bb,),
        in_specs=[col(P_Q), col(P_F), col(P_I), col(P_G), _full(lbp.shape), _full(ng.shape),
                  pl.BlockSpec((None, bb, B_DIM, B_HEAD), lambda i: (layer, i, 0, 0)),
                  pl.BlockSpec(memory_space=pl.ANY)],
        out_specs=[pl.BlockSpec((bb, B_DIM), lambda i: (i, 0)),
                   pl.BlockSpec((None, bb, B_DIM, B_HEAD), lambda i: (layer, i, 0, 0))],
        out_shape=[jax.ShapeDtypeStruct((bsz, B_DIM), BF16),
                   jax.ShapeDtypeStruct(sflat.shape, F32)],
        input_output_aliases={7: 1},
        scratch_shapes=[pltpu.VMEM((bb, B_DIM), F32)],
        compiler_params=_cparams(("parallel",)),
        name="hgrn_step",
    )(proj, proj, proj, proj, lbp, ng, sflat, s_new.reshape(sflat.shape))
    return o, sn.reshape(s_all.shape)


RW_NAMES = ("mu_r", "mu_k", "mu_v", "mu_lr", "w0", "w2", "a0", "a2", "g2", "k_k", "k_a", "r_k")


def _rwkv_params(p):
    mu = p["shift_mu"].reshape(1, C_COLS)
    row = lambda a: a.reshape(1, C_DIM)
    return dict(
        mu_r=mu[:, :C_DIM], mu_k=mu[:, C_DIM:2 * C_DIM], mu_v=mu[:, 2 * C_DIM:3 * C_DIM],
        mu_lr=mu[:, 3 * C_DIM:],
        w0=row(p["w0"]), w2=p["w2"].astype(BF16), a0=row(p["a0"]), a2=p["a2"].astype(BF16),
        g2=p["g2"].astype(BF16), k_k=row(p["k_k"]), k_a=row(p["k_a"]), r_k=row(p["r_k"]),
        lnx_w=row(p["lnx_w"]), lnx_b=row(p["lnx_b"]),
    )


def _head_sum(x):
    return [jnp.sum(x[:, h * C_HEAD:(h + 1) * C_HEAD], axis=-1, keepdims=True) for h in range(C_HEADS)]


def _head_bcast(cols, fn=lambda c: c):
    rows = cols[0].shape[0]
    return jnp.concatenate([jnp.broadcast_to(fn(c), (rows, C_HEAD)) for c in cols], axis=-1)


def _rwkv_prep(r, k, v, lr, pr, pk, pv, plr, prm):
    mu_r, mu_k, mu_v, mu_lr, w0, w2, a0, a2, g2, k_k, k_a, r_k = prm
    r = r + (pr - r) * mu_r
    k = k + (pk - k) * mu_k
    v = v + (pv - v) * mu_v
    lr = lr + (plr - lr) * mu_lr
    wl, al, gl = lr[:, :C_LR_W], lr[:, C_LR_W:C_LR_W + C_LR_A], lr[:, C_LR_W + C_LR_A:]
    wraw = -_softplus(-(w0 + _dot(jnp.tanh(wl), w2))) - 0.5
    ld = -jnp.exp(wraw)
    a = jax.nn.sigmoid(a0 + _dot(al, a2))
    g = _dot(jax.nn.sigmoid(gl), g2)
    kk = k * k_k
    kk = kk * _head_bcast(_head_sum(kk * kk), lambda n2: 1.0 / jnp.maximum(jnp.sqrt(n2), L2_EPS))
    k2 = k * (1.0 + (a - 1.0) * k_a)
    bonus = _head_bcast(_head_sum(r * k2 * r_k)) * v
    return r, k2, v, kk, kk * a, ld, g, bonus


def _rwkv_out(y, bonus, g, lnx_w, lnx_b):
    n = float(C_HEAD)
    mean = _head_bcast(_head_sum(y), lambda s: s / n)
    yc = y - mean
    rstd = _head_bcast(_head_sum(yc * yc), lambda s: lax.rsqrt(s / n + GN_EPS))
    return (yc * rstd * lnx_w + lnx_b + bonus) * g


RW_GH = 4
RW_GW = RW_GH * C_HEAD


def _bd_mask():
    return _iota((RW_GW, RW_GW), 0) // C_HEAD == _iota((RW_GW, RW_GW), 1) // C_HEAD


def _bd_rows(a):
    return jnp.where(_bd_mask(), jnp.concatenate([a] * RW_GH, axis=0), 0.0)


def _bd_lanes(s):
    return jnp.where(_bd_mask(), jnp.concatenate([s] * RW_GH, axis=1), 0.0)


def _bd_fold(m):
    m = jnp.where(_bd_mask(), m, 0.0)
    out = m[:, :C_HEAD]
    for h in range(1, RW_GH):
        out = out + m[:, h * C_HEAD:(h + 1) * C_HEAD]
    return out


def _to_stack(a):
    return jnp.concatenate([a[:, h * C_HEAD:(h + 1) * C_HEAD] for h in range(RW_GH)], axis=0)


def _to_all(s):
    return jnp.concatenate([s[h * C_HEAD:(h + 1) * C_HEAD, :] for h in range(RW_GH)], axis=1)


def _rwkv_a_kernel(r_ref, k_ref, v_ref, lr_ref, rp_ref, kp_ref, vp_ref, lrp_ref, *rest):
    prm = tuple(x[...] for x in rest[:len(RW_NAMES)])
    (kd_ref, rd_ref, bg_ref, ge_ref, bonus_ref, gate_ref, aab_ref, rb_ref, m2_ref, rkv_ref, vk_ref,
     xs_scr, b1_scr, b2_scr, b3_scr, b4_scr) = rest[len(RW_NAMES):]
    c = pl.program_id(1)
    tb, cs = RWKV_BLOCK, RWKV_CHUNK
    first = (c == 0)

    def prev(cur_ref, prev_ref, width):
        xs_scr[8:8 + tb, 0:width] = cur_ref[...]
        xs_scr[7:8, 0:width] = jnp.where(first, 0.0, prev_ref[7:8, :])
        return xs_scr[pl.ds(7, tb), 0:width]

    pr = prev(r_ref, rp_ref, C_DIM)
    pk = prev(k_ref, kp_ref, C_DIM)
    pv = prev(v_ref, vp_ref, C_DIM)
    plr = prev(lr_ref, lrp_ref, C_LR)
    r, k2, v, kk, beta, ld, g, bonus = _rwkv_prep(
        r_ref[...], k_ref[...], v_ref[...], lr_ref[...], pr, pk, pv, plr, prm)
    bonus_ref[...] = bonus
    gate_ref[...] = g

    r_i, c_i = _iota((tb, tb), 0), _iota((tb, tb), 1)
    same = (r_i // cs == c_i // cs)
    lcum = _dot_hi((same & (c_i <= r_i)).astype(F32), ld)
    lend = _dot_hi(same.astype(F32), ld)
    einv = jnp.exp(-lcum)
    eend = jnp.exp(lend - lcum)
    ge_ref[...] = jnp.exp(lend)
    kd_ref[...] = kk * jnp.exp(lcum - ld)
    rd_ref[...] = r * jnp.exp(lcum)
    bg_ref[...] = beta * eend
    b1_scr[...] = beta * einv
    b2_scr[...] = k2 * einv
    b3_scr[...] = k2 * eend
    b4_scr[...] = v
    gw = RW_GW
    tpos, spos = _iota((gw, cs), 0) % cs, _iota((gw, cs), 1)
    strict, incl = tpos > spos, tpos >= spos
    hc = C_HEAD

    def chunk(j, carry):
        r0 = pl.multiple_of(j * cs, cs)
        q0 = pl.multiple_of(j * gw, gw)
        for g in range(C_HEADS // RW_GH):
            gs = slice(g * gw, (g + 1) * gw)
            ls = slice(g * hc, (g + 1) * hc)
            kd = kd_ref[pl.ds(r0, cs), gs]
            rd = rd_ref[pl.ds(r0, cs), gs]
            v = b4_scr[pl.ds(r0, cs), gs]
            lhs = jnp.concatenate([_bd_rows(kd), _bd_rows(rd)], axis=0)
            rhs = jnp.concatenate([b1_scr[pl.ds(r0, cs), gs], b2_scr[pl.ds(r0, cs), gs]], axis=0)
            gm = _dot_nt(lhs, rhs)
            aab_ref[pl.ds(q0, gw), ls] = jnp.where(strict, gm[:gw, :hc], 0.0)
            aak = jnp.where(strict, gm[:gw, hc:], 0.0)
            rb_ref[pl.ds(q0, gw), ls] = jnp.where(incl, gm[gw:, :hc], 0.0)
            rk = jnp.where(incl, gm[gw:, hc:], 0.0)
            mr = _dot(jnp.concatenate([_bd_lanes(aak), _bd_lanes(rk)], axis=0), _to_stack(v))
            m2_ref[pl.ds(q0, gw), ls] = mr[:gw]
            rkv_ref[pl.ds(q0, gw), ls] = mr[gw:]
            vk_ref[pl.ds(q0, gw), ls] = _bd_fold(_dot_tn(v, b3_scr[pl.ds(r0, cs), gs]))
        return carry

    lax.fori_loop(0, tb // cs, chunk, 0)


def _tri_solve_kernel(a_ref, t_ref, a_scr, t_scr):
    n, hf = RWKV_CHUNK, RWKV_CHUNK // 2
    npb = a_scr.shape[2]
    ng = a_scr.shape[1] // n

    def load(t, carry):
        a_scr[t] = a_ref[pl.ds(t, npb, stride=n), :].T
        return carry

    lax.fori_loop(0, n, load, 0)
    jrow = _iota((hf, npb), 0)
    zero = jnp.zeros((hf, npb), F32)

    def coef(t, g, s):
        return a_scr[t, pl.ds(g * n + s, 1), :]

    def row_lo(t, carry):
        def inner(s, accs):
            return tuple(acc - coef(t, g, s) * t_scr[s, g * n:g * n + hf, :] for g, acc in enumerate(accs))

        accs = lax.fori_loop(0, t, inner, ((jrow == t).astype(F32),) * ng)
        for g, acc in enumerate(accs):
            t_scr[t, g * n:g * n + hf, :] = acc
            t_scr[t, g * n + hf:(g + 1) * n, :] = zero
        return carry

    lax.fori_loop(0, hf, row_lo, 0)

    def row_hi(t, carry):
        def inner_lo(s, accs):
            return tuple(acc - coef(t, g, s) * t_scr[s, g * n:g * n + hf, :] for g, acc in enumerate(accs))

        def inner_hi(s, accs):
            lo, hi = accs[:ng], accs[ng:]
            lo = tuple(acc - coef(t, g, s) * t_scr[s, g * n:g * n + hf, :] for g, acc in enumerate(lo))
            hi = tuple(acc - coef(t, g, s) * t_scr[s, g * n + hf:(g + 1) * n, :] for g, acc in enumerate(hi))
            return lo + hi

        lo = lax.fori_loop(0, hf, inner_lo, (zero,) * ng)
        accs = lax.fori_loop(hf, t, inner_hi, lo + ((jrow + hf == t).astype(F32),) * ng)
        for g in range(ng):
            t_scr[t, g * n:g * n + hf, :] = accs[g]
            t_scr[t, g * n + hf:(g + 1) * n, :] = accs[ng + g]
        return carry

    lax.fori_loop(hf, n, row_hi, 0)

    def store(t, carry):
        t_ref[pl.ds(t, npb, stride=n), :] = t_scr[t].T
        return carry

    lax.fori_loop(0, n, store, 0)


def _tri_solve(a):
    rows, width = a.shape
    n = RWKV_CHUNK
    npb = min(LANES, rows // n)
    return pl.pallas_call(
        _tri_solve_kernel,
        grid=(rows // (npb * n),),
        in_specs=[pl.BlockSpec((npb * n, width), lambda i: (i, 0))],
        out_specs=pl.BlockSpec((npb * n, width), lambda i: (i, 0)),
        out_shape=jax.ShapeDtypeStruct(a.shape, F32),
        scratch_shapes=[pltpu.VMEM((n, width, npb), F32)] * 2,
        compiler_params=_cparams(("parallel",)),
        name="tri_solve",
    )(a)


def _rwkv_c_kernel(t_ref, kd_ref, rd_ref, bg_ref, ge_ref, bonus_ref, gate_ref, rb_ref, m2_ref, rkv_ref, vk_ref,
                   lnw, lnb, y_ref, st_ref, s_scr, y_scr):
    c = pl.program_id(1)
    tb, cs, gw, hc = RWKV_BLOCK, RWKV_CHUNK, RW_GW, C_HEAD

    @pl.when(c == 0)
    def _():
        s_scr[...] = jnp.zeros_like(s_scr)

    def chunk(j, carry):
        r0 = pl.multiple_of(j * cs, cs)
        q0 = pl.multiple_of(j * gw, gw)
        for g in range(C_HEADS // RW_GH):
            gs = slice(g * gw, (g + 1) * gw)
            ls = slice(g * hc, (g + 1) * hc)
            rhs = jnp.concatenate([_to_stack(kd_ref[pl.ds(r0, cs), gs]), m2_ref[pl.ds(q0, gw), ls]], axis=1)
            wu = _dot(_bd_lanes(t_ref[pl.ds(q0, gw), ls]), rhs)
            rbwu = _dot(_bd_lanes(rb_ref[pl.ds(q0, gw), ls]), wu)
            rt = _to_stack(rd_ref[pl.ds(r0, cs), gs]) - rbwu[:, :hc]
            yc = rkv_ref[pl.ds(q0, gw), ls] - rbwu[:, hc:]
            wu_bd = jnp.concatenate([_bd_lanes(wu[:, :hc]), _bd_lanes(wu[:, hc:])], axis=1)
            nu = _dot_tn(wu_bd, _to_stack(bg_ref[pl.ds(r0, cs), gs]))
            ge_row = ge_ref[pl.ds(r0, 1), gs]
            ge_st = jnp.concatenate(
                [jnp.broadcast_to(ge_row[:, h * hc:(h + 1) * hc], (hc, hc)) for h in range(RW_GH)], axis=0)
            s = s_scr[g]
            s_bd = _bd_lanes(s)
            y_scr[pl.ds(r0, cs), gs] = _dot_nt(_to_all(rt), s_bd) + _to_all(yc)
            s_scr[g] = s * ge_st - _dot(s_bd, nu[:gw]) + (vk_ref[pl.ds(q0, gw), ls] - nu[gw:])
        return carry

    lax.fori_loop(0, tb // cs, chunk, 0)
    y_ref[...] = _rwkv_out(y_scr[...], bonus_ref[...], gate_ref[...], lnw[...], lnb[...]).astype(BF16)

    @pl.when(c == pl.num_programs(1) - 1)
    def _():
        st_ref[...] = s_scr[...]


def _rwkv_prompt(proj, rp):
    bsz, seq, _ = proj.shape
    tb, cs = RWKV_BLOCK, RWKV_CHUNK
    nblk = seq // tb

    def col(width, off):
        return pl.BlockSpec((None, tb, width), lambda b, c: (b, c, off // width))

    def pcol(width, off):
        return pl.BlockSpec((None, 8, width), lambda b, c: (b, jnp.maximum(c * (tb // 8) - 1, 0), off // width))

    ng = C_HEADS // RW_GH
    srows = tb // cs * RW_GW
    tok = pl.BlockSpec((None, tb, C_DIM), lambda b, c: (b, c, 0))
    stk = pl.BlockSpec((None, srows, ng * C_HEAD), lambda b, c: (b, c, 0))
    tok_shape = jax.ShapeDtypeStruct((bsz, seq, C_DIM), F32)
    stk_shape = jax.ShapeDtypeStruct((bsz, nblk * srows, ng * C_HEAD), F32)
    prm = [rp[n] for n in RW_NAMES]
    outs = pl.pallas_call(
        _rwkv_a_kernel,
        grid=(bsz, nblk),
        in_specs=[col(C_DIM, P_R), col(C_DIM, P_K), col(C_DIM, P_V), col(C_LR, P_LR),
                  pcol(C_DIM, P_R), pcol(C_DIM, P_K), pcol(C_DIM, P_V), pcol(C_LR, P_LR)]
                 + [_full(x.shape) for x in prm],
        out_specs=[tok] * 6 + [stk] * 5,
        out_shape=[tok_shape] * 6 + [stk_shape] * 5,
        scratch_shapes=[pltpu.VMEM((tb + 8, C_DIM), F32)] + [pltpu.VMEM((tb, C_DIM), F32)] * 4,
        compiler_params=_cparams(("parallel", "parallel")),
        name="rwkv_prep",
    )(proj, proj, proj, proj, proj, proj, proj, proj, *prm)
    kd, rd, bg, ge, bonus, gate, aab, rb, m2, rkv, vk = outs

    tmat = _tri_solve(aab.reshape(-1, ng * C_HEAD)).reshape(stk_shape.shape)

    y, st = pl.pallas_call(
        _rwkv_c_kernel,
        grid=(bsz, nblk),
        in_specs=[stk] + [tok] * 6 + [stk] * 4 + [_full((1, C_DIM))] * 2,
        out_specs=[tok, pl.BlockSpec((None, ng, RW_GW, C_HEAD), lambda b, c: (b, 0, 0, 0))],
        out_shape=[jax.ShapeDtypeStruct((bsz, seq, C_DIM), BF16),
                   jax.ShapeDtypeStruct((bsz, ng, RW_GW, C_HEAD), F32)],
        scratch_shapes=[pltpu.VMEM((ng, RW_GW, C_HEAD), F32), pltpu.VMEM((tb, C_DIM), F32)],
        compiler_params=_cparams(("parallel", "arbitrary")),
        name="rwkv_scan",
    )(tmat, kd, rd, bg, ge, bonus, gate, rb, m2, rkv, vk, rp["lnx_w"], rp["lnx_b"])
    return y, st.reshape(bsz, C_HEADS, C_HEAD, C_HEAD)


def _rwkv_step_kernel(r_ref, k_ref, v_ref, lr_ref, sh_ref, *rest):
    prm = tuple(x[...] for x in rest[:len(RW_NAMES)])
    lnw, lnb, s_ref, _, y_ref, so_ref = rest[len(RW_NAMES):]
    bb = STEP_BB
    sh = sh_ref[...]
    r, k2, v, kk, beta, ld, g, bonus = _rwkv_prep(
        r_ref[...], k_ref[...], v_ref[...], lr_ref[...],
        sh[:, :C_DIM], sh[:, C_DIM:2 * C_DIM], sh[:, 2 * C_DIM:3 * C_DIM], sh[:, 3 * C_DIM:], prm)
    w = jnp.exp(ld)
    v_t = v.T
    lane = _iota((C_DIM, LANES), 1)
    heads = [[x[:, h * C_HEAD:(h + 1) * C_HEAD] for x in (kk, w, beta, k2, r)] for h in range(C_HEADS)]

    def rows(i, b):
        return jnp.concatenate(
            [jnp.broadcast_to(heads[h][i][b:b + 1, :], (C_HEAD, C_HEAD)) for h in range(C_HEADS)], axis=0)

    y_t = jnp.zeros((C_DIM, LANES), F32)
    for b in range(bb):
        s = s_ref[b]
        sa = -jnp.sum(s * rows(0, b), axis=-1, keepdims=True)
        sn = s * rows(1, b) + sa * rows(2, b) + v_t[:, b:b + 1] * rows(3, b)
        so_ref[b] = sn
        y_t = jnp.where(lane == b, jnp.sum(sn * rows(4, b), axis=-1, keepdims=True), y_t)
    y = y_t.T[0:bb, :]
    y_ref[...] = _rwkv_out(y, bonus, g, lnw[...], lnb[...]).astype(BF16)


def _rwkv_step(proj, shift, s_all, s_new, layer, rp):
    bsz = proj.shape[0]
    bb = STEP_BB
    sflat = s_all.reshape(s_all.shape[0], bsz, C_DIM, C_HEAD)

    def col(width, off):
        return pl.BlockSpec((bb, width), lambda i: (i, off // width))

    prm = [rp[n] for n in RW_NAMES]
    y, sn = pl.pallas_call(
        _rwkv_step_kernel,
        grid=(bsz // bb,),
        in_specs=[col(C_DIM, P_R), col(C_DIM, P_K), col(C_DIM, P_V), col(C_LR, P_LR),
                  pl.BlockSpec((bb, C_COLS), lambda i: (i, 0))]
                 + [_full(x.shape) for x in prm] + [_full((1, C_DIM))] * 2
                 + [pl.BlockSpec((None, bb, C_DIM, C_HEAD), lambda i: (layer, i, 0, 0)),
                    pl.BlockSpec(memory_space=pl.ANY)],
        out_specs=[pl.BlockSpec((bb, C_DIM), lambda i: (i, 0)),
                   pl.BlockSpec((None, bb, C_DIM, C_HEAD), lambda i: (layer, i, 0, 0))],
        out_shape=[jax.ShapeDtypeStruct((bsz, C_DIM), BF16),
                   jax.ShapeDtypeStruct(sflat.shape, F32)],
        input_output_aliases={len(prm) + 8: 1},
        compiler_params=_cparams(("parallel",)),
        name="rwkv_step",
    )(proj, proj, proj, proj, shift, *prm, rp["lnx_w"], rp["lnx_b"], sflat, s_new.reshape(sflat.shape))
    return y, sn.reshape(s_all.shape)


def _layout_w_in_kernel(w_ref, o_ref):
    o = A_DIM + CONV_CH
    rows = w_ref.shape[0]
    o_ref[:, :o] = w_ref[:, :o].astype(BF16)
    o_ref[:, o:P_DT] = w_ref[:, o + A_HEADS:].astype(BF16)
    dt_tile = jnp.where(_iota((rows, LANES), 1) < A_HEADS, w_ref[:, o:o + LANES], 0.0)
    o_ref[:, P_DT:P_DT + LANES] = dt_tile.astype(BF16)
    o_ref[:, P_DT + LANES:] = jnp.zeros((rows, P_COLS - P_DT - LANES), BF16)


def _layout_w_in(w_all, layer):
    k = w_all.shape[1]
    tr = 256
    return pl.pallas_call(
        _layout_w_in_kernel,
        grid=(k // tr,),
        in_specs=[pl.BlockSpec((None, tr, IN_COLS), lambda i: (layer, i, 0))],
        out_specs=pl.BlockSpec((tr, P_COLS), lambda i: (i, 0)),
        out_shape=jax.ShapeDtypeStruct((k, P_COLS), BF16),
        compiler_params=_cparams(("parallel",)),
        name="layout_w_in",
    )(w_all)


def _xbc_cols(proj):
    return proj[..., P_X:P_X + CONV_CH]


def _moe_route(logits, n_tok):
    top_logit, top_idx = lax.top_k(logits, TOP_K)
    gate = jax.nn.softmax(top_logit, axis=-1)
    n_assign = n_tok * TOP_K
    flat_e = top_idx.reshape(-1).astype(jnp.int32)
    order = jnp.argsort(flat_e)
    onehot = jax.nn.one_hot(flat_e, N_EXPERTS, dtype=jnp.int32)
    counts = jnp.sum(onehot, axis=0)
    padded = (counts + MOE_BM - 1) // MOE_BM * MOE_BM
    pad_end = jnp.cumsum(padded)
    pad_start = pad_end - padded
    start = jnp.cumsum(counts) - counts
    rank = jnp.sum((jnp.cumsum(onehot, axis=0) - onehot) * onehot, axis=1)
    slot_of = (pad_start[flat_e] + rank).reshape(n_tok, TOP_K)
    n_blocks = -(-n_assign // MOE_BM) + N_EXPERTS
    n_slots = n_blocks * MOE_BM
    block_start = jnp.arange(n_blocks, dtype=jnp.int32) * MOE_BM
    nvalid = (pad_end[-1] // MOE_BM).astype(jnp.int32)
    bexp = jnp.minimum(jnp.searchsorted(pad_end, block_start, side="right"), N_EXPERTS - 1).astype(jnp.int32)
    slot_e = jnp.repeat(bexp, MOE_BM)
    within = jnp.arange(n_slots, dtype=jnp.int32) - pad_start[slot_e]
    src = order[jnp.clip(start[slot_e] + within, 0, n_assign - 1)] // TOP_K
    slot_tok = jnp.where(within < counts[slot_e], src, n_tok)
    left = counts[bexp] - (block_start - pad_start[bexp])
    nsub = jnp.clip((left + MOE_SUB - 1) // MOE_SUB, 0, MOE_BM // MOE_SUB).astype(jnp.int32)
    last = bexp[jnp.maximum(nvalid - 1, 0)]
    bexp = jnp.where(jnp.arange(n_blocks) < nvalid, bexp, last)
    first = jnp.concatenate([jnp.ones((1,), jnp.int32), (bexp[1:] != bexp[:-1]).astype(jnp.int32)])
    return gate, slot_tok, slot_of, bexp, first, nsub


def _moe(xp, xs, h, logits, wg, wu, wd, g, final):
    tp = xp.shape[0]
    n_tok = h.shape[0]
    gate, slot_tok, slot_of, bexp, first, nsub = _moe_route(logits[:, :N_EXPERTS], n_tok)
    xb = h[jnp.where(slot_tok == n_tok, 0, slot_tok)]
    yb = _moe_experts(xb, bexp, first, nsub, wg, wu, wd)
    return (_moe_combine(xp, yb, slot_of[:tp], gate[:tp], g, final),
            _moe_combine(xs, yb, slot_of[tp:], gate[tp:], g, final))


def kernel(x_prompt, x_sample, state_ssm, state_conv, state_hgrn, state_rwkv, state_shift, norm1_g, w_in, conv_w, conv_b, dt_bias, a_log, d_skip, ssm_norm_g, lb_logits, hgrn_norm_g, shift_mu, w0, w2, a0, a2, g2, k_k, k_a, r_k, lnx_w, lnx_b, w_out, norm2_g, ffn_w_gate, ffn_w_up, ffn_w_down, router_w, exp_w_gate, exp_w_up, exp_w_down, final_norm_g):
    depth = w_in.shape[0]
    bp, seq, _ = x_prompt.shape
    bs = x_sample.shape[0]
    tp = bp * seq
    lb_soft = jax.nn.softmax(lb_logits.astype(F32), axis=0)
    lower_bounds = jnp.clip(jnp.cumsum(lb_soft, axis=0) - lb_soft[0:1], 0.0, 1.0)

    xp = x_prompt.reshape(tp, D_MODEL)
    xs = x_sample.reshape(bs, D_MODEL)
    outs = {k: [] for k in ("p_ssm", "p_conv", "p_hgrn", "p_rwkv", "p_shift", "s_conv", "s_shift")}
    ssm_s, hgrn_s, rwkv_s = (jnp.zeros(s.shape, F32) for s in (state_ssm, state_hgrn, state_rwkv))
    normed = False
    for layer in range(depth):
        p = dict(conv_w=conv_w[layer], conv_b=conv_b[layer], dt_bias=dt_bias[layer], a_log=a_log[layer],
                 d_skip=d_skip[layer], ssm_norm_g=ssm_norm_g[layer], shift_mu=shift_mu[layer],
                 w0=w0[layer], w2=w2[layer], a0=a0[layer], a2=a2[layer], g2=g2[layer], k_k=k_k[layer],
                 k_a=k_a[layer], r_k=r_k[layer].reshape(-1), lnx_w=lnx_w[layer], lnx_b=lnx_b[layer])
        sp = _ssd_params(p)
        rp = _rwkv_params(p)
        lb = lower_bounds[layer]
        lbp = jnp.stack([1.0 - lb, jnp.log(lb), jnp.log1p(-lb)])
        hng = hgrn_norm_g[layer].reshape(1, B_HEAD)
        w_in_l = _layout_w_in(w_in, layer)
        w_out_l = w_out[layer].astype(BF16)
        j = layer // 2
        routed = layer % 2 == 1
        if routed:
            rw_f = jnp.pad(router_w[j], ((0, 0), (0, LANES - N_EXPERTS)))
            rw_hi = rw_f.astype(BF16)
            rw = jnp.concatenate([rw_hi, (rw_f - rw_hi.astype(F32)).astype(BF16)], axis=1)
            h_all = jnp.zeros((tp + bs, D_MODEL), BF16)
            l_all = jnp.zeros((tp + bs, LANES), F32)

        proj = _norm_matmul(xp, norm1_g[layer], w_in_l, 256).reshape(bp, seq, P_COLS)
        ya, ssm_p = _ssd_prompt(proj, sp)
        yb, hgrn_p = _hgrn_prompt(proj, lbp, hng)
        yc, rwkv_p = _rwkv_prompt(proj, rp)
        xp, hp, l_all = _out_proj(ya.reshape(tp, A_DIM), yb.reshape(tp, B_DIM), yc.reshape(tp, C_DIM),
                                  w_out_l, xp, norm2_g[layer], 256, (rw, h_all, l_all, 0) if routed else None)
        outs["p_ssm"].append(ssm_p)
        outs["p_conv"].append(_xbc_cols(proj[:, seq - (CONV_W - 1):, :]))
        outs["p_hgrn"].append(hgrn_p)
        outs["p_rwkv"].append(rwkv_p)
        outs["p_shift"].append(proj[:, seq - 1, P_R:P_R + C_COLS])

        projs = _norm_matmul(xs, norm1_g[layer], w_in_l, 128)
        ya, ssm_s = _ssd_step(projs, state_conv[layer], state_ssm, ssm_s, layer, p, sp)
        yb, hgrn_s = _hgrn_step(projs, state_hgrn, hgrn_s, layer, lbp, hng)
        yc, rwkv_s = _rwkv_step(projs, state_shift[layer], state_rwkv, rwkv_s, layer, rp)
        xs, hs, l_all = _out_proj(ya, yb, yc, w_out_l, xs, norm2_g[layer], 128,
                                  (rw, hp, l_all, tp) if routed else None)
        outs["s_conv"].append(jnp.concatenate([state_conv[layer][:, 1:, :], _xbc_cols(projs)[:, None, :]], axis=1))
        outs["s_shift"].append(projs[:, P_R:P_R + C_COLS])

        if not routed:
            xp = _ffn(xp, hp, ffn_w_gate[j], ffn_w_up[j], ffn_w_down[j])
            xs = _ffn(xs, hs, ffn_w_gate[j], ffn_w_up[j], ffn_w_down[j])
        else:
            normed = layer == depth - 1
            xp, xs = _moe(xp, xs, hs, l_all, exp_w_gate[j], exp_w_up[j], exp_w_down[j], final_norm_g, normed)

    if not normed:
        xp, xs = _final_norm(xp, final_norm_g, 512), _final_norm(xs, final_norm_g, 128)
    y_prompt = xp.reshape(bp, seq, D_MODEL)
    y_sample = xs.reshape(bs, 1, D_MODEL)
    st = {k: jnp.stack(v) for k, v in outs.items()}
    return (y_prompt, y_sample, st["p_ssm"], st["p_conv"], st["p_hgrn"], st["p_rwkv"], st["p_shift"],
            ssm_s, st["s_conv"], hgrn_s, rwkv_s, st["s_shift"])
```

```python
import functools

import jax
import jax.numpy as jnp
from jax import lax
from jax.experimental import pallas as pl
from jax.experimental.pallas import tpu as pltpu

F32 = jnp.float32
BF16 = jnp.bfloat16
HI = lax.Precision.HIGHEST

D_MODEL = 2048
A_DIM, A_HEAD, A_HEADS, A_GROUPS, A_STATE = 1024, 64, 16, 2, 128
A_GN = A_GROUPS * A_STATE
CONV_W = 4
CONV_CH = A_DIM + 2 * A_GN
B_DIM, B_HEAD, B_HEADS = 512, 128, 4
C_DIM, C_HEAD, C_HEADS = 512, 64, 8
C_LR_W, C_LR_A, C_LR_G = 64, 64, 128
C_LR = C_LR_W + C_LR_A + C_LR_G
C_COLS = 3 * C_DIM + C_LR
IN_COLS = A_DIM + CONV_CH + A_HEADS + 4 * B_DIM + C_COLS
D_FF, N_EXPERTS, TOP_K, E_FF = 5504, 8, 2, 7168
NORM_EPS, GN_EPS, L2_EPS = 1e-6, 64e-5, 1e-12

P_Z, P_X, P_B, P_C = 0, 1024, 2048, 2304
P_Q, P_F, P_I, P_G = 2560, 3072, 3584, 4096
P_R, P_K, P_V, P_LR, P_DT, P_COLS = 4608, 5120, 5632, 6144, 6400, 6656
LANES = 128

SSD_CHUNK = 256
HGRN_BLOCK, HGRN_CHUNK = 256, 16
RWKV_BLOCK, RWKV_CHUNK = 256, 64
STEP_BB = 8

FFN_TF, FFN_TN, FFN_TM_UP, FFN_TM_DOWN = 512, 1024, 1024, 256
MOE_BM, MOE_SUB, MOE_TF, MOE_TN, MOE_KP = 512, 256, 1024, 1024, 2

NT_DIMS = (((1,), (1,)), ((), ()))
TN_DIMS = (((0,), (0,)), ((), ()))


def _cparams(sem, vmem_mb=48):
    return pltpu.CompilerParams(dimension_semantics=sem, vmem_limit_bytes=vmem_mb * 1024 * 1024)


def _dot(a, b):
    return jnp.dot(a.astype(BF16), b.astype(BF16), preferred_element_type=F32)


def _dot_nt(a, b):
    return lax.dot_general(a.astype(BF16), b.astype(BF16), NT_DIMS, preferred_element_type=F32)


def _dot_tn(a, b):
    return lax.dot_general(a.astype(BF16), b.astype(BF16), TN_DIMS, preferred_element_type=F32)


def _dot_hi(a, b):
    return jnp.dot(a, b, precision=HI, preferred_element_type=F32)


def _dot_nt_hi(a, b):
    return lax.dot_general(a, b, NT_DIMS, precision=HI, preferred_element_type=F32)


def _silu(x):
    return x * jax.nn.sigmoid(x)


def _softplus(x):
    return jnp.maximum(x, 0.0) + jnp.log1p(jnp.exp(-jnp.abs(x)))


def _iota(shape, dim):
    return lax.broadcasted_iota(jnp.int32, shape, dim)


def _rms(x, g):
    return x * lax.rsqrt(jnp.mean(x * x, axis=-1, keepdims=True) + NORM_EPS) * g


def _full(shape):
    nd = len(shape)
    return pl.BlockSpec(shape, lambda *_: (0,) * nd)


def _norm_matmul_kernel(x_ref, g_ref, w_ref, o_ref):
    h = _rms(x_ref[...], g_ref[...]).astype(BF16)
    o_ref[...] = jnp.dot(h, w_ref[...], preferred_element_type=F32)


def _norm_matmul(x, g, w, tm):
    m, k = x.shape
    n = w.shape[1]
    return pl.pallas_call(
        _norm_matmul_kernel,
        grid=(m // tm,),
        in_specs=[pl.BlockSpec((tm, k), lambda i: (i, 0)),
                  pl.BlockSpec((1, k), lambda i: (0, 0)),
                  pl.BlockSpec((k, n), lambda i: (0, 0), pipeline_mode=pl.Buffered(1))],
        out_specs=pl.BlockSpec((tm, n), lambda i: (i, 0)),
        out_shape=jax.ShapeDtypeStruct((m, n), F32),
        compiler_params=_cparams(("parallel",), 56),
        name="norm_matmul",
    )(x, g.reshape(1, k), w)


def _out_proj_kernel(routed, ya_ref, yb_ref, yc_ref, w_ref, r_ref, g_ref, *rest):
    x = (r_ref[...]
         + jnp.dot(ya_ref[...], w_ref[0:A_DIM, :], preferred_element_type=F32)
         + jnp.dot(yb_ref[...], w_ref[A_DIM:A_DIM + B_DIM, :], preferred_element_type=F32)
         + jnp.dot(yc_ref[...], w_ref[A_DIM + B_DIM:, :], preferred_element_type=F32))
    h = _rms(x, g_ref[...])
    h_hi = h.astype(BF16)
    if routed:
        rw_ref, _, _, x_ref, h_ref, l_ref = rest
        h_lo = (h - h_hi.astype(F32)).astype(BF16)
        rw = rw_ref[...]
        l2 = jnp.dot(h_hi, rw, preferred_element_type=F32)
        l_ref[...] = (l2[:, :LANES] + l2[:, LANES:]
                      + jnp.dot(h_lo, rw[:, :LANES], preferred_element_type=F32))
        h_ref[...] = h
    else:
        x_ref, h_ref = rest
        h_ref[...] = h_hi
    x_ref[...] = x


def _out_proj(ya, yb, yc, w, res, g, tm, routed=None):
    m = ya.shape[0]
    n = w.shape[1]
    row = lambda width: pl.BlockSpec((tm, width), lambda i: (i, 0))
    in_specs = [row(A_DIM), row(B_DIM), row(C_DIM), _full(w.shape), row(n), _full((1, n))]
    args = [ya, yb, yc, w, res, g.reshape(1, n)]
    aliases = {}
    if routed is None:
        out_specs = [row(n), row(n)]
        out_shape = [jax.ShapeDtypeStruct((m, n), F32), jax.ShapeDtypeStruct((m, n), BF16)]
    else:
        rw, h_buf, l_buf, row0 = routed
        blk0 = row0 // tm
        off = lambda width: pl.BlockSpec((tm, width), lambda i: (i + blk0, 0))
        in_specs += [_full(rw.shape), pl.BlockSpec(memory_space=pl.ANY), pl.BlockSpec(memory_space=pl.ANY)]
        args += [rw, h_buf, l_buf]
        out_specs = [row(n), off(n), off(LANES)]
        out_shape = [jax.ShapeDtypeStruct((m, n), F32), jax.ShapeDtypeStruct(h_buf.shape, F32),
                     jax.ShapeDtypeStruct(l_buf.shape, F32)]
        aliases = {len(args) - 2: 1, len(args) - 1: 2}
    outs = pl.pallas_call(
        functools.partial(_out_proj_kernel, routed is not None),
        grid=(m // tm,),
        in_specs=in_specs, out_specs=out_specs, out_shape=out_shape,
        input_output_aliases=aliases,
        compiler_params=_cparams(("parallel",)),
        name="out_proj",
    )(*args)
    return outs if routed is not None else (*outs, None)


def _ffn_up_kernel(h_ref, wg_ref, wu_ref, o_ref, wg_s, wu_s):
    @pl.when(pl.program_id(1) == 0)
    def _():
        wg_s[...] = wg_ref[...].astype(BF16)
        wu_s[...] = wu_ref[...].astype(BF16)

    h = h_ref[...]
    a = jnp.dot(h, wg_s[...], preferred_element_type=F32)
    u = jnp.dot(h, wu_s[...], preferred_element_type=F32)
    o_ref[...] = (_silu(a) * u).astype(BF16)


def _ffn_down_kernel(h_ref, wd_ref, x_ref, o_ref, wd_s):
    @pl.when(pl.program_id(1) == 0)
    def _():
        wd_s[...] = wd_ref[...].astype(BF16)

    o_ref[...] = x_ref[...] + jnp.dot(h_ref[...], wd_s[...], preferred_element_type=F32)


def _ffn(x, h, wg, wu, wd):
    m, k = h.shape
    ff = wg.shape[1]
    tf, tn = FFN_TF, FFN_TN
    tu = min(m, FFN_TM_UP)
    tm = min(m, FFN_TM_DOWN)
    act = pl.pallas_call(
        _ffn_up_kernel,
        grid=(pl.cdiv(ff, tf), m // tu),
        in_specs=[pl.BlockSpec((tu, k), lambda f, i: (i, 0)),
                  pl.BlockSpec((k, tf), lambda f, i: (0, f)),
                  pl.BlockSpec((k, tf), lambda f, i: (0, f))],
        out_specs=pl.BlockSpec((tu, tf), lambda f, i: (i, f)),
        out_shape=jax.ShapeDtypeStruct((m, ff), BF16),
        scratch_shapes=[pltpu.VMEM((k, tf), BF16)] * 2,
        compiler_params=_cparams(("arbitrary", "arbitrary")),
        name="ffn_up",
    )(h, wg, wu)
    return pl.pallas_call(
        _ffn_down_kernel,
        grid=(k // tn, m // tm),
        in_specs=[pl.BlockSpec((tm, ff), lambda n, i: (i, 0)),
                  pl.BlockSpec((ff, tn), lambda n, i: (0, n), pipeline_mode=pl.Buffered(1)),
                  pl.BlockSpec((tm, tn), lambda n, i: (i, n))],
        out_specs=pl.BlockSpec((tm, tn), lambda n, i: (i, n)),
        out_shape=jax.ShapeDtypeStruct((m, k), F32),
        scratch_shapes=[pltpu.VMEM((ff, tn), BF16)],
        compiler_params=_cparams(("arbitrary", "arbitrary"), 56),
        name="ffn_down",
    )(act, wd, x)


def _moe_rows(nsub, o_ref, compute):
    bm = o_ref.shape[0]
    for ns in range(bm // MOE_SUB + 1):
        @pl.when(nsub == ns)
        def _(rows=ns * MOE_SUB):
            if rows:
                o_ref[0:rows, :] = compute(rows).astype(o_ref.dtype)
            if rows < bm:
                o_ref[rows:, :] = jnp.zeros((bm - rows, o_ref.shape[1]), o_ref.dtype)


def _weight_stream(sched, w_hbms, wbuf, sem, dsts):
    bexp_ref, first_ref, run_ref, nxt_ref, lastrun_ref, nruns_ref = sched
    col, b = pl.program_id(0), pl.program_id(1)
    ncol = pl.num_programs(0)
    width = wbuf.shape[-1]

    def copies(e, c, slot):
        cols = pl.ds(pl.multiple_of(c * width, width), width)
        return [pltpu.make_async_copy(w.at[e, :, cols], wbuf.at[slot, i], sem.at[slot])
                for i, w in enumerate(w_hbms)]

    @pl.when(first_ref[b] == 1)
    def _():
        slot = lax.rem(col * nruns_ref[0] + run_ref[b], 2)
        last = lastrun_ref[b]

        @pl.when((col == 0) & (b == 0))
        def _():
            for cp in copies(bexp_ref[0], 0, 0):
                cp.start()

        @pl.when((last == 0) | (col < ncol - 1))
        def _():
            for cp in copies(nxt_ref[b], col + last, 1 - slot):
                cp.start(priority=1)

        for cp in copies(bexp_ref[b], col, slot):
            cp.wait()
        for i, dst in enumerate(dsts):
            dst[...] = wbuf[slot, i].astype(BF16)


def _moe_up_kernel(*refs):
    sched, (nsub_ref, _, x_ref, wg_hbm, wu_hbm, o_ref, wbuf, sem, wg_s, wu_s) = refs[:6], refs[6:]
    _weight_stream(sched, (wg_hbm, wu_hbm), wbuf, sem, (wg_s, wu_s))

    def compute(rows):
        x = x_ref[0:rows, :]
        a = jnp.dot(x, wg_s[...], preferred_element_type=F32)
        u = jnp.dot(x, wu_s[...], preferred_element_type=F32)
        return _silu(a) * u

    _moe_rows(nsub_ref[pl.program_id(1)], o_ref, compute)


def _moe_down_kernel(*refs):
    sched, (nsub_ref, _, h_ref, wd_hbm, o_ref, wbuf, sem, wd_s) = refs[:6], refs[6:]
    bexp_ref, first_ref, _, nxt_ref, lastrun_ref, _ = sched
    col, b, kk = pl.program_id(0), pl.program_id(1), pl.program_id(2)
    ncol = pl.num_programs(0)
    parts, kp, width = wbuf.shape

    def copies(e, c):
        cols = pl.ds(pl.multiple_of(c * width, width), width)
        return [pltpu.make_async_copy(wd_hbm.at[e, pl.ds(i * kp, kp), cols], wbuf.at[i], sem.at[i])
                for i in range(parts)]

    @pl.when((first_ref[b] == 1) & (kk == 0))
    def _():
        last = lastrun_ref[b]

        @pl.when((col == 0) & (b == 0))
        def _():
            for cp in copies(bexp_ref[0], 0):
                cp.start()

        for cp in copies(bexp_ref[b], col):
            cp.wait()
        rc = 512

        def cast(r, carry):
            rows = pl.ds(pl.multiple_of(r * rc, rc), rc)
            for i in range(parts):
                wd_s[i, rows, :] = wbuf[i, rows, :].astype(BF16)
            return carry

        lax.fori_loop(0, kp // rc, cast, 0)

        @pl.when((last == 0) | (col < ncol - 1))
        def _():
            for cp in copies(nxt_ref[b], col + last):
                cp.start(priority=1)

    valid = nsub_ref[b] > 0
    for i in range(parts):
        @pl.when(valid & (kk == i))
        def _(i=i):
            acc = jnp.dot(h_ref[...], wd_s[i], preferred_element_type=F32)
            if i == 0:
                o_ref[...] = acc
            else:
                o_ref[...] += acc

    @pl.when(jnp.logical_not(valid) & (kk == 0))
    def _():
        o_ref[...] = jnp.zeros_like(o_ref)


def _run_schedule(bexp, first):
    nb = bexp.shape[0]
    ids = jnp.arange(nb, dtype=jnp.int32)
    run = jnp.cumsum(first).astype(jnp.int32) - 1
    nruns = run[-1] + 1
    starts = jnp.where(first == 1, ids, nb)
    nxt_start = jnp.concatenate([lax.cummin(starts[::-1])[::-1][1:], jnp.full((1,), nb, jnp.int32)])
    nxt = jnp.where(nxt_start < nb, bexp[jnp.minimum(nxt_start, nb - 1)], bexp[0]).astype(jnp.int32)
    lastrun = (run == nruns - 1).astype(jnp.int32)
    return bexp, first, run, nxt, lastrun, nruns.reshape(1)


def _moe_experts(xb, bexp, first, nsub, wg, wu, wd):
    n_slots, k = xb.shape
    ff = wg.shape[2]
    nb = n_slots // MOE_BM
    tf, tn = MOE_TF, MOE_TN
    hbm = pl.BlockSpec(memory_space=pl.ANY)

    def src_blocks(ns):
        return lax.cummax(jnp.where(ns > 0, jnp.arange(ns.shape[0], dtype=jnp.int32), 0))

    act = pl.pallas_call(
        _moe_up_kernel,
        grid_spec=pltpu.PrefetchScalarGridSpec(
            num_scalar_prefetch=8,
            grid=(ff // tf, nb),
            in_specs=[pl.BlockSpec((MOE_BM, k), lambda f, b, *s: (s[7][b], 0)), hbm, hbm],
            out_specs=pl.BlockSpec((MOE_BM, tf), lambda f, b, *s: (b, f)),
            scratch_shapes=[pltpu.VMEM((2, 2, k, tf), F32), pltpu.SemaphoreType.DMA((2,)),
                            pltpu.VMEM((k, tf), BF16), pltpu.VMEM((k, tf), BF16)],
        ),
        out_shape=jax.ShapeDtypeStruct((n_slots, ff), BF16),
        compiler_params=_cparams(("arbitrary", "arbitrary"), 56),
        name="moe_up",
    )(*_run_schedule(bexp, first), nsub, src_blocks(nsub), xb, wg, wu)
    per = MOE_BM // MOE_SUB
    bexp_d = jnp.repeat(bexp, per)
    first_d = jnp.concatenate([jnp.ones((1,), jnp.int32), (bexp_d[1:] != bexp_d[:-1]).astype(jnp.int32)])
    nsub_d = jnp.clip(jnp.repeat(nsub, per) - jnp.tile(jnp.arange(per, dtype=jnp.int32), nb), 0, 1)
    return pl.pallas_call(
        _moe_down_kernel,
        grid_spec=pltpu.PrefetchScalarGridSpec(
            num_scalar_prefetch=8,
            grid=(k // tn, nb * per, MOE_KP),
            in_specs=[pl.BlockSpec((MOE_SUB, ff // MOE_KP), lambda n, b, kk, *s: (s[7][b], kk)), hbm],
            out_specs=pl.BlockSpec((MOE_SUB, tn), lambda n, b, kk, *s: (b, n)),
            scratch_shapes=[pltpu.VMEM((MOE_KP, ff // MOE_KP, tn), F32), pltpu.SemaphoreType.DMA((MOE_KP,)),
                            pltpu.VMEM((MOE_KP, ff // MOE_KP, tn), BF16)],
        ),
        out_shape=jax.ShapeDtypeStruct((n_slots, k), F32),
        compiler_params=_cparams(("arbitrary", "arbitrary", "arbitrary"), 56),
        name="moe_down",
    )(*_run_schedule(bexp_d, first_d), nsub_d, src_blocks(nsub_d), act, wd)


def _final_norm_kernel(x_ref, g_ref, o_ref):
    o_ref[...] = _rms(x_ref[...], g_ref[...])


def _final_norm(x, g, tm):
    m, k = x.shape
    return pl.pallas_call(
        _final_norm_kernel,
        grid=(m // tm,),
        in_specs=[pl.BlockSpec((tm, k), lambda i: (i, 0)),
                  pl.BlockSpec((1, k), lambda i: (0, 0))],
        out_specs=pl.BlockSpec((tm, k), lambda i: (i, 0)),
        out_shape=jax.ShapeDtypeStruct((m, k), F32),
        compiler_params=_cparams(("parallel",)),
        name="final_norm",
    )(x, g.reshape(1, k))


def _row_copy(src_hbm, row, dst, r, sem):
    return pltpu.make_async_copy(src_hbm.at[pl.ds(row, 1), :], dst.at[pl.ds(r, 1), :], sem)


def _moe_gather_kernel(idx_ref, h_hbm, o_ref, buf, sem):
    rows = buf.shape[0]
    base = pl.program_id(0) * rows

    def issue(r, carry):
        _row_copy(h_hbm, idx_ref[base + 2 * r], buf, 2 * r, sem).start(priority=0)
        _row_copy(h_hbm, idx_ref[base + 2 * r + 1], buf, 2 * r + 1, sem).start(priority=1)
        return carry

    lax.fori_loop(0, rows // 2, issue, 0)
    pltpu.make_async_copy(buf, buf, sem).wait()
    o_ref[...] = buf[...].astype(BF16)


def _moe_gather(idx, h):
    n_slots = idx.shape[0]
    k = h.shape[1]
    rows = MOE_SUB
    return pl.pallas_call(
        _moe_gather_kernel,
        grid_spec=pltpu.PrefetchScalarGridSpec(
            num_scalar_prefetch=1,
            grid=(n_slots // rows,),
            in_specs=[pl.BlockSpec(memory_space=pl.ANY)],
            out_specs=pl.BlockSpec((rows, k), lambda i, idx: (i, 0)),
            scratch_shapes=[pltpu.VMEM((rows, k), F32), pltpu.SemaphoreType.DMA],
        ),
        out_shape=jax.ShapeDtypeStruct((n_slots, k), BF16),
        compiler_params=_cparams(("arbitrary",)),
        name="moe_gather",
    )(idx, h)


def _moe_combine_kernel(final, s0_ref, s1_ref, x_ref, gate_ref, g_ref, yb_hbm, o_ref, buf, sem):
    rows = x_ref.shape[0]
    base = pl.program_id(0) * rows

    def issue(r, carry):
        _row_copy(yb_hbm, s0_ref[base + r], buf.at[0], r, sem).start(priority=0)
        _row_copy(yb_hbm, s1_ref[base + r], buf.at[1], r, sem).start(priority=1)
        return carry

    lax.fori_loop(0, rows, issue, 0)
    pltpu.make_async_copy(buf, buf, sem).wait()
    gate = gate_ref[...]
    y = x_ref[...] + gate[:, 0:1] * buf[0] + gate[:, 1:2] * buf[1]
    o_ref[...] = _rms(y, g_ref[...]) if final else y


def _moe_combine(x, yb, slot_of, gate, g, final):
    m, k = x.shape
    rows = min(m, 256)
    gate_p = jnp.pad(gate, ((0, 0), (0, LANES - TOP_K)))
    return pl.pallas_call(
        functools.partial(_moe_combine_kernel, final),
        grid_spec=pltpu.PrefetchScalarGridSpec(
            num_scalar_prefetch=2,
            grid=(m // rows,),
            in_specs=[pl.BlockSpec((rows, k), lambda i, s0, s1: (i, 0)),
                      pl.BlockSpec((rows, LANES), lambda i, s0, s1: (i, 0)),
                      pl.BlockSpec((1, k), lambda i, s0, s1: (0, 0)),
                      pl.BlockSpec(memory_space=pl.ANY)],
            out_specs=pl.BlockSpec((rows, k), lambda i, s0, s1: (i, 0)),
            scratch_shapes=[pltpu.VMEM((TOP_K, rows, k), F32), pltpu.SemaphoreType.DMA],
        ),
        out_shape=jax.ShapeDtypeStruct((m, k), F32),
        compiler_params=_cparams(("arbitrary",)),
        name="moe_combine",
    )(slot_of[:, 0], slot_of[:, 1], x, gate_p, g.reshape(1, k), yb)


def _ssd_gate_norm(y, z, ng):
    y = y * _silu(z)
    gw = A_DIM // A_GROUPS
    outs = []
    for g in range(A_GROUPS):
        yg = y[:, g * gw:(g + 1) * gw]
        outs.append(yg * lax.rsqrt(jnp.mean(yg * yg, axis=-1, keepdims=True) + NORM_EPS))
    return jnp.concatenate(outs, axis=-1) * ng


def _ssd_prompt_kernel(z_ref, x_ref, b_ref, c_ref, dt_ref, cwx, cwb, cwc, cbx, cbb, cbc,
                       dtb, aneg, dsk, ng, y_ref, st_ref, h_scr, xpx, xpb, xpc, yacc):
    c = pl.program_id(1)
    ch = SSD_CHUNK

    @pl.when(c == 0)
    def _():
        h_scr[...] = jnp.zeros_like(h_scr)
        xpx[0:8, :] = jnp.zeros((8, A_DIM), F32)
        xpb[0:8, :] = jnp.zeros((8, A_STATE * A_GROUPS), F32)
        xpc[0:8, :] = jnp.zeros((8, A_STATE * A_GROUPS), F32)

    xpx[8:8 + ch, :] = x_ref[...]
    xpb[8:8 + ch, :] = b_ref[...]
    xpc[8:8 + ch, :] = c_ref[...]

    def conv(xp, cw, cb):
        acc = cb[...] + xp[pl.ds(8 - (CONV_W - 1), ch), :] * cw[0:1, :]
        for tap in range(1, CONV_W):
            acc = acc + xp[pl.ds(8 - (CONV_W - 1) + tap, ch), :] * cw[tap:tap + 1, :]
        return _silu(acc)

    ux = conv(xpx, cwx, cbx)
    ub = conv(xpb, cwb, cbb)
    uc = conv(xpc, cwc, cbc)
    xpx[0:8, :] = xpx[ch:ch + 8, :]
    xpb[0:8, :] = xpb[ch:ch + 8, :]
    xpc[0:8, :] = xpc[ch:ch + 8, :]

    dt = _softplus(dt_ref[...] + dtb[...])
    la = dt * aneg[...]
    tri = (_iota((ch, ch), 1) <= _iota((ch, ch), 0)).astype(F32)
    cum = _dot_hi(tri, la)
    eye = (_iota((A_HEADS, LANES), 0) == _iota((A_HEADS, LANES), 1)).astype(F32)
    cum_r = _dot_nt_hi(eye, cum)
    dt_r = _dot_nt_hi(eye, dt)
    cl = cum[ch - 1:ch, :]
    tail = jnp.exp(cl - cum) * dt
    ecum = jnp.exp(cum)
    ecl = jnp.exp(cl)
    causal = _iota((ch, ch), 1) <= _iota((ch, ch), 0)
    rep = A_HEADS // A_GROUPS

    for g in range(A_GROUPS):
        bg = ub[:, g * A_STATE:(g + 1) * A_STATE].astype(BF16)
        cg = uc[:, g * A_STATE:(g + 1) * A_STATE].astype(BF16)
        cb_g = _dot_nt(cg, bg)
        for h in range(g * rep, (g + 1) * rep):
            hs = slice(h * A_HEAD, (h + 1) * A_HEAD)
            seg = cum[:, h:h + 1] - cum_r[h:h + 1, :]
            w = jnp.exp(jnp.where(causal, seg, -jnp.inf)) * cb_g * dt_r[h:h + 1, :]
            xh = ux[:, hs]
            hst = h_scr[h]
            y_h = _dot(w, xh) + _dot_nt(cg, hst) * ecum[:, h:h + 1]
            yacc[:, hs] = y_h
            h_scr[h] = hst * ecl[:, h:h + 1] + _dot_tn(xh * tail[:, h:h + 1], bg)

    y = yacc[...] + dsk[...] * ux
    y_ref[...] = _ssd_gate_norm(y, z_ref[...], ng[...]).astype(BF16)

    @pl.when(c == pl.num_programs(1) - 1)
    def _():
        st_ref[...] = h_scr[...]


def _ssd_params(p):
    cw, cb = p["conv_w"], p["conv_b"].reshape(1, CONV_CH)
    pad = LANES - A_HEADS
    return dict(
        cwx=cw[:, :A_DIM], cwb=cw[:, A_DIM:A_DIM + A_GN], cwc=cw[:, A_DIM + A_GN:],
        cbx=cb[:, :A_DIM], cbb=cb[:, A_DIM:A_DIM + A_GN], cbc=cb[:, A_DIM + A_GN:],
        dtb=jnp.pad(p["dt_bias"], (0, pad)).reshape(1, LANES),
        aneg=jnp.pad(-jnp.exp(p["a_log"]), (0, pad)).reshape(1, LANES),
        dsk=jnp.repeat(p["d_skip"], A_HEAD).reshape(1, A_DIM),
        ng=p["ssm_norm_g"].reshape(1, A_DIM),
    )


def _ssd_prompt(proj, sp):
    bsz, seq, _ = proj.shape
    ch = SSD_CHUNK
    gn = A_GN

    def col(width, off):
        return pl.BlockSpec((None, ch, width), lambda b, c: (b, c, off // width))

    names = ("cwx", "cwb", "cwc", "cbx", "cbb", "cbc", "dtb", "aneg", "dsk", "ng")
    y, st = pl.pallas_call(
        _ssd_prompt_kernel,
        grid=(bsz, seq // ch),
        in_specs=[col(A_DIM, P_Z), col(A_DIM, P_X), col(gn, P_B), col(gn, P_C), col(LANES, P_DT)]
                 + [_full(sp[n].shape) for n in names],
        out_specs=[pl.BlockSpec((None, ch, A_DIM), lambda b, c: (b, c, 0)),
                   pl.BlockSpec((None, A_HEADS, A_HEAD, A_STATE), lambda b, c: (b, 0, 0, 0))],
        out_shape=[jax.ShapeDtypeStruct((bsz, seq, A_DIM), BF16),
                   jax.ShapeDtypeStruct((bsz, A_HEADS, A_HEAD, A_STATE), F32)],
        scratch_shapes=[pltpu.VMEM((A_HEADS, A_HEAD, A_STATE), F32),
                        pltpu.VMEM((ch + 8, A_DIM), F32),
                        pltpu.VMEM((ch + 8, gn), F32),
                        pltpu.VMEM((ch + 8, gn), F32),
                        pltpu.VMEM((ch, A_DIM), F32)],
        compiler_params=_cparams(("parallel", "arbitrary")),
        name="ssd_prompt",
    )(proj, proj, proj, proj, proj, *[sp[n] for n in names])
    return y, st


def _ssd_step_kernel(z_ref, x_ref, b_ref, c_ref, dt_ref, cs_ref, cw, cb, dtb, anegx, expand, dsk, ng,
                     h_ref, _, y_ref, ho_ref, y_scr):
    bb = STEP_BB
    xbc = (x_ref[...], b_ref[...], c_ref[...])
    offs = (0, A_DIM, A_DIM + A_GN, CONV_CH)
    u = []
    for i in range(3):
        sl = slice(offs[i], offs[i + 1])
        acc = cb[:, sl] + xbc[i] * cw[CONV_W - 1:CONV_W, sl]
        for tap in range(CONV_W - 1):
            acc = acc + cs_ref[tap][:, sl] * cw[tap:tap + 1, sl]
        u.append(_silu(acc))
    ux, ub, uc = u
    dt = _softplus(dt_ref[...] + dtb[...])
    dtx = _dot_hi(dt, expand[...])
    dec_t = jnp.exp(dtx * anegx[...]).T
    xdt_t = (ux * dtx).T
    gw = A_DIM // A_GROUPS
    for b in range(bb):
        for g in range(A_GROUPS):
            rs = slice(g * gw, (g + 1) * gw)
            ns = slice(g * A_STATE, (g + 1) * A_STATE)
            hg = h_ref[b, rs, :] * dec_t[rs, b:b + 1] + xdt_t[rs, b:b + 1] * ub[b:b + 1, ns]
            ho_ref[b, rs, :] = hg
            y_scr[b:b + 1, rs] = _dot_nt(uc[:, ns], hg)[b:b + 1, :]
    y = y_scr[...] + dsk[...] * ux
    y_ref[...] = _ssd_gate_norm(y, z_ref[...], ng[...]).astype(BF16)


def _ssd_step(proj, conv_state, h_all, h_new, layer, p, sp):
    bsz = proj.shape[0]
    bb = STEP_BB
    cs = jnp.swapaxes(conv_state, 0, 1)
    hflat = h_all.reshape(h_all.shape[0], bsz, A_DIM, A_STATE)
    expand = (jnp.arange(LANES)[:, None] == (jnp.arange(A_DIM) // A_HEAD)[None, :]).astype(F32)
    anegx = jnp.repeat(-jnp.exp(p["a_log"]), A_HEAD).reshape(1, A_DIM)
    cw, cb = p["conv_w"], p["conv_b"].reshape(1, CONV_CH)

    def col(width, off):
        return pl.BlockSpec((bb, width), lambda i: (i, off // width))

    y, hn = pl.pallas_call(
        _ssd_step_kernel,
        grid=(bsz // bb,),
        in_specs=[col(A_DIM, P_Z), col(A_DIM, P_X), col(A_GN, P_B), col(A_GN, P_C), col(LANES, P_DT),
                  pl.BlockSpec((CONV_W - 1, bb, CONV_CH), lambda i: (0, i, 0)),
                  _full(cw.shape), _full(cb.shape), _full(sp["dtb"].shape), _full(anegx.shape),
                  _full(expand.shape), _full(sp["dsk"].shape), _full(sp["ng"].shape),
                  pl.BlockSpec((None, bb, A_DIM, A_STATE), lambda i: (layer, i, 0, 0)),
                  pl.BlockSpec(memory_space=pl.ANY)],
        out_specs=[pl.BlockSpec((bb, A_DIM), lambda i: (i, 0)),
                   pl.BlockSpec((None, bb, A_DIM, A_STATE), lambda i: (layer, i, 0, 0))],
        out_shape=[jax.ShapeDtypeStruct((bsz, A_DIM), BF16),
                   jax.ShapeDtypeStruct(hflat.shape, F32)],
        input_output_aliases={14: 1},
        scratch_shapes=[pltpu.VMEM((bb, A_DIM), F32)],
        compiler_params=_cparams(("parallel",)),
        name="ssd_step",
    )(proj, proj, proj, proj, proj, cs, cw, cb, sp["dtb"], anegx, expand, sp["dsk"], sp["ng"], hflat,
      h_new.reshape(hflat.shape))
    return y, hn.reshape(h_all.shape)


def _hgrn_gates(q, f, lbp):
    qf = _silu(q)
    kf = lbp[0:1, :] * jax.nn.sigmoid(-f)
    log_sig = jnp.minimum(f, 0.0) - jnp.log1p(jnp.exp(-jnp.abs(f)))
    a = lbp[1:2, :]
    b = lbp[2:3, :] + log_sig
    logf = jnp.maximum(a, b) + jnp.log1p(jnp.exp(-jnp.abs(a - b)))
    return qf, kf, logf


def _hgrn_out(o, g, ng):
    outs = []
    for h in range(B_HEADS):
        oh = o[:, h * B_HEAD:(h + 1) * B_HEAD]
        outs.append(oh * lax.rsqrt(jnp.mean(oh * oh, axis=-1, keepdims=True) + NORM_EPS) * ng)
    return jnp.concatenate(outs, axis=-1) * _silu(g)


def _hgrn_prompt_kernel(q_ref, f_ref, i_ref, g_ref, lbp, ng, o_ref, st_ref,
                        s_scr, cum_scr, q_scr, k_scr, o_scr):
    c = pl.program_id(1)
    tb, cs = HGRN_BLOCK, HGRN_CHUNK

    @pl.when(c == 0)
    def _():
        s_scr[...] = jnp.zeros_like(s_scr)

    qf, kf, logf = _hgrn_gates(q_ref[...], f_ref[...], lbp[...])
    r_i, c_i = _iota((tb, tb), 0), _iota((tb, tb), 1)
    bd = ((r_i // cs == c_i // cs) & (c_i <= r_i)).astype(F32)
    cum_scr[...] = _dot_hi(bd, logf)
    q_scr[...] = qf
    k_scr[...] = kf
    rows = _iota((cs, B_HEAD), 0)

    def chunk(cc, carry):
        r0 = pl.multiple_of(cc * cs, cs)
        for h in range(B_HEADS):
            hs = slice(h * B_HEAD, (h + 1) * B_HEAD)
            cu = cum_scr[pl.ds(r0, cs), hs]
            q = q_scr[pl.ds(r0, cs), hs]
            k = k_scr[pl.ds(r0, cs), hs]
            v = i_ref[pl.ds(r0, cs), hs]
            st = s_scr[h]
            o = _dot_nt(q * jnp.exp(cu), st)
            for s in range(cs):
                d = jnp.exp(jnp.where(rows >= s, cu - cu[s:s + 1, :], -jnp.inf))
                att = jnp.sum(q * k[s:s + 1, :] * d, axis=-1, keepdims=True)
                o = o + att * v[s:s + 1, :]
            o_scr[pl.ds(r0, cs), hs] = o
            cl = cu[cs - 1:cs, :]
            s_scr[h] = st * jnp.exp(cl) + _dot_tn(v, k * jnp.exp(cl - cu))
        return carry

    lax.fori_loop(0, tb // cs, chunk, 0)
    o_ref[...] = _hgrn_out(o_scr[...], g_ref[...], ng[...]).astype(BF16)

    @pl.when(c == pl.num_programs(1) - 1)
    def _():
        for h in range(B_HEADS):
            st_ref[h] = s_scr[h].T


def _hgrn_prompt(proj, lbp, ng):
    bsz, seq, _ = proj.shape
    tb = HGRN_BLOCK

    def col(off):
        return pl.BlockSpec((None, tb, B_DIM), lambda b, c: (b, c, off // B_DIM))

    o, st = pl.pallas_call(
        _hgrn_prompt_kernel,
        grid=(bsz, seq // tb),
        in_specs=[col(P_Q), col(P_F), col(P_I), col(P_G), _full(lbp.shape), _full(ng.shape)],
        out_specs=[pl.BlockSpec((None, tb, B_DIM), lambda b, c: (b, c, 0)),
                   pl.BlockSpec((None, B_HEADS, B_HEAD, B_HEAD), lambda b, c: (b, 0, 0, 0))],
        out_shape=[jax.ShapeDtypeStruct((bsz, seq, B_DIM), BF16),
                   jax.ShapeDtypeStruct((bsz, B_HEADS, B_HEAD, B_HEAD), F32)],
        scratch_shapes=[pltpu.VMEM((B_HEADS, B_HEAD, B_HEAD), F32)]
                       + [pltpu.VMEM((tb, B_DIM), F32)] * 4,
        compiler_params=_cparams(("parallel", "arbitrary")),
        name="hgrn_prompt",
    )(proj, proj, proj, proj, lbp, ng)
    return o, st


def _hgrn_step_kernel(q_ref, f_ref, i_ref, g_ref, lbp, ng, s_ref, _, o_ref, so_ref, o_scr):
    bb = STEP_BB
    qf, kf, logf = _hgrn_gates(q_ref[...], f_ref[...], lbp[...])
    vf = i_ref[...]
    dec_t = jnp.exp(logf).T
    k_t = kf.T
    q_t = qf.T
    for b in range(bb):
        for h in range(B_HEADS):
            hs = slice(h * B_HEAD, (h + 1) * B_HEAD)
            sh = s_ref[b, hs, :] * dec_t[hs, b:b + 1] + k_t[hs, b:b + 1] * vf[b:b + 1, hs]
            so_ref[b, hs, :] = sh
            o_scr[b:b + 1, hs] = jnp.sum(q_t[hs, b:b + 1] * sh, axis=0, keepdims=True)
    o_ref[...] = _hgrn_out(o_scr[...], g_ref[...], ng[...]).astype(BF16)


def _hgrn_step(proj, s_all, s_new, layer, lbp, ng):
    bsz = proj.shape[0]
    bb = STEP_BB
    sflat = s_all.reshape(s_all.shape[0], bsz, B_DIM, B_HEAD)

    def col(off):
        return pl.BlockSpec((bb, B_DIM), lambda i: (i, off // B_DIM))

    o, sn = pl.pallas_call(
        _hgrn_step_kernel,
        grid=(bsz // bb,),
        in_specs=[col(P_Q), col(P_F), col(P_I), col(P_G), _full(lbp.shape), _full(ng.shape),
                  pl.BlockSpec((None, bb, B_DIM, B_HEAD), lambda i: (layer, i, 0, 0)),
                  pl.BlockSpec(memory_space=pl.ANY)],
        out_specs=[pl.BlockSpec((bb, B_DIM), lambda i: (i, 0)),
                   pl.BlockSpec((None, bb, B_DIM, B_HEAD), lambda i: (layer, i, 0, 0))],
        out_shape=[jax.ShapeDtypeStruct((bsz, B_DIM), BF16),
                   jax.ShapeDtypeStruct(sflat.shape, F32)],
        input_output_aliases={7: 1},
        scratch_shapes=[pltpu.VMEM((bb, B_DIM), F32)],
        compiler_params=_cparams(("parallel",)),
        name="hgrn_step",
    )(proj, proj, proj, proj, lbp, ng, sflat, s_new.reshape(sflat.shape))
    return o, sn.reshape(s_all.shape)


RW_NAMES = ("mu_r", "mu_k", "mu_v", "mu_lr", "w0", "w2", "a0", "a2", "g2", "k_k", "k_a", "r_k")


def _rwkv_params(p):
    mu = p["shift_mu"].reshape(1, C_COLS)
    row = lambda a: a.reshape(1, C_DIM)
    return dict(
        mu_r=mu[:, :C_DIM], mu_k=mu[:, C_DIM:2 * C_DIM], mu_v=mu[:, 2 * C_DIM:3 * C_DIM],
        mu_lr=mu[:, 3 * C_DIM:],
        w0=row(p["w0"]), w2=p["w2"].astype(BF16), a0=row(p["a0"]), a2=p["a2"].astype(BF16),
        g2=p["g2"].astype(BF16), k_k=row(p["k_k"]), k_a=row(p["k_a"]), r_k=row(p["r_k"]),
        lnx_w=row(p["lnx_w"]), lnx_b=row(p["lnx_b"]),
    )


def _head_sum(x):
    return [jnp.sum(x[:, h * C_HEAD:(h + 1) * C_HEAD], axis=-1, keepdims=True) for h in range(C_HEADS)]


def _head_bcast(cols, fn=lambda c: c):
    rows = cols[0].shape[0]
    return jnp.concatenate([jnp.broadcast_to(fn(c), (rows, C_HEAD)) for c in cols], axis=-1)


def _rwkv_prep(r, k, v, lr, pr, pk, pv, plr, prm):
    mu_r, mu_k, mu_v, mu_lr, w0, w2, a0, a2, g2, k_k, k_a, r_k = prm
    r = r + (pr - r) * mu_r
    k = k + (pk - k) * mu_k
    v = v + (pv - v) * mu_v
    lr = lr + (plr - lr) * mu_lr
    wl, al, gl = lr[:, :C_LR_W], lr[:, C_LR_W:C_LR_W + C_LR_A], lr[:, C_LR_W + C_LR_A:]
    wraw = -_softplus(-(w0 + _dot(jnp.tanh(wl), w2))) - 0.5
    ld = -jnp.exp(wraw)
    a = jax.nn.sigmoid(a0 + _dot(al, a2))
    g = _dot(jax.nn.sigmoid(gl), g2)
    kk = k * k_k
    kk = kk * _head_bcast(_head_sum(kk * kk), lambda n2: 1.0 / jnp.maximum(jnp.sqrt(n2), L2_EPS))
    k2 = k * (1.0 + (a - 1.0) * k_a)
    bonus = _head_bcast(_head_sum(r * k2 * r_k)) * v
    return r, k2, v, kk, kk * a, ld, g, bonus


def _rwkv_out(y, bonus, g, lnx_w, lnx_b):
    n = float(C_HEAD)
    mean = _head_bcast(_head_sum(y), lambda s: s / n)
    yc = y - mean
    rstd = _head_bcast(_head_sum(yc * yc), lambda s: lax.rsqrt(s / n + GN_EPS))
    return (yc * rstd * lnx_w + lnx_b + bonus) * g


RW_GH = 4
RW_GW = RW_GH * C_HEAD


def _bd_mask():
    return _iota((RW_GW, RW_GW), 0) // C_HEAD == _iota((RW_GW, RW_GW), 1) // C_HEAD


def _bd_rows(a):
    return jnp.where(_bd_mask(), jnp.concatenate([a] * RW_GH, axis=0), 0.0)


def _bd_lanes(s):
    return jnp.where(_bd_mask(), jnp.concatenate([s] * RW_GH, axis=1), 0.0)


def _bd_fold(m):
    m = jnp.where(_bd_mask(), m, 0.0)
    out = m[:, :C_HEAD]
    for h in range(1, RW_GH):
        out = out + m[:, h * C_HEAD:(h + 1) * C_HEAD]
    return out


def _to_stack(a):
    return jnp.concatenate([a[:, h * C_HEAD:(h + 1) * C_HEAD] for h in range(RW_GH)], axis=0)


def _to_all(s):
    return jnp.concatenate([s[h * C_HEAD:(h + 1) * C_HEAD, :] for h in range(RW_GH)], axis=1)


def _rwkv_a_kernel(r_ref, k_ref, v_ref, lr_ref, rp_ref, kp_ref, vp_ref, lrp_ref, *rest):
    prm = tuple(x[...] for x in rest[:len(RW_NAMES)])
    (kd_ref, rd_ref, bg_ref, ge_ref, bonus_ref, gate_ref, aab_ref, rb_ref, m2_ref, rkv_ref, vk_ref,
     xs_scr, b1_scr, b2_scr, b3_scr, b4_scr) = rest[len(RW_NAMES):]
    c = pl.program_id(1)
    tb, cs = RWKV_BLOCK, RWKV_CHUNK
    first = (c == 0)

    def prev(cur_ref, prev_ref, width):
        xs_scr[8:8 + tb, 0:width] = cur_ref[...]
        xs_scr[7:8, 0:width] = jnp.where(first, 0.0, prev_ref[7:8, :])
        return xs_scr[pl.ds(7, tb), 0:width]

    pr = prev(r_ref, rp_ref, C_DIM)
    pk = prev(k_ref, kp_ref, C_DIM)
    pv = prev(v_ref, vp_ref, C_DIM)
    plr = prev(lr_ref, lrp_ref, C_LR)
    r, k2, v, kk, beta, ld, g, bonus = _rwkv_prep(
        r_ref[...], k_ref[...], v_ref[...], lr_ref[...], pr, pk, pv, plr, prm)
    bonus_ref[...] = bonus
    gate_ref[...] = g

    r_i, c_i = _iota((tb, tb), 0), _iota((tb, tb), 1)
    same = (r_i // cs == c_i // cs)
    lcum = _dot_hi((same & (c_i <= r_i)).astype(F32), ld)
    lend = _dot_hi(same.astype(F32), ld)
    einv = jnp.exp(-lcum)
    eend = jnp.exp(lend - lcum)
    ge_ref[...] = jnp.exp(lend)
    kd_ref[...] = kk * jnp.exp(lcum - ld)
    rd_ref[...] = r * jnp.exp(lcum)
    bg_ref[...] = beta * eend
    b1_scr[...] = beta * einv
    b2_scr[...] = k2 * einv
    b3_scr[...] = k2 * eend
    b4_scr[...] = v
    gw = RW_GW
    tpos, spos = _iota((gw, cs), 0) % cs, _iota((gw, cs), 1)
    strict, incl = tpos > spos, tpos >= spos
    hc = C_HEAD

    def chunk(j, carry):
        r0 = pl.multiple_of(j * cs, cs)
        q0 = pl.multiple_of(j * gw, gw)
        for g in range(C_HEADS // RW_GH):
            gs = slice(g * gw, (g + 1) * gw)
            ls = slice(g * hc, (g + 1) * hc)
            kd = kd_ref[pl.ds(r0, cs), gs]
            rd = rd_ref[pl.ds(r0, cs), gs]
            v = b4_scr[pl.ds(r0, cs), gs]
            lhs = jnp.concatenate([_bd_rows(kd), _bd_rows(rd)], axis=0)
            rhs = jnp.concatenate([b1_scr[pl.ds(r0, cs), gs], b2_scr[pl.ds(r0, cs), gs]], axis=0)
            gm = _dot_nt(lhs, rhs)
            aab_ref[pl.ds(q0, gw), ls] = jnp.where(strict, gm[:gw, :hc], 0.0)
            aak = jnp.where(strict, gm[:gw, hc:], 0.0)
            rb_ref[pl.ds(q0, gw), ls] = jnp.where(incl, gm[gw:, :hc], 0.0)
            rk = jnp.where(incl, gm[gw:, hc:], 0.0)
            mr = _dot(jnp.concatenate([_bd_lanes(aak), _bd_lanes(rk)], axis=0), _to_stack(v))
            m2_ref[pl.ds(q0, gw), ls] = mr[:gw]
            rkv_ref[pl.ds(q0, gw), ls] = mr[gw:]
            vk_ref[pl.ds(q0, gw), ls] = _bd_fold(_dot_tn(v, b3_scr[pl.ds(r0, cs), gs]))
        return carry

    lax.fori_loop(0, tb // cs, chunk, 0)


def _tri_solve_kernel(a_ref, t_ref, a_scr, t_scr):
    n, hf = RWKV_CHUNK, RWKV_CHUNK // 2
    npb = a_scr.shape[2]
    ng = a_scr.shape[1] // n

    def load(t, carry):
        a_scr[t] = a_ref[pl.ds(t, npb, stride=n), :].T
        return carry

    lax.fori_loop(0, n, load, 0)
    jrow = _iota((hf, npb), 0)
    zero = jnp.zeros((hf, npb), F32)

    def coef(t, g, s):
        return a_scr[t, pl.ds(g * n + s, 1), :]

    def row_lo(t, carry):
        def inner(s, accs):
            return tuple(acc - coef(t, g, s) * t_scr[s, g * n:g * n + hf, :] for g, acc in enumerate(accs))

        accs = lax.fori_loop(0, t, inner, ((jrow == t).astype(F32),) * ng)
        for g, acc in enumerate(accs):
            t_scr[t, g * n:g * n + hf, :] = acc
            t_scr[t, g * n + hf:(g + 1) * n, :] = zero
        return carry

    lax.fori_loop(0, hf, row_lo, 0)

    def row_hi(t, carry):
        def inner_lo(s, accs):
            return tuple(acc - coef(t, g, s) * t_scr[s, g * n:g * n + hf, :] for g, acc in enumerate(accs))

        def inner_hi(s, accs):
            lo, hi = accs[:ng], accs[ng:]
            lo = tuple(acc - coef(t, g, s) * t_scr[s, g * n:g * n + hf, :] for g, acc in enumerate(lo))
            hi = tuple(acc - coef(t, g, s) * t_scr[s, g * n + hf:(g + 1) * n, :] for g, acc in enumerate(hi))
            return lo + hi

        lo = lax.fori_loop(0, hf, inner_lo, (zero,) * ng)
        accs = lax.fori_loop(hf, t, inner_hi, lo + ((jrow + hf == t).astype(F32),) * ng)
        for g in range(ng):
            t_scr[t, g * n:g * n + hf, :] = accs[g]
            t_scr[t, g * n + hf:(g + 1) * n, :] = accs[ng + g]
        return carry

    lax.fori_loop(hf, n, row_hi, 0)

    def store(t, carry):
        t_ref[pl.ds(t, npb, stride=n), :] = t_scr[t].T
        return carry

    lax.fori_loop(0, n, store, 0)


def _tri_solve(a):
    rows, width = a.shape
    n = RWKV_CHUNK
    npb = min(LANES, rows // n)
    return pl.pallas_call(
        _tri_solve_kernel,
        grid=(rows // (npb * n),),
        in_specs=[pl.BlockSpec((npb * n, width), lambda i: (i, 0))],
        out_specs=pl.BlockSpec((npb * n, width), lambda i: (i, 0)),
        out_shape=jax.ShapeDtypeStruct(a.shape, F32),
        scratch_shapes=[pltpu.VMEM((n, width, npb), F32)] * 2,
        compiler_params=_cparams(("parallel",)),
        name="tri_solve",
    )(a)


def _rwkv_c_kernel(t_ref, kd_ref, rd_ref, bg_ref, ge_ref, bonus_ref, gate_ref, rb_ref, m2_ref, rkv_ref, vk_ref,
                   lnw, lnb, y_ref, st_ref, s_scr, y_scr):
    c = pl.program_id(1)
    tb, cs, gw, hc = RWKV_BLOCK, RWKV_CHUNK, RW_GW, C_HEAD

    @pl.when(c == 0)
    def _():
        s_scr[...] = jnp.zeros_like(s_scr)

    def chunk(j, carry):
        r0 = pl.multiple_of(j * cs, cs)
        q0 = pl.multiple_of(j * gw, gw)
        for g in range(C_HEADS // RW_GH):
            gs = slice(g * gw, (g + 1) * gw)
            ls = slice(g * hc, (g + 1) * hc)
            rhs = jnp.concatenate([_to_stack(kd_ref[pl.ds(r0, cs), gs]), m2_ref[pl.ds(q0, gw), ls]], axis=1)
            wu = _dot(_bd_lanes(t_ref[pl.ds(q0, gw), ls]), rhs)
            rbwu = _dot(_bd_lanes(rb_ref[pl.ds(q0, gw), ls]), wu)
            rt = _to_stack(rd_ref[pl.ds(r0, cs), gs]) - rbwu[:, :hc]
            yc = rkv_ref[pl.ds(q0, gw), ls] - rbwu[:, hc:]
            wu_bd = jnp.concatenate([_bd_lanes(wu[:, :hc]), _bd_lanes(wu[:, hc:])], axis=1)
            nu = _dot_tn(wu_bd, _to_stack(bg_ref[pl.ds(r0, cs), gs]))
            ge_row = ge_ref[pl.ds(r0, 1), gs]
            ge_st = jnp.concatenate(
                [jnp.broadcast_to(ge_row[:, h * hc:(h + 1) * hc], (hc, hc)) for h in range(RW_GH)], axis=0)
            s = s_scr[g]
            s_bd = _bd_lanes(s)
            y_scr[pl.ds(r0, cs), gs] = _dot_nt(_to_all(rt), s_bd) + _to_all(yc)
            s_scr[g] = s * ge_st - _dot(s_bd, nu[:gw]) + (vk_ref[pl.ds(q0, gw), ls] - nu[gw:])
        return carry

    lax.fori_loop(0, tb // cs, chunk, 0)
    y_ref[...] = _rwkv_out(y_scr[...], bonus_ref[...], gate_ref[...], lnw[...], lnb[...]).astype(BF16)

    @pl.when(c == pl.num_programs(1) - 1)
    def _():
        st_ref[...] = s_scr[...]


def _rwkv_prompt(proj, rp):
    bsz, seq, _ = proj.shape
    tb, cs = RWKV_BLOCK, RWKV_CHUNK
    nblk = seq // tb

    def col(width, off):
        return pl.BlockSpec((None, tb, width), lambda b, c: (b, c, off // width))

    def pcol(width, off):
        return pl.BlockSpec((None, 8, width), lambda b, c: (b, jnp.maximum(c * (tb // 8) - 1, 0), off // width))

    ng = C_HEADS // RW_GH
    srows = tb // cs * RW_GW
    tok = pl.BlockSpec((None, tb, C_DIM), lambda b, c: (b, c, 0))
    stk = pl.BlockSpec((None, srows, ng * C_HEAD), lambda b, c: (b, c, 0))
    tok_shape = jax.ShapeDtypeStruct((bsz, seq, C_DIM), F32)
    stk_shape = jax.ShapeDtypeStruct((bsz, nblk * srows, ng * C_HEAD), F32)
    prm = [rp[n] for n in RW_NAMES]
    outs = pl.pallas_call(
        _rwkv_a_kernel,
        grid=(bsz, nblk),
        in_specs=[col(C_DIM, P_R), col(C_DIM, P_K), col(C_DIM, P_V), col(C_LR, P_LR),
                  pcol(C_DIM, P_R), pcol(C_DIM, P_K), pcol(C_DIM, P_V), pcol(C_LR, P_LR)]
                 + [_full(x.shape) for x in prm],
        out_specs=[tok] * 6 + [stk] * 5,
        out_shape=[tok_shape] * 6 + [stk_shape] * 5,
        scratch_shapes=[pltpu.VMEM((tb + 8, C_DIM), F32)] + [pltpu.VMEM((tb, C_DIM), F32)] * 4,
        compiler_params=_cparams(("parallel", "parallel")),
        name="rwkv_prep",
    )(proj, proj, proj, proj, proj, proj, proj, proj, *prm)
    kd, rd, bg, ge, bonus, gate, aab, rb, m2, rkv, vk = outs

    tmat = _tri_solve(aab.reshape(-1, ng * C_HEAD)).reshape(stk_shape.shape)

    y, st = pl.pallas_call(
        _rwkv_c_kernel,
        grid=(bsz, nblk),
        in_specs=[stk] + [tok] * 6 + [stk] * 4 + [_full((1, C_DIM))] * 2,
        out_specs=[tok, pl.BlockSpec((None, ng, RW_GW, C_HEAD), lambda b, c: (b, 0, 0, 0))],
        out_shape=[jax.ShapeDtypeStruct((bsz, seq, C_DIM), BF16),
                   jax.ShapeDtypeStruct((bsz, ng, RW_GW, C_HEAD), F32)],
        scratch_shapes=[pltpu.VMEM((ng, RW_GW, C_HEAD), F32), pltpu.VMEM((tb, C_DIM), F32)],
        compiler_params=_cparams(("parallel", "arbitrary")),
        name="rwkv_scan",
    )(tmat, kd, rd, bg, ge, bonus, gate, rb, m2, rkv, vk, rp["lnx_w"], rp["lnx_b"])
    return y, st.reshape(bsz, C_HEADS, C_HEAD, C_HEAD)


def _rwkv_step_kernel(r_ref, k_ref, v_ref, lr_ref, sh_ref, *rest):
    prm = tuple(x[...] for x in rest[:len(RW_NAMES)])
    lnw, lnb, s_ref, _, y_ref, so_ref = rest[len(RW_NAMES):]
    bb = STEP_BB
    sh = sh_ref[...]
    r, k2, v, kk, beta, ld, g, bonus = _rwkv_prep(
        r_ref[...], k_ref[...], v_ref[...], lr_ref[...],
        sh[:, :C_DIM], sh[:, C_DIM:2 * C_DIM], sh[:, 2 * C_DIM:3 * C_DIM], sh[:, 3 * C_DIM:], prm)
    w = jnp.exp(ld)
    v_t = v.T
    lane = _iota((C_DIM, LANES), 1)
    heads = [[x[:, h * C_HEAD:(h + 1) * C_HEAD] for x in (kk, w, beta, k2, r)] for h in range(C_HEADS)]

    def rows(i, b):
        return jnp.concatenate(
            [jnp.broadcast_to(heads[h][i][b:b + 1, :], (C_HEAD, C_HEAD)) for h in range(C_HEADS)], axis=0)

    y_t = jnp.zeros((C_DIM, LANES), F32)
    for b in range(bb):
        s = s_ref[b]
        sa = -jnp.sum(s * rows(0, b), axis=-1, keepdims=True)
        sn = s * rows(1, b) + sa * rows(2, b) + v_t[:, b:b + 1] * rows(3, b)
        so_ref[b] = sn
        y_t = jnp.where(lane == b, jnp.sum(sn * rows(4, b), axis=-1, keepdims=True), y_t)
    y = y_t.T[0:bb, :]
    y_ref[...] = _rwkv_out(y, bonus, g, lnw[...], lnb[...]).astype(BF16)


def _rwkv_step(proj, shift, s_all, s_new, layer, rp):
    bsz = proj.shape[0]
    bb = STEP_BB
    sflat = s_all.reshape(s_all.shape[0], bsz, C_DIM, C_HEAD)

    def col(width, off):
        return pl.BlockSpec((bb, width), lambda i: (i, off // width))

    prm = [rp[n] for n in RW_NAMES]
    y, sn = pl.pallas_call(
        _rwkv_step_kernel,
        grid=(bsz // bb,),
        in_specs=[col(C_DIM, P_R), col(C_DIM, P_K), col(C_DIM, P_V), col(C_LR, P_LR),
                  pl.BlockSpec((bb, C_COLS), lambda i: (i, 0))]
                 + [_full(x.shape) for x in prm] + [_full((1, C_DIM))] * 2
                 + [pl.BlockSpec((None, bb, C_DIM, C_HEAD), lambda i: (layer, i, 0, 0)),
                    pl.BlockSpec(memory_space=pl.ANY)],
        out_specs=[pl.BlockSpec((bb, C_DIM), lambda i: (i, 0)),
                   pl.BlockSpec((None, bb, C_DIM, C_HEAD), lambda i: (layer, i, 0, 0))],
        out_shape=[jax.ShapeDtypeStruct((bsz, C_DIM), BF16),
                   jax.ShapeDtypeStruct(sflat.shape, F32)],
        input_output_aliases={len(prm) + 8: 1},
        compiler_params=_cparams(("parallel",)),
        name="rwkv_step",
    )(proj, proj, proj, proj, shift, *prm, rp["lnx_w"], rp["lnx_b"], sflat, s_new.reshape(sflat.shape))
    return y, sn.reshape(s_all.shape)


def _layout_w_in_kernel(w_ref, o_ref):
    o = A_DIM + CONV_CH
    rows = w_ref.shape[0]
    o_ref[:, :o] = w_ref[:, :o].astype(BF16)
    o_ref[:, o:P_DT] = w_ref[:, o + A_HEADS:].astype(BF16)
    dt_tile = jnp.where(_iota((rows, LANES), 1) < A_HEADS, w_ref[:, o:o + LANES], 0.0)
    o_ref[:, P_DT:P_DT + LANES] = dt_tile.astype(BF16)
    o_ref[:, P_DT + LANES:] = jnp.zeros((rows, P_COLS - P_DT - LANES), BF16)


def _layout_w_in(w_all, layer):
    k = w_all.shape[1]
    tr = 256
    return pl.pallas_call(
        _layout_w_in_kernel,
        grid=(k // tr,),
        in_specs=[pl.BlockSpec((None, tr, IN_COLS), lambda i: (layer, i, 0))],
        out_specs=pl.BlockSpec((tr, P_COLS), lambda i: (i, 0)),
        out_shape=jax.ShapeDtypeStruct((k, P_COLS), BF16),
        compiler_params=_cparams(("parallel",)),
        name="layout_w_in",
    )(w_all)


def _xbc_cols(proj):
    return proj[..., P_X:P_X + CONV_CH]


def _moe_route(logits, n_tok):
    top_logit, top_idx = lax.top_k(logits, TOP_K)
    gate = jax.nn.softmax(top_logit, axis=-1)
    n_assign = n_tok * TOP_K
    flat_e = top_idx.reshape(-1).astype(jnp.int32)
    order = jnp.argsort(flat_e)
    onehot = jax.nn.one_hot(flat_e, N_EXPERTS, dtype=jnp.int32)
    counts = jnp.sum(onehot, axis=0)
    padded = (counts + MOE_BM - 1) // MOE_BM * MOE_BM
    pad_end = jnp.cumsum(padded)
    pad_start = pad_end - padded
    start = jnp.cumsum(counts) - counts
    rank = jnp.sum((jnp.cumsum(onehot, axis=0) - onehot) * onehot, axis=1)
    slot_of = (pad_start[flat_e] + rank).reshape(n_tok, TOP_K)
    n_blocks = -(-n_assign // MOE_BM) + N_EXPERTS
    n_slots = n_blocks * MOE_BM
    block_start = jnp.arange(n_blocks, dtype=jnp.int32) * MOE_BM
    nvalid = (pad_end[-1] // MOE_BM).astype(jnp.int32)
    bexp = jnp.minimum(jnp.searchsorted(pad_end, block_start, side="right"), N_EXPERTS - 1).astype(jnp.int32)
    slot_e = jnp.repeat(bexp, MOE_BM)
    within = jnp.arange(n_slots, dtype=jnp.int32) - pad_start[slot_e]
    src = order[jnp.clip(start[slot_e] + within, 0, n_assign - 1)] // TOP_K
    slot_tok = jnp.where(within < counts[slot_e], src, n_tok)
    left = counts[bexp] - (block_start - pad_start[bexp])
    nsub = jnp.clip((left + MOE_SUB - 1) // MOE_SUB, 0, MOE_BM // MOE_SUB).astype(jnp.int32)
    last = bexp[jnp.maximum(nvalid - 1, 0)]
    bexp = jnp.where(jnp.arange(n_blocks) < nvalid, bexp, last)
    first = jnp.concatenate([jnp.ones((1,), jnp.int32), (bexp[1:] != bexp[:-1]).astype(jnp.int32)])
    return gate, slot_tok, slot_of, bexp, first, nsub


def _moe(xp, xs, h, logits, wg, wu, wd, g, final):
    tp = xp.shape[0]
    n_tok = h.shape[0]
    gate, slot_tok, slot_of, bexp, first, nsub = _moe_route(logits[:, :N_EXPERTS], n_tok)
    xb = _moe_gather(jnp.where(slot_tok == n_tok, 0, slot_tok), h)
    yb = _moe_experts(xb, bexp, first, nsub, wg, wu, wd)
    return (_moe_combine(xp, yb, slot_of[:tp], gate[:tp], g, final),
            _moe_combine(xs, yb, slot_of[tp:], gate[tp:], g, final))


def kernel(x_prompt, x_sample, state_ssm, state_conv, state_hgrn, state_rwkv, state_shift, norm1_g, w_in, conv_w, conv_b, dt_bias, a_log, d_skip, ssm_norm_g, lb_logits, hgrn_norm_g, shift_mu, w0, w2, a0, a2, g2, k_k, k_a, r_k, lnx_w, lnx_b, w_out, norm2_g, ffn_w_gate, ffn_w_up, ffn_w_down, router_w, exp_w_gate, exp_w_up, exp_w_down, final_norm_g):
    depth = w_in.shape[0]
    bp, seq, _ = x_prompt.shape
    bs = x_sample.shape[0]
    tp = bp * seq
    lb_soft = jax.nn.softmax(lb_logits.astype(F32), axis=0)
    lower_bounds = jnp.clip(jnp.cumsum(lb_soft, axis=0) - lb_soft[0:1], 0.0, 1.0)

    xp = x_prompt.reshape(tp, D_MODEL)
    xs = x_sample.reshape(bs, D_MODEL)
    outs = {k: [] for k in ("p_ssm", "p_conv", "p_hgrn", "p_rwkv", "p_shift", "s_conv", "s_shift")}
    ssm_s, hgrn_s, rwkv_s = (jnp.zeros(s.shape, F32) for s in (state_ssm, state_hgrn, state_rwkv))
    normed = False
    for layer in range(depth):
        p = dict(conv_w=conv_w[layer], conv_b=conv_b[layer], dt_bias=dt_bias[layer], a_log=a_log[layer],
                 d_skip=d_skip[layer], ssm_norm_g=ssm_norm_g[layer], shift_mu=shift_mu[layer],
                 w0=w0[layer], w2=w2[layer], a0=a0[layer], a2=a2[layer], g2=g2[layer], k_k=k_k[layer],
                 k_a=k_a[layer], r_k=r_k[layer].reshape(-1), lnx_w=lnx_w[layer], lnx_b=lnx_b[layer])
        sp = _ssd_params(p)
        rp = _rwkv_params(p)
        lb = lower_bounds[layer]
        lbp = jnp.stack([1.0 - lb, jnp.log(lb), jnp.log1p(-lb)])
        hng = hgrn_norm_g[layer].reshape(1, B_HEAD)
        w_in_l = _layout_w_in(w_in, layer)
        w_out_l = w_out[layer].astype(BF16)
        j = layer // 2
        routed = layer % 2 == 1
        if routed:
            rw_f = jnp.pad(router_w[j], ((0, 0), (0, LANES - N_EXPERTS)))
            rw_hi = rw_f.astype(BF16)
            rw = jnp.concatenate([rw_hi, (rw_f - rw_hi.astype(F32)).astype(BF16)], axis=1)
            h_all = jnp.zeros((tp + bs, D_MODEL), F32)
            l_all = jnp.zeros((tp + bs, LANES), F32)

        proj = _norm_matmul(xp, norm1_g[layer], w_in_l, 256).reshape(bp, seq, P_COLS)
        ya, ssm_p = _ssd_prompt(proj, sp)
        yb, hgrn_p = _hgrn_prompt(proj, lbp, hng)
        yc, rwkv_p = _rwkv_prompt(proj, rp)
        xp, hp, l_all = _out_proj(ya.reshape(tp, A_DIM), yb.reshape(tp, B_DIM), yc.reshape(tp, C_DIM),
                                  w_out_l, xp, norm2_g[layer], 256, (rw, h_all, l_all, 0) if routed else None)
        outs["p_ssm"].append(ssm_p)
        outs["p_conv"].append(_xbc_cols(proj[:, seq - (CONV_W - 1):, :]))
        outs["p_hgrn"].append(hgrn_p)
        outs["p_rwkv"].append(rwkv_p)
        outs["p_shift"].append(proj[:, seq - 1, P_R:P_R + C_COLS])

        projs = _norm_matmul(xs, norm1_g[layer], w_in_l, 128)
        ya, ssm_s = _ssd_step(projs, state_conv[layer], state_ssm, ssm_s, layer, p, sp)
        yb, hgrn_s = _hgrn_step(projs, state_hgrn, hgrn_s, layer, lbp, hng)
        yc, rwkv_s = _rwkv_step(projs, state_shift[layer], state_rwkv, rwkv_s, layer, rp)
        xs, hs, l_all = _out_proj(ya, yb, yc, w_out_l, xs, norm2_g[layer], 128,
                                  (rw, hp, l_all, tp) if routed else None)
        outs["s_conv"].append(jnp.concatenate([state_conv[layer][:, 1:, :], _xbc_cols(projs)[:, None, :]], axis=1))
        outs["s_shift"].append(projs[:, P_R:P_R + C_COLS])

        if not routed:
            xp = _ffn(xp, hp, ffn_w_gate[j], ffn_w_up[j], ffn_w_down[j])
            xs = _ffn(xs, hs, ffn_w_gate[j], ffn_w_up[j], ffn_w_down[j])
        else:
            normed = layer == depth - 1
            xp, xs = _moe(xp, xs, hs, l_all, exp_w_gate[j], exp_w_up[j], exp_w_down[j], final_norm_g, normed)

    if not normed:
        xp, xs = _final_norm(xp, final_norm_g, 512), _final_norm(xs, final_norm_g, 128)
    y_prompt = xp.reshape(bp, seq, D_MODEL)
    y_sample = xs.reshape(bs, 1, D_MODEL)
    st = {k: jnp.stack(v) for k, v in outs.items()}
    return (y_prompt, y_sample, st["p_ssm"], st["p_conv"], st["p_hgrn"], st["p_rwkv"], st["p_shift"],
            ssm_s, st["s_conv"], hgrn_s, rwkv_s, st["s_shift"])
```

```python
import functools

import jax
import jax.numpy as jnp
from jax import lax
from jax.experimental import pallas as pl
from jax.experimental.pallas import tpu as pltpu

F32 = jnp.float32
BF16 = jnp.bfloat16
HI = lax.Precision.HIGHEST

D_MODEL = 2048
A_DIM, A_HEAD, A_HEADS, A_GROUPS, A_STATE = 1024, 64, 16, 2, 128
A_GN = A_GROUPS * A_STATE
CONV_W = 4
CONV_CH = A_DIM + 2 * A_GN
B_DIM, B_HEAD, B_HEADS = 512, 128, 4
C_DIM, C_HEAD, C_HEADS = 512, 64, 8
C_LR_W, C_LR_A, C_LR_G = 64, 64, 128
C_LR = C_LR_W + C_LR_A + C_LR_G
C_COLS = 3 * C_DIM + C_LR
IN_COLS = A_DIM + CONV_CH + A_HEADS + 4 * B_DIM + C_COLS
D_FF, N_EXPERTS, TOP_K, E_FF = 5504, 8, 2, 7168
NORM_EPS, GN_EPS, L2_EPS = 1e-6, 64e-5, 1e-12

P_Z, P_X, P_B, P_C = 0, 1024, 2048, 2304
P_Q, P_F, P_I, P_G = 2560, 3072, 3584, 4096
P_R, P_K, P_V, P_LR, P_DT, P_COLS = 4608, 5120, 5632, 6144, 6400, 6656
LANES = 128

SSD_CHUNK = 256
HGRN_BLOCK, HGRN_CHUNK = 256, 16
RWKV_BLOCK, RWKV_CHUNK = 256, 64
STEP_BB = 8

FFN_TF, FFN_TN, FFN_TM_UP, FFN_TM_DOWN = 512, 1024, 1024, 256
MOE_BM, MOE_SUB, MOE_TF, MOE_TN, MOE_KP = 512, 256, 1024, 1024, 2

NT_DIMS = (((1,), (1,)), ((), ()))
TN_DIMS = (((0,), (0,)), ((), ()))


def _cparams(sem, vmem_mb=48):
    return pltpu.CompilerParams(dimension_semantics=sem, vmem_limit_bytes=vmem_mb * 1024 * 1024)


def _dot(a, b):
    return jnp.dot(a.astype(BF16), b.astype(BF16), preferred_element_type=F32)


def _dot_nt(a, b):
    return lax.dot_general(a.astype(BF16), b.astype(BF16), NT_DIMS, preferred_element_type=F32)


def _dot_tn(a, b):
    return lax.dot_general(a.astype(BF16), b.astype(BF16), TN_DIMS, preferred_element_type=F32)


def _dot_hi(a, b):
    return jnp.dot(a, b, precision=HI, preferred_element_type=F32)


def _dot_nt_hi(a, b):
    return lax.dot_general(a, b, NT_DIMS, precision=HI, preferred_element_type=F32)


def _silu(x):
    return x * jax.nn.sigmoid(x)


def _softplus(x):
    return jnp.maximum(x, 0.0) + jnp.log1p(jnp.exp(-jnp.abs(x)))


def _iota(shape, dim):
    return lax.broadcasted_iota(jnp.int32, shape, dim)


def _rms(x, g):
    return x * lax.rsqrt(jnp.mean(x * x, axis=-1, keepdims=True) + NORM_EPS) * g


def _full(shape):
    nd = len(shape)
    return pl.BlockSpec(shape, lambda *_: (0,) * nd)


def _norm_matmul_kernel(x_ref, g_ref, w_ref, o_ref):
    h = _rms(x_ref[...], g_ref[...]).astype(BF16)
    o_ref[...] = jnp.dot(h, w_ref[...], preferred_element_type=F32)


def _norm_matmul(x, g, w, tm):
    m, k = x.shape
    n = w.shape[1]
    return pl.pallas_call(
        _norm_matmul_kernel,
        grid=(m // tm,),
        in_specs=[pl.BlockSpec((tm, k), lambda i: (i, 0)),
                  pl.BlockSpec((1, k), lambda i: (0, 0)),
                  pl.BlockSpec((k, n), lambda i: (0, 0), pipeline_mode=pl.Buffered(1))],
        out_specs=pl.BlockSpec((tm, n), lambda i: (i, 0)),
        out_shape=jax.ShapeDtypeStruct((m, n), F32),
        compiler_params=_cparams(("parallel",), 56),
        name="norm_matmul",
    )(x, g.reshape(1, k), w)


def _out_proj_kernel(routed, ya_ref, yb_ref, yc_ref, w_ref, r_ref, g_ref, *rest):
    x = (r_ref[...]
         + jnp.dot(ya_ref[...], w_ref[0:A_DIM, :], preferred_element_type=F32)
         + jnp.dot(yb_ref[...], w_ref[A_DIM:A_DIM + B_DIM, :], preferred_element_type=F32)
         + jnp.dot(yc_ref[...], w_ref[A_DIM + B_DIM:, :], preferred_element_type=F32))
    h = _rms(x, g_ref[...])
    h_hi = h.astype(BF16)
    if routed:
        rw_ref, _, _, x_ref, h_ref, l_ref = rest
        h_lo = (h - h_hi.astype(F32)).astype(BF16)
        rw = rw_ref[...]
        l2 = jnp.dot(h_hi, rw, preferred_element_type=F32)
        l_ref[...] = (l2[:, :LANES] + l2[:, LANES:]
                      + jnp.dot(h_lo, rw[:, :LANES], preferred_element_type=F32))
    else:
        x_ref, h_ref = rest
    x_ref[...] = x
    h_ref[...] = h_hi


def _out_proj(ya, yb, yc, w, res, g, tm, routed=None):
    m = ya.shape[0]
    n = w.shape[1]
    row = lambda width: pl.BlockSpec((tm, width), lambda i: (i, 0))
    in_specs = [row(A_DIM), row(B_DIM), row(C_DIM), _full(w.shape), row(n), _full((1, n))]
    args = [ya, yb, yc, w, res, g.reshape(1, n)]
    aliases = {}
    if routed is None:
        out_specs = [row(n), row(n)]
        out_shape = [jax.ShapeDtypeStruct((m, n), F32), jax.ShapeDtypeStruct((m, n), BF16)]
    else:
        rw, h_buf, l_buf, row0 = routed
        blk0 = row0 // tm
        off = lambda width: pl.BlockSpec((tm, width), lambda i: (i + blk0, 0))
        in_specs += [_full(rw.shape), pl.BlockSpec(memory_space=pl.ANY), pl.BlockSpec(memory_space=pl.ANY)]
        args += [rw, h_buf, l_buf]
        out_specs = [row(n), off(n), off(LANES)]
        out_shape = [jax.ShapeDtypeStruct((m, n), F32), jax.ShapeDtypeStruct(h_buf.shape, BF16),
                     jax.ShapeDtypeStruct(l_buf.shape, F32)]
        aliases = {len(args) - 2: 1, len(args) - 1: 2}
    outs = pl.pallas_call(
        functools.partial(_out_proj_kernel, routed is not None),
        grid=(m // tm,),
        in_specs=in_specs, out_specs=out_specs, out_shape=out_shape,
        input_output_aliases=aliases,
        compiler_params=_cparams(("parallel",)),
        name="out_proj",
    )(*args)
    return outs if routed is not None else (*outs, None)


def _ffn_up_kernel(h_ref, wg_ref, wu_ref, o_ref, wg_s, wu_s):
    @pl.when(pl.program_id(1) == 0)
    def _():
        wg_s[...] = wg_ref[...].astype(BF16)
        wu_s[...] = wu_ref[...].astype(BF16)

    h = h_ref[...]
    a = jnp.dot(h, wg_s[...], preferred_element_type=F32)
    u = jnp.dot(h, wu_s[...], preferred_element_type=F32)
    o_ref[...] = (_silu(a) * u).astype(BF16)


def _ffn_down_kernel(h_ref, wd_ref, x_ref, o_ref, wd_s):
    @pl.when(pl.program_id(1) == 0)
    def _():
        wd_s[...] = wd_ref[...].astype(BF16)

    o_ref[...] = x_ref[...] + jnp.dot(h_ref[...], wd_s[...], preferred_element_type=F32)


def _ffn(x, h, wg, wu, wd):
    m, k = h.shape
    ff = wg.shape[1]
    tf, tn = FFN_TF, FFN_TN
    tu = min(m, FFN_TM_UP)
    tm = min(m, FFN_TM_DOWN)
    act = pl.pallas_call(
        _ffn_up_kernel,
        grid=(pl.cdiv(ff, tf), m // tu),
        in_specs=[pl.BlockSpec((tu, k), lambda f, i: (i, 0)),
                  pl.BlockSpec((k, tf), lambda f, i: (0, f)),
                  pl.BlockSpec((k, tf), lambda f, i: (0, f))],
        out_specs=pl.BlockSpec((tu, tf), lambda f, i: (i, f)),
        out_shape=jax.ShapeDtypeStruct((m, ff), BF16),
        scratch_shapes=[pltpu.VMEM((k, tf), BF16)] * 2,
        compiler_params=_cparams(("arbitrary", "arbitrary")),
        name="ffn_up",
    )(h, wg, wu)
    return pl.pallas_call(
        _ffn_down_kernel,
        grid=(k // tn, m // tm),
        in_specs=[pl.BlockSpec((tm, ff), lambda n, i: (i, 0)),
                  pl.BlockSpec((ff, tn), lambda n, i: (0, n), pipeline_mode=pl.Buffered(1)),
                  pl.BlockSpec((tm, tn), lambda n, i: (i, n))],
        out_specs=pl.BlockSpec((tm, tn), lambda n, i: (i, n)),
        out_shape=jax.ShapeDtypeStruct((m, k), F32),
        scratch_shapes=[pltpu.VMEM((ff, tn), BF16)],
        compiler_params=_cparams(("arbitrary", "arbitrary"), 56),
        name="ffn_down",
    )(act, wd, x)


def _moe_rows(nsub, o_ref, compute):
    bm = o_ref.shape[0]
    for ns in range(bm // MOE_SUB + 1):
        @pl.when(nsub == ns)
        def _(rows=ns * MOE_SUB):
            if rows:
                o_ref[0:rows, :] = compute(rows).astype(o_ref.dtype)
            if rows < bm:
                o_ref[rows:, :] = jnp.zeros((bm - rows, o_ref.shape[1]), o_ref.dtype)


def _weight_stream(sched, w_hbms, wbuf, sem, dsts):
    bexp_ref, first_ref, run_ref, nxt_ref, lastrun_ref, nruns_ref = sched
    col, b = pl.program_id(0), pl.program_id(1)
    ncol = pl.num_programs(0)
    width = wbuf.shape[-1]

    def copies(e, c, slot):
        cols = pl.ds(pl.multiple_of(c * width, width), width)
        return [pltpu.make_async_copy(w.at[e, :, cols], wbuf.at[slot, i], sem.at[slot])
                for i, w in enumerate(w_hbms)]

    @pl.when(first_ref[b] == 1)
    def _():
        slot = lax.rem(col * nruns_ref[0] + run_ref[b], 2)
        last = lastrun_ref[b]

        @pl.when((col == 0) & (b == 0))
        def _():
            for cp in copies(bexp_ref[0], 0, 0):
                cp.start()

        @pl.when((last == 0) | (col < ncol - 1))
        def _():
            for cp in copies(nxt_ref[b], col + last, 1 - slot):
                cp.start(priority=1)

        for cp in copies(bexp_ref[b], col, slot):
            cp.wait()
        for i, dst in enumerate(dsts):
            dst[...] = wbuf[slot, i].astype(BF16)


def _moe_up_kernel(*refs):
    sched, (nsub_ref, _, x_ref, wg_hbm, wu_hbm, o_ref, wbuf, sem, wg_s, wu_s) = refs[:6], refs[6:]
    _weight_stream(sched, (wg_hbm, wu_hbm), wbuf, sem, (wg_s, wu_s))

    def compute(rows):
        x = x_ref[0:rows, :]
        a = jnp.dot(x, wg_s[...], preferred_element_type=F32)
        u = jnp.dot(x, wu_s[...], preferred_element_type=F32)
        return _silu(a) * u

    _moe_rows(nsub_ref[pl.program_id(1)], o_ref, compute)


def _moe_down_kernel(*refs):
    sched, (nsub_ref, _, h_ref, wd_hbm, o_ref, wbuf, sem, wd_s) = refs[:6], refs[6:]
    bexp_ref, first_ref, _, nxt_ref, lastrun_ref, _ = sched
    col, b, kk = pl.program_id(0), pl.program_id(1), pl.program_id(2)
    ncol = pl.num_programs(0)
    parts, kp, width = wbuf.shape

    def copies(e, c):
        cols = pl.ds(pl.multiple_of(c * width, width), width)
        return [pltpu.make_async_copy(wd_hbm.at[e, pl.ds(i * kp, kp), cols], wbuf.at[i], sem.at[i])
                for i in range(parts)]

    @pl.when((first_ref[b] == 1) & (kk == 0))
    def _():
        last = lastrun_ref[b]

        @pl.when((col == 0) & (b == 0))
        def _():
            for cp in copies(bexp_ref[0], 0):
                cp.start()

        for cp in copies(bexp_ref[b], col):
            cp.wait()
        rc = 512

        def cast(r, carry):
            rows = pl.ds(pl.multiple_of(r * rc, rc), rc)
            for i in range(parts):
                wd_s[i, rows, :] = wbuf[i, rows, :].astype(BF16)
            return carry

        lax.fori_loop(0, kp // rc, cast, 0)

        @pl.when((last == 0) | (col < ncol - 1))
        def _():
            for cp in copies(nxt_ref[b], col + last):
                cp.start(priority=1)

    valid = nsub_ref[b] > 0
    for i in range(parts):
        @pl.when(valid & (kk == i))
        def _(i=i):
            acc = jnp.dot(h_ref[...], wd_s[i], preferred_element_type=F32)
            if i == 0:
                o_ref[...] = acc
            else:
                o_ref[...] += acc

    @pl.when(jnp.logical_not(valid) & (kk == 0))
    def _():
        o_ref[...] = jnp.zeros_like(o_ref)


def _run_schedule(bexp, first):
    nb = bexp.shape[0]
    ids = jnp.arange(nb, dtype=jnp.int32)
    run = jnp.cumsum(first).astype(jnp.int32) - 1
    nruns = run[-1] + 1
    starts = jnp.where(first == 1, ids, nb)
    nxt_start = jnp.concatenate([lax.cummin(starts[::-1])[::-1][1:], jnp.full((1,), nb, jnp.int32)])
    nxt = jnp.where(nxt_start < nb, bexp[jnp.minimum(nxt_start, nb - 1)], bexp[0]).astype(jnp.int32)
    lastrun = (run == nruns - 1).astype(jnp.int32)
    return bexp, first, run, nxt, lastrun, nruns.reshape(1)


def _moe_experts(xb, bexp, first, nsub, wg, wu, wd):
    n_slots, k = xb.shape
    ff = wg.shape[2]
    nb = n_slots // MOE_BM
    tf, tn = MOE_TF, MOE_TN
    hbm = pl.BlockSpec(memory_space=pl.ANY)

    def src_blocks(ns):
        return lax.cummax(jnp.where(ns > 0, jnp.arange(ns.shape[0], dtype=jnp.int32), 0))

    act = pl.pallas_call(
        _moe_up_kernel,
        grid_spec=pltpu.PrefetchScalarGridSpec(
            num_scalar_prefetch=8,
            grid=(ff // tf, nb),
            in_specs=[pl.BlockSpec((MOE_BM, k), lambda f, b, *s: (s[7][b], 0)), hbm, hbm],
            out_specs=pl.BlockSpec((MOE_BM, tf), lambda f, b, *s: (b, f)),
            scratch_shapes=[pltpu.VMEM((2, 2, k, tf), F32), pltpu.SemaphoreType.DMA((2,)),
                            pltpu.VMEM((k, tf), BF16), pltpu.VMEM((k, tf), BF16)],
        ),
        out_shape=jax.ShapeDtypeStruct((n_slots, ff), BF16),
        compiler_params=_cparams(("arbitrary", "arbitrary"), 56),
        name="moe_up",
    )(*_run_schedule(bexp, first), nsub, src_blocks(nsub), xb, wg, wu)
    per = MOE_BM // MOE_SUB
    bexp_d = jnp.repeat(bexp, per)
    first_d = jnp.concatenate([jnp.ones((1,), jnp.int32), (bexp_d[1:] != bexp_d[:-1]).astype(jnp.int32)])
    nsub_d = jnp.clip(jnp.repeat(nsub, per) - jnp.tile(jnp.arange(per, dtype=jnp.int32), nb), 0, 1)
    return pl.pallas_call(
        _moe_down_kernel,
        grid_spec=pltpu.PrefetchScalarGridSpec(
            num_scalar_prefetch=8,
            grid=(k // tn, nb * per, MOE_KP),
            in_specs=[pl.BlockSpec((MOE_SUB, ff // MOE_KP), lambda n, b, kk, *s: (s[7][b], kk)), hbm],
            out_specs=pl.BlockSpec((MOE_SUB, tn), lambda n, b, kk, *s: (b, n)),
            scratch_shapes=[pltpu.VMEM((MOE_KP, ff // MOE_KP, tn), F32), pltpu.SemaphoreType.DMA((MOE_KP,)),
                            pltpu.VMEM((MOE_KP, ff // MOE_KP, tn), BF16)],
        ),
        out_shape=jax.ShapeDtypeStruct((n_slots, k), F32),
        compiler_params=_cparams(("arbitrary", "arbitrary", "arbitrary"), 56),
        name="moe_down",
    )(*_run_schedule(bexp_d, first_d), nsub_d, src_blocks(nsub_d), act, wd)


def _final_norm_kernel(x_ref, g_ref, o_ref):
    o_ref[...] = _rms(x_ref[...], g_ref[...])


def _final_norm(x, g, tm):
    m, k = x.shape
    return pl.pallas_call(
        _final_norm_kernel,
        grid=(m // tm,),
        in_specs=[pl.BlockSpec((tm, k), lambda i: (i, 0)),
                  pl.BlockSpec((1, k), lambda i: (0, 0))],
        out_specs=pl.BlockSpec((tm, k), lambda i: (i, 0)),
        out_shape=jax.ShapeDtypeStruct((m, k), F32),
        compiler_params=_cparams(("parallel",)),
        name="final_norm",
    )(x, g.reshape(1, k))


def _row_copy(src_hbm, row, dst, r, sem):
    return pltpu.make_async_copy(src_hbm.at[pl.ds(row, 1), :], dst.at[pl.ds(r, 1), :], sem)


def _moe_combine_kernel(final, s0_ref, s1_ref, x_ref, gate_ref, g_ref, yb_hbm, o_ref, buf, sem):
    rows = x_ref.shape[0]
    base = pl.program_id(0) * rows

    def issue(r, carry):
        _row_copy(yb_hbm, s0_ref[base + r], buf.at[0], r, sem).start(priority=0)
        _row_copy(yb_hbm, s1_ref[base + r], buf.at[1], r, sem).start(priority=1)
        return carry

    lax.fori_loop(0, rows, issue, 0)
    pltpu.make_async_copy(buf, buf, sem).wait()
    gate = gate_ref[...]
    y = x_ref[...] + gate[:, 0:1] * buf[0] + gate[:, 1:2] * buf[1]
    o_ref[...] = _rms(y, g_ref[...]) if final else y


def _moe_combine(x, yb, slot_of, gate, g, final):
    m, k = x.shape
    rows = min(m, 512)
    gate_p = jnp.pad(gate, ((0, 0), (0, LANES - TOP_K)))
    return pl.pallas_call(
        functools.partial(_moe_combine_kernel, final),
        grid_spec=pltpu.PrefetchScalarGridSpec(
            num_scalar_prefetch=2,
            grid=(m // rows,),
            in_specs=[pl.BlockSpec((rows, k), lambda i, s0, s1: (i, 0)),
                      pl.BlockSpec((rows, LANES), lambda i, s0, s1: (i, 0)),
                      pl.BlockSpec((1, k), lambda i, s0, s1: (0, 0)),
                      pl.BlockSpec(memory_space=pl.ANY)],
            out_specs=pl.BlockSpec((rows, k), lambda i, s0, s1: (i, 0)),
            scratch_shapes=[pltpu.VMEM((TOP_K, rows, k), F32), pltpu.SemaphoreType.DMA],
        ),
        out_shape=jax.ShapeDtypeStruct((m, k), F32),
        compiler_params=_cparams(("arbitrary",)),
        name="moe_combine",
    )(slot_of[:, 0], slot_of[:, 1], x, gate_p, g.reshape(1, k), yb)


def _ssd_gate_norm(y, z, ng):
    y = y * _silu(z)
    gw = A_DIM // A_GROUPS
    outs = []
    for g in range(A_GROUPS):
        yg = y[:, g * gw:(g + 1) * gw]
        outs.append(yg * lax.rsqrt(jnp.mean(yg * yg, axis=-1, keepdims=True) + NORM_EPS))
    return jnp.concatenate(outs, axis=-1) * ng


def _ssd_prompt_kernel(z_ref, x_ref, b_ref, c_ref, dt_ref, cwx, cwb, cwc, cbx, cbb, cbc,
                       dtb, aneg, dsk, ng, y_ref, st_ref, h_scr, xpx, xpb, xpc, yacc):
    c = pl.program_id(1)
    ch = SSD_CHUNK

    @pl.when(c == 0)
    def _():
        h_scr[...] = jnp.zeros_like(h_scr)
        xpx[0:8, :] = jnp.zeros((8, A_DIM), F32)
        xpb[0:8, :] = jnp.zeros((8, A_STATE * A_GROUPS), F32)
        xpc[0:8, :] = jnp.zeros((8, A_STATE * A_GROUPS), F32)

    xpx[8:8 + ch, :] = x_ref[...]
    xpb[8:8 + ch, :] = b_ref[...]
    xpc[8:8 + ch, :] = c_ref[...]

    def conv(xp, cw, cb):
        acc = cb[...] + xp[pl.ds(8 - (CONV_W - 1), ch), :] * cw[0:1, :]
        for tap in range(1, CONV_W):
            acc = acc + xp[pl.ds(8 - (CONV_W - 1) + tap, ch), :] * cw[tap:tap + 1, :]
        return _silu(acc)

    ux = conv(xpx, cwx, cbx)
    ub = conv(xpb, cwb, cbb)
    uc = conv(xpc, cwc, cbc)
    xpx[0:8, :] = xpx[ch:ch + 8, :]
    xpb[0:8, :] = xpb[ch:ch + 8, :]
    xpc[0:8, :] = xpc[ch:ch + 8, :]

    dt = _softplus(dt_ref[...] + dtb[...])
    la = dt * aneg[...]
    tri = (_iota((ch, ch), 1) <= _iota((ch, ch), 0)).astype(F32)
    cum = _dot_hi(tri, la)
    eye = (_iota((A_HEADS, LANES), 0) == _iota((A_HEADS, LANES), 1)).astype(F32)
    cum_r = _dot_nt_hi(eye, cum)
    dt_r = _dot_nt_hi(eye, dt)
    cl = cum[ch - 1:ch, :]
    tail = jnp.exp(cl - cum) * dt
    ecum = jnp.exp(cum)
    ecl = jnp.exp(cl)
    causal = _iota((ch, ch), 1) <= _iota((ch, ch), 0)
    rep = A_HEADS // A_GROUPS

    for g in range(A_GROUPS):
        bg = ub[:, g * A_STATE:(g + 1) * A_STATE].astype(BF16)
        cg = uc[:, g * A_STATE:(g + 1) * A_STATE].astype(BF16)
        cb_g = _dot_nt(cg, bg)
        for h in range(g * rep, (g + 1) * rep):
            hs = slice(h * A_HEAD, (h + 1) * A_HEAD)
            seg = cum[:, h:h + 1] - cum_r[h:h + 1, :]
            w = jnp.exp(jnp.where(causal, seg, -jnp.inf)) * cb_g * dt_r[h:h + 1, :]
            xh = ux[:, hs]
            hst = h_scr[h]
            y_h = _dot(w, xh) + _dot_nt(cg, hst) * ecum[:, h:h + 1]
            yacc[:, hs] = y_h
            h_scr[h] = hst * ecl[:, h:h + 1] + _dot_tn(xh * tail[:, h:h + 1], bg)

    y = yacc[...] + dsk[...] * ux
    y_ref[...] = _ssd_gate_norm(y, z_ref[...], ng[...]).astype(BF16)

    @pl.when(c == pl.num_programs(1) - 1)
    def _():
        st_ref[...] = h_scr[...]


def _ssd_params(p):
    cw, cb = p["conv_w"], p["conv_b"].reshape(1, CONV_CH)
    pad = LANES - A_HEADS
    return dict(
        cwx=cw[:, :A_DIM], cwb=cw[:, A_DIM:A_DIM + A_GN], cwc=cw[:, A_DIM + A_GN:],
        cbx=cb[:, :A_DIM], cbb=cb[:, A_DIM:A_DIM + A_GN], cbc=cb[:, A_DIM + A_GN:],
        dtb=jnp.pad(p["dt_bias"], (0, pad)).reshape(1, LANES),
        aneg=jnp.pad(-jnp.exp(p["a_log"]), (0, pad)).reshape(1, LANES),
        dsk=jnp.repeat(p["d_skip"], A_HEAD).reshape(1, A_DIM),
        ng=p["ssm_norm_g"].reshape(1, A_DIM),
    )


def _ssd_prompt(proj, sp):
    bsz, seq, _ = proj.shape
    ch = SSD_CHUNK
    gn = A_GN

    def col(width, off):
        return pl.BlockSpec((None, ch, width), lambda b, c: (b, c, off // width))

    names = ("cwx", "cwb", "cwc", "cbx", "cbb", "cbc", "dtb", "aneg", "dsk", "ng")
    y, st = pl.pallas_call(
        _ssd_prompt_kernel,
        grid=(bsz, seq // ch),
        in_specs=[col(A_DIM, P_Z), col(A_DIM, P_X), col(gn, P_B), col(gn, P_C), col(LANES, P_DT)]
                 + [_full(sp[n].shape) for n in names],
        out_specs=[pl.BlockSpec((None, ch, A_DIM), lambda b, c: (b, c, 0)),
                   pl.BlockSpec((None, A_HEADS, A_HEAD, A_STATE), lambda b, c: (b, 0, 0, 0))],
        out_shape=[jax.ShapeDtypeStruct((bsz, seq, A_DIM), BF16),
                   jax.ShapeDtypeStruct((bsz, A_HEADS, A_HEAD, A_STATE), F32)],
        scratch_shapes=[pltpu.VMEM((A_HEADS, A_HEAD, A_STATE), F32),
                        pltpu.VMEM((ch + 8, A_DIM), F32),
                        pltpu.VMEM((ch + 8, gn), F32),
                        pltpu.VMEM((ch + 8, gn), F32),
                        pltpu.VMEM((ch, A_DIM), F32)],
        compiler_params=_cparams(("parallel", "arbitrary")),
        name="ssd_prompt",
    )(proj, proj, proj, proj, proj, *[sp[n] for n in names])
    return y, st


def _ssd_step_kernel(z_ref, x_ref, b_ref, c_ref, dt_ref, cs_ref, cw, cb, dtb, anegx, expand, dsk, ng,
                     h_ref, _, y_ref, ho_ref, y_scr):
    bb = STEP_BB
    xbc = (x_ref[...], b_ref[...], c_ref[...])
    offs = (0, A_DIM, A_DIM + A_GN, CONV_CH)
    u = []
    for i in range(3):
        sl = slice(offs[i], offs[i + 1])
        acc = cb[:, sl] + xbc[i] * cw[CONV_W - 1:CONV_W, sl]
        for tap in range(CONV_W - 1):
            acc = acc + cs_ref[tap][:, sl] * cw[tap:tap + 1, sl]
        u.append(_silu(acc))
    ux, ub, uc = u
    dt = _softplus(dt_ref[...] + dtb[...])
    dtx = _dot_hi(dt, expand[...])
    dec_t = jnp.exp(dtx * anegx[...]).T
    xdt_t = (ux * dtx).T
    gw = A_DIM // A_GROUPS
    for b in range(bb):
        for g in range(A_GROUPS):
            rs = slice(g * gw, (g + 1) * gw)
            ns = slice(g * A_STATE, (g + 1) * A_STATE)
            hg = h_ref[b, rs, :] * dec_t[rs, b:b + 1] + xdt_t[rs, b:b + 1] * ub[b:b + 1, ns]
            ho_ref[b, rs, :] = hg
            y_scr[b:b + 1, rs] = _dot_nt(uc[:, ns], hg)[b:b + 1, :]
    y = y_scr[...] + dsk[...] * ux
    y_ref[...] = _ssd_gate_norm(y, z_ref[...], ng[...]).astype(BF16)


def _ssd_step(proj, conv_state, h_all, h_new, layer, p, sp):
    bsz = proj.shape[0]
    bb = STEP_BB
    cs = jnp.swapaxes(conv_state, 0, 1)
    hflat = h_all.reshape(h_all.shape[0], bsz, A_DIM, A_STATE)
    expand = (jnp.arange(LANES)[:, None] == (jnp.arange(A_DIM) // A_HEAD)[None, :]).astype(F32)
    anegx = jnp.repeat(-jnp.exp(p["a_log"]), A_HEAD).reshape(1, A_DIM)
    cw, cb = p["conv_w"], p["conv_b"].reshape(1, CONV_CH)

    def col(width, off):
        return pl.BlockSpec((bb, width), lambda i: (i, off // width))

    y, hn = pl.pallas_call(
        _ssd_step_kernel,
        grid=(bsz // bb,),
        in_specs=[col(A_DIM, P_Z), col(A_DIM, P_X), col(A_GN, P_B), col(A_GN, P_C), col(LANES, P_DT),
                  pl.BlockSpec((CONV_W - 1, bb, CONV_CH), lambda i: (0, i, 0)),
                  _full(cw.shape), _full(cb.shape), _full(sp["dtb"].shape), _full(anegx.shape),
                  _full(expand.shape), _full(sp["dsk"].shape), _full(sp["ng"].shape),
                  pl.BlockSpec((None, bb, A_DIM, A_STATE), lambda i: (layer, i, 0, 0)),
                  pl.BlockSpec(memory_space=pl.ANY)],
        out_specs=[pl.BlockSpec((bb, A_DIM), lambda i: (i, 0)),
                   pl.BlockSpec((None, bb, A_DIM, A_STATE), lambda i: (layer, i, 0, 0))],
        out_shape=[jax.ShapeDtypeStruct((bsz, A_DIM), BF16),
                   jax.ShapeDtypeStruct(hflat.shape, F32)],
        input_output_aliases={14: 1},
        scratch_shapes=[pltpu.VMEM((bb, A_DIM), F32)],
        compiler_params=_cparams(("parallel",)),
        name="ssd_step",
    )(proj, proj, proj, proj, proj, cs, cw, cb, sp["dtb"], anegx, expand, sp["dsk"], sp["ng"], hflat,
      h_new.reshape(hflat.shape))
    return y, hn.reshape(h_all.shape)


def _hgrn_gates(q, f, lbp):
    qf = _silu(q)
    kf = lbp[0:1, :] * jax.nn.sigmoid(-f)
    log_sig = jnp.minimum(f, 0.0) - jnp.log1p(jnp.exp(-jnp.abs(f)))
    a = lbp[1:2, :]
    b = lbp[2:3, :] + log_sig
    logf = jnp.maximum(a, b) + jnp.log1p(jnp.exp(-jnp.abs(a - b)))
    return qf, kf, logf


def _hgrn_out(o, g, ng):
    outs = []
    for h in range(B_HEADS):
        oh = o[:, h * B_HEAD:(h + 1) * B_HEAD]
        outs.append(oh * lax.rsqrt(jnp.mean(oh * oh, axis=-1, keepdims=True) + NORM_EPS) * ng)
    return jnp.concatenate(outs, axis=-1) * _silu(g)


def _hgrn_prompt_kernel(q_ref, f_ref, i_ref, g_ref, lbp, ng, o_ref, st_ref,
                        s_scr, cum_scr, q_scr, k_scr, o_scr):
    c = pl.program_id(1)
    tb, cs = HGRN_BLOCK, HGRN_CHUNK

    @pl.when(c == 0)
    def _():
        s_scr[...] = jnp.zeros_like(s_scr)

    qf, kf, logf = _hgrn_gates(q_ref[...], f_ref[...], lbp[...])
    r_i, c_i = _iota((tb, tb), 0), _iota((tb, tb), 1)
    bd = ((r_i // cs == c_i // cs) & (c_i <= r_i)).astype(F32)
    cum_scr[...] = _dot_hi(bd, logf)
    q_scr[...] = qf
    k_scr[...] = kf
    rows = _iota((cs, B_HEAD), 0)

    def chunk(cc, carry):
        r0 = pl.multiple_of(cc * cs, cs)
        for h in range(B_HEADS):
            hs = slice(h * B_HEAD, (h + 1) * B_HEAD)
            cu = cum_scr[pl.ds(r0, cs), hs]
            q = q_scr[pl.ds(r0, cs), hs]
            k = k_scr[pl.ds(r0, cs), hs]
            v = i_ref[pl.ds(r0, cs), hs]
            st = s_scr[h]
            o = _dot_nt(q * jnp.exp(cu), st)
            for s in range(cs):
                d = jnp.exp(jnp.where(rows >= s, cu - cu[s:s + 1, :], -jnp.inf))
                att = jnp.sum(q * k[s:s + 1, :] * d, axis=-1, keepdims=True)
                o = o + att * v[s:s + 1, :]
            o_scr[pl.ds(r0, cs), hs] = o
            cl = cu[cs - 1:cs, :]
            s_scr[h] = st * jnp.exp(cl) + _dot_tn(v, k * jnp.exp(cl - cu))
        return carry

    lax.fori_loop(0, tb // cs, chunk, 0)
    o_ref[...] = _hgrn_out(o_scr[...], g_ref[...], ng[...]).astype(BF16)

    @pl.when(c == pl.num_programs(1) - 1)
    def _():
        for h in range(B_HEADS):
            st_ref[h] = s_scr[h].T


def _hgrn_prompt(proj, lbp, ng):
    bsz, seq, _ = proj.shape
    tb = HGRN_BLOCK

    def col(off):
        return pl.BlockSpec((None, tb, B_DIM), lambda b, c: (b, c, off // B_DIM))

    o, st = pl.pallas_call(
        _hgrn_prompt_kernel,
        grid=(bsz, seq // tb),
        in_specs=[col(P_Q), col(P_F), col(P_I), col(P_G), _full(lbp.shape), _full(ng.shape)],
        out_specs=[pl.BlockSpec((None, tb, B_DIM), lambda b, c: (b, c, 0)),
                   pl.BlockSpec((None, B_HEADS, B_HEAD, B_HEAD), lambda b, c: (b, 0, 0, 0))],
        out_shape=[jax.ShapeDtypeStruct((bsz, seq, B_DIM), BF16),
                   jax.ShapeDtypeStruct((bsz, B_HEADS, B_HEAD, B_HEAD), F32)],
        scratch_shapes=[pltpu.VMEM((B_HEADS, B_HEAD, B_HEAD), F32)]
                       + [pltpu.VMEM((tb, B_DIM), F32)] * 4,
        compiler_params=_cparams(("parallel", "arbitrary")),
        name="hgrn_prompt",
    )(proj, proj, proj, proj, lbp, ng)
    return o, st


def _hgrn_step_kernel(q_ref, f_ref, i_ref, g_ref, lbp, ng, s_ref, _, o_ref, so_ref, o_scr):
    bb = STEP_BB
    qf, kf, logf = _hgrn_gates(q_ref[...], f_ref[...], lbp[...])
    vf = i_ref[...]
    dec_t = jnp.exp(logf).T
    k_t = kf.T
    q_t = qf.T
    for b in range(bb):
        for h in range(B_HEADS):
            hs = slice(h * B_HEAD, (h + 1) * B_HEAD)
            sh = s_ref[b, hs, :] * dec_t[hs, b:b + 1] + k_t[hs, b:b + 1] * vf[b:b + 1, hs]
            so_ref[b, hs, :] = sh
            o_scr[b:b + 1, hs] = jnp.sum(q_t[hs, b:b + 1] * sh, axis=0, keepdims=True)
    o_ref[...] = _hgrn_out(o_scr[...], g_ref[...], ng[...]).astype(BF16)


def _hgrn_step(proj, s_all, s_new, layer, lbp, ng):
    bsz = proj.shape[0]
    bb = STEP_BB
    sflat = s_all.reshape(s_all.shape[0], bsz, B_DIM, B_HEAD)

    def col(off):
        return pl.BlockSpec((bb, B_DIM), lambda i: (i, off // B_DIM))

    o, sn = pl.pallas_call(
        _hgrn_step_kernel,
        grid=(bsz // bb,),
        in_specs=[col(P_Q), col(P_F), col(P_I), col(P_G), _full(lbp.shape), _full(ng.shape),
                  pl.BlockSpec((None, bb, B_DIM, B_HEAD), lambda i: (layer, i, 0, 0)),
                  pl.BlockSpec(memory_space=pl.ANY)],
        out_specs=[pl.BlockSpec((bb, B_DIM), lambda i: (i, 0)),
                   pl.BlockSpec((None, bb, B_DIM, B_HEAD), lambda i: (layer, i, 0, 0))],
        out_shape=[jax.ShapeDtypeStruct((bsz, B_DIM), BF16),
                   jax.ShapeDtypeStruct(sflat.shape, F32)],
        input_output_aliases={7: 1},
        scratch_shapes=[pltpu.VMEM((bb, B_DIM), F32)],
        compiler_params=_cparams(("parallel",)),
        name="hgrn_step",
    )(proj, proj, proj, proj, lbp, ng, sflat, s_new.reshape(sflat.shape))
    return o, sn.reshape(s_all.shape)


RW_NAMES = ("mu_r", "mu_k", "mu_v", "mu_lr", "w0", "w2", "a0", "a2", "g2", "k_k", "k_a", "r_k")


def _rwkv_params(p):
    mu = p["shift_mu"].reshape(1, C_COLS)
    row = lambda a: a.reshape(1, C_DIM)
    return dict(
        mu_r=mu[:, :C_DIM], mu_k=mu[:, C_DIM:2 * C_DIM], mu_v=mu[:, 2 * C_DIM:3 * C_DIM],
        mu_lr=mu[:, 3 * C_DIM:],
        w0=row(p["w0"]), w2=p["w2"].astype(BF16), a0=row(p["a0"]), a2=p["a2"].astype(BF16),
        g2=p["g2"].astype(BF16), k_k=row(p["k_k"]), k_a=row(p["k_a"]), r_k=row(p["r_k"]),
        lnx_w=row(p["lnx_w"]), lnx_b=row(p["lnx_b"]),
    )


def _head_sum(x):
    return [jnp.sum(x[:, h * C_HEAD:(h + 1) * C_HEAD], axis=-1, keepdims=True) for h in range(C_HEADS)]


def _head_bcast(cols, fn=lambda c: c):
    rows = cols[0].shape[0]
    return jnp.concatenate([jnp.broadcast_to(fn(c), (rows, C_HEAD)) for c in cols], axis=-1)


def _rwkv_prep(r, k, v, lr, pr, pk, pv, plr, prm):
    mu_r, mu_k, mu_v, mu_lr, w0, w2, a0, a2, g2, k_k, k_a, r_k = prm
    r = r + (pr - r) * mu_r
    k = k + (pk - k) * mu_k
    v = v + (pv - v) * mu_v
    lr = lr + (plr - lr) * mu_lr
    wl, al, gl = lr[:, :C_LR_W], lr[:, C_LR_W:C_LR_W + C_LR_A], lr[:, C_LR_W + C_LR_A:]
    wraw = -_softplus(-(w0 + _dot(jnp.tanh(wl), w2))) - 0.5
    ld = -jnp.exp(wraw)
    a = jax.nn.sigmoid(a0 + _dot(al, a2))
    g = _dot(jax.nn.sigmoid(gl), g2)
    kk = k * k_k
    kk = kk * _head_bcast(_head_sum(kk * kk), lambda n2: 1.0 / jnp.maximum(jnp.sqrt(n2), L2_EPS))
    k2 = k * (1.0 + (a - 1.0) * k_a)
    bonus = _head_bcast(_head_sum(r * k2 * r_k)) * v
    return r, k2, v, kk, kk * a, ld, g, bonus


def _rwkv_out(y, bonus, g, lnx_w, lnx_b):
    n = float(C_HEAD)
    mean = _head_bcast(_head_sum(y), lambda s: s / n)
    yc = y - mean
    rstd = _head_bcast(_head_sum(yc * yc), lambda s: lax.rsqrt(s / n + GN_EPS))
    return (yc * rstd * lnx_w + lnx_b + bonus) * g


RW_GH = 4
RW_GW = RW_GH * C_HEAD


def _bd_mask():
    return _iota((RW_GW, RW_GW), 0) // C_HEAD == _iota((RW_GW, RW_GW), 1) // C_HEAD


def _bd_rows(a):
    return jnp.where(_bd_mask(), jnp.concatenate([a] * RW_GH, axis=0), 0.0)


def _bd_lanes(s):
    return jnp.where(_bd_mask(), jnp.concatenate([s] * RW_GH, axis=1), 0.0)


def _bd_fold(m):
    m = jnp.where(_bd_mask(), m, 0.0)
    out = m[:, :C_HEAD]
    for h in range(1, RW_GH):
        out = out + m[:, h * C_HEAD:(h + 1) * C_HEAD]
    return out


def _to_stack(a):
    return jnp.concatenate([a[:, h * C_HEAD:(h + 1) * C_HEAD] for h in range(RW_GH)], axis=0)


def _to_all(s):
    return jnp.concatenate([s[h * C_HEAD:(h + 1) * C_HEAD, :] for h in range(RW_GH)], axis=1)


def _rwkv_a_kernel(r_ref, k_ref, v_ref, lr_ref, rp_ref, kp_ref, vp_ref, lrp_ref, *rest):
    prm = tuple(x[...] for x in rest[:len(RW_NAMES)])
    (kd_ref, rd_ref, bg_ref, ge_ref, bonus_ref, gate_ref, aab_ref, rb_ref, m2_ref, rkv_ref, vk_ref,
     xs_scr, b1_scr, b2_scr, b3_scr, b4_scr) = rest[len(RW_NAMES):]
    c = pl.program_id(1)
    tb, cs = RWKV_BLOCK, RWKV_CHUNK
    first = (c == 0)

    def prev(cur_ref, prev_ref, width):
        xs_scr[8:8 + tb, 0:width] = cur_ref[...]
        xs_scr[7:8, 0:width] = jnp.where(first, 0.0, prev_ref[7:8, :])
        return xs_scr[pl.ds(7, tb), 0:width]

    pr = prev(r_ref, rp_ref, C_DIM)
    pk = prev(k_ref, kp_ref, C_DIM)
    pv = prev(v_ref, vp_ref, C_DIM)
    plr = prev(lr_ref, lrp_ref, C_LR)
    r, k2, v, kk, beta, ld, g, bonus = _rwkv_prep(
        r_ref[...], k_ref[...], v_ref[...], lr_ref[...], pr, pk, pv, plr, prm)
    bonus_ref[...] = bonus
    gate_ref[...] = g

    r_i, c_i = _iota((tb, tb), 0), _iota((tb, tb), 1)
    same = (r_i // cs == c_i // cs)
    lcum = _dot_hi((same & (c_i <= r_i)).astype(F32), ld)
    lend = _dot_hi(same.astype(F32), ld)
    einv = jnp.exp(-lcum)
    eend = jnp.exp(lend - lcum)
    ge_ref[...] = jnp.exp(lend)
    kd_ref[...] = kk * jnp.exp(lcum - ld)
    rd_ref[...] = r * jnp.exp(lcum)
    bg_ref[...] = beta * eend
    b1_scr[...] = beta * einv
    b2_scr[...] = k2 * einv
    b3_scr[...] = k2 * eend
    b4_scr[...] = v
    gw = RW_GW
    tpos, spos = _iota((gw, cs), 0) % cs, _iota((gw, cs), 1)
    strict, incl = tpos > spos, tpos >= spos
    hc = C_HEAD

    def chunk(j, carry):
        r0 = pl.multiple_of(j * cs, cs)
        q0 = pl.multiple_of(j * gw, gw)
        for g in range(C_HEADS // RW_GH):
            gs = slice(g * gw, (g + 1) * gw)
            ls = slice(g * hc, (g + 1) * hc)
            kd = kd_ref[pl.ds(r0, cs), gs]
            rd = rd_ref[pl.ds(r0, cs), gs]
            v = b4_scr[pl.ds(r0, cs), gs]
            lhs = jnp.concatenate([_bd_rows(kd), _bd_rows(rd)], axis=0)
            rhs = jnp.concatenate([b1_scr[pl.ds(r0, cs), gs], b2_scr[pl.ds(r0, cs), gs]], axis=0)
            gm = _dot_nt(lhs, rhs)
            aab_ref[pl.ds(q0, gw), ls] = jnp.where(strict, gm[:gw, :hc], 0.0)
            aak = jnp.where(strict, gm[:gw, hc:], 0.0)
            rb_ref[pl.ds(q0, gw), ls] = jnp.where(incl, gm[gw:, :hc], 0.0)
            rk = jnp.where(incl, gm[gw:, hc:], 0.0)
            mr = _dot(jnp.concatenate([_bd_lanes(aak), _bd_lanes(rk)], axis=0), _to_stack(v))
            m2_ref[pl.ds(q0, gw), ls] = mr[:gw]
            rkv_ref[pl.ds(q0, gw), ls] = mr[gw:]
            vk_ref[pl.ds(q0, gw), ls] = _bd_fold(_dot_tn(v, b3_scr[pl.ds(r0, cs), gs]))
        return carry

    lax.fori_loop(0, tb // cs, chunk, 0)


def _tri_solve_kernel(a_ref, t_ref, a_scr, t_scr):
    n, hf = RWKV_CHUNK, RWKV_CHUNK // 2
    npb = a_scr.shape[2]
    ng = a_scr.shape[1] // n

    def load(t, carry):
        a_scr[t] = a_ref[pl.ds(t, npb, stride=n), :].T
        return carry

    lax.fori_loop(0, n, load, 0)
    jrow = _iota((hf, npb), 0)
    zero = jnp.zeros((hf, npb), F32)

    def coef(t, g, s):
        return a_scr[t, pl.ds(g * n + s, 1), :]

    def row_lo(t, carry):
        def inner(s, accs):
            return tuple(acc - coef(t, g, s) * t_scr[s, g * n:g * n + hf, :] for g, acc in enumerate(accs))

        accs = lax.fori_loop(0, t, inner, ((jrow == t).astype(F32),) * ng)
        for g, acc in enumerate(accs):
            t_scr[t, g * n:g * n + hf, :] = acc
            t_scr[t, g * n + hf:(g + 1) * n, :] = zero
        return carry

    lax.fori_loop(0, hf, row_lo, 0)

    def row_hi(t, carry):
        def inner_lo(s, accs):
            return tuple(acc - coef(t, g, s) * t_scr[s, g * n:g * n + hf, :] for g, acc in enumerate(accs))

        def inner_hi(s, accs):
            lo, hi = accs[:ng], accs[ng:]
            lo = tuple(acc - coef(t, g, s) * t_scr[s, g * n:g * n + hf, :] for g, acc in enumerate(lo))
            hi = tuple(acc - coef(t, g, s) * t_scr[s, g * n + hf:(g + 1) * n, :] for g, acc in enumerate(hi))
            return lo + hi

        lo = lax.fori_loop(0, hf, inner_lo, (zero,) * ng)
        accs = lax.fori_loop(hf, t, inner_hi, lo + ((jrow + hf == t).astype(F32),) * ng)
        for g in range(ng):
            t_scr[t, g * n:g * n + hf, :] = accs[g]
            t_scr[t, g * n + hf:(g + 1) * n, :] = accs[ng + g]
        return carry

    lax.fori_loop(hf, n, row_hi, 0)

    def store(t, carry):
        t_ref[pl.ds(t, npb, stride=n), :] = t_scr[t].T
        return carry

    lax.fori_loop(0, n, store, 0)


def _tri_solve(a):
    rows, width = a.shape
    n = RWKV_CHUNK
    npb = min(LANES, rows // n)
    return pl.pallas_call(
        _tri_solve_kernel,
        grid=(rows // (npb * n),),
        in_specs=[pl.BlockSpec((npb * n, width), lambda i: (i, 0))],
        out_specs=pl.BlockSpec((npb * n, width), lambda i: (i, 0)),
        out_shape=jax.ShapeDtypeStruct(a.shape, F32),
        scratch_shapes=[pltpu.VMEM((n, width, npb), F32)] * 2,
        compiler_params=_cparams(("parallel",)),
        name="tri_solve",
    )(a)


def _rwkv_c_kernel(t_ref, kd_ref, rd_ref, bg_ref, ge_ref, bonus_ref, gate_ref, rb_ref, m2_ref, rkv_ref, vk_ref,
                   lnw, lnb, y_ref, st_ref, s_scr, y_scr):
    c = pl.program_id(1)
    tb, cs, gw, hc = RWKV_BLOCK, RWKV_CHUNK, RW_GW, C_HEAD

    @pl.when(c == 0)
    def _():
        s_scr[...] = jnp.zeros_like(s_scr)

    def chunk(j, carry):
        r0 = pl.multiple_of(j * cs, cs)
        q0 = pl.multiple_of(j * gw, gw)
        for g in range(C_HEADS // RW_GH):
            gs = slice(g * gw, (g + 1) * gw)
            ls = slice(g * hc, (g + 1) * hc)
            rhs = jnp.concatenate([_to_stack(kd_ref[pl.ds(r0, cs), gs]), m2_ref[pl.ds(q0, gw), ls]], axis=1)
            wu = _dot(_bd_lanes(t_ref[pl.ds(q0, gw), ls]), rhs)
            rbwu = _dot(_bd_lanes(rb_ref[pl.ds(q0, gw), ls]), wu)
            rt = _to_stack(rd_ref[pl.ds(r0, cs), gs]) - rbwu[:, :hc]
            yc = rkv_ref[pl.ds(q0, gw), ls] - rbwu[:, hc:]
            wu_bd = jnp.concatenate([_bd_lanes(wu[:, :hc]), _bd_lanes(wu[:, hc:])], axis=1)
            nu = _dot_tn(wu_bd, _to_stack(bg_ref[pl.ds(r0, cs), gs]))
            ge_row = ge_ref[pl.ds(r0, 1), gs]
            ge_st = jnp.concatenate(
                [jnp.broadcast_to(ge_row[:, h * hc:(h + 1) * hc], (hc, hc)) for h in range(RW_GH)], axis=0)
            s = s_scr[g]
            s_bd = _bd_lanes(s)
            y_scr[pl.ds(r0, cs), gs] = _dot_nt(_to_all(rt), s_bd) + _to_all(yc)
            s_scr[g] = s * ge_st - _dot(s_bd, nu[:gw]) + (vk_ref[pl.ds(q0, gw), ls] - nu[gw:])
        return carry

    lax.fori_loop(0, tb // cs, chunk, 0)
    y_ref[...] = _rwkv_out(y_scr[...], bonus_ref[...], gate_ref[...], lnw[...], lnb[...]).astype(BF16)

    @pl.when(c == pl.num_programs(1) - 1)
    def _():
        st_ref[...] = s_scr[...]


def _rwkv_prompt(proj, rp):
    bsz, seq, _ = proj.shape
    tb, cs = RWKV_BLOCK, RWKV_CHUNK
    nblk = seq // tb

    def col(width, off):
        return pl.BlockSpec((None, tb, width), lambda b, c: (b, c, off // width))

    def pcol(width, off):
        return pl.BlockSpec((None, 8, width), lambda b, c: (b, jnp.maximum(c * (tb // 8) - 1, 0), off // width))

    ng = C_HEADS // RW_GH
    srows = tb // cs * RW_GW
    tok = pl.BlockSpec((None, tb, C_DIM), lambda b, c: (b, c, 0))
    stk = pl.BlockSpec((None, srows, ng * C_HEAD), lambda b, c: (b, c, 0))
    tok_shape = jax.ShapeDtypeStruct((bsz, seq, C_DIM), F32)
    stk_shape = jax.ShapeDtypeStruct((bsz, nblk * srows, ng * C_HEAD), F32)
    prm = [rp[n] for n in RW_NAMES]
    outs = pl.pallas_call(
        _rwkv_a_kernel,
        grid=(bsz, nblk),
        in_specs=[col(C_DIM, P_R), col(C_DIM, P_K), col(C_DIM, P_V), col(C_LR, P_LR),
                  pcol(C_DIM, P_R), pcol(C_DIM, P_K), pcol(C_DIM, P_V), pcol(C_LR, P_LR)]
                 + [_full(x.shape) for x in prm],
        out_specs=[tok] * 6 + [stk] * 5,
        out_shape=[tok_shape] * 6 + [stk_shape] * 5,
        scratch_shapes=[pltpu.VMEM((tb + 8, C_DIM), F32)] + [pltpu.VMEM((tb, C_DIM), F32)] * 4,
        compiler_params=_cparams(("parallel", "parallel")),
        name="rwkv_prep",
    )(proj, proj, proj, proj, proj, proj, proj, proj, *prm)
    kd, rd, bg, ge, bonus, gate, aab, rb, m2, rkv, vk = outs

    tmat = _tri_solve(aab.reshape(-1, ng * C_HEAD)).reshape(stk_shape.shape)

    y, st = pl.pallas_call(
        _rwkv_c_kernel,
        grid=(bsz, nblk),
        in_specs=[stk] + [tok] * 6 + [stk] * 4 + [_full((1, C_DIM))] * 2,
        out_specs=[tok, pl.BlockSpec((None, ng, RW_GW, C_HEAD), lambda b, c: (b, 0, 0, 0))],
        out_shape=[jax.ShapeDtypeStruct((bsz, seq, C_DIM), BF16),
                   jax.ShapeDtypeStruct((bsz, ng, RW_GW, C_HEAD), F32)],
        scratch_shapes=[pltpu.VMEM((ng, RW_GW, C_HEAD), F32), pltpu.VMEM((tb, C_DIM), F32)],
        compiler_params=_cparams(("parallel", "arbitrary")),
        name="rwkv_scan",
    )(tmat, kd, rd, bg, ge, bonus, gate, rb, m2, rkv, vk, rp["lnx_w"], rp["lnx_b"])
    return y, st.reshape(bsz, C_HEADS, C_HEAD, C_HEAD)


def _rwkv_step_kernel(r_ref, k_ref, v_ref, lr_ref, sh_ref, *rest):
    prm = tuple(x[...] for x in rest[:len(RW_NAMES)])
    lnw, lnb, s_ref, _, y_ref, so_ref = rest[len(RW_NAMES):]
    bb = STEP_BB
    sh = sh_ref[...]
    r, k2, v, kk, beta, ld, g, bonus = _rwkv_prep(
        r_ref[...], k_ref[...], v_ref[...], lr_ref[...],
        sh[:, :C_DIM], sh[:, C_DIM:2 * C_DIM], sh[:, 2 * C_DIM:3 * C_DIM], sh[:, 3 * C_DIM:], prm)
    w = jnp.exp(ld)
    v_t = v.T
    lane = _iota((C_DIM, LANES), 1)
    heads = [[x[:, h * C_HEAD:(h + 1) * C_HEAD] for x in (kk, w, beta, k2, r)] for h in range(C_HEADS)]

    def rows(i, b):
        return jnp.concatenate(
            [jnp.broadcast_to(heads[h][i][b:b + 1, :], (C_HEAD, C_HEAD)) for h in range(C_HEADS)], axis=0)

    y_t = jnp.zeros((C_DIM, LANES), F32)
    for b in range(bb):
        s = s_ref[b]
        sa = -jnp.sum(s * rows(0, b), axis=-1, keepdims=True)
        sn = s * rows(1, b) + sa * rows(2, b) + v_t[:, b:b + 1] * rows(3, b)
        so_ref[b] = sn
        y_t = jnp.where(lane == b, jnp.sum(sn * rows(4, b), axis=-1, keepdims=True), y_t)
    y = y_t.T[0:bb, :]
    y_ref[...] = _rwkv_out(y, bonus, g, lnw[...], lnb[...]).astype(BF16)


def _rwkv_step(proj, shift, s_all, s_new, layer, rp):
    bsz = proj.shape[0]
    bb = STEP_BB
    sflat = s_all.reshape(s_all.shape[0], bsz, C_DIM, C_HEAD)

    def col(width, off):
        return pl.BlockSpec((bb, width), lambda i: (i, off // width))

    prm = [rp[n] for n in RW_NAMES]
    y, sn = pl.pallas_call(
        _rwkv_step_kernel,
        grid=(bsz // bb,),
        in_specs=[col(C_DIM, P_R), col(C_DIM, P_K), col(C_DIM, P_V), col(C_LR, P_LR),
                  pl.BlockSpec((bb, C_COLS), lambda i: (i, 0))]
                 + [_full(x.shape) for x in prm] + [_full((1, C_DIM))] * 2
                 + [pl.BlockSpec((None, bb, C_DIM, C_HEAD), lambda i: (layer, i, 0, 0)),
                    pl.BlockSpec(memory_space=pl.ANY)],
        out_specs=[pl.BlockSpec((bb, C_DIM), lambda i: (i, 0)),
                   pl.BlockSpec((None, bb, C_DIM, C_HEAD), lambda i: (layer, i, 0, 0))],
        out_shape=[jax.ShapeDtypeStruct((bsz, C_DIM), BF16),
                   jax.ShapeDtypeStruct(sflat.shape, F32)],
        input_output_aliases={len(prm) + 8: 1},
        compiler_params=_cparams(("parallel",)),
        name="rwkv_step",
    )(proj, proj, proj, proj, shift, *prm, rp["lnx_w"], rp["lnx_b"], sflat, s_new.reshape(sflat.shape))
    return y, sn.reshape(s_all.shape)


def _layout_w_in_kernel(w_ref, o_ref):
    o = A_DIM + CONV_CH
    rows = w_ref.shape[0]
    o_ref[:, :o] = w_ref[:, :o].astype(BF16)
    o_ref[:, o:P_DT] = w_ref[:, o + A_HEADS:].astype(BF16)
    dt_tile = jnp.where(_iota((rows, LANES), 1) < A_HEADS, w_ref[:, o:o + LANES], 0.0)
    o_ref[:, P_DT:P_DT + LANES] = dt_tile.astype(BF16)
    o_ref[:, P_DT + LANES:] = jnp.zeros((rows, P_COLS - P_DT - LANES), BF16)


def _layout_w_in(w_all, layer):
    k = w_all.shape[1]
    tr = 256
    return pl.pallas_call(
        _layout_w_in_kernel,
        grid=(k // tr,),
        in_specs=[pl.BlockSpec((None, tr, IN_COLS), lambda i: (layer, i, 0))],
        out_specs=pl.BlockSpec((tr, P_COLS), lambda i: (i, 0)),
        out_shape=jax.ShapeDtypeStruct((k, P_COLS), BF16),
        compiler_params=_cparams(("parallel",)),
        name="layout_w_in",
    )(w_all)


def _xbc_cols(proj):
    return proj[..., P_X:P_X + CONV_CH]


def _moe_route(logits, n_tok):
    top_logit, top_idx = lax.top_k(logits, TOP_K)
    gate = jax.nn.softmax(top_logit, axis=-1)
    n_assign = n_tok * TOP_K
    flat_e = top_idx.reshape(-1).astype(jnp.int32)
    order = jnp.argsort(flat_e)
    onehot = jax.nn.one_hot(flat_e, N_EXPERTS, dtype=jnp.int32)
    counts = jnp.sum(onehot, axis=0)
    padded = (counts + MOE_BM - 1) // MOE_BM * MOE_BM
    pad_end = jnp.cumsum(padded)
    pad_start = pad_end - padded
    start = jnp.cumsum(counts) - counts
    rank = jnp.sum((jnp.cumsum(onehot, axis=0) - onehot) * onehot, axis=1)
    slot_of = (pad_start[flat_e] + rank).reshape(n_tok, TOP_K)
    n_blocks = -(-n_assign // MOE_BM) + N_EXPERTS
    n_slots = n_blocks * MOE_BM
    block_start = jnp.arange(n_blocks, dtype=jnp.int32) * MOE_BM
    nvalid = (pad_end[-1] // MOE_BM).astype(jnp.int32)
    bexp = jnp.minimum(jnp.searchsorted(pad_end, block_start, side="right"), N_EXPERTS - 1).astype(jnp.int32)
    slot_e = jnp.repeat(bexp, MOE_BM)
    within = jnp.arange(n_slots, dtype=jnp.int32) - pad_start[slot_e]
    src = order[jnp.clip(start[slot_e] + within, 0, n_assign - 1)] // TOP_K
    slot_tok = jnp.where(within < counts[slot_e], src, n_tok)
    left = counts[bexp] - (block_start - pad_start[bexp])
    nsub = jnp.clip((left + MOE_SUB - 1) // MOE_SUB, 0, MOE_BM // MOE_SUB).astype(jnp.int32)
    last = bexp[jnp.maximum(nvalid - 1, 0)]
    bexp = jnp.where(jnp.arange(n_blocks) < nvalid, bexp, last)
    first = jnp.concatenate([jnp.ones((1,), jnp.int32), (bexp[1:] != bexp[:-1]).astype(jnp.int32)])
    return gate, slot_tok, slot_of, bexp, first, nsub


def _moe(xp, xs, h, logits, wg, wu, wd, g, final):
    tp = xp.shape[0]
    n_tok = h.shape[0]
    gate, slot_tok, slot_of, bexp, first, nsub = _moe_route(logits[:, :N_EXPERTS], n_tok)
    xb = h[jnp.where(slot_tok == n_tok, 0, slot_tok)]
    yb = _moe_experts(xb, bexp, first, nsub, wg, wu, wd)
    return (_moe_combine(xp, yb, slot_of[:tp], gate[:tp], g, final),
            _moe_combine(xs, yb, slot_of[tp:], gate[tp:], g, final))


def kernel(x_prompt, x_sample, state_ssm, state_conv, state_hgrn, state_rwkv, state_shift, norm1_g, w_in, conv_w, conv_b, dt_bias, a_log, d_skip, ssm_norm_g, lb_logits, hgrn_norm_g, shift_mu, w0, w2, a0, a2, g2, k_k, k_a, r_k, lnx_w, lnx_b, w_out, norm2_g, ffn_w_gate, ffn_w_up, ffn_w_down, router_w, exp_w_gate, exp_w_up, exp_w_down, final_norm_g):
    depth = w_in.shape[0]
    bp, seq, _ = x_prompt.shape
    bs = x_sample.shape[0]
    tp = bp * seq
    lb_soft = jax.nn.softmax(lb_logits.astype(F32), axis=0)
    lower_bounds = jnp.clip(jnp.cumsum(lb_soft, axis=0) - lb_soft[0:1], 0.0, 1.0)

    xp = x_prompt.reshape(tp, D_MODEL)
    xs = x_sample.reshape(bs, D_MODEL)
    outs = {k: [] for k in ("p_ssm", "p_conv", "p_hgrn", "p_rwkv", "p_shift", "s_conv", "s_shift")}
    ssm_s, hgrn_s, rwkv_s = (jnp.zeros(s.shape, F32) for s in (state_ssm, state_hgrn, state_rwkv))
    normed = False
    for layer in range(depth):
        p = dict(conv_w=conv_w[layer], conv_b=conv_b[layer], dt_bias=dt_bias[layer], a_log=a_log[layer],
                 d_skip=d_skip[layer], ssm_norm_g=ssm_norm_g[layer], shift_mu=shift_mu[layer],
                 w0=w0[layer], w2=w2[layer], a0=a0[layer], a2=a2[layer], g2=g2[layer], k_k=k_k[layer],
                 k_a=k_a[layer], r_k=r_k[layer].reshape(-1), lnx_w=lnx_w[layer], lnx_b=lnx_b[layer])
        sp = _ssd_params(p)
        rp = _rwkv_params(p)
        lb = lower_bounds[layer]
        lbp = jnp.stack([1.0 - lb, jnp.log(lb), jnp.log1p(-lb)])
        hng = hgrn_norm_g[layer].reshape(1, B_HEAD)
        w_in_l = _layout_w_in(w_in, layer)
        w_out_l = w_out[layer].astype(BF16)
        j = layer // 2
        routed = layer % 2 == 1
        if routed:
            rw_f = jnp.pad(router_w[j], ((0, 0), (0, LANES - N_EXPERTS)))
            rw_hi = rw_f.astype(BF16)
            rw = jnp.concatenate([rw_hi, (rw_f - rw_hi.astype(F32)).astype(BF16)], axis=1)
            h_all = jnp.zeros((tp + bs, D_MODEL), BF16)
            l_all = jnp.zeros((tp + bs, LANES), F32)

        proj = _norm_matmul(xp, norm1_g[layer], w_in_l, 256).reshape(bp, seq, P_COLS)
        ya, ssm_p = _ssd_prompt(proj, sp)
        yb, hgrn_p = _hgrn_prompt(proj, lbp, hng)
        yc, rwkv_p = _rwkv_prompt(proj, rp)
        xp, hp, l_all = _out_proj(ya.reshape(tp, A_DIM), yb.reshape(tp, B_DIM), yc.reshape(tp, C_DIM),
                                  w_out_l, xp, norm2_g[layer], 512, (rw, h_all, l_all, 0) if routed else None)
        outs["p_ssm"].append(ssm_p)
        outs["p_conv"].append(_xbc_cols(proj[:, seq - (CONV_W - 1):, :]))
        outs["p_hgrn"].append(hgrn_p)
        outs["p_rwkv"].append(rwkv_p)
        outs["p_shift"].append(proj[:, seq - 1, P_R:P_R + C_COLS])

        projs = _norm_matmul(xs, norm1_g[layer], w_in_l, 128)
        ya, ssm_s = _ssd_step(projs, state_conv[layer], state_ssm, ssm_s, layer, p, sp)
        yb, hgrn_s = _hgrn_step(projs, state_hgrn, hgrn_s, layer, lbp, hng)
        yc, rwkv_s = _rwkv_step(projs, state_shift[layer], state_rwkv, rwkv_s, layer, rp)
        xs, hs, l_all = _out_proj(ya, yb, yc, w_out_l, xs, norm2_g[layer], 128,
                                  (rw, hp, l_all, tp) if routed else None)
        outs["s_conv"].append(jnp.concatenate([state_conv[layer][:, 1:, :], _xbc_cols(projs)[:, None, :]], axis=1))
        outs["s_shift"].append(projs[:, P_R:P_R + C_COLS])

        if not routed:
            xp = _ffn(xp, hp, ffn_w_gate[j], ffn_w_up[j], ffn_w_down[j])
            xs = _ffn(xs, hs, ffn_w_gate[j], ffn_w_up[j], ffn_w_down[j])
        else:
            normed = layer == depth - 1
            xp, xs = _moe(xp, xs, hs, l_all, exp_w_gate[j], exp_w_up[j], exp_w_down[j], final_norm_g, normed)

    if not normed:
        xp, xs = _final_norm(xp, final_norm_g, 512), _final_norm(xs, final_norm_g, 128)
    y_prompt = xp.reshape(bp, seq, D_MODEL)
    y_sample = xs.reshape(bs, 1, D_MODEL)
    st = {k: jnp.stack(v) for k, v in outs.items()}
    return (y_prompt, y_sample, st["p_ssm"], st["p_conv"], st["p_hgrn"], st["p_rwkv"], st["p_shift"],
            ssm_s, st["s_conv"], hgrn_s, rwkv_s, st["s_shift"])
```

```python
import functools

import jax
import jax.numpy as jnp
from jax import lax
from jax.experimental import pallas as pl
from jax.experimental.pallas import tpu as pltpu

F32 = jnp.float32
BF16 = jnp.bfloat16
HI = lax.Precision.HIGHEST

D_MODEL = 2048
A_DIM, A_HEAD, A_HEADS, A_GROUPS, A_STATE = 1024, 64, 16, 2, 128
A_GN = A_GROUPS * A_STATE
CONV_W = 4
CONV_CH = A_DIM + 2 * A_GN
B_DIM, B_HEAD, B_HEADS = 512, 128, 4
C_DIM, C_HEAD, C_HEADS = 512, 64, 8
C_LR_W, C_LR_A, C_LR_G = 64, 64, 128
C_LR = C_LR_W + C_LR_A + C_LR_G
C_COLS = 3 * C_DIM + C_LR
IN_COLS = A_DIM + CONV_CH + A_HEADS + 4 * B_DIM + C_COLS
D_FF, N_EXPERTS, TOP_K, E_FF = 5504, 8, 2, 7168
NORM_EPS, GN_EPS, L2_EPS = 1e-6, 64e-5, 1e-12

P_Z, P_X, P_B, P_C = 0, 1024, 2048, 2304
P_Q, P_F, P_I, P_G = 2560, 3072, 3584, 4096
P_R, P_K, P_V, P_LR, P_DT, P_COLS = 4608, 5120, 5632, 6144, 6400, 6656
LANES = 128

SSD_CHUNK = 256
HGRN_BLOCK, HGRN_CHUNK = 256, 16
RWKV_BLOCK, RWKV_CHUNK = 256, 64
STEP_BB = 8

FFN_TF, FFN_TN, FFN_TM_UP, FFN_TM_DOWN = 512, 1024, 1024, 256
MOE_BM, MOE_SUB, MOE_TF, MOE_TN, MOE_KP = 512, 256, 1024, 1024, 2

NT_DIMS = (((1,), (1,)), ((), ()))
TN_DIMS = (((0,), (0,)), ((), ()))


def _cparams(sem, vmem_mb=48):
    return pltpu.CompilerParams(dimension_semantics=sem, vmem_limit_bytes=vmem_mb * 1024 * 1024)


def _dot(a, b):
    return jnp.dot(a.astype(BF16), b.astype(BF16), preferred_element_type=F32)


def _dot_nt(a, b):
    return lax.dot_general(a.astype(BF16), b.astype(BF16), NT_DIMS, preferred_element_type=F32)


def _dot_tn(a, b):
    return lax.dot_general(a.astype(BF16), b.astype(BF16), TN_DIMS, preferred_element_type=F32)


def _dot_hi(a, b):
    return jnp.dot(a, b, precision=HI, preferred_element_type=F32)


def _dot_nt_hi(a, b):
    return lax.dot_general(a, b, NT_DIMS, precision=HI, preferred_element_type=F32)


def _silu(x):
    return x * jax.nn.sigmoid(x)


def _softplus(x):
    return jnp.maximum(x, 0.0) + jnp.log1p(jnp.exp(-jnp.abs(x)))


def _iota(shape, dim):
    return lax.broadcasted_iota(jnp.int32, shape, dim)


def _rms(x, g):
    return x * lax.rsqrt(jnp.mean(x * x, axis=-1, keepdims=True) + NORM_EPS) * g


def _full(shape):
    nd = len(shape)
    return pl.BlockSpec(shape, lambda *_: (0,) * nd)


def _norm_matmul_kernel(x_ref, g_ref, w_ref, o_ref):
    h = _rms(x_ref[...], g_ref[...]).astype(BF16)
    o_ref[...] = lax.dot_general(h, w_ref[...], NT_DIMS, preferred_element_type=F32)


def _norm_matmul(x, g, w, tm):
    m, k = x.shape
    n = w.shape[0]
    return pl.pallas_call(
        _norm_matmul_kernel,
        grid=(m // tm,),
        in_specs=[pl.BlockSpec((tm, k), lambda i: (i, 0)),
                  pl.BlockSpec((1, k), lambda i: (0, 0)),
                  pl.BlockSpec((n, k), lambda i: (0, 0), pipeline_mode=pl.Buffered(1))],
        out_specs=pl.BlockSpec((tm, n), lambda i: (i, 0)),
        out_shape=jax.ShapeDtypeStruct((m, n), F32),
        compiler_params=_cparams(("parallel",), 56),
        name="norm_matmul",
    )(x, g.reshape(1, k), w)


def _out_proj_kernel(routed, ya_ref, yb_ref, yc_ref, w_ref, r_ref, g_ref, *rest):
    x = (r_ref[...]
         + jnp.dot(ya_ref[...], w_ref[0:A_DIM, :], preferred_element_type=F32)
         + jnp.dot(yb_ref[...], w_ref[A_DIM:A_DIM + B_DIM, :], preferred_element_type=F32)
         + jnp.dot(yc_ref[...], w_ref[A_DIM + B_DIM:, :], preferred_element_type=F32))
    h = _rms(x, g_ref[...])
    h_hi = h.astype(BF16)
    if routed:
        rw_ref, _, _, x_ref, h_ref, l_ref = rest
        h_lo = (h - h_hi.astype(F32)).astype(BF16)
        rw = rw_ref[...]
        l2 = jnp.dot(h_hi, rw, preferred_element_type=F32)
        l_ref[...] = (l2[:, :LANES] + l2[:, LANES:]
                      + jnp.dot(h_lo, rw[:, :LANES], preferred_element_type=F32))
    else:
        x_ref, h_ref = rest
    x_ref[...] = x
    h_ref[...] = h_hi


def _out_proj(ya, yb, yc, w, res, g, tm, routed=None):
    m = ya.shape[0]
    n = w.shape[1]
    row = lambda width: pl.BlockSpec((tm, width), lambda i: (i, 0))
    in_specs = [row(A_DIM), row(B_DIM), row(C_DIM), _full(w.shape), row(n), _full((1, n))]
    args = [ya, yb, yc, w, res, g.reshape(1, n)]
    aliases = {}
    if routed is None:
        out_specs = [row(n), row(n)]
        out_shape = [jax.ShapeDtypeStruct((m, n), F32), jax.ShapeDtypeStruct((m, n), BF16)]
    else:
        rw, h_buf, l_buf, row0 = routed
        blk0 = row0 // tm
        off = lambda width: pl.BlockSpec((tm, width), lambda i: (i + blk0, 0))
        in_specs += [_full(rw.shape), pl.BlockSpec(memory_space=pl.ANY), pl.BlockSpec(memory_space=pl.ANY)]
        args += [rw, h_buf, l_buf]
        out_specs = [row(n), off(n), off(LANES)]
        out_shape = [jax.ShapeDtypeStruct((m, n), F32), jax.ShapeDtypeStruct(h_buf.shape, BF16),
                     jax.ShapeDtypeStruct(l_buf.shape, F32)]
        aliases = {len(args) - 2: 1, len(args) - 1: 2}
    outs = pl.pallas_call(
        functools.partial(_out_proj_kernel, routed is not None),
        grid=(m // tm,),
        in_specs=in_specs, out_specs=out_specs, out_shape=out_shape,
        input_output_aliases=aliases,
        compiler_params=_cparams(("parallel",)),
        name="out_proj",
    )(*args)
    return outs if routed is not None else (*outs, None)


def _ffn_up_kernel(h_ref, wg_ref, wu_ref, o_ref, wg_s, wu_s):
    @pl.when(pl.program_id(1) == 0)
    def _():
        wg_s[...] = wg_ref[...].astype(BF16)
        wu_s[...] = wu_ref[...].astype(BF16)

    h = h_ref[...]
    a = jnp.dot(h, wg_s[...], preferred_element_type=F32)
    u = jnp.dot(h, wu_s[...], preferred_element_type=F32)
    o_ref[...] = (_silu(a) * u).astype(BF16)


def _ffn_down_kernel(h_ref, wd_ref, x_ref, o_ref, wd_s):
    @pl.when(pl.program_id(1) == 0)
    def _():
        wd_s[...] = wd_ref[...].astype(BF16)

    o_ref[...] = x_ref[...] + jnp.dot(h_ref[...], wd_s[...], preferred_element_type=F32)


def _ffn(x, h, wg, wu, wd):
    m, k = h.shape
    ff = wg.shape[1]
    tf, tn = FFN_TF, FFN_TN
    tu = min(m, FFN_TM_UP)
    tm = min(m, FFN_TM_DOWN)
    act = pl.pallas_call(
        _ffn_up_kernel,
        grid=(pl.cdiv(ff, tf), m // tu),
        in_specs=[pl.BlockSpec((tu, k), lambda f, i: (i, 0)),
                  pl.BlockSpec((k, tf), lambda f, i: (0, f)),
                  pl.BlockSpec((k, tf), lambda f, i: (0, f))],
        out_specs=pl.BlockSpec((tu, tf), lambda f, i: (i, f)),
        out_shape=jax.ShapeDtypeStruct((m, ff), BF16),
        scratch_shapes=[pltpu.VMEM((k, tf), BF16)] * 2,
        compiler_params=_cparams(("arbitrary", "arbitrary")),
        name="ffn_up",
    )(h, wg, wu)
    return pl.pallas_call(
        _ffn_down_kernel,
        grid=(k // tn, m // tm),
        in_specs=[pl.BlockSpec((tm, ff), lambda n, i: (i, 0)),
                  pl.BlockSpec((ff, tn), lambda n, i: (0, n), pipeline_mode=pl.Buffered(1)),
                  pl.BlockSpec((tm, tn), lambda n, i: (i, n))],
        out_specs=pl.BlockSpec((tm, tn), lambda n, i: (i, n)),
        out_shape=jax.ShapeDtypeStruct((m, k), F32),
        scratch_shapes=[pltpu.VMEM((ff, tn), BF16)],
        compiler_params=_cparams(("arbitrary", "arbitrary"), 56),
        name="ffn_down",
    )(act, wd, x)


def _moe_rows(nsub, o_ref, compute):
    bm = o_ref.shape[0]
    for ns in range(bm // MOE_SUB + 1):
        @pl.when(nsub == ns)
        def _(rows=ns * MOE_SUB):
            if rows:
                o_ref[0:rows, :] = compute(rows).astype(o_ref.dtype)
            if rows < bm:
                o_ref[rows:, :] = jnp.zeros((bm - rows, o_ref.shape[1]), o_ref.dtype)


def _weight_stream(sched, w_hbms, wbuf, sem, dsts):
    bexp_ref, first_ref, run_ref, nxt_ref, lastrun_ref, nruns_ref = sched
    col, b = pl.program_id(0), pl.program_id(1)
    ncol = pl.num_programs(0)
    width = wbuf.shape[-1]

    def copies(e, c, slot):
        cols = pl.ds(pl.multiple_of(c * width, width), width)
        return [pltpu.make_async_copy(w.at[e, :, cols], wbuf.at[slot, i], sem.at[slot])
                for i, w in enumerate(w_hbms)]

    @pl.when(first_ref[b] == 1)
    def _():
        slot = lax.rem(col * nruns_ref[0] + run_ref[b], 2)
        last = lastrun_ref[b]

        @pl.when((col == 0) & (b == 0))
        def _():
            for cp in copies(bexp_ref[0], 0, 0):
                cp.start()

        @pl.when((last == 0) | (col < ncol - 1))
        def _():
            for cp in copies(nxt_ref[b], col + last, 1 - slot):
                cp.start(priority=1)

        for cp in copies(bexp_ref[b], col, slot):
            cp.wait()
        for i, dst in enumerate(dsts):
            dst[...] = wbuf[slot, i].astype(BF16)


def _moe_up_kernel(*refs):
    sched, (nsub_ref, _, x_ref, wg_hbm, wu_hbm, o_ref, wbuf, sem, wg_s, wu_s) = refs[:6], refs[6:]
    _weight_stream(sched, (wg_hbm, wu_hbm), wbuf, sem, (wg_s, wu_s))

    def compute(rows):
        x = x_ref[0:rows, :]
        a = jnp.dot(x, wg_s[...], preferred_element_type=F32)
        u = jnp.dot(x, wu_s[...], preferred_element_type=F32)
        return _silu(a) * u

    _moe_rows(nsub_ref[pl.program_id(1)], o_ref, compute)


def _moe_down_kernel(*refs):
    sched, (nsub_ref, _, h_ref, wd_hbm, o_ref, wbuf, sem, wd_s) = refs[:6], refs[6:]
    bexp_ref, first_ref, _, nxt_ref, lastrun_ref, _ = sched
    col, b, kk = pl.program_id(0), pl.program_id(1), pl.program_id(2)
    ncol = pl.num_programs(0)
    parts, kp, width = wbuf.shape

    def copies(e, c):
        cols = pl.ds(pl.multiple_of(c * width, width), width)
        return [pltpu.make_async_copy(wd_hbm.at[e, pl.ds(i * kp, kp), cols], wbuf.at[i], sem.at[i])
                for i in range(parts)]

    @pl.when((first_ref[b] == 1) & (kk == 0))
    def _():
        last = lastrun_ref[b]

        @pl.when((col == 0) & (b == 0))
        def _():
            for cp in copies(bexp_ref[0], 0):
                cp.start()

        for cp in copies(bexp_ref[b], col):
            cp.wait()
        rc = 512

        def cast(r, carry):
            rows = pl.ds(pl.multiple_of(r * rc, rc), rc)
            for i in range(parts):
                wd_s[i, rows, :] = wbuf[i, rows, :].astype(BF16)
            return carry

        lax.fori_loop(0, kp // rc, cast, 0)

        @pl.when((last == 0) | (col < ncol - 1))
        def _():
            for cp in copies(nxt_ref[b], col + last):
                cp.start(priority=1)

    valid = nsub_ref[b] > 0
    for i in range(parts):
        @pl.when(valid & (kk == i))
        def _(i=i):
            acc = jnp.dot(h_ref[...], wd_s[i], preferred_element_type=F32)
            if i == 0:
                o_ref[...] = acc
            else:
                o_ref[...] += acc

    @pl.when(jnp.logical_not(valid) & (kk == 0))
    def _():
        o_ref[...] = jnp.zeros_like(o_ref)


def _run_schedule(bexp, first):
    nb = bexp.shape[0]
    ids = jnp.arange(nb, dtype=jnp.int32)
    run = jnp.cumsum(first).astype(jnp.int32) - 1
    nruns = run[-1] + 1
    starts = jnp.where(first == 1, ids, nb)
    nxt_start = jnp.concatenate([lax.cummin(starts[::-1])[::-1][1:], jnp.full((1,), nb, jnp.int32)])
    nxt = jnp.where(nxt_start < nb, bexp[jnp.minimum(nxt_start, nb - 1)], bexp[0]).astype(jnp.int32)
    lastrun = (run == nruns - 1).astype(jnp.int32)
    return bexp, first, run, nxt, lastrun, nruns.reshape(1)


def _moe_experts(xb, bexp, first, nsub, wg, wu, wd):
    n_slots, k = xb.shape
    ff = wg.shape[2]
    nb = n_slots // MOE_BM
    tf, tn = MOE_TF, MOE_TN
    hbm = pl.BlockSpec(memory_space=pl.ANY)

    def src_blocks(ns):
        return lax.cummax(jnp.where(ns > 0, jnp.arange(ns.shape[0], dtype=jnp.int32), 0))

    act = pl.pallas_call(
        _moe_up_kernel,
        grid_spec=pltpu.PrefetchScalarGridSpec(
            num_scalar_prefetch=8,
            grid=(ff // tf, nb),
            in_specs=[pl.BlockSpec((MOE_BM, k), lambda f, b, *s: (s[7][b], 0)), hbm, hbm],
            out_specs=pl.BlockSpec((MOE_BM, tf), lambda f, b, *s: (b, f)),
            scratch_shapes=[pltpu.VMEM((2, 2, k, tf), F32), pltpu.SemaphoreType.DMA((2,)),
                            pltpu.VMEM((k, tf), BF16), pltpu.VMEM((k, tf), BF16)],
        ),
        out_shape=jax.ShapeDtypeStruct((n_slots, ff), BF16),
        compiler_params=_cparams(("arbitrary", "arbitrary"), 56),
        name="moe_up",
    )(*_run_schedule(bexp, first), nsub, src_blocks(nsub), xb, wg, wu)
    per = MOE_BM // MOE_SUB
    bexp_d = jnp.repeat(bexp, per)
    first_d = jnp.concatenate([jnp.ones((1,), jnp.int32), (bexp_d[1:] != bexp_d[:-1]).astype(jnp.int32)])
    nsub_d = jnp.clip(jnp.repeat(nsub, per) - jnp.tile(jnp.arange(per, dtype=jnp.int32), nb), 0, 1)
    return pl.pallas_call(
        _moe_down_kernel,
        grid_spec=pltpu.PrefetchScalarGridSpec(
            num_scalar_prefetch=8,
            grid=(k // tn, nb * per, MOE_KP),
            in_specs=[pl.BlockSpec((MOE_SUB, ff // MOE_KP), lambda n, b, kk, *s: (s[7][b], kk)), hbm],
            out_specs=pl.BlockSpec((MOE_SUB, tn), lambda n, b, kk, *s: (b, n)),
            scratch_shapes=[pltpu.VMEM((MOE_KP, ff // MOE_KP, tn), F32), pltpu.SemaphoreType.DMA((MOE_KP,)),
                            pltpu.VMEM((MOE_KP, ff // MOE_KP, tn), BF16)],
        ),
        out_shape=jax.ShapeDtypeStruct((n_slots, k), F32),
        compiler_params=_cparams(("arbitrary", "arbitrary", "arbitrary"), 56),
        name="moe_down",
    )(*_run_schedule(bexp_d, first_d), nsub_d, src_blocks(nsub_d), act, wd)


def _final_norm_kernel(x_ref, g_ref, o_ref):
    o_ref[...] = _rms(x_ref[...], g_ref[...])


def _final_norm(x, g, tm):
    m, k = x.shape
    return pl.pallas_call(
        _final_norm_kernel,
        grid=(m // tm,),
        in_specs=[pl.BlockSpec((tm, k), lambda i: (i, 0)),
                  pl.BlockSpec((1, k), lambda i: (0, 0))],
        out_specs=pl.BlockSpec((tm, k), lambda i: (i, 0)),
        out_shape=jax.ShapeDtypeStruct((m, k), F32),
        compiler_params=_cparams(("parallel",)),
        name="final_norm",
    )(x, g.reshape(1, k))


def _row_copy(src_hbm, row, dst, r, sem):
    return pltpu.make_async_copy(src_hbm.at[pl.ds(row, 1), :], dst.at[pl.ds(r, 1), :], sem)


def _moe_combine_kernel(final, s0_ref, s1_ref, x_ref, gate_ref, g_ref, yb_hbm, o_ref, buf, sem):
    rows = x_ref.shape[0]
    base = pl.program_id(0) * rows

    def issue(r, carry):
        _row_copy(yb_hbm, s0_ref[base + r], buf.at[0], r, sem).start(priority=0)
        _row_copy(yb_hbm, s1_ref[base + r], buf.at[1], r, sem).start(priority=1)
        return carry

    lax.fori_loop(0, rows, issue, 0)
    pltpu.make_async_copy(buf, buf, sem).wait()
    gate = gate_ref[...]
    y = x_ref[...] + gate[:, 0:1] * buf[0] + gate[:, 1:2] * buf[1]
    o_ref[...] = _rms(y, g_ref[...]) if final else y


def _moe_combine(x, yb, slot_of, gate, g, final):
    m, k = x.shape
    rows = min(m, 512)
    gate_p = jnp.pad(gate, ((0, 0), (0, LANES - TOP_K)))
    return pl.pallas_call(
        functools.partial(_moe_combine_kernel, final),
        grid_spec=pltpu.PrefetchScalarGridSpec(
            num_scalar_prefetch=2,
            grid=(m // rows,),
            in_specs=[pl.BlockSpec((rows, k), lambda i, s0, s1: (i, 0)),
                      pl.BlockSpec((rows, LANES), lambda i, s0, s1: (i, 0)),
                      pl.BlockSpec((1, k), lambda i, s0, s1: (0, 0)),
                      pl.BlockSpec(memory_space=pl.ANY)],
            out_specs=pl.BlockSpec((rows, k), lambda i, s0, s1: (i, 0)),
            scratch_shapes=[pltpu.VMEM((TOP_K, rows, k), F32), pltpu.SemaphoreType.DMA],
        ),
        out_shape=jax.ShapeDtypeStruct((m, k), F32),
        compiler_params=_cparams(("arbitrary",)),
        name="moe_combine",
    )(slot_of[:, 0], slot_of[:, 1], x, gate_p, g.reshape(1, k), yb)


def _ssd_gate_norm(y, z, ng):
    y = y * _silu(z)
    gw = A_DIM // A_GROUPS
    outs = []
    for g in range(A_GROUPS):
        yg = y[:, g * gw:(g + 1) * gw]
        outs.append(yg * lax.rsqrt(jnp.mean(yg * yg, axis=-1, keepdims=True) + NORM_EPS))
    return jnp.concatenate(outs, axis=-1) * ng


def _ssd_prompt_kernel(z_ref, x_ref, b_ref, c_ref, dt_ref, cwx, cwb, cwc, cbx, cbb, cbc,
                       dtb, aneg, dsk, ng, y_ref, st_ref, h_scr, xpx, xpb, xpc, yacc):
    c = pl.program_id(1)
    ch = SSD_CHUNK

    @pl.when(c == 0)
    def _():
        h_scr[...] = jnp.zeros_like(h_scr)
        xpx[0:8, :] = jnp.zeros((8, A_DIM), F32)
        xpb[0:8, :] = jnp.zeros((8, A_STATE * A_GROUPS), F32)
        xpc[0:8, :] = jnp.zeros((8, A_STATE * A_GROUPS), F32)

    xpx[8:8 + ch, :] = x_ref[...]
    xpb[8:8 + ch, :] = b_ref[...]
    xpc[8:8 + ch, :] = c_ref[...]

    def conv(xp, cw, cb):
        acc = cb[...] + xp[pl.ds(8 - (CONV_W - 1), ch), :] * cw[0:1, :]
        for tap in range(1, CONV_W):
            acc = acc + xp[pl.ds(8 - (CONV_W - 1) + tap, ch), :] * cw[tap:tap + 1, :]
        return _silu(acc)

    ux = conv(xpx, cwx, cbx)
    ub = conv(xpb, cwb, cbb)
    uc = conv(xpc, cwc, cbc)
    xpx[0:8, :] = xpx[ch:ch + 8, :]
    xpb[0:8, :] = xpb[ch:ch + 8, :]
    xpc[0:8, :] = xpc[ch:ch + 8, :]

    dt = _softplus(dt_ref[...] + dtb[...])
    la = dt * aneg[...]
    tri = (_iota((ch, ch), 1) <= _iota((ch, ch), 0)).astype(F32)
    cum = _dot_hi(tri, la)
    eye = (_iota((A_HEADS, LANES), 0) == _iota((A_HEADS, LANES), 1)).astype(F32)
    cum_r = _dot_nt_hi(eye, cum)
    dt_r = _dot_nt_hi(eye, dt)
    cl = cum[ch - 1:ch, :]
    tail = jnp.exp(cl - cum) * dt
    ecum = jnp.exp(cum)
    ecl = jnp.exp(cl)
    causal = _iota((ch, ch), 1) <= _iota((ch, ch), 0)
    rep = A_HEADS // A_GROUPS

    for g in range(A_GROUPS):
        bg = ub[:, g * A_STATE:(g + 1) * A_STATE].astype(BF16)
        cg = uc[:, g * A_STATE:(g + 1) * A_STATE].astype(BF16)
        cb_g = _dot_nt(cg, bg)
        for h in range(g * rep, (g + 1) * rep):
            hs = slice(h * A_HEAD, (h + 1) * A_HEAD)
            seg = cum[:, h:h + 1] - cum_r[h:h + 1, :]
            w = jnp.exp(jnp.where(causal, seg, -jnp.inf)) * cb_g * dt_r[h:h + 1, :]
            xh = ux[:, hs]
            hst = h_scr[h]
            y_h = _dot(w, xh) + _dot_nt(cg, hst) * ecum[:, h:h + 1]
            yacc[:, hs] = y_h
            h_scr[h] = hst * ecl[:, h:h + 1] + _dot_tn(xh * tail[:, h:h + 1], bg)

    y = yacc[...] + dsk[...] * ux
    y_ref[...] = _ssd_gate_norm(y, z_ref[...], ng[...]).astype(BF16)

    @pl.when(c == pl.num_programs(1) - 1)
    def _():
        st_ref[...] = h_scr[...]


def _ssd_params(p):
    cw, cb = p["conv_w"], p["conv_b"].reshape(1, CONV_CH)
    pad = LANES - A_HEADS
    return dict(
        cwx=cw[:, :A_DIM], cwb=cw[:, A_DIM:A_DIM + A_GN], cwc=cw[:, A_DIM + A_GN:],
        cbx=cb[:, :A_DIM], cbb=cb[:, A_DIM:A_DIM + A_GN], cbc=cb[:, A_DIM + A_GN:],
        dtb=jnp.pad(p["dt_bias"], (0, pad)).reshape(1, LANES),
        aneg=jnp.pad(-jnp.exp(p["a_log"]), (0, pad)).reshape(1, LANES),
        dsk=jnp.repeat(p["d_skip"], A_HEAD).reshape(1, A_DIM),
        ng=p["ssm_norm_g"].reshape(1, A_DIM),
    )


def _ssd_prompt(proj, sp):
    bsz, seq, _ = proj.shape
    ch = SSD_CHUNK
    gn = A_GN

    def col(width, off):
        return pl.BlockSpec((None, ch, width), lambda b, c: (b, c, off // width))

    names = ("cwx", "cwb", "cwc", "cbx", "cbb", "cbc", "dtb", "aneg", "dsk", "ng")
    y, st = pl.pallas_call(
        _ssd_prompt_kernel,
        grid=(bsz, seq // ch),
        in_specs=[col(A_DIM, P_Z), col(A_DIM, P_X), col(gn, P_B), col(gn, P_C), col(LANES, P_DT)]
                 + [_full(sp[n].shape) for n in names],
        out_specs=[pl.BlockSpec((None, ch, A_DIM), lambda b, c: (b, c, 0)),
                   pl.BlockSpec((None, A_HEADS, A_HEAD, A_STATE), lambda b, c: (b, 0, 0, 0))],
        out_shape=[jax.ShapeDtypeStruct((bsz, seq, A_DIM), BF16),
                   jax.ShapeDtypeStruct((bsz, A_HEADS, A_HEAD, A_STATE), F32)],
        scratch_shapes=[pltpu.VMEM((A_HEADS, A_HEAD, A_STATE), F32),
                        pltpu.VMEM((ch + 8, A_DIM), F32),
                        pltpu.VMEM((ch + 8, gn), F32),
                        pltpu.VMEM((ch + 8, gn), F32),
                        pltpu.VMEM((ch, A_DIM), F32)],
        compiler_params=_cparams(("parallel", "arbitrary")),
        name="ssd_prompt",
    )(proj, proj, proj, proj, proj, *[sp[n] for n in names])
    return y, st


def _ssd_step_kernel(z_ref, x_ref, b_ref, c_ref, dt_ref, cs_ref, cw, cb, dtb, anegx, expand, dsk, ng,
                     h_ref, _, y_ref, ho_ref, y_scr):
    bb = STEP_BB
    xbc = (x_ref[...], b_ref[...], c_ref[...])
    offs = (0, A_DIM, A_DIM + A_GN, CONV_CH)
    u = []
    for i in range(3):
        sl = slice(offs[i], offs[i + 1])
        acc = cb[:, sl] + xbc[i] * cw[CONV_W - 1:CONV_W, sl]
        for tap in range(CONV_W - 1):
            acc = acc + cs_ref[tap][:, sl] * cw[tap:tap + 1, sl]
        u.append(_silu(acc))
    ux, ub, uc = u
    dt = _softplus(dt_ref[...] + dtb[...])
    dtx = _dot_hi(dt, expand[...])
    dec_t = jnp.exp(dtx * anegx[...]).T
    xdt_t = (ux * dtx).T
    gw = A_DIM // A_GROUPS
    for b in range(bb):
        for g in range(A_GROUPS):
            rs = slice(g * gw, (g + 1) * gw)
            ns = slice(g * A_STATE, (g + 1) * A_STATE)
            hg = h_ref[b, rs, :] * dec_t[rs, b:b + 1] + xdt_t[rs, b:b + 1] * ub[b:b + 1, ns]
            ho_ref[b, rs, :] = hg
            y_scr[b:b + 1, rs] = _dot_nt(uc[:, ns], hg)[b:b + 1, :]
    y = y_scr[...] + dsk[...] * ux
    y_ref[...] = _ssd_gate_norm(y, z_ref[...], ng[...]).astype(BF16)


def _ssd_step(proj, conv_state, h_all, h_new, layer, p, sp):
    bsz = proj.shape[0]
    bb = STEP_BB
    cs = jnp.swapaxes(conv_state, 0, 1)
    hflat = h_all.reshape(h_all.shape[0], bsz, A_DIM, A_STATE)
    expand = (jnp.arange(LANES)[:, None] == (jnp.arange(A_DIM) // A_HEAD)[None, :]).astype(F32)
    anegx = jnp.repeat(-jnp.exp(p["a_log"]), A_HEAD).reshape(1, A_DIM)
    cw, cb = p["conv_w"], p["conv_b"].reshape(1, CONV_CH)

    def col(width, off):
        return pl.BlockSpec((bb, width), lambda i: (i, off // width))

    y, hn = pl.pallas_call(
        _ssd_step_kernel,
        grid=(bsz // bb,),
        in_specs=[col(A_DIM, P_Z), col(A_DIM, P_X), col(A_GN, P_B), col(A_GN, P_C), col(LANES, P_DT),
                  pl.BlockSpec((CONV_W - 1, bb, CONV_CH), lambda i: (0, i, 0)),
                  _full(cw.shape), _full(cb.shape), _full(sp["dtb"].shape), _full(anegx.shape),
                  _full(expand.shape), _full(sp["dsk"].shape), _full(sp["ng"].shape),
                  pl.BlockSpec((None, bb, A_DIM, A_STATE), lambda i: (layer, i, 0, 0)),
                  pl.BlockSpec(memory_space=pl.ANY)],
        out_specs=[pl.BlockSpec((bb, A_DIM), lambda i: (i, 0)),
                   pl.BlockSpec((None, bb, A_DIM, A_STATE), lambda i: (layer, i, 0, 0))],
        out_shape=[jax.ShapeDtypeStruct((bsz, A_DIM), BF16),
                   jax.ShapeDtypeStruct(hflat.shape, F32)],
        input_output_aliases={14: 1},
        scratch_shapes=[pltpu.VMEM((bb, A_DIM), F32)],
        compiler_params=_cparams(("parallel",)),
        name="ssd_step",
    )(proj, proj, proj, proj, proj, cs, cw, cb, sp["dtb"], anegx, expand, sp["dsk"], sp["ng"], hflat,
      h_new.reshape(hflat.shape))
    return y, hn.reshape(h_all.shape)


def _hgrn_gates(q, f, lbp):
    qf = _silu(q)
    kf = lbp[0:1, :] * jax.nn.sigmoid(-f)
    log_sig = jnp.minimum(f, 0.0) - jnp.log1p(jnp.exp(-jnp.abs(f)))
    a = lbp[1:2, :]
    b = lbp[2:3, :] + log_sig
    logf = jnp.maximum(a, b) + jnp.log1p(jnp.exp(-jnp.abs(a - b)))
    return qf, kf, logf


def _hgrn_out(o, g, ng):
    outs = []
    for h in range(B_HEADS):
        oh = o[:, h * B_HEAD:(h + 1) * B_HEAD]
        outs.append(oh * lax.rsqrt(jnp.mean(oh * oh, axis=-1, keepdims=True) + NORM_EPS) * ng)
    return jnp.concatenate(outs, axis=-1) * _silu(g)


def _hgrn_prompt_kernel(q_ref, f_ref, i_ref, g_ref, lbp, ng, o_ref, st_ref,
                        s_scr, cum_scr, q_scr, k_scr, o_scr):
    c = pl.program_id(1)
    tb, cs = HGRN_BLOCK, HGRN_CHUNK

    @pl.when(c == 0)
    def _():
        s_scr[...] = jnp.zeros_like(s_scr)

    qf, kf, logf = _hgrn_gates(q_ref[...], f_ref[...], lbp[...])
    r_i, c_i = _iota((tb, tb), 0), _iota((tb, tb), 1)
    bd = ((r_i // cs == c_i // cs) & (c_i <= r_i)).astype(F32)
    cum_scr[...] = _dot_hi(bd, logf)
    q_scr[...] = qf
    k_scr[...] = kf
    rows = _iota((cs, B_HEAD), 0)

    def chunk(cc, carry):
        r0 = pl.multiple_of(cc * cs, cs)
        for h in range(B_HEADS):
            hs = slice(h * B_HEAD, (h + 1) * B_HEAD)
            cu = cum_scr[pl.ds(r0, cs), hs]
            q = q_scr[pl.ds(r0, cs), hs]
            k = k_scr[pl.ds(r0, cs), hs]
            v = i_ref[pl.ds(r0, cs), hs]
            st = s_scr[h]
            o = _dot_nt(q * jnp.exp(cu), st)
            for s in range(cs):
                d = jnp.exp(jnp.where(rows >= s, cu - cu[s:s + 1, :], -jnp.inf))
                att = jnp.sum(q * k[s:s + 1, :] * d, axis=-1, keepdims=True)
                o = o + att * v[s:s + 1, :]
            o_scr[pl.ds(r0, cs), hs] = o
            cl = cu[cs - 1:cs, :]
            s_scr[h] = st * jnp.exp(cl) + _dot_tn(v, k * jnp.exp(cl - cu))
        return carry

    lax.fori_loop(0, tb // cs, chunk, 0)
    o_ref[...] = _hgrn_out(o_scr[...], g_ref[...], ng[...]).astype(BF16)

    @pl.when(c == pl.num_programs(1) - 1)
    def _():
        for h in range(B_HEADS):
            st_ref[h] = s_scr[h].T


def _hgrn_prompt(proj, lbp, ng):
    bsz, seq, _ = proj.shape
    tb = HGRN_BLOCK

    def col(off):
        return pl.BlockSpec((None, tb, B_DIM), lambda b, c: (b, c, off // B_DIM))

    o, st = pl.pallas_call(
        _hgrn_prompt_kernel,
        grid=(bsz, seq // tb),
        in_specs=[col(P_Q), col(P_F), col(P_I), col(P_G), _full(lbp.shape), _full(ng.shape)],
        out_specs=[pl.BlockSpec((None, tb, B_DIM), lambda b, c: (b, c, 0)),
                   pl.BlockSpec((None, B_HEADS, B_HEAD, B_HEAD), lambda b, c: (b, 0, 0, 0))],
        out_shape=[jax.ShapeDtypeStruct((bsz, seq, B_DIM), BF16),
                   jax.ShapeDtypeStruct((bsz, B_HEADS, B_HEAD, B_HEAD), F32)],
        scratch_shapes=[pltpu.VMEM((B_HEADS, B_HEAD, B_HEAD), F32)]
                       + [pltpu.VMEM((tb, B_DIM), F32)] * 4,
        compiler_params=_cparams(("parallel", "arbitrary")),
        name="hgrn_prompt",
    )(proj, proj, proj, proj, lbp, ng)
    return o, st


def _hgrn_step_kernel(q_ref, f_ref, i_ref, g_ref, lbp, ng, s_ref, _, o_ref, so_ref, o_scr):
    bb = STEP_BB
    qf, kf, logf = _hgrn_gates(q_ref[...], f_ref[...], lbp[...])
    vf = i_ref[...]
    dec_t = jnp.exp(logf).T
    k_t = kf.T
    q_t = qf.T
    for b in range(bb):
        for h in range(B_HEADS):
            hs = slice(h * B_HEAD, (h + 1) * B_HEAD)
            sh = s_ref[b, hs, :] * dec_t[hs, b:b + 1] + k_t[hs, b:b + 1] * vf[b:b + 1, hs]
            so_ref[b, hs, :] = sh
            o_scr[b:b + 1, hs] = jnp.sum(q_t[hs, b:b + 1] * sh, axis=0, keepdims=True)
    o_ref[...] = _hgrn_out(o_scr[...], g_ref[...], ng[...]).astype(BF16)


def _hgrn_step(proj, s_all, s_new, layer, lbp, ng):
    bsz = proj.shape[0]
    bb = STEP_BB
    sflat = s_all.reshape(s_all.shape[0], bsz, B_DIM, B_HEAD)

    def col(off):
        return pl.BlockSpec((bb, B_DIM), lambda i: (i, off // B_DIM))

    o, sn = pl.pallas_call(
        _hgrn_step_kernel,
        grid=(bsz // bb,),
        in_specs=[col(P_Q), col(P_F), col(P_I), col(P_G), _full(lbp.shape), _full(ng.shape),
                  pl.BlockSpec((None, bb, B_DIM, B_HEAD), lambda i: (layer, i, 0, 0)),
                  pl.BlockSpec(memory_space=pl.ANY)],
        out_specs=[pl.BlockSpec((bb, B_DIM), lambda i: (i, 0)),
                   pl.BlockSpec((None, bb, B_DIM, B_HEAD), lambda i: (layer, i, 0, 0))],
        out_shape=[jax.ShapeDtypeStruct((bsz, B_DIM), BF16),
                   jax.ShapeDtypeStruct(sflat.shape, F32)],
        input_output_aliases={7: 1},
        scratch_shapes=[pltpu.VMEM((bb, B_DIM), F32)],
        compiler_params=_cparams(("parallel",)),
        name="hgrn_step",
    )(proj, proj, proj, proj, lbp, ng, sflat, s_new.reshape(sflat.shape))
    return o, sn.reshape(s_all.shape)


RW_NAMES = ("mu_r", "mu_k", "mu_v", "mu_lr", "w0", "w2", "a0", "a2", "g2", "k_k", "k_a", "r_k")


def _rwkv_params(p):
    mu = p["shift_mu"].reshape(1, C_COLS)
    row = lambda a: a.reshape(1, C_DIM)
    return dict(
        mu_r=mu[:, :C_DIM], mu_k=mu[:, C_DIM:2 * C_DIM], mu_v=mu[:, 2 * C_DIM:3 * C_DIM],
        mu_lr=mu[:, 3 * C_DIM:],
        w0=row(p["w0"]), w2=p["w2"].astype(BF16), a0=row(p["a0"]), a2=p["a2"].astype(BF16),
        g2=p["g2"].astype(BF16), k_k=row(p["k_k"]), k_a=row(p["k_a"]), r_k=row(p["r_k"]),
        lnx_w=row(p["lnx_w"]), lnx_b=row(p["lnx_b"]),
    )


def _head_sum(x):
    return [jnp.sum(x[:, h * C_HEAD:(h + 1) * C_HEAD], axis=-1, keepdims=True) for h in range(C_HEADS)]


def _head_bcast(cols, fn=lambda c: c):
    rows = cols[0].shape[0]
    return jnp.concatenate([jnp.broadcast_to(fn(c), (rows, C_HEAD)) for c in cols], axis=-1)


def _rwkv_prep(r, k, v, lr, pr, pk, pv, plr, prm):
    mu_r, mu_k, mu_v, mu_lr, w0, w2, a0, a2, g2, k_k, k_a, r_k = prm
    r = r + (pr - r) * mu_r
    k = k + (pk - k) * mu_k
    v = v + (pv - v) * mu_v
    lr = lr + (plr - lr) * mu_lr
    wl, al, gl = lr[:, :C_LR_W], lr[:, C_LR_W:C_LR_W + C_LR_A], lr[:, C_LR_W + C_LR_A:]
    wraw = -_softplus(-(w0 + _dot(jnp.tanh(wl), w2))) - 0.5
    ld = -jnp.exp(wraw)
    a = jax.nn.sigmoid(a0 + _dot(al, a2))
    g = _dot(jax.nn.sigmoid(gl), g2)
    kk = k * k_k
    kk = kk * _head_bcast(_head_sum(kk * kk), lambda n2: 1.0 / jnp.maximum(jnp.sqrt(n2), L2_EPS))
    k2 = k * (1.0 + (a - 1.0) * k_a)
    bonus = _head_bcast(_head_sum(r * k2 * r_k)) * v
    return r, k2, v, kk, kk * a, ld, g, bonus


def _rwkv_out(y, bonus, g, lnx_w, lnx_b):
    n = float(C_HEAD)
    mean = _head_bcast(_head_sum(y), lambda s: s / n)
    yc = y - mean
    rstd = _head_bcast(_head_sum(yc * yc), lambda s: lax.rsqrt(s / n + GN_EPS))
    return (yc * rstd * lnx_w + lnx_b + bonus) * g


RW_GH = 4
RW_GW = RW_GH * C_HEAD


def _bd_mask():
    return _iota((RW_GW, RW_GW), 0) // C_HEAD == _iota((RW_GW, RW_GW), 1) // C_HEAD


def _bd_rows(a):
    return jnp.where(_bd_mask(), jnp.concatenate([a] * RW_GH, axis=0), 0.0)


def _bd_lanes(s):
    return jnp.where(_bd_mask(), jnp.concatenate([s] * RW_GH, axis=1), 0.0)


def _bd_fold(m):
    m = jnp.where(_bd_mask(), m, 0.0)
    out = m[:, :C_HEAD]
    for h in range(1, RW_GH):
        out = out + m[:, h * C_HEAD:(h + 1) * C_HEAD]
    return out


def _to_stack(a):
    return jnp.concatenate([a[:, h * C_HEAD:(h + 1) * C_HEAD] for h in range(RW_GH)], axis=0)


def _to_all(s):
    return jnp.concatenate([s[h * C_HEAD:(h + 1) * C_HEAD, :] for h in range(RW_GH)], axis=1)


def _rwkv_a_kernel(r_ref, k_ref, v_ref, lr_ref, rp_ref, kp_ref, vp_ref, lrp_ref, *rest):
    prm = tuple(x[...] for x in rest[:len(RW_NAMES)])
    (kd_ref, rd_ref, bg_ref, ge_ref, bonus_ref, gate_ref, aab_ref, rb_ref, m2_ref, rkv_ref, vk_ref,
     xs_scr, b1_scr, b2_scr, b3_scr, b4_scr) = rest[len(RW_NAMES):]
    c = pl.program_id(1)
    tb, cs = RWKV_BLOCK, RWKV_CHUNK
    first = (c == 0)

    def prev(cur_ref, prev_ref, width):
        xs_scr[8:8 + tb, 0:width] = cur_ref[...]
        xs_scr[7:8, 0:width] = jnp.where(first, 0.0, prev_ref[7:8, :])
        return xs_scr[pl.ds(7, tb), 0:width]

    pr = prev(r_ref, rp_ref, C_DIM)
    pk = prev(k_ref, kp_ref, C_DIM)
    pv = prev(v_ref, vp_ref, C_DIM)
    plr = prev(lr_ref, lrp_ref, C_LR)
    r, k2, v, kk, beta, ld, g, bonus = _rwkv_prep(
        r_ref[...], k_ref[...], v_ref[...], lr_ref[...], pr, pk, pv, plr, prm)
    bonus_ref[...] = bonus
    gate_ref[...] = g

    r_i, c_i = _iota((tb, tb), 0), _iota((tb, tb), 1)
    same = (r_i // cs == c_i // cs)
    lcum = _dot_hi((same & (c_i <= r_i)).astype(F32), ld)
    lend = _dot_hi(same.astype(F32), ld)
    einv = jnp.exp(-lcum)
    eend = jnp.exp(lend - lcum)
    ge_ref[...] = jnp.exp(lend)
    kd_ref[...] = kk * jnp.exp(lcum - ld)
    rd_ref[...] = r * jnp.exp(lcum)
    bg_ref[...] = beta * eend
    b1_scr[...] = beta * einv
    b2_scr[...] = k2 * einv
    b3_scr[...] = k2 * eend
    b4_scr[...] = v
    gw = RW_GW
    tpos, spos = _iota((gw, cs), 0) % cs, _iota((gw, cs), 1)
    strict, incl = tpos > spos, tpos >= spos
    hc = C_HEAD

    def chunk(j, carry):
        r0 = pl.multiple_of(j * cs, cs)
        q0 = pl.multiple_of(j * gw, gw)
        for g in range(C_HEADS // RW_GH):
            gs = slice(g * gw, (g + 1) * gw)
            ls = slice(g * hc, (g + 1) * hc)
            kd = kd_ref[pl.ds(r0, cs), gs]
            rd = rd_ref[pl.ds(r0, cs), gs]
            v = b4_scr[pl.ds(r0, cs), gs]
            lhs = jnp.concatenate([_bd_rows(kd), _bd_rows(rd)], axis=0)
            rhs = jnp.concatenate([b1_scr[pl.ds(r0, cs), gs], b2_scr[pl.ds(r0, cs), gs]], axis=0)
            gm = _dot_nt(lhs, rhs)
            aab_ref[pl.ds(q0, gw), ls] = jnp.where(strict, gm[:gw, :hc], 0.0)
            aak = jnp.where(strict, gm[:gw, hc:], 0.0)
            rb_ref[pl.ds(q0, gw), ls] = jnp.where(incl, gm[gw:, :hc], 0.0)
            rk = jnp.where(incl, gm[gw:, hc:], 0.0)
            mr = _dot(jnp.concatenate([_bd_lanes(aak), _bd_lanes(rk)], axis=0), _to_stack(v))
            m2_ref[pl.ds(q0, gw), ls] = mr[:gw]
            rkv_ref[pl.ds(q0, gw), ls] = mr[gw:]
            vk_ref[pl.ds(q0, gw), ls] = _bd_fold(_dot_tn(v, b3_scr[pl.ds(r0, cs), gs]))
        return carry

    lax.fori_loop(0, tb // cs, chunk, 0)


def _tri_solve_kernel(a_ref, t_ref, a_scr, t_scr):
    n, hf = RWKV_CHUNK, RWKV_CHUNK // 2
    npb = a_scr.shape[2]
    ng = a_scr.shape[1] // n

    def load(t, carry):
        a_scr[t] = a_ref[pl.ds(t, npb, stride=n), :].T
        return carry

    lax.fori_loop(0, n, load, 0)
    jrow = _iota((hf, npb), 0)
    zero = jnp.zeros((hf, npb), F32)

    def coef(t, g, s):
        return a_scr[t, pl.ds(g * n + s, 1), :]

    def row_lo(t, carry):
        def inner(s, accs):
            return tuple(acc - coef(t, g, s) * t_scr[s, g * n:g * n + hf, :] for g, acc in enumerate(accs))

        accs = lax.fori_loop(0, t, inner, ((jrow == t).astype(F32),) * ng)
        for g, acc in enumerate(accs):
            t_scr[t, g * n:g * n + hf, :] = acc
            t_scr[t, g * n + hf:(g + 1) * n, :] = zero
        return carry

    lax.fori_loop(0, hf, row_lo, 0)

    def row_hi(t, carry):
        def inner_lo(s, accs):
            return tuple(acc - coef(t, g, s) * t_scr[s, g * n:g * n + hf, :] for g, acc in enumerate(accs))

        def inner_hi(s, accs):
            lo, hi = accs[:ng], accs[ng:]
            lo = tuple(acc - coef(t, g, s) * t_scr[s, g * n:g * n + hf, :] for g, acc in enumerate(lo))
            hi = tuple(acc - coef(t, g, s) * t_scr[s, g * n + hf:(g + 1) * n, :] for g, acc in enumerate(hi))
            return lo + hi

        lo = lax.fori_loop(0, hf, inner_lo, (zero,) * ng)
        accs = lax.fori_loop(hf, t, inner_hi, lo + ((jrow + hf == t).astype(F32),) * ng)
        for g in range(ng):
            t_scr[t, g * n:g * n + hf, :] = accs[g]
            t_scr[t, g * n + hf:(g + 1) * n, :] = accs[ng + g]
        return carry

    lax.fori_loop(hf, n, row_hi, 0)

    def store(t, carry):
        t_ref[pl.ds(t, npb, stride=n), :] = t_scr[t].T
        return carry

    lax.fori_loop(0, n, store, 0)


def _tri_solve(a):
    rows, width = a.shape
    n = RWKV_CHUNK
    npb = min(LANES, rows // n)
    return pl.pallas_call(
        _tri_solve_kernel,
        grid=(rows // (npb * n),),
        in_specs=[pl.BlockSpec((npb * n, width), lambda i: (i, 0))],
        out_specs=pl.BlockSpec((npb * n, width), lambda i: (i, 0)),
        out_shape=jax.ShapeDtypeStruct(a.shape, F32),
        scratch_shapes=[pltpu.VMEM((n, width, npb), F32)] * 2,
        compiler_params=_cparams(("parallel",)),
        name="tri_solve",
    )(a)


def _rwkv_c_kernel(t_ref, kd_ref, rd_ref, bg_ref, ge_ref, bonus_ref, gate_ref, rb_ref, m2_ref, rkv_ref, vk_ref,
                   lnw, lnb, y_ref, st_ref, s_scr, y_scr):
    c = pl.program_id(1)
    tb, cs, gw, hc = RWKV_BLOCK, RWKV_CHUNK, RW_GW, C_HEAD

    @pl.when(c == 0)
    def _():
        s_scr[...] = jnp.zeros_like(s_scr)

    def chunk(j, carry):
        r0 = pl.multiple_of(j * cs, cs)
        q0 = pl.multiple_of(j * gw, gw)
        for g in range(C_HEADS // RW_GH):
            gs = slice(g * gw, (g + 1) * gw)
            ls = slice(g * hc, (g + 1) * hc)
            rhs = jnp.concatenate([_to_stack(kd_ref[pl.ds(r0, cs), gs]), m2_ref[pl.ds(q0, gw), ls]], axis=1)
            wu = _dot(_bd_lanes(t_ref[pl.ds(q0, gw), ls]), rhs)
            rbwu = _dot(_bd_lanes(rb_ref[pl.ds(q0, gw), ls]), wu)
            rt = _to_stack(rd_ref[pl.ds(r0, cs), gs]) - rbwu[:, :hc]
            yc = rkv_ref[pl.ds(q0, gw), ls] - rbwu[:, hc:]
            wu_bd = jnp.concatenate([_bd_lanes(wu[:, :hc]), _bd_lanes(wu[:, hc:])], axis=1)
            nu = _dot_tn(wu_bd, _to_stack(bg_ref[pl.ds(r0, cs), gs]))
            ge_row = ge_ref[pl.ds(r0, 1), gs]
            ge_st = jnp.concatenate(
                [jnp.broadcast_to(ge_row[:, h * hc:(h + 1) * hc], (hc, hc)) for h in range(RW_GH)], axis=0)
            s = s_scr[g]
            s_bd = _bd_lanes(s)
            y_scr[pl.ds(r0, cs), gs] = _dot_nt(_to_all(rt), s_bd) + _to_all(yc)
            s_scr[g] = s * ge_st - _dot(s_bd, nu[:gw]) + (vk_ref[pl.ds(q0, gw), ls] - nu[gw:])
        return carry

    lax.fori_loop(0, tb // cs, chunk, 0)
    y_ref[...] = _rwkv_out(y_scr[...], bonus_ref[...], gate_ref[...], lnw[...], lnb[...]).astype(BF16)

    @pl.when(c == pl.num_programs(1) - 1)
    def _():
        st_ref[...] = s_scr[...]


def _rwkv_prompt(proj, rp):
    bsz, seq, _ = proj.shape
    tb, cs = RWKV_BLOCK, RWKV_CHUNK
    nblk = seq // tb

    def col(width, off):
        return pl.BlockSpec((None, tb, width), lambda b, c: (b, c, off // width))

    def pcol(width, off):
        return pl.BlockSpec((None, 8, width), lambda b, c: (b, jnp.maximum(c * (tb // 8) - 1, 0), off // width))

    ng = C_HEADS // RW_GH
    srows = tb // cs * RW_GW
    tok = pl.BlockSpec((None, tb, C_DIM), lambda b, c: (b, c, 0))
    stk = pl.BlockSpec((None, srows, ng * C_HEAD), lambda b, c: (b, c, 0))
    tok_shape = jax.ShapeDtypeStruct((bsz, seq, C_DIM), F32)
    stk_shape = jax.ShapeDtypeStruct((bsz, nblk * srows, ng * C_HEAD), F32)
    prm = [rp[n] for n in RW_NAMES]
    outs = pl.pallas_call(
        _rwkv_a_kernel,
        grid=(bsz, nblk),
        in_specs=[col(C_DIM, P_R), col(C_DIM, P_K), col(C_DIM, P_V), col(C_LR, P_LR),
                  pcol(C_DIM, P_R), pcol(C_DIM, P_K), pcol(C_DIM, P_V), pcol(C_LR, P_LR)]
                 + [_full(x.shape) for x in prm],
        out_specs=[tok] * 6 + [stk] * 5,
        out_shape=[tok_shape] * 6 + [stk_shape] * 5,
        scratch_shapes=[pltpu.VMEM((tb + 8, C_DIM), F32)] + [pltpu.VMEM((tb, C_DIM), F32)] * 4,
        compiler_params=_cparams(("parallel", "parallel")),
        name="rwkv_prep",
    )(proj, proj, proj, proj, proj, proj, proj, proj, *prm)
    kd, rd, bg, ge, bonus, gate, aab, rb, m2, rkv, vk = outs

    tmat = _tri_solve(aab.reshape(-1, ng * C_HEAD)).reshape(stk_shape.shape)

    y, st = pl.pallas_call(
        _rwkv_c_kernel,
        grid=(bsz, nblk),
        in_specs=[stk] + [tok] * 6 + [stk] * 4 + [_full((1, C_DIM))] * 2,
        out_specs=[tok, pl.BlockSpec((None, ng, RW_GW, C_HEAD), lambda b, c: (b, 0, 0, 0))],
        out_shape=[jax.ShapeDtypeStruct((bsz, seq, C_DIM), BF16),
                   jax.ShapeDtypeStruct((bsz, ng, RW_GW, C_HEAD), F32)],
        scratch_shapes=[pltpu.VMEM((ng, RW_GW, C_HEAD), F32), pltpu.VMEM((tb, C_DIM), F32)],
        compiler_params=_cparams(("parallel", "arbitrary")),
        name="rwkv_scan",
    )(tmat, kd, rd, bg, ge, bonus, gate, rb, m2, rkv, vk, rp["lnx_w"], rp["lnx_b"])
    return y, st.reshape(bsz, C_HEADS, C_HEAD, C_HEAD)


def _rwkv_step_kernel(r_ref, k_ref, v_ref, lr_ref, sh_ref, *rest):
    prm = tuple(x[...] for x in rest[:len(RW_NAMES)])
    lnw, lnb, s_ref, _, y_ref, so_ref = rest[len(RW_NAMES):]
    bb = STEP_BB
    sh = sh_ref[...]
    r, k2, v, kk, beta, ld, g, bonus = _rwkv_prep(
        r_ref[...], k_ref[...], v_ref[...], lr_ref[...],
        sh[:, :C_DIM], sh[:, C_DIM:2 * C_DIM], sh[:, 2 * C_DIM:3 * C_DIM], sh[:, 3 * C_DIM:], prm)
    w = jnp.exp(ld)
    v_t = v.T
    lane = _iota((C_DIM, LANES), 1)
    heads = [[x[:, h * C_HEAD:(h + 1) * C_HEAD] for x in (kk, w, beta, k2, r)] for h in range(C_HEADS)]

    def rows(i, b):
        return jnp.concatenate(
            [jnp.broadcast_to(heads[h][i][b:b + 1, :], (C_HEAD, C_HEAD)) for h in range(C_HEADS)], axis=0)

    y_t = jnp.zeros((C_DIM, LANES), F32)
    for b in range(bb):
        s = s_ref[b]
        sa = -jnp.sum(s * rows(0, b), axis=-1, keepdims=True)
        sn = s * rows(1, b) + sa * rows(2, b) + v_t[:, b:b + 1] * rows(3, b)
        so_ref[b] = sn
        y_t = jnp.where(lane == b, jnp.sum(sn * rows(4, b), axis=-1, keepdims=True), y_t)
    y = y_t.T[0:bb, :]
    y_ref[...] = _rwkv_out(y, bonus, g, lnw[...], lnb[...]).astype(BF16)


def _rwkv_step(proj, shift, s_all, s_new, layer, rp):
    bsz = proj.shape[0]
    bb = STEP_BB
    sflat = s_all.reshape(s_all.shape[0], bsz, C_DIM, C_HEAD)

    def col(width, off):
        return pl.BlockSpec((bb, width), lambda i: (i, off // width))

    prm = [rp[n] for n in RW_NAMES]
    y, sn = pl.pallas_call(
        _rwkv_step_kernel,
        grid=(bsz // bb,),
        in_specs=[col(C_DIM, P_R), col(C_DIM, P_K), col(C_DIM, P_V), col(C_LR, P_LR),
                  pl.BlockSpec((bb, C_COLS), lambda i: (i, 0))]
                 + [_full(x.shape) for x in prm] + [_full((1, C_DIM))] * 2
                 + [pl.BlockSpec((None, bb, C_DIM, C_HEAD), lambda i: (layer, i, 0, 0)),
                    pl.BlockSpec(memory_space=pl.ANY)],
        out_specs=[pl.BlockSpec((bb, C_DIM), lambda i: (i, 0)),
                   pl.BlockSpec((None, bb, C_DIM, C_HEAD), lambda i: (layer, i, 0, 0))],
        out_shape=[jax.ShapeDtypeStruct((bsz, C_DIM), BF16),
                   jax.ShapeDtypeStruct(sflat.shape, F32)],
        input_output_aliases={len(prm) + 8: 1},
        compiler_params=_cparams(("parallel",)),
        name="rwkv_step",
    )(proj, proj, proj, proj, shift, *prm, rp["lnx_w"], rp["lnx_b"], sflat, s_new.reshape(sflat.shape))
    return y, sn.reshape(s_all.shape)


def _layout_w_in_kernel(w_ref, o_ref):
    o = A_DIM + CONV_CH
    cols = w_ref.shape[1]
    o_ref[:o, :] = w_ref[:o, :].astype(BF16)
    o_ref[o:P_DT, :] = w_ref[o + A_HEADS:, :].astype(BF16)
    o_ref[P_DT:P_DT + A_HEADS, :] = w_ref[o:o + A_HEADS, :].astype(BF16)
    o_ref[P_DT + A_HEADS:, :] = jnp.zeros((P_COLS - P_DT - A_HEADS, cols), BF16)


def _layout_w_in(w_all, layer):
    wt = jnp.swapaxes(w_all, 1, 2)
    k = wt.shape[2]
    tc = 256
    return pl.pallas_call(
        _layout_w_in_kernel,
        grid=(k // tc,),
        in_specs=[pl.BlockSpec((None, IN_COLS, tc), lambda i: (layer, 0, i))],
        out_specs=pl.BlockSpec((P_COLS, tc), lambda i: (0, i)),
        out_shape=jax.ShapeDtypeStruct((P_COLS, k), BF16),
        compiler_params=_cparams(("parallel",)),
        name="layout_w_in",
    )(wt)


def _xbc_cols(proj):
    return proj[..., P_X:P_X + CONV_CH]


def _moe_route(logits, n_tok):
    top_logit, top_idx = lax.top_k(logits, TOP_K)
    gate = jax.nn.softmax(top_logit, axis=-1)
    n_assign = n_tok * TOP_K
    flat_e = top_idx.reshape(-1).astype(jnp.int32)
    order = jnp.argsort(flat_e)
    onehot = jax.nn.one_hot(flat_e, N_EXPERTS, dtype=jnp.int32)
    counts = jnp.sum(onehot, axis=0)
    padded = (counts + MOE_BM - 1) // MOE_BM * MOE_BM
    pad_end = jnp.cumsum(padded)
    pad_start = pad_end - padded
    start = jnp.cumsum(counts) - counts
    rank = jnp.sum((jnp.cumsum(onehot, axis=0) - onehot) * onehot, axis=1)
    slot_of = (pad_start[flat_e] + rank).reshape(n_tok, TOP_K)
    n_blocks = -(-n_assign // MOE_BM) + N_EXPERTS
    n_slots = n_blocks * MOE_BM
    block_start = jnp.arange(n_blocks, dtype=jnp.int32) * MOE_BM
    nvalid = (pad_end[-1] // MOE_BM).astype(jnp.int32)
    bexp = jnp.minimum(jnp.searchsorted(pad_end, block_start, side="right"), N_EXPERTS - 1).astype(jnp.int32)
    slot_e = jnp.repeat(bexp, MOE_BM)
    within = jnp.arange(n_slots, dtype=jnp.int32) - pad_start[slot_e]
    src = order[jnp.clip(start[slot_e] + within, 0, n_assign - 1)] // TOP_K
    slot_tok = jnp.where(within < counts[slot_e], src, n_tok)
    left = counts[bexp] - (block_start - pad_start[bexp])
    nsub = jnp.clip((left + MOE_SUB - 1) // MOE_SUB, 0, MOE_BM // MOE_SUB).astype(jnp.int32)
    last = bexp[jnp.maximum(nvalid - 1, 0)]
    bexp = jnp.where(jnp.arange(n_blocks) < nvalid, bexp, last)
    first = jnp.concatenate([jnp.ones((1,), jnp.int32), (bexp[1:] != bexp[:-1]).astype(jnp.int32)])
    return gate, slot_tok, slot_of, bexp, first, nsub


def _moe(xp, xs, h, logits, wg, wu, wd, g, final):
    tp = xp.shape[0]
    n_tok = h.shape[0]
    gate, slot_tok, slot_of, bexp, first, nsub = _moe_route(logits[:, :N_EXPERTS], n_tok)
    xb = h[jnp.where(slot_tok == n_tok, 0, slot_tok)]
    yb = _moe_experts(xb, bexp, first, nsub, wg, wu, wd)
    return (_moe_combine(xp, yb, slot_of[:tp], gate[:tp], g, final),
            _moe_combine(xs, yb, slot_of[tp:], gate[tp:], g, final))


def kernel(x_prompt, x_sample, state_ssm, state_conv, state_hgrn, state_rwkv, state_shift, norm1_g, w_in, conv_w, conv_b, dt_bias, a_log, d_skip, ssm_norm_g, lb_logits, hgrn_norm_g, shift_mu, w0, w2, a0, a2, g2, k_k, k_a, r_k, lnx_w, lnx_b, w_out, norm2_g, ffn_w_gate, ffn_w_up, ffn_w_down, router_w, exp_w_gate, exp_w_up, exp_w_down, final_norm_g):
    depth = w_in.shape[0]
    bp, seq, _ = x_prompt.shape
    bs = x_sample.shape[0]
    tp = bp * seq
    lb_soft = jax.nn.softmax(lb_logits.astype(F32), axis=0)
    lower_bounds = jnp.clip(jnp.cumsum(lb_soft, axis=0) - lb_soft[0:1], 0.0, 1.0)

    xp = x_prompt.reshape(tp, D_MODEL)
    xs = x_sample.reshape(bs, D_MODEL)
    outs = {k: [] for k in ("p_ssm", "p_conv", "p_hgrn", "p_rwkv", "p_shift", "s_conv", "s_shift")}
    ssm_s, hgrn_s, rwkv_s = (jnp.zeros(s.shape, F32) for s in (state_ssm, state_hgrn, state_rwkv))
    normed = False
    for layer in range(depth):
        p = dict(conv_w=conv_w[layer], conv_b=conv_b[layer], dt_bias=dt_bias[layer], a_log=a_log[layer],
                 d_skip=d_skip[layer], ssm_norm_g=ssm_norm_g[layer], shift_mu=shift_mu[layer],
                 w0=w0[layer], w2=w2[layer], a0=a0[layer], a2=a2[layer], g2=g2[layer], k_k=k_k[layer],
                 k_a=k_a[layer], r_k=r_k[layer].reshape(-1), lnx_w=lnx_w[layer], lnx_b=lnx_b[layer])
        sp = _ssd_params(p)
        rp = _rwkv_params(p)
        lb = lower_bounds[layer]
        lbp = jnp.stack([1.0 - lb, jnp.log(lb), jnp.log1p(-lb)])
        hng = hgrn_norm_g[layer].reshape(1, B_HEAD)
        w_in_l = _layout_w_in(w_in, layer)
        w_out_l = w_out[layer].astype(BF16)
        j = layer // 2
        routed = layer % 2 == 1
        if routed:
            rw_f = jnp.pad(router_w[j], ((0, 0), (0, LANES - N_EXPERTS)))
            rw_hi = rw_f.astype(BF16)
            rw = jnp.concatenate([rw_hi, (rw_f - rw_hi.astype(F32)).astype(BF16)], axis=1)
            h_all = jnp.zeros((tp + bs, D_MODEL), BF16)
            l_all = jnp.zeros((tp + bs, LANES), F32)

        proj = _norm_matmul(xp, norm1_g[layer], w_in_l, 256).reshape(bp, seq, P_COLS)
        ya, ssm_p = _ssd_prompt(proj, sp)
        yb, hgrn_p = _hgrn_prompt(proj, lbp, hng)
        yc, rwkv_p = _rwkv_prompt(proj, rp)
        xp, hp, l_all = _out_proj(ya.reshape(tp, A_DIM), yb.reshape(tp, B_DIM), yc.reshape(tp, C_DIM),
                                  w_out_l, xp, norm2_g[layer], 512, (rw, h_all, l_all, 0) if routed else None)
        outs["p_ssm"].append(ssm_p)
        outs["p_conv"].append(_xbc_cols(proj[:, seq - (CONV_W - 1):, :]))
        outs["p_hgrn"].append(hgrn_p)
        outs["p_rwkv"].append(rwkv_p)
        outs["p_shift"].append(proj[:, seq - 1, P_R:P_R + C_COLS])

        projs = _norm_matmul(xs, norm1_g[layer], w_in_l, 128)
        ya, ssm_s = _ssd_step(projs, state_conv[layer], state_ssm, ssm_s, layer, p, sp)
        yb, hgrn_s = _hgrn_step(projs, state_hgrn, hgrn_s, layer, lbp, hng)
        yc, rwkv_s = _rwkv_step(projs, state_shift[layer], state_rwkv, rwkv_s, layer, rp)
        xs, hs, l_all = _out_proj(ya, yb, yc, w_out_l, xs, norm2_g[layer], 128,
                                  (rw, hp, l_all, tp) if routed else None)
        outs["s_conv"].append(jnp.concatenate([state_conv[layer][:, 1:, :], _xbc_cols(projs)[:, None, :]], axis=1))
        outs["s_shift"].append(projs[:, P_R:P_R + C_COLS])

        if not routed:
            xp = _ffn(xp, hp, ffn_w_gate[j], ffn_w_up[j], ffn_w_down[j])
            xs = _ffn(xs, hs, ffn_w_gate[j], ffn_w_up[j], ffn_w_down[j])
        else:
            normed = layer == depth - 1
            xp, xs = _moe(xp, xs, hs, l_all, exp_w_gate[j], exp_w_up[j], exp_w_down[j], final_norm_g, normed)

    if not normed:
        xp, xs = _final_norm(xp, final_norm_g, 512), _final_norm(xs, final_norm_g, 128)
    y_prompt = xp.reshape(bp, seq, D_MODEL)
    y_sample = xs.reshape(bs, 1, D_MODEL)
    st = {k: jnp.stack(v) for k, v in outs.items()}
    return (y_prompt, y_sample, st["p_ssm"], st["p_conv"], st["p_hgrn"], st["p_rwkv"], st["p_shift"],
            ssm_s, st["s_conv"], hgrn_s, rwkv_s, st["s_shift"])
```
